```python
import math
import jax, jax.numpy as jnp
from jax import lax
import numpy as np

D_MODEL = 4096
BATCH = 8
SEQ = 4096
DEPTH = 1

D_MIX = D_MODEL
D_SSM = D_MIX // 2
D_SC = D_MIX - D_SSM
SSM_HEAD_DIM = 64
SSM_HEADS = D_SSM // SSM_HEAD_DIM
SSM_GROUPS = 8
SSM_STATE = 128
SSM_CONV = 5
SSM_CHUNK = 128
SC_CONV = 3
SC_GROUPS = 16
D_XBC = D_SSM + 2 * SSM_GROUPS * SSM_STATE
D_IN = D_SSM + D_XBC + 2 * SSM_HEADS + 3 * D_SC
D_FF = 4 * D_MODEL
N_MOD = 6
DEEPNORM_ALPHA = (2 * DEPTH) ** 0.25
DEEPNORM_BETA = (8 * DEPTH) ** -0.25
DT_PROJ_SCALE = 0.1
LN_EPS = 1e-5
RMS_EPS = 1e-5

kernel_name = "hymba_ssd_shortconv_deepnorm_adaln_encoder"


def layer_norm(x, g, b):
    xf = x.astype(jnp.float32)
    mu = jnp.mean(xf, axis=-1, keepdims=True)
    var = jnp.mean(jnp.square(xf - mu), axis=-1, keepdims=True)
    return ((xf - mu) * lax.rsqrt(var + LN_EPS) * g + b).astype(x.dtype)


def group_rms_norm(y, w, n_groups):
    bsz, s, d = y.shape
    yf = y.astype(jnp.float32).reshape(bsz, s, n_groups, d // n_groups)
    yf = yf * lax.rsqrt(jnp.mean(yf * yf, axis=-1, keepdims=True) + RMS_EPS)
    return (yf.reshape(bsz, s, d) * w).astype(y.dtype)


def dwconv_centred(u, w):
    k_w, ch = w.shape
    return lax.conv_general_dilated(
        u, w[:, None, :].astype(u.dtype), window_strides=(1,),
        padding=[(k_w // 2, k_w // 2)], dimension_numbers=('NWC', 'WIO', 'NWC'),
        feature_group_count=ch)


def ssd_chunked(x, dt, a, b_in, c_in):
    bsz, s, h, p = x.shape
    g, n = b_in.shape[2], b_in.shape[3]
    r = h // g
    q = SSM_CHUNK
    nc = s // q
    xc = x.reshape(bsz, nc, q, g, r, p)
    dtc = dt.reshape(bsz, nc, q, g, r)
    bc = b_in.reshape(bsz, nc, q, g, n)
    cc = c_in.reshape(bsz, nc, q, g, n)
    a_cum = jnp.cumsum(dtc * a.reshape(g, r), axis=2)
    xdt = xc * dtc[..., None]
    lower = jnp.tril(jnp.ones((q, q), dtype=bool))[:, :, None, None]
    seg = a_cum[:, :, :, None] - a_cum[:, :, None, :]
    decay = jnp.exp(jnp.where(lower, seg, -jnp.inf))
    scores = jnp.einsum('bcign,bcjgn->bcijg', cc, bc)
    y_diag = jnp.einsum('bcijgr,bcjgrp->bcigrp', scores[..., None] * decay, xdt)
    to_end = jnp.exp(a_cum[:, :, -1:] - a_cum)
    states = jnp.einsum('bclgn,bclgrp->bcgrpn', bc, xdt * to_end[..., None])
    chunk_decay = jnp.exp(a_cum[:, :, -1])

    def step(carry, inp):
        st, dec = inp
        return carry * dec[..., None, None] + st, carry

    init = jnp.zeros((bsz, g, r, p, n), dtype=states.dtype)
    _, prev = lax.scan(step, init, (jnp.moveaxis(states, 1, 0), jnp.moveaxis(chunk_decay, 1, 0)))
    prev = jnp.moveaxis(prev, 0, 1)
    y_off = jnp.einsum('bclgn,bcgrpn->bclgrp', cc, prev) * jnp.exp(a_cum)[..., None]
    return (y_diag + y_off).reshape(bsz, s, h, p).astype(x.dtype)


def ssd_mixer(u_z, u_xbc, u_dt, conv_w, conv_b, dt_bias_f, dt_bias_b, a_log_f, a_log_b, d_skip, norm_w):
    bsz, s, _ = u_z.shape
    xbc = jax.nn.silu(dwconv_centred(u_xbc, conv_w) + conv_b)
    xs, bs, cs = jnp.split(xbc, [D_SSM, D_SSM + SSM_GROUPS * SSM_STATE], axis=-1)
    xs = xs.reshape(bsz, s, SSM_HEADS, SSM_HEAD_DIM)
    bs = bs.reshape(bsz, s, SSM_GROUPS, SSM_STATE)
    cs = cs.reshape(bsz, s, SSM_GROUPS, SSM_STATE)
    dt = u_dt.astype(jnp.float32)
    dt_f = jax.nn.softplus(dt[..., :SSM_HEADS] + dt_bias_f)
    dt_b = jax.nn.softplus(dt[..., SSM_HEADS:] + dt_bias_b)
    a_f = -jnp.exp(a_log_f.astype(jnp.float32))
    a_b = -jnp.exp(a_log_b.astype(jnp.float32))
    flip = lambda t: jnp.flip(t, axis=1)
    y_f = ssd_chunked(xs, dt_f, a_f, bs, cs)
    y_b = flip(ssd_chunked(flip(xs), flip(dt_b), a_b, flip(bs), flip(cs)))
    y = y_f + y_b + d_skip[:, None] * xs
    y = y.reshape(bsz, s, D_SSM) * jax.nn.silu(u_z)
    return group_rms_norm(y, norm_w, SSM_GROUPS)


def short_conv_mixer(u_h, u_b, u_c, conv_w, norm_w):
    y = u_b * dwconv_centred(u_c * u_h, conv_w)
    return group_rms_norm(y, norm_w, SC_GROUPS)


def _fwd_setup_inputs(seed: int = 0) -> dict:
    key = jax.random.key(seed)
    ks = jax.random.split(key, 24)
    nrm = jax.random.normal
    x = nrm(ks[0], (BATCH, SEQ, D_MODEL), jnp.float32)
    c = nrm(ks[1], (BATCH, D_MODEL), jnp.float32)
    w_ada = nrm(ks[2], (DEPTH, D_MODEL, N_MOD * D_MODEL), jnp.float32) * (0.1 * D_MODEL ** -0.5)
    b_ada = 0.01 * nrm(ks[3], (DEPTH, N_MOD * D_MODEL), jnp.float32)
    dt_lo = D_SSM + D_XBC
    col_scale = jnp.ones((D_IN,), jnp.float32).at[dt_lo:dt_lo + 2 * SSM_HEADS].set(DT_PROJ_SCALE)
    w_in = nrm(ks[4], (DEPTH, D_MODEL, D_IN), jnp.float32) * (D_MODEL ** -0.5) * col_scale
    ssm_conv_w = nrm(ks[5], (DEPTH, SSM_CONV, D_XBC), jnp.float32) * SSM_CONV ** -0.5
    ssm_conv_b = 0.01 * nrm(ks[6], (DEPTH, D_XBC), jnp.float32)

    def dt_bias(k):
        dt0 = jnp.exp(jax.random.uniform(k, (DEPTH, SSM_HEADS), jnp.float32, math.log(1e-3), math.log(1e-1)))
        return dt0 + jnp.log(-jnp.expm1(-dt0))

    ssm_dt_bias_f = dt_bias(ks[7])
    ssm_dt_bias_b = dt_bias(ks[8])
    ssm_a_log_f = jnp.log(jax.random.uniform(ks[9], (DEPTH, SSM_HEADS), jnp.float32, 1.0, 16.0))
    ssm_a_log_b = jnp.log(jax.random.uniform(ks[10], (DEPTH, SSM_HEADS), jnp.float32, 1.0, 16.0))
    ssm_d = 1.0 + 0.1 * nrm(ks[11], (DEPTH, SSM_HEADS), jnp.float32)
    ssm_norm_w = 1.0 + 0.02 * nrm(ks[12], (DEPTH, D_SSM), jnp.float32)
    sc_conv_w = nrm(ks[13], (DEPTH, SC_CONV, D_SC), jnp.float32) * SC_CONV ** -0.5
    sc_norm_w = 1.0 + 0.02 * nrm(ks[14], (DEPTH, D_SC), jnp.float32)
    w_out = nrm(ks[15], (DEPTH, D_MIX, D_MODEL), jnp.float32) * (D_MIX ** -0.5 * DEEPNORM_BETA)
    ln1_g = 1.0 + 0.02 * nrm(ks[16], (DEPTH, D_MODEL), jnp.float32)
    ln1_b = 0.01 * nrm(ks[17], (DEPTH, D_MODEL), jnp.float32)
    w_up = nrm(ks[18], (DEPTH, D_MODEL, D_FF), jnp.float32) * D_MODEL ** -0.5
    w_down = nrm(ks[19], (DEPTH, D_FF, D_MODEL), jnp.float32) * (D_FF ** -0.5 * DEEPNORM_BETA)
    ln2_g = 1.0 + 0.02 * nrm(ks[20], (DEPTH, D_MODEL), jnp.float32)
    ln2_b = 0.01 * nrm(ks[21], (DEPTH, D_MODEL), jnp.float32)
    return {"x": x, "c": c, "w_ada": w_ada, "b_ada": b_ada, "w_in": w_in,
            "ssm_conv_w": ssm_conv_w, "ssm_conv_b": ssm_conv_b,
            "ssm_dt_bias_f": ssm_dt_bias_f, "ssm_dt_bias_b": ssm_dt_bias_b,
            "ssm_a_log_f": ssm_a_log_f, "ssm_a_log_b": ssm_a_log_b, "ssm_d": ssm_d,
            "ssm_norm_w": ssm_norm_w, "sc_conv_w": sc_conv_w, "sc_norm_w": sc_norm_w,
            "w_out": w_out, "ln1_g": ln1_g, "ln1_b": ln1_b, "w_up": w_up, "w_down": w_down,
            "ln2_g": ln2_g, "ln2_b": ln2_b}


def _fwd_reference(x, c, w_ada, b_ada, w_in, ssm_conv_w, ssm_conv_b, ssm_dt_bias_f, ssm_dt_bias_b,
              ssm_a_log_f, ssm_a_log_b, ssm_d, ssm_norm_w, sc_conv_w, sc_norm_w, w_out,
              ln1_g, ln1_b, w_up, w_down, ln2_g, ln2_b):
    bounds = [int(v) for v in np.cumsum([D_SSM, D_XBC, 2 * SSM_HEADS, D_SC, D_SC])]
    for l in range(DEPTH):
        mod = jnp.einsum('bd,dm->bm', jax.nn.silu(c), w_ada[l]) + b_ada[l]
        shift1, scale1, gate1, shift2, scale2, gate2 = jnp.split(mod[:, None, :], N_MOD, axis=-1)
        h = x * (1.0 + scale1) + shift1
        proj = jnp.einsum('bsd,de->bse', h, w_in[l])
        u_z, u_xbc, u_dt, u_h, u_b, u_c = jnp.split(proj, bounds, axis=-1)
        y_ssm = ssd_mixer(u_z, u_xbc, u_dt, ssm_conv_w[l], ssm_conv_b[l], ssm_dt_bias_f[l],
                          ssm_dt_bias_b[l], ssm_a_log_f[l], ssm_a_log_b[l], ssm_d[l], ssm_norm_w[l])
        y_sc = short_conv_mixer(u_h, u_b, u_c, sc_conv_w[l], sc_norm_w[l])
        mix = jnp.einsum('bse,ed->bsd', jnp.concatenate([y_ssm, y_sc], axis=-1), w_out[l])
        x = layer_norm(DEEPNORM_ALPHA * x + (1.0 + gate1) * mix, ln1_g[l], ln1_b[l])
        h = x * (1.0 + scale2) + shift2
        ff = jnp.square(jax.nn.relu(jnp.einsum('bsd,df->bsf', h, w_up[l])))
        ff = jnp.einsum('bsf,fd->bsd', ff, w_down[l])
        x = layer_norm(DEEPNORM_ALPHA * x + (1.0 + gate2) * ff, ln2_g[l], ln2_b[l])
    return x


import jax as _jax
import jax.numpy as _jnp

TWIN_FORMAT = 'train_step'
FWD_PARAMS = ['x', 'c', 'w_ada', 'b_ada', 'w_in', 'ssm_conv_w', 'ssm_conv_b', 'ssm_dt_bias_f', 'ssm_dt_bias_b', 'ssm_a_log_f', 'ssm_a_log_b', 'ssm_d', 'ssm_norm_w', 'sc_conv_w', 'sc_norm_w', 'w_out', 'ln1_g', 'ln1_b', 'w_up', 'w_down', 'ln2_g', 'ln2_b']
TWIN_WEIGHTS = ['w_ada', 'b_ada', 'w_in', 'ssm_conv_w', 'ssm_conv_b', 'ssm_dt_bias_f', 'ssm_dt_bias_b', 'ssm_a_log_f', 'ssm_a_log_b', 'ssm_d', 'ssm_norm_w', 'sc_conv_w', 'sc_norm_w', 'w_out', 'ln1_g', 'ln1_b', 'w_up', 'w_down', 'ln2_g', 'ln2_b']
TWIN_DIFF_INPUT = 'x'
TWIN_INPUTS = ['x', 'c', 'w_ada', 'b_ada', 'w_in', 'ssm_conv_w', 'ssm_conv_b', 'ssm_dt_bias_f', 'ssm_dt_bias_b', 'ssm_a_log_f', 'ssm_a_log_b', 'ssm_d', 'ssm_norm_w', 'sc_conv_w', 'sc_norm_w', 'w_out', 'ln1_g', 'ln1_b', 'w_up', 'w_down', 'ln2_g', 'ln2_b', 'loss_target', 'm_w_ada', 'm_b_ada', 'm_w_in', 'm_ssm_conv_w', 'm_ssm_conv_b', 'm_ssm_dt_bias_f', 'm_ssm_dt_bias_b', 'm_ssm_a_log_f', 'm_ssm_a_log_b', 'm_ssm_d', 'm_ssm_norm_w', 'm_sc_conv_w', 'm_sc_norm_w', 'm_w_out', 'm_ln1_g', 'm_ln1_b', 'm_w_up', 'm_w_down', 'm_ln2_g', 'm_ln2_b', 'v_w_ada', 'v_b_ada', 'v_w_in', 'v_ssm_conv_w', 'v_ssm_conv_b', 'v_ssm_dt_bias_f', 'v_ssm_dt_bias_b', 'v_ssm_a_log_f', 'v_ssm_a_log_b', 'v_ssm_d', 'v_ssm_norm_w', 'v_sc_conv_w', 'v_sc_norm_w', 'v_w_out', 'v_ln1_g', 'v_ln1_b', 'v_w_up', 'v_w_down', 'v_ln2_g', 'v_ln2_b']
TWIN_OUTPUTS = ['loss', 'grad_x', 'grad_w_ada', 'grad_b_ada', 'grad_w_in', 'grad_ssm_conv_w', 'grad_ssm_conv_b', 'grad_ssm_dt_bias_f', 'grad_ssm_dt_bias_b', 'grad_ssm_a_log_f', 'grad_ssm_a_log_b', 'grad_ssm_d', 'grad_ssm_norm_w', 'grad_sc_conv_w', 'grad_sc_norm_w', 'grad_w_out', 'grad_ln1_g', 'grad_ln1_b', 'grad_w_up', 'grad_w_down', 'grad_ln2_g', 'grad_ln2_b', 'delta_w_ada', 'delta_b_ada', 'delta_w_in', 'delta_ssm_conv_w', 'delta_ssm_conv_b', 'delta_ssm_dt_bias_f', 'delta_ssm_dt_bias_b', 'delta_ssm_a_log_f', 'delta_ssm_a_log_b', 'delta_ssm_d', 'delta_ssm_norm_w', 'delta_sc_conv_w', 'delta_sc_norm_w', 'delta_w_out', 'delta_ln1_g', 'delta_ln1_b', 'delta_w_up', 'delta_w_down', 'delta_ln2_g', 'delta_ln2_b', 'new_m_w_ada', 'new_m_b_ada', 'new_m_w_in', 'new_m_ssm_conv_w', 'new_m_ssm_conv_b', 'new_m_ssm_dt_bias_f', 'new_m_ssm_dt_bias_b', 'new_m_ssm_a_log_f', 'new_m_ssm_a_log_b', 'new_m_ssm_d', 'new_m_ssm_norm_w', 'new_m_sc_conv_w', 'new_m_sc_norm_w', 'new_m_w_out', 'new_m_ln1_g', 'new_m_ln1_b', 'new_m_w_up', 'new_m_w_down', 'new_m_ln2_g', 'new_m_ln2_b', 'new_v_w_ada', 'new_v_b_ada', 'new_v_w_in', 'new_v_ssm_conv_w', 'new_v_ssm_conv_b', 'new_v_ssm_dt_bias_f', 'new_v_ssm_dt_bias_b', 'new_v_ssm_a_log_f', 'new_v_ssm_a_log_b', 'new_v_ssm_d', 'new_v_ssm_norm_w', 'new_v_sc_conv_w', 'new_v_sc_norm_w', 'new_v_w_out', 'new_v_ln1_g', 'new_v_ln1_b', 'new_v_w_up', 'new_v_w_down', 'new_v_ln2_g', 'new_v_ln2_b']
TWIN_LEAF_KINDS = {'loss': 'loss', 'grad_x': 'grad_x', 'grad_w_ada': 'grad_w', 'grad_b_ada': 'grad_w', 'grad_w_in': 'grad_w', 'grad_ssm_conv_w': 'grad_w', 'grad_ssm_conv_b': 'grad_w', 'grad_ssm_dt_bias_f': 'grad_w', 'grad_ssm_dt_bias_b': 'grad_w', 'grad_ssm_a_log_f': 'grad_w', 'grad_ssm_a_log_b': 'grad_w', 'grad_ssm_d': 'grad_w', 'grad_ssm_norm_w': 'grad_w', 'grad_sc_conv_w': 'grad_w', 'grad_sc_norm_w': 'grad_w', 'grad_w_out': 'grad_w', 'grad_ln1_g': 'grad_w', 'grad_ln1_b': 'grad_w', 'grad_w_up': 'grad_w', 'grad_w_down': 'grad_w', 'grad_ln2_g': 'grad_w', 'grad_ln2_b': 'grad_w', 'delta_w_ada': 'delta_w', 'delta_b_ada': 'delta_w', 'delta_w_in': 'delta_w', 'delta_ssm_conv_w': 'delta_w', 'delta_ssm_conv_b': 'delta_w', 'delta_ssm_dt_bias_f': 'delta_w', 'delta_ssm_dt_bias_b': 'delta_w', 'delta_ssm_a_log_f': 'delta_w', 'delta_ssm_a_log_b': 'delta_w', 'delta_ssm_d': 'delta_w', 'delta_ssm_norm_w': 'delta_w', 'delta_sc_conv_w': 'delta_w', 'delta_sc_norm_w': 'delta_w', 'delta_w_out': 'delta_w', 'delta_ln1_g': 'delta_w', 'delta_ln1_b': 'delta_w', 'delta_w_up': 'delta_w', 'delta_w_down': 'delta_w', 'delta_ln2_g': 'delta_w', 'delta_ln2_b': 'delta_w', 'new_m_w_ada': 'new_m', 'new_m_b_ada': 'new_m', 'new_m_w_in': 'new_m', 'new_m_ssm_conv_w': 'new_m', 'new_m_ssm_conv_b': 'new_m', 'new_m_ssm_dt_bias_f': 'new_m', 'new_m_ssm_dt_bias_b': 'new_m', 'new_m_ssm_a_log_f': 'new_m', 'new_m_ssm_a_log_b': 'new_m', 'new_m_ssm_d': 'new_m', 'new_m_ssm_norm_w': 'new_m', 'new_m_sc_conv_w': 'new_m', 'new_m_sc_norm_w': 'new_m', 'new_m_w_out': 'new_m', 'new_m_ln1_g': 'new_m', 'new_m_ln1_b': 'new_m', 'new_m_w_up': 'new_m', 'new_m_w_down': 'new_m', 'new_m_ln2_g': 'new_m', 'new_m_ln2_b': 'new_m', 'new_v_w_ada': 'new_v', 'new_v_b_ada': 'new_v', 'new_v_w_in': 'new_v', 'new_v_ssm_conv_w': 'new_v', 'new_v_ssm_conv_b': 'new_v', 'new_v_ssm_dt_bias_f': 'new_v', 'new_v_ssm_dt_bias_b': 'new_v', 'new_v_ssm_a_log_f': 'new_v', 'new_v_ssm_a_log_b': 'new_v', 'new_v_ssm_d': 'new_v', 'new_v_ssm_norm_w': 'new_v', 'new_v_sc_conv_w': 'new_v', 'new_v_sc_norm_w': 'new_v', 'new_v_w_out': 'new_v', 'new_v_ln1_g': 'new_v', 'new_v_ln1_b': 'new_v', 'new_v_w_up': 'new_v', 'new_v_w_down': 'new_v', 'new_v_ln2_g': 'new_v', 'new_v_ln2_b': 'new_v'}


def _forward(args):
    return _fwd_reference(*[args[k] for k in FWD_PARAMS])


def _output_shape():
    out = _jax.eval_shape(lambda: _forward(_fwd_setup_inputs(0)))
    return out.shape, out.dtype

N_MICROBATCH = 1
ADAM_LR = 0.001
ADAM_B1 = 0.9
ADAM_B2 = 0.999
ADAM_EPS = 1e-08
ADAM_WD = 0.01
ADAM_STEP = 10
PER_EXAMPLE_BATCH_AXIS = {'x': 0, 'c': 0, 'loss_target': 0}
SHARED_INPUTS = []
_WEIGHT_DTYPES = {'w_ada': _jnp.float32, 'b_ada': _jnp.float32, 'w_in': _jnp.float32, 'ssm_conv_w': _jnp.float32, 'ssm_conv_b': _jnp.float32, 'ssm_dt_bias_f': _jnp.float32, 'ssm_dt_bias_b': _jnp.float32, 'ssm_a_log_f': _jnp.float32, 'ssm_a_log_b': _jnp.float32, 'ssm_d': _jnp.float32, 'ssm_norm_w': _jnp.float32, 'sc_conv_w': _jnp.float32, 'sc_norm_w': _jnp.float32, 'w_out': _jnp.float32, 'ln1_g': _jnp.float32, 'ln1_b': _jnp.float32, 'w_up': _jnp.float32, 'w_down': _jnp.float32, 'ln2_g': _jnp.float32, 'ln2_b': _jnp.float32}
MOMENT_SCALE = {'w_ada': 1.922766e-02, 'b_ada': 3.868949e-02, 'w_in': 1.913004e-02, 'ssm_conv_w': 1.408068e-02, 'ssm_conv_b': 2.801244e-02, 'ssm_dt_bias_f': 3.129581e-02, 'ssm_dt_bias_b': 2.078606e-02, 'ssm_a_log_f': 3.873345e-02, 'ssm_a_log_b': 4.677511e-02, 'ssm_d': 9.902644e-02, 'ssm_norm_w': 2.065113e-02, 'sc_conv_w': 2.113337e-02, 'sc_norm_w': 2.115400e-02, 'w_out': 3.551808e-02, 'ln1_g': 1.773224e-01, 'ln1_b': 7.216423e-02, 'w_up': 1.347814e-02, 'w_down': 4.443655e-02, 'ln2_g': 8.015387e+00, 'ln2_b': 1.810858e+00}


def _to_microbatches(a, axis):
    t = _jnp.moveaxis(a, axis, 0)
    t = t.reshape((N_MICROBATCH, t.shape[0] // N_MICROBATCH) + t.shape[1:])
    return _jnp.moveaxis(t, 1, axis + 1)


def setup_inputs(seed: int = 0) -> dict:
    inp = _fwd_setup_inputs(seed)
    key = _jax.random.fold_in(_jax.random.key(seed), 7919)
    shape, _ = _output_shape()
    out = dict(inp)
    out["loss_target"] = _jax.random.normal(_jax.random.fold_in(key, 0), shape, _jnp.float32)
    for i, name in enumerate(TWIN_WEIGHTS):
        w = inp[name].astype(_jnp.float32)
        if MOMENT_SCALE is None:
            s = _jnp.sqrt(_jnp.mean(_jnp.square(w)) + 1e-30)
        else:
            s = MOMENT_SCALE[name]
        km, kv = _jax.random.split(_jax.random.fold_in(key, i + 1))
        out[name] = w
        out["m_" + name] = s * _jax.random.normal(km, w.shape, _jnp.float32)
        out["v_" + name] = (s * s) * _jax.random.uniform(kv, w.shape, _jnp.float32, 0.5, 1.5)
    if N_MICROBATCH > 1:
        for name, axis in PER_EXAMPLE_BATCH_AXIS.items():
            out[name] = _to_microbatches(out[name], axis)
    return {'x': out['x'], 'c': out['c'], 'w_ada': out['w_ada'], 'b_ada': out['b_ada'], 'w_in': out['w_in'], 'ssm_conv_w': out['ssm_conv_w'], 'ssm_conv_b': out['ssm_conv_b'], 'ssm_dt_bias_f': out['ssm_dt_bias_f'], 'ssm_dt_bias_b': out['ssm_dt_bias_b'], 'ssm_a_log_f': out['ssm_a_log_f'], 'ssm_a_log_b': out['ssm_a_log_b'], 'ssm_d': out['ssm_d'], 'ssm_norm_w': out['ssm_norm_w'], 'sc_conv_w': out['sc_conv_w'], 'sc_norm_w': out['sc_norm_w'], 'w_out': out['w_out'], 'ln1_g': out['ln1_g'], 'ln1_b': out['ln1_b'], 'w_up': out['w_up'], 'w_down': out['w_down'], 'ln2_g': out['ln2_g'], 'ln2_b': out['ln2_b'], 'loss_target': out['loss_target'], 'm_w_ada': out['m_w_ada'], 'm_b_ada': out['m_b_ada'], 'm_w_in': out['m_w_in'], 'm_ssm_conv_w': out['m_ssm_conv_w'], 'm_ssm_conv_b': out['m_ssm_conv_b'], 'm_ssm_dt_bias_f': out['m_ssm_dt_bias_f'], 'm_ssm_dt_bias_b': out['m_ssm_dt_bias_b'], 'm_ssm_a_log_f': out['m_ssm_a_log_f'], 'm_ssm_a_log_b': out['m_ssm_a_log_b'], 'm_ssm_d': out['m_ssm_d'], 'm_ssm_norm_w': out['m_ssm_norm_w'], 'm_sc_conv_w': out['m_sc_conv_w'], 'm_sc_norm_w': out['m_sc_norm_w'], 'm_w_out': out['m_w_out'], 'm_ln1_g': out['m_ln1_g'], 'm_ln1_b': out['m_ln1_b'], 'm_w_up': out['m_w_up'], 'm_w_down': out['m_w_down'], 'm_ln2_g': out['m_ln2_g'], 'm_ln2_b': out['m_ln2_b'], 'v_w_ada': out['v_w_ada'], 'v_b_ada': out['v_b_ada'], 'v_w_in': out['v_w_in'], 'v_ssm_conv_w': out['v_ssm_conv_w'], 'v_ssm_conv_b': out['v_ssm_conv_b'], 'v_ssm_dt_bias_f': out['v_ssm_dt_bias_f'], 'v_ssm_dt_bias_b': out['v_ssm_dt_bias_b'], 'v_ssm_a_log_f': out['v_ssm_a_log_f'], 'v_ssm_a_log_b': out['v_ssm_a_log_b'], 'v_ssm_d': out['v_ssm_d'], 'v_ssm_norm_w': out['v_ssm_norm_w'], 'v_sc_conv_w': out['v_sc_conv_w'], 'v_sc_norm_w': out['v_sc_norm_w'], 'v_w_out': out['v_w_out'], 'v_ln1_g': out['v_ln1_g'], 'v_ln1_b': out['v_ln1_b'], 'v_w_up': out['v_w_up'], 'v_w_down': out['v_w_down'], 'v_ln2_g': out['v_ln2_g'], 'v_ln2_b': out['v_ln2_b']}


def _loss(weights, diff, rest, loss_target):
    with _jax.named_scope("forward"):
        args = {**rest, TWIN_DIFF_INPUT: diff, **{k: w.astype(_WEIGHT_DTYPES[k]) for k, w in weights.items()}}
        y = _forward(args)
    with _jax.named_scope("loss_head"):
        err = _jnp.square(y.astype(_jnp.float32) - loss_target)
        return 0.5 * _jnp.sum(_jnp.mean(err, axis=-1)) if err.ndim else 0.5 * err


def _adamw(w, g, m, v):
    m = ADAM_B1 * m + (1.0 - ADAM_B1) * g
    v = ADAM_B2 * v + (1.0 - ADAM_B2) * _jnp.square(g)
    m_hat = m / (1.0 - ADAM_B1 ** ADAM_STEP)
    v_hat = v / (1.0 - ADAM_B2 ** ADAM_STEP)
    delta = -ADAM_LR * (m_hat / (_jnp.sqrt(v_hat) + ADAM_EPS) + ADAM_WD * w)
    return delta, m, v


def reference(x, c, w_ada, b_ada, w_in, ssm_conv_w, ssm_conv_b, ssm_dt_bias_f, ssm_dt_bias_b, ssm_a_log_f, ssm_a_log_b, ssm_d, ssm_norm_w, sc_conv_w, sc_norm_w, w_out, ln1_g, ln1_b, w_up, w_down, ln2_g, ln2_b, loss_target, m_w_ada, m_b_ada, m_w_in, m_ssm_conv_w, m_ssm_conv_b, m_ssm_dt_bias_f, m_ssm_dt_bias_b, m_ssm_a_log_f, m_ssm_a_log_b, m_ssm_d, m_ssm_norm_w, m_sc_conv_w, m_sc_norm_w, m_w_out, m_ln1_g, m_ln1_b, m_w_up, m_w_down, m_ln2_g, m_ln2_b, v_w_ada, v_b_ada, v_w_in, v_ssm_conv_w, v_ssm_conv_b, v_ssm_dt_bias_f, v_ssm_dt_bias_b, v_ssm_a_log_f, v_ssm_a_log_b, v_ssm_d, v_ssm_norm_w, v_sc_conv_w, v_sc_norm_w, v_w_out, v_ln1_g, v_ln1_b, v_w_up, v_w_down, v_ln2_g, v_ln2_b):
    given = dict(x=x, c=c, w_ada=w_ada, b_ada=b_ada, w_in=w_in, ssm_conv_w=ssm_conv_w, ssm_conv_b=ssm_conv_b, ssm_dt_bias_f=ssm_dt_bias_f, ssm_dt_bias_b=ssm_dt_bias_b, ssm_a_log_f=ssm_a_log_f, ssm_a_log_b=ssm_a_log_b, ssm_d=ssm_d, ssm_norm_w=ssm_norm_w, sc_conv_w=sc_conv_w, sc_norm_w=sc_norm_w, w_out=w_out, ln1_g=ln1_g, ln1_b=ln1_b, w_up=w_up, w_down=w_down, ln2_g=ln2_g, ln2_b=ln2_b, loss_target=loss_target, m_w_ada=m_w_ada, m_b_ada=m_b_ada, m_w_in=m_w_in, m_ssm_conv_w=m_ssm_conv_w, m_ssm_conv_b=m_ssm_conv_b, m_ssm_dt_bias_f=m_ssm_dt_bias_f, m_ssm_dt_bias_b=m_ssm_dt_bias_b, m_ssm_a_log_f=m_ssm_a_log_f, m_ssm_a_log_b=m_ssm_a_log_b, m_ssm_d=m_ssm_d, m_ssm_norm_w=m_ssm_norm_w, m_sc_conv_w=m_sc_conv_w, m_sc_norm_w=m_sc_norm_w, m_w_out=m_w_out, m_ln1_g=m_ln1_g, m_ln1_b=m_ln1_b, m_w_up=m_w_up, m_w_down=m_w_down, m_ln2_g=m_ln2_g, m_ln2_b=m_ln2_b, v_w_ada=v_w_ada, v_b_ada=v_b_ada, v_w_in=v_w_in, v_ssm_conv_w=v_ssm_conv_w, v_ssm_conv_b=v_ssm_conv_b, v_ssm_dt_bias_f=v_ssm_dt_bias_f, v_ssm_dt_bias_b=v_ssm_dt_bias_b, v_ssm_a_log_f=v_ssm_a_log_f, v_ssm_a_log_b=v_ssm_a_log_b, v_ssm_d=v_ssm_d, v_ssm_norm_w=v_ssm_norm_w, v_sc_conv_w=v_sc_conv_w, v_sc_norm_w=v_sc_norm_w, v_w_out=v_w_out, v_ln1_g=v_ln1_g, v_ln1_b=v_ln1_b, v_w_up=v_w_up, v_w_down=v_w_down, v_ln2_g=v_ln2_g, v_ln2_b=v_ln2_b)
    weights = {n: given[n] for n in TWIN_WEIGHTS}
    shared = {n: given[n] for n in SHARED_INPUTS}
    per_example = {n: given[n] for n in ['x', 'c']}
    grad_fn = _jax.value_and_grad(_loss, argnums=(0, 1))

    def one_microbatch(ex, loss_target):
        ex = dict(ex)
        diff = ex.pop(TWIN_DIFF_INPUT)
        return grad_fn(weights, diff, {**shared, **ex}, loss_target)

    if N_MICROBATCH == 1:
        loss, (grad_w, grad_x) = one_microbatch(per_example, given["loss_target"])
    else:
        def body(carry, xs):
            loss_sum, grad_sum = carry
            l_k, (gw_k, gx_k) = one_microbatch(xs[0], xs[1])
            with _jax.named_scope("update"):
                return (loss_sum + l_k, _jax.tree.map(_jnp.add, grad_sum, gw_k)), gx_k

        init = (_jnp.zeros((), _jnp.float32), _jax.tree.map(_jnp.zeros_like, weights))
        (loss, grad_w), grad_x = _jax.lax.scan(body, init, (per_example, given["loss_target"]))
    with _jax.named_scope("update"):
        delta_w, new_m, new_v = {}, {}, {}
        for n in TWIN_WEIGHTS:
            delta_w[n], new_m[n], new_v[n] = _adamw(weights[n], grad_w[n], given["m_" + n], given["v_" + n])
    return (loss, grad_x, *[grad_w[n] for n in TWIN_WEIGHTS], *[delta_w[n] for n in TWIN_WEIGHTS],
            *[new_m[n] for n in TWIN_WEIGHTS], *[new_v[n] for n in TWIN_WEIGHTS])
```

```python
import functools

import jax
import jax.numpy as jnp
from jax import lax
from jax.experimental import pallas as pl
from jax.experimental.pallas import tpu as pltpu

F32 = jnp.float32
BF16 = jnp.bfloat16

CHUNK = 128
HEAD_DIM = 64
STATE = 128
HEADS_PER_GROUP = 4
SC_GROUP_WIDTH = 128
SSM_CONV = 5
SC_CONV = 3
N_MOD = 6
DEEPNORM_ALPHA = 2.0 ** 0.25
LN_EPS = 1e-5
RMS_EPS = 1e-5
ADAM_LR = 0.001
ADAM_B1 = 0.9
ADAM_B2 = 0.999
ADAM_EPS = 1e-08
ADAM_WD = 0.01
ADAM_STEP = 10
N_CHIPS = 4
N_DEV = 8
LANES = 128
SUBLANES = 8
HALO = 8
VMEM_LIMIT = 56 * 1024 * 1024
MESH = pl.DeviceIdType.MESH


def _params(sem=None):
    return pltpu.CompilerParams(dimension_semantics=sem, vmem_limit_bytes=VMEM_LIMIT)


def _pick(n, target, mult=LANES):
    best = None
    t = mult
    while t <= min(n, target):
        if n % t == 0:
            best = t
        t += mult
    return best if best is not None else n


ROW_TILE_BYTES = 1 << 20


def _tile_rows(s, width, mult=SUBLANES):
    return _pick(s, max(mult, ROW_TILE_BYTES // (4 * width)), mult)


def _sigmoid(v):
    return 1.0 / (1.0 + jnp.exp(-v))


def _silu(v):
    return v * _sigmoid(v)


def _dsilu(v):
    s = _sigmoid(v)
    return s * (1.0 + v * (1.0 - s))


def _softplus(v):
    e = jnp.exp(-jnp.abs(v))
    return jnp.maximum(v, 0.0) + jnp.where(e < 1e-4, e - 0.5 * e * e, jnp.log(1.0 + e))


def _dot(a, b):
    return jnp.dot(a, b, preferred_element_type=F32)


def _dot_nt(a, b):
    return lax.dot_general(a, b, (((1,), (1,)), ((), ())), preferred_element_type=F32)


def _dot_tn(a, b):
    return lax.dot_general(a, b, (((0,), (0,)), ((), ())), preferred_element_type=F32)


def _split3(v):
    hi = v.astype(BF16)
    r1 = v - hi.astype(F32)
    mid = r1.astype(BF16)
    lo = (r1 - mid.astype(F32)).astype(BF16)
    return hi, mid, lo


def _dot3_r(v, onehot):
    hi, mid, lo = _split3(v)
    return _dot(hi, onehot) + _dot(mid, onehot) + _dot(lo, onehot)


def _dot3_l(onehot, v):
    hi, mid, lo = _split3(v)
    return _dot(onehot, hi) + _dot(onehot, mid) + _dot(onehot, lo)


def _matmul(a, b, *, name, ta=False, tb=False, out_dtype=F32, b_blocks=1, out_blocks=1,
            tm=1024, tn=1024, tk=1024):
    if ta:
        K, M = a.shape
    else:
        M, K = a.shape
    if b_blocks > 1:
        nb, r_, c_ = b.shape
        if tb:
            N, K2 = r_, c_ * nb
        else:
            K2, N = r_, c_ * nb
    else:
        if tb:
            N, K2 = b.shape
        else:
            K2, N = b.shape
    assert K == K2, (a.shape, b.shape, ta, tb)
    tm = _pick(M, tm)
    n_unit = N // b_blocks if (b_blocks > 1 and not tb) else N
    n_unit = min(n_unit, N // out_blocks)
    tn = _pick(n_unit, tn)
    k_unit = K // b_blocks if (b_blocks > 1 and tb) else K
    tk = _pick(k_unit, tk)
    gm, gn, gk = M // tm, N // tn, K // tk

    if ta:
        a_spec = pl.BlockSpec((tk, tm), lambda i, j, k: (k, i))
    else:
        a_spec = pl.BlockSpec((tm, tk), lambda i, j, k: (i, k))
    if b_blocks > 1 and not tb:
        per = (N // b_blocks) // tn
        b_spec = pl.BlockSpec((None, tk, tn), lambda i, j, k: (j // per, k, j % per))
    elif b_blocks > 1 and tb:
        per = (K // b_blocks) // tk
        b_spec = pl.BlockSpec((None, tn, tk), lambda i, j, k: (k // per, j, k % per))
    elif tb:
        b_spec = pl.BlockSpec((tn, tk), lambda i, j, k: (j, k))
    else:
        b_spec = pl.BlockSpec((tk, tn), lambda i, j, k: (k, j))
    if out_blocks > 1:
        per_o = (N // out_blocks) // tn
        o_spec = pl.BlockSpec((None, tm, tn), lambda i, j, k: (j // per_o, i, j % per_o))
        o_shape = jax.ShapeDtypeStruct((out_blocks, M, N // out_blocks), out_dtype)
    else:
        o_spec = pl.BlockSpec((tm, tn), lambda i, j, k: (i, j))
        o_shape = jax.ShapeDtypeStruct((M, N), out_dtype)

    def body(a_ref, b_ref, o_ref, acc):
        k = pl.program_id(2)
        av = a_ref[...].astype(BF16)
        bv = b_ref[...].astype(BF16)
        if ta and tb:
            raise NotImplementedError
        if ta:
            p = _dot_tn(av, bv)
        elif tb:
            p = _dot_nt(av, bv)
        else:
            p = _dot(av, bv)

        @pl.when(k == 0)
        def _():
            acc[...] = p

        @pl.when(k > 0)
        def _():
            acc[...] += p

        @pl.when(k == gk - 1)
        def _():
            o_ref[...] = acc[...].astype(out_dtype)

    return pl.pallas_call(
        body, name=name, grid=(gm, gn, gk), in_specs=[a_spec, b_spec], out_specs=o_spec,
        out_shape=o_shape, scratch_shapes=[pltpu.VMEM((tm, tn), F32)],
        compiler_params=_params(("parallel", "parallel", "arbitrary")),
    )(a, b)


def _row(tr, w, blk=0):
    return pl.BlockSpec((tr, w), lambda i: (i, blk))


def _full(shape):
    nd = len(shape)
    return pl.BlockSpec(shape, lambda i: (0,) * nd)


def _halo_specs(s, tr, w, blk=0):
    per = tr // HALO
    last = s // HALO - 1
    return [
        pl.BlockSpec((HALO, w), lambda i: (jnp.maximum(i * per - 1, 0), blk)),
        pl.BlockSpec((tr, w), lambda i: (i, blk)),
        pl.BlockSpec((HALO, w), lambda i: (jnp.minimum((i + 1) * per, last), blk)),
    ]


def _ext(prev_ref, cur_ref, next_ref, s, tr):
    i = pl.program_id(0)
    e = jnp.concatenate([prev_ref[...].astype(F32), cur_ref[...].astype(F32), next_ref[...].astype(F32)], axis=0)
    rid = i * tr - HALO + lax.broadcasted_iota(jnp.int32, e.shape, 0)
    return jnp.where((rid >= 0) & (rid < s), e, 0.0)


def _valid_rows(shape, s, tr):
    i = pl.program_id(0)
    rid = i * tr - HALO + lax.broadcasted_iota(jnp.int32, shape, 0)
    return (rid >= 0) & (rid < s)


def _shift(e, k):
    if k == 0:
        return e
    n = e.shape[0]
    return pltpu.roll(e, (n - k) % n, 0)


def _acc_rows(ref, v):
    s = jnp.sum(v, axis=0, keepdims=True)

    @pl.when(pl.program_id(0) == 0)
    def _():
        ref[...] = s

    @pl.when(pl.program_id(0) > 0)
    def _():
        ref[...] += s


def _rowcall(body, name, s, tr, in_specs, out_specs, out_shape, args):
    return pl.pallas_call(
        body, name=name, grid=(s // tr,), in_specs=in_specs, out_specs=out_specs, out_shape=out_shape,
        compiler_params=_params(("arbitrary",)),
    )(*args)


def _modulate(x, scale, shift, *, name):
    s, d = x.shape
    tr = _tile_rows(s, d)

    def body(x_ref, sc_ref, sh_ref, o_ref):
        o_ref[...] = (x_ref[...] * (1.0 + sc_ref[...]) + sh_ref[...]).astype(BF16)

    return _rowcall(body, name, s, tr, [_row(tr, d), _full((1, d)), _full((1, d))], _row(tr, d),
                    jax.ShapeDtypeStruct((s, d), BF16), (x, scale, shift))


def _ln_stats(r):
    mu = jnp.mean(r, axis=-1, keepdims=True)
    xc = r - mu
    var = jnp.mean(xc * xc, axis=-1, keepdims=True)
    rstd = lax.rsqrt(var + LN_EPS)
    return xc * rstd, rstd


def _ln1_fwd(x, mix, gate, g, b, scale2, shift2):
    s, d = x.shape
    tr = _tile_rows(s, d)

    def body(x_ref, m_ref, gt_ref, g_ref, b_ref, sc_ref, sh_ref, x1_ref, h2_ref):
        r = DEEPNORM_ALPHA * x_ref[...] + (1.0 + gt_ref[...]) * m_ref[...]
        xh, _ = _ln_stats(r)
        x1 = xh * g_ref[...] + b_ref[...]
        x1_ref[...] = x1
        h2_ref[...] = (x1 * (1.0 + sc_ref[...]) + sh_ref[...]).astype(BF16)

    v = _full((1, d))
    return _rowcall(body, "ln1_fwd", s, tr, [_row(tr, d), _row(tr, d), v, v, v, v, v],
                    [_row(tr, d), _row(tr, d)],
                    [jax.ShapeDtypeStruct((s, d), F32), jax.ShapeDtypeStruct((s, d), BF16)],
                    (x, mix, gate, g, b, scale2, shift2))


def _relu2(u):
    s, f = u.shape
    tr = _tile_rows(s, f)

    def body(u_ref, o_ref):
        r = jnp.maximum(u_ref[...], 0.0)
        o_ref[...] = (r * r).astype(BF16)

    return _rowcall(body, "relu2", s, tr, [_row(tr, f)], _row(tr, f), jax.ShapeDtypeStruct((s, f), BF16), (u,))


def _ln2_loss_bwd(x1, f, target, gate, g, b):
    s, d = x1.shape
    tr = _tile_rows(s, d)

    def body(x1_ref, f_ref, t_ref, gt_ref, g_ref, b_ref, df_ref, dr_ref, loss_ref, dg_ref, db_ref, dgt_ref):
        fv = f_ref[...]
        r = DEEPNORM_ALPHA * x1_ref[...] + (1.0 + gt_ref[...]) * fv
        xh, rstd = _ln_stats(r)
        y = xh * g_ref[...] + b_ref[...]
        err = y - t_ref[...]
        _acc_rows(loss_ref, 0.5 * jnp.mean(err * err, axis=-1, keepdims=True))
        dy = err * (1.0 / d)
        _acc_rows(dg_ref, dy * xh)
        _acc_rows(db_ref, dy)
        dxh = dy * g_ref[...]
        dr = rstd * (dxh - jnp.mean(dxh, axis=-1, keepdims=True) - xh * jnp.mean(dxh * xh, axis=-1, keepdims=True))
        dr_ref[...] = dr
        df_ref[...] = ((1.0 + gt_ref[...]) * dr).astype(BF16)
        _acc_rows(dgt_ref, dr * fv)

    v = _full((1, d))
    one = _full((1, 1))
    return _rowcall(body, "ln2_loss_bwd", s, tr, [_row(tr, d), _row(tr, d), _row(tr, d), v, v, v],
                    [_row(tr, d), _row(tr, d), one, v, v, v],
                    [jax.ShapeDtypeStruct((s, d), BF16), jax.ShapeDtypeStruct((s, d), F32),
                     jax.ShapeDtypeStruct((1, 1), F32)] + [jax.ShapeDtypeStruct((1, d), F32)] * 3,
                    (x1, f, target, gate, g, b))


def _relu2_bwd(dff, u):
    s, f = u.shape
    tr = _tile_rows(s, f)

    def body(d_ref, u_ref, o_ref):
        o_ref[...] = (d_ref[...] * 2.0 * jnp.maximum(u_ref[...], 0.0)).astype(BF16)

    return _rowcall(body, "relu2_bwd", s, tr, [_row(tr, f), _row(tr, f)], _row(tr, f),
                    jax.ShapeDtypeStruct((s, f), BF16), (dff, u))


def _ln1_bwd(dh2, dr2, x1, x, mix, scale2, gate1, g1):
    s, d = x.shape
    tr = _tile_rows(s, d)

    def body(dh_ref, dr2_ref, x1_ref, x_ref, m_ref, sc_ref, gt_ref, g_ref,
             dm_ref, dxa_ref, dsc_ref, dsh_ref, dg_ref, db_ref, dgt_ref):
        dh = dh_ref[...]
        _acc_rows(dsc_ref, dh * x1_ref[...])
        _acc_rows(dsh_ref, dh)
        dy = dh * (1.0 + sc_ref[...]) + DEEPNORM_ALPHA * dr2_ref[...]
        mv = m_ref[...]
        r = DEEPNORM_ALPHA * x_ref[...] + (1.0 + gt_ref[...]) * mv
        xh, rstd = _ln_stats(r)
        _acc_rows(dg_ref, dy * xh)
        _acc_rows(db_ref, dy)
        dxh = dy * g_ref[...]
        dr = rstd * (dxh - jnp.mean(dxh, axis=-1, keepdims=True) - xh * jnp.mean(dxh * xh, axis=-1, keepdims=True))
        dm_ref[...] = ((1.0 + gt_ref[...]) * dr).astype(BF16)
        dxa_ref[...] = DEEPNORM_ALPHA * dr
        _acc_rows(dgt_ref, dr * mv)

    v = _full((1, d))
    return _rowcall(body, "ln1_bwd", s, tr, [_row(tr, d)] * 5 + [v, v, v],
                    [_row(tr, d), _row(tr, d), v, v, v, v, v],
                    [jax.ShapeDtypeStruct((s, d), BF16), jax.ShapeDtypeStruct((s, d), F32)]
                    + [jax.ShapeDtypeStruct((1, d), F32)] * 5,
                    (dh2, dr2, x1, x, mix, scale2, gate1, g1))


def _dx_final(dxa, dh1, x, scale1):
    s, d = x.shape
    tr = _tile_rows(s, d)

    def body(a_ref, dh_ref, x_ref, sc_ref, o_ref, dsc_ref, dsh_ref):
        dh = dh_ref[...]
        o_ref[...] = a_ref[...] + dh * (1.0 + sc_ref[...])
        _acc_rows(dsc_ref, dh * x_ref[...])
        _acc_rows(dsh_ref, dh)

    v = _full((1, d))
    return _rowcall(body, "dx_final", s, tr, [_row(tr, d)] * 3 + [v], [_row(tr, d), v, v],
                    [jax.ShapeDtypeStruct((s, d), F32)] + [jax.ShapeDtypeStruct((1, d), F32)] * 2,
                    (dxa, dh1, x, scale1))


def _conv_silu_fwd(proj, conv_w, conv_b, s, u):
    tr = _tile_rows(s, u)
    w = 2 * u
    half = SSM_CONV // 2

    def body(p0, c0, n0, p1, c1, n1, w_ref, b_ref, o_ref):
        for blk, (pr, cr, nr) in enumerate(((p0, c0, n0), (p1, c1, n1))):
            e = _ext(pr, cr, nr, s, tr)
            wv = w_ref[:, blk * u:(blk + 1) * u]
            acc = jnp.zeros_like(e)
            for k in range(SSM_CONV):
                acc = acc + _shift(e, k - half) * wv[k:k + 1, :]
            pre = acc[HALO:HALO + tr] + b_ref[:, blk * u:(blk + 1) * u]
            o_ref[:, blk * u:(blk + 1) * u] = _silu(pre)

    in_specs = _halo_specs(s, tr, u, 1) + _halo_specs(s, tr, u, 2) + [_full((SSM_CONV, w)), _full((1, w))]
    return _rowcall(body, "conv_silu_fwd", s, tr, in_specs, _row(tr, w), jax.ShapeDtypeStruct((s, w), F32),
                    (proj,) * 6 + (conv_w, conv_b))


def _conv_silu_bwd(proj, dxbc, conv_w, conv_b, s, u):
    tr = _tile_rows(s, u)
    w = 2 * u
    half = SSM_CONV // 2

    def body(p0, c0, n0, p1, c1, n1, dp0, dc0, dn0, dp1, dc1, dn1, w_ref, b_ref, du_ref, dw_ref, db_ref):
        for blk, (ur, dr) in enumerate((((p0, c0, n0), (dp0, dc0, dn0)), ((p1, c1, n1), (dp1, dc1, dn1)))):
            e = _ext(*ur, s, tr)
            de = _ext(*dr, s, tr)
            wv = w_ref[:, blk * u:(blk + 1) * u]
            acc = jnp.zeros_like(e)
            for k in range(SSM_CONV):
                acc = acc + _shift(e, k - half) * wv[k:k + 1, :]
            pre = acc + b_ref[:, blk * u:(blk + 1) * u]
            dpre = jnp.where(_valid_rows(e.shape, s, tr), de * _dsilu(pre), 0.0)
            du = jnp.zeros_like(e)
            rows = []
            for k in range(SSM_CONV):
                du = du + _shift(dpre, half - k) * wv[k:k + 1, :]
                rows.append(jnp.sum((_shift(e, k - half) * dpre)[HALO:HALO + tr], axis=0, keepdims=True))
            du_ref[:, blk * u:(blk + 1) * u] = du[HALO:HALO + tr].astype(BF16)
            dwv = jnp.concatenate(rows + [jnp.zeros((SUBLANES - SSM_CONV, u), F32)], axis=0)
            dbv = jnp.sum(dpre[HALO:HALO + tr], axis=0, keepdims=True)
            first = pl.program_id(0) == 0

            @pl.when(first)
            def _():
                dw_ref[:, blk * u:(blk + 1) * u] = dwv
                db_ref[:, blk * u:(blk + 1) * u] = dbv

            @pl.when(jnp.logical_not(first))
            def _():
                dw_ref[:, blk * u:(blk + 1) * u] += dwv
                db_ref[:, blk * u:(blk + 1) * u] += dbv

    in_specs = (_halo_specs(s, tr, u, 1) + _halo_specs(s, tr, u, 2) + _halo_specs(s, tr, u, 0)
                + _halo_specs(s, tr, u, 1) + [_full((SSM_CONV, w)), _full((1, w))])
    return _rowcall(body, "conv_silu_bwd", s, tr, in_specs,
                    [_row(tr, w), _full((SUBLANES, w)), _full((1, w))],
                    [jax.ShapeDtypeStruct((s, w), BF16), jax.ShapeDtypeStruct((SUBLANES, w), F32),
                     jax.ShapeDtypeStruct((1, w), F32)],
                    (proj,) * 6 + (dxbc,) * 6 + (conv_w, conv_b))


def _expanders(h):
    col64 = jnp.arange(2 * h * HEAD_DIM) // HEAD_DIM
    col128 = jnp.arange(2 * h * LANES) // LANES
    row = jnp.arange(LANES)[:, None]
    return (row == col64[None, :]).astype(BF16), (row == col128[None, :]).astype(BF16)


def _dt_prep(proj, bias_row, a_row, s, u, h):
    q = CHUNK
    e64, e128 = _expanders(h)
    ds = h * HEAD_DIM
    dtblk = (6 * u) // LANES

    def body(raw_ref, b_ref, a_ref, e64_ref, e128_ref, dt_ref, cum_ref, dte_ref, cume_ref):
        lane = lax.broadcasted_iota(jnp.int32, (q, LANES), 1)
        dt = jnp.where(lane < 2 * h, _softplus(raw_ref[...] + b_ref[...]), 0.0)
        da = dt * a_ref[...]
        ii = lax.broadcasted_iota(jnp.int32, (q, q), 0)
        kk = lax.broadcasted_iota(jnp.int32, (q, q), 1)
        lower = (kk <= ii).astype(F32).astype(BF16)
        upper = (kk >= ii).astype(F32).astype(BF16)
        cum = jnp.where(lane < h, _dot3_l(lower, da), _dot3_l(upper, da))
        dt_ref[...] = dt
        cum_ref[...] = cum
        dte = _dot3_r(dt, e64_ref[...])
        cume = _dot3_r(cum, e128_ref[...])
        dte_ref[0] = dte[:, :ds]
        dte_ref[1] = dte[:, ds:]
        cume_ref[0] = cume[:, :h * LANES]
        cume_ref[1] = cume[:, h * LANES:]

    in_specs = [pl.BlockSpec((q, LANES), lambda i: (i, dtblk)), _full((1, LANES)), _full((1, LANES)),
                _full(e64.shape), _full(e128.shape)]
    out_specs = [_row(q, LANES), _row(q, LANES),
                 pl.BlockSpec((2, q, ds), lambda i: (0, i, 0)), pl.BlockSpec((2, q, h * LANES), lambda i: (0, i, 0))]
    out_shape = [jax.ShapeDtypeStruct((s, LANES), F32), jax.ShapeDtypeStruct((s, LANES), F32),
                 jax.ShapeDtypeStruct((2, s, ds), F32), jax.ShapeDtypeStruct((2, s, h * LANES), F32)]
    return _rowcall(body, "dt_prep", s, q, in_specs, out_specs, out_shape, (proj, bias_row, a_row, e64, e128))


def _ssd_specs(s, h, g):
    q = CHUNK
    nc = s // q
    ds = h * HEAD_DIM
    nb = g * STATE
    return q, nc, ds, nb


def _ssd_fwd(xbc, dt_e, cum_e, cum_t, s, h, g):
    q, nc, ds, nb = _ssd_specs(s, h, g)
    npair = h // 2

    def cidx(d, i):
        return jnp.where(d == 0, i, nc - 1 - i)

    def body(x_ref, b_ref, c_ref, dt_ref, cum_ref, cumt_ref, y_ref, sp_ref, st):
        d = pl.program_id(0)
        i = pl.program_id(1)

        @pl.when(i == 0)
        def _():
            st[...] = jnp.zeros_like(st)

        rev = d == 1
        ii = lax.broadcasted_iota(jnp.int32, (q, q), 0)
        jj = lax.broadcasted_iota(jnp.int32, (q, q), 1)
        sgn = jnp.where(rev, -1, 1)
        mask = (jj - ii) * sgn <= 0
        left = lax.broadcasted_iota(jnp.int32, (q, LANES), 1) < HEAD_DIM

        def group(gi, carry):
            goff = pl.multiple_of(gi * STATE, STATE)
            cg = c_ref[:, pl.ds(goff, STATE)].astype(BF16)
            bg = b_ref[:, pl.ds(goff, STATE)].astype(BF16)
            gm = _dot_nt(cg, bg)
            for p in range(HEADS_PER_GROUP // 2):
                pr = gi * (HEADS_PER_GROUP // 2) + p
                off = pl.multiple_of(pr * LANES, LANES)
                xd = x_ref[:, pl.ds(off, LANES)] * dt_ref[:, pl.ds(off, LANES)]
                ms = []
                cols = []
                for hl in range(2):
                    hh = 2 * pr + hl
                    col = cum_ref[:, pl.ds(pl.multiple_of(hh * LANES, LANES), LANES)]
                    row = cumt_ref[pl.ds(hh, 1), :]
                    lm = jnp.where(mask, jnp.exp(jnp.minimum(col - row, 0.0)), 0.0)
                    ms.append((gm * lm).astype(BF16))
                    cols.append(col)
                y = _dot(ms[0], jnp.where(left, xd, 0.0).astype(BF16)) + _dot(ms[1], jnp.where(left, 0.0, xd).astype(BF16))
                ce = jnp.where(left, cols[0], cols[1])
                sprev = st[pr]
                sp_ref[pr] = sprev
                y = y + jnp.exp(ce) * _dot(cg, sprev.astype(BF16))
                y_ref[:, pl.ds(off, LANES)] = y
                tot = jnp.where(rev, ce[0:1, :], ce[q - 1:q, :])
                v = (xd * jnp.exp(tot - ce)).astype(BF16)
                st[pr] = jnp.exp(tot) * sprev + _dot_tn(bg, v)
            return carry

        lax.fori_loop(0, g, group, 0)

    in_specs = [
        pl.BlockSpec((q, ds), lambda d, i: (cidx(d, i), 0)),
        pl.BlockSpec((q, nb), lambda d, i: (cidx(d, i), ds // nb)),
        pl.BlockSpec((q, nb), lambda d, i: (cidx(d, i), ds // nb + 1)),
        pl.BlockSpec((None, q, ds), lambda d, i: (d, cidx(d, i), 0)),
        pl.BlockSpec((None, q, h * LANES), lambda d, i: (d, cidx(d, i), 0)),
        pl.BlockSpec((None, h, q), lambda d, i: (d, 0, cidx(d, i))),
    ]
    out_specs = [
        pl.BlockSpec((None, q, ds), lambda d, i: (d, cidx(d, i), 0)),
        pl.BlockSpec((None, None, npair, STATE, LANES), lambda d, i: (d, cidx(d, i), 0, 0, 0)),
    ]
    out_shape = [jax.ShapeDtypeStruct((2, s, ds), F32), jax.ShapeDtypeStruct((2, nc, npair, STATE, LANES), F32)]
    return pl.pallas_call(
        body, name="ssd_fwd", grid=(2, nc), in_specs=in_specs, out_specs=out_specs, out_shape=out_shape,
        scratch_shapes=[pltpu.VMEM((npair, STATE, LANES), F32)],
        compiler_params=_params(("arbitrary", "arbitrary")),
    )(xbc, xbc, xbc, dt_e, cum_e, cum_t)


def _ssd_bwd(xbc, dt_e, cum_e, cum_t, dt_t, a_col, dy, sp, s, h, g):
    q, nc, ds, nb = _ssd_specs(s, h, g)
    npair = h // 2

    def cidx(d, i):
        return jnp.where(d == 0, nc - 1 - i, i)

    def body(x_ref, b_ref, c_ref, dt_ref, cum_ref, cumt_ref, dtt_ref, a_ref, dy_ref, sp_ref,
             dx_ref, db_ref, dc_ref, ddt_ref, da_ref, dst, rowp):
        d = pl.program_id(0)
        i = pl.program_id(1)

        @pl.when(i == 0)
        def _():
            dst[...] = jnp.zeros_like(dst)
            da_ref[...] = jnp.zeros_like(da_ref)

        rev = d == 1
        ii = lax.broadcasted_iota(jnp.int32, (q, q), 0)
        jj = lax.broadcasted_iota(jnp.int32, (q, q), 1)
        sgn = jnp.where(rev, -1, 1)
        mask = (jj - ii) * sgn <= 0
        lane = lax.broadcasted_iota(jnp.int32, (q, LANES), 1)
        left = lane < HEAD_DIM
        rowp[...] = jnp.zeros_like(rowp)

        def group(gi, carry):
            acc_dcum, acc_tot, acc_dxx = carry
            goff = pl.multiple_of(gi * STATE, STATE)
            cg = c_ref[:, pl.ds(goff, STATE)].astype(BF16)
            bg = b_ref[:, pl.ds(goff, STATE)].astype(BF16)
            gm = _dot_nt(cg, bg)
            dgm = jnp.zeros((q, q), F32)
            dcg = jnp.zeros((q, STATE), F32)
            dbg = jnp.zeros((q, STATE), F32)
            for p in range(HEADS_PER_GROUP // 2):
                pr = gi * (HEADS_PER_GROUP // 2) + p
                off = pl.multiple_of(pr * LANES, LANES)
                xv = x_ref[:, pl.ds(off, LANES)]
                dte = dt_ref[:, pl.ds(off, LANES)]
                xd = xv * dte
                xdb = xd.astype(BF16)
                dyv = dy_ref[:, pl.ds(off, LANES)]
                sprev = sp_ref[pr]
                sprevb = sprev.astype(BF16)
                dsn = dst[pr]
                dsnb = dsn.astype(BF16)
                cols = [cum_ref[:, pl.ds(pl.multiple_of((2 * pr + hl) * LANES, LANES), LANES)] for hl in range(2)]
                ce = jnp.where(left, cols[0], cols[1])
                tot = jnp.where(rev, ce[0:1, :], ce[q - 1:q, :])
                et = jnp.exp(tot)
                r = jnp.exp(tot - ce)
                e = jnp.exp(ce)
                yoff = e * _dot(cg, sprevb)
                dz = (e * dyv).astype(BF16)
                dcg = dcg + _dot_nt(dz, sprevb)
                dsprev = _dot_tn(cg, dz) + et * dsn
                f1 = dyv * yoff
                v = (xd * r).astype(BF16)
                dbg = dbg + _dot_nt(v, dsnb)
                dv = _dot(bg, dsnb)
                dxd = dv * r
                tt = dv * xd * r
                wt = dsn * sprev * et
                for hl in range(2):
                    hh = 2 * pr + hl
                    hm = left if hl == 0 else jnp.logical_not(left)
                    row = cumt_ref[pl.ds(hh, 1), :]
                    lm = jnp.where(mask, jnp.exp(jnp.minimum(cols[hl] - row, 0.0)), 0.0)
                    mf = gm * lm
                    dym = jnp.where(hm, dyv, 0.0).astype(BF16)
                    dm = _dot_nt(dym, xdb)
                    dxd = dxd + _dot_tn(mf.astype(BF16), dym)
                    dgm = dgm + dm * lm
                    em = dm * mf
                    rowp[pl.ds(hh, 1), :] = rowp[pl.ds(hh, 1), :] - jnp.sum(em, axis=0, keepdims=True)
                    colq = (jnp.sum(em, axis=1, keepdims=True)
                            + jnp.sum(jnp.where(hm, f1 - tt, 0.0), axis=1, keepdims=True))
                    acc_dcum = jnp.where(lane == hh, colq, acc_dcum)
                    totq = jnp.sum(jnp.sum(jnp.where(hm, tt + wt, 0.0), axis=1, keepdims=True), axis=0, keepdims=True)
                    acc_tot = jnp.where(lane == hh, totq, acc_tot)
                dxx = dxd * xv
                for hl in range(2):
                    hh = 2 * pr + hl
                    hm = left if hl == 0 else jnp.logical_not(left)
                    acc_dxx = jnp.where(lane == hh, jnp.sum(jnp.where(hm, dxx, 0.0), axis=1, keepdims=True), acc_dxx)
                dx_ref[:, pl.ds(off, LANES)] = dxd * dte
                dst[pr] = dsprev
            dgb = dgm.astype(BF16)
            dc_ref[:, pl.ds(goff, STATE)] = dcg + _dot(dgb, bg)
            db_ref[:, pl.ds(goff, STATE)] = dbg + _dot_tn(dgb, cg)
            return acc_dcum, acc_tot, acc_dxx

        zero = jnp.zeros((q, LANES), F32)
        acc_dcum, acc_tot, acc_dxx = lax.fori_loop(0, g, group, (zero, zero, zero))
        dcum_t = rowp[...] + jnp.transpose(acc_dcum)[:h]
        rmat = ((ii - jj) * sgn >= 0).astype(F32).astype(BF16)
        da_t = _dot3_r(dcum_t, rmat) + jnp.transpose(acc_tot)[:h]
        ddt_ref[...] = da_t * a_ref[...] + jnp.transpose(acc_dxx)[:h]
        da_ref[...] += da_t * dtt_ref[...]

    in_specs = [
        pl.BlockSpec((q, ds), lambda d, i: (cidx(d, i), 0)),
        pl.BlockSpec((q, nb), lambda d, i: (cidx(d, i), ds // nb)),
        pl.BlockSpec((q, nb), lambda d, i: (cidx(d, i), ds // nb + 1)),
        pl.BlockSpec((None, q, ds), lambda d, i: (d, cidx(d, i), 0)),
        pl.BlockSpec((None, q, h * LANES), lambda d, i: (d, cidx(d, i), 0)),
        pl.BlockSpec((None, h, q), lambda d, i: (d, 0, cidx(d, i))),
        pl.BlockSpec((None, h, q), lambda d, i: (d, 0, cidx(d, i))),
        pl.BlockSpec((None, h, LANES), lambda d, i: (d, 0, 0)),
        pl.BlockSpec((q, ds), lambda d, i: (cidx(d, i), 0)),
        pl.BlockSpec((None, None, npair, STATE, LANES), lambda d, i: (d, cidx(d, i), 0, 0, 0)),
    ]
    out_specs = [
        pl.BlockSpec((None, q, ds), lambda d, i: (d, cidx(d, i), 0)),
        pl.BlockSpec((None, q, nb), lambda d, i: (d, cidx(d, i), 0)),
        pl.BlockSpec((None, q, nb), lambda d, i: (d, cidx(d, i), 0)),
        pl.BlockSpec((None, h, q), lambda d, i: (d, 0, cidx(d, i))),
        pl.BlockSpec((None, h, LANES), lambda d, i: (d, 0, 0)),
    ]
    out_shape = [jax.ShapeDtypeStruct((2, s, ds), F32), jax.ShapeDtypeStruct((2, s, nb), F32),
                 jax.ShapeDtypeStruct((2, s, nb), F32), jax.ShapeDtypeStruct((2, h, s), F32),
                 jax.ShapeDtypeStruct((2, h, LANES), F32)]
    return pl.pallas_call(
        body, name="ssd_bwd", grid=(2, nc), in_specs=in_specs, out_specs=out_specs, out_shape=out_shape,
        scratch_shapes=[pltpu.VMEM((npair, STATE, LANES), F32), pltpu.VMEM((h, q), F32)],
        compiler_params=_params(("arbitrary", "arbitrary")),
    )(xbc, xbc, xbc, dt_e, cum_e, cum_t, dt_t, a_col, dy, sp)


def _dt_bwd(ddt, proj, bias_row, s, u, h):
    tr = _tile_rows(s, 4 * LANES)
    dtblk = (6 * u) // LANES

    def body(d_ref, raw_ref, b_ref, o_ref, db_ref):
        lane = lax.broadcasted_iota(jnp.int32, (tr, LANES), 1)
        v = jnp.where(lane < 2 * h, d_ref[...] * _sigmoid(raw_ref[...] + b_ref[...]), 0.0)
        o_ref[...] = v.astype(BF16)
        _acc_rows(db_ref, v)

    return _rowcall(body, "dt_bwd", s, tr, [_row(tr, LANES), _row(tr, LANES, dtblk), _full((1, LANES))],
                    [_row(tr, LANES), _full((1, LANES))],
                    [jax.ShapeDtypeStruct((s, LANES), BF16), jax.ShapeDtypeStruct((1, LANES), F32)],
                    (ddt, proj, bias_row))


def _group_rms(v, gw):
    outs, facs = [], []
    for k in range(v.shape[1] // gw):
        blk = v[:, k * gw:(k + 1) * gw]
        f = lax.rsqrt(jnp.mean(blk * blk, axis=-1, keepdims=True) + RMS_EPS)
        outs.append(blk * f)
        facs.append(jnp.broadcast_to(f, blk.shape))
    return jnp.concatenate(outs, axis=1), jnp.concatenate(facs, axis=1)


def _group_rms_bwd(dn, n, fac, gw):
    outs = []
    for k in range(n.shape[1] // gw):
        sl = slice(k * gw, (k + 1) * gw)
        outs.append(fac[:, sl] * (dn[:, sl] - n[:, sl] * jnp.mean(dn[:, sl] * n[:, sl], axis=-1, keepdims=True)))
    return jnp.concatenate(outs, axis=1)


def _gate_norm_fwd(y2, xbc, proj, d_e, norm_w, s, u, g):
    tr = _tile_rows(s, u)
    gw = u // g

    def body(y_ref, x_ref, z_ref, d_ref, w_ref, o_ref):
        ys = y_ref[0] + y_ref[1] + d_ref[...] * x_ref[...]
        n, _ = _group_rms(ys * _silu(z_ref[...]), gw)
        o_ref[...] = (n * w_ref[...]).astype(BF16)

    return _rowcall(body, "gate_norm_fwd", s, tr,
                    [pl.BlockSpec((2, tr, u), lambda i: (0, i, 0)), _row(tr, u), _row(tr, u), _full((1, u)), _full((1, u))],
                    _row(tr, u), jax.ShapeDtypeStruct((s, u), BF16), (y2, xbc, proj, d_e, norm_w))


def _gate_norm_bwd(dymix, y2, xbc, proj, d_e, norm_w, s, u, g):
    tr = _tile_rows(s, u)
    gw = u // g

    def body(dy_ref, y_ref, x_ref, z_ref, d_ref, w_ref, dys_ref, dz_ref, dxs_ref, dw_ref, dd_ref):
        xv = x_ref[...]
        zv = z_ref[...]
        ys = y_ref[0] + y_ref[1] + d_ref[...] * xv
        sz = _silu(zv)
        n, fac = _group_rms(ys * sz, gw)
        dout = dy_ref[...]
        _acc_rows(dw_ref, dout * n)
        dyg = _group_rms_bwd(dout * w_ref[...], n, fac, gw)
        dys = dyg * sz
        dys_ref[...] = dys
        dz_ref[...] = (dyg * ys * _dsilu(zv)).astype(BF16)
        dxs_ref[...] = dys * d_ref[...]
        _acc_rows(dd_ref, dys * xv)

    v = _full((1, u))
    return _rowcall(body, "gate_norm_bwd", s, tr,
                    [_row(tr, u), pl.BlockSpec((2, tr, u), lambda i: (0, i, 0)), _row(tr, u), _row(tr, u), v, v],
                    [_row(tr, u), _row(tr, u), _row(tr, u), v, v],
                    [jax.ShapeDtypeStruct((s, u), F32), jax.ShapeDtypeStruct((s, u), BF16),
                     jax.ShapeDtypeStruct((s, u), F32), jax.ShapeDtypeStruct((1, u), F32), jax.ShapeDtypeStruct((1, u), F32)],
                    (dymix, y2, xbc, proj, d_e, norm_w))


def _shortconv_fwd(proj, conv_w, norm_w, s, u):
    tr = _tile_rows(s, u)
    half = SC_CONV // 2

    def body(hp, hc, hn, b_ref, cp, cc, cn, cw_ref, w_ref, o_ref):
        t = _ext(hp, hc, hn, s, tr) * _ext(cp, cc, cn, s, tr)
        wv = cw_ref[...]
        acc = jnp.zeros_like(t)
        for k in range(SC_CONV):
            acc = acc + _shift(t, k - half) * wv[k:k + 1, :]
        n, _ = _group_rms(b_ref[...] * acc[HALO:HALO + tr], SC_GROUP_WIDTH)
        o_ref[...] = (n * w_ref[...]).astype(BF16)

    in_specs = _halo_specs(s, tr, u, 3) + [_row(tr, u, 4)] + _halo_specs(s, tr, u, 5) + [_full((SC_CONV, u)), _full((1, u))]
    return _rowcall(body, "shortconv_fwd", s, tr, in_specs, _row(tr, u), jax.ShapeDtypeStruct((s, u), BF16),
                    (proj,) * 7 + (conv_w, norm_w))


def _shortconv_bwd(dymix, proj, conv_w, norm_w, s, u):
    tr = _tile_rows(s, u)
    half = SC_CONV // 2

    def body(dp, dc_, dn, hp, hc, hn, bp, bc, bn, cp, cc, cn, cw_ref, w_ref, dh_ref, db_ref, dcc_ref, dcw_ref, dw_ref):
        dout = _ext(dp, dc_, dn, s, tr)
        hv = _ext(hp, hc, hn, s, tr)
        bv = _ext(bp, bc, bn, s, tr)
        cv = _ext(cp, cc, cn, s, tr)
        t = hv * cv
        wv = cw_ref[...]
        acc = jnp.zeros_like(t)
        for k in range(SC_CONV):
            acc = acc + _shift(t, k - half) * wv[k:k + 1, :]
        n, fac = _group_rms(bv * acc, SC_GROUP_WIDTH)
        cur = slice(HALO, HALO + tr)
        _acc_rows(dw_ref, (dout * n)[cur])
        dyv = _group_rms_bwd(dout * w_ref[...], n, fac, SC_GROUP_WIDTH)
        db_ref[...] = (dyv * acc)[cur].astype(BF16)
        dv = dyv * bv
        dt = jnp.zeros_like(t)
        rows = []
        for k in range(SC_CONV):
            dt = dt + _shift(dv, half - k) * wv[k:k + 1, :]
            rows.append(jnp.sum((_shift(t, k - half) * dv)[cur], axis=0, keepdims=True))
        dh_ref[...] = (dt * cv)[cur].astype(BF16)
        dcc_ref[...] = (dt * hv)[cur].astype(BF16)
        dwv = jnp.concatenate(rows + [jnp.zeros((SUBLANES - SC_CONV, u), F32)], axis=0)
        first = pl.program_id(0) == 0

        @pl.when(first)
        def _():
            dcw_ref[...] = dwv

        @pl.when(jnp.logical_not(first))
        def _():
            dcw_ref[...] += dwv

    in_specs = (_halo_specs(s, tr, u, 1) + _halo_specs(s, tr, u, 3) + _halo_specs(s, tr, u, 4) + _halo_specs(s, tr, u, 5)
                + [_full((SC_CONV, u)), _full((1, u))])
    return _rowcall(body, "shortconv_bwd", s, tr, in_specs,
                    [_row(tr, u)] * 3 + [_full((SUBLANES, u)), _full((1, u))],
                    [jax.ShapeDtypeStruct((s, u), BF16)] * 3
                    + [jax.ShapeDtypeStruct((SUBLANES, u), F32), jax.ShapeDtypeStruct((1, u), F32)],
                    (dymix,) * 3 + (proj,) * 9 + (conv_w, norm_w))


def _adam_math(w, g, m, v):
    m2 = ADAM_B1 * m + (1.0 - ADAM_B1) * g
    v2 = ADAM_B2 * v + (1.0 - ADAM_B2) * (g * g)
    m_hat = m2 / (1.0 - ADAM_B1 ** ADAM_STEP)
    v_hat = v2 / (1.0 - ADAM_B2 ** ADAM_STEP)
    delta = -ADAM_LR * (m_hat / (jnp.sqrt(v_hat) + ADAM_EPS) + ADAM_WD * w)
    return delta, m2, v2


def _adam_rows(r, c):
    return _pick(r, max(SUBLANES, (1 << 20) // (4 * c)), SUBLANES)


def _adam(w, g, m, v, *, name):
    r, c = w.shape
    tr = _adam_rows(r, c)

    def body(w_ref, g_ref, m_ref, v_ref, d_ref, m2_ref, v2_ref):
        d_ref[...], m2_ref[...], v2_ref[...] = _adam_math(w_ref[...], g_ref[...], m_ref[...], v_ref[...])

    return _rowcall(body, name, r, tr, [_row(tr, c)] * 4, [_row(tr, c)] * 3,
                    [jax.ShapeDtypeStruct((r, c), F32)] * 3, (w, g, m, v))


def _adam_outer(w, a_t, bmat, m, v, *, name):
    r, c = w.shape
    tr = _adam_rows(r, c)
    kk = a_t.shape[1]

    def body(w_ref, a_ref, b_ref, m_ref, v_ref, g_ref, d_ref, m2_ref, v2_ref):
        g = _dot(a_ref[...].astype(BF16), b_ref[...].astype(BF16))
        g_ref[...] = g
        d_ref[...], m2_ref[...], v2_ref[...] = _adam_math(w_ref[...], g, m_ref[...], v_ref[...])

    return _rowcall(body, name, r, tr, [_row(tr, c), _row(tr, kk), _full((kk, c)), _row(tr, c), _row(tr, c)],
                    [_row(tr, c)] * 4, [jax.ShapeDtypeStruct((r, c), F32)] * 4, (w, a_t, bmat, m, v))


ANY = pl.BlockSpec(memory_space=pl.ANY)
VMEM_WHOLE = pl.BlockSpec(memory_space=pltpu.VMEM)


def _place():
    x, y, c = lax.axis_index("x"), lax.axis_index("y"), lax.axis_index("c")
    return x, y, c


def _allgather_small(v, *, name):
    m_per, n = v.shape

    def body(x_ref, out_ref, send_sems, recv_sems, local_sem):
        x, y, c = _place()
        me, sibling = (x, y, c), (x, y, 1 - c)
        chips = [(1 - x, y), (x, 1 - y), (1 - x, 1 - y)]

        def rows(px, py, pc):
            return out_ref.at[pl.ds((4 * px + 2 * py + pc) * m_per, m_per), :]

        def copy(k, block, to, src=None):
            return pltpu.make_async_remote_copy(
                src_ref=rows(*block) if src is None else src, dst_ref=rows(*block),
                send_sem=send_sems.at[k], recv_sem=recv_sems.at[k], device_id=to, device_id_type=MESH)

        mine = pltpu.make_async_copy(x_ref, rows(*me), local_sem)
        mine.start()
        first = [copy(0, me, sibling, src=x_ref)]
        first += [copy(1 + j, me, (*chip, c), src=x_ref) for j, chip in enumerate(chips)]
        for cp in first:
            cp.start()
        passed = [copy(4 + j, (*chip, c), sibling) for j, chip in enumerate(chips)]
        for j, chip in enumerate(chips):
            copy(1 + j, (*chip, c), me).wait_recv()
            passed[j].start()
        copy(0, sibling, me).wait_recv()
        for j, chip in enumerate(chips):
            copy(4 + j, (*chip, 1 - c), me).wait_recv()
        for cp in first + passed:
            cp.wait_send()
        mine.wait()

    return pl.pallas_call(
        body, name=name, out_shape=jax.ShapeDtypeStruct((N_DEV * m_per, n), v.dtype),
        in_specs=[VMEM_WHOLE], out_specs=VMEM_WHOLE,
        scratch_shapes=[pltpu.SemaphoreType.DMA((7,)), pltpu.SemaphoreType.DMA((7,)), pltpu.SemaphoreType.DMA],
        compiler_params=pltpu.CompilerParams(vmem_limit_bytes=VMEM_LIMIT),
    )(v)


def _gather_weight(wl, *, name):
    r, c_ = wl.shape
    half = r // 2

    def body(w_ref, out_ref, send_sems, recv_sems, local_sem):
        x, y, c = _place()
        me, sibling = (x, y, c), (x, y, 1 - c)
        chips = [(1 - x, y), (x, 1 - y), (1 - x, 1 - y)]

        def blk(px, py, pc):
            return out_ref.at[2 * px + py, pl.ds(pc * half, half), :]

        def copy(k, block, to, src=None):
            return pltpu.make_async_remote_copy(
                src_ref=blk(*block) if src is None else src, dst_ref=blk(*block),
                send_sem=send_sems.at[k], recv_sem=recv_sems.at[k], device_id=to, device_id_type=MESH)

        mine = pltpu.make_async_copy(w_ref, out_ref.at[2 * x + y], local_sem)
        mine.start()
        my_half = w_ref.at[pl.ds(c * half, half), :]
        first = [copy(j, me, (*chip, c), src=my_half) for j, chip in enumerate(chips)]
        for cp in first:
            cp.start()
        passed = [copy(3 + j, (*chip, c), sibling) for j, chip in enumerate(chips)]
        for j, chip in enumerate(chips):
            copy(j, (*chip, c), me).wait_recv()
            passed[j].start()
        for j, chip in enumerate(chips):
            copy(3 + j, (*chip, 1 - c), me).wait_recv()
        for cp in first + passed:
            cp.wait_send()
        mine.wait()

    return pl.pallas_call(
        body, name=name, out_shape=jax.ShapeDtypeStruct((N_CHIPS, r, c_), wl.dtype),
        in_specs=[ANY], out_specs=ANY,
        scratch_shapes=[pltpu.SemaphoreType.DMA((6,)), pltpu.SemaphoreType.DMA((6,)), pltpu.SemaphoreType.DMA],
    )(wl)


def _scatter_grads(gfull, *, name):
    _, r, c_ = gfull.shape
    half = r // 2

    def body(g_ref, recv_ref, send_sems, recv_sems, local_sem):
        x, y, c = _place()

        def copy(q):
            tx = 1 - x if q & 4 else x
            ty = 1 - y if q & 2 else y
            tc = 1 - c if q & 1 else c
            return pltpu.make_async_remote_copy(
                src_ref=g_ref.at[2 * tx + ty, pl.ds(tc * half, half), :], dst_ref=recv_ref.at[q],
                send_sem=send_sems.at[q - 1], recv_sem=recv_sems.at[q - 1], device_id=(tx, ty, tc), device_id_type=MESH)

        mine = pltpu.make_async_copy(g_ref.at[2 * x + y, pl.ds(c * half, half), :], recv_ref.at[0], local_sem)
        mine.start()
        copies = [copy(q) for q in range(1, N_DEV)]
        for cp in copies:
            cp.start()
        for cp in copies:
            cp.wait_recv()
        for cp in copies:
            cp.wait_send()
        mine.wait()

    return pl.pallas_call(
        body, name=name, out_shape=jax.ShapeDtypeStruct((N_DEV, half, c_), gfull.dtype),
        in_specs=[ANY], out_specs=ANY,
        scratch_shapes=[pltpu.SemaphoreType.DMA((7,)), pltpu.SemaphoreType.DMA((7,)), pltpu.SemaphoreType.DMA],
    )(gfull)


def _sum_slots(recv, *, name):
    n, r, c_ = recv.shape
    tr = _pick(r, max(16, (1 << 19) // (2 * c_)), 16)

    def body(r_ref, o_ref):
        acc = r_ref[0].astype(F32)
        for k in range(1, n):
            acc = acc + r_ref[k].astype(F32)
        o_ref[...] = acc

    return _rowcall(body, name, r, tr, [pl.BlockSpec((n, tr, c_), lambda i: (0, i, 0))], _row(tr, c_),
                    jax.ShapeDtypeStruct((r, c_), F32), (recv,))


def _swap_halves(hv, *, name):
    half, c_ = hv.shape

    def body(h_ref, out_ref, send_sem, recv_sem, local_sem):
        x, y, c = _place()
        mine = pltpu.make_async_copy(h_ref, out_ref.at[pl.ds(c * half, half), :], local_sem)
        mine.start()
        send = pltpu.make_async_remote_copy(
            src_ref=h_ref, dst_ref=out_ref.at[pl.ds(c * half, half), :], send_sem=send_sem, recv_sem=recv_sem,
            device_id=(x, y, 1 - c), device_id_type=MESH)
        send.start()
        pltpu.make_async_remote_copy(
            src_ref=h_ref, dst_ref=out_ref.at[pl.ds((1 - c) * half, half), :], send_sem=send_sem, recv_sem=recv_sem,
            device_id=(x, y, 1 - c), device_id_type=MESH).wait_recv()
        send.wait_send()
        mine.wait()

    return pl.pallas_call(
        body, name=name, out_shape=jax.ShapeDtypeStruct((2 * half, c_), hv.dtype),
        in_specs=[ANY], out_specs=ANY,
        scratch_shapes=[pltpu.SemaphoreType.DMA, pltpu.SemaphoreType.DMA, pltpu.SemaphoreType.DMA],
    )(hv)


def _reduce_to_owner(gfull, *, name):
    recv = _scatter_grads(gfull, name=name + "_scatter")
    return _swap_halves(_sum_slots(recv, name=name + "_sum"), name=name + "_swap")


PACK_ROWS = 16


def _pack(parts):
    flat = [p.reshape(-1).astype(F32) for p in parts]
    n = sum(f.shape[0] for f in flat)
    unit = PACK_ROWS * LANES
    total = -(-n // unit) * unit
    if total > n:
        flat.append(jnp.zeros((total - n,), F32))
    where, off = [], 0
    for p in parts:
        where.append((off, p.shape))
        off += p.size
    return jnp.concatenate(flat).reshape(total // LANES, LANES), where


def _unpack(flat, where):
    v = flat.reshape(-1)
    return [v[off:off + _size(shape)].reshape(shape) for off, shape in where]


def _size(shape):
    n = 1
    for d in shape:
        n *= d
    return n


def _sample_step(x, target, mods, w_in_p, w_out, w_up_blk, w_down, sp):
    s, d = x.shape
    u = d // 2
    h = u // HEAD_DIM
    g = h // HEADS_PER_GROUP
    pw = w_in_p.shape[1]
    shift1, scale1, gate1, shift2, scale2, gate2 = mods

    a_f = -jnp.exp(sp["ssm_a_log_f"].reshape(-1))
    a_b = -jnp.exp(sp["ssm_a_log_b"].reshape(-1))
    pad_l = LANES - 2 * h
    a_row = jnp.pad(jnp.concatenate([a_f, a_b]), (0, pad_l)).reshape(1, LANES)
    bias_row = jnp.pad(jnp.concatenate([sp["ssm_dt_bias_f"].reshape(-1), sp["ssm_dt_bias_b"].reshape(-1)]),
                       (0, pad_l)).reshape(1, LANES)
    a_col = jnp.broadcast_to(jnp.stack([a_f, a_b])[:, :, None], (2, h, LANES))
    d_e = jnp.repeat(sp["ssm_d"].reshape(-1), HEAD_DIM).reshape(1, u)
    conv_w, conv_b = sp["ssm_conv_w"], sp["ssm_conv_b"].reshape(1, 2 * u)
    sc_conv_w = sp["sc_conv_w"]
    ssm_norm_w, sc_norm_w = sp["ssm_norm_w"].reshape(1, u), sp["sc_norm_w"].reshape(1, u)
    ln1_g, ln1_b = sp["ln1_g"].reshape(1, d), sp["ln1_b"].reshape(1, d)
    ln2_g, ln2_b = sp["ln2_g"].reshape(1, d), sp["ln2_b"].reshape(1, d)

    h1 = _modulate(x, scale1, shift1, name="modulate1")
    proj = _matmul(h1, w_in_p, name="mm_proj", tn=1280)
    xbc = _conv_silu_fwd(proj, conv_w, conv_b, s, u)
    dt, cum, dt_e, cum_e = _dt_prep(proj, bias_row, a_row, s, u, h)
    cum_t = jnp.stack([cum[:, :h].T, cum[:, h:2 * h].T])
    dt_t = jnp.stack([dt[:, :h].T, dt[:, h:2 * h].T])
    y2, states = _ssd_fwd(xbc, dt_e, cum_e, cum_t, s, h, g)
    y_ssm = _gate_norm_fwd(y2, xbc, proj, d_e, ssm_norm_w, s, u, g)
    y_sc = _shortconv_fwd(proj, sc_conv_w, sc_norm_w, s, u)
    ymix = jnp.concatenate([y_ssm, y_sc], axis=1)
    mix = _matmul(ymix, w_out, name="mm_mix")
    x1, h2 = _ln1_fwd(x, mix, gate1, ln1_g, ln1_b, scale2, shift2)
    up = _matmul(h2, w_up_blk, name="mm_up", b_blocks=N_CHIPS)
    ff = _relu2(up)
    f = _matmul(ff, w_down, name="mm_down")
    df, dr2, loss, dg2, db2, dgate2 = _ln2_loss_bwd(x1, f, target, gate2, ln2_g, ln2_b)

    gw_down = _matmul(ff, df, name="mm_gw_down", ta=True, out_dtype=BF16)
    dff = _matmul(df, w_down, name="mm_dff", tb=True)
    du = _relu2_bwd(dff, up)
    gw_up = _matmul(h2, du, name="mm_gw_up", ta=True, out_dtype=BF16, out_blocks=N_CHIPS)
    dh2 = _matmul(du, w_up_blk, name="mm_dh2", tb=True, b_blocks=N_CHIPS)
    dmix, dxa, dscale2, dshift2, dg1, db1, dgate1 = _ln1_bwd(dh2, dr2, x1, x, mix, scale2, gate1, ln1_g)
    gw_out = _matmul(ymix, dmix, name="mm_gw_out", ta=True, out_dtype=BF16)
    dymix = _matmul(dmix, w_out, name="mm_dymix", tb=True)
    dys, dz, dxs, dnw, dd_e = _gate_norm_bwd(dymix, y2, xbc, proj, d_e, ssm_norm_w, s, u, g)
    dx2, dbb, dcc, ddt_t, da = _ssd_bwd(xbc, dt_e, cum_e, cum_t, dt_t, a_col, dys, states, s, h, g)
    dxbc = jnp.concatenate([dx2[0] + dx2[1] + dxs, dbb[0] + dbb[1], dcc[0] + dcc[1]], axis=1)
    du_xbc, dcw, dcb = _conv_silu_bwd(proj, dxbc, conv_w, conv_b, s, u)
    ddt = jnp.pad(jnp.concatenate([ddt_t[0].T, ddt_t[1].T], axis=1), ((0, 0), (0, pad_l)))
    ddt_raw, dbias = _dt_bwd(ddt, proj, bias_row, s, u, h)
    dh_sc, db_sc, dc_sc, dscw, dscnw = _shortconv_bwd(dymix, proj, sc_conv_w, sc_norm_w, s, u)
    dproj = jnp.concatenate([dz, du_xbc, dh_sc, db_sc, dc_sc, ddt_raw,
                             jnp.zeros((s, pw - 6 * u - LANES), BF16)], axis=1)
    gw_in_p = _matmul(h1, dproj, name="mm_gw_in", ta=True, out_dtype=BF16, tn=1280)
    dh1 = _matmul(dproj, w_in_p, name="mm_dh1", tb=True, tk=1280)
    grad_x, dscale1, dshift1 = _dx_final(dxa, dh1, x, scale1)

    small = {
        "dmod": jnp.concatenate([dshift1, dscale1, dgate1, dshift2, dscale2, dgate2], axis=1),
        "ssm_conv_b": dcb,
        "ssm_dt_bias_f": dbias[0, :h], "ssm_dt_bias_b": dbias[0, h:2 * h],
        "ssm_a_log_f": jnp.sum(da[0], axis=1) * a_f, "ssm_a_log_b": jnp.sum(da[1], axis=1) * a_b,
        "ssm_d": jnp.sum(dd_e.reshape(h, HEAD_DIM), axis=1),
        "ssm_norm_w": dnw, "sc_norm_w": dscnw,
        "ln1_g": dg1, "ln1_b": db1, "ln2_g": dg2, "ln2_b": db2,
        "ssm_conv_w": dcw[:SSM_CONV], "sc_conv_w": dscw[:SC_CONV],
    }
    big = {"w_in_p": gw_in_p, "w_out": gw_out, "w_up": gw_up, "w_down": gw_down}
    return loss, grad_x, big, small


WEIGHTS = ['w_ada', 'b_ada', 'w_in', 'ssm_conv_w', 'ssm_conv_b', 'ssm_dt_bias_f', 'ssm_dt_bias_b', 'ssm_a_log_f',
           'ssm_a_log_b', 'ssm_d', 'ssm_norm_w', 'sc_conv_w', 'sc_norm_w', 'w_out', 'ln1_g', 'ln1_b', 'w_up', 'w_down',
           'ln2_g', 'ln2_b']
BIG = ('w_ada', 'w_in', 'w_out', 'w_up', 'w_down')
SMALL = tuple(n for n in WEIGHTS if n not in BIG)
SMALL_SHARDED = ('ssm_conv_w', 'sc_conv_w')


def _p_layout_width(u):
    return -(-(6 * u + LANES) // 512) * 512


def kernel(x, c, w_ada, b_ada, w_in, ssm_conv_w, ssm_conv_b, ssm_dt_bias_f, ssm_dt_bias_b, ssm_a_log_f, ssm_a_log_b, ssm_d, ssm_norm_w, sc_conv_w, sc_norm_w, w_out, ln1_g, ln1_b, w_up, w_down, ln2_g, ln2_b, loss_target, m_w_ada, m_b_ada, m_w_in, m_ssm_conv_w, m_ssm_conv_b, m_ssm_dt_bias_f, m_ssm_dt_bias_b, m_ssm_a_log_f, m_ssm_a_log_b, m_ssm_d, m_ssm_norm_w, m_sc_conv_w, m_sc_norm_w, m_w_out, m_ln1_g, m_ln1_b, m_w_up, m_w_down, m_ln2_g, m_ln2_b, v_w_ada, v_b_ada, v_w_in, v_ssm_conv_w, v_ssm_conv_b, v_ssm_dt_bias_f, v_ssm_dt_bias_b, v_ssm_a_log_f, v_ssm_a_log_b, v_ssm_d, v_ssm_norm_w, v_sc_conv_w, v_sc_norm_w, v_w_out, v_ln1_g, v_ln1_b, v_w_up, v_w_down, v_ln2_g, v_ln2_b):
    given = dict(locals())
    w = {n: given[n][0] for n in WEIGHTS}
    m = {n: given["m_" + n][0] for n in WEIGHTS}
    v = {n: given["v_" + n][0] for n in WEIGHTS}
    xs, tgt = x[0], loss_target[0]
    s, d = xs.shape
    u = d // 2
    h = u // HEAD_DIM
    nmod = N_MOD * d
    nmod_loc = nmod // N_CHIPS
    ax, ay, ac = lax.axis_index("x"), lax.axis_index("y"), lax.axis_index("c")
    chip = 2 * ax + ay
    me = 2 * chip + ac

    pay1, where1 = _pack([c[0], w["ssm_conv_w"], w["sc_conv_w"]])
    g1 = _allgather_small(pay1, name="ag_inputs").reshape(N_DEV, -1)
    per_dev = [_unpack(g1[k], where1) for k in range(N_DEV)]
    c_all = jnp.stack([p[0] for p in per_dev])
    ssm_conv_w_full = jnp.concatenate([per_dev[2 * k][1] for k in range(N_CHIPS)], axis=1)
    sc_conv_w_full = jnp.concatenate([per_dev[2 * k][2] for k in range(N_CHIPS)], axis=1)

    sc_all = _silu(c_all)
    sc16 = jnp.pad(sc_all, ((0, 16 - N_DEV), (0, 0)))
    b_loc = lax.dynamic_slice(w["b_ada"], (chip * nmod_loc,), (nmod_loc,))
    mod_loc = _matmul(sc16, w["w_ada"], name="mm_mod")[:N_DEV] + b_loc[None, :]
    pay2, where2 = _pack([mod_loc])
    g2 = _allgather_small(pay2, name="ag_mod").reshape(N_DEV, -1)
    mod_blocks = jnp.stack([_unpack(g2[2 * k], where2)[0] for k in range(N_CHIPS)])
    mod_mine = lax.dynamic_index_in_dim(mod_blocks, me, axis=1, keepdims=False).reshape(N_MOD, 1, d)
    mods = [mod_mine[k] for k in range(N_MOD)]

    din = w["w_in"].shape[1] * N_CHIPS
    g_in = _gather_weight(w["w_in"].astype(BF16), name="gather_w_in")
    w_in_full = jnp.concatenate([g_in[k] for k in range(N_CHIPS)], axis=1)
    pw = _p_layout_width(u)
    w_in_p = jnp.concatenate([w_in_full[:, :3 * u], w_in_full[:, 3 * u + 2 * h:], w_in_full[:, 3 * u:3 * u + 2 * h],
                              jnp.zeros((d, pw - din), BF16)], axis=1)
    w_out_full = _gather_weight(w["w_out"].astype(BF16), name="gather_w_out").reshape(d, d)
    w_up_blk = _gather_weight(w["w_up"].astype(BF16), name="gather_w_up")
    dff_ = w["w_up"].shape[1] * N_CHIPS
    w_down_full = _gather_weight(w["w_down"].astype(BF16), name="gather_w_down").reshape(dff_, d)

    sp = {n: w[n] for n in SMALL}
    sp["ssm_conv_w"], sp["sc_conv_w"] = ssm_conv_w_full, sc_conv_w_full
    loss_loc, grad_x, big, small = _sample_step(xs, tgt, mods, w_in_p, w_out_full, w_up_blk, w_down_full, sp)

    small_names = ["dmod"] + [n for n in SMALL if n != "b_ada"]
    pay3, where3 = _pack([loss_loc] + [small[n] for n in small_names])
    g3 = _allgather_small(pay3, name="ag_small_grads")
    tot = _unpack(_sum_slots(g3.reshape(N_DEV, -1, LANES), name="sum_small_grads"), where3)
    loss = tot[0].reshape(())
    gsum = dict(zip(small_names, tot[1:]))
    dmod_all = jnp.stack([_unpack(g3.reshape(N_DEV, -1)[k], where3)[1].reshape(-1) for k in range(N_DEV)])

    grads = {}
    grads["b_ada"] = gsum["dmod"].reshape(-1)
    for n in SMALL:
        if n in SMALL_SHARDED:
            loc = w[n].shape[1]
            grads[n] = lax.dynamic_slice_in_dim(gsum[n], chip * loc, loc, axis=1)
        elif n != "b_ada":
            grads[n] = gsum[n].reshape(w[n].shape)

    gp = big["w_in_p"]
    g_in_full = jnp.concatenate([gp[:, :3 * u], gp[:, 6 * u:6 * u + 2 * h], gp[:, 3 * u:6 * u]], axis=1)
    g_in_blk = jnp.transpose(g_in_full.reshape(d, N_CHIPS, din // N_CHIPS), (1, 0, 2))
    grads["w_in"] = _reduce_to_owner(g_in_blk, name="rs_w_in")
    grads["w_out"] = _reduce_to_owner(big["w_out"].reshape(N_CHIPS, d // N_CHIPS, d), name="rs_w_out")
    grads["w_up"] = _reduce_to_owner(big["w_up"], name="rs_w_up")
    grads["w_down"] = _reduce_to_owner(big["w_down"].reshape(N_CHIPS, dff_ // N_CHIPS, d), name="rs_w_down")

    delta, new_m, new_v = {}, {}, {}
    dm_loc = lax.dynamic_slice_in_dim(dmod_all, chip * nmod_loc, nmod_loc, axis=1)
    grads["w_ada"], delta["w_ada"], new_m["w_ada"], new_v["w_ada"] = _adam_outer(
        w["w_ada"], sc16.T, jnp.pad(dm_loc, ((0, 16 - N_DEV), (0, 0))), m["w_ada"], v["w_ada"], name="adam_w_ada")
    for n in ("w_in", "w_out", "w_up", "w_down"):
        delta[n], new_m[n], new_v[n] = _adam(w[n], grads[n], m[n], v[n], name="adam_" + n)
    pw_, where_s = _pack([w[n] for n in SMALL])
    pg_, _ = _pack([grads[n] for n in SMALL])
    pm_, _ = _pack([m[n] for n in SMALL])
    pv_, _ = _pack([v[n] for n in SMALL])
    sd, sm, sv = _adam(pw_, pg_, pm_, pv_, name="adam_small")
    for n, a, b_, c_ in zip(SMALL, _unpack(sd, where_s), _unpack(sm, where_s), _unpack(sv, where_s)):
        delta[n], new_m[n], new_v[n] = a, b_, c_

    def lead(t):
        return t[None]

    return (loss, grad_x[None], *[lead(grads[n].reshape(w[n].shape)) for n in WEIGHTS],
            *[lead(delta[n]) for n in WEIGHTS], *[lead(new_m[n]) for n in WEIGHTS], *[lead(new_v[n]) for n in WEIGHTS])
```

```python
import functools

import jax
import jax.numpy as jnp
from jax import lax
from jax.experimental import pallas as pl
from jax.experimental.pallas import tpu as pltpu

F32 = jnp.float32
BF16 = jnp.bfloat16

CHUNK = 128
HEAD_DIM = 64
STATE = 128
HEADS_PER_GROUP = 4
SC_GROUP_WIDTH = 128
SSM_CONV = 5
SC_CONV = 3
N_MOD = 6
DEEPNORM_ALPHA = 2.0 ** 0.25
LN_EPS = 1e-5
RMS_EPS = 1e-5
ADAM_LR = 0.001
ADAM_B1 = 0.9
ADAM_B2 = 0.999
ADAM_EPS = 1e-08
ADAM_WD = 0.01
ADAM_STEP = 10
N_CHIPS = 4
N_DEV = 8
LANES = 128
SUBLANES = 8
HALO = 8
VMEM_LIMIT = 56 * 1024 * 1024
MESH = pl.DeviceIdType.MESH


def _params(sem=None):
    return pltpu.CompilerParams(dimension_semantics=sem, vmem_limit_bytes=VMEM_LIMIT)


def _pick(n, target, mult=LANES):
    best = None
    t = mult
    while t <= min(n, target):
        if n % t == 0:
            best = t
        t += mult
    return best if best is not None else n


ROW_TILE_BYTES = 1 << 20


def _tile_rows(s, width, mult=SUBLANES):
    return _pick(s, max(mult, ROW_TILE_BYTES // (4 * width)), mult)


def _sigmoid(v):
    return 1.0 / (1.0 + jnp.exp(-v))


def _silu(v):
    return v * _sigmoid(v)


def _dsilu(v):
    s = _sigmoid(v)
    return s * (1.0 + v * (1.0 - s))


def _softplus(v):
    e = jnp.exp(-jnp.abs(v))
    return jnp.maximum(v, 0.0) + jnp.where(e < 1e-4, e - 0.5 * e * e, jnp.log(1.0 + e))


def _dot(a, b):
    return jnp.dot(a, b, preferred_element_type=F32)


def _dot_nt(a, b):
    return lax.dot_general(a, b, (((1,), (1,)), ((), ())), preferred_element_type=F32)


def _dot_tn(a, b):
    return lax.dot_general(a, b, (((0,), (0,)), ((), ())), preferred_element_type=F32)


def _split3(v):
    hi = v.astype(BF16)
    r1 = v - hi.astype(F32)
    mid = r1.astype(BF16)
    lo = (r1 - mid.astype(F32)).astype(BF16)
    return hi, mid, lo


def _dot3_r(v, onehot):
    hi, mid, lo = _split3(v)
    return _dot(hi, onehot) + _dot(mid, onehot) + _dot(lo, onehot)


def _dot3_l(onehot, v):
    hi, mid, lo = _split3(v)
    return _dot(onehot, hi) + _dot(onehot, mid) + _dot(onehot, lo)


def _matmul(a, b, *, name, ta=False, tb=False, out_dtype=F32, b_blocks=1, out_blocks=1,
            tm=1024, tn=1024, tk=1024):
    if ta:
        K, M = a.shape
    else:
        M, K = a.shape
    if b_blocks > 1:
        nb, r_, c_ = b.shape
        if tb:
            N, K2 = r_, c_ * nb
        else:
            K2, N = r_, c_ * nb
    else:
        if tb:
            N, K2 = b.shape
        else:
            K2, N = b.shape
    assert K == K2, (a.shape, b.shape, ta, tb)
    tm = _pick(M, tm)
    n_unit = N // b_blocks if (b_blocks > 1 and not tb) else N
    n_unit = min(n_unit, N // out_blocks)
    tn = _pick(n_unit, tn)
    k_unit = K // b_blocks if (b_blocks > 1 and tb) else K
    tk = _pick(k_unit, tk)
    gm, gn, gk = M // tm, N // tn, K // tk

    if ta:
        a_spec = pl.BlockSpec((tk, tm), lambda i, j, k: (k, i))
    else:
        a_spec = pl.BlockSpec((tm, tk), lambda i, j, k: (i, k))
    if b_blocks > 1 and not tb:
        per = (N // b_blocks) // tn
        b_spec = pl.BlockSpec((None, tk, tn), lambda i, j, k: (j // per, k, j % per))
    elif b_blocks > 1 and tb:
        per = (K // b_blocks) // tk
        b_spec = pl.BlockSpec((None, tn, tk), lambda i, j, k: (k // per, j, k % per))
    elif tb:
        b_spec = pl.BlockSpec((tn, tk), lambda i, j, k: (j, k))
    else:
        b_spec = pl.BlockSpec((tk, tn), lambda i, j, k: (k, j))
    if out_blocks > 1:
        per_o = (N // out_blocks) // tn
        o_spec = pl.BlockSpec((None, tm, tn), lambda i, j, k: (j // per_o, i, j % per_o))
        o_shape = jax.ShapeDtypeStruct((out_blocks, M, N // out_blocks), out_dtype)
    else:
        o_spec = pl.BlockSpec((tm, tn), lambda i, j, k: (i, j))
        o_shape = jax.ShapeDtypeStruct((M, N), out_dtype)

    def body(a_ref, b_ref, o_ref, acc):
        k = pl.program_id(2)
        av = a_ref[...].astype(BF16)
        bv = b_ref[...].astype(BF16)
        if ta and tb:
            raise NotImplementedError
        if ta:
            p = _dot_tn(av, bv)
        elif tb:
            p = _dot_nt(av, bv)
        else:
            p = _dot(av, bv)

        @pl.when(k == 0)
        def _():
            acc[...] = p

        @pl.when(k > 0)
        def _():
            acc[...] += p

        @pl.when(k == gk - 1)
        def _():
            o_ref[...] = acc[...].astype(out_dtype)

    return pl.pallas_call(
        body, name=name, grid=(gm, gn, gk), in_specs=[a_spec, b_spec], out_specs=o_spec,
        out_shape=o_shape, scratch_shapes=[pltpu.VMEM((tm, tn), F32)],
        compiler_params=_params(("parallel", "parallel", "arbitrary")),
    )(a, b)


def _row(tr, w, blk=0):
    return pl.BlockSpec((tr, w), lambda i: (i, blk))


def _full(shape):
    nd = len(shape)
    return pl.BlockSpec(shape, lambda i: (0,) * nd)


def _halo_specs(s, tr, w, blk=0):
    per = tr // HALO
    last = s // HALO - 1
    return [
        pl.BlockSpec((HALO, w), lambda i: (jnp.maximum(i * per - 1, 0), blk)),
        pl.BlockSpec((tr, w), lambda i: (i, blk)),
        pl.BlockSpec((HALO, w), lambda i: (jnp.minimum((i + 1) * per, last), blk)),
    ]


def _ext(prev_ref, cur_ref, next_ref, s, tr):
    i = pl.program_id(0)
    e = jnp.concatenate([prev_ref[...].astype(F32), cur_ref[...].astype(F32), next_ref[...].astype(F32)], axis=0)
    rid = i * tr - HALO + lax.broadcasted_iota(jnp.int32, e.shape, 0)
    return jnp.where((rid >= 0) & (rid < s), e, 0.0)


def _valid_rows(shape, s, tr):
    i = pl.program_id(0)
    rid = i * tr - HALO + lax.broadcasted_iota(jnp.int32, shape, 0)
    return (rid >= 0) & (rid < s)


def _shift(e, k):
    if k == 0:
        return e
    n = e.shape[0]
    return pltpu.roll(e, (n - k) % n, 0)


def _acc_rows(ref, v):
    s = jnp.sum(v, axis=0, keepdims=True)

    @pl.when(pl.program_id(0) == 0)
    def _():
        ref[...] = s

    @pl.when(pl.program_id(0) > 0)
    def _():
        ref[...] += s


def _rowcall(body, name, s, tr, in_specs, out_specs, out_shape, args):
    return pl.pallas_call(
        body, name=name, grid=(s // tr,), in_specs=in_specs, out_specs=out_specs, out_shape=out_shape,
        compiler_params=_params(("arbitrary",)),
    )(*args)


def _modulate(x, scale, shift, *, name):
    s, d = x.shape
    tr = _tile_rows(s, d)

    def body(x_ref, sc_ref, sh_ref, o_ref):
        o_ref[...] = (x_ref[...] * (1.0 + sc_ref[...]) + sh_ref[...]).astype(BF16)

    return _rowcall(body, name, s, tr, [_row(tr, d), _full((1, d)), _full((1, d))], _row(tr, d),
                    jax.ShapeDtypeStruct((s, d), BF16), (x, scale, shift))


def _ln_stats(r):
    mu = jnp.mean(r, axis=-1, keepdims=True)
    xc = r - mu
    var = jnp.mean(xc * xc, axis=-1, keepdims=True)
    rstd = lax.rsqrt(var + LN_EPS)
    return xc * rstd, rstd


def _ln1_fwd(x, mix, gate, g, b, scale2, shift2):
    s, d = x.shape
    tr = _tile_rows(s, d)

    def body(x_ref, m_ref, gt_ref, g_ref, b_ref, sc_ref, sh_ref, x1_ref, h2_ref):
        r = DEEPNORM_ALPHA * x_ref[...] + (1.0 + gt_ref[...]) * m_ref[...]
        xh, _ = _ln_stats(r)
        x1 = xh * g_ref[...] + b_ref[...]
        x1_ref[...] = x1
        h2_ref[...] = (x1 * (1.0 + sc_ref[...]) + sh_ref[...]).astype(BF16)

    v = _full((1, d))
    return _rowcall(body, "ln1_fwd", s, tr, [_row(tr, d), _row(tr, d), v, v, v, v, v],
                    [_row(tr, d), _row(tr, d)],
                    [jax.ShapeDtypeStruct((s, d), F32), jax.ShapeDtypeStruct((s, d), BF16)],
                    (x, mix, gate, g, b, scale2, shift2))


def _relu2(u):
    s, f = u.shape
    tr = _tile_rows(s, f)

    def body(u_ref, o_ref):
        r = jnp.maximum(u_ref[...], 0.0)
        o_ref[...] = (r * r).astype(BF16)

    return _rowcall(body, "relu2", s, tr, [_row(tr, f)], _row(tr, f), jax.ShapeDtypeStruct((s, f), BF16), (u,))


def _ln2_loss_bwd(x1, f, target, gate, g, b):
    s, d = x1.shape
    tr = _tile_rows(s, d)

    def body(x1_ref, f_ref, t_ref, gt_ref, g_ref, b_ref, df_ref, dr_ref, loss_ref, dg_ref, db_ref, dgt_ref):
        fv = f_ref[...]
        r = DEEPNORM_ALPHA * x1_ref[...] + (1.0 + gt_ref[...]) * fv
        xh, rstd = _ln_stats(r)
        y = xh * g_ref[...] + b_ref[...]
        err = y - t_ref[...]
        _acc_rows(loss_ref, 0.5 * jnp.mean(err * err, axis=-1, keepdims=True))
        dy = err * (1.0 / d)
        _acc_rows(dg_ref, dy * xh)
        _acc_rows(db_ref, dy)
        dxh = dy * g_ref[...]
        dr = rstd * (dxh - jnp.mean(dxh, axis=-1, keepdims=True) - xh * jnp.mean(dxh * xh, axis=-1, keepdims=True))
        dr_ref[...] = dr
        df_ref[...] = ((1.0 + gt_ref[...]) * dr).astype(BF16)
        _acc_rows(dgt_ref, dr * fv)

    v = _full((1, d))
    one = _full((1, 1))
    return _rowcall(body, "ln2_loss_bwd", s, tr, [_row(tr, d), _row(tr, d), _row(tr, d), v, v, v],
                    [_row(tr, d), _row(tr, d), one, v, v, v],
                    [jax.ShapeDtypeStruct((s, d), BF16), jax.ShapeDtypeStruct((s, d), F32),
                     jax.ShapeDtypeStruct((1, 1), F32)] + [jax.ShapeDtypeStruct((1, d), F32)] * 3,
                    (x1, f, target, gate, g, b))


def _relu2_bwd(dff, u):
    s, f = u.shape
    tr = _tile_rows(s, f)

    def body(d_ref, u_ref, o_ref):
        o_ref[...] = (d_ref[...] * 2.0 * jnp.maximum(u_ref[...], 0.0)).astype(BF16)

    return _rowcall(body, "relu2_bwd", s, tr, [_row(tr, f), _row(tr, f)], _row(tr, f),
                    jax.ShapeDtypeStruct((s, f), BF16), (dff, u))


def _ln1_bwd(dh2, dr2, x1, x, mix, scale2, gate1, g1):
    s, d = x.shape
    tr = _tile_rows(s, d)

    def body(dh_ref, dr2_ref, x1_ref, x_ref, m_ref, sc_ref, gt_ref, g_ref,
             dm_ref, dxa_ref, dsc_ref, dsh_ref, dg_ref, db_ref, dgt_ref):
        dh = dh_ref[...]
        _acc_rows(dsc_ref, dh * x1_ref[...])
        _acc_rows(dsh_ref, dh)
        dy = dh * (1.0 + sc_ref[...]) + DEEPNORM_ALPHA * dr2_ref[...]
        mv = m_ref[...]
        r = DEEPNORM_ALPHA * x_ref[...] + (1.0 + gt_ref[...]) * mv
        xh, rstd = _ln_stats(r)
        _acc_rows(dg_ref, dy * xh)
        _acc_rows(db_ref, dy)
        dxh = dy * g_ref[...]
        dr = rstd * (dxh - jnp.mean(dxh, axis=-1, keepdims=True) - xh * jnp.mean(dxh * xh, axis=-1, keepdims=True))
        dm_ref[...] = ((1.0 + gt_ref[...]) * dr).astype(BF16)
        dxa_ref[...] = DEEPNORM_ALPHA * dr
        _acc_rows(dgt_ref, dr * mv)

    v = _full((1, d))
    return _rowcall(body, "ln1_bwd", s, tr, [_row(tr, d)] * 5 + [v, v, v],
                    [_row(tr, d), _row(tr, d), v, v, v, v, v],
                    [jax.ShapeDtypeStruct((s, d), BF16), jax.ShapeDtypeStruct((s, d), F32)]
                    + [jax.ShapeDtypeStruct((1, d), F32)] * 5,
                    (dh2, dr2, x1, x, mix, scale2, gate1, g1))


def _dx_final(dxa, dh1, x, scale1):
    s, d = x.shape
    tr = _tile_rows(s, d)

    def body(a_ref, dh_ref, x_ref, sc_ref, o_ref, dsc_ref, dsh_ref):
        dh = dh_ref[...]
        o_ref[...] = a_ref[...] + dh * (1.0 + sc_ref[...])
        _acc_rows(dsc_ref, dh * x_ref[...])
        _acc_rows(dsh_ref, dh)

    v = _full((1, d))
    return _rowcall(body, "dx_final", s, tr, [_row(tr, d)] * 3 + [v], [_row(tr, d), v, v],
                    [jax.ShapeDtypeStruct((s, d), F32)] + [jax.ShapeDtypeStruct((1, d), F32)] * 2,
                    (dxa, dh1, x, scale1))


def _conv_silu_fwd(proj, conv_w, conv_b, s, u):
    tr = _tile_rows(s, u)
    w = 2 * u
    half = SSM_CONV // 2

    def body(p0, c0, n0, p1, c1, n1, w_ref, b_ref, o_ref):
        for blk, (pr, cr, nr) in enumerate(((p0, c0, n0), (p1, c1, n1))):
            e = _ext(pr, cr, nr, s, tr)
            wv = w_ref[:, blk * u:(blk + 1) * u]
            acc = jnp.zeros_like(e)
            for k in range(SSM_CONV):
                acc = acc + _shift(e, k - half) * wv[k:k + 1, :]
            pre = acc[HALO:HALO + tr] + b_ref[:, blk * u:(blk + 1) * u]
            o_ref[:, blk * u:(blk + 1) * u] = _silu(pre)

    in_specs = _halo_specs(s, tr, u, 1) + _halo_specs(s, tr, u, 2) + [_full((SSM_CONV, w)), _full((1, w))]
    return _rowcall(body, "conv_silu_fwd", s, tr, in_specs, _row(tr, w), jax.ShapeDtypeStruct((s, w), F32),
                    (proj,) * 6 + (conv_w, conv_b))


def _conv_silu_bwd(proj, dxbc, conv_w, conv_b, s, u):
    tr = _tile_rows(s, u)
    w = 2 * u
    half = SSM_CONV // 2

    def body(p0, c0, n0, p1, c1, n1, dp0, dc0, dn0, dp1, dc1, dn1, w_ref, b_ref, du_ref, dw_ref, db_ref):
        for blk, (ur, dr) in enumerate((((p0, c0, n0), (dp0, dc0, dn0)), ((p1, c1, n1), (dp1, dc1, dn1)))):
            e = _ext(*ur, s, tr)
            de = _ext(*dr, s, tr)
            wv = w_ref[:, blk * u:(blk + 1) * u]
            acc = jnp.zeros_like(e)
            for k in range(SSM_CONV):
                acc = acc + _shift(e, k - half) * wv[k:k + 1, :]
            pre = acc + b_ref[:, blk * u:(blk + 1) * u]
            dpre = jnp.where(_valid_rows(e.shape, s, tr), de * _dsilu(pre), 0.0)
            du = jnp.zeros_like(e)
            rows = []
            for k in range(SSM_CONV):
                du = du + _shift(dpre, half - k) * wv[k:k + 1, :]
                rows.append(jnp.sum((_shift(e, k - half) * dpre)[HALO:HALO + tr], axis=0, keepdims=True))
            du_ref[:, blk * u:(blk + 1) * u] = du[HALO:HALO + tr].astype(BF16)
            dwv = jnp.concatenate(rows + [jnp.zeros((SUBLANES - SSM_CONV, u), F32)], axis=0)
            dbv = jnp.sum(dpre[HALO:HALO + tr], axis=0, keepdims=True)
            first = pl.program_id(0) == 0

            @pl.when(first)
            def _():
                dw_ref[:, blk * u:(blk + 1) * u] = dwv
                db_ref[:, blk * u:(blk + 1) * u] = dbv

            @pl.when(jnp.logical_not(first))
            def _():
                dw_ref[:, blk * u:(blk + 1) * u] += dwv
                db_ref[:, blk * u:(blk + 1) * u] += dbv

    in_specs = (_halo_specs(s, tr, u, 1) + _halo_specs(s, tr, u, 2) + _halo_specs(s, tr, u, 0)
                + _halo_specs(s, tr, u, 1) + [_full((SSM_CONV, w)), _full((1, w))])
    return _rowcall(body, "conv_silu_bwd", s, tr, in_specs,
                    [_row(tr, w), _full((SUBLANES, w)), _full((1, w))],
                    [jax.ShapeDtypeStruct((s, w), BF16), jax.ShapeDtypeStruct((SUBLANES, w), F32),
                     jax.ShapeDtypeStruct((1, w), F32)],
                    (proj,) * 6 + (dxbc,) * 6 + (conv_w, conv_b))


def _expanders(h):
    col64 = jnp.arange(2 * h * HEAD_DIM) // HEAD_DIM
    col128 = jnp.arange(2 * h * LANES) // LANES
    row = jnp.arange(LANES)[:, None]
    return (row == col64[None, :]).astype(BF16), (row == col128[None, :]).astype(BF16)


def _dt_prep(proj, bias_row, a_row, s, u, h):
    q = CHUNK
    e64, e128 = _expanders(h)
    ds = h * HEAD_DIM
    dtblk = (6 * u) // LANES

    def body(raw_ref, b_ref, a_ref, e64_ref, e128_ref, dt_ref, cum_ref, dte_ref, cume_ref):
        lane = lax.broadcasted_iota(jnp.int32, (q, LANES), 1)
        dt = jnp.where(lane < 2 * h, _softplus(raw_ref[...] + b_ref[...]), 0.0)
        da = dt * a_ref[...]
        ii = lax.broadcasted_iota(jnp.int32, (q, q), 0)
        kk = lax.broadcasted_iota(jnp.int32, (q, q), 1)
        lower = (kk <= ii).astype(F32).astype(BF16)
        upper = (kk >= ii).astype(F32).astype(BF16)
        cum = jnp.where(lane < h, _dot3_l(lower, da), _dot3_l(upper, da))
        dt_ref[...] = dt
        cum_ref[...] = cum
        dte = _dot3_r(dt, e64_ref[...])
        cume = _dot3_r(cum, e128_ref[...])
        dte_ref[0] = dte[:, :ds]
        dte_ref[1] = dte[:, ds:]
        cume_ref[0] = cume[:, :h * LANES]
        cume_ref[1] = cume[:, h * LANES:]

    in_specs = [pl.BlockSpec((q, LANES), lambda i: (i, dtblk)), _full((1, LANES)), _full((1, LANES)),
                _full(e64.shape), _full(e128.shape)]
    out_specs = [_row(q, LANES), _row(q, LANES),
                 pl.BlockSpec((2, q, ds), lambda i: (0, i, 0)), pl.BlockSpec((2, q, h * LANES), lambda i: (0, i, 0))]
    out_shape = [jax.ShapeDtypeStruct((s, LANES), F32), jax.ShapeDtypeStruct((s, LANES), F32),
                 jax.ShapeDtypeStruct((2, s, ds), F32), jax.ShapeDtypeStruct((2, s, h * LANES), F32)]
    return _rowcall(body, "dt_prep", s, q, in_specs, out_specs, out_shape, (proj, bias_row, a_row, e64, e128))


def _ssd_specs(s, h, g):
    q = CHUNK
    nc = s // q
    ds = h * HEAD_DIM
    nb = g * STATE
    return q, nc, ds, nb


def _ssd_fwd(xbc, dt_e, cum_e, cum_t, s, h, g):
    q, nc, ds, nb = _ssd_specs(s, h, g)
    npair = h // 2

    def cidx(d, i):
        return jnp.where(d == 0, i, nc - 1 - i)

    def body(x_ref, b_ref, c_ref, dt_ref, cum_ref, cumt_ref, y_ref, sp_ref, st):
        d = pl.program_id(0)
        i = pl.program_id(1)

        @pl.when(i == 0)
        def _():
            st[...] = jnp.zeros_like(st)

        rev = d == 1
        ii = lax.broadcasted_iota(jnp.int32, (q, q), 0)
        jj = lax.broadcasted_iota(jnp.int32, (q, q), 1)
        sgn = jnp.where(rev, -1, 1)
        mask = (jj - ii) * sgn <= 0
        left = lax.broadcasted_iota(jnp.int32, (q, LANES), 1) < HEAD_DIM

        def group(gi, carry):
            goff = pl.multiple_of(gi * STATE, STATE)
            cg = c_ref[:, pl.ds(goff, STATE)].astype(BF16)
            bg = b_ref[:, pl.ds(goff, STATE)].astype(BF16)
            gm = _dot_nt(cg, bg)
            for p in range(HEADS_PER_GROUP // 2):
                pr = gi * (HEADS_PER_GROUP // 2) + p
                off = pl.multiple_of(pr * LANES, LANES)
                xd = x_ref[:, pl.ds(off, LANES)] * dt_ref[:, pl.ds(off, LANES)]
                ms = []
                cols = []
                for hl in range(2):
                    hh = 2 * pr + hl
                    col = cum_ref[:, pl.ds(pl.multiple_of(hh * LANES, LANES), LANES)]
                    row = cumt_ref[pl.ds(hh, 1), :]
                    lm = jnp.where(mask, jnp.exp(jnp.minimum(col - row, 0.0)), 0.0)
                    ms.append((gm * lm).astype(BF16))
                    cols.append(col)
                y = _dot(ms[0], jnp.where(left, xd, 0.0).astype(BF16)) + _dot(ms[1], jnp.where(left, 0.0, xd).astype(BF16))
                ce = jnp.where(left, cols[0], cols[1])
                sprev = st[pr]
                sp_ref[pr] = sprev
                y = y + jnp.exp(ce) * _dot(cg, sprev.astype(BF16))
                y_ref[:, pl.ds(off, LANES)] = y
                tot = jnp.where(rev, ce[0:1, :], ce[q - 1:q, :])
                v = (xd * jnp.exp(tot - ce)).astype(BF16)
                st[pr] = jnp.exp(tot) * sprev + _dot_tn(bg, v)
            return carry

        lax.fori_loop(0, g, group, 0)

    in_specs = [
        pl.BlockSpec((q, ds), lambda d, i: (cidx(d, i), 0)),
        pl.BlockSpec((q, nb), lambda d, i: (cidx(d, i), ds // nb)),
        pl.BlockSpec((q, nb), lambda d, i: (cidx(d, i), ds // nb + 1)),
        pl.BlockSpec((None, q, ds), lambda d, i: (d, cidx(d, i), 0)),
        pl.BlockSpec((None, q, h * LANES), lambda d, i: (d, cidx(d, i), 0)),
        pl.BlockSpec((None, h, q), lambda d, i: (d, 0, cidx(d, i))),
    ]
    out_specs = [
        pl.BlockSpec((None, q, ds), lambda d, i: (d, cidx(d, i), 0)),
        pl.BlockSpec((None, None, npair, STATE, LANES), lambda d, i: (d, cidx(d, i), 0, 0, 0)),
    ]
    out_shape = [jax.ShapeDtypeStruct((2, s, ds), F32), jax.ShapeDtypeStruct((2, nc, npair, STATE, LANES), F32)]
    return pl.pallas_call(
        body, name="ssd_fwd", grid=(2, nc), in_specs=in_specs, out_specs=out_specs, out_shape=out_shape,
        scratch_shapes=[pltpu.VMEM((npair, STATE, LANES), F32)],
        compiler_params=_params(("arbitrary", "arbitrary")),
    )(xbc, xbc, xbc, dt_e, cum_e, cum_t)


def _ssd_bwd(xbc, dt_e, cum_e, cum_t, dt_t, a_col, dy, sp, s, h, g):
    q, nc, ds, nb = _ssd_specs(s, h, g)
    npair = h // 2

    def cidx(d, i):
        return jnp.where(d == 0, nc - 1 - i, i)

    def body(x_ref, b_ref, c_ref, dt_ref, cum_ref, cumt_ref, dtt_ref, a_ref, dy_ref, sp_ref,
             dx_ref, db_ref, dc_ref, ddt_ref, da_ref, dst, rowp):
        d = pl.program_id(0)
        i = pl.program_id(1)

        @pl.when(i == 0)
        def _():
            dst[...] = jnp.zeros_like(dst)
            da_ref[...] = jnp.zeros_like(da_ref)

        rev = d == 1
        ii = lax.broadcasted_iota(jnp.int32, (q, q), 0)
        jj = lax.broadcasted_iota(jnp.int32, (q, q), 1)
        sgn = jnp.where(rev, -1, 1)
        mask = (jj - ii) * sgn <= 0
        lane = lax.broadcasted_iota(jnp.int32, (q, LANES), 1)
        left = lane < HEAD_DIM
        rowp[...] = jnp.zeros_like(rowp)

        def group(gi, carry):
            acc_dcum, acc_tot, acc_dxx = carry
            goff = pl.multiple_of(gi * STATE, STATE)
            cg = c_ref[:, pl.ds(goff, STATE)].astype(BF16)
            bg = b_ref[:, pl.ds(goff, STATE)].astype(BF16)
            gm = _dot_nt(cg, bg)
            dgm = jnp.zeros((q, q), F32)
            dcg = jnp.zeros((q, STATE), F32)
            dbg = jnp.zeros((q, STATE), F32)
            for p in range(HEADS_PER_GROUP // 2):
                pr = gi * (HEADS_PER_GROUP // 2) + p
                off = pl.multiple_of(pr * LANES, LANES)
                xv = x_ref[:, pl.ds(off, LANES)]
                dte = dt_ref[:, pl.ds(off, LANES)]
                xd = xv * dte
                xdb = xd.astype(BF16)
                dyv = dy_ref[:, pl.ds(off, LANES)]
                sprev = sp_ref[pr]
                sprevb = sprev.astype(BF16)
                dsn = dst[pr]
                dsnb = dsn.astype(BF16)
                cols = [cum_ref[:, pl.ds(pl.multiple_of((2 * pr + hl) * LANES, LANES), LANES)] for hl in range(2)]
                ce = jnp.where(left, cols[0], cols[1])
                tot = jnp.where(rev, ce[0:1, :], ce[q - 1:q, :])
                et = jnp.exp(tot)
                r = jnp.exp(tot - ce)
                e = jnp.exp(ce)
                yoff = e * _dot(cg, sprevb)
                dz = (e * dyv).astype(BF16)
                dcg = dcg + _dot_nt(dz, sprevb)
                dsprev = _dot_tn(cg, dz) + et * dsn
                f1 = dyv * yoff
                v = (xd * r).astype(BF16)
                dbg = dbg + _dot_nt(v, dsnb)
                dv = _dot(bg, dsnb)
                dxd = dv * r
                tt = dv * xd * r
                wt = dsn * sprev * et
                for hl in range(2):
                    hh = 2 * pr + hl
                    hm = left if hl == 0 else jnp.logical_not(left)
                    row = cumt_ref[pl.ds(hh, 1), :]
                    lm = jnp.where(mask, jnp.exp(jnp.minimum(cols[hl] - row, 0.0)), 0.0)
                    mf = gm * lm
                    dym = jnp.where(hm, dyv, 0.0).astype(BF16)
                    dm = _dot_nt(dym, xdb)
                    dxd = dxd + _dot_tn(mf.astype(BF16), dym)
                    dgm = dgm + dm * lm
                    em = dm * mf
                    rowp[pl.ds(hh, 1), :] = rowp[pl.ds(hh, 1), :] - jnp.sum(em, axis=0, keepdims=True)
                    colq = (jnp.sum(em, axis=1, keepdims=True)
                            + jnp.sum(jnp.where(hm, f1 - tt, 0.0), axis=1, keepdims=True))
                    acc_dcum = jnp.where(lane == hh, colq, acc_dcum)
                    totq = jnp.sum(jnp.sum(jnp.where(hm, tt + wt, 0.0), axis=1, keepdims=True), axis=0, keepdims=True)
                    acc_tot = jnp.where(lane == hh, totq, acc_tot)
                dxx = dxd * xv
                for hl in range(2):
                    hh = 2 * pr + hl
                    hm = left if hl == 0 else jnp.logical_not(left)
                    acc_dxx = jnp.where(lane == hh, jnp.sum(jnp.where(hm, dxx, 0.0), axis=1, keepdims=True), acc_dxx)
                dx_ref[:, pl.ds(off, LANES)] = dxd * dte
                dst[pr] = dsprev
            dgb = dgm.astype(BF16)
            dc_ref[:, pl.ds(goff, STATE)] = dcg + _dot(dgb, bg)
            db_ref[:, pl.ds(goff, STATE)] = dbg + _dot_tn(dgb, cg)
            return acc_dcum, acc_tot, acc_dxx

        zero = jnp.zeros((q, LANES), F32)
        acc_dcum, acc_tot, acc_dxx = lax.fori_loop(0, g, group, (zero, zero, zero))
        dcum_t = rowp[...] + jnp.transpose(acc_dcum)[:h]
        rmat = ((ii - jj) * sgn >= 0).astype(F32).astype(BF16)
        da_t = _dot3_r(dcum_t, rmat) + jnp.transpose(acc_tot)[:h]
        ddt_ref[...] = da_t * a_ref[...] + jnp.transpose(acc_dxx)[:h]
        da_ref[...] += da_t * dtt_ref[...]

    in_specs = [
        pl.BlockSpec((q, ds), lambda d, i: (cidx(d, i), 0)),
        pl.BlockSpec((q, nb), lambda d, i: (cidx(d, i), ds // nb)),
        pl.BlockSpec((q, nb), lambda d, i: (cidx(d, i), ds // nb + 1)),
        pl.BlockSpec((None, q, ds), lambda d, i: (d, cidx(d, i), 0)),
        pl.BlockSpec((None, q, h * LANES), lambda d, i: (d, cidx(d, i), 0)),
        pl.BlockSpec((None, h, q), lambda d, i: (d, 0, cidx(d, i))),
        pl.BlockSpec((None, h, q), lambda d, i: (d, 0, cidx(d, i))),
        pl.BlockSpec((None, h, LANES), lambda d, i: (d, 0, 0)),
        pl.BlockSpec((q, ds), lambda d, i: (cidx(d, i), 0)),
        pl.BlockSpec((None, None, npair, STATE, LANES), lambda d, i: (d, cidx(d, i), 0, 0, 0)),
    ]
    out_specs = [
        pl.BlockSpec((None, q, ds), lambda d, i: (d, cidx(d, i), 0)),
        pl.BlockSpec((None, q, nb), lambda d, i: (d, cidx(d, i), 0)),
        pl.BlockSpec((None, q, nb), lambda d, i: (d, cidx(d, i), 0)),
        pl.BlockSpec((None, h, q), lambda d, i: (d, 0, cidx(d, i))),
        pl.BlockSpec((None, h, LANES), lambda d, i: (d, 0, 0)),
    ]
    out_shape = [jax.ShapeDtypeStruct((2, s, ds), F32), jax.ShapeDtypeStruct((2, s, nb), F32),
                 jax.ShapeDtypeStruct((2, s, nb), F32), jax.ShapeDtypeStruct((2, h, s), F32),
                 jax.ShapeDtypeStruct((2, h, LANES), F32)]
    return pl.pallas_call(
        body, name="ssd_bwd", grid=(2, nc), in_specs=in_specs, out_specs=out_specs, out_shape=out_shape,
        scratch_shapes=[pltpu.VMEM((npair, STATE, LANES), F32), pltpu.VMEM((h, q), F32)],
        compiler_params=_params(("arbitrary", "arbitrary")),
    )(xbc, xbc, xbc, dt_e, cum_e, cum_t, dt_t, a_col, dy, sp)


def _dt_bwd(ddt, proj, bias_row, s, u, h):
    tr = _tile_rows(s, 4 * LANES)
    dtblk = (6 * u) // LANES

    def body(d_ref, raw_ref, b_ref, o_ref, db_ref):
        lane = lax.broadcasted_iota(jnp.int32, (tr, LANES), 1)
        v = jnp.where(lane < 2 * h, d_ref[...] * _sigmoid(raw_ref[...] + b_ref[...]), 0.0)
        o_ref[...] = v.astype(BF16)
        _acc_rows(db_ref, v)

    return _rowcall(body, "dt_bwd", s, tr, [_row(tr, LANES), _row(tr, LANES, dtblk), _full((1, LANES))],
                    [_row(tr, LANES), _full((1, LANES))],
                    [jax.ShapeDtypeStruct((s, LANES), BF16), jax.ShapeDtypeStruct((1, LANES), F32)],
                    (ddt, proj, bias_row))


def _group_rms(v, gw):
    outs, facs = [], []
    for k in range(v.shape[1] // gw):
        blk = v[:, k * gw:(k + 1) * gw]
        f = lax.rsqrt(jnp.mean(blk * blk, axis=-1, keepdims=True) + RMS_EPS)
        outs.append(blk * f)
        facs.append(jnp.broadcast_to(f, blk.shape))
    return jnp.concatenate(outs, axis=1), jnp.concatenate(facs, axis=1)


def _group_rms_bwd(dn, n, fac, gw):
    outs = []
    for k in range(n.shape[1] // gw):
        sl = slice(k * gw, (k + 1) * gw)
        outs.append(fac[:, sl] * (dn[:, sl] - n[:, sl] * jnp.mean(dn[:, sl] * n[:, sl], axis=-1, keepdims=True)))
    return jnp.concatenate(outs, axis=1)


def _gate_norm_fwd(y2, xbc, proj, d_e, norm_w, s, u, g):
    tr = _tile_rows(s, u)
    gw = u // g

    def body(y_ref, x_ref, z_ref, d_ref, w_ref, o_ref):
        ys = y_ref[0] + y_ref[1] + d_ref[...] * x_ref[...]
        n, _ = _group_rms(ys * _silu(z_ref[...]), gw)
        o_ref[...] = (n * w_ref[...]).astype(BF16)

    return _rowcall(body, "gate_norm_fwd", s, tr,
                    [pl.BlockSpec((2, tr, u), lambda i: (0, i, 0)), _row(tr, u), _row(tr, u), _full((1, u)), _full((1, u))],
                    _row(tr, u), jax.ShapeDtypeStruct((s, u), BF16), (y2, xbc, proj, d_e, norm_w))


def _gate_norm_bwd(dymix, y2, xbc, proj, d_e, norm_w, s, u, g):
    tr = _tile_rows(s, u)
    gw = u // g

    def body(dy_ref, y_ref, x_ref, z_ref, d_ref, w_ref, dys_ref, dz_ref, dxs_ref, dw_ref, dd_ref):
        xv = x_ref[...]
        zv = z_ref[...]
        ys = y_ref[0] + y_ref[1] + d_ref[...] * xv
        sz = _silu(zv)
        n, fac = _group_rms(ys * sz, gw)
        dout = dy_ref[...]
        _acc_rows(dw_ref, dout * n)
        dyg = _group_rms_bwd(dout * w_ref[...], n, fac, gw)
        dys = dyg * sz
        dys_ref[...] = dys
        dz_ref[...] = (dyg * ys * _dsilu(zv)).astype(BF16)
        dxs_ref[...] = dys * d_ref[...]
        _acc_rows(dd_ref, dys * xv)

    v = _full((1, u))
    return _rowcall(body, "gate_norm_bwd", s, tr,
                    [_row(tr, u), pl.BlockSpec((2, tr, u), lambda i: (0, i, 0)), _row(tr, u), _row(tr, u), v, v],
                    [_row(tr, u), _row(tr, u), _row(tr, u), v, v],
                    [jax.ShapeDtypeStruct((s, u), F32), jax.ShapeDtypeStruct((s, u), BF16),
                     jax.ShapeDtypeStruct((s, u), F32), jax.ShapeDtypeStruct((1, u), F32), jax.ShapeDtypeStruct((1, u), F32)],
                    (dymix, y2, xbc, proj, d_e, norm_w))


def _shortconv_fwd(proj, conv_w, norm_w, s, u):
    tr = _tile_rows(s, u)
    half = SC_CONV // 2

    def body(hp, hc, hn, b_ref, cp, cc, cn, cw_ref, w_ref, o_ref):
        t = _ext(hp, hc, hn, s, tr) * _ext(cp, cc, cn, s, tr)
        wv = cw_ref[...]
        acc = jnp.zeros_like(t)
        for k in range(SC_CONV):
            acc = acc + _shift(t, k - half) * wv[k:k + 1, :]
        n, _ = _group_rms(b_ref[...] * acc[HALO:HALO + tr], SC_GROUP_WIDTH)
        o_ref[...] = (n * w_ref[...]).astype(BF16)

    in_specs = _halo_specs(s, tr, u, 3) + [_row(tr, u, 4)] + _halo_specs(s, tr, u, 5) + [_full((SC_CONV, u)), _full((1, u))]
    return _rowcall(body, "shortconv_fwd", s, tr, in_specs, _row(tr, u), jax.ShapeDtypeStruct((s, u), BF16),
                    (proj,) * 7 + (conv_w, norm_w))


def _shortconv_bwd(dymix, proj, conv_w, norm_w, s, u):
    tr = _tile_rows(s, u)
    half = SC_CONV // 2

    def body(dp, dc_, dn, hp, hc, hn, bp, bc, bn, cp, cc, cn, cw_ref, w_ref, dh_ref, db_ref, dcc_ref, dcw_ref, dw_ref):
        dout = _ext(dp, dc_, dn, s, tr)
        hv = _ext(hp, hc, hn, s, tr)
        bv = _ext(bp, bc, bn, s, tr)
        cv = _ext(cp, cc, cn, s, tr)
        t = hv * cv
        wv = cw_ref[...]
        acc = jnp.zeros_like(t)
        for k in range(SC_CONV):
            acc = acc + _shift(t, k - half) * wv[k:k + 1, :]
        n, fac = _group_rms(bv * acc, SC_GROUP_WIDTH)
        cur = slice(HALO, HALO + tr)
        _acc_rows(dw_ref, (dout * n)[cur])
        dyv = _group_rms_bwd(dout * w_ref[...], n, fac, SC_GROUP_WIDTH)
        db_ref[...] = (dyv * acc)[cur].astype(BF16)
        dv = dyv * bv
        dt = jnp.zeros_like(t)
        rows = []
        for k in range(SC_CONV):
            dt = dt + _shift(dv, half - k) * wv[k:k + 1, :]
            rows.append(jnp.sum((_shift(t, k - half) * dv)[cur], axis=0, keepdims=True))
        dh_ref[...] = (dt * cv)[cur].astype(BF16)
        dcc_ref[...] = (dt * hv)[cur].astype(BF16)
        dwv = jnp.concatenate(rows + [jnp.zeros((SUBLANES - SC_CONV, u), F32)], axis=0)
        first = pl.program_id(0) == 0

        @pl.when(first)
        def _():
            dcw_ref[...] = dwv

        @pl.when(jnp.logical_not(first))
        def _():
            dcw_ref[...] += dwv

    in_specs = (_halo_specs(s, tr, u, 1) + _halo_specs(s, tr, u, 3) + _halo_specs(s, tr, u, 4) + _halo_specs(s, tr, u, 5)
                + [_full((SC_CONV, u)), _full((1, u))])
    return _rowcall(body, "shortconv_bwd", s, tr, in_specs,
                    [_row(tr, u)] * 3 + [_full((SUBLANES, u)), _full((1, u))],
                    [jax.ShapeDtypeStruct((s, u), BF16)] * 3
                    + [jax.ShapeDtypeStruct((SUBLANES, u), F32), jax.ShapeDtypeStruct((1, u), F32)],
                    (dymix,) * 3 + (proj,) * 9 + (conv_w, norm_w))


def _adam_math(w, g, m, v):
    m2 = ADAM_B1 * m + (1.0 - ADAM_B1) * g
    v2 = ADAM_B2 * v + (1.0 - ADAM_B2) * (g * g)
    m_hat = m2 / (1.0 - ADAM_B1 ** ADAM_STEP)
    v_hat = v2 / (1.0 - ADAM_B2 ** ADAM_STEP)
    delta = -ADAM_LR * (m_hat / (jnp.sqrt(v_hat) + ADAM_EPS) + ADAM_WD * w)
    return delta, m2, v2


def _adam_rows(r, c):
    return _pick(r, max(SUBLANES, (1 << 20) // (4 * c)), SUBLANES)


def _adam(w, g, m, v, *, name):
    r, c = w.shape
    tr = _adam_rows(r, c)

    def body(w_ref, g_ref, m_ref, v_ref, d_ref, m2_ref, v2_ref):
        d_ref[...], m2_ref[...], v2_ref[...] = _adam_math(w_ref[...], g_ref[...], m_ref[...], v_ref[...])

    return _rowcall(body, name, r, tr, [_row(tr, c)] * 4, [_row(tr, c)] * 3,
                    [jax.ShapeDtypeStruct((r, c), F32)] * 3, (w, g, m, v))


def _adam_outer(w, a_t, bmat, m, v, *, name):
    r, c = w.shape
    tr = _adam_rows(r, c)
    kk = a_t.shape[1]

    def body(w_ref, a_ref, b_ref, m_ref, v_ref, g_ref, d_ref, m2_ref, v2_ref):
        g = _dot(a_ref[...].astype(BF16), b_ref[...].astype(BF16))
        g_ref[...] = g
        d_ref[...], m2_ref[...], v2_ref[...] = _adam_math(w_ref[...], g, m_ref[...], v_ref[...])

    return _rowcall(body, name, r, tr, [_row(tr, c), _row(tr, kk), _full((kk, c)), _row(tr, c), _row(tr, c)],
                    [_row(tr, c)] * 4, [jax.ShapeDtypeStruct((r, c), F32)] * 4, (w, a_t, bmat, m, v))


ANY = pl.BlockSpec(memory_space=pl.ANY)
VMEM_WHOLE = pl.BlockSpec(memory_space=pltpu.VMEM)


def _place():
    x, y, c = lax.axis_index("x"), lax.axis_index("y"), lax.axis_index("c")
    return x, y, c


DMA_CHUNKS = 8


def _n_chunks(rows):
    n = DMA_CHUNKS
    while n > 1 and rows % (16 * n):
        n //= 2
    return n


def _allgather_small(v, *, name):
    m_per, n = v.shape

    def body(x_ref, out_ref, send_sems, recv_sems, local_sem):
        x, y, c = _place()
        me, sibling = (x, y, c), (x, y, 1 - c)
        chips = [(1 - x, y), (x, 1 - y), (1 - x, 1 - y)]

        def rows(px, py, pc):
            return out_ref.at[pl.ds((4 * px + 2 * py + pc) * m_per, m_per), :]

        def copy(k, block, to, src=None):
            return pltpu.make_async_remote_copy(
                src_ref=rows(*block) if src is None else src, dst_ref=rows(*block),
                send_sem=send_sems.at[k], recv_sem=recv_sems.at[k], device_id=to, device_id_type=MESH)

        mine = pltpu.make_async_copy(x_ref, rows(*me), local_sem)
        mine.start()
        first = [copy(0, me, sibling, src=x_ref)]
        first += [copy(1 + j, me, (*chip, c), src=x_ref) for j, chip in enumerate(chips)]
        for cp in first:
            cp.start()
        passed = [copy(4 + j, (*chip, c), sibling) for j, chip in enumerate(chips)]
        for j, chip in enumerate(chips):
            copy(1 + j, (*chip, c), me).wait_recv()
            passed[j].start()
        copy(0, sibling, me).wait_recv()
        for j, chip in enumerate(chips):
            copy(4 + j, (*chip, 1 - c), me).wait_recv()
        for cp in first + passed:
            cp.wait_send()
        mine.wait()

    return pl.pallas_call(
        body, name=name, out_shape=jax.ShapeDtypeStruct((N_DEV * m_per, n), v.dtype),
        in_specs=[VMEM_WHOLE], out_specs=VMEM_WHOLE,
        scratch_shapes=[pltpu.SemaphoreType.DMA((7,)), pltpu.SemaphoreType.DMA((7,)), pltpu.SemaphoreType.DMA],
        compiler_params=pltpu.CompilerParams(vmem_limit_bytes=VMEM_LIMIT),
    )(v)


def _gather_weight(wl, *, name):
    r, c_ = wl.shape
    half = r // 2
    nch = _n_chunks(half)
    rows = half // nch

    def body(w_ref, out_ref, send_sems, recv_sems, local_sems):
        x, y, c = _place()
        me, sibling = (x, y, c), (x, y, 1 - c)
        chips = [(1 - x, y), (x, 1 - y), (1 - x, 1 - y)]

        def blk(px, py, pc, i):
            return out_ref.at[2 * px + py, pl.ds(pc * half + i * rows, rows), :]

        def copy(k, i, block, to, src=None):
            return pltpu.make_async_remote_copy(
                src_ref=blk(*block, i) if src is None else src, dst_ref=blk(*block, i),
                send_sem=send_sems.at[k * nch + i], recv_sem=recv_sems.at[k * nch + i], device_id=to, device_id_type=MESH)

        first = [copy(j, i, me, (*chip, c), src=w_ref.at[pl.ds(c * half + i * rows, rows), :])
                 for i in range(nch) for j, chip in enumerate(chips)]
        for cp in first:
            cp.start()
        mine = [pltpu.make_async_copy(w_ref.at[pl.ds(i * rows, rows), :], out_ref.at[2 * x + y, pl.ds(i * rows, rows), :],
                                      local_sems.at[i]) for i in range(2 * nch)]
        for cp in mine:
            cp.start()
        passed = []
        for i in range(nch):
            for j, chip in enumerate(chips):
                copy(j, i, (*chip, c), me).wait_recv()
                passed.append(copy(3 + j, i, (*chip, c), sibling))
                passed[-1].start()
        for i in range(nch):
            for j, chip in enumerate(chips):
                copy(3 + j, i, (*chip, 1 - c), me).wait_recv()
        for cp in first + passed:
            cp.wait_send()
        for cp in mine:
            cp.wait()

    return pl.pallas_call(
        body, name=name, out_shape=jax.ShapeDtypeStruct((N_CHIPS, r, c_), wl.dtype),
        in_specs=[ANY], out_specs=ANY,
        scratch_shapes=[pltpu.SemaphoreType.DMA((6 * nch,)), pltpu.SemaphoreType.DMA((6 * nch,)),
                        pltpu.SemaphoreType.DMA((2 * nch,))],
    )(wl)


def _scatter_grads(gfull, *, name):
    _, r, c_ = gfull.shape
    half = r // 2
    nch = _n_chunks(half)
    rows = half // nch

    def body(g_ref, recv_ref, send_sems, recv_sems, local_sems):
        x, y, c = _place()

        def copy(q, i):
            tx = 1 - x if q & 4 else x
            ty = 1 - y if q & 2 else y
            tc = 1 - c if q & 1 else c
            return pltpu.make_async_remote_copy(
                src_ref=g_ref.at[2 * tx + ty, pl.ds(tc * half + i * rows, rows), :],
                dst_ref=recv_ref.at[q, pl.ds(i * rows, rows), :],
                send_sem=send_sems.at[(q - 1) * nch + i], recv_sem=recv_sems.at[(q - 1) * nch + i],
                device_id=(tx, ty, tc), device_id_type=MESH)

        copies = [copy(q, i) for i in range(nch) for q in range(N_DEV - 1, 0, -1)]
        for cp in copies:
            cp.start()
        mine = [pltpu.make_async_copy(g_ref.at[2 * x + y, pl.ds(c * half + i * rows, rows), :],
                                      recv_ref.at[0, pl.ds(i * rows, rows), :], local_sems.at[i]) for i in range(nch)]
        for cp in mine:
            cp.start()
        for cp in copies:
            cp.wait_recv()
        for cp in copies:
            cp.wait_send()
        for cp in mine:
            cp.wait()

    return pl.pallas_call(
        body, name=name, out_shape=jax.ShapeDtypeStruct((N_DEV, half, c_), gfull.dtype),
        in_specs=[ANY], out_specs=ANY,
        scratch_shapes=[pltpu.SemaphoreType.DMA((7 * nch,)), pltpu.SemaphoreType.DMA((7 * nch,)),
                        pltpu.SemaphoreType.DMA((nch,))],
    )(gfull)


def _sum_slots(recv, *, name):
    n, r, c_ = recv.shape
    tr = _pick(r, max(16, (1 << 19) // (2 * c_)), 16)

    def body(r_ref, o_ref):
        acc = r_ref[0].astype(F32)
        for k in range(1, n):
            acc = acc + r_ref[k].astype(F32)
        o_ref[...] = acc

    return _rowcall(body, name, r, tr, [pl.BlockSpec((n, tr, c_), lambda i: (0, i, 0))], _row(tr, c_),
                    jax.ShapeDtypeStruct((r, c_), F32), (recv,))


def _swap_halves(hv, *, name):
    half, c_ = hv.shape
    nch = _n_chunks(half)
    rows = half // nch

    def body(h_ref, out_ref, send_sems, recv_sems, local_sems):
        x, y, c = _place()

        def copy(i, pc):
            return pltpu.make_async_remote_copy(
                src_ref=h_ref.at[pl.ds(i * rows, rows), :], dst_ref=out_ref.at[pl.ds(pc * half + i * rows, rows), :],
                send_sem=send_sems.at[i], recv_sem=recv_sems.at[i], device_id=(x, y, 1 - c), device_id_type=MESH)

        sends = [copy(i, c) for i in range(nch)]
        for cp in sends:
            cp.start()
        mine = [pltpu.make_async_copy(h_ref.at[pl.ds(i * rows, rows), :], out_ref.at[pl.ds(c * half + i * rows, rows), :],
                                      local_sems.at[i]) for i in range(nch)]
        for cp in mine:
            cp.start()
        for i in range(nch):
            copy(i, 1 - c).wait_recv()
        for cp in sends:
            cp.wait_send()
        for cp in mine:
            cp.wait()

    return pl.pallas_call(
        body, name=name, out_shape=jax.ShapeDtypeStruct((2 * half, c_), hv.dtype),
        in_specs=[ANY], out_specs=ANY,
        scratch_shapes=[pltpu.SemaphoreType.DMA((nch,)), pltpu.SemaphoreType.DMA((nch,)), pltpu.SemaphoreType.DMA((nch,))],
    )(hv)


def _reduce_to_owner(gfull, *, name):
    recv = _scatter_grads(gfull, name=name + "_scatter")
    return _swap_halves(_sum_slots(recv, name=name + "_sum"), name=name + "_swap")


PACK_ROWS = 16


def _pack(parts):
    flat = [p.reshape(-1).astype(F32) for p in parts]
    n = sum(f.shape[0] for f in flat)
    unit = PACK_ROWS * LANES
    total = -(-n // unit) * unit
    if total > n:
        flat.append(jnp.zeros((total - n,), F32))
    where, off = [], 0
    for p in parts:
        where.append((off, p.shape))
        off += p.size
    return jnp.concatenate(flat).reshape(total // LANES, LANES), where


def _unpack(flat, where):
    v = flat.reshape(-1)
    return [v[off:off + _size(shape)].reshape(shape) for off, shape in where]


def _size(shape):
    n = 1
    for d in shape:
        n *= d
    return n


def _sample_step(x, target, mods, w_in_p, w_out, w_up_blk, w_down, sp):
    s, d = x.shape
    u = d // 2
    h = u // HEAD_DIM
    g = h // HEADS_PER_GROUP
    pw = w_in_p.shape[1]
    shift1, scale1, gate1, shift2, scale2, gate2 = mods

    a_f = -jnp.exp(sp["ssm_a_log_f"].reshape(-1))
    a_b = -jnp.exp(sp["ssm_a_log_b"].reshape(-1))
    pad_l = LANES - 2 * h
    a_row = jnp.pad(jnp.concatenate([a_f, a_b]), (0, pad_l)).reshape(1, LANES)
    bias_row = jnp.pad(jnp.concatenate([sp["ssm_dt_bias_f"].reshape(-1), sp["ssm_dt_bias_b"].reshape(-1)]),
                       (0, pad_l)).reshape(1, LANES)
    a_col = jnp.broadcast_to(jnp.stack([a_f, a_b])[:, :, None], (2, h, LANES))
    d_e = jnp.repeat(sp["ssm_d"].reshape(-1), HEAD_DIM).reshape(1, u)
    conv_w, conv_b = sp["ssm_conv_w"], sp["ssm_conv_b"].reshape(1, 2 * u)
    sc_conv_w = sp["sc_conv_w"]
    ssm_norm_w, sc_norm_w = sp["ssm_norm_w"].reshape(1, u), sp["sc_norm_w"].reshape(1, u)
    ln1_g, ln1_b = sp["ln1_g"].reshape(1, d), sp["ln1_b"].reshape(1, d)
    ln2_g, ln2_b = sp["ln2_g"].reshape(1, d), sp["ln2_b"].reshape(1, d)

    h1 = _modulate(x, scale1, shift1, name="modulate1")
    proj = _matmul(h1, w_in_p, name="mm_proj", tn=1280)
    xbc = _conv_silu_fwd(proj, conv_w, conv_b, s, u)
    dt, cum, dt_e, cum_e = _dt_prep(proj, bias_row, a_row, s, u, h)
    cum_t = jnp.stack([cum[:, :h].T, cum[:, h:2 * h].T])
    dt_t = jnp.stack([dt[:, :h].T, dt[:, h:2 * h].T])
    y2, states = _ssd_fwd(xbc, dt_e, cum_e, cum_t, s, h, g)
    y_ssm = _gate_norm_fwd(y2, xbc, proj, d_e, ssm_norm_w, s, u, g)
    y_sc = _shortconv_fwd(proj, sc_conv_w, sc_norm_w, s, u)
    ymix = jnp.concatenate([y_ssm, y_sc], axis=1)
    mix = _matmul(ymix, w_out, name="mm_mix")
    x1, h2 = _ln1_fwd(x, mix, gate1, ln1_g, ln1_b, scale2, shift2)
    up = _matmul(h2, w_up_blk, name="mm_up", b_blocks=N_CHIPS)
    ff = _relu2(up)
    f = _matmul(ff, w_down, name="mm_down")
    df, dr2, loss, dg2, db2, dgate2 = _ln2_loss_bwd(x1, f, target, gate2, ln2_g, ln2_b)

    gw_down = _matmul(ff, df, name="mm_gw_down", ta=True, out_dtype=BF16)
    dff = _matmul(df, w_down, name="mm_dff", tb=True)
    du = _relu2_bwd(dff, up)
    gw_up = _matmul(h2, du, name="mm_gw_up", ta=True, out_dtype=BF16, out_blocks=N_CHIPS)
    dh2 = _matmul(du, w_up_blk, name="mm_dh2", tb=True, b_blocks=N_CHIPS)
    dmix, dxa, dscale2, dshift2, dg1, db1, dgate1 = _ln1_bwd(dh2, dr2, x1, x, mix, scale2, gate1, ln1_g)
    gw_out = _matmul(ymix, dmix, name="mm_gw_out", ta=True, out_dtype=BF16)
    dymix = _matmul(dmix, w_out, name="mm_dymix", tb=True)
    dys, dz, dxs, dnw, dd_e = _gate_norm_bwd(dymix, y2, xbc, proj, d_e, ssm_norm_w, s, u, g)
    dx2, dbb, dcc, ddt_t, da = _ssd_bwd(xbc, dt_e, cum_e, cum_t, dt_t, a_col, dys, states, s, h, g)
    dxbc = jnp.concatenate([dx2[0] + dx2[1] + dxs, dbb[0] + dbb[1], dcc[0] + dcc[1]], axis=1)
    du_xbc, dcw, dcb = _conv_silu_bwd(proj, dxbc, conv_w, conv_b, s, u)
    ddt = jnp.pad(jnp.concatenate([ddt_t[0].T, ddt_t[1].T], axis=1), ((0, 0), (0, pad_l)))
    ddt_raw, dbias = _dt_bwd(ddt, proj, bias_row, s, u, h)
    dh_sc, db_sc, dc_sc, dscw, dscnw = _shortconv_bwd(dymix, proj, sc_conv_w, sc_norm_w, s, u)
    dproj = jnp.concatenate([dz, du_xbc, dh_sc, db_sc, dc_sc, ddt_raw,
                             jnp.zeros((s, pw - 6 * u - LANES), BF16)], axis=1)
    gw_in_p = _matmul(h1, dproj, name="mm_gw_in", ta=True, out_dtype=BF16, tn=1280)
    dh1 = _matmul(dproj, w_in_p, name="mm_dh1", tb=True, tk=1280)
    grad_x, dscale1, dshift1 = _dx_final(dxa, dh1, x, scale1)

    small = {
        "dmod": jnp.concatenate([dshift1, dscale1, dgate1, dshift2, dscale2, dgate2], axis=1),
        "ssm_conv_b": dcb,
        "ssm_dt_bias_f": dbias[0, :h], "ssm_dt_bias_b": dbias[0, h:2 * h],
        "ssm_a_log_f": jnp.sum(da[0], axis=1) * a_f, "ssm_a_log_b": jnp.sum(da[1], axis=1) * a_b,
        "ssm_d": jnp.sum(dd_e.reshape(h, HEAD_DIM), axis=1),
        "ssm_norm_w": dnw, "sc_norm_w": dscnw,
        "ln1_g": dg1, "ln1_b": db1, "ln2_g": dg2, "ln2_b": db2,
        "ssm_conv_w": dcw[:SSM_CONV], "sc_conv_w": dscw[:SC_CONV],
    }
    big = {"w_in_p": gw_in_p, "w_out": gw_out, "w_up": gw_up, "w_down": gw_down}
    return loss, grad_x, big, small


WEIGHTS = ['w_ada', 'b_ada', 'w_in', 'ssm_conv_w', 'ssm_conv_b', 'ssm_dt_bias_f', 'ssm_dt_bias_b', 'ssm_a_log_f',
           'ssm_a_log_b', 'ssm_d', 'ssm_norm_w', 'sc_conv_w', 'sc_norm_w', 'w_out', 'ln1_g', 'ln1_b', 'w_up', 'w_down',
           'ln2_g', 'ln2_b']
BIG = ('w_ada', 'w_in', 'w_out', 'w_up', 'w_down')
SMALL = tuple(n for n in WEIGHTS if n not in BIG)
SMALL_SHARDED = ('ssm_conv_w', 'sc_conv_w')


def _p_layout_width(u):
    return -(-(6 * u + LANES) // 512) * 512


def kernel(x, c, w_ada, b_ada, w_in, ssm_conv_w, ssm_conv_b, ssm_dt_bias_f, ssm_dt_bias_b, ssm_a_log_f, ssm_a_log_b, ssm_d, ssm_norm_w, sc_conv_w, sc_norm_w, w_out, ln1_g, ln1_b, w_up, w_down, ln2_g, ln2_b, loss_target, m_w_ada, m_b_ada, m_w_in, m_ssm_conv_w, m_ssm_conv_b, m_ssm_dt_bias_f, m_ssm_dt_bias_b, m_ssm_a_log_f, m_ssm_a_log_b, m_ssm_d, m_ssm_norm_w, m_sc_conv_w, m_sc_norm_w, m_w_out, m_ln1_g, m_ln1_b, m_w_up, m_w_down, m_ln2_g, m_ln2_b, v_w_ada, v_b_ada, v_w_in, v_ssm_conv_w, v_ssm_conv_b, v_ssm_dt_bias_f, v_ssm_dt_bias_b, v_ssm_a_log_f, v_ssm_a_log_b, v_ssm_d, v_ssm_norm_w, v_sc_conv_w, v_sc_norm_w, v_w_out, v_ln1_g, v_ln1_b, v_w_up, v_w_down, v_ln2_g, v_ln2_b):
    given = dict(locals())
    w = {n: given[n][0] for n in WEIGHTS}
    m = {n: given["m_" + n][0] for n in WEIGHTS}
    v = {n: given["v_" + n][0] for n in WEIGHTS}
    xs, tgt = x[0], loss_target[0]
    s, d = xs.shape
    u = d // 2
    h = u // HEAD_DIM
    nmod = N_MOD * d
    nmod_loc = nmod // N_CHIPS
    ax, ay, ac = lax.axis_index("x"), lax.axis_index("y"), lax.axis_index("c")
    chip = 2 * ax + ay
    me = 2 * chip + ac

    pay1, where1 = _pack([c[0], w["ssm_conv_w"], w["sc_conv_w"]])
    g1 = _allgather_small(pay1, name="ag_inputs").reshape(N_DEV, -1)
    per_dev = [_unpack(g1[k], where1) for k in range(N_DEV)]
    c_all = jnp.stack([p[0] for p in per_dev])
    ssm_conv_w_full = jnp.concatenate([per_dev[2 * k][1] for k in range(N_CHIPS)], axis=1)
    sc_conv_w_full = jnp.concatenate([per_dev[2 * k][2] for k in range(N_CHIPS)], axis=1)

    sc_all = _silu(c_all)
    sc16 = jnp.pad(sc_all, ((0, 16 - N_DEV), (0, 0)))
    b_loc = lax.dynamic_slice(w["b_ada"], (chip * nmod_loc,), (nmod_loc,))
    mod_loc = _matmul(sc16, w["w_ada"], name="mm_mod")[:N_DEV] + b_loc[None, :]
    pay2, where2 = _pack([mod_loc])
    g2 = _allgather_small(pay2, name="ag_mod").reshape(N_DEV, -1)
    mod_blocks = jnp.stack([_unpack(g2[2 * k], where2)[0] for k in range(N_CHIPS)])
    mod_mine = lax.dynamic_index_in_dim(mod_blocks, me, axis=1, keepdims=False).reshape(N_MOD, 1, d)
    mods = [mod_mine[k] for k in range(N_MOD)]

    din = w["w_in"].shape[1] * N_CHIPS
    g_in = _gather_weight(w["w_in"].astype(BF16), name="gather_w_in")
    w_in_full = jnp.concatenate([g_in[k] for k in range(N_CHIPS)], axis=1)
    pw = _p_layout_width(u)
    w_in_p = jnp.concatenate([w_in_full[:, :3 * u], w_in_full[:, 3 * u + 2 * h:], w_in_full[:, 3 * u:3 * u + 2 * h],
                              jnp.zeros((d, pw - din), BF16)], axis=1)
    w_out_full = _gather_weight(w["w_out"].astype(BF16), name="gather_w_out").reshape(d, d)
    w_up_blk = _gather_weight(w["w_up"].astype(BF16), name="gather_w_up")
    dff_ = w["w_up"].shape[1] * N_CHIPS
    w_down_full = _gather_weight(w["w_down"].astype(BF16), name="gather_w_down").reshape(dff_, d)

    sp = {n: w[n] for n in SMALL}
    sp["ssm_conv_w"], sp["sc_conv_w"] = ssm_conv_w_full, sc_conv_w_full
    loss_loc, grad_x, big, small = _sample_step(xs, tgt, mods, w_in_p, w_out_full, w_up_blk, w_down_full, sp)

    small_names = ["dmod"] + [n for n in SMALL if n != "b_ada"]
    pay3, where3 = _pack([loss_loc] + [small[n] for n in small_names])
    g3 = _allgather_small(pay3, name="ag_small_grads")
    tot = _unpack(_sum_slots(g3.reshape(N_DEV, -1, LANES), name="sum_small_grads"), where3)
    loss = tot[0].reshape(())
    gsum = dict(zip(small_names, tot[1:]))
    dmod_all = jnp.stack([_unpack(g3.reshape(N_DEV, -1)[k], where3)[1].reshape(-1) for k in range(N_DEV)])

    grads = {}
    grads["b_ada"] = gsum["dmod"].reshape(-1)
    for n in SMALL:
        if n in SMALL_SHARDED:
            loc = w[n].shape[1]
            grads[n] = lax.dynamic_slice_in_dim(gsum[n], chip * loc, loc, axis=1)
        elif n != "b_ada":
            grads[n] = gsum[n].reshape(w[n].shape)

    gp = big["w_in_p"]
    g_in_full = jnp.concatenate([gp[:, :3 * u], gp[:, 6 * u:6 * u + 2 * h], gp[:, 3 * u:6 * u]], axis=1)
    g_in_blk = jnp.transpose(g_in_full.reshape(d, N_CHIPS, din // N_CHIPS), (1, 0, 2))
    grads["w_in"] = _reduce_to_owner(g_in_blk, name="rs_w_in")
    grads["w_out"] = _reduce_to_owner(big["w_out"].reshape(N_CHIPS, d // N_CHIPS, d), name="rs_w_out")
    grads["w_up"] = _reduce_to_owner(big["w_up"], name="rs_w_up")
    grads["w_down"] = _reduce_to_owner(big["w_down"].reshape(N_CHIPS, dff_ // N_CHIPS, d), name="rs_w_down")

    delta, new_m, new_v = {}, {}, {}
    dm_loc = lax.dynamic_slice_in_dim(dmod_all, chip * nmod_loc, nmod_loc, axis=1)
    grads["w_ada"], delta["w_ada"], new_m["w_ada"], new_v["w_ada"] = _adam_outer(
        w["w_ada"], sc16.T, jnp.pad(dm_loc, ((0, 16 - N_DEV), (0, 0))), m["w_ada"], v["w_ada"], name="adam_w_ada")
    for n in ("w_in", "w_out", "w_up", "w_down"):
        delta[n], new_m[n], new_v[n] = _adam(w[n], grads[n], m[n], v[n], name="adam_" + n)
    pw_, where_s = _pack([w[n] for n in SMALL])
    pg_, _ = _pack([grads[n] for n in SMALL])
    pm_, _ = _pack([m[n] for n in SMALL])
    pv_, _ = _pack([v[n] for n in SMALL])
    sd, sm, sv = _adam(pw_, pg_, pm_, pv_, name="adam_small")
    for n, a, b_, c_ in zip(SMALL, _unpack(sd, where_s), _unpack(sm, where_s), _unpack(sv, where_s)):
        delta[n], new_m[n], new_v[n] = a, b_, c_

    def lead(t):
        return t[None]

    return (loss, grad_x[None], *[lead(grads[n].reshape(w[n].shape)) for n in WEIGHTS],
            *[lead(delta[n]) for n in WEIGHTS], *[lead(new_m[n]) for n in WEIGHTS], *[lead(new_v[n]) for n in WEIGHTS])
```

```python
import functools

import jax
import jax.numpy as jnp
from jax import lax
from jax.experimental import pallas as pl
from jax.experimental.pallas import tpu as pltpu

F32 = jnp.float32
BF16 = jnp.bfloat16

CHUNK = 128
HEAD_DIM = 64
STATE = 128
HEADS_PER_GROUP = 4
SC_GROUP_WIDTH = 128
SSM_CONV = 5
SC_CONV = 3
N_MOD = 6
DEEPNORM_ALPHA = 2.0 ** 0.25
LN_EPS = 1e-5
RMS_EPS = 1e-5
ADAM_LR = 0.001
ADAM_B1 = 0.9
ADAM_B2 = 0.999
ADAM_EPS = 1e-08
ADAM_WD = 0.01
ADAM_STEP = 10
N_CHIPS = 4
N_DEV = 8
LANES = 128
SUBLANES = 8
HALO = 8
VMEM_LIMIT = 56 * 1024 * 1024
MESH = pl.DeviceIdType.MESH


def _params(sem=None):
    return pltpu.CompilerParams(dimension_semantics=sem, vmem_limit_bytes=VMEM_LIMIT)


def _pick(n, target, mult=LANES):
    best = None
    t = mult
    while t <= min(n, target):
        if n % t == 0:
            best = t
        t += mult
    return best if best is not None else n


ROW_TILE_BYTES = 1 << 20


def _tile_rows(s, width, mult=SUBLANES):
    return _pick(s, max(mult, ROW_TILE_BYTES // (4 * width)), mult)


def _sigmoid(v):
    return 1.0 / (1.0 + jnp.exp(-v))


def _silu(v):
    return v * _sigmoid(v)


def _dsilu(v):
    s = _sigmoid(v)
    return s * (1.0 + v * (1.0 - s))


def _softplus(v):
    e = jnp.exp(-jnp.abs(v))
    return jnp.maximum(v, 0.0) + jnp.where(e < 1e-4, e - 0.5 * e * e, jnp.log(1.0 + e))


def _dot(a, b):
    return jnp.dot(a, b, preferred_element_type=F32)


def _dot_nt(a, b):
    return lax.dot_general(a, b, (((1,), (1,)), ((), ())), preferred_element_type=F32)


def _dot_tn(a, b):
    return lax.dot_general(a, b, (((0,), (0,)), ((), ())), preferred_element_type=F32)


def _split3(v):
    hi = v.astype(BF16)
    r1 = v - hi.astype(F32)
    mid = r1.astype(BF16)
    lo = (r1 - mid.astype(F32)).astype(BF16)
    return hi, mid, lo


def _dot3_r(v, onehot):
    hi, mid, lo = _split3(v)
    return _dot(hi, onehot) + _dot(mid, onehot) + _dot(lo, onehot)


def _dot3_l(onehot, v):
    hi, mid, lo = _split3(v)
    return _dot(onehot, hi) + _dot(onehot, mid) + _dot(onehot, lo)


def _matmul(a, b, *, name, ta=False, tb=False, out_dtype=F32, b_blocks=1, out_blocks=1,
            tm=1024, tn=1024, tk=1024):
    if ta:
        K, M = a.shape
    else:
        M, K = a.shape
    if b_blocks > 1:
        nb, r_, c_ = b.shape
        if tb:
            N, K2 = r_, c_ * nb
        else:
            K2, N = r_, c_ * nb
    else:
        if tb:
            N, K2 = b.shape
        else:
            K2, N = b.shape
    assert K == K2, (a.shape, b.shape, ta, tb)
    tm = _pick(M, tm)
    n_unit = N // b_blocks if (b_blocks > 1 and not tb) else N
    n_unit = min(n_unit, N // out_blocks)
    tn = _pick(n_unit, tn)
    k_unit = K // b_blocks if (b_blocks > 1 and tb) else K
    tk = _pick(k_unit, tk)
    gm, gn, gk = M // tm, N // tn, K // tk

    if ta:
        a_spec = pl.BlockSpec((tk, tm), lambda i, j, k: (k, i))
    else:
        a_spec = pl.BlockSpec((tm, tk), lambda i, j, k: (i, k))
    if b_blocks > 1 and not tb:
        per = (N // b_blocks) // tn
        b_spec = pl.BlockSpec((None, tk, tn), lambda i, j, k: (j // per, k, j % per))
    elif b_blocks > 1 and tb:
        per = (K // b_blocks) // tk
        b_spec = pl.BlockSpec((None, tn, tk), lambda i, j, k: (k // per, j, k % per))
    elif tb:
        b_spec = pl.BlockSpec((tn, tk), lambda i, j, k: (j, k))
    else:
        b_spec = pl.BlockSpec((tk, tn), lambda i, j, k: (k, j))
    if out_blocks > 1:
        per_o = (N // out_blocks) // tn
        o_spec = pl.BlockSpec((None, tm, tn), lambda i, j, k: (j // per_o, i, j % per_o))
        o_shape = jax.ShapeDtypeStruct((out_blocks, M, N // out_blocks), out_dtype)
    else:
        o_spec = pl.BlockSpec((tm, tn), lambda i, j, k: (i, j))
        o_shape = jax.ShapeDtypeStruct((M, N), out_dtype)

    def body(a_ref, b_ref, o_ref, acc):
        k = pl.program_id(2)
        av = a_ref[...].astype(BF16)
        bv = b_ref[...].astype(BF16)
        if ta and tb:
            raise NotImplementedError
        if ta:
            p = _dot_tn(av, bv)
        elif tb:
            p = _dot_nt(av, bv)
        else:
            p = _dot(av, bv)

        @pl.when(k == 0)
        def _():
            acc[...] = p

        @pl.when(k > 0)
        def _():
            acc[...] += p

        @pl.when(k == gk - 1)
        def _():
            o_ref[...] = acc[...].astype(out_dtype)

    return pl.pallas_call(
        body, name=name, grid=(gm, gn, gk), in_specs=[a_spec, b_spec], out_specs=o_spec,
        out_shape=o_shape, scratch_shapes=[pltpu.VMEM((tm, tn), F32)],
        compiler_params=_params(("parallel", "parallel", "arbitrary")),
    )(a, b)


def _row(tr, w, blk=0):
    return pl.BlockSpec((tr, w), lambda i: (i, blk))


def _full(shape):
    nd = len(shape)
    return pl.BlockSpec(shape, lambda i: (0,) * nd)


def _halo_specs(s, tr, w, blk=0):
    per = tr // HALO
    last = s // HALO - 1
    return [
        pl.BlockSpec((HALO, w), lambda i: (jnp.maximum(i * per - 1, 0), blk)),
        pl.BlockSpec((tr, w), lambda i: (i, blk)),
        pl.BlockSpec((HALO, w), lambda i: (jnp.minimum((i + 1) * per, last), blk)),
    ]


def _ext(prev_ref, cur_ref, next_ref, s, tr):
    i = pl.program_id(0)
    e = jnp.concatenate([prev_ref[...].astype(F32), cur_ref[...].astype(F32), next_ref[...].astype(F32)], axis=0)
    rid = i * tr - HALO + lax.broadcasted_iota(jnp.int32, e.shape, 0)
    return jnp.where((rid >= 0) & (rid < s), e, 0.0)


def _valid_rows(shape, s, tr):
    i = pl.program_id(0)
    rid = i * tr - HALO + lax.broadcasted_iota(jnp.int32, shape, 0)
    return (rid >= 0) & (rid < s)


def _shift(e, k):
    if k == 0:
        return e
    n = e.shape[0]
    return pltpu.roll(e, (n - k) % n, 0)


def _acc_rows(ref, v):
    s = jnp.sum(v, axis=0, keepdims=True)

    @pl.when(pl.program_id(0) == 0)
    def _():
        ref[...] = s

    @pl.when(pl.program_id(0) > 0)
    def _():
        ref[...] += s


def _rowcall(body, name, s, tr, in_specs, out_specs, out_shape, args):
    return pl.pallas_call(
        body, name=name, grid=(s // tr,), in_specs=in_specs, out_specs=out_specs, out_shape=out_shape,
        compiler_params=_params(("arbitrary",)),
    )(*args)


def _modulate(x, scale, shift, *, name):
    s, d = x.shape
    tr = _tile_rows(s, d)

    def body(x_ref, sc_ref, sh_ref, o_ref):
        o_ref[...] = (x_ref[...] * (1.0 + sc_ref[...]) + sh_ref[...]).astype(BF16)

    return _rowcall(body, name, s, tr, [_row(tr, d), _full((1, d)), _full((1, d))], _row(tr, d),
                    jax.ShapeDtypeStruct((s, d), BF16), (x, scale, shift))


def _ln_stats(r):
    mu = jnp.mean(r, axis=-1, keepdims=True)
    xc = r - mu
    var = jnp.mean(xc * xc, axis=-1, keepdims=True)
    rstd = lax.rsqrt(var + LN_EPS)
    return xc * rstd, rstd


def _ln1_fwd(x, mix, gate, g, b, scale2, shift2):
    s, d = x.shape
    tr = _tile_rows(s, d)

    def body(x_ref, m_ref, gt_ref, g_ref, b_ref, sc_ref, sh_ref, x1_ref, h2_ref):
        r = DEEPNORM_ALPHA * x_ref[...] + (1.0 + gt_ref[...]) * m_ref[...]
        xh, _ = _ln_stats(r)
        x1 = xh * g_ref[...] + b_ref[...]
        x1_ref[...] = x1
        h2_ref[...] = (x1 * (1.0 + sc_ref[...]) + sh_ref[...]).astype(BF16)

    v = _full((1, d))
    return _rowcall(body, "ln1_fwd", s, tr, [_row(tr, d), _row(tr, d), v, v, v, v, v],
                    [_row(tr, d), _row(tr, d)],
                    [jax.ShapeDtypeStruct((s, d), F32), jax.ShapeDtypeStruct((s, d), BF16)],
                    (x, mix, gate, g, b, scale2, shift2))


def _relu2(u):
    s, f = u.shape
    tr = _tile_rows(s, f)

    def body(u_ref, o_ref):
        r = jnp.maximum(u_ref[...], 0.0)
        o_ref[...] = (r * r).astype(BF16)

    return _rowcall(body, "relu2", s, tr, [_row(tr, f)], _row(tr, f), jax.ShapeDtypeStruct((s, f), BF16), (u,))


def _ln2_loss_bwd(x1, f, target, gate, g, b):
    s, d = x1.shape
    tr = _tile_rows(s, d)

    def body(x1_ref, f_ref, t_ref, gt_ref, g_ref, b_ref, df_ref, dr_ref, loss_ref, dg_ref, db_ref, dgt_ref):
        fv = f_ref[...]
        r = DEEPNORM_ALPHA * x1_ref[...] + (1.0 + gt_ref[...]) * fv
        xh, rstd = _ln_stats(r)
        y = xh * g_ref[...] + b_ref[...]
        err = y - t_ref[...]
        _acc_rows(loss_ref, 0.5 * jnp.mean(err * err, axis=-1, keepdims=True))
        dy = err * (1.0 / d)
        _acc_rows(dg_ref, dy * xh)
        _acc_rows(db_ref, dy)
        dxh = dy * g_ref[...]
        dr = rstd * (dxh - jnp.mean(dxh, axis=-1, keepdims=True) - xh * jnp.mean(dxh * xh, axis=-1, keepdims=True))
        dr_ref[...] = dr
        df_ref[...] = ((1.0 + gt_ref[...]) * dr).astype(BF16)
        _acc_rows(dgt_ref, dr * fv)

    v = _full((1, d))
    one = _full((1, 1))
    return _rowcall(body, "ln2_loss_bwd", s, tr, [_row(tr, d), _row(tr, d), _row(tr, d), v, v, v],
                    [_row(tr, d), _row(tr, d), one, v, v, v],
                    [jax.ShapeDtypeStruct((s, d), BF16), jax.ShapeDtypeStruct((s, d), F32),
                     jax.ShapeDtypeStruct((1, 1), F32)] + [jax.ShapeDtypeStruct((1, d), F32)] * 3,
                    (x1, f, target, gate, g, b))


def _relu2_bwd(dff, u):
    s, f = u.shape
    tr = _tile_rows(s, f)

    def body(d_ref, u_ref, o_ref):
        o_ref[...] = (d_ref[...] * 2.0 * jnp.maximum(u_ref[...], 0.0)).astype(BF16)

    return _rowcall(body, "relu2_bwd", s, tr, [_row(tr, f), _row(tr, f)], _row(tr, f),
                    jax.ShapeDtypeStruct((s, f), BF16), (dff, u))


def _ln1_bwd(dh2, dr2, x1, x, mix, scale2, gate1, g1):
    s, d = x.shape
    tr = _tile_rows(s, d)

    def body(dh_ref, dr2_ref, x1_ref, x_ref, m_ref, sc_ref, gt_ref, g_ref,
             dm_ref, dxa_ref, dsc_ref, dsh_ref, dg_ref, db_ref, dgt_ref):
        dh = dh_ref[...]
        _acc_rows(dsc_ref, dh * x1_ref[...])
        _acc_rows(dsh_ref, dh)
        dy = dh * (1.0 + sc_ref[...]) + DEEPNORM_ALPHA * dr2_ref[...]
        mv = m_ref[...]
        r = DEEPNORM_ALPHA * x_ref[...] + (1.0 + gt_ref[...]) * mv
        xh, rstd = _ln_stats(r)
        _acc_rows(dg_ref, dy * xh)
        _acc_rows(db_ref, dy)
        dxh = dy * g_ref[...]
        dr = rstd * (dxh - jnp.mean(dxh, axis=-1, keepdims=True) - xh * jnp.mean(dxh * xh, axis=-1, keepdims=True))
        dm_ref[...] = ((1.0 + gt_ref[...]) * dr).astype(BF16)
        dxa_ref[...] = DEEPNORM_ALPHA * dr
        _acc_rows(dgt_ref, dr * mv)

    v = _full((1, d))
    return _rowcall(body, "ln1_bwd", s, tr, [_row(tr, d)] * 5 + [v, v, v],
                    [_row(tr, d), _row(tr, d), v, v, v, v, v],
                    [jax.ShapeDtypeStruct((s, d), BF16), jax.ShapeDtypeStruct((s, d), F32)]
                    + [jax.ShapeDtypeStruct((1, d), F32)] * 5,
                    (dh2, dr2, x1, x, mix, scale2, gate1, g1))


def _dx_final(dxa, dh1, x, scale1):
    s, d = x.shape
    tr = _tile_rows(s, d)

    def body(a_ref, dh_ref, x_ref, sc_ref, o_ref, dsc_ref, dsh_ref):
        dh = dh_ref[...]
        o_ref[...] = a_ref[...] + dh * (1.0 + sc_ref[...])
        _acc_rows(dsc_ref, dh * x_ref[...])
        _acc_rows(dsh_ref, dh)

    v = _full((1, d))
    return _rowcall(body, "dx_final", s, tr, [_row(tr, d)] * 3 + [v], [_row(tr, d), v, v],
                    [jax.ShapeDtypeStruct((s, d), F32)] + [jax.ShapeDtypeStruct((1, d), F32)] * 2,
                    (dxa, dh1, x, scale1))


def _conv_silu_fwd(proj, conv_w, conv_b, s, u):
    tr = _tile_rows(s, u)
    w = 2 * u
    half = SSM_CONV // 2

    def body(p0, c0, n0, p1, c1, n1, w_ref, b_ref, o_ref):
        for blk, (pr, cr, nr) in enumerate(((p0, c0, n0), (p1, c1, n1))):
            e = _ext(pr, cr, nr, s, tr)
            wv = w_ref[:, blk * u:(blk + 1) * u]
            acc = jnp.zeros_like(e)
            for k in range(SSM_CONV):
                acc = acc + _shift(e, k - half) * wv[k:k + 1, :]
            pre = acc[HALO:HALO + tr] + b_ref[:, blk * u:(blk + 1) * u]
            o_ref[:, blk * u:(blk + 1) * u] = _silu(pre)

    in_specs = _halo_specs(s, tr, u, 1) + _halo_specs(s, tr, u, 2) + [_full((SSM_CONV, w)), _full((1, w))]
    return _rowcall(body, "conv_silu_fwd", s, tr, in_specs, _row(tr, w), jax.ShapeDtypeStruct((s, w), F32),
                    (proj,) * 6 + (conv_w, conv_b))


def _conv_silu_bwd(proj, dxbc, conv_w, conv_b, s, u):
    tr = _tile_rows(s, u)
    w = 2 * u
    half = SSM_CONV // 2

    def body(p0, c0, n0, p1, c1, n1, dp0, dc0, dn0, dp1, dc1, dn1, w_ref, b_ref, du_ref, dw_ref, db_ref):
        for blk, (ur, dr) in enumerate((((p0, c0, n0), (dp0, dc0, dn0)), ((p1, c1, n1), (dp1, dc1, dn1)))):
            e = _ext(*ur, s, tr)
            de = _ext(*dr, s, tr)
            wv = w_ref[:, blk * u:(blk + 1) * u]
            acc = jnp.zeros_like(e)
            for k in range(SSM_CONV):
                acc = acc + _shift(e, k - half) * wv[k:k + 1, :]
            pre = acc + b_ref[:, blk * u:(blk + 1) * u]
            dpre = jnp.where(_valid_rows(e.shape, s, tr), de * _dsilu(pre), 0.0)
            du = jnp.zeros_like(e)
            rows = []
            for k in range(SSM_CONV):
                du = du + _shift(dpre, half - k) * wv[k:k + 1, :]
                rows.append(jnp.sum((_shift(e, k - half) * dpre)[HALO:HALO + tr], axis=0, keepdims=True))
            du_ref[:, blk * u:(blk + 1) * u] = du[HALO:HALO + tr].astype(BF16)
            dwv = jnp.concatenate(rows + [jnp.zeros((SUBLANES - SSM_CONV, u), F32)], axis=0)
            dbv = jnp.sum(dpre[HALO:HALO + tr], axis=0, keepdims=True)
            first = pl.program_id(0) == 0

            @pl.when(first)
            def _():
                dw_ref[:, blk * u:(blk + 1) * u] = dwv
                db_ref[:, blk * u:(blk + 1) * u] = dbv

            @pl.when(jnp.logical_not(first))
            def _():
                dw_ref[:, blk * u:(blk + 1) * u] += dwv
                db_ref[:, blk * u:(blk + 1) * u] += dbv

    in_specs = (_halo_specs(s, tr, u, 1) + _halo_specs(s, tr, u, 2) + _halo_specs(s, tr, u, 0)
                + _halo_specs(s, tr, u, 1) + [_full((SSM_CONV, w)), _full((1, w))])
    return _rowcall(body, "conv_silu_bwd", s, tr, in_specs,
                    [_row(tr, w), _full((SUBLANES, w)), _full((1, w))],
                    [jax.ShapeDtypeStruct((s, w), BF16), jax.ShapeDtypeStruct((SUBLANES, w), F32),
                     jax.ShapeDtypeStruct((1, w), F32)],
                    (proj,) * 6 + (dxbc,) * 6 + (conv_w, conv_b))


def _expanders(h):
    col64 = jnp.arange(2 * h * HEAD_DIM) // HEAD_DIM
    col128 = jnp.arange(2 * h * LANES) // LANES
    row = jnp.arange(LANES)[:, None]
    return (row == col64[None, :]).astype(BF16), (row == col128[None, :]).astype(BF16)


def _dt_prep(proj, bias_row, a_row, s, u, h):
    q = CHUNK
    e64, e128 = _expanders(h)
    ds = h * HEAD_DIM
    dtblk = (6 * u) // LANES

    def body(raw_ref, b_ref, a_ref, e64_ref, e128_ref, dt_ref, cum_ref, dte_ref, cume_ref):
        lane = lax.broadcasted_iota(jnp.int32, (q, LANES), 1)
        dt = jnp.where(lane < 2 * h, _softplus(raw_ref[...] + b_ref[...]), 0.0)
        da = dt * a_ref[...]
        ii = lax.broadcasted_iota(jnp.int32, (q, q), 0)
        kk = lax.broadcasted_iota(jnp.int32, (q, q), 1)
        lower = (kk <= ii).astype(F32).astype(BF16)
        upper = (kk >= ii).astype(F32).astype(BF16)
        cum = jnp.where(lane < h, _dot3_l(lower, da), _dot3_l(upper, da))
        dt_ref[...] = dt
        cum_ref[...] = cum
        dte = _dot3_r(dt, e64_ref[...])
        cume = _dot3_r(cum, e128_ref[...])
        dte_ref[0] = dte[:, :ds]
        dte_ref[1] = dte[:, ds:]
        cume_ref[0] = cume[:, :h * LANES]
        cume_ref[1] = cume[:, h * LANES:]

    in_specs = [pl.BlockSpec((q, LANES), lambda i: (i, dtblk)), _full((1, LANES)), _full((1, LANES)),
                _full(e64.shape), _full(e128.shape)]
    out_specs = [_row(q, LANES), _row(q, LANES),
                 pl.BlockSpec((2, q, ds), lambda i: (0, i, 0)), pl.BlockSpec((2, q, h * LANES), lambda i: (0, i, 0))]
    out_shape = [jax.ShapeDtypeStruct((s, LANES), F32), jax.ShapeDtypeStruct((s, LANES), F32),
                 jax.ShapeDtypeStruct((2, s, ds), F32), jax.ShapeDtypeStruct((2, s, h * LANES), F32)]
    return _rowcall(body, "dt_prep", s, q, in_specs, out_specs, out_shape, (proj, bias_row, a_row, e64, e128))


def _ssd_specs(s, h, g):
    q = CHUNK
    nc = s // q
    ds = h * HEAD_DIM
    nb = g * STATE
    return q, nc, ds, nb


def _ssd_fwd(xbc, dt_e, cum_e, cum_t, s, h, g):
    q, nc, ds, nb = _ssd_specs(s, h, g)
    npair = h // 2

    def cidx(d, i):
        return jnp.where(d == 0, i, nc - 1 - i)

    def body(x_ref, b_ref, c_ref, dt_ref, cum_ref, cumt_ref, y_ref, sp_ref, st):
        d = pl.program_id(0)
        i = pl.program_id(1)

        @pl.when(i == 0)
        def _():
            st[...] = jnp.zeros_like(st)

        rev = d == 1
        ii = lax.broadcasted_iota(jnp.int32, (q, q), 0)
        jj = lax.broadcasted_iota(jnp.int32, (q, q), 1)
        sgn = jnp.where(rev, -1, 1)
        mask = (jj - ii) * sgn <= 0
        left = lax.broadcasted_iota(jnp.int32, (q, LANES), 1) < HEAD_DIM

        def group(gi, carry):
            goff = pl.multiple_of(gi * STATE, STATE)
            cg = c_ref[:, pl.ds(goff, STATE)].astype(BF16)
            bg = b_ref[:, pl.ds(goff, STATE)].astype(BF16)
            gm = _dot_nt(cg, bg)
            for p in range(HEADS_PER_GROUP // 2):
                pr = gi * (HEADS_PER_GROUP // 2) + p
                off = pl.multiple_of(pr * LANES, LANES)
                xd = x_ref[:, pl.ds(off, LANES)] * dt_ref[:, pl.ds(off, LANES)]
                ms = []
                cols = []
                for hl in range(2):
                    hh = 2 * pr + hl
                    col = cum_ref[:, pl.ds(pl.multiple_of(hh * LANES, LANES), LANES)]
                    row = cumt_ref[pl.ds(hh, 1), :]
                    lm = jnp.where(mask, jnp.exp(jnp.minimum(col - row, 0.0)), 0.0)
                    ms.append((gm * lm).astype(BF16))
                    cols.append(col)
                y = _dot(ms[0], jnp.where(left, xd, 0.0).astype(BF16)) + _dot(ms[1], jnp.where(left, 0.0, xd).astype(BF16))
                ce = jnp.where(left, cols[0], cols[1])
                sprev = st[pr]
                sp_ref[pr] = sprev
                y = y + jnp.exp(ce) * _dot(cg, sprev.astype(BF16))
                y_ref[:, pl.ds(off, LANES)] = y
                tot = jnp.where(rev, ce[0:1, :], ce[q - 1:q, :])
                v = (xd * jnp.exp(tot - ce)).astype(BF16)
                st[pr] = jnp.exp(tot) * sprev + _dot_tn(bg, v)
            return carry

        lax.fori_loop(0, g, group, 0)

    in_specs = [
        pl.BlockSpec((q, ds), lambda d, i: (cidx(d, i), 0)),
        pl.BlockSpec((q, nb), lambda d, i: (cidx(d, i), ds // nb)),
        pl.BlockSpec((q, nb), lambda d, i: (cidx(d, i), ds // nb + 1)),
        pl.BlockSpec((None, q, ds), lambda d, i: (d, cidx(d, i), 0)),
        pl.BlockSpec((None, q, h * LANES), lambda d, i: (d, cidx(d, i), 0)),
        pl.BlockSpec((None, h, q), lambda d, i: (d, 0, cidx(d, i))),
    ]
    out_specs = [
        pl.BlockSpec((None, q, ds), lambda d, i: (d, cidx(d, i), 0)),
        pl.BlockSpec((None, None, npair, STATE, LANES), lambda d, i: (d, cidx(d, i), 0, 0, 0)),
    ]
    out_shape = [jax.ShapeDtypeStruct((2, s, ds), F32), jax.ShapeDtypeStruct((2, nc, npair, STATE, LANES), F32)]
    return pl.pallas_call(
        body, name="ssd_fwd", grid=(2, nc), in_specs=in_specs, out_specs=out_specs, out_shape=out_shape,
        scratch_shapes=[pltpu.VMEM((npair, STATE, LANES), F32)],
        compiler_params=_params(("arbitrary", "arbitrary")),
    )(xbc, xbc, xbc, dt_e, cum_e, cum_t)


def _ssd_bwd(xbc, dt_e, cum_e, cum_t, dt_t, a_col, dy, sp, s, h, g):
    q, nc, ds, nb = _ssd_specs(s, h, g)
    npair = h // 2

    def cidx(d, i):
        return jnp.where(d == 0, nc - 1 - i, i)

    def body(x_ref, b_ref, c_ref, dt_ref, cum_ref, cumt_ref, dtt_ref, a_ref, dy_ref, sp_ref,
             dx_ref, db_ref, dc_ref, ddt_ref, da_ref, dst, rowp):
        d = pl.program_id(0)
        i = pl.program_id(1)

        @pl.when(i == 0)
        def _():
            dst[...] = jnp.zeros_like(dst)
            da_ref[...] = jnp.zeros_like(da_ref)

        rev = d == 1
        ii = lax.broadcasted_iota(jnp.int32, (q, q), 0)
        jj = lax.broadcasted_iota(jnp.int32, (q, q), 1)
        sgn = jnp.where(rev, -1, 1)
        mask = (jj - ii) * sgn <= 0
        lane = lax.broadcasted_iota(jnp.int32, (q, LANES), 1)
        left = lane < HEAD_DIM
        rowp[...] = jnp.zeros_like(rowp)

        def group(gi, carry):
            acc_dcum, acc_tot, acc_dxx = carry
            goff = pl.multiple_of(gi * STATE, STATE)
            cg = c_ref[:, pl.ds(goff, STATE)].astype(BF16)
            bg = b_ref[:, pl.ds(goff, STATE)].astype(BF16)
            gm = _dot_nt(cg, bg)
            dgm = jnp.zeros((q, q), F32)
            dcg = jnp.zeros((q, STATE), F32)
            dbg = jnp.zeros((q, STATE), F32)
            for p in range(HEADS_PER_GROUP // 2):
                pr = gi * (HEADS_PER_GROUP // 2) + p
                off = pl.multiple_of(pr * LANES, LANES)
                xv = x_ref[:, pl.ds(off, LANES)]
                dte = dt_ref[:, pl.ds(off, LANES)]
                xd = xv * dte
                xdb = xd.astype(BF16)
                dyv = dy_ref[:, pl.ds(off, LANES)]
                sprev = sp_ref[pr]
                sprevb = sprev.astype(BF16)
                dsn = dst[pr]
                dsnb = dsn.astype(BF16)
                cols = [cum_ref[:, pl.ds(pl.multiple_of((2 * pr + hl) * LANES, LANES), LANES)] for hl in range(2)]
                ce = jnp.where(left, cols[0], cols[1])
                tot = jnp.where(rev, ce[0:1, :], ce[q - 1:q, :])
                et = jnp.exp(tot)
                r = jnp.exp(tot - ce)
                e = jnp.exp(ce)
                yoff = e * _dot(cg, sprevb)
                dz = (e * dyv).astype(BF16)
                dcg = dcg + _dot_nt(dz, sprevb)
                dsprev = _dot_tn(cg, dz) + et * dsn
                f1 = dyv * yoff
                v = (xd * r).astype(BF16)
                dbg = dbg + _dot_nt(v, dsnb)
                dv = _dot(bg, dsnb)
                dxd = dv * r
                tt = dv * xd * r
                wt = dsn * sprev * et
                for hl in range(2):
                    hh = 2 * pr + hl
                    hm = left if hl == 0 else jnp.logical_not(left)
                    row = cumt_ref[pl.ds(hh, 1), :]
                    lm = jnp.where(mask, jnp.exp(jnp.minimum(cols[hl] - row, 0.0)), 0.0)
                    mf = gm * lm
                    dym = jnp.where(hm, dyv, 0.0).astype(BF16)
                    dm = _dot_nt(dym, xdb)
                    dxd = dxd + _dot_tn(mf.astype(BF16), dym)
                    dgm = dgm + dm * lm
                    em = dm * mf
                    rowp[pl.ds(hh, 1), :] = rowp[pl.ds(hh, 1), :] - jnp.sum(em, axis=0, keepdims=True)
                    colq = (jnp.sum(em, axis=1, keepdims=True)
                            + jnp.sum(jnp.where(hm, f1 - tt, 0.0), axis=1, keepdims=True))
                    acc_dcum = jnp.where(lane == hh, colq, acc_dcum)
                    totq = jnp.sum(jnp.sum(jnp.where(hm, tt + wt, 0.0), axis=1, keepdims=True), axis=0, keepdims=True)
                    acc_tot = jnp.where(lane == hh, totq, acc_tot)
                dxx = dxd * xv
                for hl in range(2):
                    hh = 2 * pr + hl
                    hm = left if hl == 0 else jnp.logical_not(left)
                    acc_dxx = jnp.where(lane == hh, jnp.sum(jnp.where(hm, dxx, 0.0), axis=1, keepdims=True), acc_dxx)
                dx_ref[:, pl.ds(off, LANES)] = dxd * dte
                dst[pr] = dsprev
            dgb = dgm.astype(BF16)
            dc_ref[:, pl.ds(goff, STATE)] = dcg + _dot(dgb, bg)
            db_ref[:, pl.ds(goff, STATE)] = dbg + _dot_tn(dgb, cg)
            return acc_dcum, acc_tot, acc_dxx

        zero = jnp.zeros((q, LANES), F32)
        acc_dcum, acc_tot, acc_dxx = lax.fori_loop(0, g, group, (zero, zero, zero))
        dcum_t = rowp[...] + jnp.transpose(acc_dcum)[:h]
        rmat = ((ii - jj) * sgn >= 0).astype(F32).astype(BF16)
        da_t = _dot3_r(dcum_t, rmat) + jnp.transpose(acc_tot)[:h]
        ddt_ref[...] = da_t * a_ref[...] + jnp.transpose(acc_dxx)[:h]
        da_ref[...] += da_t * dtt_ref[...]

    in_specs = [
        pl.BlockSpec((q, ds), lambda d, i: (cidx(d, i), 0)),
        pl.BlockSpec((q, nb), lambda d, i: (cidx(d, i), ds // nb)),
        pl.BlockSpec((q, nb), lambda d, i: (cidx(d, i), ds // nb + 1)),
        pl.BlockSpec((None, q, ds), lambda d, i: (d, cidx(d, i), 0)),
        pl.BlockSpec((None, q, h * LANES), lambda d, i: (d, cidx(d, i), 0)),
        pl.BlockSpec((None, h, q), lambda d, i: (d, 0, cidx(d, i))),
        pl.BlockSpec((None, h, q), lambda d, i: (d, 0, cidx(d, i))),
        pl.BlockSpec((None, h, LANES), lambda d, i: (d, 0, 0)),
        pl.BlockSpec((q, ds), lambda d, i: (cidx(d, i), 0)),
        pl.BlockSpec((None, None, npair, STATE, LANES), lambda d, i: (d, cidx(d, i), 0, 0, 0)),
    ]
    out_specs = [
        pl.BlockSpec((None, q, ds), lambda d, i: (d, cidx(d, i), 0)),
        pl.BlockSpec((None, q, nb), lambda d, i: (d, cidx(d, i), 0)),
        pl.BlockSpec((None, q, nb), lambda d, i: (d, cidx(d, i), 0)),
        pl.BlockSpec((None, h, q), lambda d, i: (d, 0, cidx(d, i))),
        pl.BlockSpec((None, h, LANES), lambda d, i: (d, 0, 0)),
    ]
    out_shape = [jax.ShapeDtypeStruct((2, s, ds), F32), jax.ShapeDtypeStruct((2, s, nb), F32),
                 jax.ShapeDtypeStruct((2, s, nb), F32), jax.ShapeDtypeStruct((2, h, s), F32),
                 jax.ShapeDtypeStruct((2, h, LANES), F32)]
    return pl.pallas_call(
        body, name="ssd_bwd", grid=(2, nc), in_specs=in_specs, out_specs=out_specs, out_shape=out_shape,
        scratch_shapes=[pltpu.VMEM((npair, STATE, LANES), F32), pltpu.VMEM((h, q), F32)],
        compiler_params=_params(("arbitrary", "arbitrary")),
    )(xbc, xbc, xbc, dt_e, cum_e, cum_t, dt_t, a_col, dy, sp)


def _dt_bwd(ddt, proj, bias_row, s, u, h):
    tr = _tile_rows(s, 4 * LANES)
    dtblk = (6 * u) // LANES

    def body(d_ref, raw_ref, b_ref, o_ref, db_ref):
        lane = lax.broadcasted_iota(jnp.int32, (tr, LANES), 1)
        v = jnp.where(lane < 2 * h, d_ref[...] * _sigmoid(raw_ref[...] + b_ref[...]), 0.0)
        o_ref[...] = v.astype(BF16)
        _acc_rows(db_ref, v)

    return _rowcall(body, "dt_bwd", s, tr, [_row(tr, LANES), _row(tr, LANES, dtblk), _full((1, LANES))],
                    [_row(tr, LANES), _full((1, LANES))],
                    [jax.ShapeDtypeStruct((s, LANES), BF16), jax.ShapeDtypeStruct((1, LANES), F32)],
                    (ddt, proj, bias_row))


def _group_rms(v, gw):
    outs, facs = [], []
    for k in range(v.shape[1] // gw):
        blk = v[:, k * gw:(k + 1) * gw]
        f = lax.rsqrt(jnp.mean(blk * blk, axis=-1, keepdims=True) + RMS_EPS)
        outs.append(blk * f)
        facs.append(jnp.broadcast_to(f, blk.shape))
    return jnp.concatenate(outs, axis=1), jnp.concatenate(facs, axis=1)


def _group_rms_bwd(dn, n, fac, gw):
    outs = []
    for k in range(n.shape[1] // gw):
        sl = slice(k * gw, (k + 1) * gw)
        outs.append(fac[:, sl] * (dn[:, sl] - n[:, sl] * jnp.mean(dn[:, sl] * n[:, sl], axis=-1, keepdims=True)))
    return jnp.concatenate(outs, axis=1)


def _gate_norm_fwd(y2, xbc, proj, d_e, norm_w, s, u, g):
    tr = _tile_rows(s, u)
    gw = u // g

    def body(y_ref, x_ref, z_ref, d_ref, w_ref, o_ref):
        ys = y_ref[0] + y_ref[1] + d_ref[...] * x_ref[...]
        n, _ = _group_rms(ys * _silu(z_ref[...]), gw)
        o_ref[...] = (n * w_ref[...]).astype(BF16)

    return _rowcall(body, "gate_norm_fwd", s, tr,
                    [pl.BlockSpec((2, tr, u), lambda i: (0, i, 0)), _row(tr, u), _row(tr, u), _full((1, u)), _full((1, u))],
                    _row(tr, u), jax.ShapeDtypeStruct((s, u), BF16), (y2, xbc, proj, d_e, norm_w))


def _gate_norm_bwd(dymix, y2, xbc, proj, d_e, norm_w, s, u, g):
    tr = _tile_rows(s, u)
    gw = u // g

    def body(dy_ref, y_ref, x_ref, z_ref, d_ref, w_ref, dys_ref, dz_ref, dxs_ref, dw_ref, dd_ref):
        xv = x_ref[...]
        zv = z_ref[...]
        ys = y_ref[0] + y_ref[1] + d_ref[...] * xv
        sz = _silu(zv)
        n, fac = _group_rms(ys * sz, gw)
        dout = dy_ref[...]
        _acc_rows(dw_ref, dout * n)
        dyg = _group_rms_bwd(dout * w_ref[...], n, fac, gw)
        dys = dyg * sz
        dys_ref[...] = dys
        dz_ref[...] = (dyg * ys * _dsilu(zv)).astype(BF16)
        dxs_ref[...] = dys * d_ref[...]
        _acc_rows(dd_ref, dys * xv)

    v = _full((1, u))
    return _rowcall(body, "gate_norm_bwd", s, tr,
                    [_row(tr, u), pl.BlockSpec((2, tr, u), lambda i: (0, i, 0)), _row(tr, u), _row(tr, u), v, v],
                    [_row(tr, u), _row(tr, u), _row(tr, u), v, v],
                    [jax.ShapeDtypeStruct((s, u), F32), jax.ShapeDtypeStruct((s, u), BF16),
                     jax.ShapeDtypeStruct((s, u), F32), jax.ShapeDtypeStruct((1, u), F32), jax.ShapeDtypeStruct((1, u), F32)],
                    (dymix, y2, xbc, proj, d_e, norm_w))


def _shortconv_fwd(proj, conv_w, norm_w, s, u):
    tr = _tile_rows(s, u)
    half = SC_CONV // 2

    def body(hp, hc, hn, b_ref, cp, cc, cn, cw_ref, w_ref, o_ref):
        t = _ext(hp, hc, hn, s, tr) * _ext(cp, cc, cn, s, tr)
        wv = cw_ref[...]
        acc = jnp.zeros_like(t)
        for k in range(SC_CONV):
            acc = acc + _shift(t, k - half) * wv[k:k + 1, :]
        n, _ = _group_rms(b_ref[...] * acc[HALO:HALO + tr], SC_GROUP_WIDTH)
        o_ref[...] = (n * w_ref[...]).astype(BF16)

    in_specs = _halo_specs(s, tr, u, 3) + [_row(tr, u, 4)] + _halo_specs(s, tr, u, 5) + [_full((SC_CONV, u)), _full((1, u))]
    return _rowcall(body, "shortconv_fwd", s, tr, in_specs, _row(tr, u), jax.ShapeDtypeStruct((s, u), BF16),
                    (proj,) * 7 + (conv_w, norm_w))


def _shortconv_bwd(dymix, proj, conv_w, norm_w, s, u):
    tr = _tile_rows(s, u)
    half = SC_CONV // 2

    def body(dp, dc_, dn, hp, hc, hn, bp, bc, bn, cp, cc, cn, cw_ref, w_ref, dh_ref, db_ref, dcc_ref, dcw_ref, dw_ref):
        dout = _ext(dp, dc_, dn, s, tr)
        hv = _ext(hp, hc, hn, s, tr)
        bv = _ext(bp, bc, bn, s, tr)
        cv = _ext(cp, cc, cn, s, tr)
        t = hv * cv
        wv = cw_ref[...]
        acc = jnp.zeros_like(t)
        for k in range(SC_CONV):
            acc = acc + _shift(t, k - half) * wv[k:k + 1, :]
        n, fac = _group_rms(bv * acc, SC_GROUP_WIDTH)
        cur = slice(HALO, HALO + tr)
        _acc_rows(dw_ref, (dout * n)[cur])
        dyv = _group_rms_bwd(dout * w_ref[...], n, fac, SC_GROUP_WIDTH)
        db_ref[...] = (dyv * acc)[cur].astype(BF16)
        dv = dyv * bv
        dt = jnp.zeros_like(t)
        rows = []
        for k in range(SC_CONV):
            dt = dt + _shift(dv, half - k) * wv[k:k + 1, :]
            rows.append(jnp.sum((_shift(t, k - half) * dv)[cur], axis=0, keepdims=True))
        dh_ref[...] = (dt * cv)[cur].astype(BF16)
        dcc_ref[...] = (dt * hv)[cur].astype(BF16)
        dwv = jnp.concatenate(rows + [jnp.zeros((SUBLANES - SC_CONV, u), F32)], axis=0)
        first = pl.program_id(0) == 0

        @pl.when(first)
        def _():
            dcw_ref[...] = dwv

        @pl.when(jnp.logical_not(first))
        def _():
            dcw_ref[...] += dwv

    in_specs = (_halo_specs(s, tr, u, 1) + _halo_specs(s, tr, u, 3) + _halo_specs(s, tr, u, 4) + _halo_specs(s, tr, u, 5)
                + [_full((SC_CONV, u)), _full((1, u))])
    return _rowcall(body, "shortconv_bwd", s, tr, in_specs,
                    [_row(tr, u)] * 3 + [_full((SUBLANES, u)), _full((1, u))],
                    [jax.ShapeDtypeStruct((s, u), BF16)] * 3
                    + [jax.ShapeDtypeStruct((SUBLANES, u), F32), jax.ShapeDtypeStruct((1, u), F32)],
                    (dymix,) * 3 + (proj,) * 9 + (conv_w, norm_w))


def _adam_math(w, g, m, v):
    m2 = ADAM_B1 * m + (1.0 - ADAM_B1) * g
    v2 = ADAM_B2 * v + (1.0 - ADAM_B2) * (g * g)
    m_hat = m2 / (1.0 - ADAM_B1 ** ADAM_STEP)
    v_hat = v2 / (1.0 - ADAM_B2 ** ADAM_STEP)
    delta = -ADAM_LR * (m_hat / (jnp.sqrt(v_hat) + ADAM_EPS) + ADAM_WD * w)
    return delta, m2, v2


def _adam_rows(r, c):
    return _pick(r, max(SUBLANES, (1 << 20) // (4 * c)), SUBLANES)


def _adam(w, g, m, v, *, name):
    r, c = w.shape
    tr = _adam_rows(r, c)

    def body(w_ref, g_ref, m_ref, v_ref, d_ref, m2_ref, v2_ref):
        d_ref[...], m2_ref[...], v2_ref[...] = _adam_math(w_ref[...], g_ref[...], m_ref[...], v_ref[...])

    return _rowcall(body, name, r, tr, [_row(tr, c)] * 4, [_row(tr, c)] * 3,
                    [jax.ShapeDtypeStruct((r, c), F32)] * 3, (w, g, m, v))


def _adam_outer(w, a_t, bmat, m, v, *, name):
    r, c = w.shape
    tr = _adam_rows(r, c)
    kk = a_t.shape[1]

    def body(w_ref, a_ref, b_ref, m_ref, v_ref, g_ref, d_ref, m2_ref, v2_ref):
        g = _dot(a_ref[...].astype(BF16), b_ref[...].astype(BF16))
        g_ref[...] = g
        d_ref[...], m2_ref[...], v2_ref[...] = _adam_math(w_ref[...], g, m_ref[...], v_ref[...])

    return _rowcall(body, name, r, tr, [_row(tr, c), _row(tr, kk), _full((kk, c)), _row(tr, c), _row(tr, c)],
                    [_row(tr, c)] * 4, [jax.ShapeDtypeStruct((r, c), F32)] * 4, (w, a_t, bmat, m, v))


ANY = pl.BlockSpec(memory_space=pl.ANY)
VMEM_WHOLE = pl.BlockSpec(memory_space=pltpu.VMEM)


def _place():
    x, y, c = lax.axis_index("x"), lax.axis_index("y"), lax.axis_index("c")
    return x, y, c


DMA_CHUNKS = 8


def _n_chunks(rows):
    n = DMA_CHUNKS
    while n > 1 and rows % (16 * n):
        n //= 2
    return n


def _allgather_small(v, *, name):
    m_per, n = v.shape

    def body(x_ref, out_ref, send_sems, recv_sems, local_sem):
        x, y, c = _place()
        me, sibling = (x, y, c), (x, y, 1 - c)
        chips = [(1 - x, y), (x, 1 - y), (1 - x, 1 - y)]

        def rows(px, py, pc):
            return out_ref.at[pl.ds((4 * px + 2 * py + pc) * m_per, m_per), :]

        def copy(k, block, to, src=None):
            return pltpu.make_async_remote_copy(
                src_ref=rows(*block) if src is None else src, dst_ref=rows(*block),
                send_sem=send_sems.at[k], recv_sem=recv_sems.at[k], device_id=to, device_id_type=MESH)

        mine = pltpu.make_async_copy(x_ref, rows(*me), local_sem)
        mine.start()
        first = [copy(0, me, sibling, src=x_ref)]
        first += [copy(1 + j, me, (*chip, c), src=x_ref) for j, chip in enumerate(chips)]
        for cp in first:
            cp.start()
        passed = [copy(4 + j, (*chip, c), sibling) for j, chip in enumerate(chips)]
        for j, chip in enumerate(chips):
            copy(1 + j, (*chip, c), me).wait_recv()
            passed[j].start()
        copy(0, sibling, me).wait_recv()
        for j, chip in enumerate(chips):
            copy(4 + j, (*chip, 1 - c), me).wait_recv()
        for cp in first + passed:
            cp.wait_send()
        mine.wait()

    return pl.pallas_call(
        body, name=name, out_shape=jax.ShapeDtypeStruct((N_DEV * m_per, n), v.dtype),
        in_specs=[VMEM_WHOLE], out_specs=VMEM_WHOLE,
        scratch_shapes=[pltpu.SemaphoreType.DMA((7,)), pltpu.SemaphoreType.DMA((7,)), pltpu.SemaphoreType.DMA],
        compiler_params=pltpu.CompilerParams(vmem_limit_bytes=VMEM_LIMIT),
    )(v)


def _place_ids():
    x, y, c = _place()
    return jnp.stack([2 * x + y, c]).astype(jnp.int32)


def _cast_into_block(wl, ids, *, name):
    r, c_ = wl.shape
    tr = _tile_rows(r, c_, 16)

    def body(ids_ref, w_ref, o_ref):
        o_ref[...] = w_ref[...].astype(BF16)

    grid_spec = pltpu.PrefetchScalarGridSpec(
        num_scalar_prefetch=1, grid=(r // tr,),
        in_specs=[pl.BlockSpec((tr, c_), lambda i, s: (i, 0))],
        out_specs=pl.BlockSpec((None, tr, c_), lambda i, s: (s[0], i, 0)))
    return pl.pallas_call(body, name=name, grid_spec=grid_spec, out_shape=jax.ShapeDtypeStruct((N_CHIPS, r, c_), BF16),
                          compiler_params=_params(("arbitrary",)))(ids, wl)


def _gather_weight(buf, *, name):
    _, r, c_ = buf.shape
    half = r // 2
    nch = _n_chunks(half)
    rows = half // nch

    def body(in_ref, out_ref, send_sems, recv_sems):
        del in_ref
        x, y, c = _place()
        me, sibling = (x, y, c), (x, y, 1 - c)
        chips = [(1 - x, y), (x, 1 - y), (1 - x, 1 - y)]

        def blk(px, py, pc, i):
            return out_ref.at[2 * px + py, pl.ds(pc * half + i * rows, rows), :]

        def copy(k, i, block, to):
            return pltpu.make_async_remote_copy(
                src_ref=blk(*block, i), dst_ref=blk(*block, i),
                send_sem=send_sems.at[k * nch + i], recv_sem=recv_sems.at[k * nch + i], device_id=to, device_id_type=MESH)

        first = [copy(j, i, me, (*chip, c)) for i in range(nch) for j, chip in enumerate(chips)]
        for cp in first:
            cp.start()
        passed = []
        for i in range(nch):
            for j, chip in enumerate(chips):
                copy(j, i, (*chip, c), me).wait_recv()
                passed.append(copy(3 + j, i, (*chip, c), sibling))
                passed[-1].start()
        for i in range(nch):
            for j, chip in enumerate(chips):
                copy(3 + j, i, (*chip, 1 - c), me).wait_recv()
        for cp in first + passed:
            cp.wait_send()

    return pl.pallas_call(
        body, name=name, out_shape=jax.ShapeDtypeStruct(buf.shape, buf.dtype),
        in_specs=[ANY], out_specs=ANY, input_output_aliases={0: 0},
        scratch_shapes=[pltpu.SemaphoreType.DMA((6 * nch,)), pltpu.SemaphoreType.DMA((6 * nch,))],
    )(buf)


def _scatter_grads(gfull, *, name):
    _, r, c_ = gfull.shape
    half = r // 2
    nch = _n_chunks(half)
    rows = half // nch

    def body(g_ref, recv_ref, send_sems, recv_sems):
        x, y, c = _place()

        def copy(q, i):
            tx = 1 - x if q & 4 else x
            ty = 1 - y if q & 2 else y
            tc = 1 - c if q & 1 else c
            return pltpu.make_async_remote_copy(
                src_ref=g_ref.at[2 * tx + ty, pl.ds(tc * half + i * rows, rows), :],
                dst_ref=recv_ref.at[q - 1, pl.ds(i * rows, rows), :],
                send_sem=send_sems.at[(q - 1) * nch + i], recv_sem=recv_sems.at[(q - 1) * nch + i],
                device_id=(tx, ty, tc), device_id_type=MESH)

        copies = [copy(q, i) for i in range(nch) for q in range(N_DEV - 1, 0, -1)]
        for cp in copies:
            cp.start()
        for cp in copies:
            cp.wait_recv()
        for cp in copies:
            cp.wait_send()

    return pl.pallas_call(
        body, name=name, out_shape=jax.ShapeDtypeStruct((N_DEV - 1, half, c_), gfull.dtype),
        in_specs=[ANY], out_specs=ANY,
        scratch_shapes=[pltpu.SemaphoreType.DMA((7 * nch,)), pltpu.SemaphoreType.DMA((7 * nch,))],
    )(gfull)


def _sum_into_half(gfull, recv, ids, *, name):
    _, r, c_ = gfull.shape
    n, half, _ = recv.shape
    tr = _pick(half, max(16, (1 << 19) // (2 * c_)), 16)
    per = half // tr

    def body(ids_ref, g_ref, r_ref, o_ref):
        acc = g_ref[...].astype(F32)
        for k in range(n):
            acc = acc + r_ref[k].astype(F32)
        o_ref[...] = acc

    grid_spec = pltpu.PrefetchScalarGridSpec(
        num_scalar_prefetch=1, grid=(per,),
        in_specs=[pl.BlockSpec((None, tr, c_), lambda i, s: (s[0], s[1] * per + i, 0)),
                  pl.BlockSpec((n, tr, c_), lambda i, s: (0, i, 0))],
        out_specs=pl.BlockSpec((tr, c_), lambda i, s: (s[1] * per + i, 0)))
    return pl.pallas_call(body, name=name, grid_spec=grid_spec, out_shape=jax.ShapeDtypeStruct((r, c_), F32),
                          compiler_params=_params(("arbitrary",)))(ids, gfull, recv)


def _sum_slots(recv, *, name):
    n, r, c_ = recv.shape
    tr = _pick(r, max(16, (1 << 19) // (2 * c_)), 16)

    def body(r_ref, o_ref):
        acc = r_ref[0].astype(F32)
        for k in range(1, n):
            acc = acc + r_ref[k].astype(F32)
        o_ref[...] = acc

    return _rowcall(body, name, r, tr, [pl.BlockSpec((n, tr, c_), lambda i: (0, i, 0))], _row(tr, c_),
                    jax.ShapeDtypeStruct((r, c_), F32), (recv,))


def _swap_halves(buf, *, name):
    r, c_ = buf.shape
    half = r // 2
    nch = _n_chunks(half)
    rows = half // nch

    def body(in_ref, out_ref, send_sems, recv_sems):
        del in_ref
        x, y, c = _place()

        def copy(i, pc):
            part = out_ref.at[pl.ds(pc * half + i * rows, rows), :]
            return pltpu.make_async_remote_copy(
                src_ref=part, dst_ref=part, send_sem=send_sems.at[i], recv_sem=recv_sems.at[i],
                device_id=(x, y, 1 - c), device_id_type=MESH)

        sends = [copy(i, c) for i in range(nch)]
        for cp in sends:
            cp.start()
        for i in range(nch):
            copy(i, 1 - c).wait_recv()
        for cp in sends:
            cp.wait_send()

    return pl.pallas_call(
        body, name=name, out_shape=jax.ShapeDtypeStruct(buf.shape, buf.dtype),
        in_specs=[ANY], out_specs=ANY, input_output_aliases={0: 0},
        scratch_shapes=[pltpu.SemaphoreType.DMA((nch,)), pltpu.SemaphoreType.DMA((nch,))],
    )(buf)


def _reduce_to_owner(gfull, ids, *, name):
    recv = _scatter_grads(gfull, name=name + "_scatter")
    return _swap_halves(_sum_into_half(gfull, recv, ids, name=name + "_sum"), name=name + "_swap")


PACK_ROWS = 16


def _pack(parts):
    flat = [p.reshape(-1).astype(F32) for p in parts]
    n = sum(f.shape[0] for f in flat)
    unit = PACK_ROWS * LANES
    total = -(-n // unit) * unit
    if total > n:
        flat.append(jnp.zeros((total - n,), F32))
    where, off = [], 0
    for p in parts:
        where.append((off, p.shape))
        off += p.size
    return jnp.concatenate(flat).reshape(total // LANES, LANES), where


def _unpack(flat, where):
    v = flat.reshape(-1)
    return [v[off:off + _size(shape)].reshape(shape) for off, shape in where]


def _size(shape):
    n = 1
    for d in shape:
        n *= d
    return n


def _sample_step(x, target, mods, w_in_p, w_out, w_up_blk, w_down, sp):
    s, d = x.shape
    u = d // 2
    h = u // HEAD_DIM
    g = h // HEADS_PER_GROUP
    pw = w_in_p.shape[1]
    shift1, scale1, gate1, shift2, scale2, gate2 = mods

    a_f = -jnp.exp(sp["ssm_a_log_f"].reshape(-1))
    a_b = -jnp.exp(sp["ssm_a_log_b"].reshape(-1))
    pad_l = LANES - 2 * h
    a_row = jnp.pad(jnp.concatenate([a_f, a_b]), (0, pad_l)).reshape(1, LANES)
    bias_row = jnp.pad(jnp.concatenate([sp["ssm_dt_bias_f"].reshape(-1), sp["ssm_dt_bias_b"].reshape(-1)]),
                       (0, pad_l)).reshape(1, LANES)
    a_col = jnp.broadcast_to(jnp.stack([a_f, a_b])[:, :, None], (2, h, LANES))
    d_e = jnp.repeat(sp["ssm_d"].reshape(-1), HEAD_DIM).reshape(1, u)
    conv_w, conv_b = sp["ssm_conv_w"], sp["ssm_conv_b"].reshape(1, 2 * u)
    sc_conv_w = sp["sc_conv_w"]
    ssm_norm_w, sc_norm_w = sp["ssm_norm_w"].reshape(1, u), sp["sc_norm_w"].reshape(1, u)
    ln1_g, ln1_b = sp["ln1_g"].reshape(1, d), sp["ln1_b"].reshape(1, d)
    ln2_g, ln2_b = sp["ln2_g"].reshape(1, d), sp["ln2_b"].reshape(1, d)

    h1 = _modulate(x, scale1, shift1, name="modulate1")
    proj = _matmul(h1, w_in_p, name="mm_proj", tn=1280)
    xbc = _conv_silu_fwd(proj, conv_w, conv_b, s, u)
    dt, cum, dt_e, cum_e = _dt_prep(proj, bias_row, a_row, s, u, h)
    cum_t = jnp.stack([cum[:, :h].T, cum[:, h:2 * h].T])
    dt_t = jnp.stack([dt[:, :h].T, dt[:, h:2 * h].T])
    y2, states = _ssd_fwd(xbc, dt_e, cum_e, cum_t, s, h, g)
    y_ssm = _gate_norm_fwd(y2, xbc, proj, d_e, ssm_norm_w, s, u, g)
    y_sc = _shortconv_fwd(proj, sc_conv_w, sc_norm_w, s, u)
    ymix = jnp.concatenate([y_ssm, y_sc], axis=1)
    mix = _matmul(ymix, w_out, name="mm_mix")
    x1, h2 = _ln1_fwd(x, mix, gate1, ln1_g, ln1_b, scale2, shift2)
    up = _matmul(h2, w_up_blk, name="mm_up", b_blocks=N_CHIPS)
    ff = _relu2(up)
    f = _matmul(ff, w_down, name="mm_down")
    df, dr2, loss, dg2, db2, dgate2 = _ln2_loss_bwd(x1, f, target, gate2, ln2_g, ln2_b)

    gw_down = _matmul(ff, df, name="mm_gw_down", ta=True, out_dtype=BF16)
    dff = _matmul(df, w_down, name="mm_dff", tb=True)
    du = _relu2_bwd(dff, up)
    gw_up = _matmul(h2, du, name="mm_gw_up", ta=True, out_dtype=BF16, out_blocks=N_CHIPS)
    dh2 = _matmul(du, w_up_blk, name="mm_dh2", tb=True, b_blocks=N_CHIPS)
    dmix, dxa, dscale2, dshift2, dg1, db1, dgate1 = _ln1_bwd(dh2, dr2, x1, x, mix, scale2, gate1, ln1_g)
    gw_out = _matmul(ymix, dmix, name="mm_gw_out", ta=True, out_dtype=BF16)
    dymix = _matmul(dmix, w_out, name="mm_dymix", tb=True)
    dys, dz, dxs, dnw, dd_e = _gate_norm_bwd(dymix, y2, xbc, proj, d_e, ssm_norm_w, s, u, g)
    dx2, dbb, dcc, ddt_t, da = _ssd_bwd(xbc, dt_e, cum_e, cum_t, dt_t, a_col, dys, states, s, h, g)
    dxbc = jnp.concatenate([dx2[0] + dx2[1] + dxs, dbb[0] + dbb[1], dcc[0] + dcc[1]], axis=1)
    du_xbc, dcw, dcb = _conv_silu_bwd(proj, dxbc, conv_w, conv_b, s, u)
    ddt = jnp.pad(jnp.concatenate([ddt_t[0].T, ddt_t[1].T], axis=1), ((0, 0), (0, pad_l)))
    ddt_raw, dbias = _dt_bwd(ddt, proj, bias_row, s, u, h)
    dh_sc, db_sc, dc_sc, dscw, dscnw = _shortconv_bwd(dymix, proj, sc_conv_w, sc_norm_w, s, u)
    dproj = jnp.concatenate([dz, du_xbc, dh_sc, db_sc, dc_sc, ddt_raw,
                             jnp.zeros((s, pw - 6 * u - LANES), BF16)], axis=1)
    gw_in_p = _matmul(h1, dproj, name="mm_gw_in", ta=True, out_dtype=BF16, tn=1280)
    dh1 = _matmul(dproj, w_in_p, name="mm_dh1", tb=True, tk=1280)
    grad_x, dscale1, dshift1 = _dx_final(dxa, dh1, x, scale1)

    small = {
        "dmod": jnp.concatenate([dshift1, dscale1, dgate1, dshift2, dscale2, dgate2], axis=1),
        "ssm_conv_b": dcb,
        "ssm_dt_bias_f": dbias[0, :h], "ssm_dt_bias_b": dbias[0, h:2 * h],
        "ssm_a_log_f": jnp.sum(da[0], axis=1) * a_f, "ssm_a_log_b": jnp.sum(da[1], axis=1) * a_b,
        "ssm_d": jnp.sum(dd_e.reshape(h, HEAD_DIM), axis=1),
        "ssm_norm_w": dnw, "sc_norm_w": dscnw,
        "ln1_g": dg1, "ln1_b": db1, "ln2_g": dg2, "ln2_b": db2,
        "ssm_conv_w": dcw[:SSM_CONV], "sc_conv_w": dscw[:SC_CONV],
    }
    big = {"w_in_p": gw_in_p, "w_out": gw_out, "w_up": gw_up, "w_down": gw_down}
    return loss, grad_x, big, small


WEIGHTS = ['w_ada', 'b_ada', 'w_in', 'ssm_conv_w', 'ssm_conv_b', 'ssm_dt_bias_f', 'ssm_dt_bias_b', 'ssm_a_log_f',
           'ssm_a_log_b', 'ssm_d', 'ssm_norm_w', 'sc_conv_w', 'sc_norm_w', 'w_out', 'ln1_g', 'ln1_b', 'w_up', 'w_down',
           'ln2_g', 'ln2_b']
BIG = ('w_ada', 'w_in', 'w_out', 'w_up', 'w_down')
SMALL = tuple(n for n in WEIGHTS if n not in BIG)
SMALL_SHARDED = ('ssm_conv_w', 'sc_conv_w')


def _p_layout_width(u):
    return -(-(6 * u + LANES) // 512) * 512


def kernel(x, c, w_ada, b_ada, w_in, ssm_conv_w, ssm_conv_b, ssm_dt_bias_f, ssm_dt_bias_b, ssm_a_log_f, ssm_a_log_b, ssm_d, ssm_norm_w, sc_conv_w, sc_norm_w, w_out, ln1_g, ln1_b, w_up, w_down, ln2_g, ln2_b, loss_target, m_w_ada, m_b_ada, m_w_in, m_ssm_conv_w, m_ssm_conv_b, m_ssm_dt_bias_f, m_ssm_dt_bias_b, m_ssm_a_log_f, m_ssm_a_log_b, m_ssm_d, m_ssm_norm_w, m_sc_conv_w, m_sc_norm_w, m_w_out, m_ln1_g, m_ln1_b, m_w_up, m_w_down, m_ln2_g, m_ln2_b, v_w_ada, v_b_ada, v_w_in, v_ssm_conv_w, v_ssm_conv_b, v_ssm_dt_bias_f, v_ssm_dt_bias_b, v_ssm_a_log_f, v_ssm_a_log_b, v_ssm_d, v_ssm_norm_w, v_sc_conv_w, v_sc_norm_w, v_w_out, v_ln1_g, v_ln1_b, v_w_up, v_w_down, v_ln2_g, v_ln2_b):
    given = dict(locals())
    w = {n: given[n][0] for n in WEIGHTS}
    m = {n: given["m_" + n][0] for n in WEIGHTS}
    v = {n: given["v_" + n][0] for n in WEIGHTS}
    xs, tgt = x[0], loss_target[0]
    s, d = xs.shape
    u = d // 2
    h = u // HEAD_DIM
    nmod = N_MOD * d
    nmod_loc = nmod // N_CHIPS
    ax, ay, ac = lax.axis_index("x"), lax.axis_index("y"), lax.axis_index("c")
    chip = 2 * ax + ay
    me = 2 * chip + ac

    pay1, where1 = _pack([c[0], w["ssm_conv_w"], w["sc_conv_w"]])
    g1 = _allgather_small(pay1, name="ag_inputs").reshape(N_DEV, -1)
    per_dev = [_unpack(g1[k], where1) for k in range(N_DEV)]
    c_all = jnp.stack([p[0] for p in per_dev])
    ssm_conv_w_full = jnp.concatenate([per_dev[2 * k][1] for k in range(N_CHIPS)], axis=1)
    sc_conv_w_full = jnp.concatenate([per_dev[2 * k][2] for k in range(N_CHIPS)], axis=1)

    sc_all = _silu(c_all)
    sc16 = jnp.pad(sc_all, ((0, 16 - N_DEV), (0, 0)))
    b_loc = lax.dynamic_slice(w["b_ada"], (chip * nmod_loc,), (nmod_loc,))
    mod_loc = _matmul(sc16, w["w_ada"], name="mm_mod")[:N_DEV] + b_loc[None, :]
    pay2, where2 = _pack([mod_loc])
    g2 = _allgather_small(pay2, name="ag_mod").reshape(N_DEV, -1)
    mod_blocks = jnp.stack([_unpack(g2[2 * k], where2)[0] for k in range(N_CHIPS)])
    mod_mine = lax.dynamic_index_in_dim(mod_blocks, me, axis=1, keepdims=False).reshape(N_MOD, 1, d)
    mods = [mod_mine[k] for k in range(N_MOD)]

    din = w["w_in"].shape[1] * N_CHIPS
    ids = _place_ids()

    def gathered(n):
        return _gather_weight(_cast_into_block(w[n], ids, name="cast_" + n), name="gather_" + n)

    g_in = gathered("w_in")
    w_in_full = jnp.concatenate([g_in[k] for k in range(N_CHIPS)], axis=1)
    pw = _p_layout_width(u)
    w_in_p = jnp.concatenate([w_in_full[:, :3 * u], w_in_full[:, 3 * u + 2 * h:], w_in_full[:, 3 * u:3 * u + 2 * h],
                              jnp.zeros((d, pw - din), BF16)], axis=1)
    w_out_full = gathered("w_out").reshape(d, d)
    w_up_blk = gathered("w_up")
    dff_ = w["w_up"].shape[1] * N_CHIPS
    w_down_full = gathered("w_down").reshape(dff_, d)

    sp = {n: w[n] for n in SMALL}
    sp["ssm_conv_w"], sp["sc_conv_w"] = ssm_conv_w_full, sc_conv_w_full
    loss_loc, grad_x, big, small = _sample_step(xs, tgt, mods, w_in_p, w_out_full, w_up_blk, w_down_full, sp)

    small_names = ["dmod"] + [n for n in SMALL if n != "b_ada"]
    pay3, where3 = _pack([loss_loc] + [small[n] for n in small_names])
    g3 = _allgather_small(pay3, name="ag_small_grads")
    tot = _unpack(_sum_slots(g3.reshape(N_DEV, -1, LANES), name="sum_small_grads"), where3)
    loss = tot[0].reshape(())
    gsum = dict(zip(small_names, tot[1:]))
    dmod_all = jnp.stack([_unpack(g3.reshape(N_DEV, -1)[k], where3)[1].reshape(-1) for k in range(N_DEV)])

    grads = {}
    grads["b_ada"] = gsum["dmod"].reshape(-1)
    for n in SMALL:
        if n in SMALL_SHARDED:
            loc = w[n].shape[1]
            grads[n] = lax.dynamic_slice_in_dim(gsum[n], chip * loc, loc, axis=1)
        elif n != "b_ada":
            grads[n] = gsum[n].reshape(w[n].shape)

    gp = big["w_in_p"]
    g_in_full = jnp.concatenate([gp[:, :3 * u], gp[:, 6 * u:6 * u + 2 * h], gp[:, 3 * u:6 * u]], axis=1)
    g_in_blk = jnp.transpose(g_in_full.reshape(d, N_CHIPS, din // N_CHIPS), (1, 0, 2))
    grads["w_in"] = _reduce_to_owner(g_in_blk, ids, name="rs_w_in")
    grads["w_out"] = _reduce_to_owner(big["w_out"].reshape(N_CHIPS, d // N_CHIPS, d), ids, name="rs_w_out")
    grads["w_up"] = _reduce_to_owner(big["w_up"], ids, name="rs_w_up")
    grads["w_down"] = _reduce_to_owner(big["w_down"].reshape(N_CHIPS, dff_ // N_CHIPS, d), ids, name="rs_w_down")

    delta, new_m, new_v = {}, {}, {}
    dm_loc = lax.dynamic_slice_in_dim(dmod_all, chip * nmod_loc, nmod_loc, axis=1)
    grads["w_ada"], delta["w_ada"], new_m["w_ada"], new_v["w_ada"] = _adam_outer(
        w["w_ada"], sc16.T, jnp.pad(dm_loc, ((0, 16 - N_DEV), (0, 0))), m["w_ada"], v["w_ada"], name="adam_w_ada")
    for n in ("w_in", "w_out", "w_up", "w_down"):
        delta[n], new_m[n], new_v[n] = _adam(w[n], grads[n], m[n], v[n], name="adam_" + n)
    pw_, where_s = _pack([w[n] for n in SMALL])
    pg_, _ = _pack([grads[n] for n in SMALL])
    pm_, _ = _pack([m[n] for n in SMALL])
    pv_, _ = _pack([v[n] for n in SMALL])
    sd, sm, sv = _adam(pw_, pg_, pm_, pv_, name="adam_small")
    for n, a, b_, c_ in zip(SMALL, _unpack(sd, where_s), _unpack(sm, where_s), _unpack(sv, where_s)):
        delta[n], new_m[n], new_v[n] = a, b_, c_

    def lead(t):
        return t[None]

    return (loss, grad_x[None], *[lead(grads[n].reshape(w[n].shape)) for n in WEIGHTS],
            *[lead(delta[n]) for n in WEIGHTS], *[lead(new_m[n]) for n in WEIGHTS], *[lead(new_v[n]) for n in WEIGHTS])
```

```python
import functools

import jax
import jax.numpy as jnp
from jax import lax
from jax.experimental import pallas as pl
from jax.experimental.pallas import tpu as pltpu

F32 = jnp.float32
BF16 = jnp.bfloat16

CHUNK = 128
HEAD_DIM = 64
STATE = 128
HEADS_PER_GROUP = 4
SC_GROUP_WIDTH = 128
SSM_CONV = 5
SC_CONV = 3
N_MOD = 6
DEEPNORM_ALPHA = 2.0 ** 0.25
LN_EPS = 1e-5
RMS_EPS = 1e-5
ADAM_LR = 0.001
ADAM_B1 = 0.9
ADAM_B2 = 0.999
ADAM_EPS = 1e-08
ADAM_WD = 0.01
ADAM_STEP = 10
N_CHIPS = 4
N_DEV = 8
LANES = 128
SUBLANES = 8
HALO = 8
VMEM_LIMIT = 56 * 1024 * 1024
MESH = pl.DeviceIdType.MESH


def _params(sem=None):
    return pltpu.CompilerParams(dimension_semantics=sem, vmem_limit_bytes=VMEM_LIMIT)


def _pick(n, target, mult=LANES):
    best = None
    t = mult
    while t <= min(n, target):
        if n % t == 0:
            best = t
        t += mult
    return best if best is not None else n


ROW_TILE_BYTES = 1 << 20


def _tile_rows(s, width, mult=SUBLANES):
    return _pick(s, max(mult, ROW_TILE_BYTES // (4 * width)), mult)


def _sigmoid(v):
    return 1.0 / (1.0 + jnp.exp(-v))


def _silu(v):
    return v * _sigmoid(v)


def _dsilu(v):
    s = _sigmoid(v)
    return s * (1.0 + v * (1.0 - s))


def _softplus(v):
    e = jnp.exp(-jnp.abs(v))
    return jnp.maximum(v, 0.0) + jnp.where(e < 1e-4, e - 0.5 * e * e, jnp.log(1.0 + e))


def _dot(a, b):
    return jnp.dot(a, b, preferred_element_type=F32)


def _dot_nt(a, b):
    return lax.dot_general(a, b, (((1,), (1,)), ((), ())), preferred_element_type=F32)


def _dot_tn(a, b):
    return lax.dot_general(a, b, (((0,), (0,)), ((), ())), preferred_element_type=F32)


def _split3(v):
    hi = v.astype(BF16)
    r1 = v - hi.astype(F32)
    mid = r1.astype(BF16)
    lo = (r1 - mid.astype(F32)).astype(BF16)
    return hi, mid, lo


def _dot3_r(v, onehot):
    hi, mid, lo = _split3(v)
    return _dot(hi, onehot) + _dot(mid, onehot) + _dot(lo, onehot)


def _dot3_l(onehot, v):
    hi, mid, lo = _split3(v)
    return _dot(onehot, hi) + _dot(onehot, mid) + _dot(onehot, lo)


def _matmul(a, b, *, name, ta=False, tb=False, out_dtype=F32, b_blocks=1, out_blocks=1,
            tm=1024, tn=1024, tk=1024, comm=None):
    if ta:
        K, M = a.shape
    else:
        M, K = a.shape
    if b_blocks > 1:
        nb, r_, c_ = b.shape
        if tb:
            N, K2 = r_, c_ * nb
        else:
            K2, N = r_, c_ * nb
    else:
        if tb:
            N, K2 = b.shape
        else:
            K2, N = b.shape
    assert K == K2, (a.shape, b.shape, ta, tb)
    tm = _pick(M, tm)
    n_unit = N // b_blocks if (b_blocks > 1 and not tb) else N
    n_unit = min(n_unit, N // out_blocks)
    tn = _pick(n_unit, tn)
    k_unit = K // b_blocks if (b_blocks > 1 and tb) else K
    tk = _pick(k_unit, tk)
    gm, gn, gk = M // tm, N // tn, K // tk

    if ta:
        a_spec = pl.BlockSpec((tk, tm), lambda i, j, k: (k, i))
    else:
        a_spec = pl.BlockSpec((tm, tk), lambda i, j, k: (i, k))
    if b_blocks > 1 and not tb:
        per = (N // b_blocks) // tn
        b_spec = pl.BlockSpec((None, tk, tn), lambda i, j, k: (j // per, k, j % per))
    elif b_blocks > 1 and tb:
        per = (K // b_blocks) // tk
        b_spec = pl.BlockSpec((None, tn, tk), lambda i, j, k: (k // per, j, k % per))
    elif tb:
        b_spec = pl.BlockSpec((tn, tk), lambda i, j, k: (j, k))
    else:
        b_spec = pl.BlockSpec((tk, tn), lambda i, j, k: (k, j))
    if out_blocks > 1:
        per_o = (N // out_blocks) // tn
        o_spec = pl.BlockSpec((None, tm, tn), lambda i, j, k: (j // per_o, i, j % per_o))
        o_shape = jax.ShapeDtypeStruct((out_blocks, M, N // out_blocks), out_dtype)
    else:
        o_spec = pl.BlockSpec((tm, tn), lambda i, j, k: (i, j))
        o_shape = jax.ShapeDtypeStruct((M, N), out_dtype)

    def body(a_ref, b_ref, o_ref, acc):
        k = pl.program_id(2)
        av = a_ref[...].astype(BF16)
        bv = b_ref[...].astype(BF16)
        if ta and tb:
            raise NotImplementedError
        if ta:
            p = _dot_tn(av, bv)
        elif tb:
            p = _dot_nt(av, bv)
        else:
            p = _dot(av, bv)

        @pl.when(k == 0)
        def _():
            acc[...] = p

        @pl.when(k > 0)
        def _():
            acc[...] += p

        @pl.when(k == gk - 1)
        def _():
            o_ref[...] = acc[...].astype(out_dtype)

    if comm is not None:
        (out,), landed = _comm_call(body, name=name, grid=(gm, gn, gk), in_specs=[a_spec, b_spec], out_specs=[o_spec],
                                    out_shape=[o_shape], scratch_shapes=[pltpu.VMEM((tm, tn), F32)], args=(a, b), comm=comm)
        return out, landed
    return pl.pallas_call(
        body, name=name, grid=(gm, gn, gk), in_specs=[a_spec, b_spec], out_specs=o_spec,
        out_shape=o_shape, scratch_shapes=[pltpu.VMEM((tm, tn), F32)],
        compiler_params=_params(("parallel", "parallel", "arbitrary")),
    )(a, b)


def _comm_call(body, *, name, grid, in_specs, out_specs, out_shape, scratch_shapes, args, comm):
    n_in, n_out, n_scr = len(in_specs), len(out_shape), len(scratch_shapes)
    c_in, c_out = list(comm["inputs"]), list(comm["out_shape"])
    nci, nco = len(c_in), len(c_out)
    hbm = pl.BlockSpec(memory_space=pl.ANY)

    def body2(*refs):
        ins, cins = refs[:n_in], refs[n_in:n_in + nci]
        o0 = n_in + nci
        outs, couts = refs[o0:o0 + n_out], refs[o0 + n_out:o0 + n_out + nco]
        s0 = o0 + n_out + nco
        scr, cscr = refs[s0:s0 + n_scr], refs[s0 + n_scr:]
        first = functools.reduce(jnp.logical_and, [pl.program_id(a) == 0 for a in range(len(grid))])
        last = functools.reduce(jnp.logical_and, [pl.program_id(a) == grid[a] - 1 for a in range(len(grid))])

        @pl.when(first)
        def _():
            comm["start"](cins, couts, cscr)

        body(*ins, *outs, *scr)

        @pl.when(last)
        def _():
            comm["finish"](cins, couts, cscr)

    res = pl.pallas_call(
        body2, name=name, grid=grid, in_specs=list(in_specs) + [hbm] * nci, out_specs=list(out_specs) + [hbm] * nco,
        out_shape=list(out_shape) + c_out, scratch_shapes=list(scratch_shapes) + list(comm["scratch"]),
        input_output_aliases={n_in + k: n_out + v for k, v in comm.get("aliases", {}).items()},
        compiler_params=_params(("arbitrary",) * len(grid)),
    )(*args, *c_in)
    return res[:n_out], res[n_out:]


def _row(tr, w, blk=0):
    return pl.BlockSpec((tr, w), lambda i: (i, blk))


def _full(shape):
    nd = len(shape)
    return pl.BlockSpec(shape, lambda i: (0,) * nd)


def _halo_specs(s, tr, w, blk=0):
    per = tr // HALO
    last = s // HALO - 1
    return [
        pl.BlockSpec((HALO, w), lambda i: (jnp.maximum(i * per - 1, 0), blk)),
        pl.BlockSpec((tr, w), lambda i: (i, blk)),
        pl.BlockSpec((HALO, w), lambda i: (jnp.minimum((i + 1) * per, last), blk)),
    ]


def _ext(prev_ref, cur_ref, next_ref, s, tr):
    i = pl.program_id(0)
    e = jnp.concatenate([prev_ref[...].astype(F32), cur_ref[...].astype(F32), next_ref[...].astype(F32)], axis=0)
    rid = i * tr - HALO + lax.broadcasted_iota(jnp.int32, e.shape, 0)
    return jnp.where((rid >= 0) & (rid < s), e, 0.0)


def _valid_rows(shape, s, tr):
    i = pl.program_id(0)
    rid = i * tr - HALO + lax.broadcasted_iota(jnp.int32, shape, 0)
    return (rid >= 0) & (rid < s)


def _shift(e, k):
    if k == 0:
        return e
    n = e.shape[0]
    return pltpu.roll(e, (n - k) % n, 0)


def _acc_rows(ref, v):
    s = jnp.sum(v, axis=0, keepdims=True)

    @pl.when(pl.program_id(0) == 0)
    def _():
        ref[...] = s

    @pl.when(pl.program_id(0) > 0)
    def _():
        ref[...] += s


def _rowcall(body, name, s, tr, in_specs, out_specs, out_shape, args):
    return pl.pallas_call(
        body, name=name, grid=(s // tr,), in_specs=in_specs, out_specs=out_specs, out_shape=out_shape,
        compiler_params=_params(("arbitrary",)),
    )(*args)


def _modulate(x, scale, shift, *, name):
    s, d = x.shape
    tr = _tile_rows(s, d)

    def body(x_ref, sc_ref, sh_ref, o_ref):
        o_ref[...] = (x_ref[...] * (1.0 + sc_ref[...]) + sh_ref[...]).astype(BF16)

    return _rowcall(body, name, s, tr, [_row(tr, d), _full((1, d)), _full((1, d))], _row(tr, d),
                    jax.ShapeDtypeStruct((s, d), BF16), (x, scale, shift))


def _ln_stats(r):
    mu = jnp.mean(r, axis=-1, keepdims=True)
    xc = r - mu
    var = jnp.mean(xc * xc, axis=-1, keepdims=True)
    rstd = lax.rsqrt(var + LN_EPS)
    return xc * rstd, rstd


def _ln1_fwd(x, mix, gate, g, b, scale2, shift2):
    s, d = x.shape
    tr = _tile_rows(s, d)

    def body(x_ref, m_ref, gt_ref, g_ref, b_ref, sc_ref, sh_ref, x1_ref, h2_ref):
        r = DEEPNORM_ALPHA * x_ref[...] + (1.0 + gt_ref[...]) * m_ref[...]
        xh, _ = _ln_stats(r)
        x1 = xh * g_ref[...] + b_ref[...]
        x1_ref[...] = x1
        h2_ref[...] = (x1 * (1.0 + sc_ref[...]) + sh_ref[...]).astype(BF16)

    v = _full((1, d))
    return _rowcall(body, "ln1_fwd", s, tr, [_row(tr, d), _row(tr, d), v, v, v, v, v],
                    [_row(tr, d), _row(tr, d)],
                    [jax.ShapeDtypeStruct((s, d), F32), jax.ShapeDtypeStruct((s, d), BF16)],
                    (x, mix, gate, g, b, scale2, shift2))


def _relu2(u):
    s, f = u.shape
    tr = _tile_rows(s, f)

    def body(u_ref, o_ref):
        r = jnp.maximum(u_ref[...], 0.0)
        o_ref[...] = (r * r).astype(BF16)

    return _rowcall(body, "relu2", s, tr, [_row(tr, f)], _row(tr, f), jax.ShapeDtypeStruct((s, f), BF16), (u,))


def _ln2_loss_bwd(x1, f, target, gate, g, b):
    s, d = x1.shape
    tr = _tile_rows(s, d)

    def body(x1_ref, f_ref, t_ref, gt_ref, g_ref, b_ref, df_ref, dr_ref, loss_ref, dg_ref, db_ref, dgt_ref):
        fv = f_ref[...]
        r = DEEPNORM_ALPHA * x1_ref[...] + (1.0 + gt_ref[...]) * fv
        xh, rstd = _ln_stats(r)
        y = xh * g_ref[...] + b_ref[...]
        err = y - t_ref[...]
        _acc_rows(loss_ref, 0.5 * jnp.mean(err * err, axis=-1, keepdims=True))
        dy = err * (1.0 / d)
        _acc_rows(dg_ref, dy * xh)
        _acc_rows(db_ref, dy)
        dxh = dy * g_ref[...]
        dr = rstd * (dxh - jnp.mean(dxh, axis=-1, keepdims=True) - xh * jnp.mean(dxh * xh, axis=-1, keepdims=True))
        dr_ref[...] = dr
        df_ref[...] = ((1.0 + gt_ref[...]) * dr).astype(BF16)
        _acc_rows(dgt_ref, dr * fv)

    v = _full((1, d))
    one = _full((1, 1))
    return _rowcall(body, "ln2_loss_bwd", s, tr, [_row(tr, d), _row(tr, d), _row(tr, d), v, v, v],
                    [_row(tr, d), _row(tr, d), one, v, v, v],
                    [jax.ShapeDtypeStruct((s, d), BF16), jax.ShapeDtypeStruct((s, d), F32),
                     jax.ShapeDtypeStruct((1, 1), F32)] + [jax.ShapeDtypeStruct((1, d), F32)] * 3,
                    (x1, f, target, gate, g, b))


def _relu2_bwd(dff, u):
    s, f = u.shape
    tr = _tile_rows(s, f)

    def body(d_ref, u_ref, o_ref):
        o_ref[...] = (d_ref[...] * 2.0 * jnp.maximum(u_ref[...], 0.0)).astype(BF16)

    return _rowcall(body, "relu2_bwd", s, tr, [_row(tr, f), _row(tr, f)], _row(tr, f),
                    jax.ShapeDtypeStruct((s, f), BF16), (dff, u))


def _ln1_bwd(dh2, dr2, x1, x, mix, scale2, gate1, g1):
    s, d = x.shape
    tr = _tile_rows(s, d)

    def body(dh_ref, dr2_ref, x1_ref, x_ref, m_ref, sc_ref, gt_ref, g_ref,
             dm_ref, dxa_ref, dsc_ref, dsh_ref, dg_ref, db_ref, dgt_ref):
        dh = dh_ref[...]
        _acc_rows(dsc_ref, dh * x1_ref[...])
        _acc_rows(dsh_ref, dh)
        dy = dh * (1.0 + sc_ref[...]) + DEEPNORM_ALPHA * dr2_ref[...]
        mv = m_ref[...]
        r = DEEPNORM_ALPHA * x_ref[...] + (1.0 + gt_ref[...]) * mv
        xh, rstd = _ln_stats(r)
        _acc_rows(dg_ref, dy * xh)
        _acc_rows(db_ref, dy)
        dxh = dy * g_ref[...]
        dr = rstd * (dxh - jnp.mean(dxh, axis=-1, keepdims=True) - xh * jnp.mean(dxh * xh, axis=-1, keepdims=True))
        dm_ref[...] = ((1.0 + gt_ref[...]) * dr).astype(BF16)
        dxa_ref[...] = DEEPNORM_ALPHA * dr
        _acc_rows(dgt_ref, dr * mv)

    v = _full((1, d))
    return _rowcall(body, "ln1_bwd", s, tr, [_row(tr, d)] * 5 + [v, v, v],
                    [_row(tr, d), _row(tr, d), v, v, v, v, v],
                    [jax.ShapeDtypeStruct((s, d), BF16), jax.ShapeDtypeStruct((s, d), F32)]
                    + [jax.ShapeDtypeStruct((1, d), F32)] * 5,
                    (dh2, dr2, x1, x, mix, scale2, gate1, g1))


def _dx_final(dxa, dh1, x, scale1):
    s, d = x.shape
    tr = _tile_rows(s, d)

    def body(a_ref, dh_ref, x_ref, sc_ref, o_ref, dsc_ref, dsh_ref):
        dh = dh_ref[...]
        o_ref[...] = a_ref[...] + dh * (1.0 + sc_ref[...])
        _acc_rows(dsc_ref, dh * x_ref[...])
        _acc_rows(dsh_ref, dh)

    v = _full((1, d))
    return _rowcall(body, "dx_final", s, tr, [_row(tr, d)] * 3 + [v], [_row(tr, d), v, v],
                    [jax.ShapeDtypeStruct((s, d), F32)] + [jax.ShapeDtypeStruct((1, d), F32)] * 2,
                    (dxa, dh1, x, scale1))


def _conv_silu_fwd(proj, conv_w, conv_b, s, u):
    tr = _tile_rows(s, u)
    w = 2 * u
    half = SSM_CONV // 2

    def body(p0, c0, n0, p1, c1, n1, w_ref, b_ref, o_ref):
        for blk, (pr, cr, nr) in enumerate(((p0, c0, n0), (p1, c1, n1))):
            e = _ext(pr, cr, nr, s, tr)
            wv = w_ref[:, blk * u:(blk + 1) * u]
            acc = jnp.zeros_like(e)
            for k in range(SSM_CONV):
                acc = acc + _shift(e, k - half) * wv[k:k + 1, :]
            pre = acc[HALO:HALO + tr] + b_ref[:, blk * u:(blk + 1) * u]
            o_ref[:, blk * u:(blk + 1) * u] = _silu(pre)

    in_specs = _halo_specs(s, tr, u, 1) + _halo_specs(s, tr, u, 2) + [_full((SSM_CONV, w)), _full((1, w))]
    return _rowcall(body, "conv_silu_fwd", s, tr, in_specs, _row(tr, w), jax.ShapeDtypeStruct((s, w), F32),
                    (proj,) * 6 + (conv_w, conv_b))


def _conv_silu_bwd(proj, dxbc, conv_w, conv_b, s, u):
    tr = _tile_rows(s, u)
    w = 2 * u
    half = SSM_CONV // 2

    def body(p0, c0, n0, p1, c1, n1, dp0, dc0, dn0, dp1, dc1, dn1, w_ref, b_ref, du_ref, dw_ref, db_ref):
        for blk, (ur, dr) in enumerate((((p0, c0, n0), (dp0, dc0, dn0)), ((p1, c1, n1), (dp1, dc1, dn1)))):
            e = _ext(*ur, s, tr)
            de = _ext(*dr, s, tr)
            wv = w_ref[:, blk * u:(blk + 1) * u]
            acc = jnp.zeros_like(e)
            for k in range(SSM_CONV):
                acc = acc + _shift(e, k - half) * wv[k:k + 1, :]
            pre = acc + b_ref[:, blk * u:(blk + 1) * u]
            dpre = jnp.where(_valid_rows(e.shape, s, tr), de * _dsilu(pre), 0.0)
            du = jnp.zeros_like(e)
            rows = []
            for k in range(SSM_CONV):
                du = du + _shift(dpre, half - k) * wv[k:k + 1, :]
                rows.append(jnp.sum((_shift(e, k - half) * dpre)[HALO:HALO + tr], axis=0, keepdims=True))
            du_ref[:, blk * u:(blk + 1) * u] = du[HALO:HALO + tr].astype(BF16)
            dwv = jnp.concatenate(rows + [jnp.zeros((SUBLANES - SSM_CONV, u), F32)], axis=0)
            dbv = jnp.sum(dpre[HALO:HALO + tr], axis=0, keepdims=True)
            first = pl.program_id(0) == 0

            @pl.when(first)
            def _():
                dw_ref[:, blk * u:(blk + 1) * u] = dwv
                db_ref[:, blk * u:(blk + 1) * u] = dbv

            @pl.when(jnp.logical_not(first))
            def _():
                dw_ref[:, blk * u:(blk + 1) * u] += dwv
                db_ref[:, blk * u:(blk + 1) * u] += dbv

    in_specs = (_halo_specs(s, tr, u, 1) + _halo_specs(s, tr, u, 2) + _halo_specs(s, tr, u, 0)
                + _halo_specs(s, tr, u, 1) + [_full((SSM_CONV, w)), _full((1, w))])
    return _rowcall(body, "conv_silu_bwd", s, tr, in_specs,
                    [_row(tr, w), _full((SUBLANES, w)), _full((1, w))],
                    [jax.ShapeDtypeStruct((s, w), BF16), jax.ShapeDtypeStruct((SUBLANES, w), F32),
                     jax.ShapeDtypeStruct((1, w), F32)],
                    (proj,) * 6 + (dxbc,) * 6 + (conv_w, conv_b))


def _expanders(h):
    col64 = jnp.arange(2 * h * HEAD_DIM) // HEAD_DIM
    col128 = jnp.arange(2 * h * LANES) // LANES
    row = jnp.arange(LANES)[:, None]
    return (row == col64[None, :]).astype(BF16), (row == col128[None, :]).astype(BF16)


def _dt_prep(proj, bias_row, a_row, s, u, h):
    q = CHUNK
    e64, e128 = _expanders(h)
    ds = h * HEAD_DIM
    dtblk = (6 * u) // LANES

    def body(raw_ref, b_ref, a_ref, e64_ref, e128_ref, dt_ref, cum_ref, dte_ref, cume_ref):
        lane = lax.broadcasted_iota(jnp.int32, (q, LANES), 1)
        dt = jnp.where(lane < 2 * h, _softplus(raw_ref[...] + b_ref[...]), 0.0)
        da = dt * a_ref[...]
        ii = lax.broadcasted_iota(jnp.int32, (q, q), 0)
        kk = lax.broadcasted_iota(jnp.int32, (q, q), 1)
        lower = (kk <= ii).astype(F32).astype(BF16)
        upper = (kk >= ii).astype(F32).astype(BF16)
        cum = jnp.where(lane < h, _dot3_l(lower, da), _dot3_l(upper, da))
        dt_ref[...] = dt
        cum_ref[...] = cum
        dte = _dot3_r(dt, e64_ref[...])
        cume = _dot3_r(cum, e128_ref[...])
        dte_ref[0] = dte[:, :ds]
        dte_ref[1] = dte[:, ds:]
        cume_ref[0] = cume[:, :h * LANES]
        cume_ref[1] = cume[:, h * LANES:]

    in_specs = [pl.BlockSpec((q, LANES), lambda i: (i, dtblk)), _full((1, LANES)), _full((1, LANES)),
                _full(e64.shape), _full(e128.shape)]
    out_specs = [_row(q, LANES), _row(q, LANES),
                 pl.BlockSpec((2, q, ds), lambda i: (0, i, 0)), pl.BlockSpec((2, q, h * LANES), lambda i: (0, i, 0))]
    out_shape = [jax.ShapeDtypeStruct((s, LANES), F32), jax.ShapeDtypeStruct((s, LANES), F32),
                 jax.ShapeDtypeStruct((2, s, ds), F32), jax.ShapeDtypeStruct((2, s, h * LANES), F32)]
    return _rowcall(body, "dt_prep", s, q, in_specs, out_specs, out_shape, (proj, bias_row, a_row, e64, e128))


def _ssd_specs(s, h, g):
    q = CHUNK
    nc = s // q
    ds = h * HEAD_DIM
    nb = g * STATE
    return q, nc, ds, nb


def _ssd_fwd(xbc, dt_e, cum_e, cum_t, s, h, g, comm=None):
    q, nc, ds, nb = _ssd_specs(s, h, g)
    npair = h // 2

    def cidx(d, i):
        return jnp.where(d == 0, i, nc - 1 - i)

    def body(x_ref, b_ref, c_ref, dt_ref, cum_ref, cumt_ref, y_ref, sp_ref, st):
        d = pl.program_id(0)
        i = pl.program_id(1)

        @pl.when(i == 0)
        def _():
            st[...] = jnp.zeros_like(st)

        rev = d == 1
        ii = lax.broadcasted_iota(jnp.int32, (q, q), 0)
        jj = lax.broadcasted_iota(jnp.int32, (q, q), 1)
        sgn = jnp.where(rev, -1, 1)
        mask = (jj - ii) * sgn <= 0
        left = lax.broadcasted_iota(jnp.int32, (q, LANES), 1) < HEAD_DIM

        def group(gi, carry):
            goff = pl.multiple_of(gi * STATE, STATE)
            cg = c_ref[:, pl.ds(goff, STATE)].astype(BF16)
            bg = b_ref[:, pl.ds(goff, STATE)].astype(BF16)
            gm = _dot_nt(cg, bg)
            for p in range(HEADS_PER_GROUP // 2):
                pr = gi * (HEADS_PER_GROUP // 2) + p
                off = pl.multiple_of(pr * LANES, LANES)
                xd = x_ref[:, pl.ds(off, LANES)] * dt_ref[:, pl.ds(off, LANES)]
                ms = []
                cols = []
                for hl in range(2):
                    hh = 2 * pr + hl
                    col = cum_ref[:, pl.ds(pl.multiple_of(hh * LANES, LANES), LANES)]
                    row = cumt_ref[pl.ds(hh, 1), :]
                    lm = jnp.where(mask, jnp.exp(jnp.minimum(col - row, 0.0)), 0.0)
                    ms.append((gm * lm).astype(BF16))
                    cols.append(col)
                y = _dot(ms[0], jnp.where(left, xd, 0.0).astype(BF16)) + _dot(ms[1], jnp.where(left, 0.0, xd).astype(BF16))
                ce = jnp.where(left, cols[0], cols[1])
                sprev = st[pr]
                sp_ref[pr] = sprev
                y = y + jnp.exp(ce) * _dot(cg, sprev.astype(BF16))
                y_ref[:, pl.ds(off, LANES)] = y
                tot = jnp.where(rev, ce[0:1, :], ce[q - 1:q, :])
                v = (xd * jnp.exp(tot - ce)).astype(BF16)
                st[pr] = jnp.exp(tot) * sprev + _dot_tn(bg, v)
            return carry

        lax.fori_loop(0, g, group, 0)

    in_specs = [
        pl.BlockSpec((q, ds), lambda d, i: (cidx(d, i), 0)),
        pl.BlockSpec((q, nb), lambda d, i: (cidx(d, i), ds // nb)),
        pl.BlockSpec((q, nb), lambda d, i: (cidx(d, i), ds // nb + 1)),
        pl.BlockSpec((None, q, ds), lambda d, i: (d, cidx(d, i), 0)),
        pl.BlockSpec((None, q, h * LANES), lambda d, i: (d, cidx(d, i), 0)),
        pl.BlockSpec((None, h, q), lambda d, i: (d, 0, cidx(d, i))),
    ]
    out_specs = [
        pl.BlockSpec((None, q, ds), lambda d, i: (d, cidx(d, i), 0)),
        pl.BlockSpec((None, None, npair, STATE, LANES), lambda d, i: (d, cidx(d, i), 0, 0, 0)),
    ]
    out_shape = [jax.ShapeDtypeStruct((2, s, ds), F32), jax.ShapeDtypeStruct((2, nc, npair, STATE, LANES), F32)]
    if comm is not None:
        return _comm_call(body, name="ssd_fwd", grid=(2, nc), in_specs=in_specs, out_specs=out_specs, out_shape=out_shape,
                          scratch_shapes=[pltpu.VMEM((npair, STATE, LANES), F32)],
                          args=(xbc, xbc, xbc, dt_e, cum_e, cum_t), comm=comm)
    return pl.pallas_call(
        body, name="ssd_fwd", grid=(2, nc), in_specs=in_specs, out_specs=out_specs, out_shape=out_shape,
        scratch_shapes=[pltpu.VMEM((npair, STATE, LANES), F32)],
        compiler_params=_params(("arbitrary", "arbitrary")),
    )(xbc, xbc, xbc, dt_e, cum_e, cum_t), ()


def _ssd_bwd(xbc, dt_e, cum_e, cum_t, dt_t, a_col, dy, sp, s, h, g):
    q, nc, ds, nb = _ssd_specs(s, h, g)
    npair = h // 2

    def cidx(d, i):
        return jnp.where(d == 0, nc - 1 - i, i)

    def body(x_ref, b_ref, c_ref, dt_ref, cum_ref, cumt_ref, dtt_ref, a_ref, dy_ref, sp_ref,
             dx_ref, db_ref, dc_ref, ddt_ref, da_ref, dst, rowp):
        d = pl.program_id(0)
        i = pl.program_id(1)

        @pl.when(i == 0)
        def _():
            dst[...] = jnp.zeros_like(dst)
            da_ref[...] = jnp.zeros_like(da_ref)

        rev = d == 1
        ii = lax.broadcasted_iota(jnp.int32, (q, q), 0)
        jj = lax.broadcasted_iota(jnp.int32, (q, q), 1)
        sgn = jnp.where(rev, -1, 1)
        mask = (jj - ii) * sgn <= 0
        lane = lax.broadcasted_iota(jnp.int32, (q, LANES), 1)
        left = lane < HEAD_DIM
        rowp[...] = jnp.zeros_like(rowp)

        def group(gi, carry):
            acc_dcum, acc_tot, acc_dxx = carry
            goff = pl.multiple_of(gi * STATE, STATE)
            cg = c_ref[:, pl.ds(goff, STATE)].astype(BF16)
            bg = b_ref[:, pl.ds(goff, STATE)].astype(BF16)
            gm = _dot_nt(cg, bg)
            dgm = jnp.zeros((q, q), F32)
            dcg = jnp.zeros((q, STATE), F32)
            dbg = jnp.zeros((q, STATE), F32)
            for p in range(HEADS_PER_GROUP // 2):
                pr = gi * (HEADS_PER_GROUP // 2) + p
                off = pl.multiple_of(pr * LANES, LANES)
                xv = x_ref[:, pl.ds(off, LANES)]
                dte = dt_ref[:, pl.ds(off, LANES)]
                xd = xv * dte
                xdb = xd.astype(BF16)
                dyv = dy_ref[:, pl.ds(off, LANES)]
                sprev = sp_ref[pr]
                sprevb = sprev.astype(BF16)
                dsn = dst[pr]
                dsnb = dsn.astype(BF16)
                cols = [cum_ref[:, pl.ds(pl.multiple_of((2 * pr + hl) * LANES, LANES), LANES)] for hl in range(2)]
                ce = jnp.where(left, cols[0], cols[1])
                tot = jnp.where(rev, ce[0:1, :], ce[q - 1:q, :])
                et = jnp.exp(tot)
                r = jnp.exp(tot - ce)
                e = jnp.exp(ce)
                yoff = e * _dot(cg, sprevb)
                dz = (e * dyv).astype(BF16)
                dcg = dcg + _dot_nt(dz, sprevb)
                dsprev = _dot_tn(cg, dz) + et * dsn
                f1 = dyv * yoff
                v = (xd * r).astype(BF16)
                dbg = dbg + _dot_nt(v, dsnb)
                dv = _dot(bg, dsnb)
                dxd = dv * r
                tt = dv * xd * r
                wt = dsn * sprev * et
                for hl in range(2):
                    hh = 2 * pr + hl
                    hm = left if hl == 0 else jnp.logical_not(left)
                    row = cumt_ref[pl.ds(hh, 1), :]
                    lm = jnp.where(mask, jnp.exp(jnp.minimum(cols[hl] - row, 0.0)), 0.0)
                    mf = gm * lm
                    dym = jnp.where(hm, dyv, 0.0).astype(BF16)
                    dm = _dot_nt(dym, xdb)
                    dxd = dxd + _dot_tn(mf.astype(BF16), dym)
                    dgm = dgm + dm * lm
                    em = dm * mf
                    rowp[pl.ds(hh, 1), :] = rowp[pl.ds(hh, 1), :] - jnp.sum(em, axis=0, keepdims=True)
                    colq = (jnp.sum(em, axis=1, keepdims=True)
                            + jnp.sum(jnp.where(hm, f1 - tt, 0.0), axis=1, keepdims=True))
                    acc_dcum = jnp.where(lane == hh, colq, acc_dcum)
                    totq = jnp.sum(jnp.sum(jnp.where(hm, tt + wt, 0.0), axis=1, keepdims=True), axis=0, keepdims=True)
                    acc_tot = jnp.where(lane == hh, totq, acc_tot)
                dxx = dxd * xv
                for hl in range(2):
                    hh = 2 * pr + hl
                    hm = left if hl == 0 else jnp.logical_not(left)
                    acc_dxx = jnp.where(lane == hh, jnp.sum(jnp.where(hm, dxx, 0.0), axis=1, keepdims=True), acc_dxx)
                dx_ref[:, pl.ds(off, LANES)] = dxd * dte
                dst[pr] = dsprev
            dgb = dgm.astype(BF16)
            dc_ref[:, pl.ds(goff, STATE)] = dcg + _dot(dgb, bg)
            db_ref[:, pl.ds(goff, STATE)] = dbg + _dot_tn(dgb, cg)
            return acc_dcum, acc_tot, acc_dxx

        zero = jnp.zeros((q, LANES), F32)
        acc_dcum, acc_tot, acc_dxx = lax.fori_loop(0, g, group, (zero, zero, zero))
        dcum_t = rowp[...] + jnp.transpose(acc_dcum)[:h]
        rmat = ((ii - jj) * sgn >= 0).astype(F32).astype(BF16)
        da_t = _dot3_r(dcum_t, rmat) + jnp.transpose(acc_tot)[:h]
        ddt_ref[...] = da_t * a_ref[...] + jnp.transpose(acc_dxx)[:h]
        da_ref[...] += da_t * dtt_ref[...]

    in_specs = [
        pl.BlockSpec((q, ds), lambda d, i: (cidx(d, i), 0)),
        pl.BlockSpec((q, nb), lambda d, i: (cidx(d, i), ds // nb)),
        pl.BlockSpec((q, nb), lambda d, i: (cidx(d, i), ds // nb + 1)),
        pl.BlockSpec((None, q, ds), lambda d, i: (d, cidx(d, i), 0)),
        pl.BlockSpec((None, q, h * LANES), lambda d, i: (d, cidx(d, i), 0)),
        pl.BlockSpec((None, h, q), lambda d, i: (d, 0, cidx(d, i))),
        pl.BlockSpec((None, h, q), lambda d, i: (d, 0, cidx(d, i))),
        pl.BlockSpec((None, h, LANES), lambda d, i: (d, 0, 0)),
        pl.BlockSpec((q, ds), lambda d, i: (cidx(d, i), 0)),
        pl.BlockSpec((None, None, npair, STATE, LANES), lambda d, i: (d, cidx(d, i), 0, 0, 0)),
    ]
    out_specs = [
        pl.BlockSpec((None, q, ds), lambda d, i: (d, cidx(d, i), 0)),
        pl.BlockSpec((None, q, nb), lambda d, i: (d, cidx(d, i), 0)),
        pl.BlockSpec((None, q, nb), lambda d, i: (d, cidx(d, i), 0)),
        pl.BlockSpec((None, h, q), lambda d, i: (d, 0, cidx(d, i))),
        pl.BlockSpec((None, h, LANES), lambda d, i: (d, 0, 0)),
    ]
    out_shape = [jax.ShapeDtypeStruct((2, s, ds), F32), jax.ShapeDtypeStruct((2, s, nb), F32),
                 jax.ShapeDtypeStruct((2, s, nb), F32), jax.ShapeDtypeStruct((2, h, s), F32),
                 jax.ShapeDtypeStruct((2, h, LANES), F32)]
    return pl.pallas_call(
        body, name="ssd_bwd", grid=(2, nc), in_specs=in_specs, out_specs=out_specs, out_shape=out_shape,
        scratch_shapes=[pltpu.VMEM((npair, STATE, LANES), F32), pltpu.VMEM((h, q), F32)],
        compiler_params=_params(("arbitrary", "arbitrary")),
    )(xbc, xbc, xbc, dt_e, cum_e, cum_t, dt_t, a_col, dy, sp)


def _dt_bwd(ddt, proj, bias_row, s, u, h):
    tr = _tile_rows(s, 4 * LANES)
    dtblk = (6 * u) // LANES

    def body(d_ref, raw_ref, b_ref, o_ref, db_ref):
        lane = lax.broadcasted_iota(jnp.int32, (tr, LANES), 1)
        v = jnp.where(lane < 2 * h, d_ref[...] * _sigmoid(raw_ref[...] + b_ref[...]), 0.0)
        o_ref[...] = v.astype(BF16)
        _acc_rows(db_ref, v)

    return _rowcall(body, "dt_bwd", s, tr, [_row(tr, LANES), _row(tr, LANES, dtblk), _full((1, LANES))],
                    [_row(tr, LANES), _full((1, LANES))],
                    [jax.ShapeDtypeStruct((s, LANES), BF16), jax.ShapeDtypeStruct((1, LANES), F32)],
                    (ddt, proj, bias_row))


def _group_rms(v, gw):
    outs, facs = [], []
    for k in range(v.shape[1] // gw):
        blk = v[:, k * gw:(k + 1) * gw]
        f = lax.rsqrt(jnp.mean(blk * blk, axis=-1, keepdims=True) + RMS_EPS)
        outs.append(blk * f)
        facs.append(jnp.broadcast_to(f, blk.shape))
    return jnp.concatenate(outs, axis=1), jnp.concatenate(facs, axis=1)


def _group_rms_bwd(dn, n, fac, gw):
    outs = []
    for k in range(n.shape[1] // gw):
        sl = slice(k * gw, (k + 1) * gw)
        outs.append(fac[:, sl] * (dn[:, sl] - n[:, sl] * jnp.mean(dn[:, sl] * n[:, sl], axis=-1, keepdims=True)))
    return jnp.concatenate(outs, axis=1)


def _gate_norm_fwd(y2, xbc, proj, d_e, norm_w, s, u, g):
    tr = _tile_rows(s, u)
    gw = u // g

    def body(y_ref, x_ref, z_ref, d_ref, w_ref, o_ref):
        ys = y_ref[0] + y_ref[1] + d_ref[...] * x_ref[...]
        n, _ = _group_rms(ys * _silu(z_ref[...]), gw)
        o_ref[...] = (n * w_ref[...]).astype(BF16)

    return _rowcall(body, "gate_norm_fwd", s, tr,
                    [pl.BlockSpec((2, tr, u), lambda i: (0, i, 0)), _row(tr, u), _row(tr, u), _full((1, u)), _full((1, u))],
                    _row(tr, u), jax.ShapeDtypeStruct((s, u), BF16), (y2, xbc, proj, d_e, norm_w))


def _gate_norm_bwd(dymix, y2, xbc, proj, d_e, norm_w, s, u, g):
    tr = _tile_rows(s, u)
    gw = u // g

    def body(dy_ref, y_ref, x_ref, z_ref, d_ref, w_ref, dys_ref, dz_ref, dxs_ref, dw_ref, dd_ref):
        xv = x_ref[...]
        zv = z_ref[...]
        ys = y_ref[0] + y_ref[1] + d_ref[...] * xv
        sz = _silu(zv)
        n, fac = _group_rms(ys * sz, gw)
        dout = dy_ref[...]
        _acc_rows(dw_ref, dout * n)
        dyg = _group_rms_bwd(dout * w_ref[...], n, fac, gw)
        dys = dyg * sz
        dys_ref[...] = dys
        dz_ref[...] = (dyg * ys * _dsilu(zv)).astype(BF16)
        dxs_ref[...] = dys * d_ref[...]
        _acc_rows(dd_ref, dys * xv)

    v = _full((1, u))
    return _rowcall(body, "gate_norm_bwd", s, tr,
                    [_row(tr, u), pl.BlockSpec((2, tr, u), lambda i: (0, i, 0)), _row(tr, u), _row(tr, u), v, v],
                    [_row(tr, u), _row(tr, u), _row(tr, u), v, v],
                    [jax.ShapeDtypeStruct((s, u), F32), jax.ShapeDtypeStruct((s, u), BF16),
                     jax.ShapeDtypeStruct((s, u), F32), jax.ShapeDtypeStruct((1, u), F32), jax.ShapeDtypeStruct((1, u), F32)],
                    (dymix, y2, xbc, proj, d_e, norm_w))


def _shortconv_fwd(proj, conv_w, norm_w, s, u):
    tr = _tile_rows(s, u)
    half = SC_CONV // 2

    def body(hp, hc, hn, b_ref, cp, cc, cn, cw_ref, w_ref, o_ref):
        t = _ext(hp, hc, hn, s, tr) * _ext(cp, cc, cn, s, tr)
        wv = cw_ref[...]
        acc = jnp.zeros_like(t)
        for k in range(SC_CONV):
            acc = acc + _shift(t, k - half) * wv[k:k + 1, :]
        n, _ = _group_rms(b_ref[...] * acc[HALO:HALO + tr], SC_GROUP_WIDTH)
        o_ref[...] = (n * w_ref[...]).astype(BF16)

    in_specs = _halo_specs(s, tr, u, 3) + [_row(tr, u, 4)] + _halo_specs(s, tr, u, 5) + [_full((SC_CONV, u)), _full((1, u))]
    return _rowcall(body, "shortconv_fwd", s, tr, in_specs, _row(tr, u), jax.ShapeDtypeStruct((s, u), BF16),
                    (proj,) * 7 + (conv_w, norm_w))


def _shortconv_bwd(dymix, proj, conv_w, norm_w, s, u):
    tr = _tile_rows(s, u)
    half = SC_CONV // 2

    def body(dp, dc_, dn, hp, hc, hn, bp, bc, bn, cp, cc, cn, cw_ref, w_ref, dh_ref, db_ref, dcc_ref, dcw_ref, dw_ref):
        dout = _ext(dp, dc_, dn, s, tr)
        hv = _ext(hp, hc, hn, s, tr)
        bv = _ext(bp, bc, bn, s, tr)
        cv = _ext(cp, cc, cn, s, tr)
        t = hv * cv
        wv = cw_ref[...]
        acc = jnp.zeros_like(t)
        for k in range(SC_CONV):
            acc = acc + _shift(t, k - half) * wv[k:k + 1, :]
        n, fac = _group_rms(bv * acc, SC_GROUP_WIDTH)
        cur = slice(HALO, HALO + tr)
        _acc_rows(dw_ref, (dout * n)[cur])
        dyv = _group_rms_bwd(dout * w_ref[...], n, fac, SC_GROUP_WIDTH)
        db_ref[...] = (dyv * acc)[cur].astype(BF16)
        dv = dyv * bv
        dt = jnp.zeros_like(t)
        rows = []
        for k in range(SC_CONV):
            dt = dt + _shift(dv, half - k) * wv[k:k + 1, :]
            rows.append(jnp.sum((_shift(t, k - half) * dv)[cur], axis=0, keepdims=True))
        dh_ref[...] = (dt * cv)[cur].astype(BF16)
        dcc_ref[...] = (dt * hv)[cur].astype(BF16)
        dwv = jnp.concatenate(rows + [jnp.zeros((SUBLANES - SC_CONV, u), F32)], axis=0)
        first = pl.program_id(0) == 0

        @pl.when(first)
        def _():
            dcw_ref[...] = dwv

        @pl.when(jnp.logical_not(first))
        def _():
            dcw_ref[...] += dwv

    in_specs = (_halo_specs(s, tr, u, 1) + _halo_specs(s, tr, u, 3) + _halo_specs(s, tr, u, 4) + _halo_specs(s, tr, u, 5)
                + [_full((SC_CONV, u)), _full((1, u))])
    return _rowcall(body, "shortconv_bwd", s, tr, in_specs,
                    [_row(tr, u)] * 3 + [_full((SUBLANES, u)), _full((1, u))],
                    [jax.ShapeDtypeStruct((s, u), BF16)] * 3
                    + [jax.ShapeDtypeStruct((SUBLANES, u), F32), jax.ShapeDtypeStruct((1, u), F32)],
                    (dymix,) * 3 + (proj,) * 9 + (conv_w, norm_w))


def _adam_math(w, g, m, v):
    m2 = ADAM_B1 * m + (1.0 - ADAM_B1) * g
    v2 = ADAM_B2 * v + (1.0 - ADAM_B2) * (g * g)
    m_hat = m2 / (1.0 - ADAM_B1 ** ADAM_STEP)
    v_hat = v2 / (1.0 - ADAM_B2 ** ADAM_STEP)
    delta = -ADAM_LR * (m_hat / (jnp.sqrt(v_hat) + ADAM_EPS) + ADAM_WD * w)
    return delta, m2, v2


def _adam_rows(r, c):
    return _pick(r, max(SUBLANES, (1 << 20) // (4 * c)), SUBLANES)


def _adam(w, g, m, v, *, name):
    r, c = w.shape
    tr = _adam_rows(r, c)

    def body(w_ref, g_ref, m_ref, v_ref, d_ref, m2_ref, v2_ref):
        d_ref[...], m2_ref[...], v2_ref[...] = _adam_math(w_ref[...], g_ref[...], m_ref[...], v_ref[...])

    return _rowcall(body, name, r, tr, [_row(tr, c)] * 4, [_row(tr, c)] * 3,
                    [jax.ShapeDtypeStruct((r, c), F32)] * 3, (w, g, m, v))


def _adam_outer(w, a_t, bmat, m, v, *, name):
    r, c = w.shape
    tr = _adam_rows(r, c)
    kk = a_t.shape[1]

    def body(w_ref, a_ref, b_ref, m_ref, v_ref, g_ref, d_ref, m2_ref, v2_ref):
        g = _dot(a_ref[...].astype(BF16), b_ref[...].astype(BF16))
        g_ref[...] = g
        d_ref[...], m2_ref[...], v2_ref[...] = _adam_math(w_ref[...], g, m_ref[...], v_ref[...])

    return _rowcall(body, name, r, tr, [_row(tr, c), _row(tr, kk), _full((kk, c)), _row(tr, c), _row(tr, c)],
                    [_row(tr, c)] * 4, [jax.ShapeDtypeStruct((r, c), F32)] * 4, (w, a_t, bmat, m, v))


ANY = pl.BlockSpec(memory_space=pl.ANY)
VMEM_WHOLE = pl.BlockSpec(memory_space=pltpu.VMEM)


def _place():
    x, y, c = lax.axis_index("x"), lax.axis_index("y"), lax.axis_index("c")
    return x, y, c


DMA_CHUNKS = 8


def _n_chunks(rows):
    n = DMA_CHUNKS
    while n > 1 and rows % (16 * n):
        n //= 2
    return n


def _allgather_small(v, *, name):
    m_per, n = v.shape

    def body(x_ref, out_ref, send_sems, recv_sems, local_sem):
        x, y, c = _place()
        me, sibling = (x, y, c), (x, y, 1 - c)
        chips = [(1 - x, y), (x, 1 - y), (1 - x, 1 - y)]

        def rows(px, py, pc):
            return out_ref.at[pl.ds((4 * px + 2 * py + pc) * m_per, m_per), :]

        def copy(k, block, to, src=None):
            return pltpu.make_async_remote_copy(
                src_ref=rows(*block) if src is None else src, dst_ref=rows(*block),
                send_sem=send_sems.at[k], recv_sem=recv_sems.at[k], device_id=to, device_id_type=MESH)

        mine = pltpu.make_async_copy(x_ref, rows(*me), local_sem)
        mine.start()
        first = [copy(0, me, sibling, src=x_ref)]
        first += [copy(1 + j, me, (*chip, c), src=x_ref) for j, chip in enumerate(chips)]
        for cp in first:
            cp.start()
        passed = [copy(4 + j, (*chip, c), sibling) for j, chip in enumerate(chips)]
        for j, chip in enumerate(chips):
            copy(1 + j, (*chip, c), me).wait_recv()
            passed[j].start()
        copy(0, sibling, me).wait_recv()
        for j, chip in enumerate(chips):
            copy(4 + j, (*chip, 1 - c), me).wait_recv()
        for cp in first + passed:
            cp.wait_send()
        mine.wait()

    return pl.pallas_call(
        body, name=name, out_shape=jax.ShapeDtypeStruct((N_DEV * m_per, n), v.dtype),
        in_specs=[VMEM_WHOLE], out_specs=VMEM_WHOLE,
        scratch_shapes=[pltpu.SemaphoreType.DMA((7,)), pltpu.SemaphoreType.DMA((7,)), pltpu.SemaphoreType.DMA],
        compiler_params=pltpu.CompilerParams(vmem_limit_bytes=VMEM_LIMIT),
    )(v)


def _chip_id():
    return 2 * lax.axis_index("x") + lax.axis_index("y")


def _core_id():
    return lax.axis_index("c")


def _cast_into_block(wl, *, name):
    r, c_ = wl.shape
    tr = _tile_rows(r, c_, 16)

    def body(w_ref, o_ref):
        o_ref[...] = w_ref[...].astype(BF16)

    return pl.pallas_call(
        body, name=name, grid=(r // tr,), in_specs=[pl.BlockSpec((tr, c_), lambda i: (i, 0))],
        out_specs=pl.BlockSpec((None, tr, c_), lambda i: (_chip_id(), i, 0)),
        out_shape=jax.ShapeDtypeStruct((N_CHIPS, r, c_), BF16), compiler_params=_params(("arbitrary",)))(wl)


def _gather_weight(buf, *, name):
    _, r, c_ = buf.shape
    half = r // 2
    nch = _n_chunks(half)
    rows = half // nch

    def body(in_ref, out_ref, send_sems, recv_sems):
        del in_ref
        x, y, c = _place()
        me, sibling = (x, y, c), (x, y, 1 - c)
        chips = [(1 - x, y), (x, 1 - y), (1 - x, 1 - y)]

        def blk(px, py, pc, i):
            return out_ref.at[2 * px + py, pl.ds(pc * half + i * rows, rows), :]

        def copy(k, i, block, to):
            return pltpu.make_async_remote_copy(
                src_ref=blk(*block, i), dst_ref=blk(*block, i),
                send_sem=send_sems.at[k * nch + i], recv_sem=recv_sems.at[k * nch + i], device_id=to, device_id_type=MESH)

        first = [copy(j, i, me, (*chip, c)) for i in range(nch) for j, chip in enumerate(chips)]
        for cp in first:
            cp.start()
        passed = []
        for i in range(nch):
            for j, chip in enumerate(chips):
                copy(j, i, (*chip, c), me).wait_recv()
                passed.append(copy(3 + j, i, (*chip, c), sibling))
                passed[-1].start()
        for i in range(nch):
            for j, chip in enumerate(chips):
                copy(3 + j, i, (*chip, 1 - c), me).wait_recv()
        for cp in first + passed:
            cp.wait_send()

    return pl.pallas_call(
        body, name=name, out_shape=jax.ShapeDtypeStruct(buf.shape, buf.dtype),
        in_specs=[ANY], out_specs=ANY, input_output_aliases={0: 0},
        scratch_shapes=[pltpu.SemaphoreType.DMA((6 * nch,)), pltpu.SemaphoreType.DMA((6 * nch,))],
    )(buf)


def _gather_comm(buf):
    _, r, c_ = buf.shape
    half = r // 2
    nch = _n_chunks(half)
    rows = half // nch

    def plan(out_ref):
        x, y, c = _place()
        me, sibling = (x, y, c), (x, y, 1 - c)
        chips = [(1 - x, y), (x, 1 - y), (1 - x, 1 - y)]
        return me, sibling, chips, c

    def copy(out_ref, sems, k, i, block, to):
        part = out_ref.at[2 * block[0] + block[1], pl.ds(block[2] * half + i * rows, rows), :]
        return pltpu.make_async_remote_copy(src_ref=part, dst_ref=part, send_sem=sems[0].at[k * nch + i],
                                            recv_sem=sems[1].at[k * nch + i], device_id=to, device_id_type=MESH)

    def start(cins, couts, sems):
        (out_ref,) = couts
        me, sibling, chips, c = plan(out_ref)
        for i in range(nch):
            for j, chip in enumerate(chips):
                copy(out_ref, sems, j, i, me, (*chip, c)).start()

    def finish(cins, couts, sems):
        (out_ref,) = couts
        me, sibling, chips, c = plan(out_ref)
        passed = []
        for i in range(nch):
            for j, chip in enumerate(chips):
                copy(out_ref, sems, j, i, (*chip, c), me).wait_recv()
                passed.append(copy(out_ref, sems, 3 + j, i, (*chip, c), sibling))
                passed[-1].start()
        for i in range(nch):
            for j, chip in enumerate(chips):
                copy(out_ref, sems, 3 + j, i, (*chip, 1 - c), me).wait_recv()
        for i in range(nch):
            for j, chip in enumerate(chips):
                copy(out_ref, sems, j, i, me, (*chip, c)).wait_send()
        for cp in passed:
            cp.wait_send()

    return dict(inputs=[buf], out_shape=[jax.ShapeDtypeStruct(buf.shape, buf.dtype)], aliases={0: 0},
                scratch=[pltpu.SemaphoreType.DMA((6 * nch,)), pltpu.SemaphoreType.DMA((6 * nch,))],
                start=start, finish=finish)


def _pair_exchange(gfull, *, name):
    nblk, r, c_ = gfull.shape
    half = r // 2
    nch = _n_chunks(half)
    rows = half // nch

    def body(g_ref, peer_ref, send_sems, recv_sems):
        x, y, c = _place()

        def copy(k, i, pc):
            return pltpu.make_async_remote_copy(
                src_ref=g_ref.at[k, pl.ds(pc * half + i * rows, rows), :], dst_ref=peer_ref.at[k, pl.ds(i * rows, rows), :],
                send_sem=send_sems.at[k * nch + i], recv_sem=recv_sems.at[k * nch + i],
                device_id=(x, y, 1 - c), device_id_type=MESH)

        sends = [copy(k, i, 1 - c) for i in range(nch) for k in range(nblk)]
        for cp in sends:
            cp.start()
        for cp in sends:
            cp.wait_recv()
        for cp in sends:
            cp.wait_send()

    return pl.pallas_call(
        body, name=name, out_shape=jax.ShapeDtypeStruct((nblk, half, c_), gfull.dtype),
        in_specs=[ANY], out_specs=ANY,
        scratch_shapes=[pltpu.SemaphoreType.DMA((nblk * nch,)), pltpu.SemaphoreType.DMA((nblk * nch,))],
    )(gfull)


def _pair_add(gfull, peer, *, name):
    nblk, r, c_ = gfull.shape
    half = r // 2
    tr = _pick(half, max(16, (1 << 20) // (2 * c_)), 16)
    per = half // tr

    def body(g_ref, p_ref, o_ref):
        o_ref[...] = (g_ref[...].astype(F32) + p_ref[...].astype(F32)).astype(BF16)

    return pl.pallas_call(
        body, name=name, grid=(nblk, per),
        in_specs=[pl.BlockSpec((None, tr, c_), lambda k, i: (k, _core_id() * per + i, 0)),
                  pl.BlockSpec((None, tr, c_), lambda k, i: (k, i, 0))],
        out_specs=pl.BlockSpec((None, tr, c_), lambda k, i: (k, i, 0)),
        out_shape=jax.ShapeDtypeStruct((nblk, half, c_), BF16),
        compiler_params=_params(("arbitrary", "arbitrary")))(gfull, peer)


def _scatter_comm(pre):
    _, half, c_ = pre.shape
    nch = _n_chunks(half)
    rows = half // nch

    def copies(cins, couts, sems):
        (p_ref,), (r_ref,) = cins, couts
        x, y, c = _place()
        out = []
        for i in range(nch):
            for j, (tx, ty) in reversed(list(enumerate([(1 - x, y), (x, 1 - y), (1 - x, 1 - y)]))):
                out.append(pltpu.make_async_remote_copy(
                    src_ref=p_ref.at[2 * tx + ty, pl.ds(i * rows, rows), :], dst_ref=r_ref.at[j, pl.ds(i * rows, rows), :],
                    send_sem=sems[0].at[j * nch + i], recv_sem=sems[1].at[j * nch + i],
                    device_id=(tx, ty, c), device_id_type=MESH))
        return out

    def start(cins, couts, sems):
        for cp in copies(cins, couts, sems):
            cp.start()

    def finish(cins, couts, sems):
        cps = copies(cins, couts, sems)
        for cp in cps:
            cp.wait_recv()
        for cp in cps:
            cp.wait_send()

    return dict(inputs=[pre], out_shape=[jax.ShapeDtypeStruct((3, half, c_), pre.dtype)], aliases={},
                scratch=[pltpu.SemaphoreType.DMA((3 * nch,)), pltpu.SemaphoreType.DMA((3 * nch,))],
                start=start, finish=finish)


def _sum_into_half(pre, recv, *, name):
    _, half, c_ = pre.shape
    n = recv.shape[0]
    tr = _pick(half, max(16, (1 << 19) // (2 * c_)), 16)
    per = half // tr

    def body(g_ref, r_ref, o_ref):
        acc = g_ref[...].astype(F32)
        for k in range(n):
            acc = acc + r_ref[k].astype(F32)
        o_ref[...] = acc

    return pl.pallas_call(
        body, name=name, grid=(per,),
        in_specs=[pl.BlockSpec((None, tr, c_), lambda i: (_chip_id(), i, 0)),
                  pl.BlockSpec((n, tr, c_), lambda i: (0, i, 0))],
        out_specs=pl.BlockSpec((tr, c_), lambda i: (_core_id() * per + i, 0)),
        out_shape=jax.ShapeDtypeStruct((2 * half, c_), F32), compiler_params=_params(("arbitrary",)))(pre, recv)


def _sum_slots(recv, *, name):
    n, r, c_ = recv.shape
    tr = _pick(r, max(16, (1 << 19) // (2 * c_)), 16)

    def body(r_ref, o_ref):
        acc = r_ref[0].astype(F32)
        for k in range(1, n):
            acc = acc + r_ref[k].astype(F32)
        o_ref[...] = acc

    return _rowcall(body, name, r, tr, [pl.BlockSpec((n, tr, c_), lambda i: (0, i, 0))], _row(tr, c_),
                    jax.ShapeDtypeStruct((r, c_), F32), (recv,))


def _swap_halves(buf, *, name):
    r, c_ = buf.shape
    half = r // 2
    nch = _n_chunks(half)
    rows = half // nch

    def body(in_ref, out_ref, send_sems, recv_sems):
        del in_ref
        x, y, c = _place()

        def copy(i, pc):
            part = out_ref.at[pl.ds(pc * half + i * rows, rows), :]
            return pltpu.make_async_remote_copy(
                src_ref=part, dst_ref=part, send_sem=send_sems.at[i], recv_sem=recv_sems.at[i],
                device_id=(x, y, 1 - c), device_id_type=MESH)

        sends = [copy(i, c) for i in range(nch)]
        for cp in sends:
            cp.start()
        for i in range(nch):
            copy(i, 1 - c).wait_recv()
        for cp in sends:
            cp.wait_send()

    return pl.pallas_call(
        body, name=name, out_shape=jax.ShapeDtypeStruct(buf.shape, buf.dtype),
        in_specs=[ANY], out_specs=ANY, input_output_aliases={0: 0},
        scratch_shapes=[pltpu.SemaphoreType.DMA((nch,)), pltpu.SemaphoreType.DMA((nch,))],
    )(buf)


def _prereduce(gfull, *, name):
    return _pair_add(gfull, _pair_exchange(gfull, name=name + "_pair"), name=name + "_padd")


def _finish_reduce(pre, recv, *, name):
    return _swap_halves(_sum_into_half(pre, recv, name=name + "_sum"), name=name + "_swap")


PACK_ROWS = 16


def _pack(parts):
    flat = [p.reshape(-1).astype(F32) for p in parts]
    n = sum(f.shape[0] for f in flat)
    unit = PACK_ROWS * LANES
    total = -(-n // unit) * unit
    if total > n:
        flat.append(jnp.zeros((total - n,), F32))
    where, off = [], 0
    for p in parts:
        where.append((off, p.shape))
        off += p.size
    return jnp.concatenate(flat).reshape(total // LANES, LANES), where


def _unpack(flat, where):
    v = flat.reshape(-1)
    return [v[off:off + _size(shape)].reshape(shape) for off, shape in where]


def _size(shape):
    n = 1
    for d in shape:
        n *= d
    return n


def _sample_step(x, target, mods, w_in_p, bufs, sp):
    s, d = x.shape
    u = d // 2
    h = u // HEAD_DIM
    g = h // HEADS_PER_GROUP
    pw = w_in_p.shape[1]
    din = 6 * u + 2 * h
    dff_ = bufs["w_up"].shape[2] * N_CHIPS
    shift1, scale1, gate1, shift2, scale2, gate2 = mods

    a_f = -jnp.exp(sp["ssm_a_log_f"].reshape(-1))
    a_b = -jnp.exp(sp["ssm_a_log_b"].reshape(-1))
    pad_l = LANES - 2 * h
    a_row = jnp.pad(jnp.concatenate([a_f, a_b]), (0, pad_l)).reshape(1, LANES)
    bias_row = jnp.pad(jnp.concatenate([sp["ssm_dt_bias_f"].reshape(-1), sp["ssm_dt_bias_b"].reshape(-1)]),
                       (0, pad_l)).reshape(1, LANES)
    a_col = jnp.broadcast_to(jnp.stack([a_f, a_b])[:, :, None], (2, h, LANES))
    d_e = jnp.repeat(sp["ssm_d"].reshape(-1), HEAD_DIM).reshape(1, u)
    conv_w, conv_b = sp["ssm_conv_w"], sp["ssm_conv_b"].reshape(1, 2 * u)
    sc_conv_w = sp["sc_conv_w"]
    ssm_norm_w, sc_norm_w = sp["ssm_norm_w"].reshape(1, u), sp["sc_norm_w"].reshape(1, u)
    ln1_g, ln1_b = sp["ln1_g"].reshape(1, d), sp["ln1_b"].reshape(1, d)
    ln2_g, ln2_b = sp["ln2_g"].reshape(1, d), sp["ln2_b"].reshape(1, d)

    h1 = _modulate(x, scale1, shift1, name="modulate1")
    proj, (w_up_blk,) = _matmul(h1, w_in_p, name="mm_proj", tn=1280, comm=_gather_comm(bufs["w_up"]))
    xbc = _conv_silu_fwd(proj, conv_w, conv_b, s, u)
    dt, cum, dt_e, cum_e = _dt_prep(proj, bias_row, a_row, s, u, h)
    cum_t = jnp.stack([cum[:, :h].T, cum[:, h:2 * h].T])
    dt_t = jnp.stack([dt[:, :h].T, dt[:, h:2 * h].T])
    (y2, states), (w_out_blk,) = _ssd_fwd(xbc, dt_e, cum_e, cum_t, s, h, g, comm=_gather_comm(bufs["w_out"]))
    w_out = w_out_blk.reshape(d, d)
    y_ssm = _gate_norm_fwd(y2, xbc, proj, d_e, ssm_norm_w, s, u, g)
    y_sc = _shortconv_fwd(proj, sc_conv_w, sc_norm_w, s, u)
    ymix = jnp.concatenate([y_ssm, y_sc], axis=1)
    mix = _matmul(ymix, w_out, name="mm_mix")
    x1, h2 = _ln1_fwd(x, mix, gate1, ln1_g, ln1_b, scale2, shift2)
    up, (w_down_blk,) = _matmul(h2, w_up_blk, name="mm_up", b_blocks=N_CHIPS, comm=_gather_comm(bufs["w_down"]))
    w_down = w_down_blk.reshape(dff_, d)
    ff = _relu2(up)
    f = _matmul(ff, w_down, name="mm_down")
    df, dr2, loss, dg2, db2, dgate2 = _ln2_loss_bwd(x1, f, target, gate2, ln2_g, ln2_b)

    gw_down = _matmul(ff, df, name="mm_gw_down", ta=True, out_dtype=BF16)
    pre_down = _prereduce(gw_down.reshape(N_CHIPS, dff_ // N_CHIPS, d), name="rs_w_down")
    dff, (rv_down,) = _matmul(df, w_down, name="mm_dff", tb=True, comm=_scatter_comm(pre_down))
    du = _relu2_bwd(dff, up)
    gw_up = _matmul(h2, du, name="mm_gw_up", ta=True, out_dtype=BF16, out_blocks=N_CHIPS)
    pre_up = _prereduce(gw_up, name="rs_w_up")
    dh2, (rv_up,) = _matmul(du, w_up_blk, name="mm_dh2", tb=True, b_blocks=N_CHIPS, comm=_scatter_comm(pre_up))
    dmix, dxa, dscale2, dshift2, dg1, db1, dgate1 = _ln1_bwd(dh2, dr2, x1, x, mix, scale2, gate1, ln1_g)
    gw_out = _matmul(ymix, dmix, name="mm_gw_out", ta=True, out_dtype=BF16)
    pre_out = _prereduce(gw_out.reshape(N_CHIPS, d // N_CHIPS, d), name="rs_w_out")
    dymix, (rv_out,) = _matmul(dmix, w_out, name="mm_dymix", tb=True, comm=_scatter_comm(pre_out))
    dys, dz, dxs, dnw, dd_e = _gate_norm_bwd(dymix, y2, xbc, proj, d_e, ssm_norm_w, s, u, g)
    dx2, dbb, dcc, ddt_t, da = _ssd_bwd(xbc, dt_e, cum_e, cum_t, dt_t, a_col, dys, states, s, h, g)
    dxbc = jnp.concatenate([dx2[0] + dx2[1] + dxs, dbb[0] + dbb[1], dcc[0] + dcc[1]], axis=1)
    du_xbc, dcw, dcb = _conv_silu_bwd(proj, dxbc, conv_w, conv_b, s, u)
    ddt = jnp.pad(jnp.concatenate([ddt_t[0].T, ddt_t[1].T], axis=1), ((0, 0), (0, pad_l)))
    ddt_raw, dbias = _dt_bwd(ddt, proj, bias_row, s, u, h)
    dh_sc, db_sc, dc_sc, dscw, dscnw = _shortconv_bwd(dymix, proj, sc_conv_w, sc_norm_w, s, u)
    dproj = jnp.concatenate([dz, du_xbc, dh_sc, db_sc, dc_sc, ddt_raw,
                             jnp.zeros((s, pw - 6 * u - LANES), BF16)], axis=1)
    gp = _matmul(h1, dproj, name="mm_gw_in", ta=True, out_dtype=BF16, tn=1280)
    g_in_full = jnp.concatenate([gp[:, :3 * u], gp[:, 6 * u:6 * u + 2 * h], gp[:, 3 * u:6 * u]], axis=1)
    g_in_blk = jnp.transpose(g_in_full.reshape(d, N_CHIPS, din // N_CHIPS), (1, 0, 2))
    pre_in = _prereduce(g_in_blk, name="rs_w_in")
    dh1, (rv_in,) = _matmul(dproj, w_in_p, name="mm_dh1", tb=True, tk=1280, comm=_scatter_comm(pre_in))
    grad_x, dscale1, dshift1 = _dx_final(dxa, dh1, x, scale1)
    big = {"w_down": _finish_reduce(pre_down, rv_down, name="rs_w_down"),
           "w_up": _finish_reduce(pre_up, rv_up, name="rs_w_up"),
           "w_out": _finish_reduce(pre_out, rv_out, name="rs_w_out"),
           "w_in": _finish_reduce(pre_in, rv_in, name="rs_w_in")}

    small = {
        "dmod": jnp.concatenate([dshift1, dscale1, dgate1, dshift2, dscale2, dgate2], axis=1),
        "ssm_conv_b": dcb,
        "ssm_dt_bias_f": dbias[0, :h], "ssm_dt_bias_b": dbias[0, h:2 * h],
        "ssm_a_log_f": jnp.sum(da[0], axis=1) * a_f, "ssm_a_log_b": jnp.sum(da[1], axis=1) * a_b,
        "ssm_d": jnp.sum(dd_e.reshape(h, HEAD_DIM), axis=1),
        "ssm_norm_w": dnw, "sc_norm_w": dscnw,
        "ln1_g": dg1, "ln1_b": db1, "ln2_g": dg2, "ln2_b": db2,
        "ssm_conv_w": dcw[:SSM_CONV], "sc_conv_w": dscw[:SC_CONV],
    }
    return loss, grad_x, big, small


WEIGHTS = ['w_ada', 'b_ada', 'w_in', 'ssm_conv_w', 'ssm_conv_b', 'ssm_dt_bias_f', 'ssm_dt_bias_b', 'ssm_a_log_f',
           'ssm_a_log_b', 'ssm_d', 'ssm_norm_w', 'sc_conv_w', 'sc_norm_w', 'w_out', 'ln1_g', 'ln1_b', 'w_up', 'w_down',
           'ln2_g', 'ln2_b']
BIG = ('w_ada', 'w_in', 'w_out', 'w_up', 'w_down')
SMALL = tuple(n for n in WEIGHTS if n not in BIG)
SMALL_SHARDED = ('ssm_conv_w', 'sc_conv_w')


def _p_layout_width(u):
    return -(-(6 * u + LANES) // 512) * 512


def kernel(x, c, w_ada, b_ada, w_in, ssm_conv_w, ssm_conv_b, ssm_dt_bias_f, ssm_dt_bias_b, ssm_a_log_f, ssm_a_log_b, ssm_d, ssm_norm_w, sc_conv_w, sc_norm_w, w_out, ln1_g, ln1_b, w_up, w_down, ln2_g, ln2_b, loss_target, m_w_ada, m_b_ada, m_w_in, m_ssm_conv_w, m_ssm_conv_b, m_ssm_dt_bias_f, m_ssm_dt_bias_b, m_ssm_a_log_f, m_ssm_a_log_b, m_ssm_d, m_ssm_norm_w, m_sc_conv_w, m_sc_norm_w, m_w_out, m_ln1_g, m_ln1_b, m_w_up, m_w_down, m_ln2_g, m_ln2_b, v_w_ada, v_b_ada, v_w_in, v_ssm_conv_w, v_ssm_conv_b, v_ssm_dt_bias_f, v_ssm_dt_bias_b, v_ssm_a_log_f, v_ssm_a_log_b, v_ssm_d, v_ssm_norm_w, v_sc_conv_w, v_sc_norm_w, v_w_out, v_ln1_g, v_ln1_b, v_w_up, v_w_down, v_ln2_g, v_ln2_b):
    given = dict(locals())
    w = {n: given[n][0] for n in WEIGHTS}
    m = {n: given["m_" + n][0] for n in WEIGHTS}
    v = {n: given["v_" + n][0] for n in WEIGHTS}
    xs, tgt = x[0], loss_target[0]
    s, d = xs.shape
    u = d // 2
    h = u // HEAD_DIM
    nmod = N_MOD * d
    nmod_loc = nmod // N_CHIPS
    ax, ay, ac = lax.axis_index("x"), lax.axis_index("y"), lax.axis_index("c")
    chip = 2 * ax + ay
    me = 2 * chip + ac

    pay1, where1 = _pack([c[0], w["ssm_conv_w"], w["sc_conv_w"]])
    g1 = _allgather_small(pay1, name="ag_inputs").reshape(N_DEV, -1)
    per_dev = [_unpack(g1[k], where1) for k in range(N_DEV)]
    c_all = jnp.stack([p[0] for p in per_dev])
    ssm_conv_w_full = jnp.concatenate([per_dev[2 * k][1] for k in range(N_CHIPS)], axis=1)
    sc_conv_w_full = jnp.concatenate([per_dev[2 * k][2] for k in range(N_CHIPS)], axis=1)

    sc_all = _silu(c_all)
    sc16 = jnp.pad(sc_all, ((0, 16 - N_DEV), (0, 0)))
    b_loc = lax.dynamic_slice(w["b_ada"], (chip * nmod_loc,), (nmod_loc,))
    mod_loc = _matmul(sc16, w["w_ada"], name="mm_mod")[:N_DEV] + b_loc[None, :]
    pay2, where2 = _pack([mod_loc])
    g2 = _allgather_small(pay2, name="ag_mod").reshape(N_DEV, -1)
    mod_blocks = jnp.stack([_unpack(g2[2 * k], where2)[0] for k in range(N_CHIPS)])
    mod_mine = lax.dynamic_index_in_dim(mod_blocks, me, axis=1, keepdims=False).reshape(N_MOD, 1, d)
    mods = [mod_mine[k] for k in range(N_MOD)]

    din = w["w_in"].shape[1] * N_CHIPS

    g_in = _gather_weight(_cast_into_block(w["w_in"], name="cast_w_in"), name="gather_w_in")
    w_in_full = jnp.concatenate([g_in[k] for k in range(N_CHIPS)], axis=1)
    pw = _p_layout_width(u)
    w_in_p = jnp.concatenate([w_in_full[:, :3 * u], w_in_full[:, 3 * u + 2 * h:], w_in_full[:, 3 * u:3 * u + 2 * h],
                              jnp.zeros((d, pw - din), BF16)], axis=1)
    bufs = {n: _cast_into_block(w[n], name="cast_" + n) for n in ("w_out", "w_up", "w_down")}

    sp = {n: w[n] for n in SMALL}
    sp["ssm_conv_w"], sp["sc_conv_w"] = ssm_conv_w_full, sc_conv_w_full
    loss_loc, grad_x, big, small = _sample_step(xs, tgt, mods, w_in_p, bufs, sp)

    small_names = ["dmod"] + [n for n in SMALL if n != "b_ada"]
    pay3, where3 = _pack([loss_loc] + [small[n] for n in small_names])
    g3 = _allgather_small(pay3, name="ag_small_grads")
    tot = _unpack(_sum_slots(g3.reshape(N_DEV, -1, LANES), name="sum_small_grads"), where3)
    loss = tot[0].reshape(())
    gsum = dict(zip(small_names, tot[1:]))
    dmod_all = jnp.stack([_unpack(g3.reshape(N_DEV, -1)[k], where3)[1].reshape(-1) for k in range(N_DEV)])

    grads = {}
    grads["b_ada"] = gsum["dmod"].reshape(-1)
    for n in SMALL:
        if n in SMALL_SHARDED:
            loc = w[n].shape[1]
            grads[n] = lax.dynamic_slice_in_dim(gsum[n], chip * loc, loc, axis=1)
        elif n != "b_ada":
            grads[n] = gsum[n].reshape(w[n].shape)

    grads.update(big)

    delta, new_m, new_v = {}, {}, {}
    dm_loc = lax.dynamic_slice_in_dim(dmod_all, chip * nmod_loc, nmod_loc, axis=1)
    grads["w_ada"], delta["w_ada"], new_m["w_ada"], new_v["w_ada"] = _adam_outer(
        w["w_ada"], sc16.T, jnp.pad(dm_loc, ((0, 16 - N_DEV), (0, 0))), m["w_ada"], v["w_ada"], name="adam_w_ada")
    for n in ("w_in", "w_out", "w_up", "w_down"):
        delta[n], new_m[n], new_v[n] = _adam(w[n], grads[n], m[n], v[n], name="adam_" + n)
    pw_, where_s = _pack([w[n] for n in SMALL])
    pg_, _ = _pack([grads[n] for n in SMALL])
    pm_, _ = _pack([m[n] for n in SMALL])
    pv_, _ = _pack([v[n] for n in SMALL])
    sd, sm, sv = _adam(pw_, pg_, pm_, pv_, name="adam_small")
    for n, a, b_, c_ in zip(SMALL, _unpack(sd, where_s), _unpack(sm, where_s), _unpack(sv, where_s)):
        delta[n], new_m[n], new_v[n] = a, b_, c_

    def lead(t):
        return t[None]

    return (loss, grad_x[None], *[lead(grads[n].reshape(w[n].shape)) for n in WEIGHTS],
            *[lead(delta[n]) for n in WEIGHTS], *[lead(new_m[n]) for n in WEIGHTS], *[lead(new_v[n]) for n in WEIGHTS])
```

```python
import functools

import jax
import jax.numpy as jnp
from jax import lax
from jax.experimental import pallas as pl
from jax.experimental.pallas import tpu as pltpu

F32 = jnp.float32
BF16 = jnp.bfloat16

CHUNK = 128
HEAD_DIM = 64
STATE = 128
HEADS_PER_GROUP = 4
SC_GROUP_WIDTH = 128
SSM_CONV = 5
SC_CONV = 3
N_MOD = 6
DEEPNORM_ALPHA = 2.0 ** 0.25
LN_EPS = 1e-5
RMS_EPS = 1e-5
ADAM_LR = 0.001
ADAM_B1 = 0.9
ADAM_B2 = 0.999
ADAM_EPS = 1e-08
ADAM_WD = 0.01
ADAM_STEP = 10
N_CHIPS = 4
N_DEV = 8
LANES = 128
SUBLANES = 8
HALO = 8
VMEM_LIMIT = 56 * 1024 * 1024
MESH = pl.DeviceIdType.MESH


def _params(sem=None):
    return pltpu.CompilerParams(dimension_semantics=sem, vmem_limit_bytes=VMEM_LIMIT)


def _pick(n, target, mult=LANES):
    best = None
    t = mult
    while t <= min(n, target):
        if n % t == 0:
            best = t
        t += mult
    return best if best is not None else n


ROW_TILE_BYTES = 1 << 20


def _tile_rows(s, width, mult=SUBLANES):
    return _pick(s, max(mult, ROW_TILE_BYTES // (4 * width)), mult)


def _sigmoid(v):
    return 1.0 / (1.0 + jnp.exp(-v))


def _silu(v):
    return v * _sigmoid(v)


def _dsilu(v):
    s = _sigmoid(v)
    return s * (1.0 + v * (1.0 - s))


def _softplus(v):
    e = jnp.exp(-jnp.abs(v))
    return jnp.maximum(v, 0.0) + jnp.where(e < 1e-4, e - 0.5 * e * e, jnp.log(1.0 + e))


def _dot(a, b):
    return jnp.dot(a, b, preferred_element_type=F32)


def _dot_nt(a, b):
    return lax.dot_general(a, b, (((1,), (1,)), ((), ())), preferred_element_type=F32)


def _dot_tn(a, b):
    return lax.dot_general(a, b, (((0,), (0,)), ((), ())), preferred_element_type=F32)


def _split3(v):
    hi = v.astype(BF16)
    r1 = v - hi.astype(F32)
    mid = r1.astype(BF16)
    lo = (r1 - mid.astype(F32)).astype(BF16)
    return hi, mid, lo


def _dot3_r(v, onehot):
    hi, mid, lo = _split3(v)
    return _dot(hi, onehot) + _dot(mid, onehot) + _dot(lo, onehot)


def _dot3_l(onehot, v):
    hi, mid, lo = _split3(v)
    return _dot(onehot, hi) + _dot(onehot, mid) + _dot(onehot, lo)


MATMUL_VMEM_BUDGET = 44 * 1024 * 1024


def _matmul(a, b, *, name, ta=False, tb=False, out_dtype=F32, b_blocks=1, out_blocks=1,
            tm=1024, tn=1024, tk=4096, comm=None, epilogue=None, extra=None):
    if ta:
        K, M = a.shape
    else:
        M, K = a.shape
    if b_blocks > 1:
        nb, r_, c_ = b.shape
        if tb:
            N, K2 = r_, c_ * nb
        else:
            K2, N = r_, c_ * nb
    else:
        if tb:
            N, K2 = b.shape
        else:
            K2, N = b.shape
    assert K == K2, (a.shape, b.shape, ta, tb)
    assert not (ta and tb)
    n_unit = N // b_blocks if (b_blocks > 1 and not tb) else N
    n_unit = min(n_unit, N // out_blocks)
    tn = _pick(n_unit, tn)
    k_unit = K // b_blocks if (b_blocks > 1 and tb) else K
    tk = _pick(k_unit, tk)
    tile_bytes = {None: jnp.dtype(out_dtype).itemsize, "relu2": 6, "relu2_bwd": 6}[epilogue]

    def vmem_need(tm_):
        need = 2 * (tm_ * tk * a.dtype.itemsize + tk * tn * b.dtype.itemsize) + 2 * tm_ * tn * tile_bytes
        return need + (tm_ * tn * 4 if K > tk else 0)

    tm = _pick(M, tm)
    while vmem_need(tm) > MATMUL_VMEM_BUDGET and tm % 2 == 0 and tm // 2 >= LANES:
        tm //= 2
    gm, gn, gk = M // tm, N // tn, K // tk

    if ta:
        a_spec = pl.BlockSpec((tk, tm), lambda i, j, k: (k, i))
    else:
        a_spec = pl.BlockSpec((tm, tk), lambda i, j, k: (i, k))
    if b_blocks > 1 and not tb:
        per = (N // b_blocks) // tn
        b_spec = pl.BlockSpec((None, tk, tn), lambda i, j, k: (j // per, k, j % per))
    elif b_blocks > 1 and tb:
        per = (K // b_blocks) // tk
        b_spec = pl.BlockSpec((None, tn, tk), lambda i, j, k: (k // per, j, k % per))
    elif tb:
        b_spec = pl.BlockSpec((tn, tk), lambda i, j, k: (j, k))
    else:
        b_spec = pl.BlockSpec((tk, tn), lambda i, j, k: (k, j))
    if out_blocks > 1:
        per_o = (N // out_blocks) // tn
        o_spec = pl.BlockSpec((None, tm, tn), lambda i, j, k: (j // per_o, i, j % per_o))
        o_shape = jax.ShapeDtypeStruct((out_blocks, M, N // out_blocks), out_dtype)
    else:
        o_spec = pl.BlockSpec((tm, tn), lambda i, j, k: (i, j))
        o_shape = jax.ShapeDtypeStruct((M, N), out_dtype)

    in_specs, args = [a_spec, b_spec], [a, b]
    out_specs, out_shape = [o_spec], [o_shape]
    if epilogue == "relu2":
        assert out_blocks == 1 and out_dtype == F32
        out_specs.append(o_spec)
        out_shape.append(jax.ShapeDtypeStruct((M, N), BF16))
    elif epilogue == "relu2_bwd":
        assert out_blocks == 1 and out_dtype == BF16
        in_specs.append(o_spec)
        args.append(extra)
    n_in = len(in_specs)

    def write(refs, p):
        o_ref = refs[n_in]
        if epilogue == "relu2":
            o_ref[...] = p
            r = jnp.maximum(p, 0.0)
            refs[n_in + 1][...] = (r * r).astype(BF16)
        elif epilogue == "relu2_bwd":
            o_ref[...] = (p * 2.0 * jnp.maximum(refs[2][...], 0.0)).astype(BF16)
        else:
            o_ref[...] = p.astype(out_dtype)

    def body(*refs):
        av = refs[0][...].astype(BF16)
        bv = refs[1][...].astype(BF16)
        p = _dot_tn(av, bv) if ta else (_dot_nt(av, bv) if tb else _dot(av, bv))
        if gk == 1:
            write(refs, p)
            return
        acc = refs[-1]
        k = pl.program_id(2)

        @pl.when(k == 0)
        def _():
            acc[...] = p

        @pl.when(jnp.logical_and(k > 0, k < gk - 1))
        def _():
            acc[...] += p

        @pl.when(k == gk - 1)
        def _():
            write(refs, acc[...] + p)

    scratch = [pltpu.VMEM((tm, tn), F32)] if gk > 1 else []
    if comm is not None:
        outs, landed = _comm_call(body, name=name, grid=(gm, gn, gk), in_specs=in_specs, out_specs=out_specs,
                                  out_shape=out_shape, scratch_shapes=scratch, args=args, comm=comm)
        return (outs[0] if len(outs) == 1 else tuple(outs)), landed
    outs = pl.pallas_call(
        body, name=name, grid=(gm, gn, gk), in_specs=in_specs, out_specs=out_specs,
        out_shape=out_shape, scratch_shapes=scratch,
        compiler_params=_params(("parallel", "parallel", "arbitrary")),
    )(*args)
    return outs[0] if len(outs) == 1 else tuple(outs)


def _comm_call(body, *, name, grid, in_specs, out_specs, out_shape, scratch_shapes, args, comm):
    n_in, n_out, n_scr = len(in_specs), len(out_shape), len(scratch_shapes)
    c_in, c_out = list(comm["inputs"]), list(comm["out_shape"])
    nci, nco = len(c_in), len(c_out)
    hbm = pl.BlockSpec(memory_space=pl.ANY)

    def body2(*refs):
        ins, cins = refs[:n_in], refs[n_in:n_in + nci]
        o0 = n_in + nci
        outs, couts = refs[o0:o0 + n_out], refs[o0 + n_out:o0 + n_out + nco]
        s0 = o0 + n_out + nco
        scr, cscr = refs[s0:s0 + n_scr], refs[s0 + n_scr:]
        first = functools.reduce(jnp.logical_and, [pl.program_id(a) == 0 for a in range(len(grid))])
        last = functools.reduce(jnp.logical_and, [pl.program_id(a) == grid[a] - 1 for a in range(len(grid))])

        @pl.when(first)
        def _():
            comm["start"](cins, couts, cscr)

        body(*ins, *outs, *scr)

        @pl.when(last)
        def _():
            comm["finish"](cins, couts, cscr)

    res = pl.pallas_call(
        body2, name=name, grid=grid, in_specs=list(in_specs) + [hbm] * nci, out_specs=list(out_specs) + [hbm] * nco,
        out_shape=list(out_shape) + c_out, scratch_shapes=list(scratch_shapes) + list(comm["scratch"]),
        input_output_aliases={n_in + k: n_out + v for k, v in comm.get("aliases", {}).items()},
        compiler_params=_params(("arbitrary",) * len(grid)),
    )(*args, *c_in)
    return res[:n_out], res[n_out:]


def _row(tr, w, blk=0):
    return pl.BlockSpec((tr, w), lambda i: (i, blk))


def _full(shape):
    nd = len(shape)
    return pl.BlockSpec(shape, lambda i: (0,) * nd)


def _halo_specs(s, tr, w, blk=0):
    per = tr // HALO
    last = s // HALO - 1
    return [
        pl.BlockSpec((HALO, w), lambda i: (jnp.maximum(i * per - 1, 0), blk)),
        pl.BlockSpec((tr, w), lambda i: (i, blk)),
        pl.BlockSpec((HALO, w), lambda i: (jnp.minimum((i + 1) * per, last), blk)),
    ]


def _ext(prev_ref, cur_ref, next_ref, s, tr):
    i = pl.program_id(0)
    e = jnp.concatenate([prev_ref[...].astype(F32), cur_ref[...].astype(F32), next_ref[...].astype(F32)], axis=0)
    rid = i * tr - HALO + lax.broadcasted_iota(jnp.int32, e.shape, 0)
    return jnp.where((rid >= 0) & (rid < s), e, 0.0)


def _valid_rows(shape, s, tr):
    i = pl.program_id(0)
    rid = i * tr - HALO + lax.broadcasted_iota(jnp.int32, shape, 0)
    return (rid >= 0) & (rid < s)


def _shift(e, k):
    if k == 0:
        return e
    n = e.shape[0]
    return pltpu.roll(e, (n - k) % n, 0)


def _acc_rows(ref, v):
    s = jnp.sum(v, axis=0, keepdims=True)

    @pl.when(pl.program_id(0) == 0)
    def _():
        ref[...] = s

    @pl.when(pl.program_id(0) > 0)
    def _():
        ref[...] += s


def _rowcall(body, name, s, tr, in_specs, out_specs, out_shape, args):
    return pl.pallas_call(
        body, name=name, grid=(s // tr,), in_specs=in_specs, out_specs=out_specs, out_shape=out_shape,
        compiler_params=_params(("arbitrary",)),
    )(*args)


def _modulate(x, scale, shift, *, name):
    s, d = x.shape
    tr = _tile_rows(s, d)

    def body(x_ref, sc_ref, sh_ref, o_ref):
        o_ref[...] = (x_ref[...] * (1.0 + sc_ref[...]) + sh_ref[...]).astype(BF16)

    return _rowcall(body, name, s, tr, [_row(tr, d), _full((1, d)), _full((1, d))], _row(tr, d),
                    jax.ShapeDtypeStruct((s, d), BF16), (x, scale, shift))


def _ln_stats(r):
    mu = jnp.mean(r, axis=-1, keepdims=True)
    xc = r - mu
    var = jnp.mean(xc * xc, axis=-1, keepdims=True)
    rstd = lax.rsqrt(var + LN_EPS)
    return xc * rstd, rstd


def _ln1_fwd(x, mix, gate, g, b, scale2, shift2):
    s, d = x.shape
    tr = _tile_rows(s, d)

    def body(x_ref, m_ref, gt_ref, g_ref, b_ref, sc_ref, sh_ref, x1_ref, h2_ref):
        r = DEEPNORM_ALPHA * x_ref[...] + (1.0 + gt_ref[...]) * m_ref[...]
        xh, _ = _ln_stats(r)
        x1 = xh * g_ref[...] + b_ref[...]
        x1_ref[...] = x1
        h2_ref[...] = (x1 * (1.0 + sc_ref[...]) + sh_ref[...]).astype(BF16)

    v = _full((1, d))
    return _rowcall(body, "ln1_fwd", s, tr, [_row(tr, d), _row(tr, d), v, v, v, v, v],
                    [_row(tr, d), _row(tr, d)],
                    [jax.ShapeDtypeStruct((s, d), F32), jax.ShapeDtypeStruct((s, d), BF16)],
                    (x, mix, gate, g, b, scale2, shift2))


def _ln2_loss_bwd(x1, f, target, gate, g, b):
    s, d = x1.shape
    tr = _tile_rows(s, d)

    def body(x1_ref, f_ref, t_ref, gt_ref, g_ref, b_ref, df_ref, dr_ref, loss_ref, dg_ref, db_ref, dgt_ref):
        fv = f_ref[...]
        r = DEEPNORM_ALPHA * x1_ref[...] + (1.0 + gt_ref[...]) * fv
        xh, rstd = _ln_stats(r)
        y = xh * g_ref[...] + b_ref[...]
        err = y - t_ref[...]
        _acc_rows(loss_ref, 0.5 * jnp.mean(err * err, axis=-1, keepdims=True))
        dy = err * (1.0 / d)
        _acc_rows(dg_ref, dy * xh)
        _acc_rows(db_ref, dy)
        dxh = dy * g_ref[...]
        dr = rstd * (dxh - jnp.mean(dxh, axis=-1, keepdims=True) - xh * jnp.mean(dxh * xh, axis=-1, keepdims=True))
        dr_ref[...] = dr
        df_ref[...] = ((1.0 + gt_ref[...]) * dr).astype(BF16)
        _acc_rows(dgt_ref, dr * fv)

    v = _full((1, d))
    one = _full((1, 1))
    return _rowcall(body, "ln2_loss_bwd", s, tr, [_row(tr, d), _row(tr, d), _row(tr, d), v, v, v],
                    [_row(tr, d), _row(tr, d), one, v, v, v],
                    [jax.ShapeDtypeStruct((s, d), BF16), jax.ShapeDtypeStruct((s, d), F32),
                     jax.ShapeDtypeStruct((1, 1), F32)] + [jax.ShapeDtypeStruct((1, d), F32)] * 3,
                    (x1, f, target, gate, g, b))


def _ln1_bwd(dh2, dr2, x1, x, mix, scale2, gate1, g1):
    s, d = x.shape
    tr = _tile_rows(s, d)

    def body(dh_ref, dr2_ref, x1_ref, x_ref, m_ref, sc_ref, gt_ref, g_ref,
             dm_ref, dxa_ref, dsc_ref, dsh_ref, dg_ref, db_ref, dgt_ref):
        dh = dh_ref[...]
        _acc_rows(dsc_ref, dh * x1_ref[...])
        _acc_rows(dsh_ref, dh)
        dy = dh * (1.0 + sc_ref[...]) + DEEPNORM_ALPHA * dr2_ref[...]
        mv = m_ref[...]
        r = DEEPNORM_ALPHA * x_ref[...] + (1.0 + gt_ref[...]) * mv
        xh, rstd = _ln_stats(r)
        _acc_rows(dg_ref, dy * xh)
        _acc_rows(db_ref, dy)
        dxh = dy * g_ref[...]
        dr = rstd * (dxh - jnp.mean(dxh, axis=-1, keepdims=True) - xh * jnp.mean(dxh * xh, axis=-1, keepdims=True))
        dm_ref[...] = ((1.0 + gt_ref[...]) * dr).astype(BF16)
        dxa_ref[...] = DEEPNORM_ALPHA * dr
        _acc_rows(dgt_ref, dr * mv)

    v = _full((1, d))
    return _rowcall(body, "ln1_bwd", s, tr, [_row(tr, d)] * 5 + [v, v, v],
                    [_row(tr, d), _row(tr, d), v, v, v, v, v],
                    [jax.ShapeDtypeStruct((s, d), BF16), jax.ShapeDtypeStruct((s, d), F32)]
                    + [jax.ShapeDtypeStruct((1, d), F32)] * 5,
                    (dh2, dr2, x1, x, mix, scale2, gate1, g1))


def _dx_final(dxa, dh1, x, scale1):
    s, d = x.shape
    tr = _tile_rows(s, d)

    def body(a_ref, dh_ref, x_ref, sc_ref, o_ref, dsc_ref, dsh_ref):
        dh = dh_ref[...]
        o_ref[...] = a_ref[...] + dh * (1.0 + sc_ref[...])
        _acc_rows(dsc_ref, dh * x_ref[...])
        _acc_rows(dsh_ref, dh)

    v = _full((1, d))
    return _rowcall(body, "dx_final", s, tr, [_row(tr, d)] * 3 + [v], [_row(tr, d), v, v],
                    [jax.ShapeDtypeStruct((s, d), F32)] + [jax.ShapeDtypeStruct((1, d), F32)] * 2,
                    (dxa, dh1, x, scale1))


def _conv_silu_fwd(proj, conv_w, conv_b, s, u):
    tr = _tile_rows(s, u)
    w = 2 * u
    half = SSM_CONV // 2

    def body(p0, c0, n0, p1, c1, n1, w_ref, b_ref, o_ref):
        for blk, (pr, cr, nr) in enumerate(((p0, c0, n0), (p1, c1, n1))):
            e = _ext(pr, cr, nr, s, tr)
            wv = w_ref[:, blk * u:(blk + 1) * u]
            acc = jnp.zeros_like(e)
            for k in range(SSM_CONV):
                acc = acc + _shift(e, k - half) * wv[k:k + 1, :]
            pre = acc[HALO:HALO + tr] + b_ref[:, blk * u:(blk + 1) * u]
            o_ref[:, blk * u:(blk + 1) * u] = _silu(pre)

    in_specs = _halo_specs(s, tr, u, 1) + _halo_specs(s, tr, u, 2) + [_full((SSM_CONV, w)), _full((1, w))]
    return _rowcall(body, "conv_silu_fwd", s, tr, in_specs, _row(tr, w), jax.ShapeDtypeStruct((s, w), F32),
                    (proj,) * 6 + (conv_w, conv_b))


def _conv_silu_bwd(proj, dxbc, conv_w, conv_b, s, u):
    tr = _tile_rows(s, u)
    w = 2 * u
    half = SSM_CONV // 2

    def body(p0, c0, n0, p1, c1, n1, dp0, dc0, dn0, dp1, dc1, dn1, w_ref, b_ref, du_ref, dw_ref, db_ref):
        for blk, (ur, dr) in enumerate((((p0, c0, n0), (dp0, dc0, dn0)), ((p1, c1, n1), (dp1, dc1, dn1)))):
            e = _ext(*ur, s, tr)
            de = _ext(*dr, s, tr)
            wv = w_ref[:, blk * u:(blk + 1) * u]
            acc = jnp.zeros_like(e)
            for k in range(SSM_CONV):
                acc = acc + _shift(e, k - half) * wv[k:k + 1, :]
            pre = acc + b_ref[:, blk * u:(blk + 1) * u]
            dpre = jnp.where(_valid_rows(e.shape, s, tr), de * _dsilu(pre), 0.0)
            du = jnp.zeros_like(e)
            rows = []
            for k in range(SSM_CONV):
                du = du + _shift(dpre, half - k) * wv[k:k + 1, :]
                rows.append(jnp.sum((_shift(e, k - half) * dpre)[HALO:HALO + tr], axis=0, keepdims=True))
            du_ref[:, blk * u:(blk + 1) * u] = du[HALO:HALO + tr].astype(BF16)
            dwv = jnp.concatenate(rows + [jnp.zeros((SUBLANES - SSM_CONV, u), F32)], axis=0)
            dbv = jnp.sum(dpre[HALO:HALO + tr], axis=0, keepdims=True)
            first = pl.program_id(0) == 0

            @pl.when(first)
            def _():
                dw_ref[:, blk * u:(blk + 1) * u] = dwv
                db_ref[:, blk * u:(blk + 1) * u] = dbv

            @pl.when(jnp.logical_not(first))
            def _():
                dw_ref[:, blk * u:(blk + 1) * u] += dwv
                db_ref[:, blk * u:(blk + 1) * u] += dbv

    in_specs = (_halo_specs(s, tr, u, 1) + _halo_specs(s, tr, u, 2) + _halo_specs(s, tr, u, 0)
                + _halo_specs(s, tr, u, 1) + [_full((SSM_CONV, w)), _full((1, w))])
    return _rowcall(body, "conv_silu_bwd", s, tr, in_specs,
                    [_row(tr, w), _full((SUBLANES, w)), _full((1, w))],
                    [jax.ShapeDtypeStruct((s, w), BF16), jax.ShapeDtypeStruct((SUBLANES, w), F32),
                     jax.ShapeDtypeStruct((1, w), F32)],
                    (proj,) * 6 + (dxbc,) * 6 + (conv_w, conv_b))


def _expanders(h):
    col64 = jnp.arange(2 * h * HEAD_DIM) // HEAD_DIM
    col128 = jnp.arange(2 * h * LANES) // LANES
    row = jnp.arange(LANES)[:, None]
    return (row == col64[None, :]).astype(BF16), (row == col128[None, :]).astype(BF16)


def _dt_prep(proj, bias_row, a_row, s, u, h):
    q = CHUNK
    e64, e128 = _expanders(h)
    ds = h * HEAD_DIM
    dtblk = (6 * u) // LANES

    def body(raw_ref, b_ref, a_ref, e64_ref, e128_ref, dt_ref, cum_ref, dte_ref, cume_ref):
        lane = lax.broadcasted_iota(jnp.int32, (q, LANES), 1)
        dt = jnp.where(lane < 2 * h, _softplus(raw_ref[...] + b_ref[...]), 0.0)
        da = dt * a_ref[...]
        ii = lax.broadcasted_iota(jnp.int32, (q, q), 0)
        kk = lax.broadcasted_iota(jnp.int32, (q, q), 1)
        lower = (kk <= ii).astype(F32).astype(BF16)
        upper = (kk >= ii).astype(F32).astype(BF16)
        cum = jnp.where(lane < h, _dot3_l(lower, da), _dot3_l(upper, da))
        dt_ref[...] = dt
        cum_ref[...] = cum
        dte = _dot3_r(dt, e64_ref[...])
        cume = _dot3_r(cum, e128_ref[...])
        dte_ref[0] = dte[:, :ds]
        dte_ref[1] = dte[:, ds:]
        cume_ref[0] = cume[:, :h * LANES]
        cume_ref[1] = cume[:, h * LANES:]

    in_specs = [pl.BlockSpec((q, LANES), lambda i: (i, dtblk)), _full((1, LANES)), _full((1, LANES)),
                _full(e64.shape), _full(e128.shape)]
    out_specs = [_row(q, LANES), _row(q, LANES),
                 pl.BlockSpec((2, q, ds), lambda i: (0, i, 0)), pl.BlockSpec((2, q, h * LANES), lambda i: (0, i, 0))]
    out_shape = [jax.ShapeDtypeStruct((s, LANES), F32), jax.ShapeDtypeStruct((s, LANES), F32),
                 jax.ShapeDtypeStruct((2, s, ds), F32), jax.ShapeDtypeStruct((2, s, h * LANES), F32)]
    return _rowcall(body, "dt_prep", s, q, in_specs, out_specs, out_shape, (proj, bias_row, a_row, e64, e128))


def _ssd_specs(s, h, g):
    q = CHUNK
    nc = s // q
    ds = h * HEAD_DIM
    nb = g * STATE
    return q, nc, ds, nb


def _ssd_fwd(xbc, dt_e, cum_e, cum_t, s, h, g, comm=None):
    q, nc, ds, nb = _ssd_specs(s, h, g)
    npair = h // 2

    def cidx(d, i):
        return jnp.where(d == 0, i, nc - 1 - i)

    def body(x_ref, b_ref, c_ref, dt_ref, cum_ref, cumt_ref, y_ref, sp_ref, st):
        d = pl.program_id(0)
        i = pl.program_id(1)

        @pl.when(i == 0)
        def _():
            st[...] = jnp.zeros_like(st)

        rev = d == 1
        ii = lax.broadcasted_iota(jnp.int32, (q, q), 0)
        jj = lax.broadcasted_iota(jnp.int32, (q, q), 1)
        sgn = jnp.where(rev, -1, 1)
        mask = (jj - ii) * sgn <= 0
        left = lax.broadcasted_iota(jnp.int32, (q, LANES), 1) < HEAD_DIM

        def group(gi, carry):
            goff = pl.multiple_of(gi * STATE, STATE)
            cg = c_ref[:, pl.ds(goff, STATE)].astype(BF16)
            bg = b_ref[:, pl.ds(goff, STATE)].astype(BF16)
            gm = _dot_nt(cg, bg)
            for p in range(HEADS_PER_GROUP // 2):
                pr = gi * (HEADS_PER_GROUP // 2) + p
                off = pl.multiple_of(pr * LANES, LANES)
                xd = x_ref[:, pl.ds(off, LANES)] * dt_ref[:, pl.ds(off, LANES)]
                ms = []
                cols = []
                for hl in range(2):
                    hh = 2 * pr + hl
                    col = cum_ref[:, pl.ds(pl.multiple_of(hh * LANES, LANES), LANES)]
                    row = cumt_ref[pl.ds(hh, 1), :]
                    lm = jnp.where(mask, jnp.exp(jnp.minimum(col - row, 0.0)), 0.0)
                    ms.append((gm * lm).astype(BF16))
                    cols.append(col)
                y = _dot(ms[0], jnp.where(left, xd, 0.0).astype(BF16)) + _dot(ms[1], jnp.where(left, 0.0, xd).astype(BF16))
                ce = jnp.where(left, cols[0], cols[1])
                sprev = st[pr]
                sp_ref[pr] = sprev
                y = y + jnp.exp(ce) * _dot(cg, sprev.astype(BF16))
                y_ref[:, pl.ds(off, LANES)] = y
                tot = jnp.where(rev, ce[0:1, :], ce[q - 1:q, :])
                v = (xd * jnp.exp(tot - ce)).astype(BF16)
                st[pr] = jnp.exp(tot) * sprev + _dot_tn(bg, v)
            return carry

        lax.fori_loop(0, g, group, 0)

    in_specs = [
        pl.BlockSpec((q, ds), lambda d, i: (cidx(d, i), 0)),
        pl.BlockSpec((q, nb), lambda d, i: (cidx(d, i), ds // nb)),
        pl.BlockSpec((q, nb), lambda d, i: (cidx(d, i), ds // nb + 1)),
        pl.BlockSpec((None, q, ds), lambda d, i: (d, cidx(d, i), 0)),
        pl.BlockSpec((None, q, h * LANES), lambda d, i: (d, cidx(d, i), 0)),
        pl.BlockSpec((None, h, q), lambda d, i: (d, 0, cidx(d, i))),
    ]
    out_specs = [
        pl.BlockSpec((None, q, ds), lambda d, i: (d, cidx(d, i), 0)),
        pl.BlockSpec((None, None, npair, STATE, LANES), lambda d, i: (d, cidx(d, i), 0, 0, 0)),
    ]
    out_shape = [jax.ShapeDtypeStruct((2, s, ds), F32), jax.ShapeDtypeStruct((2, nc, npair, STATE, LANES), F32)]
    if comm is not None:
        return _comm_call(body, name="ssd_fwd", grid=(2, nc), in_specs=in_specs, out_specs=out_specs, out_shape=out_shape,
                          scratch_shapes=[pltpu.VMEM((npair, STATE, LANES), F32)],
                          args=(xbc, xbc, xbc, dt_e, cum_e, cum_t), comm=comm)
    return pl.pallas_call(
        body, name="ssd_fwd", grid=(2, nc), in_specs=in_specs, out_specs=out_specs, out_shape=out_shape,
        scratch_shapes=[pltpu.VMEM((npair, STATE, LANES), F32)],
        compiler_params=_params(("arbitrary", "arbitrary")),
    )(xbc, xbc, xbc, dt_e, cum_e, cum_t), ()


def _ssd_bwd(xbc, dt_e, cum_e, cum_t, dt_t, a_col, dy, sp, s, h, g):
    q, nc, ds, nb = _ssd_specs(s, h, g)
    npair = h // 2

    def cidx(d, i):
        return jnp.where(d == 0, nc - 1 - i, i)

    def body(x_ref, b_ref, c_ref, dt_ref, cum_ref, cumt_ref, dtt_ref, a_ref, dy_ref, sp_ref,
             dx_ref, db_ref, dc_ref, ddt_ref, da_ref, dst, rowp):
        d = pl.program_id(0)
        i = pl.program_id(1)

        @pl.when(i == 0)
        def _():
            dst[...] = jnp.zeros_like(dst)
            da_ref[...] = jnp.zeros_like(da_ref)

        rev = d == 1
        ii = lax.broadcasted_iota(jnp.int32, (q, q), 0)
        jj = lax.broadcasted_iota(jnp.int32, (q, q), 1)
        sgn = jnp.where(rev, -1, 1)
        mask = (jj - ii) * sgn <= 0
        lane = lax.broadcasted_iota(jnp.int32, (q, LANES), 1)
        left = lane < HEAD_DIM
        rowp[...] = jnp.zeros_like(rowp)

        def group(gi, carry):
            acc_dcum, acc_tot, acc_dxx = carry
            goff = pl.multiple_of(gi * STATE, STATE)
            cg = c_ref[:, pl.ds(goff, STATE)].astype(BF16)
            bg = b_ref[:, pl.ds(goff, STATE)].astype(BF16)
            gm = _dot_nt(cg, bg)
            dgm = jnp.zeros((q, q), F32)
            dcg = jnp.zeros((q, STATE), F32)
            dbg = jnp.zeros((q, STATE), F32)
            for p in range(HEADS_PER_GROUP // 2):
                pr = gi * (HEADS_PER_GROUP // 2) + p
                off = pl.multiple_of(pr * LANES, LANES)
                xv = x_ref[:, pl.ds(off, LANES)]
                dte = dt_ref[:, pl.ds(off, LANES)]
                xd = xv * dte
                xdb = xd.astype(BF16)
                dyv = dy_ref[:, pl.ds(off, LANES)]
                sprev = sp_ref[pr]
                sprevb = sprev.astype(BF16)
                dsn = dst[pr]
                dsnb = dsn.astype(BF16)
                cols = [cum_ref[:, pl.ds(pl.multiple_of((2 * pr + hl) * LANES, LANES), LANES)] for hl in range(2)]
                ce = jnp.where(left, cols[0], cols[1])
                tot = jnp.where(rev, ce[0:1, :], ce[q - 1:q, :])
                et = jnp.exp(tot)
                r = jnp.exp(tot - ce)
                e = jnp.exp(ce)
                yoff = e * _dot(cg, sprevb)
                dz = (e * dyv).astype(BF16)
                dcg = dcg + _dot_nt(dz, sprevb)
                dsprev = _dot_tn(cg, dz) + et * dsn
                f1 = dyv * yoff
                v = (xd * r).astype(BF16)
                dbg = dbg + _dot_nt(v, dsnb)
                dv = _dot(bg, dsnb)
                dxd = dv * r
                tt = dv * xd * r
                wt = dsn * sprev * et
                for hl in range(2):
                    hh = 2 * pr + hl
                    hm = left if hl == 0 else jnp.logical_not(left)
                    row = cumt_ref[pl.ds(hh, 1), :]
                    lm = jnp.where(mask, jnp.exp(jnp.minimum(cols[hl] - row, 0.0)), 0.0)
                    mf = gm * lm
                    dym = jnp.where(hm, dyv, 0.0).astype(BF16)
                    dm = _dot_nt(dym, xdb)
                    dxd = dxd + _dot_tn(mf.astype(BF16), dym)
                    dgm = dgm + dm * lm
                    em = dm * mf
                    rowp[pl.ds(hh, 1), :] = rowp[pl.ds(hh, 1), :] - jnp.sum(em, axis=0, keepdims=True)
                    colq = (jnp.sum(em, axis=1, keepdims=True)
                            + jnp.sum(jnp.where(hm, f1 - tt, 0.0), axis=1, keepdims=True))
                    acc_dcum = jnp.where(lane == hh, colq, acc_dcum)
                    totq = jnp.sum(jnp.sum(jnp.where(hm, tt + wt, 0.0), axis=1, keepdims=True), axis=0, keepdims=True)
                    acc_tot = jnp.where(lane == hh, totq, acc_tot)
                dxx = dxd * xv
                for hl in range(2):
                    hh = 2 * pr + hl
                    hm = left if hl == 0 else jnp.logical_not(left)
                    acc_dxx = jnp.where(lane == hh, jnp.sum(jnp.where(hm, dxx, 0.0), axis=1, keepdims=True), acc_dxx)
                dx_ref[:, pl.ds(off, LANES)] = dxd * dte
                dst[pr] = dsprev
            dgb = dgm.astype(BF16)
            dc_ref[:, pl.ds(goff, STATE)] = dcg + _dot(dgb, bg)
            db_ref[:, pl.ds(goff, STATE)] = dbg + _dot_tn(dgb, cg)
            return acc_dcum, acc_tot, acc_dxx

        zero = jnp.zeros((q, LANES), F32)
        acc_dcum, acc_tot, acc_dxx = lax.fori_loop(0, g, group, (zero, zero, zero))
        dcum_t = rowp[...] + jnp.transpose(acc_dcum)[:h]
        rmat = ((ii - jj) * sgn >= 0).astype(F32).astype(BF16)
        da_t = _dot3_r(dcum_t, rmat) + jnp.transpose(acc_tot)[:h]
        ddt_ref[...] = da_t * a_ref[...] + jnp.transpose(acc_dxx)[:h]
        da_ref[...] += da_t * dtt_ref[...]

    in_specs = [
        pl.BlockSpec((q, ds), lambda d, i: (cidx(d, i), 0)),
        pl.BlockSpec((q, nb), lambda d, i: (cidx(d, i), ds // nb)),
        pl.BlockSpec((q, nb), lambda d, i: (cidx(d, i), ds // nb + 1)),
        pl.BlockSpec((None, q, ds), lambda d, i: (d, cidx(d, i), 0)),
        pl.BlockSpec((None, q, h * LANES), lambda d, i: (d, cidx(d, i), 0)),
        pl.BlockSpec((None, h, q), lambda d, i: (d, 0, cidx(d, i))),
        pl.BlockSpec((None, h, q), lambda d, i: (d, 0, cidx(d, i))),
        pl.BlockSpec((None, h, LANES), lambda d, i: (d, 0, 0)),
        pl.BlockSpec((q, ds), lambda d, i: (cidx(d, i), 0)),
        pl.BlockSpec((None, None, npair, STATE, LANES), lambda d, i: (d, cidx(d, i), 0, 0, 0)),
    ]
    out_specs = [
        pl.BlockSpec((None, q, ds), lambda d, i: (d, cidx(d, i), 0)),
        pl.BlockSpec((None, q, nb), lambda d, i: (d, cidx(d, i), 0)),
        pl.BlockSpec((None, q, nb), lambda d, i: (d, cidx(d, i), 0)),
        pl.BlockSpec((None, h, q), lambda d, i: (d, 0, cidx(d, i))),
        pl.BlockSpec((None, h, LANES), lambda d, i: (d, 0, 0)),
    ]
    out_shape = [jax.ShapeDtypeStruct((2, s, ds), F32), jax.ShapeDtypeStruct((2, s, nb), F32),
                 jax.ShapeDtypeStruct((2, s, nb), F32), jax.ShapeDtypeStruct((2, h, s), F32),
                 jax.ShapeDtypeStruct((2, h, LANES), F32)]
    return pl.pallas_call(
        body, name="ssd_bwd", grid=(2, nc), in_specs=in_specs, out_specs=out_specs, out_shape=out_shape,
        scratch_shapes=[pltpu.VMEM((npair, STATE, LANES), F32), pltpu.VMEM((h, q), F32)],
        compiler_params=_params(("arbitrary", "arbitrary")),
    )(xbc, xbc, xbc, dt_e, cum_e, cum_t, dt_t, a_col, dy, sp)


def _dt_bwd(ddt, proj, bias_row, s, u, h):
    tr = _tile_rows(s, 4 * LANES)
    dtblk = (6 * u) // LANES

    def body(d_ref, raw_ref, b_ref, o_ref, db_ref):
        lane = lax.broadcasted_iota(jnp.int32, (tr, LANES), 1)
        v = jnp.where(lane < 2 * h, d_ref[...] * _sigmoid(raw_ref[...] + b_ref[...]), 0.0)
        o_ref[...] = v.astype(BF16)
        _acc_rows(db_ref, v)

    return _rowcall(body, "dt_bwd", s, tr, [_row(tr, LANES), _row(tr, LANES, dtblk), _full((1, LANES))],
                    [_row(tr, LANES), _full((1, LANES))],
                    [jax.ShapeDtypeStruct((s, LANES), BF16), jax.ShapeDtypeStruct((1, LANES), F32)],
                    (ddt, proj, bias_row))


def _group_rms(v, gw):
    outs, facs = [], []
    for k in range(v.shape[1] // gw):
        blk = v[:, k * gw:(k + 1) * gw]
        f = lax.rsqrt(jnp.mean(blk * blk, axis=-1, keepdims=True) + RMS_EPS)
        outs.append(blk * f)
        facs.append(jnp.broadcast_to(f, blk.shape))
    return jnp.concatenate(outs, axis=1), jnp.concatenate(facs, axis=1)


def _group_rms_bwd(dn, n, fac, gw):
    outs = []
    for k in range(n.shape[1] // gw):
        sl = slice(k * gw, (k + 1) * gw)
        outs.append(fac[:, sl] * (dn[:, sl] - n[:, sl] * jnp.mean(dn[:, sl] * n[:, sl], axis=-1, keepdims=True)))
    return jnp.concatenate(outs, axis=1)


def _gate_norm_fwd(y2, xbc, proj, d_e, norm_w, s, u, g):
    tr = _tile_rows(s, u)
    gw = u // g

    def body(y_ref, x_ref, z_ref, d_ref, w_ref, o_ref):
        ys = y_ref[0] + y_ref[1] + d_ref[...] * x_ref[...]
        n, _ = _group_rms(ys * _silu(z_ref[...]), gw)
        o_ref[...] = (n * w_ref[...]).astype(BF16)

    return _rowcall(body, "gate_norm_fwd", s, tr,
                    [pl.BlockSpec((2, tr, u), lambda i: (0, i, 0)), _row(tr, u), _row(tr, u), _full((1, u)), _full((1, u))],
                    _row(tr, u), jax.ShapeDtypeStruct((s, u), BF16), (y2, xbc, proj, d_e, norm_w))


def _gate_norm_bwd(dymix, y2, xbc, proj, d_e, norm_w, s, u, g):
    tr = _tile_rows(s, u)
    gw = u // g

    def body(dy_ref, y_ref, x_ref, z_ref, d_ref, w_ref, dys_ref, dz_ref, dxs_ref, dw_ref, dd_ref):
        xv = x_ref[...]
        zv = z_ref[...]
        ys = y_ref[0] + y_ref[1] + d_ref[...] * xv
        sz = _silu(zv)
        n, fac = _group_rms(ys * sz, gw)
        dout = dy_ref[...]
        _acc_rows(dw_ref, dout * n)
        dyg = _group_rms_bwd(dout * w_ref[...], n, fac, gw)
        dys = dyg * sz
        dys_ref[...] = dys
        dz_ref[...] = (dyg * ys * _dsilu(zv)).astype(BF16)
        dxs_ref[...] = dys * d_ref[...]
        _acc_rows(dd_ref, dys * xv)

    v = _full((1, u))
    return _rowcall(body, "gate_norm_bwd", s, tr,
                    [_row(tr, u), pl.BlockSpec((2, tr, u), lambda i: (0, i, 0)), _row(tr, u), _row(tr, u), v, v],
                    [_row(tr, u), _row(tr, u), _row(tr, u), v, v],
                    [jax.ShapeDtypeStruct((s, u), F32), jax.ShapeDtypeStruct((s, u), BF16),
                     jax.ShapeDtypeStruct((s, u), F32), jax.ShapeDtypeStruct((1, u), F32), jax.ShapeDtypeStruct((1, u), F32)],
                    (dymix, y2, xbc, proj, d_e, norm_w))


def _shortconv_fwd(proj, conv_w, norm_w, s, u):
    tr = _tile_rows(s, u)
    half = SC_CONV // 2

    def body(hp, hc, hn, b_ref, cp, cc, cn, cw_ref, w_ref, o_ref):
        t = _ext(hp, hc, hn, s, tr) * _ext(cp, cc, cn, s, tr)
        wv = cw_ref[...]
        acc = jnp.zeros_like(t)
        for k in range(SC_CONV):
            acc = acc + _shift(t, k - half) * wv[k:k + 1, :]
        n, _ = _group_rms(b_ref[...] * acc[HALO:HALO + tr], SC_GROUP_WIDTH)
        o_ref[...] = (n * w_ref[...]).astype(BF16)

    in_specs = _halo_specs(s, tr, u, 3) + [_row(tr, u, 4)] + _halo_specs(s, tr, u, 5) + [_full((SC_CONV, u)), _full((1, u))]
    return _rowcall(body, "shortconv_fwd", s, tr, in_specs, _row(tr, u), jax.ShapeDtypeStruct((s, u), BF16),
                    (proj,) * 7 + (conv_w, norm_w))


def _shortconv_bwd(dymix, proj, conv_w, norm_w, s, u):
    tr = _tile_rows(s, u)
    half = SC_CONV // 2

    def body(dp, dc_, dn, hp, hc, hn, bp, bc, bn, cp, cc, cn, cw_ref, w_ref, dh_ref, db_ref, dcc_ref, dcw_ref, dw_ref):
        dout = _ext(dp, dc_, dn, s, tr)
        hv = _ext(hp, hc, hn, s, tr)
        bv = _ext(bp, bc, bn, s, tr)
        cv = _ext(cp, cc, cn, s, tr)
        t = hv * cv
        wv = cw_ref[...]
        acc = jnp.zeros_like(t)
        for k in range(SC_CONV):
            acc = acc + _shift(t, k - half) * wv[k:k + 1, :]
        n, fac = _group_rms(bv * acc, SC_GROUP_WIDTH)
        cur = slice(HALO, HALO + tr)
        _acc_rows(dw_ref, (dout * n)[cur])
        dyv = _group_rms_bwd(dout * w_ref[...], n, fac, SC_GROUP_WIDTH)
        db_ref[...] = (dyv * acc)[cur].astype(BF16)
        dv = dyv * bv
        dt = jnp.zeros_like(t)
        rows = []
        for k in range(SC_CONV):
            dt = dt + _shift(dv, half - k) * wv[k:k + 1, :]
            rows.append(jnp.sum((_shift(t, k - half) * dv)[cur], axis=0, keepdims=True))
        dh_ref[...] = (dt * cv)[cur].astype(BF16)
        dcc_ref[...] = (dt * hv)[cur].astype(BF16)
        dwv = jnp.concatenate(rows + [jnp.zeros((SUBLANES - SC_CONV, u), F32)], axis=0)
        first = pl.program_id(0) == 0

        @pl.when(first)
        def _():
            dcw_ref[...] = dwv

        @pl.when(jnp.logical_not(first))
        def _():
            dcw_ref[...] += dwv

    in_specs = (_halo_specs(s, tr, u, 1) + _halo_specs(s, tr, u, 3) + _halo_specs(s, tr, u, 4) + _halo_specs(s, tr, u, 5)
                + [_full((SC_CONV, u)), _full((1, u))])
    return _rowcall(body, "shortconv_bwd", s, tr, in_specs,
                    [_row(tr, u)] * 3 + [_full((SUBLANES, u)), _full((1, u))],
                    [jax.ShapeDtypeStruct((s, u), BF16)] * 3
                    + [jax.ShapeDtypeStruct((SUBLANES, u), F32), jax.ShapeDtypeStruct((1, u), F32)],
                    (dymix,) * 3 + (proj,) * 9 + (conv_w, norm_w))


def _adam_math(w, g, m, v):
    m2 = ADAM_B1 * m + (1.0 - ADAM_B1) * g
    v2 = ADAM_B2 * v + (1.0 - ADAM_B2) * (g * g)
    m_hat = m2 / (1.0 - ADAM_B1 ** ADAM_STEP)
    v_hat = v2 / (1.0 - ADAM_B2 ** ADAM_STEP)
    delta = -ADAM_LR * (m_hat / (jnp.sqrt(v_hat) + ADAM_EPS) + ADAM_WD * w)
    return delta, m2, v2


def _adam_rows(r, c):
    return _pick(r, max(SUBLANES, (1 << 20) // (4 * c)), SUBLANES)


def _adam(w, g, m, v, *, name):
    r, c = w.shape
    tr = _adam_rows(r, c)

    def body(w_ref, g_ref, m_ref, v_ref, d_ref, m2_ref, v2_ref):
        d_ref[...], m2_ref[...], v2_ref[...] = _adam_math(w_ref[...], g_ref[...], m_ref[...], v_ref[...])

    return _rowcall(body, name, r, tr, [_row(tr, c)] * 4, [_row(tr, c)] * 3,
                    [jax.ShapeDtypeStruct((r, c), F32)] * 3, (w, g, m, v))


def _adam_outer(w, a_t, bmat, m, v, *, name):
    r, c = w.shape
    tr = _adam_rows(r, c)
    kk = a_t.shape[1]

    def body(w_ref, a_ref, b_ref, m_ref, v_ref, g_ref, d_ref, m2_ref, v2_ref):
        g = _dot(a_ref[...].astype(BF16), b_ref[...].astype(BF16))
        g_ref[...] = g
        d_ref[...], m2_ref[...], v2_ref[...] = _adam_math(w_ref[...], g, m_ref[...], v_ref[...])

    return _rowcall(body, name, r, tr, [_row(tr, c), _row(tr, kk), _full((kk, c)), _row(tr, c), _row(tr, c)],
                    [_row(tr, c)] * 4, [jax.ShapeDtypeStruct((r, c), F32)] * 4, (w, a_t, bmat, m, v))


ANY = pl.BlockSpec(memory_space=pl.ANY)
VMEM_WHOLE = pl.BlockSpec(memory_space=pltpu.VMEM)


def _place():
    x, y, c = lax.axis_index("x"), lax.axis_index("y"), lax.axis_index("c")
    return x, y, c


DMA_CHUNKS = 8


def _n_chunks(rows):
    n = DMA_CHUNKS
    while n > 1 and rows % (16 * n):
        n //= 2
    return n


def _allgather_small(v, *, name):
    m_per, n = v.shape

    def body(x_ref, out_ref, send_sems, recv_sems, local_sem):
        x, y, c = _place()
        me, sibling = (x, y, c), (x, y, 1 - c)
        chips = [(1 - x, y), (x, 1 - y), (1 - x, 1 - y)]

        def rows(px, py, pc):
            return out_ref.at[pl.ds((4 * px + 2 * py + pc) * m_per, m_per), :]

        def copy(k, block, to, src=None):
            return pltpu.make_async_remote_copy(
                src_ref=rows(*block) if src is None else src, dst_ref=rows(*block),
                send_sem=send_sems.at[k], recv_sem=recv_sems.at[k], device_id=to, device_id_type=MESH)

        mine = pltpu.make_async_copy(x_ref, rows(*me), local_sem)
        mine.start()
        first = [copy(0, me, sibling, src=x_ref)]
        first += [copy(1 + j, me, (*chip, c), src=x_ref) for j, chip in enumerate(chips)]
        for cp in first:
            cp.start()
        passed = [copy(4 + j, (*chip, c), sibling) for j, chip in enumerate(chips)]
        for j, chip in enumerate(chips):
            copy(1 + j, (*chip, c), me).wait_recv()
            passed[j].start()
        copy(0, sibling, me).wait_recv()
        for j, chip in enumerate(chips):
            copy(4 + j, (*chip, 1 - c), me).wait_recv()
        for cp in first + passed:
            cp.wait_send()
        mine.wait()

    return pl.pallas_call(
        body, name=name, out_shape=jax.ShapeDtypeStruct((N_DEV * m_per, n), v.dtype),
        in_specs=[VMEM_WHOLE], out_specs=VMEM_WHOLE,
        scratch_shapes=[pltpu.SemaphoreType.DMA((7,)), pltpu.SemaphoreType.DMA((7,)), pltpu.SemaphoreType.DMA],
        compiler_params=pltpu.CompilerParams(vmem_limit_bytes=VMEM_LIMIT),
    )(v)


def _chip_id():
    return 2 * lax.axis_index("x") + lax.axis_index("y")


def _core_id():
    return lax.axis_index("c")


def _cast_into_block(wl, *, name):
    r, c_ = wl.shape
    tr = _tile_rows(r, c_, 16)

    def body(w_ref, o_ref):
        o_ref[...] = w_ref[...].astype(BF16)

    return pl.pallas_call(
        body, name=name, grid=(r // tr,), in_specs=[pl.BlockSpec((tr, c_), lambda i: (i, 0))],
        out_specs=pl.BlockSpec((None, tr, c_), lambda i: (_chip_id(), i, 0)),
        out_shape=jax.ShapeDtypeStruct((N_CHIPS, r, c_), BF16), compiler_params=_params(("arbitrary",)))(wl)


def _gather_weight(buf, *, name):
    _, r, c_ = buf.shape
    half = r // 2
    nch = _n_chunks(half)
    rows = half // nch

    def body(in_ref, out_ref, send_sems, recv_sems):
        del in_ref
        x, y, c = _place()
        me, sibling = (x, y, c), (x, y, 1 - c)
        chips = [(1 - x, y), (x, 1 - y), (1 - x, 1 - y)]

        def blk(px, py, pc, i):
            return out_ref.at[2 * px + py, pl.ds(pc * half + i * rows, rows), :]

        def copy(k, i, block, to):
            return pltpu.make_async_remote_copy(
                src_ref=blk(*block, i), dst_ref=blk(*block, i),
                send_sem=send_sems.at[k * nch + i], recv_sem=recv_sems.at[k * nch + i], device_id=to, device_id_type=MESH)

        first = [copy(j, i, me, (*chip, c)) for i in range(nch) for j, chip in enumerate(chips)]
        for cp in first:
            cp.start()
        passed = []
        for i in range(nch):
            for j, chip in enumerate(chips):
                copy(j, i, (*chip, c), me).wait_recv()
                passed.append(copy(3 + j, i, (*chip, c), sibling))
                passed[-1].start()
        for i in range(nch):
            for j, chip in enumerate(chips):
                copy(3 + j, i, (*chip, 1 - c), me).wait_recv()
        for cp in first + passed:
            cp.wait_send()

    return pl.pallas_call(
        body, name=name, out_shape=jax.ShapeDtypeStruct(buf.shape, buf.dtype),
        in_specs=[ANY], out_specs=ANY, input_output_aliases={0: 0},
        scratch_shapes=[pltpu.SemaphoreType.DMA((6 * nch,)), pltpu.SemaphoreType.DMA((6 * nch,))],
    )(buf)


def _gather_comm(buf):
    _, r, c_ = buf.shape
    half = r // 2
    nch = _n_chunks(half)
    rows = half // nch

    def plan(out_ref):
        x, y, c = _place()
        me, sibling = (x, y, c), (x, y, 1 - c)
        chips = [(1 - x, y), (x, 1 - y), (1 - x, 1 - y)]
        return me, sibling, chips, c

    def copy(out_ref, sems, k, i, block, to):
        part = out_ref.at[2 * block[0] + block[1], pl.ds(block[2] * half + i * rows, rows), :]
        return pltpu.make_async_remote_copy(src_ref=part, dst_ref=part, send_sem=sems[0].at[k * nch + i],
                                            recv_sem=sems[1].at[k * nch + i], device_id=to, device_id_type=MESH)

    def start(cins, couts, sems):
        (out_ref,) = couts
        me, sibling, chips, c = plan(out_ref)
        for i in range(nch):
            for j, chip in enumerate(chips):
                copy(out_ref, sems, j, i, me, (*chip, c)).start()

    def finish(cins, couts, sems):
        (out_ref,) = couts
        me, sibling, chips, c = plan(out_ref)
        passed = []
        for i in range(nch):
            for j, chip in enumerate(chips):
                copy(out_ref, sems, j, i, (*chip, c), me).wait_recv()
                passed.append(copy(out_ref, sems, 3 + j, i, (*chip, c), sibling))
                passed[-1].start()
        for i in range(nch):
            for j, chip in enumerate(chips):
                copy(out_ref, sems, 3 + j, i, (*chip, 1 - c), me).wait_recv()
        for i in range(nch):
            for j, chip in enumerate(chips):
                copy(out_ref, sems, j, i, me, (*chip, c)).wait_send()
        for cp in passed:
            cp.wait_send()

    return dict(inputs=[buf], out_shape=[jax.ShapeDtypeStruct(buf.shape, buf.dtype)], aliases={0: 0},
                scratch=[pltpu.SemaphoreType.DMA((6 * nch,)), pltpu.SemaphoreType.DMA((6 * nch,))],
                start=start, finish=finish)


def _pair_exchange(gfull, *, name):
    nblk, r, c_ = gfull.shape
    half = r // 2
    nch = _n_chunks(half)
    rows = half // nch

    def body(g_ref, peer_ref, send_sems, recv_sems):
        x, y, c = _place()

        def copy(k, i, pc):
            return pltpu.make_async_remote_copy(
                src_ref=g_ref.at[k, pl.ds(pc * half + i * rows, rows), :], dst_ref=peer_ref.at[k, pl.ds(i * rows, rows), :],
                send_sem=send_sems.at[k * nch + i], recv_sem=recv_sems.at[k * nch + i],
                device_id=(x, y, 1 - c), device_id_type=MESH)

        sends = [copy(k, i, 1 - c) for i in range(nch) for k in range(nblk)]
        for cp in sends:
            cp.start()
        for cp in sends:
            cp.wait_recv()
        for cp in sends:
            cp.wait_send()

    return pl.pallas_call(
        body, name=name, out_shape=jax.ShapeDtypeStruct((nblk, half, c_), gfull.dtype),
        in_specs=[ANY], out_specs=ANY,
        scratch_shapes=[pltpu.SemaphoreType.DMA((nblk * nch,)), pltpu.SemaphoreType.DMA((nblk * nch,))],
    )(gfull)


def _pair_add(gfull, peer, *, name):
    nblk, r, c_ = gfull.shape
    half = r // 2
    tr = _pick(half, max(16, (1 << 20) // (2 * c_)), 16)
    per = half // tr

    def body(g_ref, p_ref, o_ref):
        o_ref[...] = (g_ref[...].astype(F32) + p_ref[...].astype(F32)).astype(BF16)

    return pl.pallas_call(
        body, name=name, grid=(nblk, per),
        in_specs=[pl.BlockSpec((None, tr, c_), lambda k, i: (k, _core_id() * per + i, 0)),
                  pl.BlockSpec((None, tr, c_), lambda k, i: (k, i, 0))],
        out_specs=pl.BlockSpec((None, tr, c_), lambda k, i: (k, i, 0)),
        out_shape=jax.ShapeDtypeStruct((nblk, half, c_), BF16),
        compiler_params=_params(("arbitrary", "arbitrary")))(gfull, peer)


def _scatter_comm(pre):
    _, half, c_ = pre.shape
    nch = _n_chunks(half)
    rows = half // nch

    def copies(cins, couts, sems):
        (p_ref,), (r_ref,) = cins, couts
        x, y, c = _place()
        out = []
        for i in range(nch):
            for j, (tx, ty) in reversed(list(enumerate([(1 - x, y), (x, 1 - y), (1 - x, 1 - y)]))):
                out.append(pltpu.make_async_remote_copy(
                    src_ref=p_ref.at[2 * tx + ty, pl.ds(i * rows, rows), :], dst_ref=r_ref.at[j, pl.ds(i * rows, rows), :],
                    send_sem=sems[0].at[j * nch + i], recv_sem=sems[1].at[j * nch + i],
                    device_id=(tx, ty, c), device_id_type=MESH))
        return out

    def start(cins, couts, sems):
        for cp in copies(cins, couts, sems):
            cp.start()

    def finish(cins, couts, sems):
        cps = copies(cins, couts, sems)
        for cp in cps:
            cp.wait_recv()
        for cp in cps:
            cp.wait_send()

    return dict(inputs=[pre], out_shape=[jax.ShapeDtypeStruct((3, half, c_), pre.dtype)], aliases={},
                scratch=[pltpu.SemaphoreType.DMA((3 * nch,)), pltpu.SemaphoreType.DMA((3 * nch,))],
                start=start, finish=finish)


def _sum_into_half(pre, recv, *, name):
    _, half, c_ = pre.shape
    n = recv.shape[0]
    tr = _pick(half, max(16, (1 << 19) // (2 * c_)), 16)
    per = half // tr

    def body(g_ref, r_ref, o_ref):
        acc = g_ref[...].astype(F32)
        for k in range(n):
            acc = acc + r_ref[k].astype(F32)
        o_ref[...] = acc

    return pl.pallas_call(
        body, name=name, grid=(per,),
        in_specs=[pl.BlockSpec((None, tr, c_), lambda i: (_chip_id(), i, 0)),
                  pl.BlockSpec((n, tr, c_), lambda i: (0, i, 0))],
        out_specs=pl.BlockSpec((tr, c_), lambda i: (_core_id() * per + i, 0)),
        out_shape=jax.ShapeDtypeStruct((2 * half, c_), F32), compiler_params=_params(("arbitrary",)))(pre, recv)


def _sum_slots(recv, *, name):
    n, r, c_ = recv.shape
    tr = _pick(r, max(16, (1 << 19) // (2 * c_)), 16)

    def body(r_ref, o_ref):
        acc = r_ref[0].astype(F32)
        for k in range(1, n):
            acc = acc + r_ref[k].astype(F32)
        o_ref[...] = acc

    return _rowcall(body, name, r, tr, [pl.BlockSpec((n, tr, c_), lambda i: (0, i, 0))], _row(tr, c_),
                    jax.ShapeDtypeStruct((r, c_), F32), (recv,))


def _swap_halves(buf, *, name):
    r, c_ = buf.shape
    half = r // 2
    nch = _n_chunks(half)
    rows = half // nch

    def body(in_ref, out_ref, send_sems, recv_sems):
        del in_ref
        x, y, c = _place()

        def copy(i, pc):
            part = out_ref.at[pl.ds(pc * half + i * rows, rows), :]
            return pltpu.make_async_remote_copy(
                src_ref=part, dst_ref=part, send_sem=send_sems.at[i], recv_sem=recv_sems.at[i],
                device_id=(x, y, 1 - c), device_id_type=MESH)

        sends = [copy(i, c) for i in range(nch)]
        for cp in sends:
            cp.start()
        for i in range(nch):
            copy(i, 1 - c).wait_recv()
        for cp in sends:
            cp.wait_send()

    return pl.pallas_call(
        body, name=name, out_shape=jax.ShapeDtypeStruct(buf.shape, buf.dtype),
        in_specs=[ANY], out_specs=ANY, input_output_aliases={0: 0},
        scratch_shapes=[pltpu.SemaphoreType.DMA((nch,)), pltpu.SemaphoreType.DMA((nch,))],
    )(buf)


def _prereduce(gfull, *, name):
    return _pair_add(gfull, _pair_exchange(gfull, name=name + "_pair"), name=name + "_padd")


def _finish_reduce(pre, recv, *, name):
    return _swap_halves(_sum_into_half(pre, recv, name=name + "_sum"), name=name + "_swap")


PACK_ROWS = 16


def _pack(parts):
    flat = [p.reshape(-1).astype(F32) for p in parts]
    n = sum(f.shape[0] for f in flat)
    unit = PACK_ROWS * LANES
    total = -(-n // unit) * unit
    if total > n:
        flat.append(jnp.zeros((total - n,), F32))
    where, off = [], 0
    for p in parts:
        where.append((off, p.shape))
        off += p.size
    return jnp.concatenate(flat).reshape(total // LANES, LANES), where


def _unpack(flat, where):
    v = flat.reshape(-1)
    return [v[off:off + _size(shape)].reshape(shape) for off, shape in where]


def _size(shape):
    n = 1
    for d in shape:
        n *= d
    return n


def _sample_step(x, target, mods, w_in_p, bufs, sp):
    s, d = x.shape
    u = d // 2
    h = u // HEAD_DIM
    g = h // HEADS_PER_GROUP
    pw = w_in_p.shape[1]
    din = 6 * u + 2 * h
    dff_ = bufs["w_up"].shape[2] * N_CHIPS
    shift1, scale1, gate1, shift2, scale2, gate2 = mods

    a_f = -jnp.exp(sp["ssm_a_log_f"].reshape(-1))
    a_b = -jnp.exp(sp["ssm_a_log_b"].reshape(-1))
    pad_l = LANES - 2 * h
    a_row = jnp.pad(jnp.concatenate([a_f, a_b]), (0, pad_l)).reshape(1, LANES)
    bias_row = jnp.pad(jnp.concatenate([sp["ssm_dt_bias_f"].reshape(-1), sp["ssm_dt_bias_b"].reshape(-1)]),
                       (0, pad_l)).reshape(1, LANES)
    a_col = jnp.broadcast_to(jnp.stack([a_f, a_b])[:, :, None], (2, h, LANES))
    d_e = jnp.repeat(sp["ssm_d"].reshape(-1), HEAD_DIM).reshape(1, u)
    conv_w, conv_b = sp["ssm_conv_w"], sp["ssm_conv_b"].reshape(1, 2 * u)
    sc_conv_w = sp["sc_conv_w"]
    ssm_norm_w, sc_norm_w = sp["ssm_norm_w"].reshape(1, u), sp["sc_norm_w"].reshape(1, u)
    ln1_g, ln1_b = sp["ln1_g"].reshape(1, d), sp["ln1_b"].reshape(1, d)
    ln2_g, ln2_b = sp["ln2_g"].reshape(1, d), sp["ln2_b"].reshape(1, d)

    h1 = _modulate(x, scale1, shift1, name="modulate1")
    proj, (w_up_blk,) = _matmul(h1, w_in_p, name="mm_proj", tn=1280, comm=_gather_comm(bufs["w_up"]))
    xbc = _conv_silu_fwd(proj, conv_w, conv_b, s, u)
    dt, cum, dt_e, cum_e = _dt_prep(proj, bias_row, a_row, s, u, h)
    cum_t = jnp.stack([cum[:, :h].T, cum[:, h:2 * h].T])
    dt_t = jnp.stack([dt[:, :h].T, dt[:, h:2 * h].T])
    (y2, states), (w_out_blk,) = _ssd_fwd(xbc, dt_e, cum_e, cum_t, s, h, g, comm=_gather_comm(bufs["w_out"]))
    w_out = w_out_blk.reshape(d, d)
    y_ssm = _gate_norm_fwd(y2, xbc, proj, d_e, ssm_norm_w, s, u, g)
    y_sc = _shortconv_fwd(proj, sc_conv_w, sc_norm_w, s, u)
    ymix = jnp.concatenate([y_ssm, y_sc], axis=1)
    mix = _matmul(ymix, w_out, name="mm_mix")
    x1, h2 = _ln1_fwd(x, mix, gate1, ln1_g, ln1_b, scale2, shift2)
    (up, ff), (w_down_blk,) = _matmul(h2, w_up_blk, name="mm_up", b_blocks=N_CHIPS, epilogue="relu2",
                                      comm=_gather_comm(bufs["w_down"]))
    w_down = w_down_blk.reshape(dff_, d)
    f = _matmul(ff, w_down, name="mm_down")
    df, dr2, loss, dg2, db2, dgate2 = _ln2_loss_bwd(x1, f, target, gate2, ln2_g, ln2_b)

    gw_down = _matmul(ff, df, name="mm_gw_down", ta=True, out_dtype=BF16)
    pre_down = _prereduce(gw_down.reshape(N_CHIPS, dff_ // N_CHIPS, d), name="rs_w_down")
    du, (rv_down,) = _matmul(df, w_down, name="mm_dff", tb=True, out_dtype=BF16, epilogue="relu2_bwd", extra=up,
                             comm=_scatter_comm(pre_down))
    gw_up = _matmul(h2, du, name="mm_gw_up", ta=True, out_dtype=BF16, out_blocks=N_CHIPS)
    pre_up = _prereduce(gw_up, name="rs_w_up")
    dh2, (rv_up,) = _matmul(du, w_up_blk, name="mm_dh2", tb=True, b_blocks=N_CHIPS, comm=_scatter_comm(pre_up))
    dmix, dxa, dscale2, dshift2, dg1, db1, dgate1 = _ln1_bwd(dh2, dr2, x1, x, mix, scale2, gate1, ln1_g)
    gw_out = _matmul(ymix, dmix, name="mm_gw_out", ta=True, out_dtype=BF16)
    pre_out = _prereduce(gw_out.reshape(N_CHIPS, d // N_CHIPS, d), name="rs_w_out")
    dymix, (rv_out,) = _matmul(dmix, w_out, name="mm_dymix", tb=True, comm=_scatter_comm(pre_out))
    dys, dz, dxs, dnw, dd_e = _gate_norm_bwd(dymix, y2, xbc, proj, d_e, ssm_norm_w, s, u, g)
    dx2, dbb, dcc, ddt_t, da = _ssd_bwd(xbc, dt_e, cum_e, cum_t, dt_t, a_col, dys, states, s, h, g)
    dxbc = jnp.concatenate([dx2[0] + dx2[1] + dxs, dbb[0] + dbb[1], dcc[0] + dcc[1]], axis=1)
    du_xbc, dcw, dcb = _conv_silu_bwd(proj, dxbc, conv_w, conv_b, s, u)
    ddt = jnp.pad(jnp.concatenate([ddt_t[0].T, ddt_t[1].T], axis=1), ((0, 0), (0, pad_l)))
    ddt_raw, dbias = _dt_bwd(ddt, proj, bias_row, s, u, h)
    dh_sc, db_sc, dc_sc, dscw, dscnw = _shortconv_bwd(dymix, proj, sc_conv_w, sc_norm_w, s, u)
    dproj = jnp.concatenate([dz, du_xbc, dh_sc, db_sc, dc_sc, ddt_raw,
                             jnp.zeros((s, pw - 6 * u - LANES), BF16)], axis=1)
    gp = _matmul(h1, dproj, name="mm_gw_in", ta=True, out_dtype=BF16, tn=1280)
    pre_in = _prereduce(_from_p_layout(gp, u, h, N_CHIPS), name="rs_w_in")
    dh1, (rv_in,) = _matmul(dproj, w_in_p, name="mm_dh1", tb=True, tk=2560, comm=_scatter_comm(pre_in))
    grad_x, dscale1, dshift1 = _dx_final(dxa, dh1, x, scale1)
    big = {"w_down": _finish_reduce(pre_down, rv_down, name="rs_w_down"),
           "w_up": _finish_reduce(pre_up, rv_up, name="rs_w_up"),
           "w_out": _finish_reduce(pre_out, rv_out, name="rs_w_out"),
           "w_in": _finish_reduce(pre_in, rv_in, name="rs_w_in")}

    small = {
        "dmod": jnp.concatenate([dshift1, dscale1, dgate1, dshift2, dscale2, dgate2], axis=1),
        "ssm_conv_b": dcb,
        "ssm_dt_bias_f": dbias[0, :h], "ssm_dt_bias_b": dbias[0, h:2 * h],
        "ssm_a_log_f": jnp.sum(da[0], axis=1) * a_f, "ssm_a_log_b": jnp.sum(da[1], axis=1) * a_b,
        "ssm_d": jnp.sum(dd_e.reshape(h, HEAD_DIM), axis=1),
        "ssm_norm_w": dnw, "sc_norm_w": dscnw,
        "ln1_g": dg1, "ln1_b": db1, "ln2_g": dg2, "ln2_b": db2,
        "ssm_conv_w": dcw[:SSM_CONV], "sc_conv_w": dscw[:SC_CONV],
    }
    return loss, grad_x, big, small


WEIGHTS = ['w_ada', 'b_ada', 'w_in', 'ssm_conv_w', 'ssm_conv_b', 'ssm_dt_bias_f', 'ssm_dt_bias_b', 'ssm_a_log_f',
           'ssm_a_log_b', 'ssm_d', 'ssm_norm_w', 'sc_conv_w', 'sc_norm_w', 'w_out', 'ln1_g', 'ln1_b', 'w_up', 'w_down',
           'ln2_g', 'ln2_b']
BIG = ('w_ada', 'w_in', 'w_out', 'w_up', 'w_down')
SMALL = tuple(n for n in WEIGHTS if n not in BIG)
SMALL_SHARDED = ('ssm_conv_w', 'sc_conv_w')


def _p_layout_width(u):
    return -(-(6 * u + LANES) // 512) * 512


def _p_segments(u, h):
    return [((0, 3 * u), 0), ((3 * u, 3 * u + 2 * h), 6 * u), ((3 * u + 2 * h, 6 * u + 2 * h), 3 * u)]


def _to_p_layout(blocks, u, h, pw):
    nblk, d, w = blocks.shape
    parts = []
    for (lo, hi), _ in sorted(_p_segments(u, h), key=lambda t: t[1]):
        for k in range(nblk):
            a, b = max(lo, k * w), min(hi, (k + 1) * w)
            if a < b:
                parts.append(blocks[k][:, a - k * w:b - k * w])
    parts.append(jnp.zeros((d, pw - nblk * w), blocks.dtype))
    return jnp.concatenate(parts, axis=1)


def _from_p_layout(gp, u, h, nblk):
    w = (6 * u + 2 * h) // nblk
    blocks = []
    for k in range(nblk):
        parts = []
        for (lo, hi), poff in _p_segments(u, h):
            a, b = max(lo, k * w), min(hi, (k + 1) * w)
            if a < b:
                parts.append(gp[:, poff + a - lo:poff + b - lo])
        blocks.append(jnp.concatenate(parts, axis=1))
    return jnp.stack(blocks)


def kernel(x, c, w_ada, b_ada, w_in, ssm_conv_w, ssm_conv_b, ssm_dt_bias_f, ssm_dt_bias_b, ssm_a_log_f, ssm_a_log_b, ssm_d, ssm_norm_w, sc_conv_w, sc_norm_w, w_out, ln1_g, ln1_b, w_up, w_down, ln2_g, ln2_b, loss_target, m_w_ada, m_b_ada, m_w_in, m_ssm_conv_w, m_ssm_conv_b, m_ssm_dt_bias_f, m_ssm_dt_bias_b, m_ssm_a_log_f, m_ssm_a_log_b, m_ssm_d, m_ssm_norm_w, m_sc_conv_w, m_sc_norm_w, m_w_out, m_ln1_g, m_ln1_b, m_w_up, m_w_down, m_ln2_g, m_ln2_b, v_w_ada, v_b_ada, v_w_in, v_ssm_conv_w, v_ssm_conv_b, v_ssm_dt_bias_f, v_ssm_dt_bias_b, v_ssm_a_log_f, v_ssm_a_log_b, v_ssm_d, v_ssm_norm_w, v_sc_conv_w, v_sc_norm_w, v_w_out, v_ln1_g, v_ln1_b, v_w_up, v_w_down, v_ln2_g, v_ln2_b):
    given = dict(locals())
    w = {n: given[n][0] for n in WEIGHTS}
    m = {n: given["m_" + n][0] for n in WEIGHTS}
    v = {n: given["v_" + n][0] for n in WEIGHTS}
    xs, tgt = x[0], loss_target[0]
    s, d = xs.shape
    u = d // 2
    h = u // HEAD_DIM
    nmod = N_MOD * d
    nmod_loc = nmod // N_CHIPS
    ax, ay, ac = lax.axis_index("x"), lax.axis_index("y"), lax.axis_index("c")
    chip = 2 * ax + ay
    me = 2 * chip + ac

    pay1, where1 = _pack([c[0], w["ssm_conv_w"], w["sc_conv_w"]])
    g1 = _allgather_small(pay1, name="ag_inputs").reshape(N_DEV, -1)
    per_dev = [_unpack(g1[k], where1) for k in range(N_DEV)]
    c_all = jnp.stack([p[0] for p in per_dev])
    ssm_conv_w_full = jnp.concatenate([per_dev[2 * k][1] for k in range(N_CHIPS)], axis=1)
    sc_conv_w_full = jnp.concatenate([per_dev[2 * k][2] for k in range(N_CHIPS)], axis=1)

    sc_all = _silu(c_all)
    sc16 = jnp.pad(sc_all, ((0, 16 - N_DEV), (0, 0)))
    b_loc = lax.dynamic_slice(w["b_ada"], (chip * nmod_loc,), (nmod_loc,))
    mod_loc = _matmul(sc16, w["w_ada"], name="mm_mod")[:N_DEV] + b_loc[None, :]
    pay2, where2 = _pack([mod_loc])
    g2 = _allgather_small(pay2, name="ag_mod").reshape(N_DEV, -1)
    mod_blocks = jnp.stack([_unpack(g2[2 * k], where2)[0] for k in range(N_CHIPS)])
    mod_mine = lax.dynamic_index_in_dim(mod_blocks, me, axis=1, keepdims=False).reshape(N_MOD, 1, d)
    mods = [mod_mine[k] for k in range(N_MOD)]

    din = w["w_in"].shape[1] * N_CHIPS

    g_in = _gather_weight(_cast_into_block(w["w_in"], name="cast_w_in"), name="gather_w_in")
    w_in_p = _to_p_layout(g_in, u, h, _p_layout_width(u))
    bufs = {n: _cast_into_block(w[n], name="cast_" + n) for n in ("w_out", "w_up", "w_down")}

    sp = {n: w[n] for n in SMALL}
    sp["ssm_conv_w"], sp["sc_conv_w"] = ssm_conv_w_full, sc_conv_w_full
    loss_loc, grad_x, big, small = _sample_step(xs, tgt, mods, w_in_p, bufs, sp)

    small_names = ["dmod"] + [n for n in SMALL if n != "b_ada"]
    pay3, where3 = _pack([loss_loc] + [small[n] for n in small_names])
    g3 = _allgather_small(pay3, name="ag_small_grads")
    tot = _unpack(_sum_slots(g3.reshape(N_DEV, -1, LANES), name="sum_small_grads"), where3)
    loss = tot[0].reshape(())
    gsum = dict(zip(small_names, tot[1:]))
    dmod_all = jnp.stack([_unpack(g3.reshape(N_DEV, -1)[k], where3)[1].reshape(-1) for k in range(N_DEV)])

    grads = {}
    grads["b_ada"] = gsum["dmod"].reshape(-1)
    for n in SMALL:
        if n in SMALL_SHARDED:
            loc = w[n].shape[1]
            grads[n] = lax.dynamic_slice_in_dim(gsum[n], chip * loc, loc, axis=1)
        elif n != "b_ada":
            grads[n] = gsum[n].reshape(w[n].shape)

    grads.update(big)

    delta, new_m, new_v = {}, {}, {}
    dm_loc = lax.dynamic_slice_in_dim(dmod_all, chip * nmod_loc, nmod_loc, axis=1)
    grads["w_ada"], delta["w_ada"], new_m["w_ada"], new_v["w_ada"] = _adam_outer(
        w["w_ada"], sc16.T, jnp.pad(dm_loc, ((0, 16 - N_DEV), (0, 0))), m["w_ada"], v["w_ada"], name="adam_w_ada")
    for n in ("w_in", "w_out", "w_up", "w_down"):
        delta[n], new_m[n], new_v[n] = _adam(w[n], grads[n], m[n], v[n], name="adam_" + n)
    pw_, where_s = _pack([w[n] for n in SMALL])
    pg_, _ = _pack([grads[n] for n in SMALL])
    pm_, _ = _pack([m[n] for n in SMALL])
    pv_, _ = _pack([v[n] for n in SMALL])
    sd, sm, sv = _adam(pw_, pg_, pm_, pv_, name="adam_small")
    for n, a, b_, c_ in zip(SMALL, _unpack(sd, where_s), _unpack(sm, where_s), _unpack(sv, where_s)):
        delta[n], new_m[n], new_v[n] = a, b_, c_

    def lead(t):
        return t[None]

    return (loss, grad_x[None], *[lead(grads[n].reshape(w[n].shape)) for n in WEIGHTS],
            *[lead(delta[n]) for n in WEIGHTS], *[lead(new_m[n]) for n in WEIGHTS], *[lead(new_v[n]) for n in WEIGHTS])
```

```python
import functools

import jax
import jax.numpy as jnp
from jax import lax
from jax.experimental import pallas as pl
from jax.experimental.pallas import tpu as pltpu

F32 = jnp.float32
BF16 = jnp.bfloat16

CHUNK = 128
HEAD_DIM = 64
STATE = 128
HEADS_PER_GROUP = 4
SC_GROUP_WIDTH = 128
SSM_CONV = 5
SC_CONV = 3
N_MOD = 6
DEEPNORM_ALPHA = 2.0 ** 0.25
LN_EPS = 1e-5
RMS_EPS = 1e-5
ADAM_LR = 0.001
ADAM_B1 = 0.9
ADAM_B2 = 0.999
ADAM_EPS = 1e-08
ADAM_WD = 0.01
ADAM_STEP = 10
N_CHIPS = 4
N_DEV = 8
LANES = 128
SUBLANES = 8
HALO = 8
VMEM_LIMIT = 56 * 1024 * 1024
MESH = pl.DeviceIdType.MESH


def _params(sem=None):
    return pltpu.CompilerParams(dimension_semantics=sem, vmem_limit_bytes=VMEM_LIMIT)


def _pick(n, target, mult=LANES):
    best = None
    t = mult
    while t <= min(n, target):
        if n % t == 0:
            best = t
        t += mult
    return best if best is not None else n


ROW_TILE_BYTES = 2 << 20


def _tile_rows(s, width, mult=SUBLANES):
    return _pick(s, max(mult, ROW_TILE_BYTES // (4 * width)), mult)


def _sigmoid(v):
    return 1.0 / (1.0 + jnp.exp(-v))


def _silu(v):
    return v * _sigmoid(v)


def _dsilu(v):
    s = _sigmoid(v)
    return s * (1.0 + v * (1.0 - s))


def _softplus(v):
    e = jnp.exp(-jnp.abs(v))
    return jnp.maximum(v, 0.0) + jnp.where(e < 1e-4, e - 0.5 * e * e, jnp.log(1.0 + e))


def _dot(a, b):
    return jnp.dot(a, b, preferred_element_type=F32)


def _dot_nt(a, b):
    return lax.dot_general(a, b, (((1,), (1,)), ((), ())), preferred_element_type=F32)


def _dot_tn(a, b):
    return lax.dot_general(a, b, (((0,), (0,)), ((), ())), preferred_element_type=F32)


def _split3(v):
    hi = v.astype(BF16)
    r1 = v - hi.astype(F32)
    mid = r1.astype(BF16)
    lo = (r1 - mid.astype(F32)).astype(BF16)
    return hi, mid, lo


def _dot3_r(v, onehot):
    hi, mid, lo = _split3(v)
    return _dot(hi, onehot) + _dot(mid, onehot) + _dot(lo, onehot)


def _dot3_l(onehot, v):
    hi, mid, lo = _split3(v)
    return _dot(onehot, hi) + _dot(onehot, mid) + _dot(onehot, lo)


MATMUL_VMEM_BUDGET = 44 * 1024 * 1024


def _matmul(a, b, *, name, ta=False, tb=False, out_dtype=F32, b_blocks=1, out_blocks=1,
            tm=1024, tn=1024, tk=4096, comm=None, epilogue=None, extra=None):
    if ta:
        K, M = a.shape
    else:
        M, K = a.shape
    if b_blocks > 1:
        nb, r_, c_ = b.shape
        if tb:
            N, K2 = r_, c_ * nb
        else:
            K2, N = r_, c_ * nb
    else:
        if tb:
            N, K2 = b.shape
        else:
            K2, N = b.shape
    assert K == K2, (a.shape, b.shape, ta, tb)
    assert not (ta and tb)
    n_unit = N // b_blocks if (b_blocks > 1 and not tb) else N
    n_unit = min(n_unit, N // out_blocks)
    tn = _pick(n_unit, tn)
    k_unit = K // b_blocks if (b_blocks > 1 and tb) else K
    tk = _pick(k_unit, tk)
    tile_bytes = {None: jnp.dtype(out_dtype).itemsize, "relu2": 6, "relu2_bwd": 6}[epilogue]

    def vmem_need(tm_):
        need = 2 * (tm_ * tk * a.dtype.itemsize + tk * tn * b.dtype.itemsize) + 2 * tm_ * tn * tile_bytes
        return need + (tm_ * tn * 4 if K > tk else 0)

    tm = _pick(M, tm)
    while vmem_need(tm) > MATMUL_VMEM_BUDGET and tm % 2 == 0 and tm // 2 >= LANES:
        tm //= 2
    gm, gn, gk = M // tm, N // tn, K // tk

    if ta:
        a_spec = pl.BlockSpec((tk, tm), lambda i, j, k: (k, i))
    else:
        a_spec = pl.BlockSpec((tm, tk), lambda i, j, k: (i, k))
    if b_blocks > 1 and not tb:
        per = (N // b_blocks) // tn
        b_spec = pl.BlockSpec((None, tk, tn), lambda i, j, k: (j // per, k, j % per))
    elif b_blocks > 1 and tb:
        per = (K // b_blocks) // tk
        b_spec = pl.BlockSpec((None, tn, tk), lambda i, j, k: (k // per, j, k % per))
    elif tb:
        b_spec = pl.BlockSpec((tn, tk), lambda i, j, k: (j, k))
    else:
        b_spec = pl.BlockSpec((tk, tn), lambda i, j, k: (k, j))
    if out_blocks > 1:
        per_o = (N // out_blocks) // tn
        o_spec = pl.BlockSpec((None, tm, tn), lambda i, j, k: (j // per_o, i, j % per_o))
        o_shape = jax.ShapeDtypeStruct((out_blocks, M, N // out_blocks), out_dtype)
    else:
        o_spec = pl.BlockSpec((tm, tn), lambda i, j, k: (i, j))
        o_shape = jax.ShapeDtypeStruct((M, N), out_dtype)

    in_specs, args = [a_spec, b_spec], [a, b]
    out_specs, out_shape = [o_spec], [o_shape]
    if epilogue == "relu2":
        assert out_blocks == 1 and out_dtype == F32
        out_specs.append(o_spec)
        out_shape.append(jax.ShapeDtypeStruct((M, N), BF16))
    elif epilogue == "relu2_bwd":
        assert out_blocks == 1 and out_dtype == BF16
        in_specs.append(o_spec)
        args.append(extra)
    n_in = len(in_specs)

    def write(refs, p):
        o_ref = refs[n_in]
        if epilogue == "relu2":
            o_ref[...] = p
            r = jnp.maximum(p, 0.0)
            refs[n_in + 1][...] = (r * r).astype(BF16)
        elif epilogue == "relu2_bwd":
            o_ref[...] = (p * 2.0 * jnp.maximum(refs[2][...], 0.0)).astype(BF16)
        else:
            o_ref[...] = p.astype(out_dtype)

    def body(*refs):
        av = refs[0][...].astype(BF16)
        bv = refs[1][...].astype(BF16)
        p = _dot_tn(av, bv) if ta else (_dot_nt(av, bv) if tb else _dot(av, bv))
        if gk == 1:
            write(refs, p)
            return
        acc = refs[-1]
        k = pl.program_id(2)

        @pl.when(k == 0)
        def _():
            acc[...] = p

        @pl.when(jnp.logical_and(k > 0, k < gk - 1))
        def _():
            acc[...] += p

        @pl.when(k == gk - 1)
        def _():
            write(refs, acc[...] + p)

    scratch = [pltpu.VMEM((tm, tn), F32)] if gk > 1 else []
    if comm is not None:
        outs, landed = _comm_call(body, name=name, grid=(gm, gn, gk), in_specs=in_specs, out_specs=out_specs,
                                  out_shape=out_shape, scratch_shapes=scratch, args=args, comm=comm)
        return (outs[0] if len(outs) == 1 else tuple(outs)), landed
    outs = pl.pallas_call(
        body, name=name, grid=(gm, gn, gk), in_specs=in_specs, out_specs=out_specs,
        out_shape=out_shape, scratch_shapes=scratch,
        compiler_params=_params(("parallel", "parallel", "arbitrary")),
    )(*args)
    return outs[0] if len(outs) == 1 else tuple(outs)


def _comm_call(body, *, name, grid, in_specs, out_specs, out_shape, scratch_shapes, args, comm):
    n_in, n_out, n_scr = len(in_specs), len(out_shape), len(scratch_shapes)
    c_in, c_out = list(comm["inputs"]), list(comm["out_shape"])
    nci, nco = len(c_in), len(c_out)
    hbm = pl.BlockSpec(memory_space=pl.ANY)

    def body2(*refs):
        ins, cins = refs[:n_in], refs[n_in:n_in + nci]
        o0 = n_in + nci
        outs, couts = refs[o0:o0 + n_out], refs[o0 + n_out:o0 + n_out + nco]
        s0 = o0 + n_out + nco
        scr, cscr = refs[s0:s0 + n_scr], refs[s0 + n_scr:]
        first = functools.reduce(jnp.logical_and, [pl.program_id(a) == 0 for a in range(len(grid))])
        last = functools.reduce(jnp.logical_and, [pl.program_id(a) == grid[a] - 1 for a in range(len(grid))])

        @pl.when(first)
        def _():
            comm["start"](cins, couts, cscr)

        body(*ins, *outs, *scr)

        @pl.when(last)
        def _():
            comm["finish"](cins, couts, cscr)

    res = pl.pallas_call(
        body2, name=name, grid=grid, in_specs=list(in_specs) + [hbm] * nci, out_specs=list(out_specs) + [hbm] * nco,
        out_shape=list(out_shape) + c_out, scratch_shapes=list(scratch_shapes) + list(comm["scratch"]),
        input_output_aliases={n_in + k: n_out + v for k, v in comm.get("aliases", {}).items()},
        compiler_params=_params(("arbitrary",) * len(grid)),
    )(*args, *c_in)
    return res[:n_out], res[n_out:]


def _row(tr, w, blk=0):
    return pl.BlockSpec((tr, w), lambda i: (i, blk))


def _full(shape):
    nd = len(shape)
    return pl.BlockSpec(shape, lambda i: (0,) * nd)


def _halo_specs(s, tr, w, blk=0):
    per = tr // HALO
    last = s // HALO - 1
    return [
        pl.BlockSpec((HALO, w), lambda i: (jnp.maximum(i * per - 1, 0), blk)),
        pl.BlockSpec((tr, w), lambda i: (i, blk)),
        pl.BlockSpec((HALO, w), lambda i: (jnp.minimum((i + 1) * per, last), blk)),
    ]


def _ext(prev_ref, cur_ref, next_ref, s, tr):
    i = pl.program_id(0)
    e = jnp.concatenate([prev_ref[...].astype(F32), cur_ref[...].astype(F32), next_ref[...].astype(F32)], axis=0)
    rid = i * tr - HALO + lax.broadcasted_iota(jnp.int32, e.shape, 0)
    return jnp.where((rid >= 0) & (rid < s), e, 0.0)


def _valid_rows(shape, s, tr):
    i = pl.program_id(0)
    rid = i * tr - HALO + lax.broadcasted_iota(jnp.int32, shape, 0)
    return (rid >= 0) & (rid < s)


def _shift(e, k):
    if k == 0:
        return e
    n = e.shape[0]
    return pltpu.roll(e, (n - k) % n, 0)


def _acc_rows(ref, v):
    s = jnp.sum(v, axis=0, keepdims=True)

    @pl.when(pl.program_id(0) == 0)
    def _():
        ref[...] = s

    @pl.when(pl.program_id(0) > 0)
    def _():
        ref[...] += s


def _rowcall(body, name, s, tr, in_specs, out_specs, out_shape, args, comm=None):
    if comm is not None:
        single = not isinstance(out_shape, (list, tuple))
        outs, landed = _comm_call(body, name=name, grid=(s // tr,), in_specs=in_specs,
                                  out_specs=[out_specs] if single else out_specs,
                                  out_shape=[out_shape] if single else out_shape, scratch_shapes=[], args=args, comm=comm)
        return (outs[0] if single else outs), landed
    return pl.pallas_call(
        body, name=name, grid=(s // tr,), in_specs=in_specs, out_specs=out_specs, out_shape=out_shape,
        compiler_params=_params(("arbitrary",)),
    )(*args)


def _modulate(x, scale, shift, *, name, comm=None):
    s, d = x.shape
    tr = _tile_rows(s, d)

    def body(x_ref, sc_ref, sh_ref, o_ref):
        o_ref[...] = (x_ref[...] * (1.0 + sc_ref[...]) + sh_ref[...]).astype(BF16)

    return _rowcall(body, name, s, tr, [_row(tr, d), _full((1, d)), _full((1, d))], _row(tr, d),
                    jax.ShapeDtypeStruct((s, d), BF16), (x, scale, shift), comm=comm)


def _ln_stats(r):
    mu = jnp.mean(r, axis=-1, keepdims=True)
    xc = r - mu
    var = jnp.mean(xc * xc, axis=-1, keepdims=True)
    rstd = lax.rsqrt(var + LN_EPS)
    return xc * rstd, rstd


def _ln1_fwd(x, mix, gate, g, b, scale2, shift2, comm=None):
    s, d = x.shape
    tr = _tile_rows(s, d)

    def body(x_ref, m_ref, gt_ref, g_ref, b_ref, sc_ref, sh_ref, x1_ref, h2_ref):
        r = DEEPNORM_ALPHA * x_ref[...] + (1.0 + gt_ref[...]) * m_ref[...]
        xh, _ = _ln_stats(r)
        x1 = xh * g_ref[...] + b_ref[...]
        x1_ref[...] = x1
        h2_ref[...] = (x1 * (1.0 + sc_ref[...]) + sh_ref[...]).astype(BF16)

    v = _full((1, d))
    return _rowcall(body, "ln1_fwd", s, tr, [_row(tr, d), _row(tr, d), v, v, v, v, v],
                    [_row(tr, d), _row(tr, d)],
                    [jax.ShapeDtypeStruct((s, d), F32), jax.ShapeDtypeStruct((s, d), BF16)],
                    (x, mix, gate, g, b, scale2, shift2), comm=comm)


def _ln2_loss_bwd(x1, f, target, gate, g, b):
    s, d = x1.shape
    tr = _tile_rows(s, d)

    def body(x1_ref, f_ref, t_ref, gt_ref, g_ref, b_ref, df_ref, dr_ref, loss_ref, dg_ref, db_ref, dgt_ref):
        fv = f_ref[...]
        r = DEEPNORM_ALPHA * x1_ref[...] + (1.0 + gt_ref[...]) * fv
        xh, rstd = _ln_stats(r)
        y = xh * g_ref[...] + b_ref[...]
        err = y - t_ref[...]
        _acc_rows(loss_ref, 0.5 * jnp.mean(err * err, axis=-1, keepdims=True))
        dy = err * (1.0 / d)
        _acc_rows(dg_ref, dy * xh)
        _acc_rows(db_ref, dy)
        dxh = dy * g_ref[...]
        dr = rstd * (dxh - jnp.mean(dxh, axis=-1, keepdims=True) - xh * jnp.mean(dxh * xh, axis=-1, keepdims=True))
        dr_ref[...] = dr
        df_ref[...] = ((1.0 + gt_ref[...]) * dr).astype(BF16)
        _acc_rows(dgt_ref, dr * fv)

    v = _full((1, d))
    one = _full((1, 1))
    return _rowcall(body, "ln2_loss_bwd", s, tr, [_row(tr, d), _row(tr, d), _row(tr, d), v, v, v],
                    [_row(tr, d), _row(tr, d), one, v, v, v],
                    [jax.ShapeDtypeStruct((s, d), BF16), jax.ShapeDtypeStruct((s, d), F32),
                     jax.ShapeDtypeStruct((1, 1), F32)] + [jax.ShapeDtypeStruct((1, d), F32)] * 3,
                    (x1, f, target, gate, g, b))


def _ln1_bwd(dh2, dr2, x1, x, mix, scale2, gate1, g1):
    s, d = x.shape
    tr = _tile_rows(s, d)

    def body(dh_ref, dr2_ref, x1_ref, x_ref, m_ref, sc_ref, gt_ref, g_ref,
             dm_ref, dxa_ref, dsc_ref, dsh_ref, dg_ref, db_ref, dgt_ref):
        dh = dh_ref[...]
        _acc_rows(dsc_ref, dh * x1_ref[...])
        _acc_rows(dsh_ref, dh)
        dy = dh * (1.0 + sc_ref[...]) + DEEPNORM_ALPHA * dr2_ref[...]
        mv = m_ref[...]
        r = DEEPNORM_ALPHA * x_ref[...] + (1.0 + gt_ref[...]) * mv
        xh, rstd = _ln_stats(r)
        _acc_rows(dg_ref, dy * xh)
        _acc_rows(db_ref, dy)
        dxh = dy * g_ref[...]
        dr = rstd * (dxh - jnp.mean(dxh, axis=-1, keepdims=True) - xh * jnp.mean(dxh * xh, axis=-1, keepdims=True))
        dm_ref[...] = ((1.0 + gt_ref[...]) * dr).astype(BF16)
        dxa_ref[...] = DEEPNORM_ALPHA * dr
        _acc_rows(dgt_ref, dr * mv)

    v = _full((1, d))
    return _rowcall(body, "ln1_bwd", s, tr, [_row(tr, d)] * 5 + [v, v, v],
                    [_row(tr, d), _row(tr, d), v, v, v, v, v],
                    [jax.ShapeDtypeStruct((s, d), BF16), jax.ShapeDtypeStruct((s, d), F32)]
                    + [jax.ShapeDtypeStruct((1, d), F32)] * 5,
                    (dh2, dr2, x1, x, mix, scale2, gate1, g1))


def _dx_final(dxa, dh1, x, scale1):
    s, d = x.shape
    tr = _tile_rows(s, d)

    def body(a_ref, dh_ref, x_ref, sc_ref, o_ref, dsc_ref, dsh_ref):
        dh = dh_ref[...]
        o_ref[...] = a_ref[...] + dh * (1.0 + sc_ref[...])
        _acc_rows(dsc_ref, dh * x_ref[...])
        _acc_rows(dsh_ref, dh)

    v = _full((1, d))
    return _rowcall(body, "dx_final", s, tr, [_row(tr, d)] * 3 + [v], [_row(tr, d), v, v],
                    [jax.ShapeDtypeStruct((s, d), F32)] + [jax.ShapeDtypeStruct((1, d), F32)] * 2,
                    (dxa, dh1, x, scale1))


def _conv_silu_fwd(proj, conv_w, conv_b, s, u, comm=None):
    tr = _tile_rows(s, u)
    w = 2 * u
    half = SSM_CONV // 2

    def body(p0, c0, n0, p1, c1, n1, w_ref, b_ref, o_ref):
        for blk, (pr, cr, nr) in enumerate(((p0, c0, n0), (p1, c1, n1))):
            e = _ext(pr, cr, nr, s, tr)
            wv = w_ref[:, blk * u:(blk + 1) * u]
            acc = jnp.zeros_like(e)
            for k in range(SSM_CONV):
                acc = acc + _shift(e, k - half) * wv[k:k + 1, :]
            pre = acc[HALO:HALO + tr] + b_ref[:, blk * u:(blk + 1) * u]
            o_ref[:, blk * u:(blk + 1) * u] = _silu(pre)

    in_specs = _halo_specs(s, tr, u, 1) + _halo_specs(s, tr, u, 2) + [_full((SSM_CONV, w)), _full((1, w))]
    return _rowcall(body, "conv_silu_fwd", s, tr, in_specs, _row(tr, w), jax.ShapeDtypeStruct((s, w), F32),
                    (proj,) * 6 + (conv_w, conv_b), comm=comm)


def _conv_silu_bwd(proj, dxbc, conv_w, conv_b, s, u):
    tr = _tile_rows(s, u)
    w = 2 * u
    half = SSM_CONV // 2

    def body(p0, c0, n0, p1, c1, n1, dp0, dc0, dn0, dp1, dc1, dn1, w_ref, b_ref, du_ref, dw_ref, db_ref):
        for blk, (ur, dr) in enumerate((((p0, c0, n0), (dp0, dc0, dn0)), ((p1, c1, n1), (dp1, dc1, dn1)))):
            e = _ext(*ur, s, tr)
            de = _ext(*dr, s, tr)
            wv = w_ref[:, blk * u:(blk + 1) * u]
            acc = jnp.zeros_like(e)
            for k in range(SSM_CONV):
                acc = acc + _shift(e, k - half) * wv[k:k + 1, :]
            pre = acc + b_ref[:, blk * u:(blk + 1) * u]
            dpre = jnp.where(_valid_rows(e.shape, s, tr), de * _dsilu(pre), 0.0)
            du = jnp.zeros_like(e)
            rows = []
            for k in range(SSM_CONV):
                du = du + _shift(dpre, half - k) * wv[k:k + 1, :]
                rows.append(jnp.sum((_shift(e, k - half) * dpre)[HALO:HALO + tr], axis=0, keepdims=True))
            du_ref[:, blk * u:(blk + 1) * u] = du[HALO:HALO + tr].astype(BF16)
            dwv = jnp.concatenate(rows + [jnp.zeros((SUBLANES - SSM_CONV, u), F32)], axis=0)
            dbv = jnp.sum(dpre[HALO:HALO + tr], axis=0, keepdims=True)
            first = pl.program_id(0) == 0

            @pl.when(first)
            def _():
                dw_ref[:, blk * u:(blk + 1) * u] = dwv
                db_ref[:, blk * u:(blk + 1) * u] = dbv

            @pl.when(jnp.logical_not(first))
            def _():
                dw_ref[:, blk * u:(blk + 1) * u] += dwv
                db_ref[:, blk * u:(blk + 1) * u] += dbv

    in_specs = (_halo_specs(s, tr, u, 1) + _halo_specs(s, tr, u, 2) + _halo_specs(s, tr, u, 0)
                + _halo_specs(s, tr, u, 1) + [_full((SSM_CONV, w)), _full((1, w))])
    return _rowcall(body, "conv_silu_bwd", s, tr, in_specs,
                    [_row(tr, w), _full((SUBLANES, w)), _full((1, w))],
                    [jax.ShapeDtypeStruct((s, w), BF16), jax.ShapeDtypeStruct((SUBLANES, w), F32),
                     jax.ShapeDtypeStruct((1, w), F32)],
                    (proj,) * 6 + (dxbc,) * 6 + (conv_w, conv_b))


def _expanders(h):
    col64 = jnp.arange(2 * h * HEAD_DIM) // HEAD_DIM
    col128 = jnp.arange(2 * h * LANES) // LANES
    row = jnp.arange(LANES)[:, None]
    return (row == col64[None, :]).astype(BF16), (row == col128[None, :]).astype(BF16)


def _dt_prep(proj, bias_row, a_row, s, u, h, comm=None):
    q = CHUNK
    e64, e128 = _expanders(h)
    ds = h * HEAD_DIM
    dtblk = (6 * u) // LANES

    def body(raw_ref, b_ref, a_ref, e64_ref, e128_ref, dt_ref, cum_ref, dte_ref, cume_ref):
        lane = lax.broadcasted_iota(jnp.int32, (q, LANES), 1)
        dt = jnp.where(lane < 2 * h, _softplus(raw_ref[...] + b_ref[...]), 0.0)
        da = dt * a_ref[...]
        ii = lax.broadcasted_iota(jnp.int32, (q, q), 0)
        kk = lax.broadcasted_iota(jnp.int32, (q, q), 1)
        lower = (kk <= ii).astype(F32).astype(BF16)
        upper = (kk >= ii).astype(F32).astype(BF16)
        cum = jnp.where(lane < h, _dot3_l(lower, da), _dot3_l(upper, da))
        dt_ref[...] = dt
        cum_ref[...] = cum
        dte = _dot3_r(dt, e64_ref[...])
        cume = _dot3_r(cum, e128_ref[...])
        dte_ref[0] = dte[:, :ds]
        dte_ref[1] = dte[:, ds:]
        cume_ref[0] = cume[:, :h * LANES]
        cume_ref[1] = cume[:, h * LANES:]

    in_specs = [pl.BlockSpec((q, LANES), lambda i: (i, dtblk)), _full((1, LANES)), _full((1, LANES)),
                _full(e64.shape), _full(e128.shape)]
    out_specs = [_row(q, LANES), _row(q, LANES),
                 pl.BlockSpec((2, q, ds), lambda i: (0, i, 0)), pl.BlockSpec((2, q, h * LANES), lambda i: (0, i, 0))]
    out_shape = [jax.ShapeDtypeStruct((s, LANES), F32), jax.ShapeDtypeStruct((s, LANES), F32),
                 jax.ShapeDtypeStruct((2, s, ds), F32), jax.ShapeDtypeStruct((2, s, h * LANES), F32)]
    return _rowcall(body, "dt_prep", s, q, in_specs, out_specs, out_shape, (proj, bias_row, a_row, e64, e128), comm=comm)


def _ssd_specs(s, h, g):
    q = CHUNK
    nc = s // q
    ds = h * HEAD_DIM
    nb = g * STATE
    return q, nc, ds, nb


def _ssd_fwd(xbc, dt_e, cum_e, cum_t, s, h, g, comm=None):
    q, nc, ds, nb = _ssd_specs(s, h, g)
    npair = h // 2

    def cidx(d, i):
        return jnp.where(d == 0, i, nc - 1 - i)

    def body(x_ref, b_ref, c_ref, dt_ref, cum_ref, cumt_ref, y_ref, sp_ref, st):
        d = pl.program_id(0)
        i = pl.program_id(1)

        @pl.when(i == 0)
        def _():
            st[...] = jnp.zeros_like(st)

        rev = d == 1
        ii = lax.broadcasted_iota(jnp.int32, (q, q), 0)
        jj = lax.broadcasted_iota(jnp.int32, (q, q), 1)
        sgn = jnp.where(rev, -1, 1)
        mask = (jj - ii) * sgn <= 0
        left = lax.broadcasted_iota(jnp.int32, (q, LANES), 1) < HEAD_DIM

        def group(gi, carry):
            goff = pl.multiple_of(gi * STATE, STATE)
            cg = c_ref[:, pl.ds(goff, STATE)].astype(BF16)
            bg = b_ref[:, pl.ds(goff, STATE)].astype(BF16)
            gm = _dot_nt(cg, bg)
            for p in range(HEADS_PER_GROUP // 2):
                pr = gi * (HEADS_PER_GROUP // 2) + p
                off = pl.multiple_of(pr * LANES, LANES)
                xd = x_ref[:, pl.ds(off, LANES)] * dt_ref[:, pl.ds(off, LANES)]
                ms = []
                cols = []
                for hl in range(2):
                    hh = 2 * pr + hl
                    col = cum_ref[:, pl.ds(pl.multiple_of(hh * LANES, LANES), LANES)]
                    row = cumt_ref[pl.ds(hh, 1), :]
                    lm = jnp.where(mask, jnp.exp(jnp.minimum(col - row, 0.0)), 0.0)
                    ms.append((gm * lm).astype(BF16))
                    cols.append(col)
                y = _dot(ms[0], jnp.where(left, xd, 0.0).astype(BF16)) + _dot(ms[1], jnp.where(left, 0.0, xd).astype(BF16))
                ce = jnp.where(left, cols[0], cols[1])
                sprev = st[pr]
                sp_ref[pr] = sprev
                y = y + jnp.exp(ce) * _dot(cg, sprev.astype(BF16))
                y_ref[:, pl.ds(off, LANES)] = y
                tot = jnp.where(rev, ce[0:1, :], ce[q - 1:q, :])
                v = (xd * jnp.exp(tot - ce)).astype(BF16)
                st[pr] = jnp.exp(tot) * sprev + _dot_tn(bg, v)
            return carry

        lax.fori_loop(0, g, group, 0)

    in_specs = [
        pl.BlockSpec((q, ds), lambda d, i: (cidx(d, i), 0)),
        pl.BlockSpec((q, nb), lambda d, i: (cidx(d, i), ds // nb)),
        pl.BlockSpec((q, nb), lambda d, i: (cidx(d, i), ds // nb + 1)),
        pl.BlockSpec((None, q, ds), lambda d, i: (d, cidx(d, i), 0)),
        pl.BlockSpec((None, q, h * LANES), lambda d, i: (d, cidx(d, i), 0)),
        pl.BlockSpec((None, h, q), lambda d, i: (d, 0, cidx(d, i))),
    ]
    out_specs = [
        pl.BlockSpec((None, q, ds), lambda d, i: (d, cidx(d, i), 0)),
        pl.BlockSpec((None, None, npair, STATE, LANES), lambda d, i: (d, cidx(d, i), 0, 0, 0)),
    ]
    out_shape = [jax.ShapeDtypeStruct((2, s, ds), F32), jax.ShapeDtypeStruct((2, nc, npair, STATE, LANES), F32)]
    if comm is not None:
        return _comm_call(body, name="ssd_fwd", grid=(2, nc), in_specs=in_specs, out_specs=out_specs, out_shape=out_shape,
                          scratch_shapes=[pltpu.VMEM((npair, STATE, LANES), F32)],
                          args=(xbc, xbc, xbc, dt_e, cum_e, cum_t), comm=comm)
    return pl.pallas_call(
        body, name="ssd_fwd", grid=(2, nc), in_specs=in_specs, out_specs=out_specs, out_shape=out_shape,
        scratch_shapes=[pltpu.VMEM((npair, STATE, LANES), F32)],
        compiler_params=_params(("arbitrary", "arbitrary")),
    )(xbc, xbc, xbc, dt_e, cum_e, cum_t), ()


def _ssd_bwd(xbc, dt_e, cum_e, cum_t, dt_t, a_col, dy, sp, s, h, g, comm=None):
    q, nc, ds, nb = _ssd_specs(s, h, g)
    npair = h // 2

    def cidx(d, i):
        return jnp.where(d == 0, nc - 1 - i, i)

    def body(x_ref, b_ref, c_ref, dt_ref, cum_ref, cumt_ref, dtt_ref, a_ref, dy_ref, sp_ref,
             dx_ref, db_ref, dc_ref, ddt_ref, da_ref, dst, rowp):
        d = pl.program_id(0)
        i = pl.program_id(1)

        @pl.when(i == 0)
        def _():
            dst[...] = jnp.zeros_like(dst)
            da_ref[...] = jnp.zeros_like(da_ref)

        rev = d == 1
        ii = lax.broadcasted_iota(jnp.int32, (q, q), 0)
        jj = lax.broadcasted_iota(jnp.int32, (q, q), 1)
        sgn = jnp.where(rev, -1, 1)
        mask = (jj - ii) * sgn <= 0
        lane = lax.broadcasted_iota(jnp.int32, (q, LANES), 1)
        left = lane < HEAD_DIM
        rowp[...] = jnp.zeros_like(rowp)

        def group(gi, carry):
            acc_dcum, acc_tot, acc_dxx = carry
            goff = pl.multiple_of(gi * STATE, STATE)
            cg = c_ref[:, pl.ds(goff, STATE)].astype(BF16)
            bg = b_ref[:, pl.ds(goff, STATE)].astype(BF16)
            gm = _dot_nt(cg, bg)
            dgm = jnp.zeros((q, q), F32)
            dcg = jnp.zeros((q, STATE), F32)
            dbg = jnp.zeros((q, STATE), F32)
            for p in range(HEADS_PER_GROUP // 2):
                pr = gi * (HEADS_PER_GROUP // 2) + p
                off = pl.multiple_of(pr * LANES, LANES)
                xv = x_ref[:, pl.ds(off, LANES)]
                dte = dt_ref[:, pl.ds(off, LANES)]
                xd = xv * dte
                xdb = xd.astype(BF16)
                dyv = dy_ref[:, pl.ds(off, LANES)]
                sprev = sp_ref[pr]
                sprevb = sprev.astype(BF16)
                dsn = dst[pr]
                dsnb = dsn.astype(BF16)
                cols = [cum_ref[:, pl.ds(pl.multiple_of((2 * pr + hl) * LANES, LANES), LANES)] for hl in range(2)]
                ce = jnp.where(left, cols[0], cols[1])
                tot = jnp.where(rev, ce[0:1, :], ce[q - 1:q, :])
                et = jnp.exp(tot)
                r = jnp.exp(tot - ce)
                e = jnp.exp(ce)
                yoff = e * _dot(cg, sprevb)
                dz = (e * dyv).astype(BF16)
                dcg = dcg + _dot_nt(dz, sprevb)
                dsprev = _dot_tn(cg, dz) + et * dsn
                f1 = dyv * yoff
                v = (xd * r).astype(BF16)
                dbg = dbg + _dot_nt(v, dsnb)
                dv = _dot(bg, dsnb)
                dxd = dv * r
                tt = dv * xd * r
                wt = dsn * sprev * et
                for hl in range(2):
                    hh = 2 * pr + hl
                    hm = left if hl == 0 else jnp.logical_not(left)
                    row = cumt_ref[pl.ds(hh, 1), :]
                    lm = jnp.where(mask, jnp.exp(jnp.minimum(cols[hl] - row, 0.0)), 0.0)
                    mf = gm * lm
                    dym = jnp.where(hm, dyv, 0.0).astype(BF16)
                    dm = _dot_nt(dym, xdb)
                    dxd = dxd + _dot_tn(mf.astype(BF16), dym)
                    dgm = dgm + dm * lm
                    em = dm * mf
                    rowp[pl.ds(hh, 1), :] = rowp[pl.ds(hh, 1), :] - jnp.sum(em, axis=0, keepdims=True)
                    colq = (jnp.sum(em, axis=1, keepdims=True)
                            + jnp.sum(jnp.where(hm, f1 - tt, 0.0), axis=1, keepdims=True))
                    acc_dcum = jnp.where(lane == hh, colq, acc_dcum)
                    totq = jnp.sum(jnp.sum(jnp.where(hm, tt + wt, 0.0), axis=1, keepdims=True), axis=0, keepdims=True)
                    acc_tot = jnp.where(lane == hh, totq, acc_tot)
                dxx = dxd * xv
                for hl in range(2):
                    hh = 2 * pr + hl
                    hm = left if hl == 0 else jnp.logical_not(left)
                    acc_dxx = jnp.where(lane == hh, jnp.sum(jnp.where(hm, dxx, 0.0), axis=1, keepdims=True), acc_dxx)
                dx_ref[:, pl.ds(off, LANES)] = dxd * dte
                dst[pr] = dsprev
            dgb = dgm.astype(BF16)
            dc_ref[:, pl.ds(goff, STATE)] = dcg + _dot(dgb, bg)
            db_ref[:, pl.ds(goff, STATE)] = dbg + _dot_tn(dgb, cg)
            return acc_dcum, acc_tot, acc_dxx

        zero = jnp.zeros((q, LANES), F32)
        acc_dcum, acc_tot, acc_dxx = lax.fori_loop(0, g, group, (zero, zero, zero))
        dcum_t = rowp[...] + jnp.transpose(acc_dcum)[:h]
        rmat = ((ii - jj) * sgn >= 0).astype(F32).astype(BF16)
        da_t = _dot3_r(dcum_t, rmat) + jnp.transpose(acc_tot)[:h]
        ddt_ref[...] = da_t * a_ref[...] + jnp.transpose(acc_dxx)[:h]
        da_ref[...] += da_t * dtt_ref[...]

    in_specs = [
        pl.BlockSpec((q, ds), lambda d, i: (cidx(d, i), 0)),
        pl.BlockSpec((q, nb), lambda d, i: (cidx(d, i), ds // nb)),
        pl.BlockSpec((q, nb), lambda d, i: (cidx(d, i), ds // nb + 1)),
        pl.BlockSpec((None, q, ds), lambda d, i: (d, cidx(d, i), 0)),
        pl.BlockSpec((None, q, h * LANES), lambda d, i: (d, cidx(d, i), 0)),
        pl.BlockSpec((None, h, q), lambda d, i: (d, 0, cidx(d, i))),
        pl.BlockSpec((None, h, q), lambda d, i: (d, 0, cidx(d, i))),
        pl.BlockSpec((None, h, LANES), lambda d, i: (d, 0, 0)),
        pl.BlockSpec((q, ds), lambda d, i: (cidx(d, i), 0)),
        pl.BlockSpec((None, None, npair, STATE, LANES), lambda d, i: (d, cidx(d, i), 0, 0, 0)),
    ]
    out_specs = [
        pl.BlockSpec((None, q, ds), lambda d, i: (d, cidx(d, i), 0)),
        pl.BlockSpec((None, q, nb), lambda d, i: (d, cidx(d, i), 0)),
        pl.BlockSpec((None, q, nb), lambda d, i: (d, cidx(d, i), 0)),
        pl.BlockSpec((None, h, q), lambda d, i: (d, 0, cidx(d, i))),
        pl.BlockSpec((None, h, LANES), lambda d, i: (d, 0, 0)),
    ]
    out_shape = [jax.ShapeDtypeStruct((2, s, ds), F32), jax.ShapeDtypeStruct((2, s, nb), F32),
                 jax.ShapeDtypeStruct((2, s, nb), F32), jax.ShapeDtypeStruct((2, h, s), F32),
                 jax.ShapeDtypeStruct((2, h, LANES), F32)]
    scratch = [pltpu.VMEM((npair, STATE, LANES), F32), pltpu.VMEM((h, q), F32)]
    args = (xbc, xbc, xbc, dt_e, cum_e, cum_t, dt_t, a_col, dy, sp)
    if comm is not None:
        return _comm_call(body, name="ssd_bwd", grid=(2, nc), in_specs=in_specs, out_specs=out_specs, out_shape=out_shape,
                          scratch_shapes=scratch, args=args, comm=comm)
    return pl.pallas_call(
        body, name="ssd_bwd", grid=(2, nc), in_specs=in_specs, out_specs=out_specs, out_shape=out_shape,
        scratch_shapes=scratch, compiler_params=_params(("arbitrary", "arbitrary")),
    )(*args), ()


def _dt_bwd(ddt, proj, bias_row, s, u, h):
    tr = _tile_rows(s, 4 * LANES)
    dtblk = (6 * u) // LANES

    def body(d_ref, raw_ref, b_ref, o_ref, db_ref):
        lane = lax.broadcasted_iota(jnp.int32, (tr, LANES), 1)
        v = jnp.where(lane < 2 * h, d_ref[...] * _sigmoid(raw_ref[...] + b_ref[...]), 0.0)
        o_ref[...] = v.astype(BF16)
        _acc_rows(db_ref, v)

    return _rowcall(body, "dt_bwd", s, tr, [_row(tr, LANES), _row(tr, LANES, dtblk), _full((1, LANES))],
                    [_row(tr, LANES), _full((1, LANES))],
                    [jax.ShapeDtypeStruct((s, LANES), BF16), jax.ShapeDtypeStruct((1, LANES), F32)],
                    (ddt, proj, bias_row))


def _group_rms(v, gw):
    outs, facs = [], []
    for k in range(v.shape[1] // gw):
        blk = v[:, k * gw:(k + 1) * gw]
        f = lax.rsqrt(jnp.mean(blk * blk, axis=-1, keepdims=True) + RMS_EPS)
        outs.append(blk * f)
        facs.append(jnp.broadcast_to(f, blk.shape))
    return jnp.concatenate(outs, axis=1), jnp.concatenate(facs, axis=1)


def _group_rms_bwd(dn, n, fac, gw):
    outs = []
    for k in range(n.shape[1] // gw):
        sl = slice(k * gw, (k + 1) * gw)
        outs.append(fac[:, sl] * (dn[:, sl] - n[:, sl] * jnp.mean(dn[:, sl] * n[:, sl], axis=-1, keepdims=True)))
    return jnp.concatenate(outs, axis=1)


def _gate_norm_fwd(y2, xbc, proj, d_e, norm_w, s, u, g):
    tr = _tile_rows(s, u)
    gw = u // g

    def body(y_ref, x_ref, z_ref, d_ref, w_ref, o_ref):
        ys = y_ref[0] + y_ref[1] + d_ref[...] * x_ref[...]
        n, _ = _group_rms(ys * _silu(z_ref[...]), gw)
        o_ref[...] = (n * w_ref[...]).astype(BF16)

    return _rowcall(body, "gate_norm_fwd", s, tr,
                    [pl.BlockSpec((2, tr, u), lambda i: (0, i, 0)), _row(tr, u), _row(tr, u), _full((1, u)), _full((1, u))],
                    _row(tr, u), jax.ShapeDtypeStruct((s, u), BF16), (y2, xbc, proj, d_e, norm_w))


def _gate_norm_bwd(dymix, y2, xbc, proj, d_e, norm_w, s, u, g):
    tr = _tile_rows(s, u)
    gw = u // g

    def body(dy_ref, y_ref, x_ref, z_ref, d_ref, w_ref, dys_ref, dz_ref, dxs_ref, dw_ref, dd_ref):
        xv = x_ref[...]
        zv = z_ref[...]
        ys = y_ref[0] + y_ref[1] + d_ref[...] * xv
        sz = _silu(zv)
        n, fac = _group_rms(ys * sz, gw)
        dout = dy_ref[...]
        _acc_rows(dw_ref, dout * n)
        dyg = _group_rms_bwd(dout * w_ref[...], n, fac, gw)
        dys = dyg * sz
        dys_ref[...] = dys
        dz_ref[...] = (dyg * ys * _dsilu(zv)).astype(BF16)
        dxs_ref[...] = dys * d_ref[...]
        _acc_rows(dd_ref, dys * xv)

    v = _full((1, u))
    return _rowcall(body, "gate_norm_bwd", s, tr,
                    [_row(tr, u), pl.BlockSpec((2, tr, u), lambda i: (0, i, 0)), _row(tr, u), _row(tr, u), v, v],
                    [_row(tr, u), _row(tr, u), _row(tr, u), v, v],
                    [jax.ShapeDtypeStruct((s, u), F32), jax.ShapeDtypeStruct((s, u), BF16),
                     jax.ShapeDtypeStruct((s, u), F32), jax.ShapeDtypeStruct((1, u), F32), jax.ShapeDtypeStruct((1, u), F32)],
                    (dymix, y2, xbc, proj, d_e, norm_w))


def _shortconv_fwd(proj, conv_w, norm_w, s, u):
    tr = _tile_rows(s, u)
    half = SC_CONV // 2

    def body(hp, hc, hn, b_ref, cp, cc, cn, cw_ref, w_ref, o_ref):
        t = _ext(hp, hc, hn, s, tr) * _ext(cp, cc, cn, s, tr)
        wv = cw_ref[...]
        acc = jnp.zeros_like(t)
        for k in range(SC_CONV):
            acc = acc + _shift(t, k - half) * wv[k:k + 1, :]
        n, _ = _group_rms(b_ref[...] * acc[HALO:HALO + tr], SC_GROUP_WIDTH)
        o_ref[...] = (n * w_ref[...]).astype(BF16)

    in_specs = _halo_specs(s, tr, u, 3) + [_row(tr, u, 4)] + _halo_specs(s, tr, u, 5) + [_full((SC_CONV, u)), _full((1, u))]
    return _rowcall(body, "shortconv_fwd", s, tr, in_specs, _row(tr, u), jax.ShapeDtypeStruct((s, u), BF16),
                    (proj,) * 7 + (conv_w, norm_w))


def _shortconv_bwd(dymix, proj, conv_w, norm_w, s, u):
    tr = _tile_rows(s, u)
    half = SC_CONV // 2

    def body(dp, dc_, dn, hp, hc, hn, bp, bc, bn, cp, cc, cn, cw_ref, w_ref, dh_ref, db_ref, dcc_ref, dcw_ref, dw_ref):
        dout = _ext(dp, dc_, dn, s, tr)
        hv = _ext(hp, hc, hn, s, tr)
        bv = _ext(bp, bc, bn, s, tr)
        cv = _ext(cp, cc, cn, s, tr)
        t = hv * cv
        wv = cw_ref[...]
        acc = jnp.zeros_like(t)
        for k in range(SC_CONV):
            acc = acc + _shift(t, k - half) * wv[k:k + 1, :]
        n, fac = _group_rms(bv * acc, SC_GROUP_WIDTH)
        cur = slice(HALO, HALO + tr)
        _acc_rows(dw_ref, (dout * n)[cur])
        dyv = _group_rms_bwd(dout * w_ref[...], n, fac, SC_GROUP_WIDTH)
        db_ref[...] = (dyv * acc)[cur].astype(BF16)
        dv = dyv * bv
        dt = jnp.zeros_like(t)
        rows = []
        for k in range(SC_CONV):
            dt = dt + _shift(dv, half - k) * wv[k:k + 1, :]
            rows.append(jnp.sum((_shift(t, k - half) * dv)[cur], axis=0, keepdims=True))
        dh_ref[...] = (dt * cv)[cur].astype(BF16)
        dcc_ref[...] = (dt * hv)[cur].astype(BF16)
        dwv = jnp.concatenate(rows + [jnp.zeros((SUBLANES - SC_CONV, u), F32)], axis=0)
        first = pl.program_id(0) == 0

        @pl.when(first)
        def _():
            dcw_ref[...] = dwv

        @pl.when(jnp.logical_not(first))
        def _():
            dcw_ref[...] += dwv

    in_specs = (_halo_specs(s, tr, u, 1) + _halo_specs(s, tr, u, 3) + _halo_specs(s, tr, u, 4) + _halo_specs(s, tr, u, 5)
                + [_full((SC_CONV, u)), _full((1, u))])
    return _rowcall(body, "shortconv_bwd", s, tr, in_specs,
                    [_row(tr, u)] * 3 + [_full((SUBLANES, u)), _full((1, u))],
                    [jax.ShapeDtypeStruct((s, u), BF16)] * 3
                    + [jax.ShapeDtypeStruct((SUBLANES, u), F32), jax.ShapeDtypeStruct((1, u), F32)],
                    (dymix,) * 3 + (proj,) * 9 + (conv_w, norm_w))


def _adam_math(w, g, m, v):
    m2 = ADAM_B1 * m + (1.0 - ADAM_B1) * g
    v2 = ADAM_B2 * v + (1.0 - ADAM_B2) * (g * g)
    m_hat = m2 / (1.0 - ADAM_B1 ** ADAM_STEP)
    v_hat = v2 / (1.0 - ADAM_B2 ** ADAM_STEP)
    delta = -ADAM_LR * (m_hat / (jnp.sqrt(v_hat) + ADAM_EPS) + ADAM_WD * w)
    return delta, m2, v2


def _adam_rows(r, c):
    return _pick(r, max(SUBLANES, (1 << 20) // (4 * c)), SUBLANES)


def _adam(w, g, m, v, *, name):
    r, c = w.shape
    tr = _adam_rows(r, c)

    def body(w_ref, g_ref, m_ref, v_ref, d_ref, m2_ref, v2_ref):
        d_ref[...], m2_ref[...], v2_ref[...] = _adam_math(w_ref[...], g_ref[...], m_ref[...], v_ref[...])

    return _rowcall(body, name, r, tr, [_row(tr, c)] * 4, [_row(tr, c)] * 3,
                    [jax.ShapeDtypeStruct((r, c), F32)] * 3, (w, g, m, v))


def _adam_outer(w, a_t, bmat, m, v, *, name):
    r, c = w.shape
    tr = _adam_rows(r, c)
    kk = a_t.shape[1]

    def body(w_ref, a_ref, b_ref, m_ref, v_ref, g_ref, d_ref, m2_ref, v2_ref):
        g = _dot(a_ref[...].astype(BF16), b_ref[...].astype(BF16))
        g_ref[...] = g
        d_ref[...], m2_ref[...], v2_ref[...] = _adam_math(w_ref[...], g, m_ref[...], v_ref[...])

    return _rowcall(body, name, r, tr, [_row(tr, c), _row(tr, kk), _full((kk, c)), _row(tr, c), _row(tr, c)],
                    [_row(tr, c)] * 4, [jax.ShapeDtypeStruct((r, c), F32)] * 4, (w, a_t, bmat, m, v))


ANY = pl.BlockSpec(memory_space=pl.ANY)
VMEM_WHOLE = pl.BlockSpec(memory_space=pltpu.VMEM)


def _place():
    x, y, c = lax.axis_index("x"), lax.axis_index("y"), lax.axis_index("c")
    return x, y, c


DMA_CHUNKS = 8


def _n_chunks(rows):
    n = DMA_CHUNKS
    while n > 1 and rows % (16 * n):
        n //= 2
    return n


def _allgather_small(v, *, name):
    m_per, n = v.shape

    def body(x_ref, out_ref, send_sems, recv_sems, local_sem):
        x, y, c = _place()
        me, sibling = (x, y, c), (x, y, 1 - c)
        chips = [(1 - x, y), (x, 1 - y), (1 - x, 1 - y)]

        def rows(px, py, pc):
            return out_ref.at[pl.ds((4 * px + 2 * py + pc) * m_per, m_per), :]

        def copy(k, block, to, src=None):
            return pltpu.make_async_remote_copy(
                src_ref=rows(*block) if src is None else src, dst_ref=rows(*block),
                send_sem=send_sems.at[k], recv_sem=recv_sems.at[k], device_id=to, device_id_type=MESH)

        mine = pltpu.make_async_copy(x_ref, rows(*me), local_sem)
        mine.start()
        first = [copy(0, me, sibling, src=x_ref)]
        first += [copy(1 + j, me, (*chip, c), src=x_ref) for j, chip in enumerate(chips)]
        for cp in first:
            cp.start()
        passed = [copy(4 + j, (*chip, c), sibling) for j, chip in enumerate(chips)]
        for j, chip in enumerate(chips):
            copy(1 + j, (*chip, c), me).wait_recv()
            passed[j].start()
        copy(0, sibling, me).wait_recv()
        for j, chip in enumerate(chips):
            copy(4 + j, (*chip, 1 - c), me).wait_recv()
        for cp in first + passed:
            cp.wait_send()
        mine.wait()

    return pl.pallas_call(
        body, name=name, out_shape=jax.ShapeDtypeStruct((N_DEV * m_per, n), v.dtype),
        in_specs=[VMEM_WHOLE], out_specs=VMEM_WHOLE,
        scratch_shapes=[pltpu.SemaphoreType.DMA((7,)), pltpu.SemaphoreType.DMA((7,)), pltpu.SemaphoreType.DMA],
        compiler_params=pltpu.CompilerParams(vmem_limit_bytes=VMEM_LIMIT),
    )(v)


def _chip_id():
    return 2 * lax.axis_index("x") + lax.axis_index("y")


def _core_id():
    return lax.axis_index("c")


def _cast_into_block(wl, *, name, comm=None):
    r, c_ = wl.shape
    tr = _tile_rows(r, c_, 16)

    def body(w_ref, o_ref):
        o_ref[...] = w_ref[...].astype(BF16)

    in_spec = pl.BlockSpec((tr, c_), lambda i: (i, 0))
    out_spec = pl.BlockSpec((None, tr, c_), lambda i: (_chip_id(), i, 0))
    out_shape = jax.ShapeDtypeStruct((N_CHIPS, r, c_), BF16)
    if comm is not None:
        (out,), landed = _comm_call(body, name=name, grid=(r // tr,), in_specs=[in_spec], out_specs=[out_spec],
                                    out_shape=[out_shape], scratch_shapes=[], args=(wl,), comm=comm)
        return out, landed
    return pl.pallas_call(body, name=name, grid=(r // tr,), in_specs=[in_spec], out_specs=out_spec, out_shape=out_shape,
                          compiler_params=_params(("arbitrary",)))(wl)


def _gather_weight(buf, *, name, part=(0, 1)):
    cm = _gather_comm(buf, part)

    def body(in_ref, out_ref, *sems):
        cm["start"]((in_ref,), (out_ref,), sems)
        cm["finish"]((in_ref,), (out_ref,), sems)

    return pl.pallas_call(
        body, name=name, out_shape=jax.ShapeDtypeStruct(buf.shape, buf.dtype),
        in_specs=[ANY], out_specs=ANY, input_output_aliases={0: 0}, scratch_shapes=cm["scratch"],
    )(buf)


def _gather_comm(buf, part=(0, 1)):
    _, r, c_ = buf.shape
    half = r // 2
    n_all = _n_chunks(half)
    rows = half // n_all
    first_chunk = round(part[0] * n_all)
    nch = round(part[1] * n_all) - first_chunk

    def plan(out_ref):
        x, y, c = _place()
        me, sibling = (x, y, c), (x, y, 1 - c)
        chips = [(1 - x, y), (x, 1 - y), (1 - x, 1 - y)]
        return me, sibling, chips, c

    def copy(out_ref, sems, k, i, block, to):
        part = out_ref.at[2 * block[0] + block[1], pl.ds(block[2] * half + (first_chunk + i) * rows, rows), :]
        return pltpu.make_async_remote_copy(src_ref=part, dst_ref=part, send_sem=sems[0].at[k * nch + i],
                                            recv_sem=sems[1].at[k * nch + i], device_id=to, device_id_type=MESH)

    def start(cins, couts, sems):
        (out_ref,) = couts
        me, sibling, chips, c = plan(out_ref)
        for i in range(nch):
            for j, chip in enumerate(chips):
                copy(out_ref, sems, j, i, me, (*chip, c)).start()

    def finish(cins, couts, sems):
        (out_ref,) = couts
        me, sibling, chips, c = plan(out_ref)
        passed = []
        for i in range(nch):
            for j, chip in enumerate(chips):
                copy(out_ref, sems, j, i, (*chip, c), me).wait_recv()
                passed.append(copy(out_ref, sems, 3 + j, i, (*chip, c), sibling))
                passed[-1].start()
        for i in range(nch):
            for j, chip in enumerate(chips):
                copy(out_ref, sems, 3 + j, i, (*chip, 1 - c), me).wait_recv()
        for i in range(nch):
            for j, chip in enumerate(chips):
                copy(out_ref, sems, j, i, me, (*chip, c)).wait_send()
        for cp in passed:
            cp.wait_send()

    return dict(inputs=[buf], out_shape=[jax.ShapeDtypeStruct(buf.shape, buf.dtype)], aliases={0: 0},
                scratch=[pltpu.SemaphoreType.DMA((6 * nch,)), pltpu.SemaphoreType.DMA((6 * nch,))],
                start=start, finish=finish)


def _merge_comms(*comms):
    inputs, outs, aliases, scratch, spans = [], [], {}, [], []
    for cm in comms:
        i0, o0, s0 = len(inputs), len(outs), len(scratch)
        inputs += cm["inputs"]
        outs += cm["out_shape"]
        scratch += cm["scratch"]
        aliases.update({i0 + k: o0 + v for k, v in cm["aliases"].items()})
        spans.append((i0, len(inputs), o0, len(outs), s0, len(scratch)))

    def run(which):
        def f(cins, couts, sems):
            for cm, (i0, i1, o0, o1, s0, s1) in zip(comms, spans):
                cm[which](cins[i0:i1], couts[o0:o1], sems[s0:s1])
        return f

    return dict(inputs=inputs, out_shape=outs, aliases=aliases, scratch=scratch, start=run("start"), finish=run("finish"))


def _pair_exchange(gfull, *, name):
    nblk, r, c_ = gfull.shape
    half = r // 2
    nch = _n_chunks(half)
    rows = half // nch

    def body(g_ref, peer_ref, send_sems, recv_sems):
        x, y, c = _place()

        def copy(k, i, pc):
            return pltpu.make_async_remote_copy(
                src_ref=g_ref.at[k, pl.ds(pc * half + i * rows, rows), :], dst_ref=peer_ref.at[k, pl.ds(i * rows, rows), :],
                send_sem=send_sems.at[k * nch + i], recv_sem=recv_sems.at[k * nch + i],
                device_id=(x, y, 1 - c), device_id_type=MESH)

        sends = [copy(k, i, 1 - c) for i in range(nch) for k in range(nblk)]
        for cp in sends:
            cp.start()
        for cp in sends:
            cp.wait_recv()
        for cp in sends:
            cp.wait_send()

    return pl.pallas_call(
        body, name=name, out_shape=jax.ShapeDtypeStruct((nblk, half, c_), gfull.dtype),
        in_specs=[ANY], out_specs=ANY,
        scratch_shapes=[pltpu.SemaphoreType.DMA((nblk * nch,)), pltpu.SemaphoreType.DMA((nblk * nch,))],
    )(gfull)


def _pair_add(gfull, peer, *, name):
    nblk, r, c_ = gfull.shape
    half = r // 2
    tr = _pick(half, max(16, (1 << 20) // (2 * c_)), 16)
    per = half // tr

    def body(g_ref, p_ref, o_ref):
        o_ref[...] = (g_ref[...].astype(F32) + p_ref[...].astype(F32)).astype(BF16)

    return pl.pallas_call(
        body, name=name, grid=(nblk, per),
        in_specs=[pl.BlockSpec((None, tr, c_), lambda k, i: (k, _core_id() * per + i, 0)),
                  pl.BlockSpec((None, tr, c_), lambda k, i: (k, i, 0))],
        out_specs=pl.BlockSpec((None, tr, c_), lambda k, i: (k, i, 0)),
        out_shape=jax.ShapeDtypeStruct((nblk, half, c_), BF16),
        compiler_params=_params(("arbitrary", "arbitrary")))(gfull, peer)


def _scatter_comm(pre, part=(0, 1), recv=None):
    _, half, c_ = pre.shape
    n_all = _n_chunks(half)
    rows = half // n_all
    first_chunk = round(part[0] * n_all)
    nch = round(part[1] * n_all) - first_chunk

    def copies(cins, couts, sems):
        p_ref, r_ref = cins[0], couts[0]
        x, y, c = _place()
        out = []
        for i in range(nch):
            at = pl.ds((first_chunk + i) * rows, rows)
            for j, (tx, ty) in reversed(list(enumerate([(1 - x, y), (x, 1 - y), (1 - x, 1 - y)]))):
                out.append(pltpu.make_async_remote_copy(
                    src_ref=p_ref.at[2 * tx + ty, at, :], dst_ref=r_ref.at[j, at, :],
                    send_sem=sems[0].at[j * nch + i], recv_sem=sems[1].at[j * nch + i],
                    device_id=(tx, ty, c), device_id_type=MESH))
        return out

    def start(cins, couts, sems):
        for cp in copies(cins, couts, sems):
            cp.start()

    def finish(cins, couts, sems):
        cps = copies(cins, couts, sems)
        for cp in cps:
            cp.wait_recv()
        for cp in cps:
            cp.wait_send()

    return dict(inputs=[pre] if recv is None else [pre, recv], out_shape=[jax.ShapeDtypeStruct((3, half, c_), pre.dtype)],
                aliases={} if recv is None else {1: 0},
                scratch=[pltpu.SemaphoreType.DMA((3 * nch,)), pltpu.SemaphoreType.DMA((3 * nch,))],
                start=start, finish=finish)


def _sum_into_half(pre, recv, *, name):
    _, half, c_ = pre.shape
    n = recv.shape[0]
    tr = _pick(half, max(16, (1 << 19) // (2 * c_)), 16)
    per = half // tr

    def body(g_ref, r_ref, o_ref):
        acc = g_ref[...].astype(F32)
        for k in range(n):
            acc = acc + r_ref[k].astype(F32)
        o_ref[...] = acc

    return pl.pallas_call(
        body, name=name, grid=(per,),
        in_specs=[pl.BlockSpec((None, tr, c_), lambda i: (_chip_id(), i, 0)),
                  pl.BlockSpec((n, tr, c_), lambda i: (0, i, 0))],
        out_specs=pl.BlockSpec((tr, c_), lambda i: (_core_id() * per + i, 0)),
        out_shape=jax.ShapeDtypeStruct((2 * half, c_), F32), compiler_params=_params(("arbitrary",)))(pre, recv)


def _sum_slots(recv, *, name):
    n, r, c_ = recv.shape
    tr = _pick(r, max(16, (1 << 19) // (2 * c_)), 16)

    def body(r_ref, o_ref):
        acc = r_ref[0].astype(F32)
        for k in range(1, n):
            acc = acc + r_ref[k].astype(F32)
        o_ref[...] = acc

    return _rowcall(body, name, r, tr, [pl.BlockSpec((n, tr, c_), lambda i: (0, i, 0))], _row(tr, c_),
                    jax.ShapeDtypeStruct((r, c_), F32), (recv,))


def _swap_halves(buf, *, name):
    r, c_ = buf.shape
    half = r // 2
    nch = _n_chunks(half)
    rows = half // nch

    def body(in_ref, out_ref, send_sems, recv_sems):
        del in_ref
        x, y, c = _place()

        def copy(i, pc):
            part = out_ref.at[pl.ds(pc * half + i * rows, rows), :]
            return pltpu.make_async_remote_copy(
                src_ref=part, dst_ref=part, send_sem=send_sems.at[i], recv_sem=recv_sems.at[i],
                device_id=(x, y, 1 - c), device_id_type=MESH)

        sends = [copy(i, c) for i in range(nch)]
        for cp in sends:
            cp.start()
        for i in range(nch):
            copy(i, 1 - c).wait_recv()
        for cp in sends:
            cp.wait_send()

    return pl.pallas_call(
        body, name=name, out_shape=jax.ShapeDtypeStruct(buf.shape, buf.dtype),
        in_specs=[ANY], out_specs=ANY, input_output_aliases={0: 0},
        scratch_shapes=[pltpu.SemaphoreType.DMA((nch,)), pltpu.SemaphoreType.DMA((nch,))],
    )(buf)


def _prereduce(gfull, *, name):
    return _pair_add(gfull, _pair_exchange(gfull, name=name + "_pair"), name=name + "_padd")


def _finish_reduce(pre, recv, *, name):
    return _swap_halves(_sum_into_half(pre, recv, name=name + "_sum"), name=name + "_swap")


PACK_ROWS = 16


def _pack(parts):
    flat = [p.reshape(-1).astype(F32) for p in parts]
    n = sum(f.shape[0] for f in flat)
    unit = PACK_ROWS * LANES
    total = -(-n // unit) * unit
    if total > n:
        flat.append(jnp.zeros((total - n,), F32))
    where, off = [], 0
    for p in parts:
        where.append((off, p.shape))
        off += p.size
    return jnp.concatenate(flat).reshape(total // LANES, LANES), where


def _unpack(flat, where):
    v = flat.reshape(-1)
    return [v[off:off + _size(shape)].reshape(shape) for off, shape in where]


def _size(shape):
    n = 1
    for d in shape:
        n *= d
    return n


def _sample_step(x, target, mods, h1, w_in_p, bufs, sp):
    s, d = x.shape
    u = d // 2
    h = u // HEAD_DIM
    g = h // HEADS_PER_GROUP
    pw = w_in_p.shape[1]
    din = 6 * u + 2 * h
    dff_ = bufs["w_up"].shape[2] * N_CHIPS
    shift1, scale1, gate1, shift2, scale2, gate2 = mods

    a_f = -jnp.exp(sp["ssm_a_log_f"].reshape(-1))
    a_b = -jnp.exp(sp["ssm_a_log_b"].reshape(-1))
    pad_l = LANES - 2 * h
    a_row = jnp.pad(jnp.concatenate([a_f, a_b]), (0, pad_l)).reshape(1, LANES)
    bias_row = jnp.pad(jnp.concatenate([sp["ssm_dt_bias_f"].reshape(-1), sp["ssm_dt_bias_b"].reshape(-1)]),
                       (0, pad_l)).reshape(1, LANES)
    a_col = jnp.broadcast_to(jnp.stack([a_f, a_b])[:, :, None], (2, h, LANES))
    d_e = jnp.repeat(sp["ssm_d"].reshape(-1), HEAD_DIM).reshape(1, u)
    conv_w, conv_b = sp["ssm_conv_w"], sp["ssm_conv_b"].reshape(1, 2 * u)
    sc_conv_w = sp["sc_conv_w"]
    ssm_norm_w, sc_norm_w = sp["ssm_norm_w"].reshape(1, u), sp["sc_norm_w"].reshape(1, u)
    ln1_g, ln1_b = sp["ln1_g"].reshape(1, d), sp["ln1_b"].reshape(1, d)
    ln2_g, ln2_b = sp["ln2_g"].reshape(1, d), sp["ln2_b"].reshape(1, d)

    e = 1.0 / DMA_CHUNKS
    proj, (w_up_b, w_out_blk) = _matmul(
        h1, w_in_p, name="mm_proj", tn=1280,
        comm=_merge_comms(_gather_comm(bufs["w_up"], (0, 3 * e)), _gather_comm(bufs["w_out"])))
    w_out = w_out_blk.reshape(d, d)
    xbc, (w_down_b,) = _conv_silu_fwd(proj, conv_w, conv_b, s, u, comm=_gather_comm(bufs["w_down"], (0, e)))
    (dt, cum, dt_e, cum_e), (w_up_b,) = _dt_prep(proj, bias_row, a_row, s, u, h, comm=_gather_comm(w_up_b, (3 * e, 4 * e)))
    cum_t = jnp.stack([cum[:, :h].T, cum[:, h:2 * h].T])
    dt_t = jnp.stack([dt[:, :h].T, dt[:, h:2 * h].T])
    (y2, states), (w_up_b,) = _ssd_fwd(xbc, dt_e, cum_e, cum_t, s, h, g, comm=_gather_comm(w_up_b, (4 * e, 6 * e)))
    y_ssm = _gate_norm_fwd(y2, xbc, proj, d_e, ssm_norm_w, s, u, g)
    y_sc = _shortconv_fwd(proj, sc_conv_w, sc_norm_w, s, u)
    ymix = jnp.concatenate([y_ssm, y_sc], axis=1)
    mix, (w_up_b,) = _matmul(ymix, w_out, name="mm_mix", comm=_gather_comm(w_up_b, (6 * e, 7 * e)))
    (x1, h2), (w_up_blk,) = _ln1_fwd(x, mix, gate1, ln1_g, ln1_b, scale2, shift2, comm=_gather_comm(w_up_b, (7 * e, 1)))
    (up, ff), (w_down_blk,) = _matmul(h2, w_up_blk, name="mm_up", b_blocks=N_CHIPS, epilogue="relu2",
                                      comm=_gather_comm(w_down_b, (e, 1)))
    w_down = w_down_blk.reshape(dff_, d)
    f = _matmul(ff, w_down, name="mm_down")
    df, dr2, loss, dg2, db2, dgate2 = _ln2_loss_bwd(x1, f, target, gate2, ln2_g, ln2_b)

    gw_down = _matmul(ff, df, name="mm_gw_down", ta=True, out_dtype=BF16)
    pre_down = _prereduce(gw_down.reshape(N_CHIPS, dff_ // N_CHIPS, d), name="rs_w_down")
    du, (rv_down,) = _matmul(df, w_down, name="mm_dff", tb=True, out_dtype=BF16, epilogue="relu2_bwd", extra=up,
                             comm=_scatter_comm(pre_down, (0, 0.5)))
    gw_up, (rv_down,) = _matmul(h2, du, name="mm_gw_up", ta=True, out_dtype=BF16, out_blocks=N_CHIPS,
                                comm=_scatter_comm(pre_down, (0.5, 1), recv=rv_down))
    pre_up = _prereduce(gw_up, name="rs_w_up")
    dh2, (rv_up,) = _matmul(du, w_up_blk, name="mm_dh2", tb=True, b_blocks=N_CHIPS, comm=_scatter_comm(pre_up, (0, 0.5)))
    dmix, dxa, dscale2, dshift2, dg1, db1, dgate1 = _ln1_bwd(dh2, dr2, x1, x, mix, scale2, gate1, ln1_g)
    gw_out = _matmul(ymix, dmix, name="mm_gw_out", ta=True, out_dtype=BF16)
    pre_out = _prereduce(gw_out.reshape(N_CHIPS, d // N_CHIPS, d), name="rs_w_out")
    dymix, (rv_out,) = _matmul(dmix, w_out, name="mm_dymix", tb=True, comm=_scatter_comm(pre_out))
    dys, dz, dxs, dnw, dd_e = _gate_norm_bwd(dymix, y2, xbc, proj, d_e, ssm_norm_w, s, u, g)
    (dx2, dbb, dcc, ddt_t, da), (rv_up,) = _ssd_bwd(xbc, dt_e, cum_e, cum_t, dt_t, a_col, dys, states, s, h, g,
                                                    comm=_scatter_comm(pre_up, (0.5, 1), recv=rv_up))
    dxbc = jnp.concatenate([dx2[0] + dx2[1] + dxs, dbb[0] + dbb[1], dcc[0] + dcc[1]], axis=1)
    du_xbc, dcw, dcb = _conv_silu_bwd(proj, dxbc, conv_w, conv_b, s, u)
    ddt = jnp.pad(jnp.concatenate([ddt_t[0].T, ddt_t[1].T], axis=1), ((0, 0), (0, pad_l)))
    ddt_raw, dbias = _dt_bwd(ddt, proj, bias_row, s, u, h)
    dh_sc, db_sc, dc_sc, dscw, dscnw = _shortconv_bwd(dymix, proj, sc_conv_w, sc_norm_w, s, u)
    dproj = jnp.concatenate([dz, du_xbc, dh_sc, db_sc, dc_sc, ddt_raw,
                             jnp.zeros((s, pw - 6 * u - LANES), BF16)], axis=1)
    gp = _matmul(h1, dproj, name="mm_gw_in", ta=True, out_dtype=BF16, tn=1280)
    pre_in = _prereduce(_from_p_layout(gp, u, h, N_CHIPS), name="rs_w_in")
    dh1, (rv_in,) = _matmul(dproj, w_in_p, name="mm_dh1", tb=True, tk=2560, comm=_scatter_comm(pre_in))
    grad_x, dscale1, dshift1 = _dx_final(dxa, dh1, x, scale1)
    big = {"w_down": _finish_reduce(pre_down, rv_down, name="rs_w_down"),
           "w_up": _finish_reduce(pre_up, rv_up, name="rs_w_up"),
           "w_out": _finish_reduce(pre_out, rv_out, name="rs_w_out"),
           "w_in": _finish_reduce(pre_in, rv_in, name="rs_w_in")}

    small = {
        "dmod": jnp.concatenate([dshift1, dscale1, dgate1, dshift2, dscale2, dgate2], axis=1),
        "ssm_conv_b": dcb,
        "ssm_dt_bias_f": dbias[0, :h], "ssm_dt_bias_b": dbias[0, h:2 * h],
        "ssm_a_log_f": jnp.sum(da[0], axis=1) * a_f, "ssm_a_log_b": jnp.sum(da[1], axis=1) * a_b,
        "ssm_d": jnp.sum(dd_e.reshape(h, HEAD_DIM), axis=1),
        "ssm_norm_w": dnw, "sc_norm_w": dscnw,
        "ln1_g": dg1, "ln1_b": db1, "ln2_g": dg2, "ln2_b": db2,
        "ssm_conv_w": dcw[:SSM_CONV], "sc_conv_w": dscw[:SC_CONV],
    }
    return loss, grad_x, big, small


WEIGHTS = ['w_ada', 'b_ada', 'w_in', 'ssm_conv_w', 'ssm_conv_b', 'ssm_dt_bias_f', 'ssm_dt_bias_b', 'ssm_a_log_f',
           'ssm_a_log_b', 'ssm_d', 'ssm_norm_w', 'sc_conv_w', 'sc_norm_w', 'w_out', 'ln1_g', 'ln1_b', 'w_up', 'w_down',
           'ln2_g', 'ln2_b']
BIG = ('w_ada', 'w_in', 'w_out', 'w_up', 'w_down')
SMALL = tuple(n for n in WEIGHTS if n not in BIG)
SMALL_SHARDED = ('ssm_conv_w', 'sc_conv_w')


def _p_layout_width(u):
    return -(-(6 * u + LANES) // 512) * 512


def _p_segments(u, h):
    return [((0, 3 * u), 0), ((3 * u, 3 * u + 2 * h), 6 * u), ((3 * u + 2 * h, 6 * u + 2 * h), 3 * u)]


def _to_p_layout(blocks, u, h, pw):
    nblk, d, w = blocks.shape
    parts = []
    for (lo, hi), _ in sorted(_p_segments(u, h), key=lambda t: t[1]):
        for k in range(nblk):
            a, b = max(lo, k * w), min(hi, (k + 1) * w)
            if a < b:
                parts.append(blocks[k][:, a - k * w:b - k * w])
    parts.append(jnp.zeros((d, pw - nblk * w), blocks.dtype))
    return jnp.concatenate(parts, axis=1)


def _from_p_layout(gp, u, h, nblk):
    w = (6 * u + 2 * h) // nblk
    blocks = []
    for k in range(nblk):
        parts = []
        for (lo, hi), poff in _p_segments(u, h):
            a, b = max(lo, k * w), min(hi, (k + 1) * w)
            if a < b:
                parts.append(gp[:, poff + a - lo:poff + b - lo])
        blocks.append(jnp.concatenate(parts, axis=1))
    return jnp.stack(blocks)


def kernel(x, c, w_ada, b_ada, w_in, ssm_conv_w, ssm_conv_b, ssm_dt_bias_f, ssm_dt_bias_b, ssm_a_log_f, ssm_a_log_b, ssm_d, ssm_norm_w, sc_conv_w, sc_norm_w, w_out, ln1_g, ln1_b, w_up, w_down, ln2_g, ln2_b, loss_target, m_w_ada, m_b_ada, m_w_in, m_ssm_conv_w, m_ssm_conv_b, m_ssm_dt_bias_f, m_ssm_dt_bias_b, m_ssm_a_log_f, m_ssm_a_log_b, m_ssm_d, m_ssm_norm_w, m_sc_conv_w, m_sc_norm_w, m_w_out, m_ln1_g, m_ln1_b, m_w_up, m_w_down, m_ln2_g, m_ln2_b, v_w_ada, v_b_ada, v_w_in, v_ssm_conv_w, v_ssm_conv_b, v_ssm_dt_bias_f, v_ssm_dt_bias_b, v_ssm_a_log_f, v_ssm_a_log_b, v_ssm_d, v_ssm_norm_w, v_sc_conv_w, v_sc_norm_w, v_w_out, v_ln1_g, v_ln1_b, v_w_up, v_w_down, v_ln2_g, v_ln2_b):
    given = dict(locals())
    w = {n: given[n][0] for n in WEIGHTS}
    m = {n: given["m_" + n][0] for n in WEIGHTS}
    v = {n: given["v_" + n][0] for n in WEIGHTS}
    xs, tgt = x[0], loss_target[0]
    s, d = xs.shape
    u = d // 2
    h = u // HEAD_DIM
    nmod = N_MOD * d
    nmod_loc = nmod // N_CHIPS
    ax, ay, ac = lax.axis_index("x"), lax.axis_index("y"), lax.axis_index("c")
    chip = 2 * ax + ay
    me = 2 * chip + ac

    pay1, where1 = _pack([c[0], w["ssm_conv_w"], w["sc_conv_w"]])
    g1 = _allgather_small(pay1, name="ag_inputs").reshape(N_DEV, -1)
    per_dev = [_unpack(g1[k], where1) for k in range(N_DEV)]
    c_all = jnp.stack([p[0] for p in per_dev])
    ssm_conv_w_full = jnp.concatenate([per_dev[2 * k][1] for k in range(N_CHIPS)], axis=1)
    sc_conv_w_full = jnp.concatenate([per_dev[2 * k][2] for k in range(N_CHIPS)], axis=1)

    sc_all = _silu(c_all)
    sc16 = jnp.pad(sc_all, ((0, 16 - N_DEV), (0, 0)))
    b_loc = lax.dynamic_slice(w["b_ada"], (chip * nmod_loc,), (nmod_loc,))
    mod_loc = _matmul(sc16, w["w_ada"], name="mm_mod")[:N_DEV] + b_loc[None, :]
    pay2, where2 = _pack([mod_loc])
    g2 = _allgather_small(pay2, name="ag_mod").reshape(N_DEV, -1)
    mod_blocks = jnp.stack([_unpack(g2[2 * k], where2)[0] for k in range(N_CHIPS)])
    mod_mine = lax.dynamic_index_in_dim(mod_blocks, me, axis=1, keepdims=False).reshape(N_MOD, 1, d)
    mods = [mod_mine[k] for k in range(N_MOD)]

    din = w["w_in"].shape[1] * N_CHIPS

    e = 1.0 / DMA_CHUNKS
    g_in = _cast_into_block(w["w_in"], name="cast_w_in")
    bufs = {"w_out": _cast_into_block(w["w_out"], name="cast_w_out")}
    bufs["w_up"], (g_in,) = _cast_into_block(w["w_up"], name="cast_w_up", comm=_gather_comm(g_in, (0, e)))
    bufs["w_down"], (g_in,) = _cast_into_block(w["w_down"], name="cast_w_down", comm=_gather_comm(g_in, (e, 2 * e)))
    h1, (g_in,) = _modulate(xs, mods[1], mods[0], name="modulate1", comm=_gather_comm(g_in, (2 * e, 3 * e)))
    g_in = _gather_weight(g_in, name="gather_w_in", part=(3 * e, 1))
    w_in_p = _to_p_layout(g_in, u, h, _p_layout_width(u))

    sp = {n: w[n] for n in SMALL}
    sp["ssm_conv_w"], sp["sc_conv_w"] = ssm_conv_w_full, sc_conv_w_full
    loss_loc, grad_x, big, small = _sample_step(xs, tgt, mods, h1, w_in_p, bufs, sp)

    small_names = ["dmod"] + [n for n in SMALL if n != "b_ada"]
    pay3, where3 = _pack([loss_loc] + [small[n] for n in small_names])
    g3 = _allgather_small(pay3, name="ag_small_grads")
    tot = _unpack(_sum_slots(g3.reshape(N_DEV, -1, LANES), name="sum_small_grads"), where3)
    loss = tot[0].reshape(())
    gsum = dict(zip(small_names, tot[1:]))
    dmod_all = jnp.stack([_unpack(g3.reshape(N_DEV, -1)[k], where3)[1].reshape(-1) for k in range(N_DEV)])

    grads = {}
    grads["b_ada"] = gsum["dmod"].reshape(-1)
    for n in SMALL:
        if n in SMALL_SHARDED:
            loc = w[n].shape[1]
            grads[n] = lax.dynamic_slice_in_dim(gsum[n], chip * loc, loc, axis=1)
        elif n != "b_ada":
            grads[n] = gsum[n].reshape(w[n].shape)

    grads.update(big)

    delta, new_m, new_v = {}, {}, {}
    dm_loc = lax.dynamic_slice_in_dim(dmod_all, chip * nmod_loc, nmod_loc, axis=1)
    grads["w_ada"], delta["w_ada"], new_m["w_ada"], new_v["w_ada"] = _adam_outer(
        w["w_ada"], sc16.T, jnp.pad(dm_loc, ((0, 16 - N_DEV), (0, 0))), m["w_ada"], v["w_ada"], name="adam_w_ada")
    for n in ("w_in", "w_out", "w_up", "w_down"):
        delta[n], new_m[n], new_v[n] = _adam(w[n], grads[n], m[n], v[n], name="adam_" + n)
    pw_, where_s = _pack([w[n] for n in SMALL])
    pg_, _ = _pack([grads[n] for n in SMALL])
    pm_, _ = _pack([m[n] for n in SMALL])
    pv_, _ = _pack([v[n] for n in SMALL])
    sd, sm, sv = _adam(pw_, pg_, pm_, pv_, name="adam_small")
    for n, a, b_, c_ in zip(SMALL, _unpack(sd, where_s), _unpack(sm, where_s), _unpack(sv, where_s)):
        delta[n], new_m[n], new_v[n] = a, b_, c_

    def lead(t):
        return t[None]

    return (loss, grad_x[None], *[lead(grads[n].reshape(w[n].shape)) for n in WEIGHTS],
            *[lead(delta[n]) for n in WEIGHTS], *[lead(new_m[n]) for n in WEIGHTS], *[lead(new_v[n]) for n in WEIGHTS])
```

```python
import functools

import jax
import jax.numpy as jnp
from jax import lax
from jax.experimental import pallas as pl
from jax.experimental.pallas import tpu as pltpu

F32 = jnp.float32
BF16 = jnp.bfloat16

CHUNK = 128
HEAD_DIM = 64
STATE = 128
HEADS_PER_GROUP = 4
SC_GROUP_WIDTH = 128
SSM_CONV = 5
SC_CONV = 3
N_MOD = 6
DEEPNORM_ALPHA = 2.0 ** 0.25
LN_EPS = 1e-5
RMS_EPS = 1e-5
ADAM_LR = 0.001
ADAM_B1 = 0.9
ADAM_B2 = 0.999
ADAM_EPS = 1e-08
ADAM_WD = 0.01
ADAM_STEP = 10
N_CHIPS = 4
N_DEV = 8
LANES = 128
SUBLANES = 8
HALO = 8
VMEM_LIMIT = 56 * 1024 * 1024
MESH = pl.DeviceIdType.MESH


def _params(sem=None):
    return pltpu.CompilerParams(dimension_semantics=sem, vmem_limit_bytes=VMEM_LIMIT)


def _pick(n, target, mult=LANES):
    best = None
    t = mult
    while t <= min(n, target):
        if n % t == 0:
            best = t
        t += mult
    return best if best is not None else n


ROW_TILE_BYTES = 2 << 20


def _tile_rows(s, width, mult=SUBLANES):
    return _pick(s, max(mult, ROW_TILE_BYTES // (4 * width)), mult)


def _sigmoid(v):
    return 1.0 / (1.0 + jnp.exp(-v))


def _silu(v):
    return v * _sigmoid(v)


def _dsilu(v):
    s = _sigmoid(v)
    return s * (1.0 + v * (1.0 - s))


def _softplus(v):
    e = jnp.exp(-jnp.abs(v))
    return jnp.maximum(v, 0.0) + jnp.where(e < 1e-4, e - 0.5 * e * e, jnp.log(1.0 + e))


def _dot(a, b):
    return jnp.dot(a, b, preferred_element_type=F32)


def _dot_nt(a, b):
    return lax.dot_general(a, b, (((1,), (1,)), ((), ())), preferred_element_type=F32)


def _dot_tn(a, b):
    return lax.dot_general(a, b, (((0,), (0,)), ((), ())), preferred_element_type=F32)


def _split3(v):
    hi = v.astype(BF16)
    r1 = v - hi.astype(F32)
    mid = r1.astype(BF16)
    lo = (r1 - mid.astype(F32)).astype(BF16)
    return hi, mid, lo


def _dot3_r(v, onehot):
    hi, mid, lo = _split3(v)
    return _dot(hi, onehot) + _dot(mid, onehot) + _dot(lo, onehot)


def _dot3_l(onehot, v):
    hi, mid, lo = _split3(v)
    return _dot(onehot, hi) + _dot(onehot, mid) + _dot(onehot, lo)


MATMUL_VMEM_BUDGET = 44 * 1024 * 1024


def _matmul(a, b, *, name, ta=False, tb=False, out_dtype=F32, b_blocks=1, out_blocks=1,
            tm=1024, tn=1024, tk=4096, comm=None, epilogue=None, extra=None):
    if ta:
        K, M = a.shape
    else:
        M, K = a.shape
    if b_blocks > 1:
        nb, r_, c_ = b.shape
        if tb:
            N, K2 = r_, c_ * nb
        else:
            K2, N = r_, c_ * nb
    else:
        if tb:
            N, K2 = b.shape
        else:
            K2, N = b.shape
    assert K == K2, (a.shape, b.shape, ta, tb)
    assert not (ta and tb)
    n_unit = N // b_blocks if (b_blocks > 1 and not tb) else N
    n_unit = min(n_unit, N // out_blocks)
    tn = _pick(n_unit, tn)
    k_unit = K // b_blocks if (b_blocks > 1 and tb) else K
    tk = _pick(k_unit, tk)
    tile_bytes = {None: jnp.dtype(out_dtype).itemsize, "relu2": 6, "relu2_bwd": 6}[epilogue]

    def vmem_need(tm_):
        need = 2 * (tm_ * tk * a.dtype.itemsize + tk * tn * b.dtype.itemsize) + 2 * tm_ * tn * tile_bytes
        return need + (tm_ * tn * 4 if K > tk else 0)

    tm = _pick(M, tm)
    while vmem_need(tm) > MATMUL_VMEM_BUDGET and tm % 2 == 0 and tm // 2 >= LANES:
        tm //= 2
    gm, gn, gk = M // tm, N // tn, K // tk

    if ta:
        a_spec = pl.BlockSpec((tk, tm), lambda i, j, k: (k, i))
    else:
        a_spec = pl.BlockSpec((tm, tk), lambda i, j, k: (i, k))
    if b_blocks > 1 and not tb:
        per = (N // b_blocks) // tn
        b_spec = pl.BlockSpec((None, tk, tn), lambda i, j, k: (j // per, k, j % per))
    elif b_blocks > 1 and tb:
        per = (K // b_blocks) // tk
        b_spec = pl.BlockSpec((None, tn, tk), lambda i, j, k: (k // per, j, k % per))
    elif tb:
        b_spec = pl.BlockSpec((tn, tk), lambda i, j, k: (j, k))
    else:
        b_spec = pl.BlockSpec((tk, tn), lambda i, j, k: (k, j))
    if out_blocks > 1:
        per_o = (N // out_blocks) // tn
        o_spec = pl.BlockSpec((None, tm, tn), lambda i, j, k: (j // per_o, i, j % per_o))
        o_shape = jax.ShapeDtypeStruct((out_blocks, M, N // out_blocks), out_dtype)
    else:
        o_spec = pl.BlockSpec((tm, tn), lambda i, j, k: (i, j))
        o_shape = jax.ShapeDtypeStruct((M, N), out_dtype)

    in_specs, args = [a_spec, b_spec], [a, b]
    out_specs, out_shape = [o_spec], [o_shape]
    if epilogue == "relu2":
        assert out_blocks == 1 and out_dtype == F32
        out_specs.append(o_spec)
        out_shape.append(jax.ShapeDtypeStruct((M, N), BF16))
    elif epilogue == "relu2_bwd":
        assert out_blocks == 1 and out_dtype == BF16
        in_specs.append(o_spec)
        args.append(extra)
    n_in = len(in_specs)

    def write(refs, p):
        o_ref = refs[n_in]
        if epilogue == "relu2":
            o_ref[...] = p
            r = jnp.maximum(p, 0.0)
            refs[n_in + 1][...] = (r * r).astype(BF16)
        elif epilogue == "relu2_bwd":
            o_ref[...] = (p * 2.0 * jnp.maximum(refs[2][...], 0.0)).astype(BF16)
        else:
            o_ref[...] = p.astype(out_dtype)

    def body(*refs):
        av = refs[0][...].astype(BF16)
        bv = refs[1][...].astype(BF16)
        p = _dot_tn(av, bv) if ta else (_dot_nt(av, bv) if tb else _dot(av, bv))
        if gk == 1:
            write(refs, p)
            return
        acc = refs[-1]
        k = pl.program_id(2)

        @pl.when(k == 0)
        def _():
            acc[...] = p

        @pl.when(jnp.logical_and(k > 0, k < gk - 1))
        def _():
            acc[...] += p

        @pl.when(k == gk - 1)
        def _():
            write(refs, acc[...] + p)

    scratch = [pltpu.VMEM((tm, tn), F32)] if gk > 1 else []
    if comm is not None:
        outs, landed = _comm_call(body, name=name, grid=(gm, gn, gk), in_specs=in_specs, out_specs=out_specs,
                                  out_shape=out_shape, scratch_shapes=scratch, args=args, comm=comm)
        return (outs[0] if len(outs) == 1 else tuple(outs)), landed
    outs = pl.pallas_call(
        body, name=name, grid=(gm, gn, gk), in_specs=in_specs, out_specs=out_specs,
        out_shape=out_shape, scratch_shapes=scratch,
        compiler_params=_params(("parallel", "parallel", "arbitrary")),
    )(*args)
    return outs[0] if len(outs) == 1 else tuple(outs)


def _comm_call(body, *, name, grid, in_specs, out_specs, out_shape, scratch_shapes, args, comm, aliases=None):
    n_in, n_out, n_scr = len(in_specs), len(out_shape), len(scratch_shapes)
    c_in, c_out = list(comm["inputs"]), list(comm["out_shape"])
    nci, nco = len(c_in), len(c_out)
    hbm = pl.BlockSpec(memory_space=pl.ANY)

    def body2(*refs):
        ins, cins = refs[:n_in], refs[n_in:n_in + nci]
        o0 = n_in + nci
        outs, couts = refs[o0:o0 + n_out], refs[o0 + n_out:o0 + n_out + nco]
        s0 = o0 + n_out + nco
        scr, cscr = refs[s0:s0 + n_scr], refs[s0 + n_scr:]
        first = functools.reduce(jnp.logical_and, [pl.program_id(a) == 0 for a in range(len(grid))])
        last = functools.reduce(jnp.logical_and, [pl.program_id(a) == grid[a] - 1 for a in range(len(grid))])

        @pl.when(first)
        def _():
            comm["start"](cins, couts, cscr)

        body(*ins, *outs, *scr)

        @pl.when(last)
        def _():
            comm["finish"](cins, couts, cscr)

    res = pl.pallas_call(
        body2, name=name, grid=grid, in_specs=list(in_specs) + [hbm] * nci, out_specs=list(out_specs) + [hbm] * nco,
        out_shape=list(out_shape) + c_out, scratch_shapes=list(scratch_shapes) + list(comm["scratch"]),
        input_output_aliases={**(aliases or {}), **{n_in + k: n_out + v for k, v in comm.get("aliases", {}).items()}},
        compiler_params=_params(("arbitrary",) * len(grid)),
    )(*args, *c_in)
    return res[:n_out], res[n_out:]


def _row(tr, w, blk=0):
    return pl.BlockSpec((tr, w), lambda i: (i, blk))


def _full(shape):
    nd = len(shape)
    return pl.BlockSpec(shape, lambda i: (0,) * nd)


def _halo_specs(s, tr, w, blk=0):
    per = tr // HALO
    last = s // HALO - 1
    return [
        pl.BlockSpec((HALO, w), lambda i: (jnp.maximum(i * per - 1, 0), blk)),
        pl.BlockSpec((tr, w), lambda i: (i, blk)),
        pl.BlockSpec((HALO, w), lambda i: (jnp.minimum((i + 1) * per, last), blk)),
    ]


def _ext(prev_ref, cur_ref, next_ref, s, tr):
    i = pl.program_id(0)
    e = jnp.concatenate([prev_ref[...].astype(F32), cur_ref[...].astype(F32), next_ref[...].astype(F32)], axis=0)
    rid = i * tr - HALO + lax.broadcasted_iota(jnp.int32, e.shape, 0)
    return jnp.where((rid >= 0) & (rid < s), e, 0.0)


def _valid_rows(shape, s, tr):
    i = pl.program_id(0)
    rid = i * tr - HALO + lax.broadcasted_iota(jnp.int32, shape, 0)
    return (rid >= 0) & (rid < s)


def _shift(e, k):
    if k == 0:
        return e
    n = e.shape[0]
    return pltpu.roll(e, (n - k) % n, 0)


def _acc_rows(ref, v):
    s = jnp.sum(v, axis=0, keepdims=True)

    @pl.when(pl.program_id(0) == 0)
    def _():
        ref[...] = s

    @pl.when(pl.program_id(0) > 0)
    def _():
        ref[...] += s


def _rowcall(body, name, s, tr, in_specs, out_specs, out_shape, args, comm=None, aliases=None):
    aliases = aliases or {}
    if comm is not None:
        single = not isinstance(out_shape, (list, tuple))
        outs, landed = _comm_call(body, name=name, grid=(s // tr,), in_specs=in_specs,
                                  out_specs=[out_specs] if single else out_specs,
                                  out_shape=[out_shape] if single else out_shape, scratch_shapes=[], args=args, comm=comm,
                                  aliases=aliases)
        return (outs[0] if single else outs), landed
    return pl.pallas_call(
        body, name=name, grid=(s // tr,), in_specs=in_specs, out_specs=out_specs, out_shape=out_shape,
        input_output_aliases=aliases, compiler_params=_params(("arbitrary",)),
    )(*args)


def _modulate(x, scale, shift, *, name, comm=None):
    s, d = x.shape
    tr = _tile_rows(s, d)

    def body(x_ref, sc_ref, sh_ref, o_ref):
        o_ref[...] = (x_ref[...] * (1.0 + sc_ref[...]) + sh_ref[...]).astype(BF16)

    return _rowcall(body, name, s, tr, [_row(tr, d), _full((1, d)), _full((1, d))], _row(tr, d),
                    jax.ShapeDtypeStruct((s, d), BF16), (x, scale, shift), comm=comm)


def _ln_stats(r):
    mu = jnp.mean(r, axis=-1, keepdims=True)
    xc = r - mu
    var = jnp.mean(xc * xc, axis=-1, keepdims=True)
    rstd = lax.rsqrt(var + LN_EPS)
    return xc * rstd, rstd


def _ln1_fwd(x, mix, gate, g, b, scale2, shift2, comm=None):
    s, d = x.shape
    tr = _tile_rows(s, d)

    def body(x_ref, m_ref, gt_ref, g_ref, b_ref, sc_ref, sh_ref, x1_ref, h2_ref):
        r = DEEPNORM_ALPHA * x_ref[...] + (1.0 + gt_ref[...]) * m_ref[...]
        xh, _ = _ln_stats(r)
        x1 = xh * g_ref[...] + b_ref[...]
        x1_ref[...] = x1
        h2_ref[...] = (x1 * (1.0 + sc_ref[...]) + sh_ref[...]).astype(BF16)

    v = _full((1, d))
    return _rowcall(body, "ln1_fwd", s, tr, [_row(tr, d), _row(tr, d), v, v, v, v, v],
                    [_row(tr, d), _row(tr, d)],
                    [jax.ShapeDtypeStruct((s, d), F32), jax.ShapeDtypeStruct((s, d), BF16)],
                    (x, mix, gate, g, b, scale2, shift2), comm=comm)


def _ln2_loss_bwd(x1, f, target, gate, g, b):
    s, d = x1.shape
    tr = _tile_rows(s, d)

    def body(x1_ref, f_ref, t_ref, gt_ref, g_ref, b_ref, df_ref, dr_ref, loss_ref, dg_ref, db_ref, dgt_ref):
        fv = f_ref[...]
        r = DEEPNORM_ALPHA * x1_ref[...] + (1.0 + gt_ref[...]) * fv
        xh, rstd = _ln_stats(r)
        y = xh * g_ref[...] + b_ref[...]
        err = y - t_ref[...]
        _acc_rows(loss_ref, 0.5 * jnp.mean(err * err, axis=-1, keepdims=True))
        dy = err * (1.0 / d)
        _acc_rows(dg_ref, dy * xh)
        _acc_rows(db_ref, dy)
        dxh = dy * g_ref[...]
        dr = rstd * (dxh - jnp.mean(dxh, axis=-1, keepdims=True) - xh * jnp.mean(dxh * xh, axis=-1, keepdims=True))
        dr_ref[...] = dr
        df_ref[...] = ((1.0 + gt_ref[...]) * dr).astype(BF16)
        _acc_rows(dgt_ref, dr * fv)

    v = _full((1, d))
    one = _full((1, 1))
    return _rowcall(body, "ln2_loss_bwd", s, tr, [_row(tr, d), _row(tr, d), _row(tr, d), v, v, v],
                    [_row(tr, d), _row(tr, d), one, v, v, v],
                    [jax.ShapeDtypeStruct((s, d), BF16), jax.ShapeDtypeStruct((s, d), F32),
                     jax.ShapeDtypeStruct((1, 1), F32)] + [jax.ShapeDtypeStruct((1, d), F32)] * 3,
                    (x1, f, target, gate, g, b))


def _ln1_bwd(dh2, dr2, x1, x, mix, scale2, gate1, g1, comm=None):
    s, d = x.shape
    tr = _tile_rows(s, d)

    def body(dh_ref, dr2_ref, x1_ref, x_ref, m_ref, sc_ref, gt_ref, g_ref,
             dm_ref, dxa_ref, dsc_ref, dsh_ref, dg_ref, db_ref, dgt_ref):
        dh = dh_ref[...]
        _acc_rows(dsc_ref, dh * x1_ref[...])
        _acc_rows(dsh_ref, dh)
        dy = dh * (1.0 + sc_ref[...]) + DEEPNORM_ALPHA * dr2_ref[...]
        mv = m_ref[...]
        r = DEEPNORM_ALPHA * x_ref[...] + (1.0 + gt_ref[...]) * mv
        xh, rstd = _ln_stats(r)
        _acc_rows(dg_ref, dy * xh)
        _acc_rows(db_ref, dy)
        dxh = dy * g_ref[...]
        dr = rstd * (dxh - jnp.mean(dxh, axis=-1, keepdims=True) - xh * jnp.mean(dxh * xh, axis=-1, keepdims=True))
        dm_ref[...] = ((1.0 + gt_ref[...]) * dr).astype(BF16)
        dxa_ref[...] = DEEPNORM_ALPHA * dr
        _acc_rows(dgt_ref, dr * mv)

    v = _full((1, d))
    return _rowcall(body, "ln1_bwd", s, tr, [_row(tr, d)] * 5 + [v, v, v],
                    [_row(tr, d), _row(tr, d), v, v, v, v, v],
                    [jax.ShapeDtypeStruct((s, d), BF16), jax.ShapeDtypeStruct((s, d), F32)]
                    + [jax.ShapeDtypeStruct((1, d), F32)] * 5,
                    (dh2, dr2, x1, x, mix, scale2, gate1, g1), comm=comm)


def _dx_final(dxa, dh1, x, scale1):
    s, d = x.shape
    tr = _tile_rows(s, d)

    def body(a_ref, dh_ref, x_ref, sc_ref, o_ref, dsc_ref, dsh_ref):
        dh = dh_ref[...]
        o_ref[...] = a_ref[...] + dh * (1.0 + sc_ref[...])
        _acc_rows(dsc_ref, dh * x_ref[...])
        _acc_rows(dsh_ref, dh)

    v = _full((1, d))
    return _rowcall(body, "dx_final", s, tr, [_row(tr, d)] * 3 + [v], [_row(tr, d), v, v],
                    [jax.ShapeDtypeStruct((s, d), F32)] + [jax.ShapeDtypeStruct((1, d), F32)] * 2,
                    (dxa, dh1, x, scale1))


P_X, P_BC, P_Z, P_H, P_B, P_C = range(6)
HBM_REF = pl.BlockSpec(memory_space=pl.ANY)


def _conv_silu_fwd(proj, conv_w, conv_b, s, u, comm=None):
    tr = _tile_rows(s, u)
    w = 2 * u
    half = SSM_CONV // 2

    def body(p0, c0, n0, p1, c1, n1, w_ref, b_ref, o_ref):
        for blk, (pr, cr, nr) in enumerate(((p0, c0, n0), (p1, c1, n1))):
            e = _ext(pr, cr, nr, s, tr)
            wv = w_ref[:, blk * u:(blk + 1) * u]
            acc = jnp.zeros_like(e)
            for k in range(SSM_CONV):
                acc = acc + _shift(e, k - half) * wv[k:k + 1, :]
            pre = acc[HALO:HALO + tr] + b_ref[:, blk * u:(blk + 1) * u]
            o_ref[:, blk * u:(blk + 1) * u] = _silu(pre)

    in_specs = _halo_specs(s, tr, u, P_X) + _halo_specs(s, tr, u, P_BC) + [_full((SSM_CONV, w)), _full((1, w))]
    return _rowcall(body, "conv_silu_fwd", s, tr, in_specs, _row(tr, w), jax.ShapeDtypeStruct((s, w), F32),
                    (proj,) * 6 + (conv_w, conv_b), comm=comm)


def _conv_silu_bwd(proj, dxbc, conv_w, conv_b, dproj, s, u, comm=None):
    tr = _tile_rows(s, u)
    w = 2 * u
    half = SSM_CONV // 2

    def body(p0, c0, n0, p1, c1, n1, dp0, dc0, dn0, dp1, dc1, dn1, w_ref, b_ref, dproj_in, du_ref, dw_ref, db_ref):
        for blk, (ur, dr) in enumerate((((p0, c0, n0), (dp0, dc0, dn0)), ((p1, c1, n1), (dp1, dc1, dn1)))):
            e = _ext(*ur, s, tr)
            de = _ext(*dr, s, tr)
            wv = w_ref[:, blk * u:(blk + 1) * u]
            acc = jnp.zeros_like(e)
            for k in range(SSM_CONV):
                acc = acc + _shift(e, k - half) * wv[k:k + 1, :]
            pre = acc + b_ref[:, blk * u:(blk + 1) * u]
            dpre = jnp.where(_valid_rows(e.shape, s, tr), de * _dsilu(pre), 0.0)
            du = jnp.zeros_like(e)
            rows = []
            for k in range(SSM_CONV):
                du = du + _shift(dpre, half - k) * wv[k:k + 1, :]
                rows.append(jnp.sum((_shift(e, k - half) * dpre)[HALO:HALO + tr], axis=0, keepdims=True))
            du_ref[:, blk * u:(blk + 1) * u] = du[HALO:HALO + tr].astype(BF16)
            dwv = jnp.concatenate(rows + [jnp.zeros((SUBLANES - SSM_CONV, u), F32)], axis=0)
            dbv = jnp.sum(dpre[HALO:HALO + tr], axis=0, keepdims=True)
            first = pl.program_id(0) == 0

            @pl.when(first)
            def _():
                dw_ref[:, blk * u:(blk + 1) * u] = dwv
                db_ref[:, blk * u:(blk + 1) * u] = dbv

            @pl.when(jnp.logical_not(first))
            def _():
                dw_ref[:, blk * u:(blk + 1) * u] += dwv
                db_ref[:, blk * u:(blk + 1) * u] += dbv

    in_specs = (_halo_specs(s, tr, u, P_X) + _halo_specs(s, tr, u, P_BC) + _halo_specs(s, tr, u, 0)
                + _halo_specs(s, tr, u, 1) + [_full((SSM_CONV, w)), _full((1, w)), HBM_REF])
    return _rowcall(body, "conv_silu_bwd", s, tr, in_specs,
                    [_row(tr, w), _full((SUBLANES, w)), _full((1, w))],
                    [jax.ShapeDtypeStruct(dproj.shape, BF16), jax.ShapeDtypeStruct((SUBLANES, w), F32),
                     jax.ShapeDtypeStruct((1, w), F32)],
                    (proj,) * 6 + (dxbc,) * 6 + (conv_w, conv_b, dproj), comm=comm, aliases={14: 0})


def _expanders(h):
    col64 = jnp.arange(2 * h * HEAD_DIM) // HEAD_DIM
    col128 = jnp.arange(2 * h * LANES) // LANES
    row = jnp.arange(LANES)[:, None]
    return (row == col64[None, :]).astype(BF16), (row == col128[None, :]).astype(BF16)


def _dt_prep(proj, bias_row, a_row, s, u, h, comm=None):
    q = CHUNK
    e64, e128 = _expanders(h)
    ds = h * HEAD_DIM
    dtblk = (6 * u) // LANES

    def body(raw_ref, b_ref, a_ref, e64_ref, e128_ref, dt_ref, cum_ref, dte_ref, cume_ref):
        lane = lax.broadcasted_iota(jnp.int32, (q, LANES), 1)
        dt = jnp.where(lane < 2 * h, _softplus(raw_ref[...] + b_ref[...]), 0.0)
        da = dt * a_ref[...]
        ii = lax.broadcasted_iota(jnp.int32, (q, q), 0)
        kk = lax.broadcasted_iota(jnp.int32, (q, q), 1)
        lower = (kk <= ii).astype(F32).astype(BF16)
        upper = (kk >= ii).astype(F32).astype(BF16)
        cum = jnp.where(lane < h, _dot3_l(lower, da), _dot3_l(upper, da))
        dt_ref[...] = dt
        cum_ref[...] = cum
        dte = _dot3_r(dt, e64_ref[...])
        cume = _dot3_r(cum, e128_ref[...])
        dte_ref[0] = dte[:, :ds]
        dte_ref[1] = dte[:, ds:]
        cume_ref[0] = cume[:, :h * LANES]
        cume_ref[1] = cume[:, h * LANES:]

    in_specs = [pl.BlockSpec((q, LANES), lambda i: (i, dtblk)), _full((1, LANES)), _full((1, LANES)),
                _full(e64.shape), _full(e128.shape)]
    out_specs = [_row(q, LANES), _row(q, LANES),
                 pl.BlockSpec((2, q, ds), lambda i: (0, i, 0)), pl.BlockSpec((2, q, h * LANES), lambda i: (0, i, 0))]
    out_shape = [jax.ShapeDtypeStruct((s, LANES), F32), jax.ShapeDtypeStruct((s, LANES), F32),
                 jax.ShapeDtypeStruct((2, s, ds), F32), jax.ShapeDtypeStruct((2, s, h * LANES), F32)]
    return _rowcall(body, "dt_prep", s, q, in_specs, out_specs, out_shape, (proj, bias_row, a_row, e64, e128), comm=comm)


def _ssd_specs(s, h, g):
    q = CHUNK
    nc = s // q
    ds = h * HEAD_DIM
    nb = g * STATE
    return q, nc, ds, nb


def _ssd_fwd(xbc, dt_e, cum_e, cum_t, s, h, g, comm=None):
    q, nc, ds, nb = _ssd_specs(s, h, g)
    npair = h // 2

    def cidx(d, i):
        return jnp.where(d == 0, i, nc - 1 - i)

    def body(x_ref, b_ref, c_ref, dt_ref, cum_ref, cumt_ref, y_ref, sp_ref, st):
        d = pl.program_id(0)
        i = pl.program_id(1)

        @pl.when(i == 0)
        def _():
            st[...] = jnp.zeros_like(st)

        rev = d == 1
        ii = lax.broadcasted_iota(jnp.int32, (q, q), 0)
        jj = lax.broadcasted_iota(jnp.int32, (q, q), 1)
        sgn = jnp.where(rev, -1, 1)
        mask = (jj - ii) * sgn <= 0
        left = lax.broadcasted_iota(jnp.int32, (q, LANES), 1) < HEAD_DIM

        def group(gi, carry):
            goff = pl.multiple_of(gi * STATE, STATE)
            cg = c_ref[:, pl.ds(goff, STATE)].astype(BF16)
            bg = b_ref[:, pl.ds(goff, STATE)].astype(BF16)
            gm = _dot_nt(cg, bg)
            for p in range(HEADS_PER_GROUP // 2):
                pr = gi * (HEADS_PER_GROUP // 2) + p
                off = pl.multiple_of(pr * LANES, LANES)
                xd = x_ref[:, pl.ds(off, LANES)] * dt_ref[:, pl.ds(off, LANES)]
                ms = []
                cols = []
                for hl in range(2):
                    hh = 2 * pr + hl
                    col = cum_ref[:, pl.ds(pl.multiple_of(hh * LANES, LANES), LANES)]
                    row = cumt_ref[pl.ds(hh, 1), :]
                    lm = jnp.where(mask, jnp.exp(jnp.minimum(col - row, 0.0)), 0.0)
                    ms.append((gm * lm).astype(BF16))
                    cols.append(col)
                y = _dot(ms[0], jnp.where(left, xd, 0.0).astype(BF16)) + _dot(ms[1], jnp.where(left, 0.0, xd).astype(BF16))
                ce = jnp.where(left, cols[0], cols[1])
                sprev = st[pr]
                sp_ref[pr] = sprev
                y = y + jnp.exp(ce) * _dot(cg, sprev.astype(BF16))
                y_ref[:, pl.ds(off, LANES)] = y
                tot = jnp.where(rev, ce[0:1, :], ce[q - 1:q, :])
                v = (xd * jnp.exp(tot - ce)).astype(BF16)
                st[pr] = jnp.exp(tot) * sprev + _dot_tn(bg, v)
            return carry

        lax.fori_loop(0, g, group, 0)

    in_specs = [
        pl.BlockSpec((q, ds), lambda d, i: (cidx(d, i), 0)),
        pl.BlockSpec((q, nb), lambda d, i: (cidx(d, i), ds // nb)),
        pl.BlockSpec((q, nb), lambda d, i: (cidx(d, i), ds // nb + 1)),
        pl.BlockSpec((None, q, ds), lambda d, i: (d, cidx(d, i), 0)),
        pl.BlockSpec((None, q, h * LANES), lambda d, i: (d, cidx(d, i), 0)),
        pl.BlockSpec((None, h, q), lambda d, i: (d, 0, cidx(d, i))),
    ]
    out_specs = [
        pl.BlockSpec((None, q, ds), lambda d, i: (d, cidx(d, i), 0)),
        pl.BlockSpec((None, None, npair, STATE, LANES), lambda d, i: (d, cidx(d, i), 0, 0, 0)),
    ]
    out_shape = [jax.ShapeDtypeStruct((2, s, ds), F32), jax.ShapeDtypeStruct((2, nc, npair, STATE, LANES), F32)]
    if comm is not None:
        return _comm_call(body, name="ssd_fwd", grid=(2, nc), in_specs=in_specs, out_specs=out_specs, out_shape=out_shape,
                          scratch_shapes=[pltpu.VMEM((npair, STATE, LANES), F32)],
                          args=(xbc, xbc, xbc, dt_e, cum_e, cum_t), comm=comm)
    return pl.pallas_call(
        body, name="ssd_fwd", grid=(2, nc), in_specs=in_specs, out_specs=out_specs, out_shape=out_shape,
        scratch_shapes=[pltpu.VMEM((npair, STATE, LANES), F32)],
        compiler_params=_params(("arbitrary", "arbitrary")),
    )(xbc, xbc, xbc, dt_e, cum_e, cum_t), ()


def _ssd_bwd(xbc, dt_e, cum_e, cum_t, dt_t, a_col, dy, sp, s, h, g, comm=None):
    q, nc, ds, nb = _ssd_specs(s, h, g)
    npair = h // 2

    def cidx(d, i):
        return jnp.where(d == 0, nc - 1 - i, i)

    def body(x_ref, b_ref, c_ref, dt_ref, cum_ref, cumt_ref, dtt_ref, a_ref, dy_ref, sp_ref,
             dx_ref, db_ref, dc_ref, ddt_ref, da_ref, dst, rowp):
        d = pl.program_id(0)
        i = pl.program_id(1)

        @pl.when(i == 0)
        def _():
            dst[...] = jnp.zeros_like(dst)
            da_ref[...] = jnp.zeros_like(da_ref)

        rev = d == 1
        ii = lax.broadcasted_iota(jnp.int32, (q, q), 0)
        jj = lax.broadcasted_iota(jnp.int32, (q, q), 1)
        sgn = jnp.where(rev, -1, 1)
        mask = (jj - ii) * sgn <= 0
        lane = lax.broadcasted_iota(jnp.int32, (q, LANES), 1)
        left = lane < HEAD_DIM
        rowp[...] = jnp.zeros_like(rowp)

        def group(gi, carry):
            acc_dcum, acc_tot, acc_dxx = carry
            goff = pl.multiple_of(gi * STATE, STATE)
            cg = c_ref[:, pl.ds(goff, STATE)].astype(BF16)
            bg = b_ref[:, pl.ds(goff, STATE)].astype(BF16)
            gm = _dot_nt(cg, bg)
            dgm = jnp.zeros((q, q), F32)
            dcg = jnp.zeros((q, STATE), F32)
            dbg = jnp.zeros((q, STATE), F32)
            for p in range(HEADS_PER_GROUP // 2):
                pr = gi * (HEADS_PER_GROUP // 2) + p
                off = pl.multiple_of(pr * LANES, LANES)
                xv = x_ref[:, pl.ds(off, LANES)]
                dte = dt_ref[:, pl.ds(off, LANES)]
                xd = xv * dte
                xdb = xd.astype(BF16)
                dyv = dy_ref[:, pl.ds(off, LANES)]
                sprev = sp_ref[pr]
                sprevb = sprev.astype(BF16)
                dsn = dst[pr]
                dsnb = dsn.astype(BF16)
                cols = [cum_ref[:, pl.ds(pl.multiple_of((2 * pr + hl) * LANES, LANES), LANES)] for hl in range(2)]
                ce = jnp.where(left, cols[0], cols[1])
                tot = jnp.where(rev, ce[0:1, :], ce[q - 1:q, :])
                et = jnp.exp(tot)
                r = jnp.exp(tot - ce)
                e = jnp.exp(ce)
                yoff = e * _dot(cg, sprevb)
                dz = (e * dyv).astype(BF16)
                dcg = dcg + _dot_nt(dz, sprevb)
                dsprev = _dot_tn(cg, dz) + et * dsn
                f1 = dyv * yoff
                v = (xd * r).astype(BF16)
                dbg = dbg + _dot_nt(v, dsnb)
                dv = _dot(bg, dsnb)
                dxd = dv * r
                tt = dv * xd * r
                wt = dsn * sprev * et
                for hl in range(2):
                    hh = 2 * pr + hl
                    hm = left if hl == 0 else jnp.logical_not(left)
                    row = cumt_ref[pl.ds(hh, 1), :]
                    lm = jnp.where(mask, jnp.exp(jnp.minimum(cols[hl] - row, 0.0)), 0.0)
                    mf = gm * lm
                    dym = jnp.where(hm, dyv, 0.0).astype(BF16)
                    dm = _dot_nt(dym, xdb)
                    dxd = dxd + _dot_tn(mf.astype(BF16), dym)
                    dgm = dgm + dm * lm
                    em = dm * mf
                    rowp[pl.ds(hh, 1), :] = rowp[pl.ds(hh, 1), :] - jnp.sum(em, axis=0, keepdims=True)
                    colq = (jnp.sum(em, axis=1, keepdims=True)
                            + jnp.sum(jnp.where(hm, f1 - tt, 0.0), axis=1, keepdims=True))
                    acc_dcum = jnp.where(lane == hh, colq, acc_dcum)
                    totq = jnp.sum(jnp.sum(jnp.where(hm, tt + wt, 0.0), axis=1, keepdims=True), axis=0, keepdims=True)
                    acc_tot = jnp.where(lane == hh, totq, acc_tot)
                dxx = dxd * xv
                for hl in range(2):
                    hh = 2 * pr + hl
                    hm = left if hl == 0 else jnp.logical_not(left)
                    acc_dxx = jnp.where(lane == hh, jnp.sum(jnp.where(hm, dxx, 0.0), axis=1, keepdims=True), acc_dxx)
                dx_ref[:, pl.ds(off, LANES)] = dxd * dte
                dst[pr] = dsprev
            dgb = dgm.astype(BF16)
            dc_ref[:, pl.ds(goff, STATE)] = dcg + _dot(dgb, bg)
            db_ref[:, pl.ds(goff, STATE)] = dbg + _dot_tn(dgb, cg)
            return acc_dcum, acc_tot, acc_dxx

        zero = jnp.zeros((q, LANES), F32)
        acc_dcum, acc_tot, acc_dxx = lax.fori_loop(0, g, group, (zero, zero, zero))
        dcum_t = rowp[...] + jnp.transpose(acc_dcum)[:h]
        rmat = ((ii - jj) * sgn >= 0).astype(F32).astype(BF16)
        da_t = _dot3_r(dcum_t, rmat) + jnp.transpose(acc_tot)[:h]
        ddt_ref[...] = da_t * a_ref[...] + jnp.transpose(acc_dxx)[:h]
        da_ref[...] += da_t * dtt_ref[...]

    in_specs = [
        pl.BlockSpec((q, ds), lambda d, i: (cidx(d, i), 0)),
        pl.BlockSpec((q, nb), lambda d, i: (cidx(d, i), ds // nb)),
        pl.BlockSpec((q, nb), lambda d, i: (cidx(d, i), ds // nb + 1)),
        pl.BlockSpec((None, q, ds), lambda d, i: (d, cidx(d, i), 0)),
        pl.BlockSpec((None, q, h * LANES), lambda d, i: (d, cidx(d, i), 0)),
        pl.BlockSpec((None, h, q), lambda d, i: (d, 0, cidx(d, i))),
        pl.BlockSpec((None, h, q), lambda d, i: (d, 0, cidx(d, i))),
        pl.BlockSpec((None, h, LANES), lambda d, i: (d, 0, 0)),
        pl.BlockSpec((q, ds), lambda d, i: (cidx(d, i), 0)),
        pl.BlockSpec((None, None, npair, STATE, LANES), lambda d, i: (d, cidx(d, i), 0, 0, 0)),
    ]
    out_specs = [
        pl.BlockSpec((None, q, ds), lambda d, i: (d, cidx(d, i), 0)),
        pl.BlockSpec((None, q, nb), lambda d, i: (d, cidx(d, i), 0)),
        pl.BlockSpec((None, q, nb), lambda d, i: (d, cidx(d, i), 0)),
        pl.BlockSpec((None, h, q), lambda d, i: (d, 0, cidx(d, i))),
        pl.BlockSpec((None, h, LANES), lambda d, i: (d, 0, 0)),
    ]
    out_shape = [jax.ShapeDtypeStruct((2, s, ds), F32), jax.ShapeDtypeStruct((2, s, nb), F32),
                 jax.ShapeDtypeStruct((2, s, nb), F32), jax.ShapeDtypeStruct((2, h, s), F32),
                 jax.ShapeDtypeStruct((2, h, LANES), F32)]
    scratch = [pltpu.VMEM((npair, STATE, LANES), F32), pltpu.VMEM((h, q), F32)]
    args = (xbc, xbc, xbc, dt_e, cum_e, cum_t, dt_t, a_col, dy, sp)
    if comm is not None:
        return _comm_call(body, name="ssd_bwd", grid=(2, nc), in_specs=in_specs, out_specs=out_specs, out_shape=out_shape,
                          scratch_shapes=scratch, args=args, comm=comm)
    return pl.pallas_call(
        body, name="ssd_bwd", grid=(2, nc), in_specs=in_specs, out_specs=out_specs, out_shape=out_shape,
        scratch_shapes=scratch, compiler_params=_params(("arbitrary", "arbitrary")),
    )(*args), ()


def _dt_bwd(ddt, proj, bias_row, dproj, s, u, h):
    tr = _tile_rows(s, 4 * LANES)
    dtblk = (6 * u) // LANES
    tail = dproj.shape[1] - 6 * u
    assert (6 * u) % tail == 0

    def body(d_ref, raw_ref, b_ref, dproj_in, o_ref, db_ref):
        lane = lax.broadcasted_iota(jnp.int32, (tr, LANES), 1)
        v = jnp.where(lane < 2 * h, d_ref[...] * _sigmoid(raw_ref[...] + b_ref[...]), 0.0)
        o_ref[:, :LANES] = v.astype(BF16)
        o_ref[:, LANES:] = jnp.zeros((tr, tail - LANES), BF16)
        _acc_rows(db_ref, v)

    return _rowcall(body, "dt_bwd", s, tr, [_row(tr, LANES), _row(tr, LANES, dtblk), _full((1, LANES)), HBM_REF],
                    [_row(tr, tail, (6 * u) // tail), _full((1, LANES))],
                    [jax.ShapeDtypeStruct(dproj.shape, BF16), jax.ShapeDtypeStruct((1, LANES), F32)],
                    (ddt, proj, bias_row, dproj), aliases={3: 0})


def _group_rms(v, gw):
    outs, facs = [], []
    for k in range(v.shape[1] // gw):
        blk = v[:, k * gw:(k + 1) * gw]
        f = lax.rsqrt(jnp.mean(blk * blk, axis=-1, keepdims=True) + RMS_EPS)
        outs.append(blk * f)
        facs.append(jnp.broadcast_to(f, blk.shape))
    return jnp.concatenate(outs, axis=1), jnp.concatenate(facs, axis=1)


def _group_rms_bwd(dn, n, fac, gw):
    outs = []
    for k in range(n.shape[1] // gw):
        sl = slice(k * gw, (k + 1) * gw)
        outs.append(fac[:, sl] * (dn[:, sl] - n[:, sl] * jnp.mean(dn[:, sl] * n[:, sl], axis=-1, keepdims=True)))
    return jnp.concatenate(outs, axis=1)


def _gate_norm_fwd(y2, xbc, proj, d_e, norm_w, s, u, g):
    tr = _tile_rows(s, u)
    gw = u // g

    def body(y_ref, x_ref, z_ref, d_ref, w_ref, o_ref):
        ys = y_ref[0] + y_ref[1] + d_ref[...] * x_ref[...]
        n, _ = _group_rms(ys * _silu(z_ref[...]), gw)
        o_ref[...] = (n * w_ref[...]).astype(BF16)

    return _rowcall(body, "gate_norm_fwd", s, tr,
                    [pl.BlockSpec((2, tr, u), lambda i: (0, i, 0)), _row(tr, u), _row(tr, u, P_Z), _full((1, u)), _full((1, u))],
                    _row(tr, u), jax.ShapeDtypeStruct((s, u), BF16), (y2, xbc, proj, d_e, norm_w))


def _gate_norm_bwd(dymix, y2, xbc, proj, d_e, norm_w, pw, s, u, g, comm=None):
    tr = _tile_rows(s, u)
    gw = u // g

    def body(dy_ref, y_ref, x_ref, z_ref, d_ref, w_ref, dys_ref, dz_ref, dxs_ref, dw_ref, dd_ref):
        xv = x_ref[...]
        zv = z_ref[...]
        ys = y_ref[0] + y_ref[1] + d_ref[...] * xv
        sz = _silu(zv)
        n, fac = _group_rms(ys * sz, gw)
        dout = dy_ref[...]
        _acc_rows(dw_ref, dout * n)
        dyg = _group_rms_bwd(dout * w_ref[...], n, fac, gw)
        dys = dyg * sz
        dys_ref[...] = dys
        dz_ref[...] = (dyg * ys * _dsilu(zv)).astype(BF16)
        dxs_ref[...] = dys * d_ref[...]
        _acc_rows(dd_ref, dys * xv)

    v = _full((1, u))
    return _rowcall(body, "gate_norm_bwd", s, tr,
                    [_row(tr, u), pl.BlockSpec((2, tr, u), lambda i: (0, i, 0)), _row(tr, u), _row(tr, u, P_Z), v, v],
                    [_row(tr, u), _row(tr, u, P_Z), _row(tr, u), v, v],
                    [jax.ShapeDtypeStruct((s, u), F32), jax.ShapeDtypeStruct((s, pw), BF16),
                     jax.ShapeDtypeStruct((s, u), F32), jax.ShapeDtypeStruct((1, u), F32), jax.ShapeDtypeStruct((1, u), F32)],
                    (dymix, y2, xbc, proj, d_e, norm_w), comm=comm)


def _shortconv_fwd(proj, conv_w, norm_w, s, u):
    tr = _tile_rows(s, u)
    half = SC_CONV // 2

    def body(hp, hc, hn, b_ref, cp, cc, cn, cw_ref, w_ref, o_ref):
        t = _ext(hp, hc, hn, s, tr) * _ext(cp, cc, cn, s, tr)
        wv = cw_ref[...]
        acc = jnp.zeros_like(t)
        for k in range(SC_CONV):
            acc = acc + _shift(t, k - half) * wv[k:k + 1, :]
        n, _ = _group_rms(b_ref[...] * acc[HALO:HALO + tr], SC_GROUP_WIDTH)
        o_ref[...] = (n * w_ref[...]).astype(BF16)

    in_specs = (_halo_specs(s, tr, u, P_H) + [_row(tr, u, P_B)] + _halo_specs(s, tr, u, P_C)
                + [_full((SC_CONV, u)), _full((1, u))])
    return _rowcall(body, "shortconv_fwd", s, tr, in_specs, _row(tr, u), jax.ShapeDtypeStruct((s, u), BF16),
                    (proj,) * 7 + (conv_w, norm_w))


def _shortconv_bwd(dymix, proj, conv_w, norm_w, dproj, s, u):
    tr = _tile_rows(s, u)
    half = SC_CONV // 2

    def body(dp, dc_, dn, hp, hc, hn, bp, bc, bn, cp, cc, cn, cw_ref, w_ref, dproj_in, o_ref, dcw_ref, dw_ref):
        dout = _ext(dp, dc_, dn, s, tr)
        hv = _ext(hp, hc, hn, s, tr)
        bv = _ext(bp, bc, bn, s, tr)
        cv = _ext(cp, cc, cn, s, tr)
        t = hv * cv
        wv = cw_ref[...]
        acc = jnp.zeros_like(t)
        for k in range(SC_CONV):
            acc = acc + _shift(t, k - half) * wv[k:k + 1, :]
        n, fac = _group_rms(bv * acc, SC_GROUP_WIDTH)
        cur = slice(HALO, HALO + tr)
        _acc_rows(dw_ref, (dout * n)[cur])
        dyv = _group_rms_bwd(dout * w_ref[...], n, fac, SC_GROUP_WIDTH)
        o_ref[:, u:2 * u] = (dyv * acc)[cur].astype(BF16)
        dv = dyv * bv
        dt = jnp.zeros_like(t)
        rows = []
        for k in range(SC_CONV):
            dt = dt + _shift(dv, half - k) * wv[k:k + 1, :]
            rows.append(jnp.sum((_shift(t, k - half) * dv)[cur], axis=0, keepdims=True))
        o_ref[:, :u] = (dt * cv)[cur].astype(BF16)
        o_ref[:, 2 * u:] = (dt * hv)[cur].astype(BF16)
        dwv = jnp.concatenate(rows + [jnp.zeros((SUBLANES - SC_CONV, u), F32)], axis=0)
        first = pl.program_id(0) == 0

        @pl.when(first)
        def _():
            dcw_ref[...] = dwv

        @pl.when(jnp.logical_not(first))
        def _():
            dcw_ref[...] += dwv

    in_specs = (_halo_specs(s, tr, u, 1) + _halo_specs(s, tr, u, P_H) + _halo_specs(s, tr, u, P_B)
                + _halo_specs(s, tr, u, P_C) + [_full((SC_CONV, u)), _full((1, u)), HBM_REF])
    return _rowcall(body, "shortconv_bwd", s, tr, in_specs,
                    [_row(tr, 3 * u, 1), _full((SUBLANES, u)), _full((1, u))],
                    [jax.ShapeDtypeStruct(dproj.shape, BF16), jax.ShapeDtypeStruct((SUBLANES, u), F32),
                     jax.ShapeDtypeStruct((1, u), F32)],
                    (dymix,) * 3 + (proj,) * 9 + (conv_w, norm_w, dproj), aliases={14: 0})


def _adam_math(w, g, m, v):
    m2 = ADAM_B1 * m + (1.0 - ADAM_B1) * g
    v2 = ADAM_B2 * v + (1.0 - ADAM_B2) * (g * g)
    m_hat = m2 / (1.0 - ADAM_B1 ** ADAM_STEP)
    v_hat = v2 / (1.0 - ADAM_B2 ** ADAM_STEP)
    delta = -ADAM_LR * (m_hat / (jnp.sqrt(v_hat) + ADAM_EPS) + ADAM_WD * w)
    return delta, m2, v2


def _adam_rows(r, c):
    return _pick(r, max(SUBLANES, (1 << 20) // (4 * c)), SUBLANES)


def _adam(w, g, m, v, *, name, comm=None):
    r, c = w.shape
    tr = _adam_rows(r, c)

    def body(w_ref, g_ref, m_ref, v_ref, d_ref, m2_ref, v2_ref):
        d_ref[...], m2_ref[...], v2_ref[...] = _adam_math(w_ref[...], g_ref[...], m_ref[...], v_ref[...])

    return _rowcall(body, name, r, tr, [_row(tr, c)] * 4, [_row(tr, c)] * 3,
                    [jax.ShapeDtypeStruct((r, c), F32)] * 3, (w, g, m, v), comm=comm)


def _adam_outer(w, a_t, bmat, m, v, *, name, comm=None):
    r, c = w.shape
    tr = _adam_rows(r, c)
    kk = a_t.shape[1]

    def body(w_ref, a_ref, b_ref, m_ref, v_ref, g_ref, d_ref, m2_ref, v2_ref):
        g = _dot(a_ref[...].astype(BF16), b_ref[...].astype(BF16))
        g_ref[...] = g
        d_ref[...], m2_ref[...], v2_ref[...] = _adam_math(w_ref[...], g, m_ref[...], v_ref[...])

    return _rowcall(body, name, r, tr, [_row(tr, c), _row(tr, kk), _full((kk, c)), _row(tr, c), _row(tr, c)],
                    [_row(tr, c)] * 4, [jax.ShapeDtypeStruct((r, c), F32)] * 4, (w, a_t, bmat, m, v), comm=comm)


ANY = pl.BlockSpec(memory_space=pl.ANY)
VMEM_WHOLE = pl.BlockSpec(memory_space=pltpu.VMEM)


def _place():
    x, y, c = lax.axis_index("x"), lax.axis_index("y"), lax.axis_index("c")
    return x, y, c


DMA_CHUNKS = 8


def _n_chunks(rows):
    n = DMA_CHUNKS
    while n > 1 and rows % (16 * n):
        n //= 2
    return n


def _allgather_small(v, *, name):
    m_per, n = v.shape

    def body(x_ref, out_ref, send_sems, recv_sems, local_sem):
        x, y, c = _place()
        me, sibling = (x, y, c), (x, y, 1 - c)
        chips = [(1 - x, y), (x, 1 - y), (1 - x, 1 - y)]

        def rows(px, py, pc):
            return out_ref.at[pl.ds((4 * px + 2 * py + pc) * m_per, m_per), :]

        def copy(k, block, to, src=None):
            return pltpu.make_async_remote_copy(
                src_ref=rows(*block) if src is None else src, dst_ref=rows(*block),
                send_sem=send_sems.at[k], recv_sem=recv_sems.at[k], device_id=to, device_id_type=MESH)

        mine = pltpu.make_async_copy(x_ref, rows(*me), local_sem)
        mine.start()
        first = [copy(0, me, sibling, src=x_ref)]
        first += [copy(1 + j, me, (*chip, c), src=x_ref) for j, chip in enumerate(chips)]
        for cp in first:
            cp.start()
        passed = [copy(4 + j, (*chip, c), sibling) for j, chip in enumerate(chips)]
        for j, chip in enumerate(chips):
            copy(1 + j, (*chip, c), me).wait_recv()
            passed[j].start()
        copy(0, sibling, me).wait_recv()
        for j, chip in enumerate(chips):
            copy(4 + j, (*chip, 1 - c), me).wait_recv()
        for cp in first + passed:
            cp.wait_send()
        mine.wait()

    return pl.pallas_call(
        body, name=name, out_shape=jax.ShapeDtypeStruct((N_DEV * m_per, n), v.dtype),
        in_specs=[VMEM_WHOLE], out_specs=VMEM_WHOLE,
        scratch_shapes=[pltpu.SemaphoreType.DMA((7,)), pltpu.SemaphoreType.DMA((7,)), pltpu.SemaphoreType.DMA],
        compiler_params=pltpu.CompilerParams(vmem_limit_bytes=VMEM_LIMIT),
    )(v)


def _chip_id():
    return 2 * lax.axis_index("x") + lax.axis_index("y")


def _core_id():
    return lax.axis_index("c")


def _cast_into_block(wl, *, name, comm=None):
    r, c_ = wl.shape
    tr = _tile_rows(r, c_, 16)

    def body(w_ref, o_ref):
        o_ref[...] = w_ref[...].astype(BF16)

    in_spec = pl.BlockSpec((tr, c_), lambda i: (i, 0))
    out_spec = pl.BlockSpec((None, tr, c_), lambda i: (_chip_id(), i, 0))
    out_shape = jax.ShapeDtypeStruct((N_CHIPS, r, c_), BF16)
    if comm is not None:
        (out,), landed = _comm_call(body, name=name, grid=(r // tr,), in_specs=[in_spec], out_specs=[out_spec],
                                    out_shape=[out_shape], scratch_shapes=[], args=(wl,), comm=comm)
        return out, landed
    return pl.pallas_call(body, name=name, grid=(r // tr,), in_specs=[in_spec], out_specs=out_spec, out_shape=out_shape,
                          compiler_params=_params(("arbitrary",)))(wl)


def _gather_weight(buf, *, name, part=(0, 1)):
    return _run_comm(_gather_comm(buf, part), name=name)[0]


def _gather_comm(buf, part=(0, 1)):
    _, r, c_ = buf.shape
    half = r // 2
    n_all = _n_chunks(half)
    rows = half // n_all
    first_chunk = round(part[0] * n_all)
    nch = round(part[1] * n_all) - first_chunk

    def plan(out_ref):
        x, y, c = _place()
        me, sibling = (x, y, c), (x, y, 1 - c)
        chips = [(1 - x, y), (x, 1 - y), (1 - x, 1 - y)]
        return me, sibling, chips, c

    def copy(out_ref, sems, k, i, block, to):
        part = out_ref.at[2 * block[0] + block[1], pl.ds(block[2] * half + (first_chunk + i) * rows, rows), :]
        return pltpu.make_async_remote_copy(src_ref=part, dst_ref=part, send_sem=sems[0].at[k * nch + i],
                                            recv_sem=sems[1].at[k * nch + i], device_id=to, device_id_type=MESH)

    def start(cins, couts, sems):
        (out_ref,) = couts
        me, sibling, chips, c = plan(out_ref)
        for i in range(nch):
            for j, chip in enumerate(chips):
                copy(out_ref, sems, j, i, me, (*chip, c)).start()

    def finish(cins, couts, sems):
        (out_ref,) = couts
        me, sibling, chips, c = plan(out_ref)
        passed = []
        for i in range(nch):
            for j, chip in enumerate(chips):
                copy(out_ref, sems, j, i, (*chip, c), me).wait_recv()
                passed.append(copy(out_ref, sems, 3 + j, i, (*chip, c), sibling))
                passed[-1].start()
        for i in range(nch):
            for j, chip in enumerate(chips):
                copy(out_ref, sems, 3 + j, i, (*chip, 1 - c), me).wait_recv()
        for i in range(nch):
            for j, chip in enumerate(chips):
                copy(out_ref, sems, j, i, me, (*chip, c)).wait_send()
        for cp in passed:
            cp.wait_send()

    return dict(inputs=[buf], out_shape=[jax.ShapeDtypeStruct(buf.shape, buf.dtype)], aliases={0: 0},
                scratch=[pltpu.SemaphoreType.DMA((6 * nch,)), pltpu.SemaphoreType.DMA((6 * nch,))],
                start=start, finish=finish)


def _merge_comms(*comms):
    inputs, outs, aliases, scratch, spans = [], [], {}, [], []
    for cm in comms:
        i0, o0, s0 = len(inputs), len(outs), len(scratch)
        inputs += cm["inputs"]
        outs += cm["out_shape"]
        scratch += cm["scratch"]
        aliases.update({i0 + k: o0 + v for k, v in cm["aliases"].items()})
        spans.append((i0, len(inputs), o0, len(outs), s0, len(scratch)))

    def run(which):
        def f(cins, couts, sems):
            for cm, (i0, i1, o0, o1, s0, s1) in zip(comms, spans):
                cm[which](cins[i0:i1], couts[o0:o1], sems[s0:s1])
        return f

    return dict(inputs=inputs, out_shape=outs, aliases=aliases, scratch=scratch, start=run("start"), finish=run("finish"))


def _pair_comm(gfull):
    nblk, r, c_ = gfull.shape
    half = r // 2
    nch = _n_chunks(half)
    rows = half // nch

    def copies(cins, couts, sems):
        g_ref, peer_ref = cins[0], couts[0]
        x, y, c = _place()
        return [pltpu.make_async_remote_copy(
            src_ref=g_ref.at[k, pl.ds((1 - c) * half + i * rows, rows), :], dst_ref=peer_ref.at[k, pl.ds(i * rows, rows), :],
            send_sem=sems[0].at[k * nch + i], recv_sem=sems[1].at[k * nch + i],
            device_id=(x, y, 1 - c), device_id_type=MESH) for i in range(nch) for k in range(nblk)]

    def start(cins, couts, sems):
        for cp in copies(cins, couts, sems):
            cp.start()

    def finish(cins, couts, sems):
        cps = copies(cins, couts, sems)
        for cp in cps:
            cp.wait_recv()
        for cp in cps:
            cp.wait_send()

    return dict(inputs=[gfull], out_shape=[jax.ShapeDtypeStruct((nblk, half, c_), gfull.dtype)], aliases={},
                scratch=[pltpu.SemaphoreType.DMA((nblk * nch,)), pltpu.SemaphoreType.DMA((nblk * nch,))],
                start=start, finish=finish)


def _swap_comm(buf):
    r, c_ = buf.shape
    half = r // 2
    nch = _n_chunks(half)
    rows = half // nch

    def copy(out_ref, sems, i, pc):
        part = out_ref.at[pl.ds(pc * half + i * rows, rows), :]
        x, y, c = _place()
        return pltpu.make_async_remote_copy(src_ref=part, dst_ref=part, send_sem=sems[0].at[i], recv_sem=sems[1].at[i],
                                            device_id=(x, y, 1 - c), device_id_type=MESH)

    def start(cins, couts, sems):
        c = _core_id()
        for i in range(nch):
            copy(couts[0], sems, i, c).start()

    def finish(cins, couts, sems):
        c = _core_id()
        for i in range(nch):
            copy(couts[0], sems, i, 1 - c).wait_recv()
        for i in range(nch):
            copy(couts[0], sems, i, c).wait_send()

    return dict(inputs=[buf], out_shape=[jax.ShapeDtypeStruct(buf.shape, buf.dtype)], aliases={0: 0},
                scratch=[pltpu.SemaphoreType.DMA((nch,)), pltpu.SemaphoreType.DMA((nch,))], start=start, finish=finish)


def _run_comm(cm, *, name):
    nci, nco = len(cm["inputs"]), len(cm["out_shape"])

    def body(*refs):
        cins, couts, sems = refs[:nci], refs[nci:nci + nco], refs[nci + nco:]
        cm["start"](cins, couts, sems)
        cm["finish"](cins, couts, sems)

    return pl.pallas_call(
        body, name=name, out_shape=list(cm["out_shape"]), in_specs=[ANY] * nci, out_specs=[ANY] * nco,
        input_output_aliases=dict(cm["aliases"]), scratch_shapes=cm["scratch"],
    )(*cm["inputs"])


def _pair_add(gfull, peer, *, name):
    nblk, r, c_ = gfull.shape
    half = r // 2
    tr = _pick(half, max(16, (1 << 20) // (2 * c_)), 16)
    per = half // tr

    def body(g_ref, p_ref, o_ref):
        o_ref[...] = (g_ref[...].astype(F32) + p_ref[...].astype(F32)).astype(BF16)

    return pl.pallas_call(
        body, name=name, grid=(nblk, per),
        in_specs=[pl.BlockSpec((None, tr, c_), lambda k, i: (k, _core_id() * per + i, 0)),
                  pl.BlockSpec((None, tr, c_), lambda k, i: (k, i, 0))],
        out_specs=pl.BlockSpec((None, tr, c_), lambda k, i: (k, i, 0)),
        out_shape=jax.ShapeDtypeStruct((nblk, half, c_), BF16),
        compiler_params=_params(("arbitrary", "arbitrary")))(gfull, peer)


def _scatter_comm(pre, part=(0, 1), recv=None):
    _, half, c_ = pre.shape
    n_all = _n_chunks(half)
    rows = half // n_all
    first_chunk = round(part[0] * n_all)
    nch = round(part[1] * n_all) - first_chunk

    def copies(cins, couts, sems):
        p_ref, r_ref = cins[0], couts[0]
        x, y, c = _place()
        out = []
        for i in range(nch):
            at = pl.ds((first_chunk + i) * rows, rows)
            for j, (tx, ty) in reversed(list(enumerate([(1 - x, y), (x, 1 - y), (1 - x, 1 - y)]))):
                out.append(pltpu.make_async_remote_copy(
                    src_ref=p_ref.at[2 * tx + ty, at, :], dst_ref=r_ref.at[j, at, :],
                    send_sem=sems[0].at[j * nch + i], recv_sem=sems[1].at[j * nch + i],
                    device_id=(tx, ty, c), device_id_type=MESH))
        return out

    def start(cins, couts, sems):
        for cp in copies(cins, couts, sems):
            cp.start()

    def finish(cins, couts, sems):
        cps = copies(cins, couts, sems)
        for cp in cps:
            cp.wait_recv()
        for cp in cps:
            cp.wait_send()

    return dict(inputs=[pre] if recv is None else [pre, recv], out_shape=[jax.ShapeDtypeStruct((3, half, c_), pre.dtype)],
                aliases={} if recv is None else {1: 0},
                scratch=[pltpu.SemaphoreType.DMA((3 * nch,)), pltpu.SemaphoreType.DMA((3 * nch,))],
                start=start, finish=finish)


def _sum_into_half(pre, recv, *, name):
    _, half, c_ = pre.shape
    n = recv.shape[0]
    tr = _pick(half, max(16, (1 << 19) // (2 * c_)), 16)
    per = half // tr

    def body(g_ref, r_ref, o_ref):
        acc = g_ref[...].astype(F32)
        for k in range(n):
            acc = acc + r_ref[k].astype(F32)
        o_ref[...] = acc

    return pl.pallas_call(
        body, name=name, grid=(per,),
        in_specs=[pl.BlockSpec((None, tr, c_), lambda i: (_chip_id(), i, 0)),
                  pl.BlockSpec((n, tr, c_), lambda i: (0, i, 0))],
        out_specs=pl.BlockSpec((tr, c_), lambda i: (_core_id() * per + i, 0)),
        out_shape=jax.ShapeDtypeStruct((2 * half, c_), F32), compiler_params=_params(("arbitrary",)))(pre, recv)


def _sum_slots(recv, *, name):
    n, r, c_ = recv.shape
    tr = _pick(r, max(16, (1 << 19) // (2 * c_)), 16)

    def body(r_ref, o_ref):
        acc = r_ref[0].astype(F32)
        for k in range(1, n):
            acc = acc + r_ref[k].astype(F32)
        o_ref[...] = acc

    return _rowcall(body, name, r, tr, [pl.BlockSpec((n, tr, c_), lambda i: (0, i, 0))], _row(tr, c_),
                    jax.ShapeDtypeStruct((r, c_), F32), (recv,))


def _prereduce(gfull, *, name):
    return _pair_add(gfull, _run_comm(_pair_comm(gfull), name=name + "_pair")[0], name=name + "_padd")


PACK_ROWS = 16


def _pack(parts):
    flat = [p.reshape(-1).astype(F32) for p in parts]
    n = sum(f.shape[0] for f in flat)
    unit = PACK_ROWS * LANES
    total = -(-n // unit) * unit
    if total > n:
        flat.append(jnp.zeros((total - n,), F32))
    where, off = [], 0
    for p in parts:
        where.append((off, p.shape))
        off += p.size
    return jnp.concatenate(flat).reshape(total // LANES, LANES), where


def _unpack(flat, where):
    v = flat.reshape(-1)
    return [v[off:off + _size(shape)].reshape(shape) for off, shape in where]


def _size(shape):
    n = 1
    for d in shape:
        n *= d
    return n


def _sample_step(x, target, mods, h1, w_in_p, bufs, sp):
    s, d = x.shape
    u = d // 2
    h = u // HEAD_DIM
    g = h // HEADS_PER_GROUP
    pw = w_in_p.shape[1]
    din = 6 * u + 2 * h
    dff_ = bufs["w_up"].shape[2] * N_CHIPS
    shift1, scale1, gate1, shift2, scale2, gate2 = mods

    a_f = -jnp.exp(sp["ssm_a_log_f"].reshape(-1))
    a_b = -jnp.exp(sp["ssm_a_log_b"].reshape(-1))
    pad_l = LANES - 2 * h
    a_row = jnp.pad(jnp.concatenate([a_f, a_b]), (0, pad_l)).reshape(1, LANES)
    bias_row = jnp.pad(jnp.concatenate([sp["ssm_dt_bias_f"].reshape(-1), sp["ssm_dt_bias_b"].reshape(-1)]),
                       (0, pad_l)).reshape(1, LANES)
    a_col = jnp.broadcast_to(jnp.stack([a_f, a_b])[:, :, None], (2, h, LANES))
    d_e = jnp.repeat(sp["ssm_d"].reshape(-1), HEAD_DIM).reshape(1, u)
    conv_w, conv_b = sp["ssm_conv_w"], sp["ssm_conv_b"].reshape(1, 2 * u)
    sc_conv_w = sp["sc_conv_w"]
    ssm_norm_w, sc_norm_w = sp["ssm_norm_w"].reshape(1, u), sp["sc_norm_w"].reshape(1, u)
    ln1_g, ln1_b = sp["ln1_g"].reshape(1, d), sp["ln1_b"].reshape(1, d)
    ln2_g, ln2_b = sp["ln2_g"].reshape(1, d), sp["ln2_b"].reshape(1, d)

    e = 1.0 / DMA_CHUNKS
    proj, (w_up_b, w_out_blk) = _matmul(
        h1, w_in_p, name="mm_proj", tn=1280,
        comm=_merge_comms(_gather_comm(bufs["w_up"], (0, 3 * e)), _gather_comm(bufs["w_out"])))
    w_out = w_out_blk.reshape(d, d)
    xbc, (w_down_b,) = _conv_silu_fwd(proj, conv_w, conv_b, s, u, comm=_gather_comm(bufs["w_down"], (0, e)))
    (dt, cum, dt_e, cum_e), (w_up_b,) = _dt_prep(proj, bias_row, a_row, s, u, h, comm=_gather_comm(w_up_b, (3 * e, 4 * e)))
    cum_t = jnp.stack([cum[:, :h].T, cum[:, h:2 * h].T])
    dt_t = jnp.stack([dt[:, :h].T, dt[:, h:2 * h].T])
    (y2, states), (w_up_b,) = _ssd_fwd(xbc, dt_e, cum_e, cum_t, s, h, g, comm=_gather_comm(w_up_b, (4 * e, 6 * e)))
    y_ssm = _gate_norm_fwd(y2, xbc, proj, d_e, ssm_norm_w, s, u, g)
    y_sc = _shortconv_fwd(proj, sc_conv_w, sc_norm_w, s, u)
    ymix = jnp.concatenate([y_ssm, y_sc], axis=1)
    mix, (w_up_b,) = _matmul(ymix, w_out, name="mm_mix", comm=_gather_comm(w_up_b, (6 * e, 7 * e)))
    (x1, h2), (w_up_blk,) = _ln1_fwd(x, mix, gate1, ln1_g, ln1_b, scale2, shift2, comm=_gather_comm(w_up_b, (7 * e, 1)))
    (up, ff), (w_down_blk,) = _matmul(h2, w_up_blk, name="mm_up", b_blocks=N_CHIPS, epilogue="relu2",
                                      comm=_gather_comm(w_down_b, (e, 1)))
    w_down = w_down_blk.reshape(dff_, d)
    f = _matmul(ff, w_down, name="mm_down")
    df, dr2, loss, dg2, db2, dgate2 = _ln2_loss_bwd(x1, f, target, gate2, ln2_g, ln2_b)

    gw_down = _matmul(ff, df, name="mm_gw_down", ta=True, out_dtype=BF16).reshape(N_CHIPS, dff_ // N_CHIPS, d)
    du, (peer,) = _matmul(df, w_down, name="mm_dff", tb=True, out_dtype=BF16, epilogue="relu2_bwd", extra=up,
                          comm=_pair_comm(gw_down))
    pre_down = _pair_add(gw_down, peer, name="rs_w_down_padd")
    gw_up, (rv_down,) = _matmul(h2, du, name="mm_gw_up", ta=True, out_dtype=BF16, out_blocks=N_CHIPS,
                                comm=_scatter_comm(pre_down, (0, 0.5)))
    dh2, (rv_down, peer) = _matmul(du, w_up_blk, name="mm_dh2", tb=True, b_blocks=N_CHIPS,
                                   comm=_merge_comms(_scatter_comm(pre_down, (0.5, 1), recv=rv_down), _pair_comm(gw_up)))
    pre_up = _pair_add(gw_up, peer, name="rs_w_up_padd")
    (dmix, dxa, dscale2, dshift2, dg1, db1, dgate1), (rv_up,) = _ln1_bwd(
        dh2, dr2, x1, x, mix, scale2, gate1, ln1_g, comm=_scatter_comm(pre_up, (0, e)))
    gw_out, (rv_up,) = _matmul(ymix, dmix, name="mm_gw_out", ta=True, out_dtype=BF16,
                               comm=_scatter_comm(pre_up, (e, 2 * e), recv=rv_up))
    gw_out = gw_out.reshape(N_CHIPS, d // N_CHIPS, d)
    dymix, (rv_up, peer) = _matmul(dmix, w_out, name="mm_dymix", tb=True,
                                   comm=_merge_comms(_scatter_comm(pre_up, (2 * e, 4 * e), recv=rv_up), _pair_comm(gw_out)))
    pre_out = _pair_add(gw_out, peer, name="rs_w_out_padd")
    dys, dproj, dxs, dnw, dd_e = _gate_norm_bwd(dymix, y2, xbc, proj, d_e, ssm_norm_w, pw, s, u, g)
    (dx2, dbb, dcc, ddt_t, da), (rv_up,) = _ssd_bwd(xbc, dt_e, cum_e, cum_t, dt_t, a_col, dys, states, s, h, g,
                                                    comm=_scatter_comm(pre_up, (4 * e, 1), recv=rv_up))
    dxbc = jnp.concatenate([dx2[0] + dx2[1] + dxs, dbb[0] + dbb[1], dcc[0] + dcc[1]], axis=1)
    (dproj, dcw, dcb), (rv_out,) = _conv_silu_bwd(proj, dxbc, conv_w, conv_b, dproj, s, u, comm=_scatter_comm(pre_out))
    ddt = jnp.pad(jnp.concatenate([ddt_t[0].T, ddt_t[1].T], axis=1), ((0, 0), (0, pad_l)))
    dproj, dbias = _dt_bwd(ddt, proj, bias_row, dproj, s, u, h)
    dproj, dscw, dscnw = _shortconv_bwd(dymix, proj, sc_conv_w, sc_norm_w, dproj, s, u)
    gp = _matmul(h1, dproj, name="mm_gw_in", ta=True, out_dtype=BF16, tn=1280)
    pre_in = _prereduce(_from_p_layout(gp, u, h, N_CHIPS), name="rs_w_in")
    dh1, (rv_in,) = _matmul(dproj, w_in_p, name="mm_dh1", tb=True, tk=2560, comm=_scatter_comm(pre_in))
    grad_x, dscale1, dshift1 = _dx_final(dxa, dh1, x, scale1)
    big = {"w_down": _sum_into_half(pre_down, rv_down, name="rs_w_down_sum"),
           "w_up": _sum_into_half(pre_up, rv_up, name="rs_w_up_sum"),
           "w_out": _sum_into_half(pre_out, rv_out, name="rs_w_out_sum"),
           "w_in": _sum_into_half(pre_in, rv_in, name="rs_w_in_sum")}

    small = {
        "dmod": jnp.concatenate([dshift1, dscale1, dgate1, dshift2, dscale2, dgate2], axis=1),
        "ssm_conv_b": dcb,
        "ssm_dt_bias_f": dbias[0, :h], "ssm_dt_bias_b": dbias[0, h:2 * h],
        "ssm_a_log_f": jnp.sum(da[0], axis=1) * a_f, "ssm_a_log_b": jnp.sum(da[1], axis=1) * a_b,
        "ssm_d": jnp.sum(dd_e.reshape(h, HEAD_DIM), axis=1),
        "ssm_norm_w": dnw, "sc_norm_w": dscnw,
        "ln1_g": dg1, "ln1_b": db1, "ln2_g": dg2, "ln2_b": db2,
        "ssm_conv_w": dcw[:SSM_CONV], "sc_conv_w": dscw[:SC_CONV],
    }
    return loss, grad_x, big, small


WEIGHTS = ['w_ada', 'b_ada', 'w_in', 'ssm_conv_w', 'ssm_conv_b', 'ssm_dt_bias_f', 'ssm_dt_bias_b', 'ssm_a_log_f',
           'ssm_a_log_b', 'ssm_d', 'ssm_norm_w', 'sc_conv_w', 'sc_norm_w', 'w_out', 'ln1_g', 'ln1_b', 'w_up', 'w_down',
           'ln2_g', 'ln2_b']
BIG = ('w_ada', 'w_in', 'w_out', 'w_up', 'w_down')
SMALL = tuple(n for n in WEIGHTS if n not in BIG)
SMALL_SHARDED = ('ssm_conv_w', 'sc_conv_w')


def _p_layout_width(u):
    return -(-(6 * u + LANES) // 512) * 512


def _p_segments(u, h):
    return [((0, u), P_Z * u), ((u, 3 * u), P_X * u), ((3 * u, 3 * u + 2 * h), 6 * u),
            ((3 * u + 2 * h, 6 * u + 2 * h), P_H * u)]


def _to_p_layout(blocks, u, h, pw):
    nblk, d, w = blocks.shape
    parts = []
    for (lo, hi), _ in sorted(_p_segments(u, h), key=lambda t: t[1]):
        for k in range(nblk):
            a, b = max(lo, k * w), min(hi, (k + 1) * w)
            if a < b:
                parts.append(blocks[k][:, a - k * w:b - k * w])
    parts.append(jnp.zeros((d, pw - nblk * w), blocks.dtype))
    return jnp.concatenate(parts, axis=1)


def _from_p_layout(gp, u, h, nblk):
    w = (6 * u + 2 * h) // nblk
    blocks = []
    for k in range(nblk):
        parts = []
        for (lo, hi), poff in _p_segments(u, h):
            a, b = max(lo, k * w), min(hi, (k + 1) * w)
            if a < b:
                parts.append(gp[:, poff + a - lo:poff + b - lo])
        blocks.append(jnp.concatenate(parts, axis=1))
    return jnp.stack(blocks)


def kernel(x, c, w_ada, b_ada, w_in, ssm_conv_w, ssm_conv_b, ssm_dt_bias_f, ssm_dt_bias_b, ssm_a_log_f, ssm_a_log_b, ssm_d, ssm_norm_w, sc_conv_w, sc_norm_w, w_out, ln1_g, ln1_b, w_up, w_down, ln2_g, ln2_b, loss_target, m_w_ada, m_b_ada, m_w_in, m_ssm_conv_w, m_ssm_conv_b, m_ssm_dt_bias_f, m_ssm_dt_bias_b, m_ssm_a_log_f, m_ssm_a_log_b, m_ssm_d, m_ssm_norm_w, m_sc_conv_w, m_sc_norm_w, m_w_out, m_ln1_g, m_ln1_b, m_w_up, m_w_down, m_ln2_g, m_ln2_b, v_w_ada, v_b_ada, v_w_in, v_ssm_conv_w, v_ssm_conv_b, v_ssm_dt_bias_f, v_ssm_dt_bias_b, v_ssm_a_log_f, v_ssm_a_log_b, v_ssm_d, v_ssm_norm_w, v_sc_conv_w, v_sc_norm_w, v_w_out, v_ln1_g, v_ln1_b, v_w_up, v_w_down, v_ln2_g, v_ln2_b):
    given = dict(locals())
    w = {n: given[n][0] for n in WEIGHTS}
    m = {n: given["m_" + n][0] for n in WEIGHTS}
    v = {n: given["v_" + n][0] for n in WEIGHTS}
    xs, tgt = x[0], loss_target[0]
    s, d = xs.shape
    u = d // 2
    h = u // HEAD_DIM
    nmod = N_MOD * d
    nmod_loc = nmod // N_CHIPS
    ax, ay, ac = lax.axis_index("x"), lax.axis_index("y"), lax.axis_index("c")
    chip = 2 * ax + ay
    me = 2 * chip + ac

    pay1, where1 = _pack([c[0], w["ssm_conv_w"], w["sc_conv_w"]])
    g1 = _allgather_small(pay1, name="ag_inputs").reshape(N_DEV, -1)
    per_dev = [_unpack(g1[k], where1) for k in range(N_DEV)]
    c_all = jnp.stack([p[0] for p in per_dev])
    ssm_conv_w_full = jnp.concatenate([per_dev[2 * k][1] for k in range(N_CHIPS)], axis=1)
    sc_conv_w_full = jnp.concatenate([per_dev[2 * k][2] for k in range(N_CHIPS)], axis=1)

    sc_all = _silu(c_all)
    sc16 = jnp.pad(sc_all, ((0, 16 - N_DEV), (0, 0)))
    b_loc = lax.dynamic_slice(w["b_ada"], (chip * nmod_loc,), (nmod_loc,))
    mod_loc = _matmul(sc16, w["w_ada"], name="mm_mod")[:N_DEV] + b_loc[None, :]
    pay2, where2 = _pack([mod_loc])
    g2 = _allgather_small(pay2, name="ag_mod").reshape(N_DEV, -1)
    mod_blocks = jnp.stack([_unpack(g2[2 * k], where2)[0] for k in range(N_CHIPS)])
    mod_mine = lax.dynamic_index_in_dim(mod_blocks, me, axis=1, keepdims=False).reshape(N_MOD, 1, d)
    mods = [mod_mine[k] for k in range(N_MOD)]

    din = w["w_in"].shape[1] * N_CHIPS

    e = 1.0 / DMA_CHUNKS
    g_in = _cast_into_block(w["w_in"], name="cast_w_in")
    bufs = {"w_out": _cast_into_block(w["w_out"], name="cast_w_out")}
    bufs["w_up"], (g_in,) = _cast_into_block(w["w_up"], name="cast_w_up", comm=_gather_comm(g_in, (0, e)))
    bufs["w_down"], (g_in,) = _cast_into_block(w["w_down"], name="cast_w_down", comm=_gather_comm(g_in, (e, 2 * e)))
    h1, (g_in,) = _modulate(xs, mods[1], mods[0], name="modulate1", comm=_gather_comm(g_in, (2 * e, 3 * e)))
    g_in = _gather_weight(g_in, name="gather_w_in", part=(3 * e, 1))
    w_in_p = _to_p_layout(g_in, u, h, _p_layout_width(u))

    sp = {n: w[n] for n in SMALL}
    sp["ssm_conv_w"], sp["sc_conv_w"] = ssm_conv_w_full, sc_conv_w_full
    loss_loc, grad_x, big, small = _sample_step(xs, tgt, mods, h1, w_in_p, bufs, sp)

    small_names = ["dmod"] + [n for n in SMALL if n != "b_ada"]
    pay3, where3 = _pack([loss_loc] + [small[n] for n in small_names])
    g3 = _allgather_small(pay3, name="ag_small_grads")
    tot = _unpack(_sum_slots(g3.reshape(N_DEV, -1, LANES), name="sum_small_grads"), where3)
    loss = tot[0].reshape(())
    gsum = dict(zip(small_names, tot[1:]))
    dmod_all = jnp.stack([_unpack(g3.reshape(N_DEV, -1)[k], where3)[1].reshape(-1) for k in range(N_DEV)])

    grads = {}
    grads["b_ada"] = gsum["dmod"].reshape(-1)
    for n in SMALL:
        if n in SMALL_SHARDED:
            loc = w[n].shape[1]
            grads[n] = lax.dynamic_slice_in_dim(gsum[n], chip * loc, loc, axis=1)
        elif n != "b_ada":
            grads[n] = gsum[n].reshape(w[n].shape)

    delta, new_m, new_v = {}, {}, {}
    dm_loc = lax.dynamic_slice_in_dim(dmod_all, chip * nmod_loc, nmod_loc, axis=1)
    (grads["w_ada"], delta["w_ada"], new_m["w_ada"], new_v["w_ada"]), (grads["w_down"], grads["w_up"]) = _adam_outer(
        w["w_ada"], sc16.T, jnp.pad(dm_loc, ((0, 16 - N_DEV), (0, 0))), m["w_ada"], v["w_ada"], name="adam_w_ada",
        comm=_merge_comms(_swap_comm(big["w_down"]), _swap_comm(big["w_up"])))
    (delta["w_down"], new_m["w_down"], new_v["w_down"]), (grads["w_out"], grads["w_in"]) = _adam(
        w["w_down"], grads["w_down"], m["w_down"], v["w_down"], name="adam_w_down",
        comm=_merge_comms(_swap_comm(big["w_out"]), _swap_comm(big["w_in"])))
    for n in ("w_up", "w_out", "w_in"):
        delta[n], new_m[n], new_v[n] = _adam(w[n], grads[n], m[n], v[n], name="adam_" + n)
    pw_, where_s = _pack([w[n] for n in SMALL])
    pg_, _ = _pack([grads[n] for n in SMALL])
    pm_, _ = _pack([m[n] for n in SMALL])
    pv_, _ = _pack([v[n] for n in SMALL])
    sd, sm, sv = _adam(pw_, pg_, pm_, pv_, name="adam_small")
    for n, a, b_, c_ in zip(SMALL, _unpack(sd, where_s), _unpack(sm, where_s), _unpack(sv, where_s)):
        delta[n], new_m[n], new_v[n] = a, b_, c_

    def lead(t):
        return t[None]

    return (loss, grad_x[None], *[lead(grads[n].reshape(w[n].shape)) for n in WEIGHTS],
            *[lead(delta[n]) for n in WEIGHTS], *[lead(new_m[n]) for n in WEIGHTS], *[lead(new_v[n]) for n in WEIGHTS])
```

```python
import functools

import jax
import jax.numpy as jnp
from jax import lax
from jax.experimental import pallas as pl
from jax.experimental.pallas import tpu as pltpu

F32 = jnp.float32
BF16 = jnp.bfloat16

CHUNK = 128
HEAD_DIM = 64
STATE = 128
HEADS_PER_GROUP = 4
SC_GROUP_WIDTH = 128
SSM_CONV = 5
SC_CONV = 3
N_MOD = 6
DEEPNORM_ALPHA = 2.0 ** 0.25
LN_EPS = 1e-5
RMS_EPS = 1e-5
ADAM_LR = 0.001
ADAM_B1 = 0.9
ADAM_B2 = 0.999
ADAM_EPS = 1e-08
ADAM_WD = 0.01
ADAM_STEP = 10
N_CHIPS = 4
N_DEV = 8
LANES = 128
SUBLANES = 8
HALO = 8
VMEM_LIMIT = 56 * 1024 * 1024
MESH = pl.DeviceIdType.MESH


def _params(sem=None):
    return pltpu.CompilerParams(dimension_semantics=sem, vmem_limit_bytes=VMEM_LIMIT)


def _pick(n, target, mult=LANES):
    best = None
    t = mult
    while t <= min(n, target):
        if n % t == 0:
            best = t
        t += mult
    return best if best is not None else n


ROW_TILE_BYTES = 2 << 20


def _tile_rows(s, width, mult=SUBLANES):
    return _pick(s, max(mult, ROW_TILE_BYTES // (4 * width)), mult)


def _sigmoid(v):
    return 1.0 / (1.0 + jnp.exp(-v))


def _silu(v):
    return v * _sigmoid(v)


def _dsilu(v):
    s = _sigmoid(v)
    return s * (1.0 + v * (1.0 - s))


def _softplus(v):
    e = jnp.exp(-jnp.abs(v))
    return jnp.maximum(v, 0.0) + jnp.where(e < 1e-4, e - 0.5 * e * e, jnp.log(1.0 + e))


def _dot(a, b):
    return jnp.dot(a, b, preferred_element_type=F32)


def _dot_nt(a, b):
    return lax.dot_general(a, b, (((1,), (1,)), ((), ())), preferred_element_type=F32)


def _dot_tn(a, b):
    return lax.dot_general(a, b, (((0,), (0,)), ((), ())), preferred_element_type=F32)


def _split3(v):
    hi = v.astype(BF16)
    r1 = v - hi.astype(F32)
    mid = r1.astype(BF16)
    lo = (r1 - mid.astype(F32)).astype(BF16)
    return hi, mid, lo


def _dot3_r(v, onehot):
    hi, mid, lo = _split3(v)
    return _dot(hi, onehot) + _dot(mid, onehot) + _dot(lo, onehot)


def _dot3_l(onehot, v):
    hi, mid, lo = _split3(v)
    return _dot(onehot, hi) + _dot(onehot, mid) + _dot(onehot, lo)


MATMUL_VMEM_BUDGET = 44 * 1024 * 1024


def _matmul(a, b, *, name, ta=False, tb=False, out_dtype=F32, b_blocks=1, out_blocks=1,
            tm=1024, tn=1024, tk=4096, comm=None, epilogue=None, extra=None):
    if ta:
        K, M = a.shape
    else:
        M, K = a.shape
    if b_blocks > 1:
        nb, r_, c_ = b.shape
        if tb:
            N, K2 = r_, c_ * nb
        else:
            K2, N = r_, c_ * nb
    else:
        if tb:
            N, K2 = b.shape
        else:
            K2, N = b.shape
    assert K == K2, (a.shape, b.shape, ta, tb)
    assert not (ta and tb)
    n_unit = N // b_blocks if (b_blocks > 1 and not tb) else N
    n_unit = min(n_unit, N // out_blocks)
    tn = _pick(n_unit, tn)
    k_unit = K // b_blocks if (b_blocks > 1 and tb) else K
    tk = _pick(k_unit, tk)
    tile_bytes = {None: jnp.dtype(out_dtype).itemsize, "relu2": 6, "relu2_bwd": 6}[epilogue]

    def vmem_need(tm_):
        need = 2 * (tm_ * tk * a.dtype.itemsize + tk * tn * b.dtype.itemsize) + 2 * tm_ * tn * tile_bytes
        return need + (tm_ * tn * 4 if K > tk else 0)

    tm = _pick(M, tm)
    while vmem_need(tm) > MATMUL_VMEM_BUDGET and tm % 2 == 0 and tm // 2 >= LANES:
        tm //= 2
    gm, gn, gk = M // tm, N // tn, K // tk

    if ta:
        a_spec = pl.BlockSpec((tk, tm), lambda i, j, k: (k, i))
    else:
        a_spec = pl.BlockSpec((tm, tk), lambda i, j, k: (i, k))
    if b_blocks > 1 and not tb:
        per = (N // b_blocks) // tn
        b_spec = pl.BlockSpec((None, tk, tn), lambda i, j, k: (j // per, k, j % per))
    elif b_blocks > 1 and tb:
        per = (K // b_blocks) // tk
        b_spec = pl.BlockSpec((None, tn, tk), lambda i, j, k: (k // per, j, k % per))
    elif tb:
        b_spec = pl.BlockSpec((tn, tk), lambda i, j, k: (j, k))
    else:
        b_spec = pl.BlockSpec((tk, tn), lambda i, j, k: (k, j))
    if out_blocks > 1:
        per_o = (N // out_blocks) // tn
        o_spec = pl.BlockSpec((None, tm, tn), lambda i, j, k: (j // per_o, i, j % per_o))
        o_shape = jax.ShapeDtypeStruct((out_blocks, M, N // out_blocks), out_dtype)
    else:
        o_spec = pl.BlockSpec((tm, tn), lambda i, j, k: (i, j))
        o_shape = jax.ShapeDtypeStruct((M, N), out_dtype)

    in_specs, args = [a_spec, b_spec], [a, b]
    out_specs, out_shape = [o_spec], [o_shape]
    if epilogue == "relu2":
        assert out_blocks == 1 and out_dtype == F32
        out_specs.append(o_spec)
        out_shape.append(jax.ShapeDtypeStruct((M, N), BF16))
    elif epilogue == "relu2_bwd":
        assert out_blocks == 1 and out_dtype == BF16
        in_specs.append(o_spec)
        args.append(extra)
    n_in = len(in_specs)

    def write(refs, p):
        o_ref = refs[n_in]
        if epilogue == "relu2":
            o_ref[...] = p
            r = jnp.maximum(p, 0.0)
            refs[n_in + 1][...] = (r * r).astype(BF16)
        elif epilogue == "relu2_bwd":
            o_ref[...] = (p * 2.0 * jnp.maximum(refs[2][...], 0.0)).astype(BF16)
        else:
            o_ref[...] = p.astype(out_dtype)

    def body(*refs):
        av = refs[0][...].astype(BF16)
        bv = refs[1][...].astype(BF16)
        p = _dot_tn(av, bv) if ta else (_dot_nt(av, bv) if tb else _dot(av, bv))
        if gk == 1:
            write(refs, p)
            return
        acc = refs[-1]
        k = pl.program_id(2)

        @pl.when(k == 0)
        def _():
            acc[...] = p

        @pl.when(jnp.logical_and(k > 0, k < gk - 1))
        def _():
            acc[...] += p

        @pl.when(k == gk - 1)
        def _():
            write(refs, acc[...] + p)

    scratch = [pltpu.VMEM((tm, tn), F32)] if gk > 1 else []
    if comm is not None:
        outs, landed = _comm_call(body, name=name, grid=(gm, gn, gk), in_specs=in_specs, out_specs=out_specs,
                                  out_shape=out_shape, scratch_shapes=scratch, args=args, comm=comm)
        return (outs[0] if len(outs) == 1 else tuple(outs)), landed
    outs = pl.pallas_call(
        body, name=name, grid=(gm, gn, gk), in_specs=in_specs, out_specs=out_specs,
        out_shape=out_shape, scratch_shapes=scratch,
        compiler_params=_params(("parallel", "parallel", "arbitrary")),
    )(*args)
    return outs[0] if len(outs) == 1 else tuple(outs)


def _comm_call(body, *, name, grid, in_specs, out_specs, out_shape, scratch_shapes, args, comm, aliases=None):
    n_in, n_out, n_scr = len(in_specs), len(out_shape), len(scratch_shapes)
    c_in, c_out = list(comm["inputs"]), list(comm["out_shape"])
    nci, nco = len(c_in), len(c_out)
    hbm = pl.BlockSpec(memory_space=pl.ANY)

    def body2(*refs):
        ins, cins = refs[:n_in], refs[n_in:n_in + nci]
        o0 = n_in + nci
        outs, couts = refs[o0:o0 + n_out], refs[o0 + n_out:o0 + n_out + nco]
        s0 = o0 + n_out + nco
        scr, cscr = refs[s0:s0 + n_scr], refs[s0 + n_scr:]
        first = functools.reduce(jnp.logical_and, [pl.program_id(a) == 0 for a in range(len(grid))])
        last = functools.reduce(jnp.logical_and, [pl.program_id(a) == grid[a] - 1 for a in range(len(grid))])

        @pl.when(first)
        def _():
            comm["start"](cins, couts, cscr)

        body(*ins, *outs, *scr)

        @pl.when(last)
        def _():
            comm["finish"](cins, couts, cscr)

    res = pl.pallas_call(
        body2, name=name, grid=grid, in_specs=list(in_specs) + [hbm] * nci, out_specs=list(out_specs) + [hbm] * nco,
        out_shape=list(out_shape) + c_out, scratch_shapes=list(scratch_shapes) + list(comm["scratch"]),
        input_output_aliases={**(aliases or {}), **{n_in + k: n_out + v for k, v in comm.get("aliases", {}).items()}},
        compiler_params=_params(("arbitrary",) * len(grid)),
    )(*args, *c_in)
    return res[:n_out], res[n_out:]


def _row(tr, w, blk=0):
    return pl.BlockSpec((tr, w), lambda i: (i, blk))


def _full(shape):
    nd = len(shape)
    return pl.BlockSpec(shape, lambda i: (0,) * nd)


def _halo_specs(s, tr, w, blk=0):
    per = tr // HALO
    last = s // HALO - 1
    return [
        pl.BlockSpec((HALO, w), lambda i: (jnp.maximum(i * per - 1, 0), blk)),
        pl.BlockSpec((tr, w), lambda i: (i, blk)),
        pl.BlockSpec((HALO, w), lambda i: (jnp.minimum((i + 1) * per, last), blk)),
    ]


def _ext(prev_ref, cur_ref, next_ref, s, tr):
    i = pl.program_id(0)
    e = jnp.concatenate([prev_ref[...].astype(F32), cur_ref[...].astype(F32), next_ref[...].astype(F32)], axis=0)
    rid = i * tr - HALO + lax.broadcasted_iota(jnp.int32, e.shape, 0)
    return jnp.where((rid >= 0) & (rid < s), e, 0.0)


def _valid_rows(shape, s, tr):
    i = pl.program_id(0)
    rid = i * tr - HALO + lax.broadcasted_iota(jnp.int32, shape, 0)
    return (rid >= 0) & (rid < s)


def _shift(e, k):
    if k == 0:
        return e
    n = e.shape[0]
    return pltpu.roll(e, (n - k) % n, 0)


def _acc_rows(ref, v):
    s = jnp.sum(v, axis=0, keepdims=True)

    @pl.when(pl.program_id(0) == 0)
    def _():
        ref[...] = s

    @pl.when(pl.program_id(0) > 0)
    def _():
        ref[...] += s


def _rowcall(body, name, s, tr, in_specs, out_specs, out_shape, args, comm=None, aliases=None):
    aliases = aliases or {}
    if comm is not None:
        single = not isinstance(out_shape, (list, tuple))
        outs, landed = _comm_call(body, name=name, grid=(s // tr,), in_specs=in_specs,
                                  out_specs=[out_specs] if single else out_specs,
                                  out_shape=[out_shape] if single else out_shape, scratch_shapes=[], args=args, comm=comm,
                                  aliases=aliases)
        return (outs[0] if single else outs), landed
    return pl.pallas_call(
        body, name=name, grid=(s // tr,), in_specs=in_specs, out_specs=out_specs, out_shape=out_shape,
        input_output_aliases=aliases, compiler_params=_params(("arbitrary",)),
    )(*args)


def _modulate(x, scale, shift, *, name, comm=None):
    s, d = x.shape
    tr = _tile_rows(s, d)

    def body(x_ref, sc_ref, sh_ref, o_ref):
        o_ref[...] = (x_ref[...] * (1.0 + sc_ref[...]) + sh_ref[...]).astype(BF16)

    return _rowcall(body, name, s, tr, [_row(tr, d), _full((1, d)), _full((1, d))], _row(tr, d),
                    jax.ShapeDtypeStruct((s, d), BF16), (x, scale, shift), comm=comm)


def _ln_stats(r):
    mu = jnp.mean(r, axis=-1, keepdims=True)
    xc = r - mu
    var = jnp.mean(xc * xc, axis=-1, keepdims=True)
    rstd = lax.rsqrt(var + LN_EPS)
    return xc * rstd, rstd


def _ln1_fwd(x, mix, gate, g, b, scale2, shift2, comm=None):
    s, d = x.shape
    tr = _tile_rows(s, d)

    def body(x_ref, m_ref, gt_ref, g_ref, b_ref, sc_ref, sh_ref, x1_ref, h2_ref):
        r = DEEPNORM_ALPHA * x_ref[...] + (1.0 + gt_ref[...]) * m_ref[...]
        xh, _ = _ln_stats(r)
        x1 = xh * g_ref[...] + b_ref[...]
        x1_ref[...] = x1
        h2_ref[...] = (x1 * (1.0 + sc_ref[...]) + sh_ref[...]).astype(BF16)

    v = _full((1, d))
    return _rowcall(body, "ln1_fwd", s, tr, [_row(tr, d), _row(tr, d), v, v, v, v, v],
                    [_row(tr, d), _row(tr, d)],
                    [jax.ShapeDtypeStruct((s, d), F32), jax.ShapeDtypeStruct((s, d), BF16)],
                    (x, mix, gate, g, b, scale2, shift2), comm=comm)


def _ln2_loss_bwd(x1, f, target, gate, g, b):
    s, d = x1.shape
    tr = _tile_rows(s, d)

    def body(x1_ref, f_ref, t_ref, gt_ref, g_ref, b_ref, df_ref, dr_ref, loss_ref, dg_ref, db_ref, dgt_ref):
        fv = f_ref[...]
        r = DEEPNORM_ALPHA * x1_ref[...] + (1.0 + gt_ref[...]) * fv
        xh, rstd = _ln_stats(r)
        y = xh * g_ref[...] + b_ref[...]
        err = y - t_ref[...]
        _acc_rows(loss_ref, 0.5 * jnp.mean(err * err, axis=-1, keepdims=True))
        dy = err * (1.0 / d)
        _acc_rows(dg_ref, dy * xh)
        _acc_rows(db_ref, dy)
        dxh = dy * g_ref[...]
        dr = rstd * (dxh - jnp.mean(dxh, axis=-1, keepdims=True) - xh * jnp.mean(dxh * xh, axis=-1, keepdims=True))
        dr_ref[...] = dr
        df_ref[...] = ((1.0 + gt_ref[...]) * dr).astype(BF16)
        _acc_rows(dgt_ref, dr * fv)

    v = _full((1, d))
    one = _full((1, 1))
    return _rowcall(body, "ln2_loss_bwd", s, tr, [_row(tr, d), _row(tr, d), _row(tr, d), v, v, v],
                    [_row(tr, d), _row(tr, d), one, v, v, v],
                    [jax.ShapeDtypeStruct((s, d), BF16), jax.ShapeDtypeStruct((s, d), F32),
                     jax.ShapeDtypeStruct((1, 1), F32)] + [jax.ShapeDtypeStruct((1, d), F32)] * 3,
                    (x1, f, target, gate, g, b))


def _ln1_bwd(dh2, dr2, x1, x, mix, scale2, gate1, g1, comm=None):
    s, d = x.shape
    tr = _tile_rows(s, d)

    def body(dh_ref, dr2_ref, x1_ref, x_ref, m_ref, sc_ref, gt_ref, g_ref,
             dm_ref, dxa_ref, dsc_ref, dsh_ref, dg_ref, db_ref, dgt_ref):
        dh = dh_ref[...]
        _acc_rows(dsc_ref, dh * x1_ref[...])
        _acc_rows(dsh_ref, dh)
        dy = dh * (1.0 + sc_ref[...]) + DEEPNORM_ALPHA * dr2_ref[...]
        mv = m_ref[...]
        r = DEEPNORM_ALPHA * x_ref[...] + (1.0 + gt_ref[...]) * mv
        xh, rstd = _ln_stats(r)
        _acc_rows(dg_ref, dy * xh)
        _acc_rows(db_ref, dy)
        dxh = dy * g_ref[...]
        dr = rstd * (dxh - jnp.mean(dxh, axis=-1, keepdims=True) - xh * jnp.mean(dxh * xh, axis=-1, keepdims=True))
        dm_ref[...] = ((1.0 + gt_ref[...]) * dr).astype(BF16)
        dxa_ref[...] = DEEPNORM_ALPHA * dr
        _acc_rows(dgt_ref, dr * mv)

    v = _full((1, d))
    return _rowcall(body, "ln1_bwd", s, tr, [_row(tr, d)] * 5 + [v, v, v],
                    [_row(tr, d), _row(tr, d), v, v, v, v, v],
                    [jax.ShapeDtypeStruct((s, d), BF16), jax.ShapeDtypeStruct((s, d), F32)]
                    + [jax.ShapeDtypeStruct((1, d), F32)] * 5,
                    (dh2, dr2, x1, x, mix, scale2, gate1, g1), comm=comm)


def _dx_final(dxa, dh1, x, scale1):
    s, d = x.shape
    tr = _tile_rows(s, d)

    def body(a_ref, dh_ref, x_ref, sc_ref, o_ref, dsc_ref, dsh_ref):
        dh = dh_ref[...]
        o_ref[...] = a_ref[...] + dh * (1.0 + sc_ref[...])
        _acc_rows(dsc_ref, dh * x_ref[...])
        _acc_rows(dsh_ref, dh)

    v = _full((1, d))
    return _rowcall(body, "dx_final", s, tr, [_row(tr, d)] * 3 + [v], [_row(tr, d), v, v],
                    [jax.ShapeDtypeStruct((s, d), F32)] + [jax.ShapeDtypeStruct((1, d), F32)] * 2,
                    (dxa, dh1, x, scale1))


P_X, P_BC, P_Z, P_H, P_B, P_C = range(6)
HBM_REF = pl.BlockSpec(memory_space=pl.ANY)


def _conv_silu_fwd(proj, conv_w, conv_b, s, u, comm=None):
    tr = _tile_rows(s, u)
    w = 2 * u
    half = SSM_CONV // 2

    def body(p0, c0, n0, p1, c1, n1, w_ref, b_ref, o_ref):
        for blk, (pr, cr, nr) in enumerate(((p0, c0, n0), (p1, c1, n1))):
            e = _ext(pr, cr, nr, s, tr)
            wv = w_ref[:, blk * u:(blk + 1) * u]
            acc = jnp.zeros_like(e)
            for k in range(SSM_CONV):
                acc = acc + _shift(e, k - half) * wv[k:k + 1, :]
            pre = acc[HALO:HALO + tr] + b_ref[:, blk * u:(blk + 1) * u]
            o_ref[:, blk * u:(blk + 1) * u] = _silu(pre)

    in_specs = _halo_specs(s, tr, u, P_X) + _halo_specs(s, tr, u, P_BC) + [_full((SSM_CONV, w)), _full((1, w))]
    return _rowcall(body, "conv_silu_fwd", s, tr, in_specs, _row(tr, w), jax.ShapeDtypeStruct((s, w), F32),
                    (proj,) * 6 + (conv_w, conv_b), comm=comm)


def _conv_silu_bwd(proj, dxbc, conv_w, conv_b, dproj, s, u, comm=None):
    tr = _tile_rows(s, u)
    w = 2 * u
    half = SSM_CONV // 2

    def body(p0, c0, n0, p1, c1, n1, dp0, dc0, dn0, dp1, dc1, dn1, w_ref, b_ref, dproj_in, du_ref, dw_ref, db_ref):
        for blk, (ur, dr) in enumerate((((p0, c0, n0), (dp0, dc0, dn0)), ((p1, c1, n1), (dp1, dc1, dn1)))):
            e = _ext(*ur, s, tr)
            de = _ext(*dr, s, tr)
            wv = w_ref[:, blk * u:(blk + 1) * u]
            acc = jnp.zeros_like(e)
            for k in range(SSM_CONV):
                acc = acc + _shift(e, k - half) * wv[k:k + 1, :]
            pre = acc + b_ref[:, blk * u:(blk + 1) * u]
            dpre = jnp.where(_valid_rows(e.shape, s, tr), de * _dsilu(pre), 0.0)
            du = jnp.zeros_like(e)
            rows = []
            for k in range(SSM_CONV):
                du = du + _shift(dpre, half - k) * wv[k:k + 1, :]
                rows.append(jnp.sum((_shift(e, k - half) * dpre)[HALO:HALO + tr], axis=0, keepdims=True))
            du_ref[:, blk * u:(blk + 1) * u] = du[HALO:HALO + tr].astype(BF16)
            dwv = jnp.concatenate(rows + [jnp.zeros((SUBLANES - SSM_CONV, u), F32)], axis=0)
            dbv = jnp.sum(dpre[HALO:HALO + tr], axis=0, keepdims=True)
            first = pl.program_id(0) == 0

            @pl.when(first)
            def _():
                dw_ref[:, blk * u:(blk + 1) * u] = dwv
                db_ref[:, blk * u:(blk + 1) * u] = dbv

            @pl.when(jnp.logical_not(first))
            def _():
                dw_ref[:, blk * u:(blk + 1) * u] += dwv
                db_ref[:, blk * u:(blk + 1) * u] += dbv

    in_specs = (_halo_specs(s, tr, u, P_X) + _halo_specs(s, tr, u, P_BC) + _halo_specs(s, tr, u, 0)
                + _halo_specs(s, tr, u, 1) + [_full((SSM_CONV, w)), _full((1, w)), HBM_REF])
    return _rowcall(body, "conv_silu_bwd", s, tr, in_specs,
                    [_row(tr, w), _full((SUBLANES, w)), _full((1, w))],
                    [jax.ShapeDtypeStruct(dproj.shape, BF16), jax.ShapeDtypeStruct((SUBLANES, w), F32),
                     jax.ShapeDtypeStruct((1, w), F32)],
                    (proj,) * 6 + (dxbc,) * 6 + (conv_w, conv_b, dproj), comm=comm, aliases={14: 0})


def _expanders(h):
    col64 = jnp.arange(2 * h * HEAD_DIM) // HEAD_DIM
    col128 = jnp.arange(2 * h * LANES) // LANES
    row = jnp.arange(LANES)[:, None]
    return (row == col64[None, :]).astype(BF16), (row == col128[None, :]).astype(BF16)


def _dt_prep(proj, bias_row, a_row, s, u, h, comm=None):
    q = CHUNK
    e64, e128 = _expanders(h)
    ds = h * HEAD_DIM
    dtblk = (6 * u) // LANES

    def body(raw_ref, b_ref, a_ref, e64_ref, e128_ref, dt_ref, cum_ref, dte_ref, cume_ref):
        lane = lax.broadcasted_iota(jnp.int32, (q, LANES), 1)
        dt = jnp.where(lane < 2 * h, _softplus(raw_ref[...] + b_ref[...]), 0.0)
        da = dt * a_ref[...]
        ii = lax.broadcasted_iota(jnp.int32, (q, q), 0)
        kk = lax.broadcasted_iota(jnp.int32, (q, q), 1)
        lower = (kk <= ii).astype(F32).astype(BF16)
        upper = (kk >= ii).astype(F32).astype(BF16)
        cum = jnp.where(lane < h, _dot3_l(lower, da), _dot3_l(upper, da))
        dt_ref[...] = dt
        cum_ref[...] = cum
        dte = _dot3_r(dt, e64_ref[...])
        cume = _dot3_r(cum, e128_ref[...])
        dte_ref[0] = dte[:, :ds]
        dte_ref[1] = dte[:, ds:]
        cume_ref[0] = cume[:, :h * LANES]
        cume_ref[1] = cume[:, h * LANES:]

    in_specs = [pl.BlockSpec((q, LANES), lambda i: (i, dtblk)), _full((1, LANES)), _full((1, LANES)),
                _full(e64.shape), _full(e128.shape)]
    out_specs = [_row(q, LANES), _row(q, LANES),
                 pl.BlockSpec((2, q, ds), lambda i: (0, i, 0)), pl.BlockSpec((2, q, h * LANES), lambda i: (0, i, 0))]
    out_shape = [jax.ShapeDtypeStruct((s, LANES), F32), jax.ShapeDtypeStruct((s, LANES), F32),
                 jax.ShapeDtypeStruct((2, s, ds), F32), jax.ShapeDtypeStruct((2, s, h * LANES), F32)]
    return _rowcall(body, "dt_prep", s, q, in_specs, out_specs, out_shape, (proj, bias_row, a_row, e64, e128), comm=comm)


def _ssd_specs(s, h, g):
    q = CHUNK
    nc = s // q
    ds = h * HEAD_DIM
    nb = g * STATE
    return q, nc, ds, nb


def _ssd_fwd(xbc, dt_e, cum_e, cum_t, s, h, g, comm=None):
    q, nc, ds, nb = _ssd_specs(s, h, g)
    npair = h // 2

    def cidx(d, i):
        return jnp.where(d == 0, i, nc - 1 - i)

    def body(x_ref, b_ref, c_ref, dt_ref, cum_ref, cumt_ref, y_ref, sp_ref, st):
        d = pl.program_id(0)
        i = pl.program_id(1)

        @pl.when(i == 0)
        def _():
            st[...] = jnp.zeros_like(st)

        rev = d == 1
        ii = lax.broadcasted_iota(jnp.int32, (q, q), 0)
        jj = lax.broadcasted_iota(jnp.int32, (q, q), 1)
        sgn = jnp.where(rev, -1, 1)
        mask = (jj - ii) * sgn <= 0
        left = lax.broadcasted_iota(jnp.int32, (q, LANES), 1) < HEAD_DIM

        def group(gi, carry):
            goff = pl.multiple_of(gi * STATE, STATE)
            cg = c_ref[:, pl.ds(goff, STATE)].astype(BF16)
            bg = b_ref[:, pl.ds(goff, STATE)].astype(BF16)
            gm = _dot_nt(cg, bg)
            for p in range(HEADS_PER_GROUP // 2):
                pr = gi * (HEADS_PER_GROUP // 2) + p
                off = pl.multiple_of(pr * LANES, LANES)
                xd = x_ref[:, pl.ds(off, LANES)] * dt_ref[:, pl.ds(off, LANES)]
                ms = []
                cols = []
                for hl in range(2):
                    hh = 2 * pr + hl
                    col = cum_ref[:, pl.ds(pl.multiple_of(hh * LANES, LANES), LANES)]
                    row = cumt_ref[pl.ds(hh, 1), :]
                    lm = jnp.where(mask, jnp.exp(jnp.minimum(col - row, 0.0)), 0.0)
                    ms.append((gm * lm).astype(BF16))
                    cols.append(col)
                y = _dot(ms[0], jnp.where(left, xd, 0.0).astype(BF16)) + _dot(ms[1], jnp.where(left, 0.0, xd).astype(BF16))
                ce = jnp.where(left, cols[0], cols[1])
                sprev = st[pr]
                sp_ref[pr] = sprev
                y = y + jnp.exp(ce) * _dot(cg, sprev.astype(BF16))
                y_ref[:, pl.ds(off, LANES)] = y
                tot = jnp.where(rev, ce[0:1, :], ce[q - 1:q, :])
                v = (xd * jnp.exp(tot - ce)).astype(BF16)
                st[pr] = jnp.exp(tot) * sprev + _dot_tn(bg, v)
            return carry

        lax.fori_loop(0, g, group, 0)

    in_specs = [
        pl.BlockSpec((q, ds), lambda d, i: (cidx(d, i), 0)),
        pl.BlockSpec((q, nb), lambda d, i: (cidx(d, i), ds // nb)),
        pl.BlockSpec((q, nb), lambda d, i: (cidx(d, i), ds // nb + 1)),
        pl.BlockSpec((None, q, ds), lambda d, i: (d, cidx(d, i), 0)),
        pl.BlockSpec((None, q, h * LANES), lambda d, i: (d, cidx(d, i), 0)),
        pl.BlockSpec((None, h, q), lambda d, i: (d, 0, cidx(d, i))),
    ]
    out_specs = [
        pl.BlockSpec((None, q, ds), lambda d, i: (d, cidx(d, i), 0)),
        pl.BlockSpec((None, None, npair, STATE, LANES), lambda d, i: (d, cidx(d, i), 0, 0, 0)),
    ]
    out_shape = [jax.ShapeDtypeStruct((2, s, ds), F32), jax.ShapeDtypeStruct((2, nc, npair, STATE, LANES), F32)]
    if comm is not None:
        return _comm_call(body, name="ssd_fwd", grid=(2, nc), in_specs=in_specs, out_specs=out_specs, out_shape=out_shape,
                          scratch_shapes=[pltpu.VMEM((npair, STATE, LANES), F32)],
                          args=(xbc, xbc, xbc, dt_e, cum_e, cum_t), comm=comm)
    return pl.pallas_call(
        body, name="ssd_fwd", grid=(2, nc), in_specs=in_specs, out_specs=out_specs, out_shape=out_shape,
        scratch_shapes=[pltpu.VMEM((npair, STATE, LANES), F32)],
        compiler_params=_params(("arbitrary", "arbitrary")),
    )(xbc, xbc, xbc, dt_e, cum_e, cum_t), ()


def _ssd_bwd(xbc, dt_e, cum_e, cum_t, dt_t, a_col, dy, sp, s, h, g, comm=None):
    q, nc, ds, nb = _ssd_specs(s, h, g)
    npair = h // 2

    def cidx(d, i):
        return jnp.where(d == 0, nc - 1 - i, i)

    def body(x_ref, b_ref, c_ref, dt_ref, cum_ref, cumt_ref, dtt_ref, a_ref, dy_ref, sp_ref,
             dx_ref, db_ref, dc_ref, ddt_ref, da_ref, dst, rowp):
        d = pl.program_id(0)
        i = pl.program_id(1)

        @pl.when(i == 0)
        def _():
            dst[...] = jnp.zeros_like(dst)
            da_ref[...] = jnp.zeros_like(da_ref)

        rev = d == 1
        ii = lax.broadcasted_iota(jnp.int32, (q, q), 0)
        jj = lax.broadcasted_iota(jnp.int32, (q, q), 1)
        sgn = jnp.where(rev, -1, 1)
        mask = (jj - ii) * sgn <= 0
        lane = lax.broadcasted_iota(jnp.int32, (q, LANES), 1)
        left = lane < HEAD_DIM
        rowp[...] = jnp.zeros_like(rowp)

        def group(gi, carry):
            acc_dcum, acc_tot, acc_dxx = carry
            goff = pl.multiple_of(gi * STATE, STATE)
            cg = c_ref[:, pl.ds(goff, STATE)].astype(BF16)
            bg = b_ref[:, pl.ds(goff, STATE)].astype(BF16)
            gm = _dot_nt(cg, bg)
            dgm = jnp.zeros((q, q), F32)
            dcg = jnp.zeros((q, STATE), F32)
            dbg = jnp.zeros((q, STATE), F32)
            for p in range(HEADS_PER_GROUP // 2):
                pr = gi * (HEADS_PER_GROUP // 2) + p
                off = pl.multiple_of(pr * LANES, LANES)
                xv = x_ref[:, pl.ds(off, LANES)]
                dte = dt_ref[:, pl.ds(off, LANES)]
                xd = xv * dte
                xdb = xd.astype(BF16)
                dyv = dy_ref[:, pl.ds(off, LANES)]
                sprev = sp_ref[pr]
                sprevb = sprev.astype(BF16)
                dsn = dst[pr]
                dsnb = dsn.astype(BF16)
                cols = [cum_ref[:, pl.ds(pl.multiple_of((2 * pr + hl) * LANES, LANES), LANES)] for hl in range(2)]
                ce = jnp.where(left, cols[0], cols[1])
                tot = jnp.where(rev, ce[0:1, :], ce[q - 1:q, :])
                et = jnp.exp(tot)
                r = jnp.exp(tot - ce)
                e = jnp.exp(ce)
                yoff = e * _dot(cg, sprevb)
                dz = (e * dyv).astype(BF16)
                dcg = dcg + _dot_nt(dz, sprevb)
                dsprev = _dot_tn(cg, dz) + et * dsn
                f1 = dyv * yoff
                v = (xd * r).astype(BF16)
                dbg = dbg + _dot_nt(v, dsnb)
                dv = _dot(bg, dsnb)
                dxd = dv * r
                tt = dv * xd * r
                wt = dsn * sprev * et
                for hl in range(2):
                    hh = 2 * pr + hl
                    hm = left if hl == 0 else jnp.logical_not(left)
                    row = cumt_ref[pl.ds(hh, 1), :]
                    lm = jnp.where(mask, jnp.exp(jnp.minimum(cols[hl] - row, 0.0)), 0.0)
                    mf = gm * lm
                    dym = jnp.where(hm, dyv, 0.0).astype(BF16)
                    dm = _dot_nt(dym, xdb)
                    dxd = dxd + _dot_tn(mf.astype(BF16), dym)
                    dgm = dgm + dm * lm
                    em = dm * mf
                    rowp[pl.ds(hh, 1), :] = rowp[pl.ds(hh, 1), :] - jnp.sum(em, axis=0, keepdims=True)
                    colq = (jnp.sum(em, axis=1, keepdims=True)
                            + jnp.sum(jnp.where(hm, f1 - tt, 0.0), axis=1, keepdims=True))
                    acc_dcum = jnp.where(lane == hh, colq, acc_dcum)
                    totq = jnp.sum(jnp.sum(jnp.where(hm, tt + wt, 0.0), axis=1, keepdims=True), axis=0, keepdims=True)
                    acc_tot = jnp.where(lane == hh, totq, acc_tot)
                dxx = dxd * xv
                for hl in range(2):
                    hh = 2 * pr + hl
                    hm = left if hl == 0 else jnp.logical_not(left)
                    acc_dxx = jnp.where(lane == hh, jnp.sum(jnp.where(hm, dxx, 0.0), axis=1, keepdims=True), acc_dxx)
                dx_ref[:, pl.ds(off, LANES)] = dxd * dte
                dst[pr] = dsprev
            dgb = dgm.astype(BF16)
            dc_ref[:, pl.ds(goff, STATE)] = dcg + _dot(dgb, bg)
            db_ref[:, pl.ds(goff, STATE)] = dbg + _dot_tn(dgb, cg)
            return acc_dcum, acc_tot, acc_dxx

        zero = jnp.zeros((q, LANES), F32)
        acc_dcum, acc_tot, acc_dxx = lax.fori_loop(0, g, group, (zero, zero, zero))
        dcum_t = rowp[...] + jnp.transpose(acc_dcum)[:h]
        rmat = ((ii - jj) * sgn >= 0).astype(F32).astype(BF16)
        da_t = _dot3_r(dcum_t, rmat) + jnp.transpose(acc_tot)[:h]
        ddt_ref[...] = da_t * a_ref[...] + jnp.transpose(acc_dxx)[:h]
        da_ref[...] += da_t * dtt_ref[...]

    in_specs = [
        pl.BlockSpec((q, ds), lambda d, i: (cidx(d, i), 0)),
        pl.BlockSpec((q, nb), lambda d, i: (cidx(d, i), ds // nb)),
        pl.BlockSpec((q, nb), lambda d, i: (cidx(d, i), ds // nb + 1)),
        pl.BlockSpec((None, q, ds), lambda d, i: (d, cidx(d, i), 0)),
        pl.BlockSpec((None, q, h * LANES), lambda d, i: (d, cidx(d, i), 0)),
        pl.BlockSpec((None, h, q), lambda d, i: (d, 0, cidx(d, i))),
        pl.BlockSpec((None, h, q), lambda d, i: (d, 0, cidx(d, i))),
        pl.BlockSpec((None, h, LANES), lambda d, i: (d, 0, 0)),
        pl.BlockSpec((q, ds), lambda d, i: (cidx(d, i), 0)),
        pl.BlockSpec((None, None, npair, STATE, LANES), lambda d, i: (d, cidx(d, i), 0, 0, 0)),
    ]
    out_specs = [
        pl.BlockSpec((None, q, ds), lambda d, i: (d, cidx(d, i), 0)),
        pl.BlockSpec((None, q, nb), lambda d, i: (d, cidx(d, i), 0)),
        pl.BlockSpec((None, q, nb), lambda d, i: (d, cidx(d, i), 0)),
        pl.BlockSpec((None, h, q), lambda d, i: (d, 0, cidx(d, i))),
        pl.BlockSpec((None, h, LANES), lambda d, i: (d, 0, 0)),
    ]
    out_shape = [jax.ShapeDtypeStruct((2, s, ds), F32), jax.ShapeDtypeStruct((2, s, nb), F32),
                 jax.ShapeDtypeStruct((2, s, nb), F32), jax.ShapeDtypeStruct((2, h, s), F32),
                 jax.ShapeDtypeStruct((2, h, LANES), F32)]
    scratch = [pltpu.VMEM((npair, STATE, LANES), F32), pltpu.VMEM((h, q), F32)]
    args = (xbc, xbc, xbc, dt_e, cum_e, cum_t, dt_t, a_col, dy, sp)
    if comm is not None:
        return _comm_call(body, name="ssd_bwd", grid=(2, nc), in_specs=in_specs, out_specs=out_specs, out_shape=out_shape,
                          scratch_shapes=scratch, args=args, comm=comm)
    return pl.pallas_call(
        body, name="ssd_bwd", grid=(2, nc), in_specs=in_specs, out_specs=out_specs, out_shape=out_shape,
        scratch_shapes=scratch, compiler_params=_params(("arbitrary", "arbitrary")),
    )(*args), ()


def _dt_bwd(ddt, proj, bias_row, dproj, s, u, h):
    tr = _tile_rows(s, 4 * LANES)
    dtblk = (6 * u) // LANES
    tail = dproj.shape[1] - 6 * u
    assert (6 * u) % tail == 0

    def body(d_ref, raw_ref, b_ref, dproj_in, o_ref, db_ref):
        lane = lax.broadcasted_iota(jnp.int32, (tr, LANES), 1)
        v = jnp.where(lane < 2 * h, d_ref[...] * _sigmoid(raw_ref[...] + b_ref[...]), 0.0)
        o_ref[:, :LANES] = v.astype(BF16)
        o_ref[:, LANES:] = jnp.zeros((tr, tail - LANES), BF16)
        _acc_rows(db_ref, v)

    return _rowcall(body, "dt_bwd", s, tr, [_row(tr, LANES), _row(tr, LANES, dtblk), _full((1, LANES)), HBM_REF],
                    [_row(tr, tail, (6 * u) // tail), _full((1, LANES))],
                    [jax.ShapeDtypeStruct(dproj.shape, BF16), jax.ShapeDtypeStruct((1, LANES), F32)],
                    (ddt, proj, bias_row, dproj), aliases={3: 0})


def _group_rms(v, gw):
    outs, facs = [], []
    for k in range(v.shape[1] // gw):
        blk = v[:, k * gw:(k + 1) * gw]
        f = lax.rsqrt(jnp.mean(blk * blk, axis=-1, keepdims=True) + RMS_EPS)
        outs.append(blk * f)
        facs.append(jnp.broadcast_to(f, blk.shape))
    return jnp.concatenate(outs, axis=1), jnp.concatenate(facs, axis=1)


def _group_rms_bwd(dn, n, fac, gw):
    outs = []
    for k in range(n.shape[1] // gw):
        sl = slice(k * gw, (k + 1) * gw)
        outs.append(fac[:, sl] * (dn[:, sl] - n[:, sl] * jnp.mean(dn[:, sl] * n[:, sl], axis=-1, keepdims=True)))
    return jnp.concatenate(outs, axis=1)


def _gate_norm_fwd(y2, xbc, proj, d_e, norm_w, s, u, g):
    tr = _tile_rows(s, u)
    gw = u // g

    def body(y_ref, x_ref, z_ref, d_ref, w_ref, o_ref):
        ys = y_ref[0] + y_ref[1] + d_ref[...] * x_ref[...]
        n, _ = _group_rms(ys * _silu(z_ref[...]), gw)
        o_ref[...] = (n * w_ref[...]).astype(BF16)

    return _rowcall(body, "gate_norm_fwd", s, tr,
                    [pl.BlockSpec((2, tr, u), lambda i: (0, i, 0)), _row(tr, u), _row(tr, u, P_Z), _full((1, u)), _full((1, u))],
                    _row(tr, u), jax.ShapeDtypeStruct((s, u), BF16), (y2, xbc, proj, d_e, norm_w))


def _gate_norm_bwd(dymix, y2, xbc, proj, d_e, norm_w, pw, s, u, g, comm=None):
    tr = _tile_rows(s, u)
    gw = u // g

    def body(dy_ref, y_ref, x_ref, z_ref, d_ref, w_ref, dys_ref, dz_ref, dxs_ref, dw_ref, dd_ref):
        xv = x_ref[...]
        zv = z_ref[...]
        ys = y_ref[0] + y_ref[1] + d_ref[...] * xv
        sz = _silu(zv)
        n, fac = _group_rms(ys * sz, gw)
        dout = dy_ref[...]
        _acc_rows(dw_ref, dout * n)
        dyg = _group_rms_bwd(dout * w_ref[...], n, fac, gw)
        dys = dyg * sz
        dys_ref[...] = dys
        dz_ref[...] = (dyg * ys * _dsilu(zv)).astype(BF16)
        dxs_ref[...] = dys * d_ref[...]
        _acc_rows(dd_ref, dys * xv)

    v = _full((1, u))
    return _rowcall(body, "gate_norm_bwd", s, tr,
                    [_row(tr, u), pl.BlockSpec((2, tr, u), lambda i: (0, i, 0)), _row(tr, u), _row(tr, u, P_Z), v, v],
                    [_row(tr, u), _row(tr, u, P_Z), _row(tr, u), v, v],
                    [jax.ShapeDtypeStruct((s, u), F32), jax.ShapeDtypeStruct((s, pw), BF16),
                     jax.ShapeDtypeStruct((s, u), F32), jax.ShapeDtypeStruct((1, u), F32), jax.ShapeDtypeStruct((1, u), F32)],
                    (dymix, y2, xbc, proj, d_e, norm_w), comm=comm)


def _shortconv_fwd(proj, conv_w, norm_w, s, u):
    tr = _tile_rows(s, u)
    half = SC_CONV // 2

    def body(hp, hc, hn, b_ref, cp, cc, cn, cw_ref, w_ref, o_ref):
        t = _ext(hp, hc, hn, s, tr) * _ext(cp, cc, cn, s, tr)
        wv = cw_ref[...]
        acc = jnp.zeros_like(t)
        for k in range(SC_CONV):
            acc = acc + _shift(t, k - half) * wv[k:k + 1, :]
        n, _ = _group_rms(b_ref[...] * acc[HALO:HALO + tr], SC_GROUP_WIDTH)
        o_ref[...] = (n * w_ref[...]).astype(BF16)

    in_specs = (_halo_specs(s, tr, u, P_H) + [_row(tr, u, P_B)] + _halo_specs(s, tr, u, P_C)
                + [_full((SC_CONV, u)), _full((1, u))])
    return _rowcall(body, "shortconv_fwd", s, tr, in_specs, _row(tr, u), jax.ShapeDtypeStruct((s, u), BF16),
                    (proj,) * 7 + (conv_w, norm_w))


def _shortconv_bwd(dymix, proj, conv_w, norm_w, dproj, s, u):
    tr = _tile_rows(s, u)
    half = SC_CONV // 2

    def body(dp, dc_, dn, hp, hc, hn, bp, bc, bn, cp, cc, cn, cw_ref, w_ref, dproj_in, o_ref, dcw_ref, dw_ref):
        dout = _ext(dp, dc_, dn, s, tr)
        hv = _ext(hp, hc, hn, s, tr)
        bv = _ext(bp, bc, bn, s, tr)
        cv = _ext(cp, cc, cn, s, tr)
        t = hv * cv
        wv = cw_ref[...]
        acc = jnp.zeros_like(t)
        for k in range(SC_CONV):
            acc = acc + _shift(t, k - half) * wv[k:k + 1, :]
        n, fac = _group_rms(bv * acc, SC_GROUP_WIDTH)
        cur = slice(HALO, HALO + tr)
        _acc_rows(dw_ref, (dout * n)[cur])
        dyv = _group_rms_bwd(dout * w_ref[...], n, fac, SC_GROUP_WIDTH)
        o_ref[:, u:2 * u] = (dyv * acc)[cur].astype(BF16)
        dv = dyv * bv
        dt = jnp.zeros_like(t)
        rows = []
        for k in range(SC_CONV):
            dt = dt + _shift(dv, half - k) * wv[k:k + 1, :]
            rows.append(jnp.sum((_shift(t, k - half) * dv)[cur], axis=0, keepdims=True))
        o_ref[:, :u] = (dt * cv)[cur].astype(BF16)
        o_ref[:, 2 * u:] = (dt * hv)[cur].astype(BF16)
        dwv = jnp.concatenate(rows + [jnp.zeros((SUBLANES - SC_CONV, u), F32)], axis=0)
        first = pl.program_id(0) == 0

        @pl.when(first)
        def _():
            dcw_ref[...] = dwv

        @pl.when(jnp.logical_not(first))
        def _():
            dcw_ref[...] += dwv

    in_specs = (_halo_specs(s, tr, u, 1) + _halo_specs(s, tr, u, P_H) + _halo_specs(s, tr, u, P_B)
                + _halo_specs(s, tr, u, P_C) + [_full((SC_CONV, u)), _full((1, u)), HBM_REF])
    return _rowcall(body, "shortconv_bwd", s, tr, in_specs,
                    [_row(tr, 3 * u, 1), _full((SUBLANES, u)), _full((1, u))],
                    [jax.ShapeDtypeStruct(dproj.shape, BF16), jax.ShapeDtypeStruct((SUBLANES, u), F32),
                     jax.ShapeDtypeStruct((1, u), F32)],
                    (dymix,) * 3 + (proj,) * 9 + (conv_w, norm_w, dproj), aliases={14: 0})


def _adam_math(w, g, m, v):
    m2 = ADAM_B1 * m + (1.0 - ADAM_B1) * g
    v2 = ADAM_B2 * v + (1.0 - ADAM_B2) * (g * g)
    m_hat = m2 / (1.0 - ADAM_B1 ** ADAM_STEP)
    v_hat = v2 / (1.0 - ADAM_B2 ** ADAM_STEP)
    delta = -ADAM_LR * (m_hat / (jnp.sqrt(v_hat) + ADAM_EPS) + ADAM_WD * w)
    return delta, m2, v2


def _adam_rows(r, c):
    return _pick(r, max(SUBLANES, (1 << 20) // (4 * c)), SUBLANES)


def _adam(w, g, m, v, *, name, emit_grad=False):
    r, c = w.shape
    tr = _adam_rows(r, c)
    n_out = 4 if emit_grad else 3

    def body(w_ref, g_ref, m_ref, v_ref, *outs):
        gv = g_ref[...]
        if emit_grad:
            outs[0][...] = gv
        outs[-3][...], outs[-2][...], outs[-1][...] = _adam_math(w_ref[...], gv, m_ref[...], v_ref[...])

    return _rowcall(body, name, r, tr, [_row(tr, c)] * 4, [_row(tr, c)] * n_out,
                    [jax.ShapeDtypeStruct((r, c), F32)] * n_out, (w, g, m, v))


def _adam_outer(w, a_t, bmat, m, v, *, name, comm=None):
    r, c = w.shape
    tr = _adam_rows(r, c)
    kk = a_t.shape[1]

    def body(w_ref, a_ref, b_ref, m_ref, v_ref, g_ref, d_ref, m2_ref, v2_ref):
        g = _dot(a_ref[...].astype(BF16), b_ref[...].astype(BF16))
        g_ref[...] = g
        d_ref[...], m2_ref[...], v2_ref[...] = _adam_math(w_ref[...], g, m_ref[...], v_ref[...])

    return _rowcall(body, name, r, tr, [_row(tr, c), _row(tr, kk), _full((kk, c)), _row(tr, c), _row(tr, c)],
                    [_row(tr, c)] * 4, [jax.ShapeDtypeStruct((r, c), F32)] * 4, (w, a_t, bmat, m, v), comm=comm)


ANY = pl.BlockSpec(memory_space=pl.ANY)
VMEM_WHOLE = pl.BlockSpec(memory_space=pltpu.VMEM)


def _place():
    x, y, c = lax.axis_index("x"), lax.axis_index("y"), lax.axis_index("c")
    return x, y, c


DMA_CHUNKS = 8


def _n_chunks(rows):
    n = DMA_CHUNKS
    while n > 1 and rows % (16 * n):
        n //= 2
    return n


def _allgather_small(v, *, name):
    m_per, n = v.shape

    def body(x_ref, out_ref, send_sems, recv_sems, local_sem):
        x, y, c = _place()
        me, sibling = (x, y, c), (x, y, 1 - c)
        chips = [(1 - x, y), (x, 1 - y), (1 - x, 1 - y)]

        def rows(px, py, pc):
            return out_ref.at[pl.ds((4 * px + 2 * py + pc) * m_per, m_per), :]

        def copy(k, block, to, src=None):
            return pltpu.make_async_remote_copy(
                src_ref=rows(*block) if src is None else src, dst_ref=rows(*block),
                send_sem=send_sems.at[k], recv_sem=recv_sems.at[k], device_id=to, device_id_type=MESH)

        mine = pltpu.make_async_copy(x_ref, rows(*me), local_sem)
        mine.start()
        first = [copy(0, me, sibling, src=x_ref)]
        first += [copy(1 + j, me, (*chip, c), src=x_ref) for j, chip in enumerate(chips)]
        for cp in first:
            cp.start()
        passed = [copy(4 + j, (*chip, c), sibling) for j, chip in enumerate(chips)]
        for j, chip in enumerate(chips):
            copy(1 + j, (*chip, c), me).wait_recv()
            passed[j].start()
        copy(0, sibling, me).wait_recv()
        for j, chip in enumerate(chips):
            copy(4 + j, (*chip, 1 - c), me).wait_recv()
        for cp in first + passed:
            cp.wait_send()
        mine.wait()

    return pl.pallas_call(
        body, name=name, out_shape=jax.ShapeDtypeStruct((N_DEV * m_per, n), v.dtype),
        in_specs=[VMEM_WHOLE], out_specs=VMEM_WHOLE,
        scratch_shapes=[pltpu.SemaphoreType.DMA((7,)), pltpu.SemaphoreType.DMA((7,)), pltpu.SemaphoreType.DMA],
        compiler_params=pltpu.CompilerParams(vmem_limit_bytes=VMEM_LIMIT),
    )(v)


def _chip_id():
    return 2 * lax.axis_index("x") + lax.axis_index("y")


def _core_id():
    return lax.axis_index("c")


def _cast_into_block(wl, *, name, comm=None):
    r, c_ = wl.shape
    tr = _tile_rows(r, c_, 16)

    def body(w_ref, o_ref):
        o_ref[...] = w_ref[...].astype(BF16)

    in_spec = pl.BlockSpec((tr, c_), lambda i: (i, 0))
    out_spec = pl.BlockSpec((None, tr, c_), lambda i: (_chip_id(), i, 0))
    out_shape = jax.ShapeDtypeStruct((N_CHIPS, r, c_), BF16)
    if comm is not None:
        (out,), landed = _comm_call(body, name=name, grid=(r // tr,), in_specs=[in_spec], out_specs=[out_spec],
                                    out_shape=[out_shape], scratch_shapes=[], args=(wl,), comm=comm)
        return out, landed
    return pl.pallas_call(body, name=name, grid=(r // tr,), in_specs=[in_spec], out_specs=out_spec, out_shape=out_shape,
                          compiler_params=_params(("arbitrary",)))(wl)


def _gather_weight(buf, *, name, part=(0, 1)):
    return _run_comm(_gather_comm(buf, part), name=name)[0]


def _gather_comm(buf, part=(0, 1)):
    _, r, c_ = buf.shape
    half = r // 2
    n_all = _n_chunks(half)
    rows = half // n_all
    first_chunk = round(part[0] * n_all)
    nch = round(part[1] * n_all) - first_chunk

    def plan(out_ref):
        x, y, c = _place()
        me, sibling = (x, y, c), (x, y, 1 - c)
        chips = [(1 - x, y), (x, 1 - y), (1 - x, 1 - y)]
        return me, sibling, chips, c

    def copy(out_ref, sems, k, i, block, to):
        part = out_ref.at[2 * block[0] + block[1], pl.ds(block[2] * half + (first_chunk + i) * rows, rows), :]
        return pltpu.make_async_remote_copy(src_ref=part, dst_ref=part, send_sem=sems[0].at[k * nch + i],
                                            recv_sem=sems[1].at[k * nch + i], device_id=to, device_id_type=MESH)

    def start(cins, couts, sems):
        (out_ref,) = couts
        me, sibling, chips, c = plan(out_ref)
        for i in range(nch):
            for j, chip in enumerate(chips):
                copy(out_ref, sems, j, i, me, (*chip, c)).start()

    def finish(cins, couts, sems):
        (out_ref,) = couts
        me, sibling, chips, c = plan(out_ref)
        passed = []
        for i in range(nch):
            for j, chip in enumerate(chips):
                copy(out_ref, sems, j, i, (*chip, c), me).wait_recv()
                passed.append(copy(out_ref, sems, 3 + j, i, (*chip, c), sibling))
                passed[-1].start()
        for i in range(nch):
            for j, chip in enumerate(chips):
                copy(out_ref, sems, 3 + j, i, (*chip, 1 - c), me).wait_recv()
        for i in range(nch):
            for j, chip in enumerate(chips):
                copy(out_ref, sems, j, i, me, (*chip, c)).wait_send()
        for cp in passed:
            cp.wait_send()

    return dict(inputs=[buf], out_shape=[jax.ShapeDtypeStruct(buf.shape, buf.dtype)], aliases={0: 0},
                scratch=[pltpu.SemaphoreType.DMA((6 * nch,)), pltpu.SemaphoreType.DMA((6 * nch,))],
                start=start, finish=finish)


def _merge_comms(*comms):
    inputs, outs, aliases, scratch, spans = [], [], {}, [], []
    for cm in comms:
        i0, o0, s0 = len(inputs), len(outs), len(scratch)
        inputs += cm["inputs"]
        outs += cm["out_shape"]
        scratch += cm["scratch"]
        aliases.update({i0 + k: o0 + v for k, v in cm["aliases"].items()})
        spans.append((i0, len(inputs), o0, len(outs), s0, len(scratch)))

    def run(which):
        def f(cins, couts, sems):
            for cm, (i0, i1, o0, o1, s0, s1) in zip(comms, spans):
                cm[which](cins[i0:i1], couts[o0:o1], sems[s0:s1])
        return f

    return dict(inputs=inputs, out_shape=outs, aliases=aliases, scratch=scratch, start=run("start"), finish=run("finish"))


def _pair_comm(gfull):
    nblk, r, c_ = gfull.shape
    half = r // 2
    nch = _n_chunks(half)
    rows = half // nch

    def copies(cins, couts, sems):
        g_ref, peer_ref = cins[0], couts[0]
        x, y, c = _place()
        return [pltpu.make_async_remote_copy(
            src_ref=g_ref.at[k, pl.ds((1 - c) * half + i * rows, rows), :], dst_ref=peer_ref.at[k, pl.ds(i * rows, rows), :],
            send_sem=sems[0].at[k * nch + i], recv_sem=sems[1].at[k * nch + i],
            device_id=(x, y, 1 - c), device_id_type=MESH) for i in range(nch) for k in range(nblk)]

    def start(cins, couts, sems):
        for cp in copies(cins, couts, sems):
            cp.start()

    def finish(cins, couts, sems):
        cps = copies(cins, couts, sems)
        for cp in cps:
            cp.wait_recv()
        for cp in cps:
            cp.wait_send()

    return dict(inputs=[gfull], out_shape=[jax.ShapeDtypeStruct((nblk, half, c_), gfull.dtype)], aliases={},
                scratch=[pltpu.SemaphoreType.DMA((nblk * nch,)), pltpu.SemaphoreType.DMA((nblk * nch,))],
                start=start, finish=finish)


def _swap_comm(buf):
    r, c_ = buf.shape
    half = r // 2
    nch = _n_chunks(half)
    rows = half // nch

    def copy(out_ref, sems, i, pc):
        part = out_ref.at[pl.ds(pc * half + i * rows, rows), :]
        x, y, c = _place()
        return pltpu.make_async_remote_copy(src_ref=part, dst_ref=part, send_sem=sems[0].at[i], recv_sem=sems[1].at[i],
                                            device_id=(x, y, 1 - c), device_id_type=MESH)

    def start(cins, couts, sems):
        c = _core_id()
        for i in range(nch):
            copy(couts[0], sems, i, c).start()

    def finish(cins, couts, sems):
        c = _core_id()
        for i in range(nch):
            copy(couts[0], sems, i, 1 - c).wait_recv()
        for i in range(nch):
            copy(couts[0], sems, i, c).wait_send()

    return dict(inputs=[buf], out_shape=[jax.ShapeDtypeStruct(buf.shape, buf.dtype)], aliases={0: 0},
                scratch=[pltpu.SemaphoreType.DMA((nch,)), pltpu.SemaphoreType.DMA((nch,))], start=start, finish=finish)


def _run_comm(cm, *, name):
    nci, nco = len(cm["inputs"]), len(cm["out_shape"])

    def body(*refs):
        cins, couts, sems = refs[:nci], refs[nci:nci + nco], refs[nci + nco:]
        cm["start"](cins, couts, sems)
        cm["finish"](cins, couts, sems)

    return pl.pallas_call(
        body, name=name, out_shape=list(cm["out_shape"]), in_specs=[ANY] * nci, out_specs=[ANY] * nco,
        input_output_aliases=dict(cm["aliases"]), scratch_shapes=cm["scratch"],
    )(*cm["inputs"])


def _pair_add(gfull, peer, *, name):
    nblk, r, c_ = gfull.shape
    half = r // 2
    tr = _pick(half, max(16, (1 << 20) // (2 * c_)), 16)
    per = half // tr

    def body(g_ref, p_ref, o_ref):
        o_ref[...] = (g_ref[...].astype(F32) + p_ref[...].astype(F32)).astype(BF16)

    return pl.pallas_call(
        body, name=name, grid=(nblk, per),
        in_specs=[pl.BlockSpec((None, tr, c_), lambda k, i: (k, _core_id() * per + i, 0)),
                  pl.BlockSpec((None, tr, c_), lambda k, i: (k, i, 0))],
        out_specs=pl.BlockSpec((None, tr, c_), lambda k, i: (k, i, 0)),
        out_shape=jax.ShapeDtypeStruct((nblk, half, c_), BF16),
        compiler_params=_params(("arbitrary", "arbitrary")))(gfull, peer)


def _scatter_comm(pre, part=(0, 1), recv=None):
    _, half, c_ = pre.shape
    n_all = _n_chunks(half)
    rows = half // n_all
    first_chunk = round(part[0] * n_all)
    nch = round(part[1] * n_all) - first_chunk

    def copies(cins, couts, sems):
        p_ref, r_ref = cins[0], couts[0]
        x, y, c = _place()
        out = []
        for i in range(nch):
            at = pl.ds((first_chunk + i) * rows, rows)
            for j, (tx, ty) in reversed(list(enumerate([(1 - x, y), (x, 1 - y), (1 - x, 1 - y)]))):
                out.append(pltpu.make_async_remote_copy(
                    src_ref=p_ref.at[2 * tx + ty, at, :], dst_ref=r_ref.at[j, at, :],
                    send_sem=sems[0].at[j * nch + i], recv_sem=sems[1].at[j * nch + i],
                    device_id=(tx, ty, c), device_id_type=MESH))
        return out

    def start(cins, couts, sems):
        for cp in copies(cins, couts, sems):
            cp.start()

    def finish(cins, couts, sems):
        cps = copies(cins, couts, sems)
        for cp in cps:
            cp.wait_recv()
        for cp in cps:
            cp.wait_send()

    return dict(inputs=[pre] if recv is None else [pre, recv], out_shape=[jax.ShapeDtypeStruct((3, half, c_), pre.dtype)],
                aliases={} if recv is None else {1: 0},
                scratch=[pltpu.SemaphoreType.DMA((3 * nch,)), pltpu.SemaphoreType.DMA((3 * nch,))],
                start=start, finish=finish)


def _sum_into_half(pre, recv, *, name, comm=None):
    _, half, c_ = pre.shape
    n = recv.shape[0]
    tr = _pick(half, max(16, (1 << 19) // (2 * c_)), 16)
    per = half // tr

    def body(g_ref, r_ref, o_ref):
        acc = g_ref[...].astype(F32)
        for k in range(n):
            acc = acc + r_ref[k].astype(F32)
        o_ref[...] = acc

    in_specs = [pl.BlockSpec((None, tr, c_), lambda i: (_chip_id(), i, 0)), pl.BlockSpec((n, tr, c_), lambda i: (0, i, 0))]
    out_spec = pl.BlockSpec((tr, c_), lambda i: (_core_id() * per + i, 0))
    out_shape = jax.ShapeDtypeStruct((2 * half, c_), F32)
    if comm is not None:
        (out,), landed = _comm_call(body, name=name, grid=(per,), in_specs=in_specs, out_specs=[out_spec],
                                    out_shape=[out_shape], scratch_shapes=[], args=(pre, recv), comm=comm)
        return out, landed
    return pl.pallas_call(body, name=name, grid=(per,), in_specs=in_specs, out_specs=out_spec, out_shape=out_shape,
                          compiler_params=_params(("arbitrary",)))(pre, recv)


def _sum_slots(recv, *, name):
    n, r, c_ = recv.shape
    tr = _pick(r, max(16, (1 << 19) // (2 * c_)), 16)

    def body(r_ref, o_ref):
        acc = r_ref[0].astype(F32)
        for k in range(1, n):
            acc = acc + r_ref[k].astype(F32)
        o_ref[...] = acc

    return _rowcall(body, name, r, tr, [pl.BlockSpec((n, tr, c_), lambda i: (0, i, 0))], _row(tr, c_),
                    jax.ShapeDtypeStruct((r, c_), F32), (recv,))


def _prereduce(gfull, *, name):
    return _pair_add(gfull, _run_comm(_pair_comm(gfull), name=name + "_pair")[0], name=name + "_padd")


PACK_ROWS = 16


def _pack(parts):
    flat = [p.reshape(-1).astype(F32) for p in parts]
    n = sum(f.shape[0] for f in flat)
    unit = PACK_ROWS * LANES
    total = -(-n // unit) * unit
    if total > n:
        flat.append(jnp.zeros((total - n,), F32))
    where, off = [], 0
    for p in parts:
        where.append((off, p.shape))
        off += p.size
    return jnp.concatenate(flat).reshape(total // LANES, LANES), where


def _unpack(flat, where):
    v = flat.reshape(-1)
    return [v[off:off + _size(shape)].reshape(shape) for off, shape in where]


def _size(shape):
    n = 1
    for d in shape:
        n *= d
    return n


def _sample_step(x, target, mods, h1, w_in_p, bufs, sp):
    s, d = x.shape
    u = d // 2
    h = u // HEAD_DIM
    g = h // HEADS_PER_GROUP
    pw = w_in_p.shape[1]
    din = 6 * u + 2 * h
    dff_ = bufs["w_up"].shape[2] * N_CHIPS
    shift1, scale1, gate1, shift2, scale2, gate2 = mods

    a_f = -jnp.exp(sp["ssm_a_log_f"].reshape(-1))
    a_b = -jnp.exp(sp["ssm_a_log_b"].reshape(-1))
    pad_l = LANES - 2 * h
    a_row = jnp.pad(jnp.concatenate([a_f, a_b]), (0, pad_l)).reshape(1, LANES)
    bias_row = jnp.pad(jnp.concatenate([sp["ssm_dt_bias_f"].reshape(-1), sp["ssm_dt_bias_b"].reshape(-1)]),
                       (0, pad_l)).reshape(1, LANES)
    a_col = jnp.broadcast_to(jnp.stack([a_f, a_b])[:, :, None], (2, h, LANES))
    d_e = jnp.repeat(sp["ssm_d"].reshape(-1), HEAD_DIM).reshape(1, u)
    conv_w, conv_b = sp["ssm_conv_w"], sp["ssm_conv_b"].reshape(1, 2 * u)
    sc_conv_w = sp["sc_conv_w"]
    ssm_norm_w, sc_norm_w = sp["ssm_norm_w"].reshape(1, u), sp["sc_norm_w"].reshape(1, u)
    ln1_g, ln1_b = sp["ln1_g"].reshape(1, d), sp["ln1_b"].reshape(1, d)
    ln2_g, ln2_b = sp["ln2_g"].reshape(1, d), sp["ln2_b"].reshape(1, d)

    e = 1.0 / DMA_CHUNKS
    proj, (w_up_b, w_out_blk) = _matmul(
        h1, w_in_p, name="mm_proj", tn=1280,
        comm=_merge_comms(_gather_comm(bufs["w_up"], (0, 3 * e)), _gather_comm(bufs["w_out"])))
    w_out = w_out_blk.reshape(d, d)
    xbc, (w_down_b,) = _conv_silu_fwd(proj, conv_w, conv_b, s, u, comm=_gather_comm(bufs["w_down"], (0, e)))
    (dt, cum, dt_e, cum_e), (w_up_b,) = _dt_prep(proj, bias_row, a_row, s, u, h, comm=_gather_comm(w_up_b, (3 * e, 4 * e)))
    cum_t = jnp.stack([cum[:, :h].T, cum[:, h:2 * h].T])
    dt_t = jnp.stack([dt[:, :h].T, dt[:, h:2 * h].T])
    (y2, states), (w_up_b,) = _ssd_fwd(xbc, dt_e, cum_e, cum_t, s, h, g, comm=_gather_comm(w_up_b, (4 * e, 6 * e)))
    y_ssm = _gate_norm_fwd(y2, xbc, proj, d_e, ssm_norm_w, s, u, g)
    y_sc = _shortconv_fwd(proj, sc_conv_w, sc_norm_w, s, u)
    ymix = jnp.concatenate([y_ssm, y_sc], axis=1)
    mix, (w_up_b,) = _matmul(ymix, w_out, name="mm_mix", comm=_gather_comm(w_up_b, (6 * e, 7 * e)))
    (x1, h2), (w_up_blk,) = _ln1_fwd(x, mix, gate1, ln1_g, ln1_b, scale2, shift2, comm=_gather_comm(w_up_b, (7 * e, 1)))
    (up, ff), (w_down_blk,) = _matmul(h2, w_up_blk, name="mm_up", b_blocks=N_CHIPS, epilogue="relu2",
                                      comm=_gather_comm(w_down_b, (e, 1)))
    w_down = w_down_blk.reshape(dff_, d)
    f = _matmul(ff, w_down, name="mm_down")
    df, dr2, loss, dg2, db2, dgate2 = _ln2_loss_bwd(x1, f, target, gate2, ln2_g, ln2_b)

    gw_down = _matmul(ff, df, name="mm_gw_down", ta=True, out_dtype=BF16).reshape(N_CHIPS, dff_ // N_CHIPS, d)
    du, (peer,) = _matmul(df, w_down, name="mm_dff", tb=True, out_dtype=BF16, epilogue="relu2_bwd", extra=up,
                          comm=_pair_comm(gw_down))
    pre_down = _pair_add(gw_down, peer, name="rs_w_down_padd")
    gw_up, (rv_down,) = _matmul(h2, du, name="mm_gw_up", ta=True, out_dtype=BF16, out_blocks=N_CHIPS,
                                comm=_scatter_comm(pre_down, (0, 0.5)))
    dh2, (rv_down, peer) = _matmul(du, w_up_blk, name="mm_dh2", tb=True, b_blocks=N_CHIPS,
                                   comm=_merge_comms(_scatter_comm(pre_down, (0.5, 1), recv=rv_down), _pair_comm(gw_up)))
    pre_up = _pair_add(gw_up, peer, name="rs_w_up_padd")
    (dmix, dxa, dscale2, dshift2, dg1, db1, dgate1), (rv_up,) = _ln1_bwd(
        dh2, dr2, x1, x, mix, scale2, gate1, ln1_g, comm=_scatter_comm(pre_up, (0, e)))
    gw_out, (rv_up,) = _matmul(ymix, dmix, name="mm_gw_out", ta=True, out_dtype=BF16,
                               comm=_scatter_comm(pre_up, (e, 2 * e), recv=rv_up))
    gw_out = gw_out.reshape(N_CHIPS, d // N_CHIPS, d)
    dymix, (rv_up, peer) = _matmul(dmix, w_out, name="mm_dymix", tb=True,
                                   comm=_merge_comms(_scatter_comm(pre_up, (2 * e, 4 * e), recv=rv_up), _pair_comm(gw_out)))
    pre_out = _pair_add(gw_out, peer, name="rs_w_out_padd")
    dys, dproj, dxs, dnw, dd_e = _gate_norm_bwd(dymix, y2, xbc, proj, d_e, ssm_norm_w, pw, s, u, g)
    (dx2, dbb, dcc, ddt_t, da), (rv_up,) = _ssd_bwd(xbc, dt_e, cum_e, cum_t, dt_t, a_col, dys, states, s, h, g,
                                                    comm=_scatter_comm(pre_up, (4 * e, 1), recv=rv_up))
    dxbc = jnp.concatenate([dx2[0] + dx2[1] + dxs, dbb[0] + dbb[1], dcc[0] + dcc[1]], axis=1)
    (dproj, dcw, dcb), (rv_out,) = _conv_silu_bwd(proj, dxbc, conv_w, conv_b, dproj, s, u, comm=_scatter_comm(pre_out))
    ddt = jnp.pad(jnp.concatenate([ddt_t[0].T, ddt_t[1].T], axis=1), ((0, 0), (0, pad_l)))
    dproj, dbias = _dt_bwd(ddt, proj, bias_row, dproj, s, u, h)
    dproj, dscw, dscnw = _shortconv_bwd(dymix, proj, sc_conv_w, sc_norm_w, dproj, s, u)
    gp = _matmul(h1, dproj, name="mm_gw_in", ta=True, out_dtype=BF16, tn=1280)
    pre_in = _prereduce(_from_p_layout(gp, u, h, N_CHIPS), name="rs_w_in")
    dh1, (rv_in,) = _matmul(dproj, w_in_p, name="mm_dh1", tb=True, tk=2560, comm=_scatter_comm(pre_in))
    grad_x, dscale1, dshift1 = _dx_final(dxa, dh1, x, scale1)
    big = {}
    half_down = _sum_into_half(pre_down, rv_down, name="rs_w_down_sum")
    half_up, (big["w_down"],) = _sum_into_half(pre_up, rv_up, name="rs_w_up_sum", comm=_swap_comm(half_down))
    half_out, (big["w_up"],) = _sum_into_half(pre_out, rv_out, name="rs_w_out_sum", comm=_swap_comm(half_up))
    half_in, (big["w_out"],) = _sum_into_half(pre_in, rv_in, name="rs_w_in_sum", comm=_swap_comm(half_out))
    big["w_in"] = _run_comm(_swap_comm(half_in), name="rs_w_in_swap")[0]

    small = {
        "dmod": jnp.concatenate([dshift1, dscale1, dgate1, dshift2, dscale2, dgate2], axis=1),
        "ssm_conv_b": dcb,
        "ssm_dt_bias_f": dbias[0, :h], "ssm_dt_bias_b": dbias[0, h:2 * h],
        "ssm_a_log_f": jnp.sum(da[0], axis=1) * a_f, "ssm_a_log_b": jnp.sum(da[1], axis=1) * a_b,
        "ssm_d": jnp.sum(dd_e.reshape(h, HEAD_DIM), axis=1),
        "ssm_norm_w": dnw, "sc_norm_w": dscnw,
        "ln1_g": dg1, "ln1_b": db1, "ln2_g": dg2, "ln2_b": db2,
        "ssm_conv_w": dcw[:SSM_CONV], "sc_conv_w": dscw[:SC_CONV],
    }
    return loss, grad_x, big, small


WEIGHTS = ['w_ada', 'b_ada', 'w_in', 'ssm_conv_w', 'ssm_conv_b', 'ssm_dt_bias_f', 'ssm_dt_bias_b', 'ssm_a_log_f',
           'ssm_a_log_b', 'ssm_d', 'ssm_norm_w', 'sc_conv_w', 'sc_norm_w', 'w_out', 'ln1_g', 'ln1_b', 'w_up', 'w_down',
           'ln2_g', 'ln2_b']
BIG = ('w_ada', 'w_in', 'w_out', 'w_up', 'w_down')
SMALL = tuple(n for n in WEIGHTS if n not in BIG)
SMALL_SHARDED = ('ssm_conv_w', 'sc_conv_w')


def _p_layout_width(u):
    return -(-(6 * u + LANES) // 512) * 512


def _p_segments(u, h):
    return [((0, u), P_Z * u), ((u, 3 * u), P_X * u), ((3 * u, 3 * u + 2 * h), 6 * u),
            ((3 * u + 2 * h, 6 * u + 2 * h), P_H * u)]


def _to_p_layout(blocks, u, h, pw):
    nblk, d, w = blocks.shape
    parts = []
    for (lo, hi), _ in sorted(_p_segments(u, h), key=lambda t: t[1]):
        for k in range(nblk):
            a, b = max(lo, k * w), min(hi, (k + 1) * w)
            if a < b:
                parts.append(blocks[k][:, a - k * w:b - k * w])
    parts.append(jnp.zeros((d, pw - nblk * w), blocks.dtype))
    return jnp.concatenate(parts, axis=1)


def _from_p_layout(gp, u, h, nblk):
    w = (6 * u + 2 * h) // nblk
    blocks = []
    for k in range(nblk):
        parts = []
        for (lo, hi), poff in _p_segments(u, h):
            a, b = max(lo, k * w), min(hi, (k + 1) * w)
            if a < b:
                parts.append(gp[:, poff + a - lo:poff + b - lo])
        blocks.append(jnp.concatenate(parts, axis=1))
    return jnp.stack(blocks)


def kernel(x, c, w_ada, b_ada, w_in, ssm_conv_w, ssm_conv_b, ssm_dt_bias_f, ssm_dt_bias_b, ssm_a_log_f, ssm_a_log_b, ssm_d, ssm_norm_w, sc_conv_w, sc_norm_w, w_out, ln1_g, ln1_b, w_up, w_down, ln2_g, ln2_b, loss_target, m_w_ada, m_b_ada, m_w_in, m_ssm_conv_w, m_ssm_conv_b, m_ssm_dt_bias_f, m_ssm_dt_bias_b, m_ssm_a_log_f, m_ssm_a_log_b, m_ssm_d, m_ssm_norm_w, m_sc_conv_w, m_sc_norm_w, m_w_out, m_ln1_g, m_ln1_b, m_w_up, m_w_down, m_ln2_g, m_ln2_b, v_w_ada, v_b_ada, v_w_in, v_ssm_conv_w, v_ssm_conv_b, v_ssm_dt_bias_f, v_ssm_dt_bias_b, v_ssm_a_log_f, v_ssm_a_log_b, v_ssm_d, v_ssm_norm_w, v_sc_conv_w, v_sc_norm_w, v_w_out, v_ln1_g, v_ln1_b, v_w_up, v_w_down, v_ln2_g, v_ln2_b):
    given = dict(locals())
    w = {n: given[n][0] for n in WEIGHTS}
    m = {n: given["m_" + n][0] for n in WEIGHTS}
    v = {n: given["v_" + n][0] for n in WEIGHTS}
    xs, tgt = x[0], loss_target[0]
    s, d = xs.shape
    u = d // 2
    h = u // HEAD_DIM
    nmod = N_MOD * d
    nmod_loc = nmod // N_CHIPS
    ax, ay, ac = lax.axis_index("x"), lax.axis_index("y"), lax.axis_index("c")
    chip = 2 * ax + ay
    me = 2 * chip + ac

    pay1, where1 = _pack([c[0], w["ssm_conv_w"], w["sc_conv_w"]])
    g1 = _allgather_small(pay1, name="ag_inputs").reshape(N_DEV, -1)
    per_dev = [_unpack(g1[k], where1) for k in range(N_DEV)]
    c_all = jnp.stack([p[0] for p in per_dev])
    ssm_conv_w_full = jnp.concatenate([per_dev[2 * k][1] for k in range(N_CHIPS)], axis=1)
    sc_conv_w_full = jnp.concatenate([per_dev[2 * k][2] for k in range(N_CHIPS)], axis=1)

    sc_all = _silu(c_all)
    sc16 = jnp.pad(sc_all, ((0, 16 - N_DEV), (0, 0)))
    b_loc = lax.dynamic_slice(w["b_ada"], (chip * nmod_loc,), (nmod_loc,))
    mod_loc = _matmul(sc16, w["w_ada"], name="mm_mod")[:N_DEV] + b_loc[None, :]
    pay2, where2 = _pack([mod_loc])
    g2 = _allgather_small(pay2, name="ag_mod").reshape(N_DEV, -1)
    mod_blocks = jnp.stack([_unpack(g2[2 * k], where2)[0] for k in range(N_CHIPS)])
    mod_mine = lax.dynamic_index_in_dim(mod_blocks, me, axis=1, keepdims=False).reshape(N_MOD, 1, d)
    mods = [mod_mine[k] for k in range(N_MOD)]

    din = w["w_in"].shape[1] * N_CHIPS

    e = 1.0 / DMA_CHUNKS
    g_in = _cast_into_block(w["w_in"], name="cast_w_in")
    bufs = {"w_out": _cast_into_block(w["w_out"], name="cast_w_out")}
    bufs["w_up"], (g_in,) = _cast_into_block(w["w_up"], name="cast_w_up", comm=_gather_comm(g_in, (0, e)))
    bufs["w_down"], (g_in,) = _cast_into_block(w["w_down"], name="cast_w_down", comm=_gather_comm(g_in, (e, 2 * e)))
    h1, (g_in,) = _modulate(xs, mods[1], mods[0], name="modulate1", comm=_gather_comm(g_in, (2 * e, 3 * e)))
    g_in = _gather_weight(g_in, name="gather_w_in", part=(3 * e, 1))
    w_in_p = _to_p_layout(g_in, u, h, _p_layout_width(u))

    sp = {n: w[n] for n in SMALL}
    sp["ssm_conv_w"], sp["sc_conv_w"] = ssm_conv_w_full, sc_conv_w_full
    loss_loc, grad_x, big, small = _sample_step(xs, tgt, mods, h1, w_in_p, bufs, sp)

    small_names = ["dmod"] + [n for n in SMALL if n != "b_ada"]
    pay3, where3 = _pack([loss_loc] + [small[n] for n in small_names])
    g3 = _allgather_small(pay3, name="ag_small_grads")
    tot = _unpack(_sum_slots(g3.reshape(N_DEV, -1, LANES), name="sum_small_grads"), where3)
    loss = tot[0].reshape(())
    gsum = dict(zip(small_names, tot[1:]))
    dmod_all = jnp.stack([_unpack(g3.reshape(N_DEV, -1)[k], where3)[1].reshape(-1) for k in range(N_DEV)])

    grads = {}
    grads["b_ada"] = gsum["dmod"].reshape(-1)
    for n in SMALL:
        if n in SMALL_SHARDED:
            loc = w[n].shape[1]
            grads[n] = lax.dynamic_slice_in_dim(gsum[n], chip * loc, loc, axis=1)
        elif n != "b_ada":
            grads[n] = gsum[n].reshape(w[n].shape)

    delta, new_m, new_v = {}, {}, {}
    dm_loc = lax.dynamic_slice_in_dim(dmod_all, chip * nmod_loc, nmod_loc, axis=1)
    grads["w_ada"], delta["w_ada"], new_m["w_ada"], new_v["w_ada"] = _adam_outer(
        w["w_ada"], sc16.T, jnp.pad(dm_loc, ((0, 16 - N_DEV), (0, 0))), m["w_ada"], v["w_ada"], name="adam_w_ada")
    for n in ("w_in", "w_out", "w_up", "w_down"):
        grads[n], delta[n], new_m[n], new_v[n] = _adam(w[n], big[n], m[n], v[n], name="adam_" + n, emit_grad=True)
    pw_, where_s = _pack([w[n] for n in SMALL])
    pg_, _ = _pack([grads[n] for n in SMALL])
    pm_, _ = _pack([m[n] for n in SMALL])
    pv_, _ = _pack([v[n] for n in SMALL])
    sd, sm, sv = _adam(pw_, pg_, pm_, pv_, name="adam_small")
    for n, a, b_, c_ in zip(SMALL, _unpack(sd, where_s), _unpack(sm, where_s), _unpack(sv, where_s)):
        delta[n], new_m[n], new_v[n] = a, b_, c_

    def lead(t):
        return t[None]

    return (loss, grad_x[None], *[lead(grads[n].reshape(w[n].shape)) for n in WEIGHTS],
            *[lead(delta[n]) for n in WEIGHTS], *[lead(new_m[n]) for n in WEIGHTS], *[lead(new_v[n]) for n in WEIGHTS])
```

```python
import functools

import jax
import jax.numpy as jnp
from jax import lax
from jax.experimental import pallas as pl
from jax.experimental.pallas import tpu as pltpu

F32 = jnp.float32
BF16 = jnp.bfloat16

CHUNK = 128
HEAD_DIM = 64
STATE = 128
HEADS_PER_GROUP = 4
SC_GROUP_WIDTH = 128
SSM_CONV = 5
SC_CONV = 3
N_MOD = 6
DEEPNORM_ALPHA = 2.0 ** 0.25
LN_EPS = 1e-5
RMS_EPS = 1e-5
ADAM_LR = 0.001
ADAM_B1 = 0.9
ADAM_B2 = 0.999
ADAM_EPS = 1e-08
ADAM_WD = 0.01
ADAM_STEP = 10
N_CHIPS = 4
N_DEV = 8
LANES = 128
SUBLANES = 8
HALO = 8
VMEM_LIMIT = 56 * 1024 * 1024
MESH = pl.DeviceIdType.MESH


def _params(sem=None):
    return pltpu.CompilerParams(dimension_semantics=sem, vmem_limit_bytes=VMEM_LIMIT)


def _pick(n, target, mult=LANES):
    best = None
    t = mult
    while t <= min(n, target):
        if n % t == 0:
            best = t
        t += mult
    return best if best is not None else n


ROW_TILE_BYTES = 2 << 20


def _tile_rows(s, width, mult=SUBLANES):
    return _pick(s, max(mult, ROW_TILE_BYTES // (4 * width)), mult)


def _sigmoid(v):
    return 1.0 / (1.0 + jnp.exp(-v))


def _silu(v):
    return v * _sigmoid(v)


def _dsilu(v):
    s = _sigmoid(v)
    return s * (1.0 + v * (1.0 - s))


def _softplus(v):
    e = jnp.exp(-jnp.abs(v))
    return jnp.maximum(v, 0.0) + jnp.where(e < 1e-4, e - 0.5 * e * e, jnp.log(1.0 + e))


def _dot(a, b):
    return jnp.dot(a, b, preferred_element_type=F32)


def _dot_nt(a, b):
    return lax.dot_general(a, b, (((1,), (1,)), ((), ())), preferred_element_type=F32)


def _dot_tn(a, b):
    return lax.dot_general(a, b, (((0,), (0,)), ((), ())), preferred_element_type=F32)


def _split3(v):
    hi = v.astype(BF16)
    r1 = v - hi.astype(F32)
    mid = r1.astype(BF16)
    lo = (r1 - mid.astype(F32)).astype(BF16)
    return hi, mid, lo


def _dot3_r(v, onehot):
    hi, mid, lo = _split3(v)
    return _dot(hi, onehot) + _dot(mid, onehot) + _dot(lo, onehot)


def _dot3_l(onehot, v):
    hi, mid, lo = _split3(v)
    return _dot(onehot, hi) + _dot(onehot, mid) + _dot(onehot, lo)


MATMUL_VMEM_BUDGET = 44 * 1024 * 1024


def _matmul(a, b, *, name, ta=False, tb=False, out_dtype=F32, b_blocks=1, out_blocks=1,
            tm=1024, tn=1024, tk=4096, comm=None, epilogue=None, extra=None):
    if ta:
        K, M = a.shape
    else:
        M, K = a.shape
    if b_blocks > 1:
        nb, r_, c_ = b.shape
        if tb:
            N, K2 = r_, c_ * nb
        else:
            K2, N = r_, c_ * nb
    else:
        if tb:
            N, K2 = b.shape
        else:
            K2, N = b.shape
    assert K == K2, (a.shape, b.shape, ta, tb)
    assert not (ta and tb)
    n_unit = N // b_blocks if (b_blocks > 1 and not tb) else N
    n_unit = min(n_unit, N // out_blocks)
    tn = _pick(n_unit, tn)
    k_unit = K // b_blocks if (b_blocks > 1 and tb) else K
    tk = _pick(k_unit, tk)
    tile_bytes = {None: jnp.dtype(out_dtype).itemsize, "relu2": 6, "relu2_bwd": 6}[epilogue]

    def vmem_need(tm_):
        need = 2 * (tm_ * tk * a.dtype.itemsize + tk * tn * b.dtype.itemsize) + 2 * tm_ * tn * tile_bytes
        return need + (tm_ * tn * 4 if K > tk else 0)

    tm = _pick(M, tm)
    while vmem_need(tm) > MATMUL_VMEM_BUDGET and tm % 2 == 0 and tm // 2 >= LANES:
        tm //= 2
    gm, gn, gk = M // tm, N // tn, K // tk

    if ta:
        a_spec = pl.BlockSpec((tk, tm), lambda i, j, k: (k, i))
    else:
        a_spec = pl.BlockSpec((tm, tk), lambda i, j, k: (i, k))
    if b_blocks > 1 and not tb:
        per = (N // b_blocks) // tn
        b_spec = pl.BlockSpec((None, tk, tn), lambda i, j, k: (j // per, k, j % per))
    elif b_blocks > 1 and tb:
        per = (K // b_blocks) // tk
        b_spec = pl.BlockSpec((None, tn, tk), lambda i, j, k: (k // per, j, k % per))
    elif tb:
        b_spec = pl.BlockSpec((tn, tk), lambda i, j, k: (j, k))
    else:
        b_spec = pl.BlockSpec((tk, tn), lambda i, j, k: (k, j))
    if out_blocks > 1:
        per_o = (N // out_blocks) // tn
        o_spec = pl.BlockSpec((None, tm, tn), lambda i, j, k: (j // per_o, i, j % per_o))
        o_shape = jax.ShapeDtypeStruct((out_blocks, M, N // out_blocks), out_dtype)
    else:
        o_spec = pl.BlockSpec((tm, tn), lambda i, j, k: (i, j))
        o_shape = jax.ShapeDtypeStruct((M, N), out_dtype)

    in_specs, args = [a_spec, b_spec], [a, b]
    out_specs, out_shape = [o_spec], [o_shape]
    if epilogue == "relu2":
        assert out_blocks == 1 and out_dtype == F32
        out_specs.append(o_spec)
        out_shape.append(jax.ShapeDtypeStruct((M, N), BF16))
    elif epilogue == "relu2_bwd":
        assert out_blocks == 1 and out_dtype == BF16
        in_specs.append(o_spec)
        args.append(extra)
    n_in = len(in_specs)

    def write(refs, p):
        o_ref = refs[n_in]
        if epilogue == "relu2":
            o_ref[...] = p
            r = jnp.maximum(p, 0.0)
            refs[n_in + 1][...] = (r * r).astype(BF16)
        elif epilogue == "relu2_bwd":
            o_ref[...] = (p * 2.0 * jnp.maximum(refs[2][...], 0.0)).astype(BF16)
        else:
            o_ref[...] = p.astype(out_dtype)

    def body(*refs):
        av = refs[0][...].astype(BF16)
        bv = refs[1][...].astype(BF16)
        p = _dot_tn(av, bv) if ta else (_dot_nt(av, bv) if tb else _dot(av, bv))
        if gk == 1:
            write(refs, p)
            return
        acc = refs[-1]
        k = pl.program_id(2)

        @pl.when(k == 0)
        def _():
            acc[...] = p

        @pl.when(jnp.logical_and(k > 0, k < gk - 1))
        def _():
            acc[...] += p

        @pl.when(k == gk - 1)
        def _():
            write(refs, acc[...] + p)

    scratch = [pltpu.VMEM((tm, tn), F32)] if gk > 1 else []
    if comm is not None:
        outs, landed = _comm_call(body, name=name, grid=(gm, gn, gk), in_specs=in_specs, out_specs=out_specs,
                                  out_shape=out_shape, scratch_shapes=scratch, args=args, comm=comm)
        return (outs[0] if len(outs) == 1 else tuple(outs)), landed
    outs = pl.pallas_call(
        body, name=name, grid=(gm, gn, gk), in_specs=in_specs, out_specs=out_specs,
        out_shape=out_shape, scratch_shapes=scratch,
        compiler_params=_params(("parallel", "parallel", "arbitrary")),
    )(*args)
    return outs[0] if len(outs) == 1 else tuple(outs)


def _comm_call(body, *, name, grid, in_specs, out_specs, out_shape, scratch_shapes, args, comm, aliases=None):
    n_in, n_out, n_scr = len(in_specs), len(out_shape), len(scratch_shapes)
    c_in, c_out = list(comm["inputs"]), list(comm["out_shape"])
    nci, nco = len(c_in), len(c_out)
    hbm = pl.BlockSpec(memory_space=pl.ANY)

    def body2(*refs):
        ins, cins = refs[:n_in], refs[n_in:n_in + nci]
        o0 = n_in + nci
        outs, couts = refs[o0:o0 + n_out], refs[o0 + n_out:o0 + n_out + nco]
        s0 = o0 + n_out + nco
        scr, cscr = refs[s0:s0 + n_scr], refs[s0 + n_scr:]
        first = functools.reduce(jnp.logical_and, [pl.program_id(a) == 0 for a in range(len(grid))])
        last = functools.reduce(jnp.logical_and, [pl.program_id(a) == grid[a] - 1 for a in range(len(grid))])

        @pl.when(first)
        def _():
            comm["start"](cins, couts, cscr)

        body(*ins, *outs, *scr)

        @pl.when(last)
        def _():
            comm["finish"](cins, couts, cscr)

    res = pl.pallas_call(
        body2, name=name, grid=grid, in_specs=list(in_specs) + [hbm] * nci, out_specs=list(out_specs) + [hbm] * nco,
        out_shape=list(out_shape) + c_out, scratch_shapes=list(scratch_shapes) + list(comm["scratch"]),
        input_output_aliases={**(aliases or {}), **{n_in + k: n_out + v for k, v in comm.get("aliases", {}).items()}},
        compiler_params=_params(("arbitrary",) * len(grid)),
    )(*args, *c_in)
    return res[:n_out], res[n_out:]


def _row(tr, w, blk=0):
    return pl.BlockSpec((tr, w), lambda i: (i, blk))


def _full(shape):
    nd = len(shape)
    return pl.BlockSpec(shape, lambda i: (0,) * nd)


def _halo_specs(s, tr, w, blk=0):
    per = tr // HALO
    last = s // HALO - 1
    return [
        pl.BlockSpec((HALO, w), lambda i: (jnp.maximum(i * per - 1, 0), blk)),
        pl.BlockSpec((tr, w), lambda i: (i, blk)),
        pl.BlockSpec((HALO, w), lambda i: (jnp.minimum((i + 1) * per, last), blk)),
    ]


def _ext(prev_ref, cur_ref, next_ref, s, tr):
    i = pl.program_id(0)
    e = jnp.concatenate([prev_ref[...].astype(F32), cur_ref[...].astype(F32), next_ref[...].astype(F32)], axis=0)
    rid = i * tr - HALO + lax.broadcasted_iota(jnp.int32, e.shape, 0)
    return jnp.where((rid >= 0) & (rid < s), e, 0.0)


def _valid_rows(shape, s, tr):
    i = pl.program_id(0)
    rid = i * tr - HALO + lax.broadcasted_iota(jnp.int32, shape, 0)
    return (rid >= 0) & (rid < s)


def _shift(e, k):
    if k == 0:
        return e
    n = e.shape[0]
    return pltpu.roll(e, (n - k) % n, 0)


def _acc_rows(ref, v):
    s = jnp.sum(v, axis=0, keepdims=True)

    @pl.when(pl.program_id(0) == 0)
    def _():
        ref[...] = s

    @pl.when(pl.program_id(0) > 0)
    def _():
        ref[...] += s


def _rowcall(body, name, s, tr, in_specs, out_specs, out_shape, args, comm=None, aliases=None):
    aliases = aliases or {}
    if comm is not None:
        single = not isinstance(out_shape, (list, tuple))
        outs, landed = _comm_call(body, name=name, grid=(s // tr,), in_specs=in_specs,
                                  out_specs=[out_specs] if single else out_specs,
                                  out_shape=[out_shape] if single else out_shape, scratch_shapes=[], args=args, comm=comm,
                                  aliases=aliases)
        return (outs[0] if single else outs), landed
    return pl.pallas_call(
        body, name=name, grid=(s // tr,), in_specs=in_specs, out_specs=out_specs, out_shape=out_shape,
        input_output_aliases=aliases, compiler_params=_params(("arbitrary",)),
    )(*args)


def _modulate(x, scale, shift, *, name, comm=None):
    s, d = x.shape
    tr = _tile_rows(s, d)

    def body(x_ref, sc_ref, sh_ref, o_ref):
        o_ref[...] = (x_ref[...] * (1.0 + sc_ref[...]) + sh_ref[...]).astype(BF16)

    return _rowcall(body, name, s, tr, [_row(tr, d), _full((1, d)), _full((1, d))], _row(tr, d),
                    jax.ShapeDtypeStruct((s, d), BF16), (x, scale, shift), comm=comm)


def _ln_stats(r):
    mu = jnp.mean(r, axis=-1, keepdims=True)
    xc = r - mu
    var = jnp.mean(xc * xc, axis=-1, keepdims=True)
    rstd = lax.rsqrt(var + LN_EPS)
    return xc * rstd, rstd


def _ln1_fwd(x, mix, gate, g, b, scale2, shift2, comm=None):
    s, d = x.shape
    tr = _tile_rows(s, d)

    def body(x_ref, m_ref, gt_ref, g_ref, b_ref, sc_ref, sh_ref, x1_ref, h2_ref):
        r = DEEPNORM_ALPHA * x_ref[...] + (1.0 + gt_ref[...]) * m_ref[...]
        xh, _ = _ln_stats(r)
        x1 = xh * g_ref[...] + b_ref[...]
        x1_ref[...] = x1
        h2_ref[...] = (x1 * (1.0 + sc_ref[...]) + sh_ref[...]).astype(BF16)

    v = _full((1, d))
    return _rowcall(body, "ln1_fwd", s, tr, [_row(tr, d), _row(tr, d), v, v, v, v, v],
                    [_row(tr, d), _row(tr, d)],
                    [jax.ShapeDtypeStruct((s, d), F32), jax.ShapeDtypeStruct((s, d), BF16)],
                    (x, mix, gate, g, b, scale2, shift2), comm=comm)


def _ln2_loss_bwd(x1, f, target, gate, g, b):
    s, d = x1.shape
    tr = _tile_rows(s, d)

    def body(x1_ref, f_ref, t_ref, gt_ref, g_ref, b_ref, df_ref, dr_ref, loss_ref, dg_ref, db_ref, dgt_ref):
        fv = f_ref[...]
        r = DEEPNORM_ALPHA * x1_ref[...] + (1.0 + gt_ref[...]) * fv
        xh, rstd = _ln_stats(r)
        y = xh * g_ref[...] + b_ref[...]
        err = y - t_ref[...]
        _acc_rows(loss_ref, 0.5 * jnp.mean(err * err, axis=-1, keepdims=True))
        dy = err * (1.0 / d)
        _acc_rows(dg_ref, dy * xh)
        _acc_rows(db_ref, dy)
        dxh = dy * g_ref[...]
        dr = rstd * (dxh - jnp.mean(dxh, axis=-1, keepdims=True) - xh * jnp.mean(dxh * xh, axis=-1, keepdims=True))
        dr_ref[...] = dr
        df_ref[...] = ((1.0 + gt_ref[...]) * dr).astype(BF16)
        _acc_rows(dgt_ref, dr * fv)

    v = _full((1, d))
    one = _full((1, 1))
    return _rowcall(body, "ln2_loss_bwd", s, tr, [_row(tr, d), _row(tr, d), _row(tr, d), v, v, v],
                    [_row(tr, d), _row(tr, d), one, v, v, v],
                    [jax.ShapeDtypeStruct((s, d), BF16), jax.ShapeDtypeStruct((s, d), F32),
                     jax.ShapeDtypeStruct((1, 1), F32)] + [jax.ShapeDtypeStruct((1, d), F32)] * 3,
                    (x1, f, target, gate, g, b))


def _ln1_bwd(dh2, dr2, x1, x, mix, scale2, gate1, g1, comm=None):
    s, d = x.shape
    tr = _tile_rows(s, d)

    def body(dh_ref, dr2_ref, x1_ref, x_ref, m_ref, sc_ref, gt_ref, g_ref,
             dm_ref, dxa_ref, dsc_ref, dsh_ref, dg_ref, db_ref, dgt_ref):
        dh = dh_ref[...]
        _acc_rows(dsc_ref, dh * x1_ref[...])
        _acc_rows(dsh_ref, dh)
        dy = dh * (1.0 + sc_ref[...]) + DEEPNORM_ALPHA * dr2_ref[...]
        mv = m_ref[...]
        r = DEEPNORM_ALPHA * x_ref[...] + (1.0 + gt_ref[...]) * mv
        xh, rstd = _ln_stats(r)
        _acc_rows(dg_ref, dy * xh)
        _acc_rows(db_ref, dy)
        dxh = dy * g_ref[...]
        dr = rstd * (dxh - jnp.mean(dxh, axis=-1, keepdims=True) - xh * jnp.mean(dxh * xh, axis=-1, keepdims=True))
        dm_ref[...] = ((1.0 + gt_ref[...]) * dr).astype(BF16)
        dxa_ref[...] = DEEPNORM_ALPHA * dr
        _acc_rows(dgt_ref, dr * mv)

    v = _full((1, d))
    return _rowcall(body, "ln1_bwd", s, tr, [_row(tr, d)] * 5 + [v, v, v],
                    [_row(tr, d), _row(tr, d), v, v, v, v, v],
                    [jax.ShapeDtypeStruct((s, d), BF16), jax.ShapeDtypeStruct((s, d), F32)]
                    + [jax.ShapeDtypeStruct((1, d), F32)] * 5,
                    (dh2, dr2, x1, x, mix, scale2, gate1, g1), comm=comm)


def _dx_final(dxa, dh1, x, scale1):
    s, d = x.shape
    tr = _tile_rows(s, d)

    def body(a_ref, dh_ref, x_ref, sc_ref, o_ref, dsc_ref, dsh_ref):
        dh = dh_ref[...]
        o_ref[...] = a_ref[...] + dh * (1.0 + sc_ref[...])
        _acc_rows(dsc_ref, dh * x_ref[...])
        _acc_rows(dsh_ref, dh)

    v = _full((1, d))
    return _rowcall(body, "dx_final", s, tr, [_row(tr, d)] * 3 + [v], [_row(tr, d), v, v],
                    [jax.ShapeDtypeStruct((s, d), F32)] + [jax.ShapeDtypeStruct((1, d), F32)] * 2,
                    (dxa, dh1, x, scale1))


P_X, P_BC, P_Z, P_H, P_B, P_C = range(6)
HBM_REF = pl.BlockSpec(memory_space=pl.ANY)


def _conv_silu_fwd(proj, conv_w, conv_b, s, u, comm=None):
    tr = _tile_rows(s, u)
    w = 2 * u
    half = SSM_CONV // 2

    def body(p0, c0, n0, p1, c1, n1, w_ref, b_ref, o_ref):
        for blk, (pr, cr, nr) in enumerate(((p0, c0, n0), (p1, c1, n1))):
            e = _ext(pr, cr, nr, s, tr)
            wv = w_ref[:, blk * u:(blk + 1) * u]
            acc = jnp.zeros_like(e)
            for k in range(SSM_CONV):
                acc = acc + _shift(e, k - half) * wv[k:k + 1, :]
            pre = acc[HALO:HALO + tr] + b_ref[:, blk * u:(blk + 1) * u]
            o_ref[:, blk * u:(blk + 1) * u] = _silu(pre)

    in_specs = _halo_specs(s, tr, u, P_X) + _halo_specs(s, tr, u, P_BC) + [_full((SSM_CONV, w)), _full((1, w))]
    return _rowcall(body, "conv_silu_fwd", s, tr, in_specs, _row(tr, w), jax.ShapeDtypeStruct((s, w), F32),
                    (proj,) * 6 + (conv_w, conv_b), comm=comm)


def _conv_silu_bwd(proj, dxbc, conv_w, conv_b, dproj, s, u, comm=None):
    tr = _tile_rows(s, u)
    w = 2 * u
    half = SSM_CONV // 2

    def body(p0, c0, n0, p1, c1, n1, dp0, dc0, dn0, dp1, dc1, dn1, w_ref, b_ref, dproj_in, du_ref, dw_ref, db_ref):
        for blk, (ur, dr) in enumerate((((p0, c0, n0), (dp0, dc0, dn0)), ((p1, c1, n1), (dp1, dc1, dn1)))):
            e = _ext(*ur, s, tr)
            de = _ext(*dr, s, tr)
            wv = w_ref[:, blk * u:(blk + 1) * u]
            acc = jnp.zeros_like(e)
            for k in range(SSM_CONV):
                acc = acc + _shift(e, k - half) * wv[k:k + 1, :]
            pre = acc + b_ref[:, blk * u:(blk + 1) * u]
            dpre = jnp.where(_valid_rows(e.shape, s, tr), de * _dsilu(pre), 0.0)
            du = jnp.zeros_like(e)
            rows = []
            for k in range(SSM_CONV):
                du = du + _shift(dpre, half - k) * wv[k:k + 1, :]
                rows.append(jnp.sum((_shift(e, k - half) * dpre)[HALO:HALO + tr], axis=0, keepdims=True))
            du_ref[:, blk * u:(blk + 1) * u] = du[HALO:HALO + tr].astype(BF16)
            dwv = jnp.concatenate(rows + [jnp.zeros((SUBLANES - SSM_CONV, u), F32)], axis=0)
            dbv = jnp.sum(dpre[HALO:HALO + tr], axis=0, keepdims=True)
            first = pl.program_id(0) == 0

            @pl.when(first)
            def _():
                dw_ref[:, blk * u:(blk + 1) * u] = dwv
                db_ref[:, blk * u:(blk + 1) * u] = dbv

            @pl.when(jnp.logical_not(first))
            def _():
                dw_ref[:, blk * u:(blk + 1) * u] += dwv
                db_ref[:, blk * u:(blk + 1) * u] += dbv

    in_specs = (_halo_specs(s, tr, u, P_X) + _halo_specs(s, tr, u, P_BC) + _halo_specs(s, tr, u, 0)
                + _halo_specs(s, tr, u, 1) + [_full((SSM_CONV, w)), _full((1, w)), HBM_REF])
    return _rowcall(body, "conv_silu_bwd", s, tr, in_specs,
                    [_row(tr, w), _full((SUBLANES, w)), _full((1, w))],
                    [jax.ShapeDtypeStruct(dproj.shape, BF16), jax.ShapeDtypeStruct((SUBLANES, w), F32),
                     jax.ShapeDtypeStruct((1, w), F32)],
                    (proj,) * 6 + (dxbc,) * 6 + (conv_w, conv_b, dproj), comm=comm, aliases={14: 0})


def _expanders(h):
    col64 = jnp.arange(2 * h * HEAD_DIM) // HEAD_DIM
    col128 = jnp.arange(2 * h * LANES) // LANES
    row = jnp.arange(LANES)[:, None]
    return (row == col64[None, :]).astype(BF16), (row == col128[None, :]).astype(BF16)


def _dt_prep(proj, bias_row, a_row, s, u, h, comm=None):
    q = CHUNK
    e64, e128 = _expanders(h)
    ds = h * HEAD_DIM
    dtblk = (6 * u) // LANES

    def body(raw_ref, b_ref, a_ref, e64_ref, e128_ref, dt_ref, cum_ref, dte_ref, cume_ref):
        lane = lax.broadcasted_iota(jnp.int32, (q, LANES), 1)
        dt = jnp.where(lane < 2 * h, _softplus(raw_ref[...] + b_ref[...]), 0.0)
        da = dt * a_ref[...]
        ii = lax.broadcasted_iota(jnp.int32, (q, q), 0)
        kk = lax.broadcasted_iota(jnp.int32, (q, q), 1)
        lower = (kk <= ii).astype(F32).astype(BF16)
        upper = (kk >= ii).astype(F32).astype(BF16)
        cum = jnp.where(lane < h, _dot3_l(lower, da), _dot3_l(upper, da))
        dt_ref[...] = dt
        cum_ref[...] = cum
        dte = _dot3_r(dt, e64_ref[...])
        cume = _dot3_r(cum, e128_ref[...])
        dte_ref[0] = dte[:, :ds]
        dte_ref[1] = dte[:, ds:]
        cume_ref[0] = cume[:, :h * LANES]
        cume_ref[1] = cume[:, h * LANES:]

    in_specs = [pl.BlockSpec((q, LANES), lambda i: (i, dtblk)), _full((1, LANES)), _full((1, LANES)),
                _full(e64.shape), _full(e128.shape)]
    out_specs = [_row(q, LANES), _row(q, LANES),
                 pl.BlockSpec((2, q, ds), lambda i: (0, i, 0)), pl.BlockSpec((2, q, h * LANES), lambda i: (0, i, 0))]
    out_shape = [jax.ShapeDtypeStruct((s, LANES), F32), jax.ShapeDtypeStruct((s, LANES), F32),
                 jax.ShapeDtypeStruct((2, s, ds), F32), jax.ShapeDtypeStruct((2, s, h * LANES), F32)]
    return _rowcall(body, "dt_prep", s, q, in_specs, out_specs, out_shape, (proj, bias_row, a_row, e64, e128), comm=comm)


def _ssd_specs(s, h, g):
    q = CHUNK
    nc = s // q
    ds = h * HEAD_DIM
    nb = g * STATE
    return q, nc, ds, nb


def _ssd_fwd(xbc, dt_e, cum_e, cum_t, s, h, g, comm=None):
    q, nc, ds, nb = _ssd_specs(s, h, g)
    npair = h // 2

    def cidx(d, i):
        return jnp.where(d == 0, i, nc - 1 - i)

    def body(x_ref, b_ref, c_ref, dt_ref, cum_ref, cumt_ref, y_ref, sp_ref, st):
        d = pl.program_id(0)
        i = pl.program_id(1)

        @pl.when(i == 0)
        def _():
            st[...] = jnp.zeros_like(st)

        rev = d == 1
        ii = lax.broadcasted_iota(jnp.int32, (q, q), 0)
        jj = lax.broadcasted_iota(jnp.int32, (q, q), 1)
        sgn = jnp.where(rev, -1, 1)
        mask = (jj - ii) * sgn <= 0
        left = lax.broadcasted_iota(jnp.int32, (q, LANES), 1) < HEAD_DIM

        def group(gi, carry):
            goff = pl.multiple_of(gi * STATE, STATE)
            cg = c_ref[:, pl.ds(goff, STATE)].astype(BF16)
            bg = b_ref[:, pl.ds(goff, STATE)].astype(BF16)
            gm = _dot_nt(cg, bg)
            for p in range(HEADS_PER_GROUP // 2):
                pr = gi * (HEADS_PER_GROUP // 2) + p
                off = pl.multiple_of(pr * LANES, LANES)
                xd = x_ref[:, pl.ds(off, LANES)] * dt_ref[:, pl.ds(off, LANES)]
                ms = []
                cols = []
                for hl in range(2):
                    hh = 2 * pr + hl
                    col = cum_ref[:, pl.ds(pl.multiple_of(hh * LANES, LANES), LANES)]
                    row = cumt_ref[pl.ds(hh, 1), :]
                    lm = jnp.where(mask, jnp.exp(jnp.minimum(col - row, 0.0)), 0.0)
                    ms.append((gm * lm).astype(BF16))
                    cols.append(col)
                y = _dot(ms[0], jnp.where(left, xd, 0.0).astype(BF16)) + _dot(ms[1], jnp.where(left, 0.0, xd).astype(BF16))
                ce = jnp.where(left, cols[0], cols[1])
                sprev = st[pr]
                sp_ref[pr] = sprev
                y = y + jnp.exp(ce) * _dot(cg, sprev.astype(BF16))
                y_ref[:, pl.ds(off, LANES)] = y
                tot = jnp.where(rev, ce[0:1, :], ce[q - 1:q, :])
                v = (xd * jnp.exp(tot - ce)).astype(BF16)
                st[pr] = jnp.exp(tot) * sprev + _dot_tn(bg, v)
            return carry

        lax.fori_loop(0, g, group, 0, unroll=2)

    in_specs = [
        pl.BlockSpec((q, ds), lambda d, i: (cidx(d, i), 0)),
        pl.BlockSpec((q, nb), lambda d, i: (cidx(d, i), ds // nb)),
        pl.BlockSpec((q, nb), lambda d, i: (cidx(d, i), ds // nb + 1)),
        pl.BlockSpec((None, q, ds), lambda d, i: (d, cidx(d, i), 0)),
        pl.BlockSpec((None, q, h * LANES), lambda d, i: (d, cidx(d, i), 0)),
        pl.BlockSpec((None, h, q), lambda d, i: (d, 0, cidx(d, i))),
    ]
    out_specs = [
        pl.BlockSpec((None, q, ds), lambda d, i: (d, cidx(d, i), 0)),
        pl.BlockSpec((None, None, npair, STATE, LANES), lambda d, i: (d, cidx(d, i), 0, 0, 0)),
    ]
    out_shape = [jax.ShapeDtypeStruct((2, s, ds), F32), jax.ShapeDtypeStruct((2, nc, npair, STATE, LANES), F32)]
    if comm is not None:
        return _comm_call(body, name="ssd_fwd", grid=(2, nc), in_specs=in_specs, out_specs=out_specs, out_shape=out_shape,
                          scratch_shapes=[pltpu.VMEM((npair, STATE, LANES), F32)],
                          args=(xbc, xbc, xbc, dt_e, cum_e, cum_t), comm=comm)
    return pl.pallas_call(
        body, name="ssd_fwd", grid=(2, nc), in_specs=in_specs, out_specs=out_specs, out_shape=out_shape,
        scratch_shapes=[pltpu.VMEM((npair, STATE, LANES), F32)],
        compiler_params=_params(("arbitrary", "arbitrary")),
    )(xbc, xbc, xbc, dt_e, cum_e, cum_t), ()


def _ssd_bwd(xbc, dt_e, cum_e, cum_t, dt_t, a_col, dy, sp, s, h, g, comm=None):
    q, nc, ds, nb = _ssd_specs(s, h, g)
    npair = h // 2

    def cidx(d, i):
        return jnp.where(d == 0, nc - 1 - i, i)

    def body(x_ref, b_ref, c_ref, dt_ref, cum_ref, cumt_ref, dtt_ref, a_ref, dy_ref, sp_ref,
             dx_ref, db_ref, dc_ref, ddt_ref, da_ref, dst, rowp):
        d = pl.program_id(0)
        i = pl.program_id(1)

        @pl.when(i == 0)
        def _():
            dst[...] = jnp.zeros_like(dst)
            da_ref[...] = jnp.zeros_like(da_ref)

        rev = d == 1
        ii = lax.broadcasted_iota(jnp.int32, (q, q), 0)
        jj = lax.broadcasted_iota(jnp.int32, (q, q), 1)
        sgn = jnp.where(rev, -1, 1)
        mask = (jj - ii) * sgn <= 0
        lane = lax.broadcasted_iota(jnp.int32, (q, LANES), 1)
        left = lane < HEAD_DIM
        rowp[...] = jnp.zeros_like(rowp)

        def group(gi, carry):
            acc_dcum, acc_tot, acc_dxx = carry
            goff = pl.multiple_of(gi * STATE, STATE)
            cg = c_ref[:, pl.ds(goff, STATE)].astype(BF16)
            bg = b_ref[:, pl.ds(goff, STATE)].astype(BF16)
            gm = _dot_nt(cg, bg)
            dgm = jnp.zeros((q, q), F32)
            dcg = jnp.zeros((q, STATE), F32)
            dbg = jnp.zeros((q, STATE), F32)
            for p in range(HEADS_PER_GROUP // 2):
                pr = gi * (HEADS_PER_GROUP // 2) + p
                off = pl.multiple_of(pr * LANES, LANES)
                xv = x_ref[:, pl.ds(off, LANES)]
                dte = dt_ref[:, pl.ds(off, LANES)]
                xd = xv * dte
                xdb = xd.astype(BF16)
                dyv = dy_ref[:, pl.ds(off, LANES)]
                sprev = sp_ref[pr]
                sprevb = sprev.astype(BF16)
                dsn = dst[pr]
                dsnb = dsn.astype(BF16)
                cols = [cum_ref[:, pl.ds(pl.multiple_of((2 * pr + hl) * LANES, LANES), LANES)] for hl in range(2)]
                ce = jnp.where(left, cols[0], cols[1])
                tot = jnp.where(rev, ce[0:1, :], ce[q - 1:q, :])
                et = jnp.exp(tot)
                r = jnp.exp(tot - ce)
                e = jnp.exp(ce)
                yoff = e * _dot(cg, sprevb)
                dz = (e * dyv).astype(BF16)
                dcg = dcg + _dot_nt(dz, sprevb)
                dsprev = _dot_tn(cg, dz) + et * dsn
                f1 = dyv * yoff
                v = (xd * r).astype(BF16)
                dbg = dbg + _dot_nt(v, dsnb)
                dv = _dot(bg, dsnb)
                dxd = dv * r
                tt = dv * xd * r
                wt = dsn * sprev * et
                for hl in range(2):
                    hh = 2 * pr + hl
                    hm = left if hl == 0 else jnp.logical_not(left)
                    row = cumt_ref[pl.ds(hh, 1), :]
                    lm = jnp.where(mask, jnp.exp(jnp.minimum(cols[hl] - row, 0.0)), 0.0)
                    mf = gm * lm
                    dym = jnp.where(hm, dyv, 0.0).astype(BF16)
                    dm = _dot_nt(dym, xdb)
                    dxd = dxd + _dot_tn(mf.astype(BF16), dym)
                    dgm = dgm + dm * lm
                    em = dm * mf
                    rowp[pl.ds(hh, 1), :] = rowp[pl.ds(hh, 1), :] - jnp.sum(em, axis=0, keepdims=True)
                    colq = (jnp.sum(em, axis=1, keepdims=True)
                            + jnp.sum(jnp.where(hm, f1 - tt, 0.0), axis=1, keepdims=True))
                    acc_dcum = jnp.where(lane == hh, colq, acc_dcum)
                    totq = jnp.sum(jnp.sum(jnp.where(hm, tt + wt, 0.0), axis=1, keepdims=True), axis=0, keepdims=True)
                    acc_tot = jnp.where(lane == hh, totq, acc_tot)
                dxx = dxd * xv
                for hl in range(2):
                    hh = 2 * pr + hl
                    hm = left if hl == 0 else jnp.logical_not(left)
                    acc_dxx = jnp.where(lane == hh, jnp.sum(jnp.where(hm, dxx, 0.0), axis=1, keepdims=True), acc_dxx)
                dx_ref[:, pl.ds(off, LANES)] = dxd * dte
                dst[pr] = dsprev
            dgb = dgm.astype(BF16)
            dc_ref[:, pl.ds(goff, STATE)] = dcg + _dot(dgb, bg)
            db_ref[:, pl.ds(goff, STATE)] = dbg + _dot_tn(dgb, cg)
            return acc_dcum, acc_tot, acc_dxx

        zero = jnp.zeros((q, LANES), F32)
        def two_groups(j, carry):
            return group(2 * j + 1, group(2 * j, carry))

        acc_dcum, acc_tot, acc_dxx = lax.fori_loop(0, g // 2, two_groups, (zero, zero, zero))
        dcum_t = rowp[...] + jnp.transpose(acc_dcum)[:h]
        rmat = ((ii - jj) * sgn >= 0).astype(F32).astype(BF16)
        da_t = _dot3_r(dcum_t, rmat) + jnp.transpose(acc_tot)[:h]
        ddt_ref[...] = da_t * a_ref[...] + jnp.transpose(acc_dxx)[:h]
        da_ref[...] += da_t * dtt_ref[...]

    in_specs = [
        pl.BlockSpec((q, ds), lambda d, i: (cidx(d, i), 0)),
        pl.BlockSpec((q, nb), lambda d, i: (cidx(d, i), ds // nb)),
        pl.BlockSpec((q, nb), lambda d, i: (cidx(d, i), ds // nb + 1)),
        pl.BlockSpec((None, q, ds), lambda d, i: (d, cidx(d, i), 0)),
        pl.BlockSpec((None, q, h * LANES), lambda d, i: (d, cidx(d, i), 0)),
        pl.BlockSpec((None, h, q), lambda d, i: (d, 0, cidx(d, i))),
        pl.BlockSpec((None, h, q), lambda d, i: (d, 0, cidx(d, i))),
        pl.BlockSpec((None, h, LANES), lambda d, i: (d, 0, 0)),
        pl.BlockSpec((q, ds), lambda d, i: (cidx(d, i), 0)),
        pl.BlockSpec((None, None, npair, STATE, LANES), lambda d, i: (d, cidx(d, i), 0, 0, 0)),
    ]
    out_specs = [
        pl.BlockSpec((None, q, ds), lambda d, i: (d, cidx(d, i), 0)),
        pl.BlockSpec((None, q, nb), lambda d, i: (d, cidx(d, i), 0)),
        pl.BlockSpec((None, q, nb), lambda d, i: (d, cidx(d, i), 0)),
        pl.BlockSpec((None, h, q), lambda d, i: (d, 0, cidx(d, i))),
        pl.BlockSpec((None, h, LANES), lambda d, i: (d, 0, 0)),
    ]
    out_shape = [jax.ShapeDtypeStruct((2, s, ds), F32), jax.ShapeDtypeStruct((2, s, nb), F32),
                 jax.ShapeDtypeStruct((2, s, nb), F32), jax.ShapeDtypeStruct((2, h, s), F32),
                 jax.ShapeDtypeStruct((2, h, LANES), F32)]
    scratch = [pltpu.VMEM((npair, STATE, LANES), F32), pltpu.VMEM((h, q), F32)]
    args = (xbc, xbc, xbc, dt_e, cum_e, cum_t, dt_t, a_col, dy, sp)
    if comm is not None:
        return _comm_call(body, name="ssd_bwd", grid=(2, nc), in_specs=in_specs, out_specs=out_specs, out_shape=out_shape,
                          scratch_shapes=scratch, args=args, comm=comm)
    return pl.pallas_call(
        body, name="ssd_bwd", grid=(2, nc), in_specs=in_specs, out_specs=out_specs, out_shape=out_shape,
        scratch_shapes=scratch, compiler_params=_params(("arbitrary", "arbitrary")),
    )(*args), ()


def _dt_bwd(ddt, proj, bias_row, dproj, s, u, h):
    tr = _tile_rows(s, 4 * LANES)
    dtblk = (6 * u) // LANES
    tail = dproj.shape[1] - 6 * u
    assert (6 * u) % tail == 0

    def body(d_ref, raw_ref, b_ref, dproj_in, o_ref, db_ref):
        lane = lax.broadcasted_iota(jnp.int32, (tr, LANES), 1)
        v = jnp.where(lane < 2 * h, d_ref[...] * _sigmoid(raw_ref[...] + b_ref[...]), 0.0)
        o_ref[:, :LANES] = v.astype(BF16)
        o_ref[:, LANES:] = jnp.zeros((tr, tail - LANES), BF16)
        _acc_rows(db_ref, v)

    return _rowcall(body, "dt_bwd", s, tr, [_row(tr, LANES), _row(tr, LANES, dtblk), _full((1, LANES)), HBM_REF],
                    [_row(tr, tail, (6 * u) // tail), _full((1, LANES))],
                    [jax.ShapeDtypeStruct(dproj.shape, BF16), jax.ShapeDtypeStruct((1, LANES), F32)],
                    (ddt, proj, bias_row, dproj), aliases={3: 0})


def _group_rms(v, gw):
    outs, facs = [], []
    for k in range(v.shape[1] // gw):
        blk = v[:, k * gw:(k + 1) * gw]
        f = lax.rsqrt(jnp.mean(blk * blk, axis=-1, keepdims=True) + RMS_EPS)
        outs.append(blk * f)
        facs.append(jnp.broadcast_to(f, blk.shape))
    return jnp.concatenate(outs, axis=1), jnp.concatenate(facs, axis=1)


def _group_rms_bwd(dn, n, fac, gw):
    outs = []
    for k in range(n.shape[1] // gw):
        sl = slice(k * gw, (k + 1) * gw)
        outs.append(fac[:, sl] * (dn[:, sl] - n[:, sl] * jnp.mean(dn[:, sl] * n[:, sl], axis=-1, keepdims=True)))
    return jnp.concatenate(outs, axis=1)


def _gate_norm_fwd(y2, xbc, proj, d_e, norm_w, s, u, g):
    tr = _tile_rows(s, u)
    gw = u // g

    def body(y_ref, x_ref, z_ref, d_ref, w_ref, o_ref):
        ys = y_ref[0] + y_ref[1] + d_ref[...] * x_ref[...]
        n, _ = _group_rms(ys * _silu(z_ref[...]), gw)
        o_ref[...] = (n * w_ref[...]).astype(BF16)

    return _rowcall(body, "gate_norm_fwd", s, tr,
                    [pl.BlockSpec((2, tr, u), lambda i: (0, i, 0)), _row(tr, u), _row(tr, u, P_Z), _full((1, u)), _full((1, u))],
                    _row(tr, u), jax.ShapeDtypeStruct((s, u), BF16), (y2, xbc, proj, d_e, norm_w))


def _gate_norm_bwd(dymix, y2, xbc, proj, d_e, norm_w, pw, s, u, g, comm=None):
    tr = _tile_rows(s, u)
    gw = u // g

    def body(dy_ref, y_ref, x_ref, z_ref, d_ref, w_ref, dys_ref, dz_ref, dxs_ref, dw_ref, dd_ref):
        xv = x_ref[...]
        zv = z_ref[...]
        ys = y_ref[0] + y_ref[1] + d_ref[...] * xv
        sz = _silu(zv)
        n, fac = _group_rms(ys * sz, gw)
        dout = dy_ref[...]
        _acc_rows(dw_ref, dout * n)
        dyg = _group_rms_bwd(dout * w_ref[...], n, fac, gw)
        dys = dyg * sz
        dys_ref[...] = dys
        dz_ref[...] = (dyg * ys * _dsilu(zv)).astype(BF16)
        dxs_ref[...] = dys * d_ref[...]
        _acc_rows(dd_ref, dys * xv)

    v = _full((1, u))
    return _rowcall(body, "gate_norm_bwd", s, tr,
                    [_row(tr, u), pl.BlockSpec((2, tr, u), lambda i: (0, i, 0)), _row(tr, u), _row(tr, u, P_Z), v, v],
                    [_row(tr, u), _row(tr, u, P_Z), _row(tr, u), v, v],
                    [jax.ShapeDtypeStruct((s, u), F32), jax.ShapeDtypeStruct((s, pw), BF16),
                     jax.ShapeDtypeStruct((s, u), F32), jax.ShapeDtypeStruct((1, u), F32), jax.ShapeDtypeStruct((1, u), F32)],
                    (dymix, y2, xbc, proj, d_e, norm_w), comm=comm)


def _shortconv_fwd(proj, conv_w, norm_w, s, u):
    tr = _tile_rows(s, u)
    half = SC_CONV // 2

    def body(hp, hc, hn, b_ref, cp, cc, cn, cw_ref, w_ref, o_ref):
        t = _ext(hp, hc, hn, s, tr) * _ext(cp, cc, cn, s, tr)
        wv = cw_ref[...]
        acc = jnp.zeros_like(t)
        for k in range(SC_CONV):
            acc = acc + _shift(t, k - half) * wv[k:k + 1, :]
        n, _ = _group_rms(b_ref[...] * acc[HALO:HALO + tr], SC_GROUP_WIDTH)
        o_ref[...] = (n * w_ref[...]).astype(BF16)

    in_specs = (_halo_specs(s, tr, u, P_H) + [_row(tr, u, P_B)] + _halo_specs(s, tr, u, P_C)
                + [_full((SC_CONV, u)), _full((1, u))])
    return _rowcall(body, "shortconv_fwd", s, tr, in_specs, _row(tr, u), jax.ShapeDtypeStruct((s, u), BF16),
                    (proj,) * 7 + (conv_w, norm_w))


def _shortconv_bwd(dymix, proj, conv_w, norm_w, dproj, s, u):
    tr = _tile_rows(s, u)
    half = SC_CONV // 2

    def body(dp, dc_, dn, hp, hc, hn, bp, bc, bn, cp, cc, cn, cw_ref, w_ref, dproj_in, o_ref, dcw_ref, dw_ref):
        dout = _ext(dp, dc_, dn, s, tr)
        hv = _ext(hp, hc, hn, s, tr)
        bv = _ext(bp, bc, bn, s, tr)
        cv = _ext(cp, cc, cn, s, tr)
        t = hv * cv
        wv = cw_ref[...]
        acc = jnp.zeros_like(t)
        for k in range(SC_CONV):
            acc = acc + _shift(t, k - half) * wv[k:k + 1, :]
        n, fac = _group_rms(bv * acc, SC_GROUP_WIDTH)
        cur = slice(HALO, HALO + tr)
        _acc_rows(dw_ref, (dout * n)[cur])
        dyv = _group_rms_bwd(dout * w_ref[...], n, fac, SC_GROUP_WIDTH)
        o_ref[:, u:2 * u] = (dyv * acc)[cur].astype(BF16)
        dv = dyv * bv
        dt = jnp.zeros_like(t)
        rows = []
        for k in range(SC_CONV):
            dt = dt + _shift(dv, half - k) * wv[k:k + 1, :]
            rows.append(jnp.sum((_shift(t, k - half) * dv)[cur], axis=0, keepdims=True))
        o_ref[:, :u] = (dt * cv)[cur].astype(BF16)
        o_ref[:, 2 * u:] = (dt * hv)[cur].astype(BF16)
        dwv = jnp.concatenate(rows + [jnp.zeros((SUBLANES - SC_CONV, u), F32)], axis=0)
        first = pl.program_id(0) == 0

        @pl.when(first)
        def _():
            dcw_ref[...] = dwv

        @pl.when(jnp.logical_not(first))
        def _():
            dcw_ref[...] += dwv

    in_specs = (_halo_specs(s, tr, u, 1) + _halo_specs(s, tr, u, P_H) + _halo_specs(s, tr, u, P_B)
                + _halo_specs(s, tr, u, P_C) + [_full((SC_CONV, u)), _full((1, u)), HBM_REF])
    return _rowcall(body, "shortconv_bwd", s, tr, in_specs,
                    [_row(tr, 3 * u, 1), _full((SUBLANES, u)), _full((1, u))],
                    [jax.ShapeDtypeStruct(dproj.shape, BF16), jax.ShapeDtypeStruct((SUBLANES, u), F32),
                     jax.ShapeDtypeStruct((1, u), F32)],
                    (dymix,) * 3 + (proj,) * 9 + (conv_w, norm_w, dproj), aliases={14: 0})


def _adam_math(w, g, m, v):
    m2 = ADAM_B1 * m + (1.0 - ADAM_B1) * g
    v2 = ADAM_B2 * v + (1.0 - ADAM_B2) * (g * g)
    m_hat = m2 / (1.0 - ADAM_B1 ** ADAM_STEP)
    v_hat = v2 / (1.0 - ADAM_B2 ** ADAM_STEP)
    delta = -ADAM_LR * (m_hat / (jnp.sqrt(v_hat) + ADAM_EPS) + ADAM_WD * w)
    return delta, m2, v2


def _adam_rows(r, c):
    return _pick(r, max(SUBLANES, (1 << 20) // (4 * c)), SUBLANES)


def _adam(w, g, m, v, *, name, emit_grad=False):
    r, c = w.shape
    tr = _adam_rows(r, c)
    n_out = 4 if emit_grad else 3

    def body(w_ref, g_ref, m_ref, v_ref, *outs):
        gv = g_ref[...]
        if emit_grad:
            outs[0][...] = gv
        outs[-3][...], outs[-2][...], outs[-1][...] = _adam_math(w_ref[...], gv, m_ref[...], v_ref[...])

    return _rowcall(body, name, r, tr, [_row(tr, c)] * 4, [_row(tr, c)] * n_out,
                    [jax.ShapeDtypeStruct((r, c), F32)] * n_out, (w, g, m, v))


def _adam_outer(w, a_t, bmat, m, v, *, name, comm=None):
    r, c = w.shape
    tr = _adam_rows(r, c)
    kk = a_t.shape[1]

    def body(w_ref, a_ref, b_ref, m_ref, v_ref, g_ref, d_ref, m2_ref, v2_ref):
        g = _dot(a_ref[...].astype(BF16), b_ref[...].astype(BF16))
        g_ref[...] = g
        d_ref[...], m2_ref[...], v2_ref[...] = _adam_math(w_ref[...], g, m_ref[...], v_ref[...])

    return _rowcall(body, name, r, tr, [_row(tr, c), _row(tr, kk), _full((kk, c)), _row(tr, c), _row(tr, c)],
                    [_row(tr, c)] * 4, [jax.ShapeDtypeStruct((r, c), F32)] * 4, (w, a_t, bmat, m, v), comm=comm)


ANY = pl.BlockSpec(memory_space=pl.ANY)
VMEM_WHOLE = pl.BlockSpec(memory_space=pltpu.VMEM)


def _place():
    x, y, c = lax.axis_index("x"), lax.axis_index("y"), lax.axis_index("c")
    return x, y, c


DMA_CHUNKS = 8


def _n_chunks(rows):
    n = DMA_CHUNKS
    while n > 1 and rows % (16 * n):
        n //= 2
    return n


def _allgather_small(v, *, name):
    m_per, n = v.shape

    def body(x_ref, out_ref, send_sems, recv_sems, local_sem):
        x, y, c = _place()
        me, sibling = (x, y, c), (x, y, 1 - c)
        chips = [(1 - x, y), (x, 1 - y), (1 - x, 1 - y)]

        def rows(px, py, pc):
            return out_ref.at[pl.ds((4 * px + 2 * py + pc) * m_per, m_per), :]

        def copy(k, block, to, src=None):
            return pltpu.make_async_remote_copy(
                src_ref=rows(*block) if src is None else src, dst_ref=rows(*block),
                send_sem=send_sems.at[k], recv_sem=recv_sems.at[k], device_id=to, device_id_type=MESH)

        mine = pltpu.make_async_copy(x_ref, rows(*me), local_sem)
        mine.start()
        first = [copy(0, me, sibling, src=x_ref)]
        first += [copy(1 + j, me, (*chip, c), src=x_ref) for j, chip in enumerate(chips)]
        for cp in first:
            cp.start()
        passed = [copy(4 + j, (*chip, c), sibling) for j, chip in enumerate(chips)]
        for j, chip in enumerate(chips):
            copy(1 + j, (*chip, c), me).wait_recv()
            passed[j].start()
        copy(0, sibling, me).wait_recv()
        for j, chip in enumerate(chips):
            copy(4 + j, (*chip, 1 - c), me).wait_recv()
        for cp in first + passed:
            cp.wait_send()
        mine.wait()

    return pl.pallas_call(
        body, name=name, out_shape=jax.ShapeDtypeStruct((N_DEV * m_per, n), v.dtype),
        in_specs=[VMEM_WHOLE], out_specs=VMEM_WHOLE,
        scratch_shapes=[pltpu.SemaphoreType.DMA((7,)), pltpu.SemaphoreType.DMA((7,)), pltpu.SemaphoreType.DMA],
        compiler_params=pltpu.CompilerParams(vmem_limit_bytes=VMEM_LIMIT),
    )(v)


def _chip_id():
    return 2 * lax.axis_index("x") + lax.axis_index("y")


def _core_id():
    return lax.axis_index("c")


def _cast_into_block(wl, *, name, comm=None):
    r, c_ = wl.shape
    tr = _tile_rows(r, c_, 16)

    def body(w_ref, o_ref):
        o_ref[...] = w_ref[...].astype(BF16)

    in_spec = pl.BlockSpec((tr, c_), lambda i: (i, 0))
    out_spec = pl.BlockSpec((None, tr, c_), lambda i: (_chip_id(), i, 0))
    out_shape = jax.ShapeDtypeStruct((N_CHIPS, r, c_), BF16)
    if comm is not None:
        (out,), landed = _comm_call(body, name=name, grid=(r // tr,), in_specs=[in_spec], out_specs=[out_spec],
                                    out_shape=[out_shape], scratch_shapes=[], args=(wl,), comm=comm)
        return out, landed
    return pl.pallas_call(body, name=name, grid=(r // tr,), in_specs=[in_spec], out_specs=out_spec, out_shape=out_shape,
                          compiler_params=_params(("arbitrary",)))(wl)


def _gather_weight(buf, *, name, part=(0, 1)):
    return _run_comm(_gather_comm(buf, part), name=name)[0]


def _gather_comm(buf, part=(0, 1)):
    _, r, c_ = buf.shape
    half = r // 2
    n_all = _n_chunks(half)
    rows = half // n_all
    first_chunk = round(part[0] * n_all)
    nch = round(part[1] * n_all) - first_chunk

    def plan(out_ref):
        x, y, c = _place()
        me, sibling = (x, y, c), (x, y, 1 - c)
        chips = [(1 - x, y), (x, 1 - y), (1 - x, 1 - y)]
        return me, sibling, chips, c

    def copy(out_ref, sems, k, i, block, to):
        part = out_ref.at[2 * block[0] + block[1], pl.ds(block[2] * half + (first_chunk + i) * rows, rows), :]
        return pltpu.make_async_remote_copy(src_ref=part, dst_ref=part, send_sem=sems[0].at[k * nch + i],
                                            recv_sem=sems[1].at[k * nch + i], device_id=to, device_id_type=MESH)

    def start(cins, couts, sems):
        (out_ref,) = couts
        me, sibling, chips, c = plan(out_ref)
        for i in range(nch):
            for j, chip in enumerate(chips):
                copy(out_ref, sems, j, i, me, (*chip, c)).start()

    def finish(cins, couts, sems):
        (out_ref,) = couts
        me, sibling, chips, c = plan(out_ref)
        passed = []
        for i in range(nch):
            for j, chip in enumerate(chips):
                copy(out_ref, sems, j, i, (*chip, c), me).wait_recv()
                passed.append(copy(out_ref, sems, 3 + j, i, (*chip, c), sibling))
                passed[-1].start()
        for i in range(nch):
            for j, chip in enumerate(chips):
                copy(out_ref, sems, 3 + j, i, (*chip, 1 - c), me).wait_recv()
        for i in range(nch):
            for j, chip in enumerate(chips):
                copy(out_ref, sems, j, i, me, (*chip, c)).wait_send()
        for cp in passed:
            cp.wait_send()

    return dict(inputs=[buf], out_shape=[jax.ShapeDtypeStruct(buf.shape, buf.dtype)], aliases={0: 0},
                scratch=[pltpu.SemaphoreType.DMA((6 * nch,)), pltpu.SemaphoreType.DMA((6 * nch,))],
                start=start, finish=finish)


def _merge_comms(*comms):
    inputs, outs, aliases, scratch, spans = [], [], {}, [], []
    for cm in comms:
        i0, o0, s0 = len(inputs), len(outs), len(scratch)
        inputs += cm["inputs"]
        outs += cm["out_shape"]
        scratch += cm["scratch"]
        aliases.update({i0 + k: o0 + v for k, v in cm["aliases"].items()})
        spans.append((i0, len(inputs), o0, len(outs), s0, len(scratch)))

    def run(which):
        def f(cins, couts, sems):
            for cm, (i0, i1, o0, o1, s0, s1) in zip(comms, spans):
                cm[which](cins[i0:i1], couts[o0:o1], sems[s0:s1])
        return f

    return dict(inputs=inputs, out_shape=outs, aliases=aliases, scratch=scratch, start=run("start"), finish=run("finish"))


def _pair_comm(gfull):
    nblk, r, c_ = gfull.shape
    half = r // 2
    nch = _n_chunks(half)
    rows = half // nch

    def copies(cins, couts, sems):
        g_ref, peer_ref = cins[0], couts[0]
        x, y, c = _place()
        return [pltpu.make_async_remote_copy(
            src_ref=g_ref.at[k, pl.ds((1 - c) * half + i * rows, rows), :], dst_ref=peer_ref.at[k, pl.ds(i * rows, rows), :],
            send_sem=sems[0].at[k * nch + i], recv_sem=sems[1].at[k * nch + i],
            device_id=(x, y, 1 - c), device_id_type=MESH) for i in range(nch) for k in range(nblk)]

    def start(cins, couts, sems):
        for cp in copies(cins, couts, sems):
            cp.start()

    def finish(cins, couts, sems):
        cps = copies(cins, couts, sems)
        for cp in cps:
            cp.wait_recv()
        for cp in cps:
            cp.wait_send()

    return dict(inputs=[gfull], out_shape=[jax.ShapeDtypeStruct((nblk, half, c_), gfull.dtype)], aliases={},
                scratch=[pltpu.SemaphoreType.DMA((nblk * nch,)), pltpu.SemaphoreType.DMA((nblk * nch,))],
                start=start, finish=finish)


def _swap_comm(buf):
    r, c_ = buf.shape
    half = r // 2
    nch = _n_chunks(half)
    rows = half // nch

    def copy(out_ref, sems, i, pc):
        part = out_ref.at[pl.ds(pc * half + i * rows, rows), :]
        x, y, c = _place()
        return pltpu.make_async_remote_copy(src_ref=part, dst_ref=part, send_sem=sems[0].at[i], recv_sem=sems[1].at[i],
                                            device_id=(x, y, 1 - c), device_id_type=MESH)

    def start(cins, couts, sems):
        c = _core_id()
        for i in range(nch):
            copy(couts[0], sems, i, c).start()

    def finish(cins, couts, sems):
        c = _core_id()
        for i in range(nch):
            copy(couts[0], sems, i, 1 - c).wait_recv()
        for i in range(nch):
            copy(couts[0], sems, i, c).wait_send()

    return dict(inputs=[buf], out_shape=[jax.ShapeDtypeStruct(buf.shape, buf.dtype)], aliases={0: 0},
                scratch=[pltpu.SemaphoreType.DMA((nch,)), pltpu.SemaphoreType.DMA((nch,))], start=start, finish=finish)


def _run_comm(cm, *, name):
    nci, nco = len(cm["inputs"]), len(cm["out_shape"])

    def body(*refs):
        cins, couts, sems = refs[:nci], refs[nci:nci + nco], refs[nci + nco:]
        cm["start"](cins, couts, sems)
        cm["finish"](cins, couts, sems)

    return pl.pallas_call(
        body, name=name, out_shape=list(cm["out_shape"]), in_specs=[ANY] * nci, out_specs=[ANY] * nco,
        input_output_aliases=dict(cm["aliases"]), scratch_shapes=cm["scratch"],
    )(*cm["inputs"])


def _pair_add(gfull, peer, *, name):
    nblk, r, c_ = gfull.shape
    half = r // 2
    tr = _pick(half, max(16, (1 << 20) // (2 * c_)), 16)
    per = half // tr

    def body(g_ref, p_ref, o_ref):
        o_ref[...] = (g_ref[...].astype(F32) + p_ref[...].astype(F32)).astype(BF16)

    return pl.pallas_call(
        body, name=name, grid=(nblk, per),
        in_specs=[pl.BlockSpec((None, tr, c_), lambda k, i: (k, _core_id() * per + i, 0)),
                  pl.BlockSpec((None, tr, c_), lambda k, i: (k, i, 0))],
        out_specs=pl.BlockSpec((None, tr, c_), lambda k, i: (k, i, 0)),
        out_shape=jax.ShapeDtypeStruct((nblk, half, c_), BF16),
        compiler_params=_params(("arbitrary", "arbitrary")))(gfull, peer)


def _scatter_comm(pre, part=(0, 1), recv=None):
    _, half, c_ = pre.shape
    n_all = _n_chunks(half)
    rows = half // n_all
    first_chunk = round(part[0] * n_all)
    nch = round(part[1] * n_all) - first_chunk

    def copies(cins, couts, sems):
        p_ref, r_ref = cins[0], couts[0]
        x, y, c = _place()
        out = []
        for i in range(nch):
            at = pl.ds((first_chunk + i) * rows, rows)
            for j, (tx, ty) in reversed(list(enumerate([(1 - x, y), (x, 1 - y), (1 - x, 1 - y)]))):
                out.append(pltpu.make_async_remote_copy(
                    src_ref=p_ref.at[2 * tx + ty, at, :], dst_ref=r_ref.at[j, at, :],
                    send_sem=sems[0].at[j * nch + i], recv_sem=sems[1].at[j * nch + i],
                    device_id=(tx, ty, c), device_id_type=MESH))
        return out

    def start(cins, couts, sems):
        for cp in copies(cins, couts, sems):
            cp.start()

    def finish(cins, couts, sems):
        cps = copies(cins, couts, sems)
        for cp in cps:
            cp.wait_recv()
        for cp in cps:
            cp.wait_send()

    return dict(inputs=[pre] if recv is None else [pre, recv], out_shape=[jax.ShapeDtypeStruct((3, half, c_), pre.dtype)],
                aliases={} if recv is None else {1: 0},
                scratch=[pltpu.SemaphoreType.DMA((3 * nch,)), pltpu.SemaphoreType.DMA((3 * nch,))],
                start=start, finish=finish)


def _sum_into_half(pre, recv, *, name, comm=None):
    _, half, c_ = pre.shape
    n = recv.shape[0]
    tr = _pick(half, max(16, (1 << 19) // (2 * c_)), 16)
    per = half // tr

    def body(g_ref, r_ref, o_ref):
        acc = g_ref[...].astype(F32)
        for k in range(n):
            acc = acc + r_ref[k].astype(F32)
        o_ref[...] = acc

    in_specs = [pl.BlockSpec((None, tr, c_), lambda i: (_chip_id(), i, 0)), pl.BlockSpec((n, tr, c_), lambda i: (0, i, 0))]
    out_spec = pl.BlockSpec((tr, c_), lambda i: (_core_id() * per + i, 0))
    out_shape = jax.ShapeDtypeStruct((2 * half, c_), F32)
    if comm is not None:
        (out,), landed = _comm_call(body, name=name, grid=(per,), in_specs=in_specs, out_specs=[out_spec],
                                    out_shape=[out_shape], scratch_shapes=[], args=(pre, recv), comm=comm)
        return out, landed
    return pl.pallas_call(body, name=name, grid=(per,), in_specs=in_specs, out_specs=out_spec, out_shape=out_shape,
                          compiler_params=_params(("arbitrary",)))(pre, recv)


def _sum_slots(recv, *, name):
    n, r, c_ = recv.shape
    tr = _pick(r, max(16, (1 << 19) // (2 * c_)), 16)

    def body(r_ref, o_ref):
        acc = r_ref[0].astype(F32)
        for k in range(1, n):
            acc = acc + r_ref[k].astype(F32)
        o_ref[...] = acc

    return _rowcall(body, name, r, tr, [pl.BlockSpec((n, tr, c_), lambda i: (0, i, 0))], _row(tr, c_),
                    jax.ShapeDtypeStruct((r, c_), F32), (recv,))


def _prereduce(gfull, *, name):
    return _pair_add(gfull, _run_comm(_pair_comm(gfull), name=name + "_pair")[0], name=name + "_padd")


PACK_ROWS = 16


def _pack(parts):
    flat = [p.reshape(-1).astype(F32) for p in parts]
    n = sum(f.shape[0] for f in flat)
    unit = PACK_ROWS * LANES
    total = -(-n // unit) * unit
    if total > n:
        flat.append(jnp.zeros((total - n,), F32))
    where, off = [], 0
    for p in parts:
        where.append((off, p.shape))
        off += p.size
    return jnp.concatenate(flat).reshape(total // LANES, LANES), where


def _unpack(flat, where):
    v = flat.reshape(-1)
    return [v[off:off + _size(shape)].reshape(shape) for off, shape in where]


def _size(shape):
    n = 1
    for d in shape:
        n *= d
    return n


def _sample_step(x, target, mods, h1, w_in_p, bufs, sp):
    s, d = x.shape
    u = d // 2
    h = u // HEAD_DIM
    g = h // HEADS_PER_GROUP
    pw = w_in_p.shape[1]
    din = 6 * u + 2 * h
    dff_ = bufs["w_up"].shape[2] * N_CHIPS
    shift1, scale1, gate1, shift2, scale2, gate2 = mods

    a_f = -jnp.exp(sp["ssm_a_log_f"].reshape(-1))
    a_b = -jnp.exp(sp["ssm_a_log_b"].reshape(-1))
    pad_l = LANES - 2 * h
    a_row = jnp.pad(jnp.concatenate([a_f, a_b]), (0, pad_l)).reshape(1, LANES)
    bias_row = jnp.pad(jnp.concatenate([sp["ssm_dt_bias_f"].reshape(-1), sp["ssm_dt_bias_b"].reshape(-1)]),
                       (0, pad_l)).reshape(1, LANES)
    a_col = jnp.broadcast_to(jnp.stack([a_f, a_b])[:, :, None], (2, h, LANES))
    d_e = jnp.repeat(sp["ssm_d"].reshape(-1), HEAD_DIM).reshape(1, u)
    conv_w, conv_b = sp["ssm_conv_w"], sp["ssm_conv_b"].reshape(1, 2 * u)
    sc_conv_w = sp["sc_conv_w"]
    ssm_norm_w, sc_norm_w = sp["ssm_norm_w"].reshape(1, u), sp["sc_norm_w"].reshape(1, u)
    ln1_g, ln1_b = sp["ln1_g"].reshape(1, d), sp["ln1_b"].reshape(1, d)
    ln2_g, ln2_b = sp["ln2_g"].reshape(1, d), sp["ln2_b"].reshape(1, d)

    e = 1.0 / DMA_CHUNKS
    proj, (w_up_b, w_out_blk) = _matmul(
        h1, w_in_p, name="mm_proj", tn=1280,
        comm=_merge_comms(_gather_comm(bufs["w_up"], (0, 3 * e)), _gather_comm(bufs["w_out"])))
    w_out = w_out_blk.reshape(d, d)
    xbc, (w_down_b,) = _conv_silu_fwd(proj, conv_w, conv_b, s, u, comm=_gather_comm(bufs["w_down"], (0, e)))
    (dt, cum, dt_e, cum_e), (w_up_b,) = _dt_prep(proj, bias_row, a_row, s, u, h, comm=_gather_comm(w_up_b, (3 * e, 4 * e)))
    cum_t = jnp.stack([cum[:, :h].T, cum[:, h:2 * h].T])
    dt_t = jnp.stack([dt[:, :h].T, dt[:, h:2 * h].T])
    (y2, states), (w_up_b,) = _ssd_fwd(xbc, dt_e, cum_e, cum_t, s, h, g, comm=_gather_comm(w_up_b, (4 * e, 6 * e)))
    y_ssm = _gate_norm_fwd(y2, xbc, proj, d_e, ssm_norm_w, s, u, g)
    y_sc = _shortconv_fwd(proj, sc_conv_w, sc_norm_w, s, u)
    ymix = jnp.concatenate([y_ssm, y_sc], axis=1)
    mix, (w_up_b,) = _matmul(ymix, w_out, name="mm_mix", comm=_gather_comm(w_up_b, (6 * e, 7 * e)))
    (x1, h2), (w_up_blk,) = _ln1_fwd(x, mix, gate1, ln1_g, ln1_b, scale2, shift2, comm=_gather_comm(w_up_b, (7 * e, 1)))
    (up, ff), (w_down_blk,) = _matmul(h2, w_up_blk, name="mm_up", b_blocks=N_CHIPS, epilogue="relu2",
                                      comm=_gather_comm(w_down_b, (e, 1)))
    w_down = w_down_blk.reshape(dff_, d)
    f = _matmul(ff, w_down, name="mm_down")
    df, dr2, loss, dg2, db2, dgate2 = _ln2_loss_bwd(x1, f, target, gate2, ln2_g, ln2_b)

    gw_down = _matmul(ff, df, name="mm_gw_down", ta=True, out_dtype=BF16).reshape(N_CHIPS, dff_ // N_CHIPS, d)
    du, (peer,) = _matmul(df, w_down, name="mm_dff", tb=True, out_dtype=BF16, epilogue="relu2_bwd", extra=up,
                          comm=_pair_comm(gw_down))
    pre_down = _pair_add(gw_down, peer, name="rs_w_down_padd")
    gw_up, (rv_down,) = _matmul(h2, du, name="mm_gw_up", ta=True, out_dtype=BF16, out_blocks=N_CHIPS,
                                comm=_scatter_comm(pre_down, (0, 0.5)))
    dh2, (rv_down, peer) = _matmul(du, w_up_blk, name="mm_dh2", tb=True, b_blocks=N_CHIPS,
                                   comm=_merge_comms(_scatter_comm(pre_down, (0.5, 1), recv=rv_down), _pair_comm(gw_up)))
    pre_up = _pair_add(gw_up, peer, name="rs_w_up_padd")
    (dmix, dxa, dscale2, dshift2, dg1, db1, dgate1), (rv_up,) = _ln1_bwd(
        dh2, dr2, x1, x, mix, scale2, gate1, ln1_g, comm=_scatter_comm(pre_up, (0, e)))
    gw_out, (rv_up,) = _matmul(ymix, dmix, name="mm_gw_out", ta=True, out_dtype=BF16,
                               comm=_scatter_comm(pre_up, (e, 2 * e), recv=rv_up))
    gw_out = gw_out.reshape(N_CHIPS, d // N_CHIPS, d)
    dymix, (rv_up, peer) = _matmul(dmix, w_out, name="mm_dymix", tb=True,
                                   comm=_merge_comms(_scatter_comm(pre_up, (2 * e, 4 * e), recv=rv_up), _pair_comm(gw_out)))
    pre_out = _pair_add(gw_out, peer, name="rs_w_out_padd")
    dys, dproj, dxs, dnw, dd_e = _gate_norm_bwd(dymix, y2, xbc, proj, d_e, ssm_norm_w, pw, s, u, g)
    (dx2, dbb, dcc, ddt_t, da), (rv_up,) = _ssd_bwd(xbc, dt_e, cum_e, cum_t, dt_t, a_col, dys, states, s, h, g,
                                                    comm=_scatter_comm(pre_up, (4 * e, 1), recv=rv_up))
    dxbc = jnp.concatenate([dx2[0] + dx2[1] + dxs, dbb[0] + dbb[1], dcc[0] + dcc[1]], axis=1)
    (dproj, dcw, dcb), (rv_out,) = _conv_silu_bwd(proj, dxbc, conv_w, conv_b, dproj, s, u, comm=_scatter_comm(pre_out))
    ddt = jnp.pad(jnp.concatenate([ddt_t[0].T, ddt_t[1].T], axis=1), ((0, 0), (0, pad_l)))
    dproj, dbias = _dt_bwd(ddt, proj, bias_row, dproj, s, u, h)
    dproj, dscw, dscnw = _shortconv_bwd(dymix, proj, sc_conv_w, sc_norm_w, dproj, s, u)
    gp = _matmul(h1, dproj, name="mm_gw_in", ta=True, out_dtype=BF16, tn=1280)
    pre_in = _from_p_layout(_prereduce(gp[None], name="rs_w_in")[0], u, h, N_CHIPS)
    dh1, (rv_in,) = _matmul(dproj, w_in_p, name="mm_dh1", tb=True, tk=2560, comm=_scatter_comm(pre_in))
    grad_x, dscale1, dshift1 = _dx_final(dxa, dh1, x, scale1)
    big = {}
    half_down = _sum_into_half(pre_down, rv_down, name="rs_w_down_sum")
    half_up, (big["w_down"],) = _sum_into_half(pre_up, rv_up, name="rs_w_up_sum", comm=_swap_comm(half_down))
    half_out, (big["w_up"],) = _sum_into_half(pre_out, rv_out, name="rs_w_out_sum", comm=_swap_comm(half_up))
    half_in, (big["w_out"],) = _sum_into_half(pre_in, rv_in, name="rs_w_in_sum", comm=_swap_comm(half_out))
    big["w_in"] = _run_comm(_swap_comm(half_in), name="rs_w_in_swap")[0]

    small = {
        "dmod": jnp.concatenate([dshift1, dscale1, dgate1, dshift2, dscale2, dgate2], axis=1),
        "ssm_conv_b": dcb,
        "ssm_dt_bias_f": dbias[0, :h], "ssm_dt_bias_b": dbias[0, h:2 * h],
        "ssm_a_log_f": jnp.sum(da[0], axis=1) * a_f, "ssm_a_log_b": jnp.sum(da[1], axis=1) * a_b,
        "ssm_d": jnp.sum(dd_e.reshape(h, HEAD_DIM), axis=1),
        "ssm_norm_w": dnw, "sc_norm_w": dscnw,
        "ln1_g": dg1, "ln1_b": db1, "ln2_g": dg2, "ln2_b": db2,
        "ssm_conv_w": dcw[:SSM_CONV], "sc_conv_w": dscw[:SC_CONV],
    }
    return loss, grad_x, big, small


WEIGHTS = ['w_ada', 'b_ada', 'w_in', 'ssm_conv_w', 'ssm_conv_b', 'ssm_dt_bias_f', 'ssm_dt_bias_b', 'ssm_a_log_f',
           'ssm_a_log_b', 'ssm_d', 'ssm_norm_w', 'sc_conv_w', 'sc_norm_w', 'w_out', 'ln1_g', 'ln1_b', 'w_up', 'w_down',
           'ln2_g', 'ln2_b']
BIG = ('w_ada', 'w_in', 'w_out', 'w_up', 'w_down')
SMALL = tuple(n for n in WEIGHTS if n not in BIG)
SMALL_SHARDED = ('ssm_conv_w', 'sc_conv_w')


def _p_layout_width(u):
    return -(-(6 * u + LANES) // 512) * 512


def _p_segments(u, h):
    return [((0, u), P_Z * u), ((u, 3 * u), P_X * u), ((3 * u, 3 * u + 2 * h), 6 * u),
            ((3 * u + 2 * h, 6 * u + 2 * h), P_H * u)]


def _to_p_layout(blocks, u, h, pw):
    nblk, d, w = blocks.shape
    parts = []
    for (lo, hi), _ in sorted(_p_segments(u, h), key=lambda t: t[1]):
        for k in range(nblk):
            a, b = max(lo, k * w), min(hi, (k + 1) * w)
            if a < b:
                parts.append(blocks[k][:, a - k * w:b - k * w])
    parts.append(jnp.zeros((d, pw - nblk * w), blocks.dtype))
    return jnp.concatenate(parts, axis=1)


def _from_p_layout(gp, u, h, nblk):
    w = (6 * u + 2 * h) // nblk
    blocks = []
    for k in range(nblk):
        parts = []
        for (lo, hi), poff in _p_segments(u, h):
            a, b = max(lo, k * w), min(hi, (k + 1) * w)
            if a < b:
                parts.append(gp[:, poff + a - lo:poff + b - lo])
        blocks.append(jnp.concatenate(parts, axis=1))
    return jnp.stack(blocks)


def kernel(x, c, w_ada, b_ada, w_in, ssm_conv_w, ssm_conv_b, ssm_dt_bias_f, ssm_dt_bias_b, ssm_a_log_f, ssm_a_log_b, ssm_d, ssm_norm_w, sc_conv_w, sc_norm_w, w_out, ln1_g, ln1_b, w_up, w_down, ln2_g, ln2_b, loss_target, m_w_ada, m_b_ada, m_w_in, m_ssm_conv_w, m_ssm_conv_b, m_ssm_dt_bias_f, m_ssm_dt_bias_b, m_ssm_a_log_f, m_ssm_a_log_b, m_ssm_d, m_ssm_norm_w, m_sc_conv_w, m_sc_norm_w, m_w_out, m_ln1_g, m_ln1_b, m_w_up, m_w_down, m_ln2_g, m_ln2_b, v_w_ada, v_b_ada, v_w_in, v_ssm_conv_w, v_ssm_conv_b, v_ssm_dt_bias_f, v_ssm_dt_bias_b, v_ssm_a_log_f, v_ssm_a_log_b, v_ssm_d, v_ssm_norm_w, v_sc_conv_w, v_sc_norm_w, v_w_out, v_ln1_g, v_ln1_b, v_w_up, v_w_down, v_ln2_g, v_ln2_b):
    given = dict(locals())
    w = {n: given[n][0] for n in WEIGHTS}
    m = {n: given["m_" + n][0] for n in WEIGHTS}
    v = {n: given["v_" + n][0] for n in WEIGHTS}
    xs, tgt = x[0], loss_target[0]
    s, d = xs.shape
    u = d // 2
    h = u // HEAD_DIM
    nmod = N_MOD * d
    nmod_loc = nmod // N_CHIPS
    ax, ay, ac = lax.axis_index("x"), lax.axis_index("y"), lax.axis_index("c")
    chip = 2 * ax + ay
    me = 2 * chip + ac

    pay1, where1 = _pack([c[0], w["ssm_conv_w"], w["sc_conv_w"]])
    g1 = _allgather_small(pay1, name="ag_inputs").reshape(N_DEV, -1)
    per_dev = [_unpack(g1[k], where1) for k in range(N_DEV)]
    c_all = jnp.stack([p[0] for p in per_dev])
    ssm_conv_w_full = jnp.concatenate([per_dev[2 * k][1] for k in range(N_CHIPS)], axis=1)
    sc_conv_w_full = jnp.concatenate([per_dev[2 * k][2] for k in range(N_CHIPS)], axis=1)

    sc_all = _silu(c_all)
    sc16 = jnp.pad(sc_all, ((0, 16 - N_DEV), (0, 0)))
    b_loc = lax.dynamic_slice(w["b_ada"], (chip * nmod_loc,), (nmod_loc,))
    mod_loc = _matmul(sc16, w["w_ada"], name="mm_mod")[:N_DEV] + b_loc[None, :]
    pay2, where2 = _pack([mod_loc])
    g2 = _allgather_small(pay2, name="ag_mod").reshape(N_DEV, -1)
    mod_blocks = jnp.stack([_unpack(g2[2 * k], where2)[0] for k in range(N_CHIPS)])
    mod_mine = lax.dynamic_index_in_dim(mod_blocks, me, axis=1, keepdims=False).reshape(N_MOD, 1, d)
    mods = [mod_mine[k] for k in range(N_MOD)]

    din = w["w_in"].shape[1] * N_CHIPS

    e = 1.0 / DMA_CHUNKS
    g_in = _cast_into_block(w["w_in"], name="cast_w_in")
    bufs = {"w_out": _cast_into_block(w["w_out"], name="cast_w_out")}
    bufs["w_up"], (g_in,) = _cast_into_block(w["w_up"], name="cast_w_up", comm=_gather_comm(g_in, (0, e)))
    bufs["w_down"], (g_in,) = _cast_into_block(w["w_down"], name="cast_w_down", comm=_gather_comm(g_in, (e, 2 * e)))
    h1, (g_in,) = _modulate(xs, mods[1], mods[0], name="modulate1", comm=_gather_comm(g_in, (2 * e, 3 * e)))
    g_in = _gather_weight(g_in, name="gather_w_in", part=(3 * e, 1))
    w_in_p = _to_p_layout(g_in, u, h, _p_layout_width(u))

    sp = {n: w[n] for n in SMALL}
    sp["ssm_conv_w"], sp["sc_conv_w"] = ssm_conv_w_full, sc_conv_w_full
    loss_loc, grad_x, big, small = _sample_step(xs, tgt, mods, h1, w_in_p, bufs, sp)

    small_names = ["dmod"] + [n for n in SMALL if n != "b_ada"]
    pay3, where3 = _pack([loss_loc] + [small[n] for n in small_names])
    g3 = _allgather_small(pay3, name="ag_small_grads")
    tot = _unpack(_sum_slots(g3.reshape(N_DEV, -1, LANES), name="sum_small_grads"), where3)
    loss = tot[0].reshape(())
    gsum = dict(zip(small_names, tot[1:]))
    dmod_all = jnp.stack([_unpack(g3.reshape(N_DEV, -1)[k], where3)[1].reshape(-1) for k in range(N_DEV)])

    grads = {}
    grads["b_ada"] = gsum["dmod"].reshape(-1)
    for n in SMALL:
        if n in SMALL_SHARDED:
            loc = w[n].shape[1]
            grads[n] = lax.dynamic_slice_in_dim(gsum[n], chip * loc, loc, axis=1)
        elif n != "b_ada":
            grads[n] = gsum[n].reshape(w[n].shape)

    delta, new_m, new_v = {}, {}, {}
    dm_loc = lax.dynamic_slice_in_dim(dmod_all, chip * nmod_loc, nmod_loc, axis=1)
    grads["w_ada"], delta["w_ada"], new_m["w_ada"], new_v["w_ada"] = _adam_outer(
        w["w_ada"], sc16.T, jnp.pad(dm_loc, ((0, 16 - N_DEV), (0, 0))), m["w_ada"], v["w_ada"], name="adam_w_ada")
    for n in ("w_in", "w_out", "w_up", "w_down"):
        grads[n], delta[n], new_m[n], new_v[n] = _adam(w[n], big[n], m[n], v[n], name="adam_" + n, emit_grad=True)
    pw_, where_s = _pack([w[n] for n in SMALL])
    pg_, _ = _pack([grads[n] for n in SMALL])
    pm_, _ = _pack([m[n] for n in SMALL])
    pv_, _ = _pack([v[n] for n in SMALL])
    sd, sm, sv = _adam(pw_, pg_, pm_, pv_, name="adam_small")
    for n, a, b_, c_ in zip(SMALL, _unpack(sd, where_s), _unpack(sm, where_s), _unpack(sv, where_s)):
        delta[n], new_m[n], new_v[n] = a, b_, c_

    def lead(t):
        return t[None]

    return (loss, grad_x[None], *[lead(grads[n].reshape(w[n].shape)) for n in WEIGHTS],
            *[lead(delta[n]) for n in WEIGHTS], *[lead(new_m[n]) for n in WEIGHTS], *[lead(new_v[n]) for n in WEIGHTS])
```

```python
import functools

import jax
import jax.numpy as jnp
from jax import lax
from jax.experimental import pallas as pl
from jax.experimental.pallas import tpu as pltpu

F32 = jnp.float32
BF16 = jnp.bfloat16

CHUNK = 128
HEAD_DIM = 64
STATE = 128
HEADS_PER_GROUP = 4
SC_GROUP_WIDTH = 128
SSM_CONV = 5
SC_CONV = 3
N_MOD = 6
DEEPNORM_ALPHA = 2.0 ** 0.25
LN_EPS = 1e-5
RMS_EPS = 1e-5
ADAM_LR = 0.001
ADAM_B1 = 0.9
ADAM_B2 = 0.999
ADAM_EPS = 1e-08
ADAM_WD = 0.01
ADAM_STEP = 10
N_CHIPS = 4
N_DEV = 8
LANES = 128
SUBLANES = 8
HALO = 8
VMEM_LIMIT = 56 * 1024 * 1024
MESH = pl.DeviceIdType.MESH


def _params(sem=None):
    return pltpu.CompilerParams(dimension_semantics=sem, vmem_limit_bytes=VMEM_LIMIT)


def _pick(n, target, mult=LANES):
    best = None
    t = mult
    while t <= min(n, target):
        if n % t == 0:
            best = t
        t += mult
    return best if best is not None else n


ROW_TILE_BYTES = 2 << 20


def _tile_rows(s, width, mult=SUBLANES):
    return _pick(s, max(mult, ROW_TILE_BYTES // (4 * width)), mult)


def _sigmoid(v):
    return 1.0 / (1.0 + jnp.exp(-v))


def _silu(v):
    return v * _sigmoid(v)


def _dsilu(v):
    s = _sigmoid(v)
    return s * (1.0 + v * (1.0 - s))


def _softplus(v):
    e = jnp.exp(-jnp.abs(v))
    return jnp.maximum(v, 0.0) + jnp.where(e < 1e-4, e - 0.5 * e * e, jnp.log(1.0 + e))


def _dot(a, b):
    return jnp.dot(a, b, preferred_element_type=F32)


def _dot_nt(a, b):
    return lax.dot_general(a, b, (((1,), (1,)), ((), ())), preferred_element_type=F32)


def _dot_tn(a, b):
    return lax.dot_general(a, b, (((0,), (0,)), ((), ())), preferred_element_type=F32)


def _split3(v):
    hi = v.astype(BF16)
    r1 = v - hi.astype(F32)
    mid = r1.astype(BF16)
    lo = (r1 - mid.astype(F32)).astype(BF16)
    return hi, mid, lo


def _dot3_r(v, onehot):
    hi, mid, lo = _split3(v)
    return _dot(hi, onehot) + _dot(mid, onehot) + _dot(lo, onehot)


def _dot3_l(onehot, v):
    hi, mid, lo = _split3(v)
    return _dot(onehot, hi) + _dot(onehot, mid) + _dot(onehot, lo)


MATMUL_VMEM_BUDGET = 44 * 1024 * 1024


def _matmul(a, b, *, name, ta=False, tb=False, out_dtype=F32, b_blocks=1, out_blocks=1,
            tm=1024, tn=1024, tk=4096, comm=None, epilogue=None, extra=None):
    if ta:
        K, M = a.shape
    else:
        M, K = a.shape
    if b_blocks > 1:
        nb, r_, c_ = b.shape
        if tb:
            N, K2 = r_, c_ * nb
        else:
            K2, N = r_, c_ * nb
    else:
        if tb:
            N, K2 = b.shape
        else:
            K2, N = b.shape
    assert K == K2, (a.shape, b.shape, ta, tb)
    assert not (ta and tb)
    n_unit = N // b_blocks if (b_blocks > 1 and not tb) else N
    n_unit = min(n_unit, N // out_blocks)
    tn = _pick(n_unit, tn)
    k_unit = K // b_blocks if (b_blocks > 1 and tb) else K
    tk = _pick(k_unit, tk)
    tile_bytes = {None: jnp.dtype(out_dtype).itemsize, "relu2": 6, "relu2_bwd": 6, "dx": 12}[epilogue]

    def vmem_need(tm_):
        need = 2 * (tm_ * tk * a.dtype.itemsize + tk * tn * b.dtype.itemsize) + 2 * tm_ * tn * tile_bytes
        return need + (tm_ * tn * 4 if K > tk else 0)

    tm = _pick(M, tm)
    while vmem_need(tm) > MATMUL_VMEM_BUDGET and tm % 2 == 0 and tm // 2 >= LANES:
        tm //= 2
    gm, gn, gk = M // tm, N // tn, K // tk

    if ta:
        a_spec = pl.BlockSpec((tk, tm), lambda i, j, k: (k, i))
    else:
        a_spec = pl.BlockSpec((tm, tk), lambda i, j, k: (i, k))
    if b_blocks > 1 and not tb:
        per = (N // b_blocks) // tn
        b_spec = pl.BlockSpec((None, tk, tn), lambda i, j, k: (j // per, k, j % per))
    elif b_blocks > 1 and tb:
        per = (K // b_blocks) // tk
        b_spec = pl.BlockSpec((None, tn, tk), lambda i, j, k: (k // per, j, k % per))
    elif tb:
        b_spec = pl.BlockSpec((tn, tk), lambda i, j, k: (j, k))
    else:
        b_spec = pl.BlockSpec((tk, tn), lambda i, j, k: (k, j))
    if out_blocks > 1:
        per_o = (N // out_blocks) // tn
        o_spec = pl.BlockSpec((None, tm, tn), lambda i, j, k: (j // per_o, i, j % per_o))
        o_shape = jax.ShapeDtypeStruct((out_blocks, M, N // out_blocks), out_dtype)
    else:
        o_spec = pl.BlockSpec((tm, tn), lambda i, j, k: (i, j))
        o_shape = jax.ShapeDtypeStruct((M, N), out_dtype)

    in_specs, args = [a_spec, b_spec], [a, b]
    out_specs, out_shape = [o_spec], [o_shape]
    if epilogue == "relu2":
        assert out_blocks == 1 and out_dtype == F32
        out_specs.append(o_spec)
        out_shape.append(jax.ShapeDtypeStruct((M, N), BF16))
    elif epilogue == "relu2_bwd":
        assert out_blocks == 1 and out_dtype == BF16
        in_specs.append(o_spec)
        args.append(extra)
    elif epilogue == "dx":
        assert out_blocks == 1 and out_dtype == F32
        col_spec = pl.BlockSpec((1, tn), lambda i, j, k: (0, j))
        in_specs += [o_spec, o_spec, col_spec]
        args += list(extra)
        out_specs += [col_spec, col_spec]
        out_shape += [jax.ShapeDtypeStruct((1, N), F32)] * 2
    n_in = len(in_specs)
    grid = (gm, gn, gk)
    row_axis = 0
    if epilogue == "dx":
        def swapped(spec):
            return pl.BlockSpec(spec.block_shape, lambda j, i, k, f=spec.index_map: f(i, j, k))
        in_specs, out_specs = [swapped(sp) for sp in in_specs], [swapped(sp) for sp in out_specs]
        grid, row_axis = (gn, gm, gk), 1

    def write(refs, p):
        o_ref = refs[n_in]
        if epilogue == "relu2":
            o_ref[...] = p
            r = jnp.maximum(p, 0.0)
            refs[n_in + 1][...] = (r * r).astype(BF16)
        elif epilogue == "relu2_bwd":
            o_ref[...] = (p * 2.0 * jnp.maximum(refs[2][...], 0.0)).astype(BF16)
        elif epilogue == "dx":
            o_ref[...] = refs[2][...] + p * (1.0 + refs[4][...])
            s1 = jnp.sum(p * refs[3][...], axis=0, keepdims=True)
            s2 = jnp.sum(p, axis=0, keepdims=True)
            first = pl.program_id(row_axis) == 0

            @pl.when(first)
            def _():
                refs[n_in + 1][...] = s1
                refs[n_in + 2][...] = s2

            @pl.when(jnp.logical_not(first))
            def _():
                refs[n_in + 1][...] += s1
                refs[n_in + 2][...] += s2
        else:
            o_ref[...] = p.astype(out_dtype)

    def body(*refs):
        av = refs[0][...].astype(BF16)
        bv = refs[1][...].astype(BF16)
        p = _dot_tn(av, bv) if ta else (_dot_nt(av, bv) if tb else _dot(av, bv))
        if gk == 1:
            write(refs, p)
            return
        acc = refs[-1]
        k = pl.program_id(2)

        @pl.when(k == 0)
        def _():
            acc[...] = p

        @pl.when(jnp.logical_and(k > 0, k < gk - 1))
        def _():
            acc[...] += p

        @pl.when(k == gk - 1)
        def _():
            write(refs, acc[...] + p)

    scratch = [pltpu.VMEM((tm, tn), F32)] if gk > 1 else []
    if comm is not None:
        outs, landed = _comm_call(body, name=name, grid=grid, in_specs=in_specs, out_specs=out_specs,
                                  out_shape=out_shape, scratch_shapes=scratch, args=args, comm=comm)
        return (outs[0] if len(outs) == 1 else tuple(outs)), landed
    outs = pl.pallas_call(
        body, name=name, grid=grid, in_specs=in_specs, out_specs=out_specs,
        out_shape=out_shape, scratch_shapes=scratch,
        compiler_params=_params(("parallel", "arbitrary" if epilogue == "dx" else "parallel", "arbitrary")),
    )(*args)
    return outs[0] if len(outs) == 1 else tuple(outs)


def _comm_call(body, *, name, grid, in_specs, out_specs, out_shape, scratch_shapes, args, comm, aliases=None):
    n_in, n_out, n_scr = len(in_specs), len(out_shape), len(scratch_shapes)
    c_in, c_out = list(comm["inputs"]), list(comm["out_shape"])
    nci, nco = len(c_in), len(c_out)
    hbm = pl.BlockSpec(memory_space=pl.ANY)

    def body2(*refs):
        ins, cins = refs[:n_in], refs[n_in:n_in + nci]
        o0 = n_in + nci
        outs, couts = refs[o0:o0 + n_out], refs[o0 + n_out:o0 + n_out + nco]
        s0 = o0 + n_out + nco
        scr, cscr = refs[s0:s0 + n_scr], refs[s0 + n_scr:]
        first = functools.reduce(jnp.logical_and, [pl.program_id(a) == 0 for a in range(len(grid))])
        last = functools.reduce(jnp.logical_and, [pl.program_id(a) == grid[a] - 1 for a in range(len(grid))])

        @pl.when(first)
        def _():
            comm["start"](cins, couts, cscr)

        body(*ins, *outs, *scr)

        @pl.when(last)
        def _():
            comm["finish"](cins, couts, cscr)

    res = pl.pallas_call(
        body2, name=name, grid=grid, in_specs=list(in_specs) + [hbm] * nci, out_specs=list(out_specs) + [hbm] * nco,
        out_shape=list(out_shape) + c_out, scratch_shapes=list(scratch_shapes) + list(comm["scratch"]),
        input_output_aliases={**(aliases or {}), **{n_in + k: n_out + v for k, v in comm.get("aliases", {}).items()}},
        compiler_params=_params(("arbitrary",) * len(grid)),
    )(*args, *c_in)
    return res[:n_out], res[n_out:]


def _row(tr, w, blk=0):
    return pl.BlockSpec((tr, w), lambda i: (i, blk))


def _full(shape):
    nd = len(shape)
    return pl.BlockSpec(shape, lambda i: (0,) * nd)


def _halo_specs(s, tr, w, blk=0):
    per = tr // HALO
    last = s // HALO - 1
    return [
        pl.BlockSpec((HALO, w), lambda i: (jnp.maximum(i * per - 1, 0), blk)),
        pl.BlockSpec((tr, w), lambda i: (i, blk)),
        pl.BlockSpec((HALO, w), lambda i: (jnp.minimum((i + 1) * per, last), blk)),
    ]


def _ext(prev_ref, cur_ref, next_ref, s, tr):
    i = pl.program_id(0)
    e = jnp.concatenate([prev_ref[...].astype(F32), cur_ref[...].astype(F32), next_ref[...].astype(F32)], axis=0)
    rid = i * tr - HALO + lax.broadcasted_iota(jnp.int32, e.shape, 0)
    return jnp.where((rid >= 0) & (rid < s), e, 0.0)


def _valid_rows(shape, s, tr):
    i = pl.program_id(0)
    rid = i * tr - HALO + lax.broadcasted_iota(jnp.int32, shape, 0)
    return (rid >= 0) & (rid < s)


def _shift(e, k):
    if k == 0:
        return e
    n = e.shape[0]
    return pltpu.roll(e, (n - k) % n, 0)


def _acc_rows(ref, v):
    s = jnp.sum(v, axis=0, keepdims=True)

    @pl.when(pl.program_id(0) == 0)
    def _():
        ref[...] = s

    @pl.when(pl.program_id(0) > 0)
    def _():
        ref[...] += s


def _rowcall(body, name, s, tr, in_specs, out_specs, out_shape, args, comm=None, aliases=None):
    aliases = aliases or {}
    if comm is not None:
        single = not isinstance(out_shape, (list, tuple))
        outs, landed = _comm_call(body, name=name, grid=(s // tr,), in_specs=in_specs,
                                  out_specs=[out_specs] if single else out_specs,
                                  out_shape=[out_shape] if single else out_shape, scratch_shapes=[], args=args, comm=comm,
                                  aliases=aliases)
        return (outs[0] if single else outs), landed
    return pl.pallas_call(
        body, name=name, grid=(s // tr,), in_specs=in_specs, out_specs=out_specs, out_shape=out_shape,
        input_output_aliases=aliases, compiler_params=_params(("arbitrary",)),
    )(*args)


def _modulate(x, scale, shift, *, name, comm=None):
    s, d = x.shape
    tr = _tile_rows(s, d)

    def body(x_ref, sc_ref, sh_ref, o_ref):
        o_ref[...] = (x_ref[...] * (1.0 + sc_ref[...]) + sh_ref[...]).astype(BF16)

    return _rowcall(body, name, s, tr, [_row(tr, d), _full((1, d)), _full((1, d))], _row(tr, d),
                    jax.ShapeDtypeStruct((s, d), BF16), (x, scale, shift), comm=comm)


def _ln_stats(r):
    mu = jnp.mean(r, axis=-1, keepdims=True)
    xc = r - mu
    var = jnp.mean(xc * xc, axis=-1, keepdims=True)
    rstd = lax.rsqrt(var + LN_EPS)
    return xc * rstd, rstd


def _ln1_fwd(x, mix, gate, g, b, scale2, shift2, comm=None):
    s, d = x.shape
    tr = _tile_rows(s, d)

    def body(x_ref, m_ref, gt_ref, g_ref, b_ref, sc_ref, sh_ref, x1_ref, h2_ref):
        r = DEEPNORM_ALPHA * x_ref[...] + (1.0 + gt_ref[...]) * m_ref[...]
        xh, _ = _ln_stats(r)
        x1 = xh * g_ref[...] + b_ref[...]
        x1_ref[...] = x1
        h2_ref[...] = (x1 * (1.0 + sc_ref[...]) + sh_ref[...]).astype(BF16)

    v = _full((1, d))
    return _rowcall(body, "ln1_fwd", s, tr, [_row(tr, d), _row(tr, d), v, v, v, v, v],
                    [_row(tr, d), _row(tr, d)],
                    [jax.ShapeDtypeStruct((s, d), F32), jax.ShapeDtypeStruct((s, d), BF16)],
                    (x, mix, gate, g, b, scale2, shift2), comm=comm)


def _ln2_loss_bwd(x1, f, target, gate, g, b):
    s, d = x1.shape
    tr = _tile_rows(s, d)

    def body(x1_ref, f_ref, t_ref, gt_ref, g_ref, b_ref, df_ref, dr_ref, loss_ref, dg_ref, db_ref, dgt_ref):
        fv = f_ref[...]
        r = DEEPNORM_ALPHA * x1_ref[...] + (1.0 + gt_ref[...]) * fv
        xh, rstd = _ln_stats(r)
        y = xh * g_ref[...] + b_ref[...]
        err = y - t_ref[...]
        _acc_rows(loss_ref, 0.5 * jnp.mean(err * err, axis=-1, keepdims=True))
        dy = err * (1.0 / d)
        _acc_rows(dg_ref, dy * xh)
        _acc_rows(db_ref, dy)
        dxh = dy * g_ref[...]
        dr = rstd * (dxh - jnp.mean(dxh, axis=-1, keepdims=True) - xh * jnp.mean(dxh * xh, axis=-1, keepdims=True))
        dr_ref[...] = dr
        df_ref[...] = ((1.0 + gt_ref[...]) * dr).astype(BF16)
        _acc_rows(dgt_ref, dr * fv)

    v = _full((1, d))
    one = _full((1, 1))
    return _rowcall(body, "ln2_loss_bwd", s, tr, [_row(tr, d), _row(tr, d), _row(tr, d), v, v, v],
                    [_row(tr, d), _row(tr, d), one, v, v, v],
                    [jax.ShapeDtypeStruct((s, d), BF16), jax.ShapeDtypeStruct((s, d), F32),
                     jax.ShapeDtypeStruct((1, 1), F32)] + [jax.ShapeDtypeStruct((1, d), F32)] * 3,
                    (x1, f, target, gate, g, b))


def _ln1_bwd(dh2, dr2, x1, x, mix, scale2, gate1, g1, comm=None):
    s, d = x.shape
    tr = _tile_rows(s, d)

    def body(dh_ref, dr2_ref, x1_ref, x_ref, m_ref, sc_ref, gt_ref, g_ref,
             dm_ref, dxa_ref, dsc_ref, dsh_ref, dg_ref, db_ref, dgt_ref):
        dh = dh_ref[...]
        _acc_rows(dsc_ref, dh * x1_ref[...])
        _acc_rows(dsh_ref, dh)
        dy = dh * (1.0 + sc_ref[...]) + DEEPNORM_ALPHA * dr2_ref[...]
        mv = m_ref[...]
        r = DEEPNORM_ALPHA * x_ref[...] + (1.0 + gt_ref[...]) * mv
        xh, rstd = _ln_stats(r)
        _acc_rows(dg_ref, dy * xh)
        _acc_rows(db_ref, dy)
        dxh = dy * g_ref[...]
        dr = rstd * (dxh - jnp.mean(dxh, axis=-1, keepdims=True) - xh * jnp.mean(dxh * xh, axis=-1, keepdims=True))
        dm_ref[...] = ((1.0 + gt_ref[...]) * dr).astype(BF16)
        dxa_ref[...] = DEEPNORM_ALPHA * dr
        _acc_rows(dgt_ref, dr * mv)

    v = _full((1, d))
    return _rowcall(body, "ln1_bwd", s, tr, [_row(tr, d)] * 5 + [v, v, v],
                    [_row(tr, d), _row(tr, d), v, v, v, v, v],
                    [jax.ShapeDtypeStruct((s, d), BF16), jax.ShapeDtypeStruct((s, d), F32)]
                    + [jax.ShapeDtypeStruct((1, d), F32)] * 5,
                    (dh2, dr2, x1, x, mix, scale2, gate1, g1), comm=comm)


P_X, P_BC, P_Z, P_H, P_B, P_C = range(6)
HBM_REF = pl.BlockSpec(memory_space=pl.ANY)


def _conv_silu_fwd(proj, conv_w, conv_b, s, u, comm=None):
    tr = _tile_rows(s, u)
    w = 2 * u
    half = SSM_CONV // 2

    def body(p0, c0, n0, p1, c1, n1, w_ref, b_ref, o_ref):
        for blk, (pr, cr, nr) in enumerate(((p0, c0, n0), (p1, c1, n1))):
            e = _ext(pr, cr, nr, s, tr)
            wv = w_ref[:, blk * u:(blk + 1) * u]
            acc = jnp.zeros_like(e)
            for k in range(SSM_CONV):
                acc = acc + _shift(e, k - half) * wv[k:k + 1, :]
            pre = acc[HALO:HALO + tr] + b_ref[:, blk * u:(blk + 1) * u]
            o_ref[:, blk * u:(blk + 1) * u] = _silu(pre)

    in_specs = _halo_specs(s, tr, u, P_X) + _halo_specs(s, tr, u, P_BC) + [_full((SSM_CONV, w)), _full((1, w))]
    return _rowcall(body, "conv_silu_fwd", s, tr, in_specs, _row(tr, w), jax.ShapeDtypeStruct((s, w), F32),
                    (proj,) * 6 + (conv_w, conv_b), comm=comm)


def _conv_silu_bwd(proj, dxbc, conv_w, conv_b, dproj, s, u, comm=None):
    tr = _tile_rows(s, u)
    w = 2 * u
    half = SSM_CONV // 2

    def body(p0, c0, n0, p1, c1, n1, dp0, dc0, dn0, dp1, dc1, dn1, w_ref, b_ref, dproj_in, du_ref, dw_ref, db_ref):
        for blk, (ur, dr) in enumerate((((p0, c0, n0), (dp0, dc0, dn0)), ((p1, c1, n1), (dp1, dc1, dn1)))):
            e = _ext(*ur, s, tr)
            de = _ext(*dr, s, tr)
            wv = w_ref[:, blk * u:(blk + 1) * u]
            acc = jnp.zeros_like(e)
            for k in range(SSM_CONV):
                acc = acc + _shift(e, k - half) * wv[k:k + 1, :]
            pre = acc + b_ref[:, blk * u:(blk + 1) * u]
            dpre = jnp.where(_valid_rows(e.shape, s, tr), de * _dsilu(pre), 0.0)
            du = jnp.zeros_like(e)
            rows = []
            for k in range(SSM_CONV):
                du = du + _shift(dpre, half - k) * wv[k:k + 1, :]
                rows.append(jnp.sum((_shift(e, k - half) * dpre)[HALO:HALO + tr], axis=0, keepdims=True))
            du_ref[:, blk * u:(blk + 1) * u] = du[HALO:HALO + tr].astype(BF16)
            dwv = jnp.concatenate(rows + [jnp.zeros((SUBLANES - SSM_CONV, u), F32)], axis=0)
            dbv = jnp.sum(dpre[HALO:HALO + tr], axis=0, keepdims=True)
            first = pl.program_id(0) == 0

            @pl.when(first)
            def _():
                dw_ref[:, blk * u:(blk + 1) * u] = dwv
                db_ref[:, blk * u:(blk + 1) * u] = dbv

            @pl.when(jnp.logical_not(first))
            def _():
                dw_ref[:, blk * u:(blk + 1) * u] += dwv
                db_ref[:, blk * u:(blk + 1) * u] += dbv

    in_specs = (_halo_specs(s, tr, u, P_X) + _halo_specs(s, tr, u, P_BC) + _halo_specs(s, tr, u, 0)
                + _halo_specs(s, tr, u, 1) + [_full((SSM_CONV, w)), _full((1, w)), HBM_REF])
    return _rowcall(body, "conv_silu_bwd", s, tr, in_specs,
                    [_row(tr, w), _full((SUBLANES, w)), _full((1, w))],
                    [jax.ShapeDtypeStruct(dproj.shape, BF16), jax.ShapeDtypeStruct((SUBLANES, w), F32),
                     jax.ShapeDtypeStruct((1, w), F32)],
                    (proj,) * 6 + (dxbc,) * 6 + (conv_w, conv_b, dproj), comm=comm, aliases={14: 0})


def _expanders(h):
    col64 = jnp.arange(2 * h * HEAD_DIM) // HEAD_DIM
    col128 = jnp.arange(2 * h * LANES) // LANES
    row = jnp.arange(LANES)[:, None]
    return (row == col64[None, :]).astype(BF16), (row == col128[None, :]).astype(BF16)


def _dt_prep(proj, bias_row, a_row, s, u, h, comm=None):
    q = CHUNK
    e64, e128 = _expanders(h)
    ds = h * HEAD_DIM
    dtblk = (6 * u) // LANES

    def body(raw_ref, b_ref, a_ref, e64_ref, e128_ref, dt_ref, cum_ref, dte_ref, cume_ref):
        lane = lax.broadcasted_iota(jnp.int32, (q, LANES), 1)
        dt = jnp.where(lane < 2 * h, _softplus(raw_ref[...] + b_ref[...]), 0.0)
        da = dt * a_ref[...]
        ii = lax.broadcasted_iota(jnp.int32, (q, q), 0)
        kk = lax.broadcasted_iota(jnp.int32, (q, q), 1)
        lower = (kk <= ii).astype(F32).astype(BF16)
        upper = (kk >= ii).astype(F32).astype(BF16)
        cum = jnp.where(lane < h, _dot3_l(lower, da), _dot3_l(upper, da))
        dt_ref[...] = dt
        cum_ref[...] = cum
        dte = _dot3_r(dt, e64_ref[...])
        cume = _dot3_r(cum, e128_ref[...])
        dte_ref[0] = dte[:, :ds]
        dte_ref[1] = dte[:, ds:]
        cume_ref[0] = cume[:, :h * LANES]
        cume_ref[1] = cume[:, h * LANES:]

    in_specs = [pl.BlockSpec((q, LANES), lambda i: (i, dtblk)), _full((1, LANES)), _full((1, LANES)),
                _full(e64.shape), _full(e128.shape)]
    out_specs = [_row(q, LANES), _row(q, LANES),
                 pl.BlockSpec((2, q, ds), lambda i: (0, i, 0)), pl.BlockSpec((2, q, h * LANES), lambda i: (0, i, 0))]
    out_shape = [jax.ShapeDtypeStruct((s, LANES), F32), jax.ShapeDtypeStruct((s, LANES), F32),
                 jax.ShapeDtypeStruct((2, s, ds), F32), jax.ShapeDtypeStruct((2, s, h * LANES), F32)]
    return _rowcall(body, "dt_prep", s, q, in_specs, out_specs, out_shape, (proj, bias_row, a_row, e64, e128), comm=comm)


def _ssd_specs(s, h, g):
    q = CHUNK
    nc = s // q
    ds = h * HEAD_DIM
    nb = g * STATE
    return q, nc, ds, nb


def _ssd_fwd(xbc, dt_e, cum_e, cum_t, s, h, g, comm=None):
    q, nc, ds, nb = _ssd_specs(s, h, g)
    npair = h // 2

    def cidx(d, i):
        return jnp.where(d == 0, i, nc - 1 - i)

    def body(x_ref, b_ref, c_ref, dt_ref, cum_ref, cumt_ref, y_ref, sp_ref, st):
        d = pl.program_id(0)
        i = pl.program_id(1)

        @pl.when(i == 0)
        def _():
            st[...] = jnp.zeros_like(st)

        rev = d == 1
        ii = lax.broadcasted_iota(jnp.int32, (q, q), 0)
        jj = lax.broadcasted_iota(jnp.int32, (q, q), 1)
        sgn = jnp.where(rev, -1, 1)
        mask = (jj - ii) * sgn <= 0
        left = lax.broadcasted_iota(jnp.int32, (q, LANES), 1) < HEAD_DIM

        def group(gi, carry):
            goff = pl.multiple_of(gi * STATE, STATE)
            cg = c_ref[:, pl.ds(goff, STATE)].astype(BF16)
            bg = b_ref[:, pl.ds(goff, STATE)].astype(BF16)
            gm = _dot_nt(cg, bg)
            for p in range(HEADS_PER_GROUP // 2):
                pr = gi * (HEADS_PER_GROUP // 2) + p
                off = pl.multiple_of(pr * LANES, LANES)
                xd = x_ref[:, pl.ds(off, LANES)] * dt_ref[:, pl.ds(off, LANES)]
                ms = []
                cols = []
                for hl in range(2):
                    hh = 2 * pr + hl
                    col = cum_ref[:, pl.ds(pl.multiple_of(hh * LANES, LANES), LANES)]
                    row = cumt_ref[pl.ds(hh, 1), :]
                    lm = jnp.where(mask, jnp.exp(jnp.minimum(col - row, 0.0)), 0.0)
                    ms.append((gm * lm).astype(BF16))
                    cols.append(col)
                y = _dot(ms[0], jnp.where(left, xd, 0.0).astype(BF16)) + _dot(ms[1], jnp.where(left, 0.0, xd).astype(BF16))
                ce = jnp.where(left, cols[0], cols[1])
                sprev = st[pr]
                sp_ref[pr] = sprev
                y = y + jnp.exp(ce) * _dot(cg, sprev.astype(BF16))
                y_ref[:, pl.ds(off, LANES)] = y
                tot = jnp.where(rev, ce[0:1, :], ce[q - 1:q, :])
                v = (xd * jnp.exp(tot - ce)).astype(BF16)
                st[pr] = jnp.exp(tot) * sprev + _dot_tn(bg, v)
            return carry

        lax.fori_loop(0, g, group, 0, unroll=2)

    in_specs = [
        pl.BlockSpec((q, ds), lambda d, i: (cidx(d, i), 0)),
        pl.BlockSpec((q, nb), lambda d, i: (cidx(d, i), ds // nb)),
        pl.BlockSpec((q, nb), lambda d, i: (cidx(d, i), ds // nb + 1)),
        pl.BlockSpec((None, q, ds), lambda d, i: (d, cidx(d, i), 0)),
        pl.BlockSpec((None, q, h * LANES), lambda d, i: (d, cidx(d, i), 0)),
        pl.BlockSpec((None, h, q), lambda d, i: (d, 0, cidx(d, i))),
    ]
    out_specs = [
        pl.BlockSpec((None, q, ds), lambda d, i: (d, cidx(d, i), 0)),
        pl.BlockSpec((None, None, npair, STATE, LANES), lambda d, i: (d, cidx(d, i), 0, 0, 0)),
    ]
    out_shape = [jax.ShapeDtypeStruct((2, s, ds), F32), jax.ShapeDtypeStruct((2, nc, npair, STATE, LANES), F32)]
    if comm is not None:
        return _comm_call(body, name="ssd_fwd", grid=(2, nc), in_specs=in_specs, out_specs=out_specs, out_shape=out_shape,
                          scratch_shapes=[pltpu.VMEM((npair, STATE, LANES), F32)],
                          args=(xbc, xbc, xbc, dt_e, cum_e, cum_t), comm=comm)
    return pl.pallas_call(
        body, name="ssd_fwd", grid=(2, nc), in_specs=in_specs, out_specs=out_specs, out_shape=out_shape,
        scratch_shapes=[pltpu.VMEM((npair, STATE, LANES), F32)],
        compiler_params=_params(("arbitrary", "arbitrary")),
    )(xbc, xbc, xbc, dt_e, cum_e, cum_t), ()


def _ssd_bwd(xbc, dt_e, cum_e, cum_t, dt_t, a_col, dy, sp, s, h, g, comm=None):
    q, nc, ds, nb = _ssd_specs(s, h, g)
    npair = h // 2

    def cidx(d, i):
        return jnp.where(d == 0, nc - 1 - i, i)

    def body(x_ref, b_ref, c_ref, dt_ref, cum_ref, cumt_ref, dtt_ref, a_ref, dy_ref, sp_ref,
             dx_ref, db_ref, dc_ref, ddt_ref, da_ref, dst, rowp):
        d = pl.program_id(0)
        i = pl.program_id(1)

        @pl.when(i == 0)
        def _():
            dst[...] = jnp.zeros_like(dst)
            da_ref[...] = jnp.zeros_like(da_ref)

        rev = d == 1
        ii = lax.broadcasted_iota(jnp.int32, (q, q), 0)
        jj = lax.broadcasted_iota(jnp.int32, (q, q), 1)
        sgn = jnp.where(rev, -1, 1)
        mask = (jj - ii) * sgn <= 0
        lane = lax.broadcasted_iota(jnp.int32, (q, LANES), 1)
        left = lane < HEAD_DIM
        rowp[...] = jnp.zeros_like(rowp)

        def group(gi, carry):
            acc_dcum, acc_tot, acc_dxx = carry
            goff = pl.multiple_of(gi * STATE, STATE)
            cg = c_ref[:, pl.ds(goff, STATE)].astype(BF16)
            bg = b_ref[:, pl.ds(goff, STATE)].astype(BF16)
            gm = _dot_nt(cg, bg)
            dgm = jnp.zeros((q, q), F32)
            dcg = jnp.zeros((q, STATE), F32)
            dbg = jnp.zeros((q, STATE), F32)
            for p in range(HEADS_PER_GROUP // 2):
                pr = gi * (HEADS_PER_GROUP // 2) + p
                off = pl.multiple_of(pr * LANES, LANES)
                xv = x_ref[:, pl.ds(off, LANES)]
                dte = dt_ref[:, pl.ds(off, LANES)]
                xd = xv * dte
                xdb = xd.astype(BF16)
                dyv = dy_ref[:, pl.ds(off, LANES)]
                sprev = sp_ref[pr]
                sprevb = sprev.astype(BF16)
                dsn = dst[pr]
                dsnb = dsn.astype(BF16)
                cols = [cum_ref[:, pl.ds(pl.multiple_of((2 * pr + hl) * LANES, LANES), LANES)] for hl in range(2)]
                ce = jnp.where(left, cols[0], cols[1])
                tot = jnp.where(rev, ce[0:1, :], ce[q - 1:q, :])
                et = jnp.exp(tot)
                r = jnp.exp(tot - ce)
                e = jnp.exp(ce)
                yoff = e * _dot(cg, sprevb)
                dz = (e * dyv).astype(BF16)
                dcg = dcg + _dot_nt(dz, sprevb)
                dsprev = _dot_tn(cg, dz) + et * dsn
                f1 = dyv * yoff
                v = (xd * r).astype(BF16)
                dbg = dbg + _dot_nt(v, dsnb)
                dv = _dot(bg, dsnb)
                dxd = dv * r
                tt = dv * xd * r
                wt = dsn * sprev * et
                for hl in range(2):
                    hh = 2 * pr + hl
                    hm = left if hl == 0 else jnp.logical_not(left)
                    row = cumt_ref[pl.ds(hh, 1), :]
                    lm = jnp.where(mask, jnp.exp(jnp.minimum(cols[hl] - row, 0.0)), 0.0)
                    mf = gm * lm
                    dym = jnp.where(hm, dyv, 0.0).astype(BF16)
                    dm = _dot_nt(dym, xdb)
                    dxd = dxd + _dot_tn(mf.astype(BF16), dym)
                    dgm = dgm + dm * lm
                    em = dm * mf
                    rowp[pl.ds(hh, 1), :] = rowp[pl.ds(hh, 1), :] - jnp.sum(em, axis=0, keepdims=True)
                    colq = (jnp.sum(em, axis=1, keepdims=True)
                            + jnp.sum(jnp.where(hm, f1 - tt, 0.0), axis=1, keepdims=True))
                    acc_dcum = jnp.where(lane == hh, colq, acc_dcum)
                    totq = jnp.sum(jnp.sum(jnp.where(hm, tt + wt, 0.0), axis=1, keepdims=True), axis=0, keepdims=True)
                    acc_tot = jnp.where(lane == hh, totq, acc_tot)
                dxx = dxd * xv
                for hl in range(2):
                    hh = 2 * pr + hl
                    hm = left if hl == 0 else jnp.logical_not(left)
                    acc_dxx = jnp.where(lane == hh, jnp.sum(jnp.where(hm, dxx, 0.0), axis=1, keepdims=True), acc_dxx)
                dx_ref[:, pl.ds(off, LANES)] = dxd * dte
                dst[pr] = dsprev
            dgb = dgm.astype(BF16)
            dc_ref[:, pl.ds(goff, STATE)] = dcg + _dot(dgb, bg)
            db_ref[:, pl.ds(goff, STATE)] = dbg + _dot_tn(dgb, cg)
            return acc_dcum, acc_tot, acc_dxx

        zero = jnp.zeros((q, LANES), F32)
        def two_groups(j, carry):
            return group(2 * j + 1, group(2 * j, carry))

        acc_dcum, acc_tot, acc_dxx = lax.fori_loop(0, g // 2, two_groups, (zero, zero, zero))
        dcum_t = rowp[...] + jnp.transpose(acc_dcum)[:h]
        rmat = ((ii - jj) * sgn >= 0).astype(F32).astype(BF16)
        da_t = _dot3_r(dcum_t, rmat) + jnp.transpose(acc_tot)[:h]
        ddt_ref[...] = da_t * a_ref[...] + jnp.transpose(acc_dxx)[:h]
        da_ref[...] += da_t * dtt_ref[...]

    in_specs = [
        pl.BlockSpec((q, ds), lambda d, i: (cidx(d, i), 0)),
        pl.BlockSpec((q, nb), lambda d, i: (cidx(d, i), ds // nb)),
        pl.BlockSpec((q, nb), lambda d, i: (cidx(d, i), ds // nb + 1)),
        pl.BlockSpec((None, q, ds), lambda d, i: (d, cidx(d, i), 0)),
        pl.BlockSpec((None, q, h * LANES), lambda d, i: (d, cidx(d, i), 0)),
        pl.BlockSpec((None, h, q), lambda d, i: (d, 0, cidx(d, i))),
        pl.BlockSpec((None, h, q), lambda d, i: (d, 0, cidx(d, i))),
        pl.BlockSpec((None, h, LANES), lambda d, i: (d, 0, 0)),
        pl.BlockSpec((q, ds), lambda d, i: (cidx(d, i), 0)),
        pl.BlockSpec((None, None, npair, STATE, LANES), lambda d, i: (d, cidx(d, i), 0, 0, 0)),
    ]
    out_specs = [
        pl.BlockSpec((None, q, ds), lambda d, i: (d, cidx(d, i), 0)),
        pl.BlockSpec((None, q, nb), lambda d, i: (d, cidx(d, i), 0)),
        pl.BlockSpec((None, q, nb), lambda d, i: (d, cidx(d, i), 0)),
        pl.BlockSpec((None, h, q), lambda d, i: (d, 0, cidx(d, i))),
        pl.BlockSpec((None, h, LANES), lambda d, i: (d, 0, 0)),
    ]
    out_shape = [jax.ShapeDtypeStruct((2, s, ds), F32), jax.ShapeDtypeStruct((2, s, nb), F32),
                 jax.ShapeDtypeStruct((2, s, nb), F32), jax.ShapeDtypeStruct((2, h, s), F32),
                 jax.ShapeDtypeStruct((2, h, LANES), F32)]
    scratch = [pltpu.VMEM((npair, STATE, LANES), F32), pltpu.VMEM((h, q), F32)]
    args = (xbc, xbc, xbc, dt_e, cum_e, cum_t, dt_t, a_col, dy, sp)
    if comm is not None:
        return _comm_call(body, name="ssd_bwd", grid=(2, nc), in_specs=in_specs, out_specs=out_specs, out_shape=out_shape,
                          scratch_shapes=scratch, args=args, comm=comm)
    return pl.pallas_call(
        body, name="ssd_bwd", grid=(2, nc), in_specs=in_specs, out_specs=out_specs, out_shape=out_shape,
        scratch_shapes=scratch, compiler_params=_params(("arbitrary", "arbitrary")),
    )(*args), ()


def _dt_bwd(ddt, proj, bias_row, dproj, s, u, h):
    tr = _tile_rows(s, 4 * LANES)
    dtblk = (6 * u) // LANES
    tail = dproj.shape[1] - 6 * u
    assert (6 * u) % tail == 0

    def body(d_ref, raw_ref, b_ref, dproj_in, o_ref, db_ref):
        lane = lax.broadcasted_iota(jnp.int32, (tr, LANES), 1)
        v = jnp.where(lane < 2 * h, d_ref[...] * _sigmoid(raw_ref[...] + b_ref[...]), 0.0)
        o_ref[:, :LANES] = v.astype(BF16)
        o_ref[:, LANES:] = jnp.zeros((tr, tail - LANES), BF16)
        _acc_rows(db_ref, v)

    return _rowcall(body, "dt_bwd", s, tr, [_row(tr, LANES), _row(tr, LANES, dtblk), _full((1, LANES)), HBM_REF],
                    [_row(tr, tail, (6 * u) // tail), _full((1, LANES))],
                    [jax.ShapeDtypeStruct(dproj.shape, BF16), jax.ShapeDtypeStruct((1, LANES), F32)],
                    (ddt, proj, bias_row, dproj), aliases={3: 0})


def _group_rms(v, gw):
    outs, facs = [], []
    for k in range(v.shape[1] // gw):
        blk = v[:, k * gw:(k + 1) * gw]
        f = lax.rsqrt(jnp.mean(blk * blk, axis=-1, keepdims=True) + RMS_EPS)
        outs.append(blk * f)
        facs.append(jnp.broadcast_to(f, blk.shape))
    return jnp.concatenate(outs, axis=1), jnp.concatenate(facs, axis=1)


def _group_rms_bwd(dn, n, fac, gw):
    outs = []
    for k in range(n.shape[1] // gw):
        sl = slice(k * gw, (k + 1) * gw)
        outs.append(fac[:, sl] * (dn[:, sl] - n[:, sl] * jnp.mean(dn[:, sl] * n[:, sl], axis=-1, keepdims=True)))
    return jnp.concatenate(outs, axis=1)


def _gate_norm_fwd(y2, xbc, proj, d_e, norm_w, s, u, g):
    tr = _tile_rows(s, u)
    gw = u // g

    def body(y_ref, x_ref, z_ref, d_ref, w_ref, o_ref):
        ys = y_ref[0] + y_ref[1] + d_ref[...] * x_ref[...]
        n, _ = _group_rms(ys * _silu(z_ref[...]), gw)
        o_ref[...] = (n * w_ref[...]).astype(BF16)

    return _rowcall(body, "gate_norm_fwd", s, tr,
                    [pl.BlockSpec((2, tr, u), lambda i: (0, i, 0)), _row(tr, u), _row(tr, u, P_Z), _full((1, u)), _full((1, u))],
                    _row(tr, u), jax.ShapeDtypeStruct((s, u), BF16), (y2, xbc, proj, d_e, norm_w))


def _gate_norm_bwd(dymix, y2, xbc, proj, d_e, norm_w, pw, s, u, g, comm=None):
    tr = _tile_rows(s, u)
    gw = u // g

    def body(dy_ref, y_ref, x_ref, z_ref, d_ref, w_ref, dys_ref, dz_ref, dxs_ref, dw_ref, dd_ref):
        xv = x_ref[...]
        zv = z_ref[...]
        ys = y_ref[0] + y_ref[1] + d_ref[...] * xv
        sz = _silu(zv)
        n, fac = _group_rms(ys * sz, gw)
        dout = dy_ref[...]
        _acc_rows(dw_ref, dout * n)
        dyg = _group_rms_bwd(dout * w_ref[...], n, fac, gw)
        dys = dyg * sz
        dys_ref[...] = dys
        dz_ref[...] = (dyg * ys * _dsilu(zv)).astype(BF16)
        dxs_ref[...] = dys * d_ref[...]
        _acc_rows(dd_ref, dys * xv)

    v = _full((1, u))
    return _rowcall(body, "gate_norm_bwd", s, tr,
                    [_row(tr, u), pl.BlockSpec((2, tr, u), lambda i: (0, i, 0)), _row(tr, u), _row(tr, u, P_Z), v, v],
                    [_row(tr, u), _row(tr, u, P_Z), _row(tr, u), v, v],
                    [jax.ShapeDtypeStruct((s, u), F32), jax.ShapeDtypeStruct((s, pw), BF16),
                     jax.ShapeDtypeStruct((s, u), F32), jax.ShapeDtypeStruct((1, u), F32), jax.ShapeDtypeStruct((1, u), F32)],
                    (dymix, y2, xbc, proj, d_e, norm_w), comm=comm)


def _shortconv_fwd(proj, conv_w, norm_w, s, u):
    tr = _tile_rows(s, u)
    half = SC_CONV // 2

    def body(hp, hc, hn, b_ref, cp, cc, cn, cw_ref, w_ref, o_ref):
        t = _ext(hp, hc, hn, s, tr) * _ext(cp, cc, cn, s, tr)
        wv = cw_ref[...]
        acc = jnp.zeros_like(t)
        for k in range(SC_CONV):
            acc = acc + _shift(t, k - half) * wv[k:k + 1, :]
        n, _ = _group_rms(b_ref[...] * acc[HALO:HALO + tr], SC_GROUP_WIDTH)
        o_ref[...] = (n * w_ref[...]).astype(BF16)

    in_specs = (_halo_specs(s, tr, u, P_H) + [_row(tr, u, P_B)] + _halo_specs(s, tr, u, P_C)
                + [_full((SC_CONV, u)), _full((1, u))])
    return _rowcall(body, "shortconv_fwd", s, tr, in_specs, _row(tr, u), jax.ShapeDtypeStruct((s, u), BF16),
                    (proj,) * 7 + (conv_w, norm_w))


def _shortconv_bwd(dymix, proj, conv_w, norm_w, dproj, s, u):
    tr = _tile_rows(s, u)
    half = SC_CONV // 2

    def body(dp, dc_, dn, hp, hc, hn, bp, bc, bn, cp, cc, cn, cw_ref, w_ref, dproj_in, o_ref, dcw_ref, dw_ref):
        dout = _ext(dp, dc_, dn, s, tr)
        hv = _ext(hp, hc, hn, s, tr)
        bv = _ext(bp, bc, bn, s, tr)
        cv = _ext(cp, cc, cn, s, tr)
        t = hv * cv
        wv = cw_ref[...]
        acc = jnp.zeros_like(t)
        for k in range(SC_CONV):
            acc = acc + _shift(t, k - half) * wv[k:k + 1, :]
        n, fac = _group_rms(bv * acc, SC_GROUP_WIDTH)
        cur = slice(HALO, HALO + tr)
        _acc_rows(dw_ref, (dout * n)[cur])
        dyv = _group_rms_bwd(dout * w_ref[...], n, fac, SC_GROUP_WIDTH)
        o_ref[:, u:2 * u] = (dyv * acc)[cur].astype(BF16)
        dv = dyv * bv
        dt = jnp.zeros_like(t)
        rows = []
        for k in range(SC_CONV):
            dt = dt + _shift(dv, half - k) * wv[k:k + 1, :]
            rows.append(jnp.sum((_shift(t, k - half) * dv)[cur], axis=0, keepdims=True))
        o_ref[:, :u] = (dt * cv)[cur].astype(BF16)
        o_ref[:, 2 * u:] = (dt * hv)[cur].astype(BF16)
        dwv = jnp.concatenate(rows + [jnp.zeros((SUBLANES - SC_CONV, u), F32)], axis=0)
        first = pl.program_id(0) == 0

        @pl.when(first)
        def _():
            dcw_ref[...] = dwv

        @pl.when(jnp.logical_not(first))
        def _():
            dcw_ref[...] += dwv

    in_specs = (_halo_specs(s, tr, u, 1) + _halo_specs(s, tr, u, P_H) + _halo_specs(s, tr, u, P_B)
                + _halo_specs(s, tr, u, P_C) + [_full((SC_CONV, u)), _full((1, u)), HBM_REF])
    return _rowcall(body, "shortconv_bwd", s, tr, in_specs,
                    [_row(tr, 3 * u, 1), _full((SUBLANES, u)), _full((1, u))],
                    [jax.ShapeDtypeStruct(dproj.shape, BF16), jax.ShapeDtypeStruct((SUBLANES, u), F32),
                     jax.ShapeDtypeStruct((1, u), F32)],
                    (dymix,) * 3 + (proj,) * 9 + (conv_w, norm_w, dproj), aliases={14: 0})


def _adam_math(w, g, m, v):
    m2 = ADAM_B1 * m + (1.0 - ADAM_B1) * g
    v2 = ADAM_B2 * v + (1.0 - ADAM_B2) * (g * g)
    m_hat = m2 / (1.0 - ADAM_B1 ** ADAM_STEP)
    v_hat = v2 / (1.0 - ADAM_B2 ** ADAM_STEP)
    delta = -ADAM_LR * (m_hat / (jnp.sqrt(v_hat) + ADAM_EPS) + ADAM_WD * w)
    return delta, m2, v2


def _adam_rows(r, c):
    return _pick(r, max(SUBLANES, (1 << 20) // (4 * c)), SUBLANES)


def _adam(w, g, m, v, *, name, emit_grad=False):
    r, c = w.shape
    tr = _adam_rows(r, c)
    n_out = 4 if emit_grad else 3

    def body(w_ref, g_ref, m_ref, v_ref, *outs):
        gv = g_ref[...]
        if emit_grad:
            outs[0][...] = gv
        outs[-3][...], outs[-2][...], outs[-1][...] = _adam_math(w_ref[...], gv, m_ref[...], v_ref[...])

    return _rowcall(body, name, r, tr, [_row(tr, c)] * 4, [_row(tr, c)] * n_out,
                    [jax.ShapeDtypeStruct((r, c), F32)] * n_out, (w, g, m, v))


def _adam_outer(w, a_t, bmat, m, v, *, name):
    r, c = w.shape
    tr = _adam_rows(r, c)
    kk = a_t.shape[1]

    def body(w_ref, a_ref, b_ref, m_ref, v_ref, g_ref, d_ref, m2_ref, v2_ref):
        g = _dot(a_ref[...].astype(BF16), b_ref[...].astype(BF16))
        g_ref[...] = g
        d_ref[...], m2_ref[...], v2_ref[...] = _adam_math(w_ref[...], g, m_ref[...], v_ref[...])

    return _rowcall(body, name, r, tr, [_row(tr, c), _row(tr, kk), _full((kk, c)), _row(tr, c), _row(tr, c)],
                    [_row(tr, c)] * 4, [jax.ShapeDtypeStruct((r, c), F32)] * 4, (w, a_t, bmat, m, v))


ANY = pl.BlockSpec(memory_space=pl.ANY)
VMEM_WHOLE = pl.BlockSpec(memory_space=pltpu.VMEM)


def _place():
    x, y, c = lax.axis_index("x"), lax.axis_index("y"), lax.axis_index("c")
    return x, y, c


DMA_CHUNKS = 8


def _n_chunks(rows):
    n = DMA_CHUNKS
    while n > 1 and rows % (16 * n):
        n //= 2
    return n


def _allgather_small(v, *, name):
    m_per, n = v.shape

    def body(x_ref, out_ref, send_sems, recv_sems, local_sem):
        x, y, c = _place()
        me, sibling = (x, y, c), (x, y, 1 - c)
        chips = [(1 - x, y), (x, 1 - y), (1 - x, 1 - y)]

        def rows(px, py, pc):
            return out_ref.at[pl.ds((4 * px + 2 * py + pc) * m_per, m_per), :]

        def copy(k, block, to, src=None):
            return pltpu.make_async_remote_copy(
                src_ref=rows(*block) if src is None else src, dst_ref=rows(*block),
                send_sem=send_sems.at[k], recv_sem=recv_sems.at[k], device_id=to, device_id_type=MESH)

        mine = pltpu.make_async_copy(x_ref, rows(*me), local_sem)
        mine.start()
        first = [copy(0, me, sibling, src=x_ref)]
        first += [copy(1 + j, me, (*chip, c), src=x_ref) for j, chip in enumerate(chips)]
        for cp in first:
            cp.start()
        passed = [copy(4 + j, (*chip, c), sibling) for j, chip in enumerate(chips)]
        for j, chip in enumerate(chips):
            copy(1 + j, (*chip, c), me).wait_recv()
            passed[j].start()
        copy(0, sibling, me).wait_recv()
        for j, chip in enumerate(chips):
            copy(4 + j, (*chip, 1 - c), me).wait_recv()
        for cp in first + passed:
            cp.wait_send()
        mine.wait()

    return pl.pallas_call(
        body, name=name, out_shape=jax.ShapeDtypeStruct((N_DEV * m_per, n), v.dtype),
        in_specs=[VMEM_WHOLE], out_specs=VMEM_WHOLE,
        scratch_shapes=[pltpu.SemaphoreType.DMA((7,)), pltpu.SemaphoreType.DMA((7,)), pltpu.SemaphoreType.DMA],
        compiler_params=pltpu.CompilerParams(vmem_limit_bytes=VMEM_LIMIT),
    )(v)


def _chip_id():
    return 2 * lax.axis_index("x") + lax.axis_index("y")


def _core_id():
    return lax.axis_index("c")


def _cast_into_block(wl, *, name, comm=None):
    r, c_ = wl.shape
    tr = _tile_rows(r, c_, 16)

    def body(w_ref, o_ref):
        o_ref[...] = w_ref[...].astype(BF16)

    in_spec = pl.BlockSpec((tr, c_), lambda i: (i, 0))
    out_spec = pl.BlockSpec((None, tr, c_), lambda i: (_chip_id(), i, 0))
    out_shape = jax.ShapeDtypeStruct((N_CHIPS, r, c_), BF16)
    if comm is not None:
        (out,), landed = _comm_call(body, name=name, grid=(r // tr,), in_specs=[in_spec], out_specs=[out_spec],
                                    out_shape=[out_shape], scratch_shapes=[], args=(wl,), comm=comm)
        return out, landed
    return pl.pallas_call(body, name=name, grid=(r // tr,), in_specs=[in_spec], out_specs=out_spec, out_shape=out_shape,
                          compiler_params=_params(("arbitrary",)))(wl)


def _gather_weight(buf, *, name, part=(0, 1)):
    return _run_comm(_gather_comm(buf, part), name=name)[0]


def _gather_comm(buf, part=(0, 1)):
    _, r, c_ = buf.shape
    half = r // 2
    n_all = _n_chunks(half)
    rows = half // n_all
    first_chunk = round(part[0] * n_all)
    nch = round(part[1] * n_all) - first_chunk

    def plan(out_ref):
        x, y, c = _place()
        me, sibling = (x, y, c), (x, y, 1 - c)
        chips = [(1 - x, y), (x, 1 - y), (1 - x, 1 - y)]
        return me, sibling, chips, c

    def copy(out_ref, sems, k, i, block, to):
        part = out_ref.at[2 * block[0] + block[1], pl.ds(block[2] * half + (first_chunk + i) * rows, rows), :]
        return pltpu.make_async_remote_copy(src_ref=part, dst_ref=part, send_sem=sems[0].at[k * nch + i],
                                            recv_sem=sems[1].at[k * nch + i], device_id=to, device_id_type=MESH)

    def start(cins, couts, sems):
        (out_ref,) = couts
        me, sibling, chips, c = plan(out_ref)
        for i in range(nch):
            for j, chip in enumerate(chips):
                copy(out_ref, sems, j, i, me, (*chip, c)).start()

    def finish(cins, couts, sems):
        (out_ref,) = couts
        me, sibling, chips, c = plan(out_ref)
        passed = []
        for i in range(nch):
            for j, chip in enumerate(chips):
                copy(out_ref, sems, j, i, (*chip, c), me).wait_recv()
                passed.append(copy(out_ref, sems, 3 + j, i, (*chip, c), sibling))
                passed[-1].start()
        for i in range(nch):
            for j, chip in enumerate(chips):
                copy(out_ref, sems, 3 + j, i, (*chip, 1 - c), me).wait_recv()
        for i in range(nch):
            for j, chip in enumerate(chips):
                copy(out_ref, sems, j, i, me, (*chip, c)).wait_send()
        for cp in passed:
            cp.wait_send()

    return dict(inputs=[buf], out_shape=[jax.ShapeDtypeStruct(buf.shape, buf.dtype)], aliases={0: 0},
                scratch=[pltpu.SemaphoreType.DMA((6 * nch,)), pltpu.SemaphoreType.DMA((6 * nch,))],
                start=start, finish=finish)


def _merge_comms(*comms):
    inputs, outs, aliases, scratch, spans = [], [], {}, [], []
    for cm in comms:
        i0, o0, s0 = len(inputs), len(outs), len(scratch)
        inputs += cm["inputs"]
        outs += cm["out_shape"]
        scratch += cm["scratch"]
        aliases.update({i0 + k: o0 + v for k, v in cm["aliases"].items()})
        spans.append((i0, len(inputs), o0, len(outs), s0, len(scratch)))

    def run(which):
        def f(cins, couts, sems):
            for cm, (i0, i1, o0, o1, s0, s1) in zip(comms, spans):
                cm[which](cins[i0:i1], couts[o0:o1], sems[s0:s1])
        return f

    return dict(inputs=inputs, out_shape=outs, aliases=aliases, scratch=scratch, start=run("start"), finish=run("finish"))


def _pair_comm(gfull):
    nblk, r, c_ = gfull.shape
    half = r // 2
    nch = _n_chunks(half)
    rows = half // nch

    def copies(cins, couts, sems):
        g_ref, peer_ref = cins[0], couts[0]
        x, y, c = _place()
        return [pltpu.make_async_remote_copy(
            src_ref=g_ref.at[k, pl.ds((1 - c) * half + i * rows, rows), :], dst_ref=peer_ref.at[k, pl.ds(i * rows, rows), :],
            send_sem=sems[0].at[k * nch + i], recv_sem=sems[1].at[k * nch + i],
            device_id=(x, y, 1 - c), device_id_type=MESH) for i in range(nch) for k in range(nblk)]

    def start(cins, couts, sems):
        for cp in copies(cins, couts, sems):
            cp.start()

    def finish(cins, couts, sems):
        cps = copies(cins, couts, sems)
        for cp in cps:
            cp.wait_recv()
        for cp in cps:
            cp.wait_send()

    return dict(inputs=[gfull], out_shape=[jax.ShapeDtypeStruct((nblk, half, c_), gfull.dtype)], aliases={},
                scratch=[pltpu.SemaphoreType.DMA((nblk * nch,)), pltpu.SemaphoreType.DMA((nblk * nch,))],
                start=start, finish=finish)


def _swap_comm(buf):
    r, c_ = buf.shape
    half = r // 2
    nch = _n_chunks(half)
    rows = half // nch

    def copy(out_ref, sems, i, pc):
        part = out_ref.at[pl.ds(pc * half + i * rows, rows), :]
        x, y, c = _place()
        return pltpu.make_async_remote_copy(src_ref=part, dst_ref=part, send_sem=sems[0].at[i], recv_sem=sems[1].at[i],
                                            device_id=(x, y, 1 - c), device_id_type=MESH)

    def start(cins, couts, sems):
        c = _core_id()
        for i in range(nch):
            copy(couts[0], sems, i, c).start()

    def finish(cins, couts, sems):
        c = _core_id()
        for i in range(nch):
            copy(couts[0], sems, i, 1 - c).wait_recv()
        for i in range(nch):
            copy(couts[0], sems, i, c).wait_send()

    return dict(inputs=[buf], out_shape=[jax.ShapeDtypeStruct(buf.shape, buf.dtype)], aliases={0: 0},
                scratch=[pltpu.SemaphoreType.DMA((nch,)), pltpu.SemaphoreType.DMA((nch,))], start=start, finish=finish)


def _run_comm(cm, *, name):
    nci, nco = len(cm["inputs"]), len(cm["out_shape"])

    def body(*refs):
        cins, couts, sems = refs[:nci], refs[nci:nci + nco], refs[nci + nco:]
        cm["start"](cins, couts, sems)
        cm["finish"](cins, couts, sems)

    return pl.pallas_call(
        body, name=name, out_shape=list(cm["out_shape"]), in_specs=[ANY] * nci, out_specs=[ANY] * nco,
        input_output_aliases=dict(cm["aliases"]), scratch_shapes=cm["scratch"],
    )(*cm["inputs"])


def _pair_add(gfull, peer, *, name):
    nblk, r, c_ = gfull.shape
    half = r // 2
    tr = _pick(half, max(16, (1 << 20) // (2 * c_)), 16)
    per = half // tr

    def body(g_ref, p_ref, o_ref):
        o_ref[...] = (g_ref[...].astype(F32) + p_ref[...].astype(F32)).astype(BF16)

    return pl.pallas_call(
        body, name=name, grid=(nblk, per),
        in_specs=[pl.BlockSpec((None, tr, c_), lambda k, i: (k, _core_id() * per + i, 0)),
                  pl.BlockSpec((None, tr, c_), lambda k, i: (k, i, 0))],
        out_specs=pl.BlockSpec((None, tr, c_), lambda k, i: (k, i, 0)),
        out_shape=jax.ShapeDtypeStruct((nblk, half, c_), BF16),
        compiler_params=_params(("arbitrary", "arbitrary")))(gfull, peer)


def _scatter_comm(pre, part=(0, 1), recv=None):
    _, half, c_ = pre.shape
    n_all = _n_chunks(half)
    rows = half // n_all
    first_chunk = round(part[0] * n_all)
    nch = round(part[1] * n_all) - first_chunk

    def copies(cins, couts, sems):
        p_ref, r_ref = cins[0], couts[0]
        x, y, c = _place()
        out = []
        for i in range(nch):
            at = pl.ds((first_chunk + i) * rows, rows)
            for j, (tx, ty) in reversed(list(enumerate([(1 - x, y), (x, 1 - y), (1 - x, 1 - y)]))):
                out.append(pltpu.make_async_remote_copy(
                    src_ref=p_ref.at[2 * tx + ty, at, :], dst_ref=r_ref.at[j, at, :],
                    send_sem=sems[0].at[j * nch + i], recv_sem=sems[1].at[j * nch + i],
                    device_id=(tx, ty, c), device_id_type=MESH))
        return out

    def start(cins, couts, sems):
        for cp in copies(cins, couts, sems):
            cp.start()

    def finish(cins, couts, sems):
        cps = copies(cins, couts, sems)
        for cp in cps:
            cp.wait_recv()
        for cp in cps:
            cp.wait_send()

    return dict(inputs=[pre] if recv is None else [pre, recv], out_shape=[jax.ShapeDtypeStruct((3, half, c_), pre.dtype)],
                aliases={} if recv is None else {1: 0},
                scratch=[pltpu.SemaphoreType.DMA((3 * nch,)), pltpu.SemaphoreType.DMA((3 * nch,))],
                start=start, finish=finish)


def _sum_into_half(pre, recv, *, name, comm=None):
    _, half, c_ = pre.shape
    n = recv.shape[0]
    tr = _pick(half, max(16, (1 << 19) // (2 * c_)), 16)
    per = half // tr

    def body(g_ref, r_ref, o_ref):
        acc = g_ref[...].astype(F32)
        for k in range(n):
            acc = acc + r_ref[k].astype(F32)
        o_ref[...] = acc

    in_specs = [pl.BlockSpec((None, tr, c_), lambda i: (_chip_id(), i, 0)), pl.BlockSpec((n, tr, c_), lambda i: (0, i, 0))]
    out_spec = pl.BlockSpec((tr, c_), lambda i: (_core_id() * per + i, 0))
    out_shape = jax.ShapeDtypeStruct((2 * half, c_), F32)
    if comm is not None:
        (out,), landed = _comm_call(body, name=name, grid=(per,), in_specs=in_specs, out_specs=[out_spec],
                                    out_shape=[out_shape], scratch_shapes=[], args=(pre, recv), comm=comm)
        return out, landed
    return pl.pallas_call(body, name=name, grid=(per,), in_specs=in_specs, out_specs=out_spec, out_shape=out_shape,
                          compiler_params=_params(("arbitrary",)))(pre, recv)


def _sum_slots(recv, *, name):
    n, r, c_ = recv.shape
    tr = _pick(r, max(16, (1 << 19) // (2 * c_)), 16)

    def body(r_ref, o_ref):
        acc = r_ref[0].astype(F32)
        for k in range(1, n):
            acc = acc + r_ref[k].astype(F32)
        o_ref[...] = acc

    return _rowcall(body, name, r, tr, [pl.BlockSpec((n, tr, c_), lambda i: (0, i, 0))], _row(tr, c_),
                    jax.ShapeDtypeStruct((r, c_), F32), (recv,))


def _prereduce(gfull, *, name):
    return _pair_add(gfull, _run_comm(_pair_comm(gfull), name=name + "_pair")[0], name=name + "_padd")


PACK_ROWS = 16


def _pack(parts):
    flat = [p.reshape(-1).astype(F32) for p in parts]
    n = sum(f.shape[0] for f in flat)
    unit = PACK_ROWS * LANES
    total = -(-n // unit) * unit
    if total > n:
        flat.append(jnp.zeros((total - n,), F32))
    where, off = [], 0
    for p in parts:
        where.append((off, p.shape))
        off += p.size
    return jnp.concatenate(flat).reshape(total // LANES, LANES), where


def _unpack(flat, where):
    v = flat.reshape(-1)
    return [v[off:off + _size(shape)].reshape(shape) for off, shape in where]


def _size(shape):
    n = 1
    for d in shape:
        n *= d
    return n


def _sample_step(x, target, mods, h1, w_in_p, bufs, sp):
    s, d = x.shape
    u = d // 2
    h = u // HEAD_DIM
    g = h // HEADS_PER_GROUP
    pw = w_in_p.shape[1]
    din = 6 * u + 2 * h
    dff_ = bufs["w_up"].shape[2] * N_CHIPS
    shift1, scale1, gate1, shift2, scale2, gate2 = mods

    a_f = -jnp.exp(sp["ssm_a_log_f"].reshape(-1))
    a_b = -jnp.exp(sp["ssm_a_log_b"].reshape(-1))
    pad_l = LANES - 2 * h
    a_row = jnp.pad(jnp.concatenate([a_f, a_b]), (0, pad_l)).reshape(1, LANES)
    bias_row = jnp.pad(jnp.concatenate([sp["ssm_dt_bias_f"].reshape(-1), sp["ssm_dt_bias_b"].reshape(-1)]),
                       (0, pad_l)).reshape(1, LANES)
    a_col = jnp.broadcast_to(jnp.stack([a_f, a_b])[:, :, None], (2, h, LANES))
    d_e = jnp.repeat(sp["ssm_d"].reshape(-1), HEAD_DIM).reshape(1, u)
    conv_w, conv_b = sp["ssm_conv_w"], sp["ssm_conv_b"].reshape(1, 2 * u)
    sc_conv_w = sp["sc_conv_w"]
    ssm_norm_w, sc_norm_w = sp["ssm_norm_w"].reshape(1, u), sp["sc_norm_w"].reshape(1, u)
    ln1_g, ln1_b = sp["ln1_g"].reshape(1, d), sp["ln1_b"].reshape(1, d)
    ln2_g, ln2_b = sp["ln2_g"].reshape(1, d), sp["ln2_b"].reshape(1, d)

    e = 1.0 / DMA_CHUNKS
    proj, (w_up_b, w_out_blk) = _matmul(
        h1, w_in_p, name="mm_proj", tn=1280,
        comm=_merge_comms(_gather_comm(bufs["w_up"], (0, 3 * e)), _gather_comm(bufs["w_out"])))
    w_out = w_out_blk.reshape(d, d)
    xbc, (w_down_b,) = _conv_silu_fwd(proj, conv_w, conv_b, s, u, comm=_gather_comm(bufs["w_down"], (0, e)))
    (dt, cum, dt_e, cum_e), (w_up_b,) = _dt_prep(proj, bias_row, a_row, s, u, h, comm=_gather_comm(w_up_b, (3 * e, 4 * e)))
    cum_t = jnp.stack([cum[:, :h].T, cum[:, h:2 * h].T])
    dt_t = jnp.stack([dt[:, :h].T, dt[:, h:2 * h].T])
    (y2, states), (w_up_b,) = _ssd_fwd(xbc, dt_e, cum_e, cum_t, s, h, g, comm=_gather_comm(w_up_b, (4 * e, 6 * e)))
    y_ssm = _gate_norm_fwd(y2, xbc, proj, d_e, ssm_norm_w, s, u, g)
    y_sc = _shortconv_fwd(proj, sc_conv_w, sc_norm_w, s, u)
    ymix = jnp.concatenate([y_ssm, y_sc], axis=1)
    mix, (w_up_b,) = _matmul(ymix, w_out, name="mm_mix", comm=_gather_comm(w_up_b, (6 * e, 7 * e)))
    (x1, h2), (w_up_blk,) = _ln1_fwd(x, mix, gate1, ln1_g, ln1_b, scale2, shift2, comm=_gather_comm(w_up_b, (7 * e, 1)))
    (up, ff), (w_down_blk,) = _matmul(h2, w_up_blk, name="mm_up", b_blocks=N_CHIPS, epilogue="relu2",
                                      comm=_gather_comm(w_down_b, (e, 1)))
    w_down = w_down_blk.reshape(dff_, d)
    f = _matmul(ff, w_down, name="mm_down")
    df, dr2, loss, dg2, db2, dgate2 = _ln2_loss_bwd(x1, f, target, gate2, ln2_g, ln2_b)

    gw_down = _matmul(ff, df, name="mm_gw_down", ta=True, out_dtype=BF16).reshape(N_CHIPS, dff_ // N_CHIPS, d)
    du, (peer,) = _matmul(df, w_down, name="mm_dff", tb=True, out_dtype=BF16, epilogue="relu2_bwd", extra=up,
                          comm=_pair_comm(gw_down))
    pre_down = _pair_add(gw_down, peer, name="rs_w_down_padd")
    gw_up, (rv_down,) = _matmul(h2, du, name="mm_gw_up", ta=True, out_dtype=BF16, out_blocks=N_CHIPS,
                                comm=_scatter_comm(pre_down, (0, 0.5)))
    dh2, (rv_down, peer) = _matmul(du, w_up_blk, name="mm_dh2", tb=True, b_blocks=N_CHIPS,
                                   comm=_merge_comms(_scatter_comm(pre_down, (0.5, 1), recv=rv_down), _pair_comm(gw_up)))
    pre_up = _pair_add(gw_up, peer, name="rs_w_up_padd")
    (dmix, dxa, dscale2, dshift2, dg1, db1, dgate1), (rv_up,) = _ln1_bwd(
        dh2, dr2, x1, x, mix, scale2, gate1, ln1_g, comm=_scatter_comm(pre_up, (0, e)))
    gw_out, (rv_up,) = _matmul(ymix, dmix, name="mm_gw_out", ta=True, out_dtype=BF16,
                               comm=_scatter_comm(pre_up, (e, 2 * e), recv=rv_up))
    gw_out = gw_out.reshape(N_CHIPS, d // N_CHIPS, d)
    dymix, (rv_up, peer) = _matmul(dmix, w_out, name="mm_dymix", tb=True,
                                   comm=_merge_comms(_scatter_comm(pre_up, (2 * e, 4 * e), recv=rv_up), _pair_comm(gw_out)))
    pre_out = _pair_add(gw_out, peer, name="rs_w_out_padd")
    dys, dproj, dxs, dnw, dd_e = _gate_norm_bwd(dymix, y2, xbc, proj, d_e, ssm_norm_w, pw, s, u, g)
    (dx2, dbb, dcc, ddt_t, da), (rv_up,) = _ssd_bwd(xbc, dt_e, cum_e, cum_t, dt_t, a_col, dys, states, s, h, g,
                                                    comm=_scatter_comm(pre_up, (4 * e, 1), recv=rv_up))
    dxbc = jnp.concatenate([dx2[0] + dx2[1] + dxs, dbb[0] + dbb[1], dcc[0] + dcc[1]], axis=1)
    (dproj, dcw, dcb), (rv_out,) = _conv_silu_bwd(proj, dxbc, conv_w, conv_b, dproj, s, u, comm=_scatter_comm(pre_out))
    ddt = jnp.pad(jnp.concatenate([ddt_t[0].T, ddt_t[1].T], axis=1), ((0, 0), (0, pad_l)))
    dproj, dbias = _dt_bwd(ddt, proj, bias_row, dproj, s, u, h)
    dproj, dscw, dscnw = _shortconv_bwd(dymix, proj, sc_conv_w, sc_norm_w, dproj, s, u)
    gp = _matmul(h1, dproj, name="mm_gw_in", ta=True, out_dtype=BF16, tn=1280)
    pre_in = _from_p_layout(_prereduce(gp[None], name="rs_w_in")[0], u, h, N_CHIPS)
    (grad_x, dscale1, dshift1), (rv_in,) = _matmul(dproj, w_in_p, name="mm_dh1", tb=True, tk=2560, epilogue="dx",
                                                   extra=(dxa, x, scale1), comm=_scatter_comm(pre_in))
    big = {}
    half_down = _sum_into_half(pre_down, rv_down, name="rs_w_down_sum")
    half_up, (big["w_down"],) = _sum_into_half(pre_up, rv_up, name="rs_w_up_sum", comm=_swap_comm(half_down))
    half_out, (big["w_up"],) = _sum_into_half(pre_out, rv_out, name="rs_w_out_sum", comm=_swap_comm(half_up))
    half_in, (big["w_out"],) = _sum_into_half(pre_in, rv_in, name="rs_w_in_sum", comm=_swap_comm(half_out))
    big["w_in"] = _run_comm(_swap_comm(half_in), name="rs_w_in_swap")[0]

    small = {
        "dmod": jnp.concatenate([dshift1, dscale1, dgate1, dshift2, dscale2, dgate2], axis=1),
        "ssm_conv_b": dcb,
        "ssm_dt_bias_f": dbias[0, :h], "ssm_dt_bias_b": dbias[0, h:2 * h],
        "ssm_a_log_f": jnp.sum(da[0], axis=1) * a_f, "ssm_a_log_b": jnp.sum(da[1], axis=1) * a_b,
        "ssm_d": jnp.sum(dd_e.reshape(h, HEAD_DIM), axis=1),
        "ssm_norm_w": dnw, "sc_norm_w": dscnw,
        "ln1_g": dg1, "ln1_b": db1, "ln2_g": dg2, "ln2_b": db2,
        "ssm_conv_w": dcw[:SSM_CONV], "sc_conv_w": dscw[:SC_CONV],
    }
    return loss, grad_x, big, small


WEIGHTS = ['w_ada', 'b_ada', 'w_in', 'ssm_conv_w', 'ssm_conv_b', 'ssm_dt_bias_f', 'ssm_dt_bias_b', 'ssm_a_log_f',
           'ssm_a_log_b', 'ssm_d', 'ssm_norm_w', 'sc_conv_w', 'sc_norm_w', 'w_out', 'ln1_g', 'ln1_b', 'w_up', 'w_down',
           'ln2_g', 'ln2_b']
BIG = ('w_ada', 'w_in', 'w_out', 'w_up', 'w_down')
SMALL = tuple(n for n in WEIGHTS if n not in BIG)
SMALL_SHARDED = ('ssm_conv_w', 'sc_conv_w')


def _p_layout_width(u):
    return -(-(6 * u + LANES) // 512) * 512


def _p_segments(u, h):
    return [((0, u), P_Z * u), ((u, 3 * u), P_X * u), ((3 * u, 3 * u + 2 * h), 6 * u),
            ((3 * u + 2 * h, 6 * u + 2 * h), P_H * u)]


def _to_p_layout(blocks, u, h, pw):
    nblk, d, w = blocks.shape
    parts = []
    for (lo, hi), _ in sorted(_p_segments(u, h), key=lambda t: t[1]):
        for k in range(nblk):
            a, b = max(lo, k * w), min(hi, (k + 1) * w)
            if a < b:
                parts.append(blocks[k][:, a - k * w:b - k * w])
    parts.append(jnp.zeros((d, pw - nblk * w), blocks.dtype))
    return jnp.concatenate(parts, axis=1)


def _from_p_layout(gp, u, h, nblk):
    w = (6 * u + 2 * h) // nblk
    blocks = []
    for k in range(nblk):
        parts = []
        for (lo, hi), poff in _p_segments(u, h):
            a, b = max(lo, k * w), min(hi, (k + 1) * w)
            if a < b:
                parts.append(gp[:, poff + a - lo:poff + b - lo])
        blocks.append(jnp.concatenate(parts, axis=1))
    return jnp.stack(blocks)


def kernel(x, c, w_ada, b_ada, w_in, ssm_conv_w, ssm_conv_b, ssm_dt_bias_f, ssm_dt_bias_b, ssm_a_log_f, ssm_a_log_b, ssm_d, ssm_norm_w, sc_conv_w, sc_norm_w, w_out, ln1_g, ln1_b, w_up, w_down, ln2_g, ln2_b, loss_target, m_w_ada, m_b_ada, m_w_in, m_ssm_conv_w, m_ssm_conv_b, m_ssm_dt_bias_f, m_ssm_dt_bias_b, m_ssm_a_log_f, m_ssm_a_log_b, m_ssm_d, m_ssm_norm_w, m_sc_conv_w, m_sc_norm_w, m_w_out, m_ln1_g, m_ln1_b, m_w_up, m_w_down, m_ln2_g, m_ln2_b, v_w_ada, v_b_ada, v_w_in, v_ssm_conv_w, v_ssm_conv_b, v_ssm_dt_bias_f, v_ssm_dt_bias_b, v_ssm_a_log_f, v_ssm_a_log_b, v_ssm_d, v_ssm_norm_w, v_sc_conv_w, v_sc_norm_w, v_w_out, v_ln1_g, v_ln1_b, v_w_up, v_w_down, v_ln2_g, v_ln2_b):
    given = dict(locals())
    w = {n: given[n][0] for n in WEIGHTS}
    m = {n: given["m_" + n][0] for n in WEIGHTS}
    v = {n: given["v_" + n][0] for n in WEIGHTS}
    xs, tgt = x[0], loss_target[0]
    s, d = xs.shape
    u = d // 2
    h = u // HEAD_DIM
    nmod = N_MOD * d
    nmod_loc = nmod // N_CHIPS
    ax, ay, ac = lax.axis_index("x"), lax.axis_index("y"), lax.axis_index("c")
    chip = 2 * ax + ay
    me = 2 * chip + ac

    pay1, where1 = _pack([c[0], w["ssm_conv_w"], w["sc_conv_w"]])
    g1 = _allgather_small(pay1, name="ag_inputs").reshape(N_DEV, -1)
    per_dev = [_unpack(g1[k], where1) for k in range(N_DEV)]
    c_all = jnp.stack([p[0] for p in per_dev])
    ssm_conv_w_full = jnp.concatenate([per_dev[2 * k][1] for k in range(N_CHIPS)], axis=1)
    sc_conv_w_full = jnp.concatenate([per_dev[2 * k][2] for k in range(N_CHIPS)], axis=1)

    sc_all = _silu(c_all)
    sc16 = jnp.pad(sc_all, ((0, 16 - N_DEV), (0, 0)))
    b_loc = lax.dynamic_slice(w["b_ada"], (chip * nmod_loc,), (nmod_loc,))
    mod_loc = _matmul(sc16, w["w_ada"], name="mm_mod")[:N_DEV] + b_loc[None, :]
    pay2, where2 = _pack([mod_loc])
    g2 = _allgather_small(pay2, name="ag_mod").reshape(N_DEV, -1)
    mod_blocks = jnp.stack([_unpack(g2[2 * k], where2)[0] for k in range(N_CHIPS)])
    mod_mine = lax.dynamic_index_in_dim(mod_blocks, me, axis=1, keepdims=False).reshape(N_MOD, 1, d)
    mods = [mod_mine[k] for k in range(N_MOD)]

    din = w["w_in"].shape[1] * N_CHIPS

    e = 1.0 / DMA_CHUNKS
    g_in = _cast_into_block(w["w_in"], name="cast_w_in")
    bufs = {"w_out": _cast_into_block(w["w_out"], name="cast_w_out")}
    bufs["w_up"], (g_in,) = _cast_into_block(w["w_up"], name="cast_w_up", comm=_gather_comm(g_in, (0, e)))
    bufs["w_down"], (g_in,) = _cast_into_block(w["w_down"], name="cast_w_down", comm=_gather_comm(g_in, (e, 2 * e)))
    h1, (g_in,) = _modulate(xs, mods[1], mods[0], name="modulate1", comm=_gather_comm(g_in, (2 * e, 3 * e)))
    g_in = _gather_weight(g_in, name="gather_w_in", part=(3 * e, 1))
    w_in_p = _to_p_layout(g_in, u, h, _p_layout_width(u))

    sp = {n: w[n] for n in SMALL}
    sp["ssm_conv_w"], sp["sc_conv_w"] = ssm_conv_w_full, sc_conv_w_full
    loss_loc, grad_x, big, small = _sample_step(xs, tgt, mods, h1, w_in_p, bufs, sp)

    small_names = ["dmod"] + [n for n in SMALL if n != "b_ada"]
    pay3, where3 = _pack([loss_loc] + [small[n] for n in small_names])
    g3 = _allgather_small(pay3, name="ag_small_grads")
    tot = _unpack(_sum_slots(g3.reshape(N_DEV, -1, LANES), name="sum_small_grads"), where3)
    loss = tot[0].reshape(())
    gsum = dict(zip(small_names, tot[1:]))
    dmod_all = jnp.stack([_unpack(g3.reshape(N_DEV, -1)[k], where3)[1].reshape(-1) for k in range(N_DEV)])

    grads = {}
    grads["b_ada"] = gsum["dmod"].reshape(-1)
    for n in SMALL:
        if n in SMALL_SHARDED:
            loc = w[n].shape[1]
            grads[n] = lax.dynamic_slice_in_dim(gsum[n], chip * loc, loc, axis=1)
        elif n != "b_ada":
            grads[n] = gsum[n].reshape(w[n].shape)

    delta, new_m, new_v = {}, {}, {}
    dm_loc = lax.dynamic_slice_in_dim(dmod_all, chip * nmod_loc, nmod_loc, axis=1)
    grads["w_ada"], delta["w_ada"], new_m["w_ada"], new_v["w_ada"] = _adam_outer(
        w["w_ada"], sc16.T, jnp.pad(dm_loc, ((0, 16 - N_DEV), (0, 0))), m["w_ada"], v["w_ada"], name="adam_w_ada")
    for n in ("w_in", "w_out", "w_up", "w_down"):
        grads[n], delta[n], new_m[n], new_v[n] = _adam(w[n], big[n], m[n], v[n], name="adam_" + n, emit_grad=True)
    pw_, where_s = _pack([w[n] for n in SMALL])
    pg_, _ = _pack([grads[n] for n in SMALL])
    pm_, _ = _pack([m[n] for n in SMALL])
    pv_, _ = _pack([v[n] for n in SMALL])
    sd, sm, sv = _adam(pw_, pg_, pm_, pv_, name="adam_small")
    for n, a, b_, c_ in zip(SMALL, _unpack(sd, where_s), _unpack(sm, where_s), _unpack(sv, where_s)):
        delta[n], new_m[n], new_v[n] = a, b_, c_

    def lead(t):
        return t[None]

    return (loss, grad_x[None], *[lead(grads[n].reshape(w[n].shape)) for n in WEIGHTS],
            *[lead(delta[n]) for n in WEIGHTS], *[lead(new_m[n]) for n in WEIGHTS], *[lead(new_v[n]) for n in WEIGHTS])
```

```python
import functools

import jax
import jax.numpy as jnp
from jax import lax
from jax.experimental import pallas as pl
from jax.experimental.pallas import tpu as pltpu

F32 = jnp.float32
BF16 = jnp.bfloat16

CHUNK = 128
HEAD_DIM = 64
STATE = 128
HEADS_PER_GROUP = 4
SC_GROUP_WIDTH = 128
SSM_CONV = 5
SC_CONV = 3
N_MOD = 6
DEEPNORM_ALPHA = 2.0 ** 0.25
LN_EPS = 1e-5
RMS_EPS = 1e-5
ADAM_LR = 0.001
ADAM_B1 = 0.9
ADAM_B2 = 0.999
ADAM_EPS = 1e-08
ADAM_WD = 0.01
ADAM_STEP = 10
N_CHIPS = 4
N_DEV = 8
LANES = 128
SUBLANES = 8
HALO = 8
VMEM_LIMIT = 56 * 1024 * 1024
MESH = pl.DeviceIdType.MESH
ANY = pl.BlockSpec(memory_space=pl.ANY)
VMEM_WHOLE = pl.BlockSpec(memory_space=pltpu.VMEM)


def _params(sem=None):
    return pltpu.CompilerParams(dimension_semantics=sem, vmem_limit_bytes=VMEM_LIMIT)


def _pick(n, target, mult=LANES):
    best = None
    t = mult
    while t <= min(n, target):
        if n % t == 0:
            best = t
        t += mult
    return best if best is not None else n


ROW_TILE_BYTES = 2 << 20


def _tile_rows(s, width, mult=SUBLANES):
    return _pick(s, max(mult, ROW_TILE_BYTES // (4 * width)), mult)


def _sigmoid(v):
    return 1.0 / (1.0 + jnp.exp(-v))


def _silu(v):
    return v * _sigmoid(v)


def _dsilu(v):
    s = _sigmoid(v)
    return s * (1.0 + v * (1.0 - s))


def _softplus(v):
    e = jnp.exp(-jnp.abs(v))
    return jnp.maximum(v, 0.0) + jnp.where(e < 1e-4, e - 0.5 * e * e, jnp.log(1.0 + e))


def _dot(a, b):
    return jnp.dot(a, b, preferred_element_type=F32)


def _dot_nt(a, b):
    return lax.dot_general(a, b, (((1,), (1,)), ((), ())), preferred_element_type=F32)


def _dot_tn(a, b):
    return lax.dot_general(a, b, (((0,), (0,)), ((), ())), preferred_element_type=F32)


def _split3(v):
    hi = v.astype(BF16)
    r1 = v - hi.astype(F32)
    mid = r1.astype(BF16)
    lo = (r1 - mid.astype(F32)).astype(BF16)
    return hi, mid, lo


def _dot3_r(v, onehot):
    hi, mid, lo = _split3(v)
    return _dot(hi, onehot) + _dot(mid, onehot) + _dot(lo, onehot)


def _dot3_l(onehot, v):
    hi, mid, lo = _split3(v)
    return _dot(onehot, hi) + _dot(onehot, mid) + _dot(onehot, lo)


MATMUL_VMEM_BUDGET = 44 * 1024 * 1024


def _matmul(a, b, *, name, ta=False, tb=False, out_dtype=F32, b_blocks=1, out_blocks=1,
            tm=1024, tn=1024, tk=4096, comm=None, epilogue=None, extra=None):
    if ta:
        K, M = a.shape
    else:
        M, K = a.shape
    if b_blocks > 1:
        nb, r_, c_ = b.shape
        if tb:
            N, K2 = r_, c_ * nb
        else:
            K2, N = r_, c_ * nb
    else:
        if tb:
            N, K2 = b.shape
        else:
            K2, N = b.shape
    assert K == K2, (a.shape, b.shape, ta, tb)
    assert not (ta and tb)
    n_unit = N // b_blocks if (b_blocks > 1 and not tb) else N
    n_unit = min(n_unit, N // out_blocks)
    tn = _pick(n_unit, tn)
    k_unit = K // b_blocks if (b_blocks > 1 and tb) else K
    tk = _pick(k_unit, tk)
    tile_bytes = {None: jnp.dtype(out_dtype).itemsize, "relu2": 6, "relu2_bwd": 6, "dx": 12}[epilogue]

    def vmem_need(tm_):
        need = 2 * (tm_ * tk * a.dtype.itemsize + tk * tn * b.dtype.itemsize) + 2 * tm_ * tn * tile_bytes
        return need + (tm_ * tn * 4 if K > tk else 0)

    tm = _pick(M, tm)
    while vmem_need(tm) > MATMUL_VMEM_BUDGET and tm % 2 == 0 and tm // 2 >= LANES:
        tm //= 2
    gm, gn, gk = M // tm, N // tn, K // tk

    if ta:
        a_spec = pl.BlockSpec((tk, tm), lambda i, j, k: (k, i))
    else:
        a_spec = pl.BlockSpec((tm, tk), lambda i, j, k: (i, k))
    if b_blocks > 1 and not tb:
        per = (N // b_blocks) // tn
        b_spec = pl.BlockSpec((None, tk, tn), lambda i, j, k: (j // per, k, j % per))
    elif b_blocks > 1 and tb:
        per = (K // b_blocks) // tk
        b_spec = pl.BlockSpec((None, tn, tk), lambda i, j, k: (k // per, j, k % per))
    elif tb:
        b_spec = pl.BlockSpec((tn, tk), lambda i, j, k: (j, k))
    else:
        b_spec = pl.BlockSpec((tk, tn), lambda i, j, k: (k, j))
    if out_blocks > 1:
        per_o = (N // out_blocks) // tn
        o_spec = pl.BlockSpec((None, tm, tn), lambda i, j, k: (j // per_o, i, j % per_o))
        o_shape = jax.ShapeDtypeStruct((out_blocks, M, N // out_blocks), out_dtype)
    else:
        o_spec = pl.BlockSpec((tm, tn), lambda i, j, k: (i, j))
        o_shape = jax.ShapeDtypeStruct((M, N), out_dtype)

    in_specs, args = [a_spec, b_spec], [a, b]
    out_specs, out_shape = [o_spec], [o_shape]
    if epilogue == "relu2":
        assert out_blocks == 1 and out_dtype == F32
        out_specs.append(o_spec)
        out_shape.append(jax.ShapeDtypeStruct((M, N), BF16))
    elif epilogue == "relu2_bwd":
        assert out_blocks == 1 and out_dtype == BF16
        in_specs.append(o_spec)
        args.append(extra)
    elif epilogue == "dx":
        assert out_blocks == 1 and out_dtype == F32
        col_spec = pl.BlockSpec((1, tn), lambda i, j, k: (0, j))
        in_specs += [o_spec, o_spec, col_spec]
        args += list(extra)
        out_specs += [col_spec, col_spec]
        out_shape += [jax.ShapeDtypeStruct((1, N), F32)] * 2
    n_in = len(in_specs)
    grid = (gm, gn, gk)
    row_axis = 0
    if epilogue == "dx":
        def swapped(spec):
            return pl.BlockSpec(spec.block_shape, lambda j, i, k, f=spec.index_map: f(i, j, k))
        in_specs, out_specs = [swapped(sp) for sp in in_specs], [swapped(sp) for sp in out_specs]
        grid, row_axis = (gn, gm, gk), 1

    def write(refs, p):
        o_ref = refs[n_in]
        if epilogue == "relu2":
            o_ref[...] = p
            r = jnp.maximum(p, 0.0)
            refs[n_in + 1][...] = (r * r).astype(BF16)
        elif epilogue == "relu2_bwd":
            o_ref[...] = (p * 2.0 * jnp.maximum(refs[2][...], 0.0)).astype(BF16)
        elif epilogue == "dx":
            o_ref[...] = refs[2][...] + p * (1.0 + refs[4][...])
            s1 = jnp.sum(p * refs[3][...], axis=0, keepdims=True)
            s2 = jnp.sum(p, axis=0, keepdims=True)
            first = pl.program_id(row_axis) == 0

            @pl.when(first)
            def _():
                refs[n_in + 1][...] = s1
                refs[n_in + 2][...] = s2

            @pl.when(jnp.logical_not(first))
            def _():
                refs[n_in + 1][...] += s1
                refs[n_in + 2][...] += s2
        else:
            o_ref[...] = p.astype(out_dtype)

    def body(*refs):
        av = refs[0][...].astype(BF16)
        bv = refs[1][...].astype(BF16)
        p = _dot_tn(av, bv) if ta else (_dot_nt(av, bv) if tb else _dot(av, bv))
        if gk == 1:
            write(refs, p)
            return
        acc = refs[-1]
        k = pl.program_id(2)

        @pl.when(k == 0)
        def _():
            acc[...] = p

        @pl.when(jnp.logical_and(k > 0, k < gk - 1))
        def _():
            acc[...] += p

        @pl.when(k == gk - 1)
        def _():
            write(refs, acc[...] + p)

    scratch = [pltpu.VMEM((tm, tn), F32)] if gk > 1 else []
    if comm is not None:
        outs, landed = _comm_call(body, name=name, grid=grid, in_specs=in_specs, out_specs=out_specs,
                                  out_shape=out_shape, scratch_shapes=scratch, args=args, comm=comm)
        return (outs[0] if len(outs) == 1 else tuple(outs)), landed
    outs = pl.pallas_call(
        body, name=name, grid=grid, in_specs=in_specs, out_specs=out_specs,
        out_shape=out_shape, scratch_shapes=scratch,
        compiler_params=_params(("parallel", "arbitrary" if epilogue == "dx" else "parallel", "arbitrary")),
    )(*args)
    return outs[0] if len(outs) == 1 else tuple(outs)


def _comm_call(body, *, name, grid, in_specs, out_specs, out_shape, scratch_shapes, args, comm, aliases=None):
    n_in, n_out, n_scr = len(in_specs), len(out_shape), len(scratch_shapes)
    c_in, c_out = list(comm["inputs"]), list(comm["out_shape"])
    nci, nco = len(c_in), len(c_out)

    def body2(*refs):
        ins, cins = refs[:n_in], refs[n_in:n_in + nci]
        o0 = n_in + nci
        outs, couts = refs[o0:o0 + n_out], refs[o0 + n_out:o0 + n_out + nco]
        s0 = o0 + n_out + nco
        scr, cscr = refs[s0:s0 + n_scr], refs[s0 + n_scr:]
        first = functools.reduce(jnp.logical_and, [pl.program_id(a) == 0 for a in range(len(grid))])
        last = functools.reduce(jnp.logical_and, [pl.program_id(a) == grid[a] - 1 for a in range(len(grid))])

        @pl.when(first)
        def _():
            comm["start"](cins, couts, cscr)

        body(*ins, *outs, *scr)

        @pl.when(last)
        def _():
            comm["finish"](cins, couts, cscr)

    res = pl.pallas_call(
        body2, name=name, grid=grid, in_specs=list(in_specs) + [ANY] * nci, out_specs=list(out_specs) + [ANY] * nco,
        out_shape=list(out_shape) + c_out, scratch_shapes=list(scratch_shapes) + list(comm["scratch"]),
        input_output_aliases={**(aliases or {}), **{n_in + k: n_out + v for k, v in comm.get("aliases", {}).items()}},
        compiler_params=_params(("arbitrary",) * len(grid)),
    )(*args, *c_in)
    return res[:n_out], res[n_out:]


def _row(tr, w, blk=0):
    return pl.BlockSpec((tr, w), lambda i: (i, blk))


def _full(shape):
    nd = len(shape)
    return pl.BlockSpec(shape, lambda i: (0,) * nd)


def _halo_specs(s, tr, w, blk=0):
    per = tr // HALO
    last = s // HALO - 1
    return [
        pl.BlockSpec((HALO, w), lambda i: (jnp.maximum(i * per - 1, 0), blk)),
        pl.BlockSpec((tr, w), lambda i: (i, blk)),
        pl.BlockSpec((HALO, w), lambda i: (jnp.minimum((i + 1) * per, last), blk)),
    ]


def _ext(prev_ref, cur_ref, next_ref, s, tr):
    i = pl.program_id(0)
    e = jnp.concatenate([prev_ref[...].astype(F32), cur_ref[...].astype(F32), next_ref[...].astype(F32)], axis=0)
    rid = i * tr - HALO + lax.broadcasted_iota(jnp.int32, e.shape, 0)
    return jnp.where((rid >= 0) & (rid < s), e, 0.0)


def _valid_rows(shape, s, tr):
    i = pl.program_id(0)
    rid = i * tr - HALO + lax.broadcasted_iota(jnp.int32, shape, 0)
    return (rid >= 0) & (rid < s)


def _shift(e, k):
    if k == 0:
        return e
    n = e.shape[0]
    return pltpu.roll(e, (n - k) % n, 0)


def _acc_rows(ref, v):
    s = jnp.sum(v, axis=0, keepdims=True)

    @pl.when(pl.program_id(0) == 0)
    def _():
        ref[...] = s

    @pl.when(pl.program_id(0) > 0)
    def _():
        ref[...] += s


def _rowcall(body, name, s, tr, in_specs, out_specs, out_shape, args, comm=None, aliases=None):
    aliases = aliases or {}
    if comm is not None:
        single = not isinstance(out_shape, (list, tuple))
        outs, landed = _comm_call(body, name=name, grid=(s // tr,), in_specs=in_specs,
                                  out_specs=[out_specs] if single else out_specs,
                                  out_shape=[out_shape] if single else out_shape, scratch_shapes=[], args=args, comm=comm,
                                  aliases=aliases)
        return (outs[0] if single else outs), landed
    return pl.pallas_call(
        body, name=name, grid=(s // tr,), in_specs=in_specs, out_specs=out_specs, out_shape=out_shape,
        input_output_aliases=aliases, compiler_params=_params(("arbitrary",)),
    )(*args)


def _modulate(x, scale, shift, *, name, comm=None):
    s, d = x.shape
    tr = _tile_rows(s, d)

    def body(x_ref, sc_ref, sh_ref, o_ref):
        o_ref[...] = (x_ref[...] * (1.0 + sc_ref[...]) + sh_ref[...]).astype(BF16)

    return _rowcall(body, name, s, tr, [_row(tr, d), _full((1, d)), _full((1, d))], _row(tr, d),
                    jax.ShapeDtypeStruct((s, d), BF16), (x, scale, shift), comm=comm)


def _ln_stats(r):
    mu = jnp.mean(r, axis=-1, keepdims=True)
    xc = r - mu
    var = jnp.mean(xc * xc, axis=-1, keepdims=True)
    rstd = lax.rsqrt(var + LN_EPS)
    return xc * rstd, rstd


def _ln1_fwd(x, mix, gate, g, b, scale2, shift2, comm=None):
    s, d = x.shape
    tr = _tile_rows(s, d)

    def body(x_ref, m_ref, gt_ref, g_ref, b_ref, sc_ref, sh_ref, x1_ref, h2_ref):
        r = DEEPNORM_ALPHA * x_ref[...] + (1.0 + gt_ref[...]) * m_ref[...]
        xh, _ = _ln_stats(r)
        x1 = xh * g_ref[...] + b_ref[...]
        x1_ref[...] = x1
        h2_ref[...] = (x1 * (1.0 + sc_ref[...]) + sh_ref[...]).astype(BF16)

    v = _full((1, d))
    return _rowcall(body, "ln1_fwd", s, tr, [_row(tr, d), _row(tr, d), v, v, v, v, v],
                    [_row(tr, d), _row(tr, d)],
                    [jax.ShapeDtypeStruct((s, d), F32), jax.ShapeDtypeStruct((s, d), BF16)],
                    (x, mix, gate, g, b, scale2, shift2), comm=comm)


def _ln2_loss_bwd(x1, f, target, gate, g, b):
    s, d = x1.shape
    tr = _tile_rows(s, d)

    def body(x1_ref, f_ref, t_ref, gt_ref, g_ref, b_ref, df_ref, dr_ref, loss_ref, dg_ref, db_ref, dgt_ref):
        fv = f_ref[...]
        r = DEEPNORM_ALPHA * x1_ref[...] + (1.0 + gt_ref[...]) * fv
        xh, rstd = _ln_stats(r)
        y = xh * g_ref[...] + b_ref[...]
        err = y - t_ref[...]
        _acc_rows(loss_ref, 0.5 * jnp.mean(err * err, axis=-1, keepdims=True))
        dy = err * (1.0 / d)
        _acc_rows(dg_ref, dy * xh)
        _acc_rows(db_ref, dy)
        dxh = dy * g_ref[...]
        dr = rstd * (dxh - jnp.mean(dxh, axis=-1, keepdims=True) - xh * jnp.mean(dxh * xh, axis=-1, keepdims=True))
        dr_ref[...] = dr
        df_ref[...] = ((1.0 + gt_ref[...]) * dr).astype(BF16)
        _acc_rows(dgt_ref, dr * fv)

    v = _full((1, d))
    one = _full((1, 1))
    return _rowcall(body, "ln2_loss_bwd", s, tr, [_row(tr, d), _row(tr, d), _row(tr, d), v, v, v],
                    [_row(tr, d), _row(tr, d), one, v, v, v],
                    [jax.ShapeDtypeStruct((s, d), BF16), jax.ShapeDtypeStruct((s, d), F32),
                     jax.ShapeDtypeStruct((1, 1), F32)] + [jax.ShapeDtypeStruct((1, d), F32)] * 3,
                    (x1, f, target, gate, g, b))


def _ln1_bwd(dh2, dr2, x1, x, mix, scale2, gate1, g1, comm=None):
    s, d = x.shape
    tr = _tile_rows(s, d)

    def body(dh_ref, dr2_ref, x1_ref, x_ref, m_ref, sc_ref, gt_ref, g_ref,
             dm_ref, dxa_ref, dsc_ref, dsh_ref, dg_ref, db_ref, dgt_ref):
        dh = dh_ref[...]
        _acc_rows(dsc_ref, dh * x1_ref[...])
        _acc_rows(dsh_ref, dh)
        dy = dh * (1.0 + sc_ref[...]) + DEEPNORM_ALPHA * dr2_ref[...]
        mv = m_ref[...]
        r = DEEPNORM_ALPHA * x_ref[...] + (1.0 + gt_ref[...]) * mv
        xh, rstd = _ln_stats(r)
        _acc_rows(dg_ref, dy * xh)
        _acc_rows(db_ref, dy)
        dxh = dy * g_ref[...]
        dr = rstd * (dxh - jnp.mean(dxh, axis=-1, keepdims=True) - xh * jnp.mean(dxh * xh, axis=-1, keepdims=True))
        dm_ref[...] = ((1.0 + gt_ref[...]) * dr).astype(BF16)
        dxa_ref[...] = DEEPNORM_ALPHA * dr
        _acc_rows(dgt_ref, dr * mv)

    v = _full((1, d))
    return _rowcall(body, "ln1_bwd", s, tr, [_row(tr, d)] * 5 + [v, v, v],
                    [_row(tr, d), _row(tr, d), v, v, v, v, v],
                    [jax.ShapeDtypeStruct((s, d), BF16), jax.ShapeDtypeStruct((s, d), F32)]
                    + [jax.ShapeDtypeStruct((1, d), F32)] * 5,
                    (dh2, dr2, x1, x, mix, scale2, gate1, g1), comm=comm)


P_X, P_BC, P_Z, P_H, P_B, P_C = range(6)


def _conv_silu_fwd(proj, conv_w, conv_b, s, u, comm=None):
    tr = _tile_rows(s, u)
    w = 2 * u
    half = SSM_CONV // 2

    def body(p0, c0, n0, p1, c1, n1, w_ref, b_ref, o_ref):
        for blk, (pr, cr, nr) in enumerate(((p0, c0, n0), (p1, c1, n1))):
            e = _ext(pr, cr, nr, s, tr)
            wv = w_ref[:, blk * u:(blk + 1) * u]
            acc = jnp.zeros_like(e)
            for k in range(SSM_CONV):
                acc = acc + _shift(e, k - half) * wv[k:k + 1, :]
            pre = acc[HALO:HALO + tr] + b_ref[:, blk * u:(blk + 1) * u]
            o_ref[:, blk * u:(blk + 1) * u] = _silu(pre)

    in_specs = _halo_specs(s, tr, u, P_X) + _halo_specs(s, tr, u, P_BC) + [_full((SSM_CONV, w)), _full((1, w))]
    return _rowcall(body, "conv_silu_fwd", s, tr, in_specs, _row(tr, w), jax.ShapeDtypeStruct((s, w), F32),
                    (proj,) * 6 + (conv_w, conv_b), comm=comm)


def _conv_silu_bwd(proj, dxbc, conv_w, conv_b, dproj, s, u, comm=None):
    tr = _tile_rows(s, u)
    w = 2 * u
    half = SSM_CONV // 2

    def body(p0, c0, n0, p1, c1, n1, dp0, dc0, dn0, dp1, dc1, dn1, w_ref, b_ref, dproj_in, du_ref, dw_ref, db_ref):
        for blk, (ur, dr) in enumerate((((p0, c0, n0), (dp0, dc0, dn0)), ((p1, c1, n1), (dp1, dc1, dn1)))):
            e = _ext(*ur, s, tr)
            de = _ext(*dr, s, tr)
            wv = w_ref[:, blk * u:(blk + 1) * u]
            acc = jnp.zeros_like(e)
            for k in range(SSM_CONV):
                acc = acc + _shift(e, k - half) * wv[k:k + 1, :]
            pre = acc + b_ref[:, blk * u:(blk + 1) * u]
            dpre = jnp.where(_valid_rows(e.shape, s, tr), de * _dsilu(pre), 0.0)
            du = jnp.zeros_like(e)
            rows = []
            for k in range(SSM_CONV):
                du = du + _shift(dpre, half - k) * wv[k:k + 1, :]
                rows.append(jnp.sum((_shift(e, k - half) * dpre)[HALO:HALO + tr], axis=0, keepdims=True))
            du_ref[:, blk * u:(blk + 1) * u] = du[HALO:HALO + tr].astype(BF16)
            dwv = jnp.concatenate(rows + [jnp.zeros((SUBLANES - SSM_CONV, u), F32)], axis=0)
            dbv = jnp.sum(dpre[HALO:HALO + tr], axis=0, keepdims=True)
            first = pl.program_id(0) == 0

            @pl.when(first)
            def _():
                dw_ref[:, blk * u:(blk + 1) * u] = dwv
                db_ref[:, blk * u:(blk + 1) * u] = dbv

            @pl.when(jnp.logical_not(first))
            def _():
                dw_ref[:, blk * u:(blk + 1) * u] += dwv
                db_ref[:, blk * u:(blk + 1) * u] += dbv

    in_specs = (_halo_specs(s, tr, u, P_X) + _halo_specs(s, tr, u, P_BC) + _halo_specs(s, tr, u, 0)
                + _halo_specs(s, tr, u, 1) + [_full((SSM_CONV, w)), _full((1, w)), ANY])
    return _rowcall(body, "conv_silu_bwd", s, tr, in_specs,
                    [_row(tr, w), _full((SUBLANES, w)), _full((1, w))],
                    [jax.ShapeDtypeStruct(dproj.shape, BF16), jax.ShapeDtypeStruct((SUBLANES, w), F32),
                     jax.ShapeDtypeStruct((1, w), F32)],
                    (proj,) * 6 + (dxbc,) * 6 + (conv_w, conv_b, dproj), comm=comm, aliases={14: 0})


def _expanders(h):
    col64 = jnp.arange(2 * h * HEAD_DIM) // HEAD_DIM
    col128 = jnp.arange(2 * h * LANES) // LANES
    row = jnp.arange(LANES)[:, None]
    return (row == col64[None, :]).astype(BF16), (row == col128[None, :]).astype(BF16)


def _dt_prep(proj, bias_row, a_row, s, u, h, comm=None):
    q = CHUNK
    e64, e128 = _expanders(h)
    ds = h * HEAD_DIM
    dtblk = (6 * u) // LANES

    def body(raw_ref, b_ref, a_ref, e64_ref, e128_ref, dt_ref, cum_ref, dte_ref, cume_ref):
        lane = lax.broadcasted_iota(jnp.int32, (q, LANES), 1)
        dt = jnp.where(lane < 2 * h, _softplus(raw_ref[...] + b_ref[...]), 0.0)
        da = dt * a_ref[...]
        ii = lax.broadcasted_iota(jnp.int32, (q, q), 0)
        kk = lax.broadcasted_iota(jnp.int32, (q, q), 1)
        lower = (kk <= ii).astype(F32).astype(BF16)
        upper = (kk >= ii).astype(F32).astype(BF16)
        cum = jnp.where(lane < h, _dot3_l(lower, da), _dot3_l(upper, da))
        dt_ref[...] = dt
        cum_ref[...] = cum
        dte = _dot3_r(dt, e64_ref[...])
        cume = _dot3_r(cum, e128_ref[...])
        dte_ref[0] = dte[:, :ds]
        dte_ref[1] = dte[:, ds:]
        cume_ref[0] = cume[:, :h * LANES]
        cume_ref[1] = cume[:, h * LANES:]

    in_specs = [pl.BlockSpec((q, LANES), lambda i: (i, dtblk)), _full((1, LANES)), _full((1, LANES)),
                _full(e64.shape), _full(e128.shape)]
    out_specs = [_row(q, LANES), _row(q, LANES),
                 pl.BlockSpec((2, q, ds), lambda i: (0, i, 0)), pl.BlockSpec((2, q, h * LANES), lambda i: (0, i, 0))]
    out_shape = [jax.ShapeDtypeStruct((s, LANES), F32), jax.ShapeDtypeStruct((s, LANES), F32),
                 jax.ShapeDtypeStruct((2, s, ds), F32), jax.ShapeDtypeStruct((2, s, h * LANES), F32)]
    return _rowcall(body, "dt_prep", s, q, in_specs, out_specs, out_shape, (proj, bias_row, a_row, e64, e128), comm=comm)


def _ssd_specs(s, h, g):
    q = CHUNK
    nc = s // q
    ds = h * HEAD_DIM
    nb = g * STATE
    return q, nc, ds, nb


def _ssd_fwd(xbc, dt_e, cum_e, cum_t, s, h, g, comm=None):
    q, nc, ds, nb = _ssd_specs(s, h, g)
    npair = h // 2

    def cidx(d, i):
        return jnp.where(d == 0, i, nc - 1 - i)

    def body(x_ref, b_ref, c_ref, dt_ref, cum_ref, cumt_ref, y_ref, sp_ref, st):
        d = pl.program_id(0)
        i = pl.program_id(1)

        @pl.when(i == 0)
        def _():
            st[...] = jnp.zeros_like(st)

        rev = d == 1
        ii = lax.broadcasted_iota(jnp.int32, (q, q), 0)
        jj = lax.broadcasted_iota(jnp.int32, (q, q), 1)
        sgn = jnp.where(rev, -1, 1)
        mask = (jj - ii) * sgn <= 0
        left = lax.broadcasted_iota(jnp.int32, (q, LANES), 1) < HEAD_DIM

        def group(gi, carry):
            goff = pl.multiple_of(gi * STATE, STATE)
            cg = c_ref[:, pl.ds(goff, STATE)].astype(BF16)
            bg = b_ref[:, pl.ds(goff, STATE)].astype(BF16)
            gm = _dot_nt(cg, bg)
            for p in range(HEADS_PER_GROUP // 2):
                pr = gi * (HEADS_PER_GROUP // 2) + p
                off = pl.multiple_of(pr * LANES, LANES)
                xd = x_ref[:, pl.ds(off, LANES)] * dt_ref[:, pl.ds(off, LANES)]
                ms = []
                cols = []
                for hl in range(2):
                    hh = 2 * pr + hl
                    col = cum_ref[:, pl.ds(pl.multiple_of(hh * LANES, LANES), LANES)]
                    row = cumt_ref[pl.ds(hh, 1), :]
                    lm = jnp.where(mask, jnp.exp(jnp.minimum(col - row, 0.0)), 0.0)
                    ms.append((gm * lm).astype(BF16))
                    cols.append(col)
                y = _dot(ms[0], jnp.where(left, xd, 0.0).astype(BF16)) + _dot(ms[1], jnp.where(left, 0.0, xd).astype(BF16))
                ce = jnp.where(left, cols[0], cols[1])
                sprev = st[pr]
                sp_ref[pr] = sprev
                y = y + jnp.exp(ce) * _dot(cg, sprev.astype(BF16))
                y_ref[:, pl.ds(off, LANES)] = y
                tot = jnp.where(rev, ce[0:1, :], ce[q - 1:q, :])
                v = (xd * jnp.exp(tot - ce)).astype(BF16)
                st[pr] = jnp.exp(tot) * sprev + _dot_tn(bg, v)
            return carry

        lax.fori_loop(0, g, group, 0, unroll=4 if g % 4 == 0 else 2)

    in_specs = [
        pl.BlockSpec((q, ds), lambda d, i: (cidx(d, i), 0)),
        pl.BlockSpec((q, nb), lambda d, i: (cidx(d, i), ds // nb)),
        pl.BlockSpec((q, nb), lambda d, i: (cidx(d, i), ds // nb + 1)),
        pl.BlockSpec((None, q, ds), lambda d, i: (d, cidx(d, i), 0)),
        pl.BlockSpec((None, q, h * LANES), lambda d, i: (d, cidx(d, i), 0)),
        pl.BlockSpec((None, h, q), lambda d, i: (d, 0, cidx(d, i))),
    ]
    out_specs = [
        pl.BlockSpec((None, q, ds), lambda d, i: (d, cidx(d, i), 0)),
        pl.BlockSpec((None, None, npair, STATE, LANES), lambda d, i: (d, cidx(d, i), 0, 0, 0)),
    ]
    out_shape = [jax.ShapeDtypeStruct((2, s, ds), F32), jax.ShapeDtypeStruct((2, nc, npair, STATE, LANES), F32)]
    if comm is not None:
        return _comm_call(body, name="ssd_fwd", grid=(2, nc), in_specs=in_specs, out_specs=out_specs, out_shape=out_shape,
                          scratch_shapes=[pltpu.VMEM((npair, STATE, LANES), F32)],
                          args=(xbc, xbc, xbc, dt_e, cum_e, cum_t), comm=comm)
    return pl.pallas_call(
        body, name="ssd_fwd", grid=(2, nc), in_specs=in_specs, out_specs=out_specs, out_shape=out_shape,
        scratch_shapes=[pltpu.VMEM((npair, STATE, LANES), F32)],
        compiler_params=_params(("arbitrary", "arbitrary")),
    )(xbc, xbc, xbc, dt_e, cum_e, cum_t), ()


def _ssd_bwd(xbc, dt_e, cum_e, cum_t, dt_t, a_col, dy, sp, s, h, g, comm=None):
    q, nc, ds, nb = _ssd_specs(s, h, g)
    npair = h // 2

    def cidx(d, i):
        return jnp.where(d == 0, nc - 1 - i, i)

    def body(x_ref, b_ref, c_ref, dt_ref, cum_ref, cumt_ref, dtt_ref, a_ref, dy_ref, sp_ref,
             dx_ref, db_ref, dc_ref, ddt_ref, da_ref, dst, rowp):
        d = pl.program_id(0)
        i = pl.program_id(1)

        @pl.when(i == 0)
        def _():
            dst[...] = jnp.zeros_like(dst)
            da_ref[...] = jnp.zeros_like(da_ref)

        rev = d == 1
        ii = lax.broadcasted_iota(jnp.int32, (q, q), 0)
        jj = lax.broadcasted_iota(jnp.int32, (q, q), 1)
        sgn = jnp.where(rev, -1, 1)
        mask = (jj - ii) * sgn <= 0
        lane = lax.broadcasted_iota(jnp.int32, (q, LANES), 1)
        left = lane < HEAD_DIM
        rowp[...] = jnp.zeros_like(rowp)

        def group(gi, carry):
            acc_dcum, acc_tot, acc_dxx = carry
            goff = pl.multiple_of(gi * STATE, STATE)
            cg = c_ref[:, pl.ds(goff, STATE)].astype(BF16)
            bg = b_ref[:, pl.ds(goff, STATE)].astype(BF16)
            gm = _dot_nt(cg, bg)
            dgm = jnp.zeros((q, q), F32)
            dcg = jnp.zeros((q, STATE), F32)
            dbg = jnp.zeros((q, STATE), F32)
            for p in range(HEADS_PER_GROUP // 2):
                pr = gi * (HEADS_PER_GROUP // 2) + p
                off = pl.multiple_of(pr * LANES, LANES)
                xv = x_ref[:, pl.ds(off, LANES)]
                dte = dt_ref[:, pl.ds(off, LANES)]
                xd = xv * dte
                xdb = xd.astype(BF16)
                dyv = dy_ref[:, pl.ds(off, LANES)]
                sprev = sp_ref[pr]
                sprevb = sprev.astype(BF16)
                dsn = dst[pr]
                dsnb = dsn.astype(BF16)
                cols = [cum_ref[:, pl.ds(pl.multiple_of((2 * pr + hl) * LANES, LANES), LANES)] for hl in range(2)]
                ce = jnp.where(left, cols[0], cols[1])
                tot = jnp.where(rev, ce[0:1, :], ce[q - 1:q, :])
                et = jnp.exp(tot)
                r = jnp.exp(tot - ce)
                e = jnp.exp(ce)
                yoff = e * _dot(cg, sprevb)
                dz = (e * dyv).astype(BF16)
                dcg = dcg + _dot_nt(dz, sprevb)
                dsprev = _dot_tn(cg, dz) + et * dsn
                f1 = dyv * yoff
                v = (xd * r).astype(BF16)
                dbg = dbg + _dot_nt(v, dsnb)
                dv = _dot(bg, dsnb)
                dxd = dv * r
                tt = dv * xd * r
                wt = dsn * sprev * et
                for hl in range(2):
                    hh = 2 * pr + hl
                    hm = left if hl == 0 else jnp.logical_not(left)
                    row = cumt_ref[pl.ds(hh, 1), :]
                    lm = jnp.where(mask, jnp.exp(jnp.minimum(cols[hl] - row, 0.0)), 0.0)
                    mf = gm * lm
                    dym = jnp.where(hm, dyv, 0.0).astype(BF16)
                    dm = _dot_nt(dym, xdb)
                    dxd = dxd + _dot_tn(mf.astype(BF16), dym)
                    dgm = dgm + dm * lm
                    em = dm * mf
                    rowp[pl.ds(hh, 1), :] = rowp[pl.ds(hh, 1), :] - jnp.sum(em, axis=0, keepdims=True)
                    colq = (jnp.sum(em, axis=1, keepdims=True)
                            + jnp.sum(jnp.where(hm, f1 - tt, 0.0), axis=1, keepdims=True))
                    acc_dcum = jnp.where(lane == hh, colq, acc_dcum)
                    totq = jnp.sum(jnp.sum(jnp.where(hm, tt + wt, 0.0), axis=1, keepdims=True), axis=0, keepdims=True)
                    acc_tot = jnp.where(lane == hh, totq, acc_tot)
                dxx = dxd * xv
                for hl in range(2):
                    hh = 2 * pr + hl
                    hm = left if hl == 0 else jnp.logical_not(left)
                    acc_dxx = jnp.where(lane == hh, jnp.sum(jnp.where(hm, dxx, 0.0), axis=1, keepdims=True), acc_dxx)
                dx_ref[:, pl.ds(off, LANES)] = dxd * dte
                dst[pr] = dsprev
            dgb = dgm.astype(BF16)
            dc_ref[:, pl.ds(goff, STATE)] = dcg + _dot(dgb, bg)
            db_ref[:, pl.ds(goff, STATE)] = dbg + _dot_tn(dgb, cg)
            return acc_dcum, acc_tot, acc_dxx

        zero = jnp.zeros((q, LANES), F32)
        per_iter = 2

        def some_groups(j, carry):
            for k in range(per_iter):
                carry = group(per_iter * j + k, carry)
            return carry

        acc_dcum, acc_tot, acc_dxx = lax.fori_loop(0, g // per_iter, some_groups, (zero, zero, zero))
        dcum_t = rowp[...] + jnp.transpose(acc_dcum)[:h]
        rmat = ((ii - jj) * sgn >= 0).astype(F32).astype(BF16)
        da_t = _dot3_r(dcum_t, rmat) + jnp.transpose(acc_tot)[:h]
        ddt_ref[...] = da_t * a_ref[...] + jnp.transpose(acc_dxx)[:h]
        da_ref[...] += da_t * dtt_ref[...]

    in_specs = [
        pl.BlockSpec((q, ds), lambda d, i: (cidx(d, i), 0)),
        pl.BlockSpec((q, nb), lambda d, i: (cidx(d, i), ds // nb)),
        pl.BlockSpec((q, nb), lambda d, i: (cidx(d, i), ds // nb + 1)),
        pl.BlockSpec((None, q, ds), lambda d, i: (d, cidx(d, i), 0)),
        pl.BlockSpec((None, q, h * LANES), lambda d, i: (d, cidx(d, i), 0)),
        pl.BlockSpec((None, h, q), lambda d, i: (d, 0, cidx(d, i))),
        pl.BlockSpec((None, h, q), lambda d, i: (d, 0, cidx(d, i))),
        pl.BlockSpec((None, h, LANES), lambda d, i: (d, 0, 0)),
        pl.BlockSpec((q, ds), lambda d, i: (cidx(d, i), 0)),
        pl.BlockSpec((None, None, npair, STATE, LANES), lambda d, i: (d, cidx(d, i), 0, 0, 0)),
    ]
    out_specs = [
        pl.BlockSpec((None, q, ds), lambda d, i: (d, cidx(d, i), 0)),
        pl.BlockSpec((None, q, nb), lambda d, i: (d, cidx(d, i), 0)),
        pl.BlockSpec((None, q, nb), lambda d, i: (d, cidx(d, i), 0)),
        pl.BlockSpec((None, h, q), lambda d, i: (d, 0, cidx(d, i))),
        pl.BlockSpec((None, h, LANES), lambda d, i: (d, 0, 0)),
    ]
    out_shape = [jax.ShapeDtypeStruct((2, s, ds), F32), jax.ShapeDtypeStruct((2, s, nb), F32),
                 jax.ShapeDtypeStruct((2, s, nb), F32), jax.ShapeDtypeStruct((2, h, s), F32),
                 jax.ShapeDtypeStruct((2, h, LANES), F32)]
    scratch = [pltpu.VMEM((npair, STATE, LANES), F32), pltpu.VMEM((h, q), F32)]
    args = (xbc, xbc, xbc, dt_e, cum_e, cum_t, dt_t, a_col, dy, sp)
    if comm is not None:
        return _comm_call(body, name="ssd_bwd", grid=(2, nc), in_specs=in_specs, out_specs=out_specs, out_shape=out_shape,
                          scratch_shapes=scratch, args=args, comm=comm)
    return pl.pallas_call(
        body, name="ssd_bwd", grid=(2, nc), in_specs=in_specs, out_specs=out_specs, out_shape=out_shape,
        scratch_shapes=scratch, compiler_params=_params(("arbitrary", "arbitrary")),
    )(*args), ()


def _dt_bwd(ddt, proj, bias_row, dproj, s, u, h):
    tr = _tile_rows(s, 4 * LANES)
    dtblk = (6 * u) // LANES
    tail = dproj.shape[1] - 6 * u
    assert (6 * u) % tail == 0

    def body(d_ref, raw_ref, b_ref, dproj_in, o_ref, db_ref):
        lane = lax.broadcasted_iota(jnp.int32, (tr, LANES), 1)
        v = jnp.where(lane < 2 * h, d_ref[...] * _sigmoid(raw_ref[...] + b_ref[...]), 0.0)
        o_ref[:, :LANES] = v.astype(BF16)
        o_ref[:, LANES:] = jnp.zeros((tr, tail - LANES), BF16)
        _acc_rows(db_ref, v)

    return _rowcall(body, "dt_bwd", s, tr, [_row(tr, LANES), _row(tr, LANES, dtblk), _full((1, LANES)), ANY],
                    [_row(tr, tail, (6 * u) // tail), _full((1, LANES))],
                    [jax.ShapeDtypeStruct(dproj.shape, BF16), jax.ShapeDtypeStruct((1, LANES), F32)],
                    (ddt, proj, bias_row, dproj), aliases={3: 0})


def _group_rms(v, gw):
    outs, facs = [], []
    for k in range(v.shape[1] // gw):
        blk = v[:, k * gw:(k + 1) * gw]
        f = lax.rsqrt(jnp.mean(blk * blk, axis=-1, keepdims=True) + RMS_EPS)
        outs.append(blk * f)
        facs.append(jnp.broadcast_to(f, blk.shape))
    return jnp.concatenate(outs, axis=1), jnp.concatenate(facs, axis=1)


def _group_rms_bwd(dn, n, fac, gw):
    outs = []
    for k in range(n.shape[1] // gw):
        sl = slice(k * gw, (k + 1) * gw)
        outs.append(fac[:, sl] * (dn[:, sl] - n[:, sl] * jnp.mean(dn[:, sl] * n[:, sl], axis=-1, keepdims=True)))
    return jnp.concatenate(outs, axis=1)


def _gate_norm_fwd(y2, xbc, proj, d_e, norm_w, s, u, g):
    tr = _tile_rows(s, u)
    gw = u // g

    def body(y_ref, x_ref, z_ref, d_ref, w_ref, o_ref):
        ys = y_ref[0] + y_ref[1] + d_ref[...] * x_ref[...]
        n, _ = _group_rms(ys * _silu(z_ref[...]), gw)
        o_ref[...] = (n * w_ref[...]).astype(BF16)

    return _rowcall(body, "gate_norm_fwd", s, tr,
                    [pl.BlockSpec((2, tr, u), lambda i: (0, i, 0)), _row(tr, u), _row(tr, u, P_Z), _full((1, u)), _full((1, u))],
                    _row(tr, u), jax.ShapeDtypeStruct((s, u), BF16), (y2, xbc, proj, d_e, norm_w))


def _gate_norm_bwd(dymix, y2, xbc, proj, d_e, norm_w, pw, s, u, g, comm=None):
    tr = _tile_rows(s, u)
    gw = u // g

    def body(dy_ref, y_ref, x_ref, z_ref, d_ref, w_ref, dys_ref, dz_ref, dxs_ref, dw_ref, dd_ref):
        xv = x_ref[...]
        zv = z_ref[...]
        ys = y_ref[0] + y_ref[1] + d_ref[...] * xv
        sz = _silu(zv)
        n, fac = _group_rms(ys * sz, gw)
        dout = dy_ref[...]
        _acc_rows(dw_ref, dout * n)
        dyg = _group_rms_bwd(dout * w_ref[...], n, fac, gw)
        dys = dyg * sz
        dys_ref[...] = dys
        dz_ref[...] = (dyg * ys * _dsilu(zv)).astype(BF16)
        dxs_ref[...] = dys * d_ref[...]
        _acc_rows(dd_ref, dys * xv)

    v = _full((1, u))
    return _rowcall(body, "gate_norm_bwd", s, tr,
                    [_row(tr, u), pl.BlockSpec((2, tr, u), lambda i: (0, i, 0)), _row(tr, u), _row(tr, u, P_Z), v, v],
                    [_row(tr, u), _row(tr, u, P_Z), _row(tr, u), v, v],
                    [jax.ShapeDtypeStruct((s, u), F32), jax.ShapeDtypeStruct((s, pw), BF16),
                     jax.ShapeDtypeStruct((s, u), F32), jax.ShapeDtypeStruct((1, u), F32), jax.ShapeDtypeStruct((1, u), F32)],
                    (dymix, y2, xbc, proj, d_e, norm_w), comm=comm)


def _shortconv_fwd(proj, conv_w, norm_w, s, u):
    tr = _tile_rows(s, u)
    half = SC_CONV // 2

    def body(hp, hc, hn, b_ref, cp, cc, cn, cw_ref, w_ref, o_ref):
        t = _ext(hp, hc, hn, s, tr) * _ext(cp, cc, cn, s, tr)
        wv = cw_ref[...]
        acc = jnp.zeros_like(t)
        for k in range(SC_CONV):
            acc = acc + _shift(t, k - half) * wv[k:k + 1, :]
        n, _ = _group_rms(b_ref[...] * acc[HALO:HALO + tr], SC_GROUP_WIDTH)
        o_ref[...] = (n * w_ref[...]).astype(BF16)

    in_specs = (_halo_specs(s, tr, u, P_H) + [_row(tr, u, P_B)] + _halo_specs(s, tr, u, P_C)
                + [_full((SC_CONV, u)), _full((1, u))])
    return _rowcall(body, "shortconv_fwd", s, tr, in_specs, _row(tr, u), jax.ShapeDtypeStruct((s, u), BF16),
                    (proj,) * 7 + (conv_w, norm_w))


def _shortconv_bwd(dymix, proj, conv_w, norm_w, dproj, s, u):
    tr = _tile_rows(s, u)
    half = SC_CONV // 2

    def body(dp, dc_, dn, hp, hc, hn, bp, bc, bn, cp, cc, cn, cw_ref, w_ref, dproj_in, o_ref, dcw_ref, dw_ref):
        dout = _ext(dp, dc_, dn, s, tr)
        hv = _ext(hp, hc, hn, s, tr)
        bv = _ext(bp, bc, bn, s, tr)
        cv = _ext(cp, cc, cn, s, tr)
        t = hv * cv
        wv = cw_ref[...]
        acc = jnp.zeros_like(t)
        for k in range(SC_CONV):
            acc = acc + _shift(t, k - half) * wv[k:k + 1, :]
        n, fac = _group_rms(bv * acc, SC_GROUP_WIDTH)
        cur = slice(HALO, HALO + tr)
        _acc_rows(dw_ref, (dout * n)[cur])
        dyv = _group_rms_bwd(dout * w_ref[...], n, fac, SC_GROUP_WIDTH)
        o_ref[:, u:2 * u] = (dyv * acc)[cur].astype(BF16)
        dv = dyv * bv
        dt = jnp.zeros_like(t)
        rows = []
        for k in range(SC_CONV):
            dt = dt + _shift(dv, half - k) * wv[k:k + 1, :]
            rows.append(jnp.sum((_shift(t, k - half) * dv)[cur], axis=0, keepdims=True))
        o_ref[:, :u] = (dt * cv)[cur].astype(BF16)
        o_ref[:, 2 * u:] = (dt * hv)[cur].astype(BF16)
        dwv = jnp.concatenate(rows + [jnp.zeros((SUBLANES - SC_CONV, u), F32)], axis=0)
        first = pl.program_id(0) == 0

        @pl.when(first)
        def _():
            dcw_ref[...] = dwv

        @pl.when(jnp.logical_not(first))
        def _():
            dcw_ref[...] += dwv

    in_specs = (_halo_specs(s, tr, u, 1) + _halo_specs(s, tr, u, P_H) + _halo_specs(s, tr, u, P_B)
                + _halo_specs(s, tr, u, P_C) + [_full((SC_CONV, u)), _full((1, u)), ANY])
    return _rowcall(body, "shortconv_bwd", s, tr, in_specs,
                    [_row(tr, 3 * u, 1), _full((SUBLANES, u)), _full((1, u))],
                    [jax.ShapeDtypeStruct(dproj.shape, BF16), jax.ShapeDtypeStruct((SUBLANES, u), F32),
                     jax.ShapeDtypeStruct((1, u), F32)],
                    (dymix,) * 3 + (proj,) * 9 + (conv_w, norm_w, dproj), aliases={14: 0})


def _adam_math(w, g, m, v):
    m2 = ADAM_B1 * m + (1.0 - ADAM_B1) * g
    v2 = ADAM_B2 * v + (1.0 - ADAM_B2) * (g * g)
    m_hat = m2 / (1.0 - ADAM_B1 ** ADAM_STEP)
    v_hat = v2 / (1.0 - ADAM_B2 ** ADAM_STEP)
    delta = -ADAM_LR * (m_hat / (jnp.sqrt(v_hat) + ADAM_EPS) + ADAM_WD * w)
    return delta, m2, v2


def _adam_rows(r, c):
    return _pick(r, max(SUBLANES, (1 << 20) // (4 * c)), SUBLANES)


def _adam(w, g, m, v, *, name, emit_grad=False):
    r, c = w.shape
    tr = _adam_rows(r, c)
    n_out = 4 if emit_grad else 3

    def body(w_ref, g_ref, m_ref, v_ref, *outs):
        gv = g_ref[...]
        if emit_grad:
            outs[0][...] = gv
        outs[-3][...], outs[-2][...], outs[-1][...] = _adam_math(w_ref[...], gv, m_ref[...], v_ref[...])

    return _rowcall(body, name, r, tr, [_row(tr, c)] * 4, [_row(tr, c)] * n_out,
                    [jax.ShapeDtypeStruct((r, c), F32)] * n_out, (w, g, m, v))


def _adam_outer(w, a_t, bmat, m, v, *, name):
    r, c = w.shape
    tr = _adam_rows(r, c)
    kk = a_t.shape[1]

    def body(w_ref, a_ref, b_ref, m_ref, v_ref, g_ref, d_ref, m2_ref, v2_ref):
        g = _dot(a_ref[...].astype(BF16), b_ref[...].astype(BF16))
        g_ref[...] = g
        d_ref[...], m2_ref[...], v2_ref[...] = _adam_math(w_ref[...], g, m_ref[...], v_ref[...])

    return _rowcall(body, name, r, tr, [_row(tr, c), _row(tr, kk), _full((kk, c)), _row(tr, c), _row(tr, c)],
                    [_row(tr, c)] * 4, [jax.ShapeDtypeStruct((r, c), F32)] * 4, (w, a_t, bmat, m, v))


def _place():
    x, y, c = lax.axis_index("x"), lax.axis_index("y"), lax.axis_index("c")
    return x, y, c


DMA_CHUNKS = 8


def _n_chunks(rows):
    n = DMA_CHUNKS
    while n > 1 and rows % (16 * n):
        n //= 2
    return n


def _allgather_small(v, *, name):
    m_per, n = v.shape

    def body(x_ref, out_ref, send_sems, recv_sems, local_sem):
        x, y, c = _place()
        me, sibling = (x, y, c), (x, y, 1 - c)
        chips = [(1 - x, y), (x, 1 - y), (1 - x, 1 - y)]

        def rows(px, py, pc):
            return out_ref.at[pl.ds((4 * px + 2 * py + pc) * m_per, m_per), :]

        def copy(k, block, to, src=None):
            return pltpu.make_async_remote_copy(
                src_ref=rows(*block) if src is None else src, dst_ref=rows(*block),
                send_sem=send_sems.at[k], recv_sem=recv_sems.at[k], device_id=to, device_id_type=MESH)

        mine = pltpu.make_async_copy(x_ref, rows(*me), local_sem)
        mine.start()
        first = [copy(0, me, sibling, src=x_ref)]
        first += [copy(1 + j, me, (*chip, c), src=x_ref) for j, chip in enumerate(chips)]
        for cp in first:
            cp.start()
        passed = [copy(4 + j, (*chip, c), sibling) for j, chip in enumerate(chips)]
        for j, chip in enumerate(chips):
            copy(1 + j, (*chip, c), me).wait_recv()
            passed[j].start()
        copy(0, sibling, me).wait_recv()
        for j, chip in enumerate(chips):
            copy(4 + j, (*chip, 1 - c), me).wait_recv()
        for cp in first + passed:
            cp.wait_send()
        mine.wait()

    return pl.pallas_call(
        body, name=name, out_shape=jax.ShapeDtypeStruct((N_DEV * m_per, n), v.dtype),
        in_specs=[VMEM_WHOLE], out_specs=VMEM_WHOLE,
        scratch_shapes=[pltpu.SemaphoreType.DMA((7,)), pltpu.SemaphoreType.DMA((7,)), pltpu.SemaphoreType.DMA],
        compiler_params=pltpu.CompilerParams(vmem_limit_bytes=VMEM_LIMIT),
    )(v)


def _chip_id():
    return 2 * lax.axis_index("x") + lax.axis_index("y")


def _core_id():
    return lax.axis_index("c")


def _cast_into_block(wl, *, name, comm=None):
    r, c_ = wl.shape
    tr = _tile_rows(r, c_, 16)

    def body(w_ref, o_ref):
        o_ref[...] = w_ref[...].astype(BF16)

    in_spec = pl.BlockSpec((tr, c_), lambda i: (i, 0))
    out_spec = pl.BlockSpec((None, tr, c_), lambda i: (_chip_id(), i, 0))
    out_shape = jax.ShapeDtypeStruct((N_CHIPS, r, c_), BF16)
    if comm is not None:
        (out,), landed = _comm_call(body, name=name, grid=(r // tr,), in_specs=[in_spec], out_specs=[out_spec],
                                    out_shape=[out_shape], scratch_shapes=[], args=(wl,), comm=comm)
        return out, landed
    return pl.pallas_call(body, name=name, grid=(r // tr,), in_specs=[in_spec], out_specs=out_spec, out_shape=out_shape,
                          compiler_params=_params(("arbitrary",)))(wl)


def _gather_weight(buf, *, name, part=(0, 1)):
    return _run_comm(_gather_comm(buf, part), name=name)[0]


def _gather_comm(buf, part=(0, 1)):
    _, r, c_ = buf.shape
    half = r // 2
    n_all = _n_chunks(half)
    rows = half // n_all
    first_chunk = round(part[0] * n_all)
    nch = round(part[1] * n_all) - first_chunk

    def plan(out_ref):
        x, y, c = _place()
        me, sibling = (x, y, c), (x, y, 1 - c)
        chips = [(1 - x, y), (x, 1 - y), (1 - x, 1 - y)]
        return me, sibling, chips, c

    def copy(out_ref, sems, k, i, block, to):
        part = out_ref.at[2 * block[0] + block[1], pl.ds(block[2] * half + (first_chunk + i) * rows, rows), :]
        return pltpu.make_async_remote_copy(src_ref=part, dst_ref=part, send_sem=sems[0].at[k * nch + i],
                                            recv_sem=sems[1].at[k * nch + i], device_id=to, device_id_type=MESH)

    def start(cins, couts, sems):
        (out_ref,) = couts
        me, sibling, chips, c = plan(out_ref)
        for i in range(nch):
            for j, chip in enumerate(chips):
                copy(out_ref, sems, j, i, me, (*chip, c)).start()

    def finish(cins, couts, sems):
        (out_ref,) = couts
        me, sibling, chips, c = plan(out_ref)
        passed = []
        for i in range(nch):
            for j, chip in enumerate(chips):
                copy(out_ref, sems, j, i, (*chip, c), me).wait_recv()
                passed.append(copy(out_ref, sems, 3 + j, i, (*chip, c), sibling))
                passed[-1].start()
        for i in range(nch):
            for j, chip in enumerate(chips):
                copy(out_ref, sems, 3 + j, i, (*chip, 1 - c), me).wait_recv()
        for i in range(nch):
            for j, chip in enumerate(chips):
                copy(out_ref, sems, j, i, me, (*chip, c)).wait_send()
        for cp in passed:
            cp.wait_send()

    return dict(inputs=[buf], out_shape=[jax.ShapeDtypeStruct(buf.shape, buf.dtype)], aliases={0: 0},
                scratch=[pltpu.SemaphoreType.DMA((6 * nch,)), pltpu.SemaphoreType.DMA((6 * nch,))],
                start=start, finish=finish)


def _merge_comms(*comms):
    inputs, outs, aliases, scratch, spans = [], [], {}, [], []
    for cm in comms:
        i0, o0, s0 = len(inputs), len(outs), len(scratch)
        inputs += cm["inputs"]
        outs += cm["out_shape"]
        scratch += cm["scratch"]
        aliases.update({i0 + k: o0 + v for k, v in cm["aliases"].items()})
        spans.append((i0, len(inputs), o0, len(outs), s0, len(scratch)))

    def run(which):
        def f(cins, couts, sems):
            for cm, (i0, i1, o0, o1, s0, s1) in zip(comms, spans):
                cm[which](cins[i0:i1], couts[o0:o1], sems[s0:s1])
        return f

    return dict(inputs=inputs, out_shape=outs, aliases=aliases, scratch=scratch, start=run("start"), finish=run("finish"))


def _pair_comm(gfull):
    nblk, r, c_ = gfull.shape
    half = r // 2
    nch = _n_chunks(half)
    rows = half // nch

    def copies(cins, couts, sems):
        g_ref, peer_ref = cins[0], couts[0]
        x, y, c = _place()
        return [pltpu.make_async_remote_copy(
            src_ref=g_ref.at[k, pl.ds((1 - c) * half + i * rows, rows), :], dst_ref=peer_ref.at[k, pl.ds(i * rows, rows), :],
            send_sem=sems[0].at[k * nch + i], recv_sem=sems[1].at[k * nch + i],
            device_id=(x, y, 1 - c), device_id_type=MESH) for i in range(nch) for k in range(nblk)]

    def start(cins, couts, sems):
        for cp in copies(cins, couts, sems):
            cp.start()

    def finish(cins, couts, sems):
        cps = copies(cins, couts, sems)
        for cp in cps:
            cp.wait_recv()
        for cp in cps:
            cp.wait_send()

    return dict(inputs=[gfull], out_shape=[jax.ShapeDtypeStruct((nblk, half, c_), gfull.dtype)], aliases={},
                scratch=[pltpu.SemaphoreType.DMA((nblk * nch,)), pltpu.SemaphoreType.DMA((nblk * nch,))],
                start=start, finish=finish)


def _swap_comm(buf):
    r, c_ = buf.shape
    half = r // 2
    nch = _n_chunks(half)
    rows = half // nch

    def copy(out_ref, sems, i, pc):
        part = out_ref.at[pl.ds(pc * half + i * rows, rows), :]
        x, y, c = _place()
        return pltpu.make_async_remote_copy(src_ref=part, dst_ref=part, send_sem=sems[0].at[i], recv_sem=sems[1].at[i],
                                            device_id=(x, y, 1 - c), device_id_type=MESH)

    def start(cins, couts, sems):
        c = _core_id()
        for i in range(nch):
            copy(couts[0], sems, i, c).start()

    def finish(cins, couts, sems):
        c = _core_id()
        for i in range(nch):
            copy(couts[0], sems, i, 1 - c).wait_recv()
        for i in range(nch):
            copy(couts[0], sems, i, c).wait_send()

    return dict(inputs=[buf], out_shape=[jax.ShapeDtypeStruct(buf.shape, buf.dtype)], aliases={0: 0},
                scratch=[pltpu.SemaphoreType.DMA((nch,)), pltpu.SemaphoreType.DMA((nch,))], start=start, finish=finish)


def _run_comm(cm, *, name):
    nci, nco = len(cm["inputs"]), len(cm["out_shape"])

    def body(*refs):
        cins, couts, sems = refs[:nci], refs[nci:nci + nco], refs[nci + nco:]
        cm["start"](cins, couts, sems)
        cm["finish"](cins, couts, sems)

    return pl.pallas_call(
        body, name=name, out_shape=list(cm["out_shape"]), in_specs=[ANY] * nci, out_specs=[ANY] * nco,
        input_output_aliases=dict(cm["aliases"]), scratch_shapes=cm["scratch"],
    )(*cm["inputs"])


def _pair_add(gfull, peer, *, name):
    nblk, r, c_ = gfull.shape
    half = r // 2
    tr = _pick(half, max(16, (1 << 20) // (2 * c_)), 16)
    per = half // tr

    def body(g_ref, p_ref, o_ref):
        o_ref[...] = (g_ref[...].astype(F32) + p_ref[...].astype(F32)).astype(BF16)

    return pl.pallas_call(
        body, name=name, grid=(nblk, per),
        in_specs=[pl.BlockSpec((None, tr, c_), lambda k, i: (k, _core_id() * per + i, 0)),
                  pl.BlockSpec((None, tr, c_), lambda k, i: (k, i, 0))],
        out_specs=pl.BlockSpec((None, tr, c_), lambda k, i: (k, i, 0)),
        out_shape=jax.ShapeDtypeStruct((nblk, half, c_), BF16),
        compiler_params=_params(("arbitrary", "arbitrary")))(gfull, peer)


def _scatter_comm(pre, part=(0, 1), recv=None):
    _, half, c_ = pre.shape
    n_all = _n_chunks(half)
    rows = half // n_all
    first_chunk = round(part[0] * n_all)
    nch = round(part[1] * n_all) - first_chunk

    def copies(cins, couts, sems):
        p_ref, r_ref = cins[0], couts[0]
        x, y, c = _place()
        out = []
        for i in range(nch):
            at = pl.ds((first_chunk + i) * rows, rows)
            for j, (tx, ty) in reversed(list(enumerate([(1 - x, y), (x, 1 - y), (1 - x, 1 - y)]))):
                out.append(pltpu.make_async_remote_copy(
                    src_ref=p_ref.at[2 * tx + ty, at, :], dst_ref=r_ref.at[j, at, :],
                    send_sem=sems[0].at[j * nch + i], recv_sem=sems[1].at[j * nch + i],
                    device_id=(tx, ty, c), device_id_type=MESH))
        return out

    def start(cins, couts, sems):
        for cp in copies(cins, couts, sems):
            cp.start()

    def finish(cins, couts, sems):
        cps = copies(cins, couts, sems)
        for cp in cps:
            cp.wait_recv()
        for cp in cps:
            cp.wait_send()

    return dict(inputs=[pre] if recv is None else [pre, recv], out_shape=[jax.ShapeDtypeStruct((3, half, c_), pre.dtype)],
                aliases={} if recv is None else {1: 0},
                scratch=[pltpu.SemaphoreType.DMA((3 * nch,)), pltpu.SemaphoreType.DMA((3 * nch,))],
                start=start, finish=finish)


def _sum_into_half(pre, recv, *, name, comm=None):
    _, half, c_ = pre.shape
    n = recv.shape[0]
    tr = _pick(half, max(16, (1 << 19) // (2 * c_)), 16)
    per = half // tr

    def body(g_ref, r_ref, o_ref):
        acc = g_ref[...].astype(F32)
        for k in range(n):
            acc = acc + r_ref[k].astype(F32)
        o_ref[...] = acc

    in_specs = [pl.BlockSpec((None, tr, c_), lambda i: (_chip_id(), i, 0)), pl.BlockSpec((n, tr, c_), lambda i: (0, i, 0))]
    out_spec = pl.BlockSpec((tr, c_), lambda i: (_core_id() * per + i, 0))
    out_shape = jax.ShapeDtypeStruct((2 * half, c_), F32)
    if comm is not None:
        (out,), landed = _comm_call(body, name=name, grid=(per,), in_specs=in_specs, out_specs=[out_spec],
                                    out_shape=[out_shape], scratch_shapes=[], args=(pre, recv), comm=comm)
        return out, landed
    return pl.pallas_call(body, name=name, grid=(per,), in_specs=in_specs, out_specs=out_spec, out_shape=out_shape,
                          compiler_params=_params(("arbitrary",)))(pre, recv)


def _sum_slots(recv, *, name):
    n, r, c_ = recv.shape
    tr = _pick(r, max(16, (1 << 19) // (2 * c_)), 16)

    def body(r_ref, o_ref):
        acc = r_ref[0].astype(F32)
        for k in range(1, n):
            acc = acc + r_ref[k].astype(F32)
        o_ref[...] = acc

    return _rowcall(body, name, r, tr, [pl.BlockSpec((n, tr, c_), lambda i: (0, i, 0))], _row(tr, c_),
                    jax.ShapeDtypeStruct((r, c_), F32), (recv,))


def _prereduce(gfull, *, name):
    return _pair_add(gfull, _run_comm(_pair_comm(gfull), name=name + "_pair")[0], name=name + "_padd")


PACK_ROWS = 16


def _pack(parts):
    flat = [p.reshape(-1).astype(F32) for p in parts]
    n = sum(f.shape[0] for f in flat)
    unit = PACK_ROWS * LANES
    total = -(-n // unit) * unit
    if total > n:
        flat.append(jnp.zeros((total - n,), F32))
    where, off = [], 0
    for p in parts:
        where.append((off, p.shape))
        off += p.size
    return jnp.concatenate(flat).reshape(total // LANES, LANES), where


def _unpack(flat, where):
    v = flat.reshape(-1)
    return [v[off:off + _size(shape)].reshape(shape) for off, shape in where]


def _size(shape):
    n = 1
    for d in shape:
        n *= d
    return n


def _sample_step(x, target, mods, h1, w_in_p, bufs, sp):
    s, d = x.shape
    u = d // 2
    h = u // HEAD_DIM
    g = h // HEADS_PER_GROUP
    pw = w_in_p.shape[1]
    din = 6 * u + 2 * h
    dff_ = bufs["w_up"].shape[2] * N_CHIPS
    shift1, scale1, gate1, shift2, scale2, gate2 = mods

    a_f = -jnp.exp(sp["ssm_a_log_f"].reshape(-1))
    a_b = -jnp.exp(sp["ssm_a_log_b"].reshape(-1))
    pad_l = LANES - 2 * h
    a_row = jnp.pad(jnp.concatenate([a_f, a_b]), (0, pad_l)).reshape(1, LANES)
    bias_row = jnp.pad(jnp.concatenate([sp["ssm_dt_bias_f"].reshape(-1), sp["ssm_dt_bias_b"].reshape(-1)]),
                       (0, pad_l)).reshape(1, LANES)
    a_col = jnp.broadcast_to(jnp.stack([a_f, a_b])[:, :, None], (2, h, LANES))
    d_e = jnp.repeat(sp["ssm_d"].reshape(-1), HEAD_DIM).reshape(1, u)
    conv_w, conv_b = sp["ssm_conv_w"], sp["ssm_conv_b"].reshape(1, 2 * u)
    sc_conv_w = sp["sc_conv_w"]
    ssm_norm_w, sc_norm_w = sp["ssm_norm_w"].reshape(1, u), sp["sc_norm_w"].reshape(1, u)
    ln1_g, ln1_b = sp["ln1_g"].reshape(1, d), sp["ln1_b"].reshape(1, d)
    ln2_g, ln2_b = sp["ln2_g"].reshape(1, d), sp["ln2_b"].reshape(1, d)

    e = 1.0 / DMA_CHUNKS
    proj, (w_up_b, w_out_blk) = _matmul(
        h1, w_in_p, name="mm_proj", tn=1280,
        comm=_merge_comms(_gather_comm(bufs["w_up"], (0, 3 * e)), _gather_comm(bufs["w_out"])))
    w_out = w_out_blk.reshape(d, d)
    xbc, (w_down_b,) = _conv_silu_fwd(proj, conv_w, conv_b, s, u, comm=_gather_comm(bufs["w_down"], (0, e)))
    (dt, cum, dt_e, cum_e), (w_up_b,) = _dt_prep(proj, bias_row, a_row, s, u, h, comm=_gather_comm(w_up_b, (3 * e, 4 * e)))
    cum_t = jnp.stack([cum[:, :h].T, cum[:, h:2 * h].T])
    dt_t = jnp.stack([dt[:, :h].T, dt[:, h:2 * h].T])
    (y2, states), (w_up_b,) = _ssd_fwd(xbc, dt_e, cum_e, cum_t, s, h, g, comm=_gather_comm(w_up_b, (4 * e, 6 * e)))
    y_ssm = _gate_norm_fwd(y2, xbc, proj, d_e, ssm_norm_w, s, u, g)
    y_sc = _shortconv_fwd(proj, sc_conv_w, sc_norm_w, s, u)
    ymix = jnp.concatenate([y_ssm, y_sc], axis=1)
    mix, (w_up_b,) = _matmul(ymix, w_out, name="mm_mix", comm=_gather_comm(w_up_b, (6 * e, 7 * e)))
    (x1, h2), (w_up_blk,) = _ln1_fwd(x, mix, gate1, ln1_g, ln1_b, scale2, shift2, comm=_gather_comm(w_up_b, (7 * e, 1)))
    (up, ff), (w_down_blk,) = _matmul(h2, w_up_blk, name="mm_up", b_blocks=N_CHIPS, epilogue="relu2",
                                      comm=_gather_comm(w_down_b, (e, 1)))
    w_down = w_down_blk.reshape(dff_, d)
    f = _matmul(ff, w_down, name="mm_down")
    df, dr2, loss, dg2, db2, dgate2 = _ln2_loss_bwd(x1, f, target, gate2, ln2_g, ln2_b)

    gw_down = _matmul(ff, df, name="mm_gw_down", ta=True, out_dtype=BF16).reshape(N_CHIPS, dff_ // N_CHIPS, d)
    du, (peer,) = _matmul(df, w_down, name="mm_dff", tb=True, out_dtype=BF16, epilogue="relu2_bwd", extra=up,
                          comm=_pair_comm(gw_down))
    pre_down = _pair_add(gw_down, peer, name="rs_w_down_padd")
    gw_up, (rv_down,) = _matmul(h2, du, name="mm_gw_up", ta=True, out_dtype=BF16, out_blocks=N_CHIPS,
                                comm=_scatter_comm(pre_down, (0, 0.5)))
    dh2, (rv_down, peer) = _matmul(du, w_up_blk, name="mm_dh2", tb=True, b_blocks=N_CHIPS,
                                   comm=_merge_comms(_scatter_comm(pre_down, (0.5, 1), recv=rv_down), _pair_comm(gw_up)))
    pre_up = _pair_add(gw_up, peer, name="rs_w_up_padd")
    (dmix, dxa, dscale2, dshift2, dg1, db1, dgate1), (rv_up,) = _ln1_bwd(
        dh2, dr2, x1, x, mix, scale2, gate1, ln1_g, comm=_scatter_comm(pre_up, (0, e)))
    gw_out, (rv_up,) = _matmul(ymix, dmix, name="mm_gw_out", ta=True, out_dtype=BF16,
                               comm=_scatter_comm(pre_up, (e, 2 * e), recv=rv_up))
    gw_out = gw_out.reshape(N_CHIPS, d // N_CHIPS, d)
    dymix, (rv_up, peer) = _matmul(dmix, w_out, name="mm_dymix", tb=True,
                                   comm=_merge_comms(_scatter_comm(pre_up, (2 * e, 4 * e), recv=rv_up), _pair_comm(gw_out)))
    pre_out = _pair_add(gw_out, peer, name="rs_w_out_padd")
    dys, dproj, dxs, dnw, dd_e = _gate_norm_bwd(dymix, y2, xbc, proj, d_e, ssm_norm_w, pw, s, u, g)
    (dx2, dbb, dcc, ddt_t, da), (rv_up,) = _ssd_bwd(xbc, dt_e, cum_e, cum_t, dt_t, a_col, dys, states, s, h, g,
                                                    comm=_scatter_comm(pre_up, (4 * e, 1), recv=rv_up))
    dxbc = jnp.concatenate([dx2[0] + dx2[1] + dxs, dbb[0] + dbb[1], dcc[0] + dcc[1]], axis=1)
    (dproj, dcw, dcb), (rv_out,) = _conv_silu_bwd(proj, dxbc, conv_w, conv_b, dproj, s, u, comm=_scatter_comm(pre_out))
    ddt = jnp.pad(jnp.concatenate([ddt_t[0].T, ddt_t[1].T], axis=1), ((0, 0), (0, pad_l)))
    dproj, dbias = _dt_bwd(ddt, proj, bias_row, dproj, s, u, h)
    dproj, dscw, dscnw = _shortconv_bwd(dymix, proj, sc_conv_w, sc_norm_w, dproj, s, u)
    gp = _matmul(h1, dproj, name="mm_gw_in", ta=True, out_dtype=BF16, tn=1280)
    pre_in = _from_p_layout(_prereduce(gp[None], name="rs_w_in")[0], u, h, N_CHIPS)
    (grad_x, dscale1, dshift1), (rv_in,) = _matmul(dproj, w_in_p, name="mm_dh1", tb=True, tk=2560, epilogue="dx",
                                                   extra=(dxa, x, scale1), comm=_scatter_comm(pre_in))
    big = {}
    half_down = _sum_into_half(pre_down, rv_down, name="rs_w_down_sum")
    half_up, (big["w_down"],) = _sum_into_half(pre_up, rv_up, name="rs_w_up_sum", comm=_swap_comm(half_down))
    half_out, (big["w_up"],) = _sum_into_half(pre_out, rv_out, name="rs_w_out_sum", comm=_swap_comm(half_up))
    half_in, (big["w_out"],) = _sum_into_half(pre_in, rv_in, name="rs_w_in_sum", comm=_swap_comm(half_out))
    big["w_in"] = _run_comm(_swap_comm(half_in), name="rs_w_in_swap")[0]

    small = {
        "dmod": jnp.concatenate([dshift1, dscale1, dgate1, dshift2, dscale2, dgate2], axis=1),
        "ssm_conv_b": dcb,
        "ssm_dt_bias_f": dbias[0, :h], "ssm_dt_bias_b": dbias[0, h:2 * h],
        "ssm_a_log_f": jnp.sum(da[0], axis=1) * a_f, "ssm_a_log_b": jnp.sum(da[1], axis=1) * a_b,
        "ssm_d": jnp.sum(dd_e.reshape(h, HEAD_DIM), axis=1),
        "ssm_norm_w": dnw, "sc_norm_w": dscnw,
        "ln1_g": dg1, "ln1_b": db1, "ln2_g": dg2, "ln2_b": db2,
        "ssm_conv_w": dcw[:SSM_CONV], "sc_conv_w": dscw[:SC_CONV],
    }
    return loss, grad_x, big, small


WEIGHTS = ['w_ada', 'b_ada', 'w_in', 'ssm_conv_w', 'ssm_conv_b', 'ssm_dt_bias_f', 'ssm_dt_bias_b', 'ssm_a_log_f',
           'ssm_a_log_b', 'ssm_d', 'ssm_norm_w', 'sc_conv_w', 'sc_norm_w', 'w_out', 'ln1_g', 'ln1_b', 'w_up', 'w_down',
           'ln2_g', 'ln2_b']
BIG = ('w_ada', 'w_in', 'w_out', 'w_up', 'w_down')
SMALL = tuple(n for n in WEIGHTS if n not in BIG)
SMALL_SHARDED = ('ssm_conv_w', 'sc_conv_w')


def _p_layout_width(u):
    return -(-(6 * u + LANES) // 512) * 512


def _p_segments(u, h):
    return [((0, u), P_Z * u), ((u, 3 * u), P_X * u), ((3 * u, 3 * u + 2 * h), 6 * u),
            ((3 * u + 2 * h, 6 * u + 2 * h), P_H * u)]


def _to_p_layout(blocks, u, h, pw):
    nblk, d, w = blocks.shape
    parts = []
    for (lo, hi), _ in sorted(_p_segments(u, h), key=lambda t: t[1]):
        for k in range(nblk):
            a, b = max(lo, k * w), min(hi, (k + 1) * w)
            if a < b:
                parts.append(blocks[k][:, a - k * w:b - k * w])
    parts.append(jnp.zeros((d, pw - nblk * w), blocks.dtype))
    return jnp.concatenate(parts, axis=1)


def _from_p_layout(gp, u, h, nblk):
    w = (6 * u + 2 * h) // nblk
    blocks = []
    for k in range(nblk):
        parts = []
        for (lo, hi), poff in _p_segments(u, h):
            a, b = max(lo, k * w), min(hi, (k + 1) * w)
            if a < b:
                parts.append(gp[:, poff + a - lo:poff + b - lo])
        blocks.append(jnp.concatenate(parts, axis=1))
    return jnp.stack(blocks)


def kernel(x, c, w_ada, b_ada, w_in, ssm_conv_w, ssm_conv_b, ssm_dt_bias_f, ssm_dt_bias_b, ssm_a_log_f, ssm_a_log_b, ssm_d, ssm_norm_w, sc_conv_w, sc_norm_w, w_out, ln1_g, ln1_b, w_up, w_down, ln2_g, ln2_b, loss_target, m_w_ada, m_b_ada, m_w_in, m_ssm_conv_w, m_ssm_conv_b, m_ssm_dt_bias_f, m_ssm_dt_bias_b, m_ssm_a_log_f, m_ssm_a_log_b, m_ssm_d, m_ssm_norm_w, m_sc_conv_w, m_sc_norm_w, m_w_out, m_ln1_g, m_ln1_b, m_w_up, m_w_down, m_ln2_g, m_ln2_b, v_w_ada, v_b_ada, v_w_in, v_ssm_conv_w, v_ssm_conv_b, v_ssm_dt_bias_f, v_ssm_dt_bias_b, v_ssm_a_log_f, v_ssm_a_log_b, v_ssm_d, v_ssm_norm_w, v_sc_conv_w, v_sc_norm_w, v_w_out, v_ln1_g, v_ln1_b, v_w_up, v_w_down, v_ln2_g, v_ln2_b):
    given = dict(locals())
    w = {n: given[n][0] for n in WEIGHTS}
    m = {n: given["m_" + n][0] for n in WEIGHTS}
    v = {n: given["v_" + n][0] for n in WEIGHTS}
    xs, tgt = x[0], loss_target[0]
    s, d = xs.shape
    u = d // 2
    h = u // HEAD_DIM
    nmod = N_MOD * d
    nmod_loc = nmod // N_CHIPS
    ax, ay, ac = lax.axis_index("x"), lax.axis_index("y"), lax.axis_index("c")
    chip = 2 * ax + ay
    me = 2 * chip + ac

    pay1, where1 = _pack([c[0], w["ssm_conv_w"], w["sc_conv_w"]])
    g1 = _allgather_small(pay1, name="ag_inputs").reshape(N_DEV, -1)
    per_dev = [_unpack(g1[k], where1) for k in range(N_DEV)]
    c_all = jnp.stack([p[0] for p in per_dev])
    ssm_conv_w_full = jnp.concatenate([per_dev[2 * k][1] for k in range(N_CHIPS)], axis=1)
    sc_conv_w_full = jnp.concatenate([per_dev[2 * k][2] for k in range(N_CHIPS)], axis=1)

    sc_all = _silu(c_all)
    sc16 = jnp.pad(sc_all, ((0, 16 - N_DEV), (0, 0)))
    b_loc = lax.dynamic_slice(w["b_ada"], (chip * nmod_loc,), (nmod_loc,))
    mod_loc = _matmul(sc16, w["w_ada"], name="mm_mod")[:N_DEV] + b_loc[None, :]
    pay2, where2 = _pack([mod_loc])
    g2 = _allgather_small(pay2, name="ag_mod").reshape(N_DEV, -1)
    mod_blocks = jnp.stack([_unpack(g2[2 * k], where2)[0] for k in range(N_CHIPS)])
    mod_mine = lax.dynamic_index_in_dim(mod_blocks, me, axis=1, keepdims=False).reshape(N_MOD, 1, d)
    mods = [mod_mine[k] for k in range(N_MOD)]

    din = w["w_in"].shape[1] * N_CHIPS

    e = 1.0 / DMA_CHUNKS
    g_in = _cast_into_block(w["w_in"], name="cast_w_in")
    bufs = {"w_out": _cast_into_block(w["w_out"], name="cast_w_out")}
    bufs["w_up"], (g_in,) = _cast_into_block(w["w_up"], name="cast_w_up", comm=_gather_comm(g_in, (0, e)))
    bufs["w_down"], (g_in,) = _cast_into_block(w["w_down"], name="cast_w_down", comm=_gather_comm(g_in, (e, 2 * e)))
    h1, (g_in,) = _modulate(xs, mods[1], mods[0], name="modulate1", comm=_gather_comm(g_in, (2 * e, 3 * e)))
    g_in = _gather_weight(g_in, name="gather_w_in", part=(3 * e, 1))
    w_in_p = _to_p_layout(g_in, u, h, _p_layout_width(u))

    sp = {n: w[n] for n in SMALL}
    sp["ssm_conv_w"], sp["sc_conv_w"] = ssm_conv_w_full, sc_conv_w_full
    loss_loc, grad_x, big, small = _sample_step(xs, tgt, mods, h1, w_in_p, bufs, sp)

    small_names = ["dmod"] + [n for n in SMALL if n != "b_ada"]
    pay3, where3 = _pack([loss_loc] + [small[n] for n in small_names])
    g3 = _allgather_small(pay3, name="ag_small_grads")
    tot = _unpack(_sum_slots(g3.reshape(N_DEV, -1, LANES), name="sum_small_grads"), where3)
    loss = tot[0].reshape(())
    gsum = dict(zip(small_names, tot[1:]))
    dmod_all = jnp.stack([_unpack(g3.reshape(N_DEV, -1)[k], where3)[1].reshape(-1) for k in range(N_DEV)])

    grads = {}
    grads["b_ada"] = gsum["dmod"].reshape(-1)
    for n in SMALL:
        if n in SMALL_SHARDED:
            loc = w[n].shape[1]
            grads[n] = lax.dynamic_slice_in_dim(gsum[n], chip * loc, loc, axis=1)
        elif n != "b_ada":
            grads[n] = gsum[n].reshape(w[n].shape)

    delta, new_m, new_v = {}, {}, {}
    dm_loc = lax.dynamic_slice_in_dim(dmod_all, chip * nmod_loc, nmod_loc, axis=1)
    grads["w_ada"], delta["w_ada"], new_m["w_ada"], new_v["w_ada"] = _adam_outer(
        w["w_ada"], sc16.T, jnp.pad(dm_loc, ((0, 16 - N_DEV), (0, 0))), m["w_ada"], v["w_ada"], name="adam_w_ada")
    for n in ("w_in", "w_out", "w_up", "w_down"):
        grads[n], delta[n], new_m[n], new_v[n] = _adam(w[n], big[n], m[n], v[n], name="adam_" + n, emit_grad=True)
    pw_, where_s = _pack([w[n] for n in SMALL])
    pg_, _ = _pack([grads[n] for n in SMALL])
    pm_, _ = _pack([m[n] for n in SMALL])
    pv_, _ = _pack([v[n] for n in SMALL])
    sd, sm, sv = _adam(pw_, pg_, pm_, pv_, name="adam_small")
    for n, a, b_, c_ in zip(SMALL, _unpack(sd, where_s), _unpack(sm, where_s), _unpack(sv, where_s)):
        delta[n], new_m[n], new_v[n] = a, b_, c_

    def lead(t):
        return t[None]

    return (loss, grad_x[None], *[lead(grads[n].reshape(w[n].shape)) for n in WEIGHTS],
            *[lead(delta[n]) for n in WEIGHTS], *[lead(new_m[n]) for n in WEIGHTS], *[lead(new_v[n]) for n in WEIGHTS])
```

```python
import functools

import jax
import jax.numpy as jnp
from jax import lax
from jax.experimental import pallas as pl
from jax.experimental.pallas import tpu as pltpu

F32 = jnp.float32
BF16 = jnp.bfloat16

CHUNK = 128
HEAD_DIM = 64
STATE = 128
HEADS_PER_GROUP = 4
SC_GROUP_WIDTH = 128
SSM_CONV = 5
SC_CONV = 3
N_MOD = 6
DEEPNORM_ALPHA = 2.0 ** 0.25
LN_EPS = 1e-5
RMS_EPS = 1e-5
ADAM_LR = 0.001
ADAM_B1 = 0.9
ADAM_B2 = 0.999
ADAM_EPS = 1e-08
ADAM_WD = 0.01
ADAM_STEP = 10
N_CHIPS = 4
N_DEV = 8
LANES = 128
SUBLANES = 8
HALO = 8
VMEM_LIMIT = 56 * 1024 * 1024
MESH = pl.DeviceIdType.MESH
ANY = pl.BlockSpec(memory_space=pl.ANY)
VMEM_WHOLE = pl.BlockSpec(memory_space=pltpu.VMEM)


def _params(sem=None):
    return pltpu.CompilerParams(dimension_semantics=sem, vmem_limit_bytes=VMEM_LIMIT)


def _pick(n, target, mult=LANES):
    best = None
    t = mult
    while t <= min(n, target):
        if n % t == 0:
            best = t
        t += mult
    return best if best is not None else n


ROW_TILE_BYTES = 2 << 20


def _tile_rows(s, width, mult=SUBLANES):
    return _pick(s, max(mult, ROW_TILE_BYTES // (4 * width)), mult)


def _sigmoid(v):
    return 1.0 / (1.0 + jnp.exp(-v))


def _silu(v):
    return v * _sigmoid(v)


def _dsilu(v):
    s = _sigmoid(v)
    return s * (1.0 + v * (1.0 - s))


def _softplus(v):
    e = jnp.exp(-jnp.abs(v))
    return jnp.maximum(v, 0.0) + jnp.where(e < 1e-4, e - 0.5 * e * e, jnp.log(1.0 + e))


def _dot(a, b):
    return jnp.dot(a, b, preferred_element_type=F32)


def _dot_nt(a, b):
    return lax.dot_general(a, b, (((1,), (1,)), ((), ())), preferred_element_type=F32)


def _dot_tn(a, b):
    return lax.dot_general(a, b, (((0,), (0,)), ((), ())), preferred_element_type=F32)


def _split3(v):
    hi = v.astype(BF16)
    r1 = v - hi.astype(F32)
    mid = r1.astype(BF16)
    lo = (r1 - mid.astype(F32)).astype(BF16)
    return hi, mid, lo


def _dot3_r(v, onehot):
    hi, mid, lo = _split3(v)
    return _dot(hi, onehot) + _dot(mid, onehot) + _dot(lo, onehot)


def _dot3_l(onehot, v):
    hi, mid, lo = _split3(v)
    return _dot(onehot, hi) + _dot(onehot, mid) + _dot(onehot, lo)


MATMUL_VMEM_BUDGET = 44 * 1024 * 1024


def _matmul(a, b, *, name, ta=False, tb=False, out_dtype=F32, b_blocks=1, out_blocks=1,
            tm=1024, tn=1024, tk=4096, comm=None, epilogue=None, extra=None):
    if ta:
        K, M = a.shape
    else:
        M, K = a.shape
    if b_blocks > 1:
        nb, r_, c_ = b.shape
        if tb:
            N, K2 = r_, c_ * nb
        else:
            K2, N = r_, c_ * nb
    else:
        if tb:
            N, K2 = b.shape
        else:
            K2, N = b.shape
    assert K == K2, (a.shape, b.shape, ta, tb)
    assert not (ta and tb)
    n_unit = N // b_blocks if (b_blocks > 1 and not tb) else N
    n_unit = min(n_unit, N // out_blocks)
    tn = _pick(n_unit, tn)
    k_unit = K // b_blocks if (b_blocks > 1 and tb) else K
    tk = _pick(k_unit, tk)
    tile_bytes = {None: jnp.dtype(out_dtype).itemsize, "relu2": 6, "relu2_bwd": 6, "dx": 12}[epilogue]

    def vmem_need(tm_):
        need = 2 * (tm_ * tk * a.dtype.itemsize + tk * tn * b.dtype.itemsize) + 2 * tm_ * tn * tile_bytes
        return need + (tm_ * tn * 4 if K > tk else 0)

    tm = _pick(M, tm)
    while vmem_need(tm) > MATMUL_VMEM_BUDGET and tm % 2 == 0 and tm // 2 >= LANES:
        tm //= 2
    gm, gn, gk = M // tm, N // tn, K // tk

    if ta:
        a_spec = pl.BlockSpec((tk, tm), lambda i, j, k: (k, i))
    else:
        a_spec = pl.BlockSpec((tm, tk), lambda i, j, k: (i, k))
    if b_blocks > 1 and not tb:
        per = (N // b_blocks) // tn
        b_spec = pl.BlockSpec((None, tk, tn), lambda i, j, k: (j // per, k, j % per))
    elif b_blocks > 1 and tb:
        per = (K // b_blocks) // tk
        b_spec = pl.BlockSpec((None, tn, tk), lambda i, j, k: (k // per, j, k % per))
    elif tb:
        b_spec = pl.BlockSpec((tn, tk), lambda i, j, k: (j, k))
    else:
        b_spec = pl.BlockSpec((tk, tn), lambda i, j, k: (k, j))
    if out_blocks > 1:
        per_o = (N // out_blocks) // tn
        o_spec = pl.BlockSpec((None, tm, tn), lambda i, j, k: (j // per_o, i, j % per_o))
        o_shape = jax.ShapeDtypeStruct((out_blocks, M, N // out_blocks), out_dtype)
    else:
        o_spec = pl.BlockSpec((tm, tn), lambda i, j, k: (i, j))
        o_shape = jax.ShapeDtypeStruct((M, N), out_dtype)

    in_specs, args = [a_spec, b_spec], [a, b]
    out_specs, out_shape = [o_spec], [o_shape]
    if epilogue == "relu2":
        assert out_blocks == 1 and out_dtype == F32
        out_specs.append(o_spec)
        out_shape.append(jax.ShapeDtypeStruct((M, N), BF16))
    elif epilogue == "relu2_bwd":
        assert out_blocks == 1 and out_dtype == BF16
        in_specs.append(o_spec)
        args.append(extra)
    elif epilogue == "dx":
        assert out_blocks == 1 and out_dtype == F32
        col_spec = pl.BlockSpec((1, tn), lambda i, j, k: (0, j))
        in_specs += [o_spec, o_spec, col_spec]
        args += list(extra)
        out_specs += [col_spec, col_spec]
        out_shape += [jax.ShapeDtypeStruct((1, N), F32)] * 2
    n_in = len(in_specs)
    grid = (gm, gn, gk)
    row_axis = 0
    if epilogue == "dx":
        def swapped(spec):
            return pl.BlockSpec(spec.block_shape, lambda j, i, k, f=spec.index_map: f(i, j, k))
        in_specs, out_specs = [swapped(sp) for sp in in_specs], [swapped(sp) for sp in out_specs]
        grid, row_axis = (gn, gm, gk), 1

    def write(refs, p):
        o_ref = refs[n_in]
        if epilogue == "relu2":
            o_ref[...] = p
            r = jnp.maximum(p, 0.0)
            refs[n_in + 1][...] = (r * r).astype(BF16)
        elif epilogue == "relu2_bwd":
            o_ref[...] = (p * 2.0 * jnp.maximum(refs[2][...], 0.0)).astype(BF16)
        elif epilogue == "dx":
            o_ref[...] = refs[2][...] + p * (1.0 + refs[4][...])
            s1 = jnp.sum(p * refs[3][...], axis=0, keepdims=True)
            s2 = jnp.sum(p, axis=0, keepdims=True)
            first = pl.program_id(row_axis) == 0

            @pl.when(first)
            def _():
                refs[n_in + 1][...] = s1
                refs[n_in + 2][...] = s2

            @pl.when(jnp.logical_not(first))
            def _():
                refs[n_in + 1][...] += s1
                refs[n_in + 2][...] += s2
        else:
            o_ref[...] = p.astype(out_dtype)

    def body(*refs):
        av = refs[0][...].astype(BF16)
        bv = refs[1][...].astype(BF16)
        p = _dot_tn(av, bv) if ta else (_dot_nt(av, bv) if tb else _dot(av, bv))
        if gk == 1:
            write(refs, p)
            return
        acc = refs[-1]
        k = pl.program_id(2)

        @pl.when(k == 0)
        def _():
            acc[...] = p

        @pl.when(jnp.logical_and(k > 0, k < gk - 1))
        def _():
            acc[...] += p

        @pl.when(k == gk - 1)
        def _():
            write(refs, acc[...] + p)

    scratch = [pltpu.VMEM((tm, tn), F32)] if gk > 1 else []
    if comm is not None:
        outs, landed = _comm_call(body, name=name, grid=grid, in_specs=in_specs, out_specs=out_specs,
                                  out_shape=out_shape, scratch_shapes=scratch, args=args, comm=comm)
        return (outs[0] if len(outs) == 1 else tuple(outs)), landed
    outs = pl.pallas_call(
        body, name=name, grid=grid, in_specs=in_specs, out_specs=out_specs,
        out_shape=out_shape, scratch_shapes=scratch,
        compiler_params=_params(("parallel", "arbitrary" if epilogue == "dx" else "parallel", "arbitrary")),
    )(*args)
    return outs[0] if len(outs) == 1 else tuple(outs)


def _comm_call(body, *, name, grid, in_specs, out_specs, out_shape, scratch_shapes, args, comm, aliases=None):
    n_in, n_out, n_scr = len(in_specs), len(out_shape), len(scratch_shapes)
    c_in, c_out = list(comm["inputs"]), list(comm["out_shape"])
    nci, nco = len(c_in), len(c_out)

    def body2(*refs):
        ins, cins = refs[:n_in], refs[n_in:n_in + nci]
        o0 = n_in + nci
        outs, couts = refs[o0:o0 + n_out], refs[o0 + n_out:o0 + n_out + nco]
        s0 = o0 + n_out + nco
        scr, cscr = refs[s0:s0 + n_scr], refs[s0 + n_scr:]
        first = functools.reduce(jnp.logical_and, [pl.program_id(a) == 0 for a in range(len(grid))])
        last = functools.reduce(jnp.logical_and, [pl.program_id(a) == grid[a] - 1 for a in range(len(grid))])

        @pl.when(first)
        def _():
            comm["start"](cins, couts, cscr)

        body(*ins, *outs, *scr)

        @pl.when(last)
        def _():
            comm["finish"](cins, couts, cscr)

    res = pl.pallas_call(
        body2, name=name, grid=grid, in_specs=list(in_specs) + [ANY] * nci, out_specs=list(out_specs) + [ANY] * nco,
        out_shape=list(out_shape) + c_out, scratch_shapes=list(scratch_shapes) + list(comm["scratch"]),
        input_output_aliases={**(aliases or {}), **{n_in + k: n_out + v for k, v in comm.get("aliases", {}).items()}},
        compiler_params=_params(("arbitrary",) * len(grid)),
    )(*args, *c_in)
    return res[:n_out], res[n_out:]


def _row(tr, w, blk=0):
    return pl.BlockSpec((tr, w), lambda i: (i, blk))


def _full(shape):
    nd = len(shape)
    return pl.BlockSpec(shape, lambda i: (0,) * nd)


def _halo_specs(s, tr, w, blk=0):
    per = tr // HALO
    last = s // HALO - 1
    return [
        pl.BlockSpec((HALO, w), lambda i: (jnp.maximum(i * per - 1, 0), blk)),
        pl.BlockSpec((tr, w), lambda i: (i, blk)),
        pl.BlockSpec((HALO, w), lambda i: (jnp.minimum((i + 1) * per, last), blk)),
    ]


def _ext(prev_ref, cur_ref, next_ref, s, tr):
    i = pl.program_id(0)
    e = jnp.concatenate([prev_ref[...].astype(F32), cur_ref[...].astype(F32), next_ref[...].astype(F32)], axis=0)
    rid = i * tr - HALO + lax.broadcasted_iota(jnp.int32, e.shape, 0)
    return jnp.where((rid >= 0) & (rid < s), e, 0.0)


def _valid_rows(shape, s, tr):
    i = pl.program_id(0)
    rid = i * tr - HALO + lax.broadcasted_iota(jnp.int32, shape, 0)
    return (rid >= 0) & (rid < s)


def _shift(e, k):
    if k == 0:
        return e
    n = e.shape[0]
    return pltpu.roll(e, (n - k) % n, 0)


def _acc_rows(ref, v):
    s = jnp.sum(v, axis=0, keepdims=True)

    @pl.when(pl.program_id(0) == 0)
    def _():
        ref[...] = s

    @pl.when(pl.program_id(0) > 0)
    def _():
        ref[...] += s


def _rowcall(body, name, s, tr, in_specs, out_specs, out_shape, args, comm=None, aliases=None):
    aliases = aliases or {}
    if comm is not None:
        single = not isinstance(out_shape, (list, tuple))
        outs, landed = _comm_call(body, name=name, grid=(s // tr,), in_specs=in_specs,
                                  out_specs=[out_specs] if single else out_specs,
                                  out_shape=[out_shape] if single else out_shape, scratch_shapes=[], args=args, comm=comm,
                                  aliases=aliases)
        return (outs[0] if single else outs), landed
    return pl.pallas_call(
        body, name=name, grid=(s // tr,), in_specs=in_specs, out_specs=out_specs, out_shape=out_shape,
        input_output_aliases=aliases, compiler_params=_params(("arbitrary",)),
    )(*args)


def _modulate(x, scale, shift, *, name, comm=None):
    s, d = x.shape
    tr = _tile_rows(s, d)

    def body(x_ref, sc_ref, sh_ref, o_ref):
        o_ref[...] = (x_ref[...] * (1.0 + sc_ref[...]) + sh_ref[...]).astype(BF16)

    return _rowcall(body, name, s, tr, [_row(tr, d), _full((1, d)), _full((1, d))], _row(tr, d),
                    jax.ShapeDtypeStruct((s, d), BF16), (x, scale, shift), comm=comm)


def _ln_stats(r):
    mu = jnp.mean(r, axis=-1, keepdims=True)
    xc = r - mu
    var = jnp.mean(xc * xc, axis=-1, keepdims=True)
    rstd = lax.rsqrt(var + LN_EPS)
    return xc * rstd, rstd


def _ln1_fwd(x, mix, gate, g, b, scale2, shift2, comm=None):
    s, d = x.shape
    tr = _tile_rows(s, d)

    def body(x_ref, m_ref, gt_ref, g_ref, b_ref, sc_ref, sh_ref, x1_ref, h2_ref):
        r = DEEPNORM_ALPHA * x_ref[...] + (1.0 + gt_ref[...]) * m_ref[...]
        xh, _ = _ln_stats(r)
        x1 = xh * g_ref[...] + b_ref[...]
        x1_ref[...] = x1
        h2_ref[...] = (x1 * (1.0 + sc_ref[...]) + sh_ref[...]).astype(BF16)

    v = _full((1, d))
    return _rowcall(body, "ln1_fwd", s, tr, [_row(tr, d), _row(tr, d), v, v, v, v, v],
                    [_row(tr, d), _row(tr, d)],
                    [jax.ShapeDtypeStruct((s, d), F32), jax.ShapeDtypeStruct((s, d), BF16)],
                    (x, mix, gate, g, b, scale2, shift2), comm=comm)


def _ln2_loss_bwd(x1, f, target, gate, g, b):
    s, d = x1.shape
    tr = _tile_rows(s, d)

    def body(x1_ref, f_ref, t_ref, gt_ref, g_ref, b_ref, df_ref, dr_ref, loss_ref, dg_ref, db_ref, dgt_ref):
        fv = f_ref[...]
        r = DEEPNORM_ALPHA * x1_ref[...] + (1.0 + gt_ref[...]) * fv
        xh, rstd = _ln_stats(r)
        y = xh * g_ref[...] + b_ref[...]
        err = y - t_ref[...]
        _acc_rows(loss_ref, 0.5 * jnp.mean(err * err, axis=-1, keepdims=True))
        dy = err * (1.0 / d)
        _acc_rows(dg_ref, dy * xh)
        _acc_rows(db_ref, dy)
        dxh = dy * g_ref[...]
        dr = rstd * (dxh - jnp.mean(dxh, axis=-1, keepdims=True) - xh * jnp.mean(dxh * xh, axis=-1, keepdims=True))
        dr_ref[...] = dr
        df_ref[...] = ((1.0 + gt_ref[...]) * dr).astype(BF16)
        _acc_rows(dgt_ref, dr * fv)

    v = _full((1, d))
    one = _full((1, 1))
    return _rowcall(body, "ln2_loss_bwd", s, tr, [_row(tr, d), _row(tr, d), _row(tr, d), v, v, v],
                    [_row(tr, d), _row(tr, d), one, v, v, v],
                    [jax.ShapeDtypeStruct((s, d), BF16), jax.ShapeDtypeStruct((s, d), F32),
                     jax.ShapeDtypeStruct((1, 1), F32)] + [jax.ShapeDtypeStruct((1, d), F32)] * 3,
                    (x1, f, target, gate, g, b))


def _ln1_bwd(dh2, dr2, x1, x, mix, scale2, gate1, g1, comm=None):
    s, d = x.shape
    tr = _tile_rows(s, d)

    def body(dh_ref, dr2_ref, x1_ref, x_ref, m_ref, sc_ref, gt_ref, g_ref,
             dm_ref, dxa_ref, dsc_ref, dsh_ref, dg_ref, db_ref, dgt_ref):
        dh = dh_ref[...]
        _acc_rows(dsc_ref, dh * x1_ref[...])
        _acc_rows(dsh_ref, dh)
        dy = dh * (1.0 + sc_ref[...]) + DEEPNORM_ALPHA * dr2_ref[...]
        mv = m_ref[...]
        r = DEEPNORM_ALPHA * x_ref[...] + (1.0 + gt_ref[...]) * mv
        xh, rstd = _ln_stats(r)
        _acc_rows(dg_ref, dy * xh)
        _acc_rows(db_ref, dy)
        dxh = dy * g_ref[...]
        dr = rstd * (dxh - jnp.mean(dxh, axis=-1, keepdims=True) - xh * jnp.mean(dxh * xh, axis=-1, keepdims=True))
        dm_ref[...] = ((1.0 + gt_ref[...]) * dr).astype(BF16)
        dxa_ref[...] = DEEPNORM_ALPHA * dr
        _acc_rows(dgt_ref, dr * mv)

    v = _full((1, d))
    return _rowcall(body, "ln1_bwd", s, tr, [_row(tr, d)] * 5 + [v, v, v],
                    [_row(tr, d), _row(tr, d), v, v, v, v, v],
                    [jax.ShapeDtypeStruct((s, d), BF16), jax.ShapeDtypeStruct((s, d), F32)]
                    + [jax.ShapeDtypeStruct((1, d), F32)] * 5,
                    (dh2, dr2, x1, x, mix, scale2, gate1, g1), comm=comm)


P_X, P_BC, P_Z, P_H, P_B, P_C = range(6)


def _conv_silu_fwd(proj, conv_w, conv_b, s, u, comm=None):
    tr = _tile_rows(s, u)
    w = 2 * u
    half = SSM_CONV // 2

    def body(p0, c0, n0, p1, c1, n1, w_ref, b_ref, o_ref):
        for blk, (pr, cr, nr) in enumerate(((p0, c0, n0), (p1, c1, n1))):
            e = _ext(pr, cr, nr, s, tr)
            wv = w_ref[:, blk * u:(blk + 1) * u]
            acc = jnp.zeros_like(e)
            for k in range(SSM_CONV):
                acc = acc + _shift(e, k - half) * wv[k:k + 1, :]
            pre = acc[HALO:HALO + tr] + b_ref[:, blk * u:(blk + 1) * u]
            o_ref[:, blk * u:(blk + 1) * u] = _silu(pre)

    in_specs = _halo_specs(s, tr, u, P_X) + _halo_specs(s, tr, u, P_BC) + [_full((SSM_CONV, w)), _full((1, w))]
    return _rowcall(body, "conv_silu_fwd", s, tr, in_specs, _row(tr, w), jax.ShapeDtypeStruct((s, w), F32),
                    (proj,) * 6 + (conv_w, conv_b), comm=comm)


def _conv_silu_bwd(proj, dxbc, conv_w, conv_b, dproj, s, u, comm=None):
    tr = _tile_rows(s, u)
    w = 2 * u
    half = SSM_CONV // 2

    def body(p0, c0, n0, p1, c1, n1, dp0, dc0, dn0, dp1, dc1, dn1, w_ref, b_ref, dproj_in, du_ref, dw_ref, db_ref):
        for blk, (ur, dr) in enumerate((((p0, c0, n0), (dp0, dc0, dn0)), ((p1, c1, n1), (dp1, dc1, dn1)))):
            e = _ext(*ur, s, tr)
            de = _ext(*dr, s, tr)
            wv = w_ref[:, blk * u:(blk + 1) * u]
            acc = jnp.zeros_like(e)
            for k in range(SSM_CONV):
                acc = acc + _shift(e, k - half) * wv[k:k + 1, :]
            pre = acc + b_ref[:, blk * u:(blk + 1) * u]
            dpre = jnp.where(_valid_rows(e.shape, s, tr), de * _dsilu(pre), 0.0)
            du = jnp.zeros_like(e)
            rows = []
            for k in range(SSM_CONV):
                du = du + _shift(dpre, half - k) * wv[k:k + 1, :]
                rows.append(jnp.sum((_shift(e, k - half) * dpre)[HALO:HALO + tr], axis=0, keepdims=True))
            du_ref[:, blk * u:(blk + 1) * u] = du[HALO:HALO + tr].astype(BF16)
            dwv = jnp.concatenate(rows + [jnp.zeros((SUBLANES - SSM_CONV, u), F32)], axis=0)
            dbv = jnp.sum(dpre[HALO:HALO + tr], axis=0, keepdims=True)
            first = pl.program_id(0) == 0

            @pl.when(first)
            def _():
                dw_ref[:, blk * u:(blk + 1) * u] = dwv
                db_ref[:, blk * u:(blk + 1) * u] = dbv

            @pl.when(jnp.logical_not(first))
            def _():
                dw_ref[:, blk * u:(blk + 1) * u] += dwv
                db_ref[:, blk * u:(blk + 1) * u] += dbv

    in_specs = (_halo_specs(s, tr, u, P_X) + _halo_specs(s, tr, u, P_BC) + _halo_specs(s, tr, u, 0)
                + _halo_specs(s, tr, u, 1) + [_full((SSM_CONV, w)), _full((1, w)), ANY])
    return _rowcall(body, "conv_silu_bwd", s, tr, in_specs,
                    [_row(tr, w), _full((SUBLANES, w)), _full((1, w))],
                    [jax.ShapeDtypeStruct(dproj.shape, BF16), jax.ShapeDtypeStruct((SUBLANES, w), F32),
                     jax.ShapeDtypeStruct((1, w), F32)],
                    (proj,) * 6 + (dxbc,) * 6 + (conv_w, conv_b, dproj), comm=comm, aliases={14: 0})


def _expanders(h):
    col64 = jnp.arange(2 * h * HEAD_DIM) // HEAD_DIM
    col128 = jnp.arange(2 * h * LANES) // LANES
    row = jnp.arange(LANES)[:, None]
    return (row == col64[None, :]).astype(BF16), (row == col128[None, :]).astype(BF16)


def _dt_prep(proj, bias_row, a_row, s, u, h, comm=None):
    q = CHUNK
    e64, e128 = _expanders(h)
    ds = h * HEAD_DIM
    dtblk = (6 * u) // LANES

    def body(raw_ref, b_ref, a_ref, e64_ref, e128_ref, dt_ref, cum_ref, dte_ref, cume_ref):
        lane = lax.broadcasted_iota(jnp.int32, (q, LANES), 1)
        dt = jnp.where(lane < 2 * h, _softplus(raw_ref[...] + b_ref[...]), 0.0)
        da = dt * a_ref[...]
        ii = lax.broadcasted_iota(jnp.int32, (q, q), 0)
        kk = lax.broadcasted_iota(jnp.int32, (q, q), 1)
        lower = (kk <= ii).astype(F32).astype(BF16)
        upper = (kk >= ii).astype(F32).astype(BF16)
        cum = jnp.where(lane < h, _dot3_l(lower, da), _dot3_l(upper, da))
        dt_ref[...] = dt
        cum_ref[...] = cum
        dte = _dot3_r(dt, e64_ref[...])
        cume = _dot3_r(cum, e128_ref[...])
        dte_ref[0] = dte[:, :ds]
        dte_ref[1] = dte[:, ds:]
        cume_ref[0] = cume[:, :h * LANES]
        cume_ref[1] = cume[:, h * LANES:]

    in_specs = [pl.BlockSpec((q, LANES), lambda i: (i, dtblk)), _full((1, LANES)), _full((1, LANES)),
                _full(e64.shape), _full(e128.shape)]
    out_specs = [_row(q, LANES), _row(q, LANES),
                 pl.BlockSpec((2, q, ds), lambda i: (0, i, 0)), pl.BlockSpec((2, q, h * LANES), lambda i: (0, i, 0))]
    out_shape = [jax.ShapeDtypeStruct((s, LANES), F32), jax.ShapeDtypeStruct((s, LANES), F32),
                 jax.ShapeDtypeStruct((2, s, ds), F32), jax.ShapeDtypeStruct((2, s, h * LANES), F32)]
    return _rowcall(body, "dt_prep", s, q, in_specs, out_specs, out_shape, (proj, bias_row, a_row, e64, e128), comm=comm)


def _ssd_specs(s, h, g):
    q = CHUNK
    nc = s // q
    ds = h * HEAD_DIM
    nb = g * STATE
    return q, nc, ds, nb


def _ssd_fwd(xbc, dt_e, cum_e, cum_t, s, h, g, comm=None):
    q, nc, ds, nb = _ssd_specs(s, h, g)
    npair = h // 2

    def cidx(d, i):
        return jnp.where(d == 0, i, nc - 1 - i)

    def body(x_ref, b_ref, c_ref, dt_ref, cum_ref, cumt_ref, y_ref, sp_ref, st):
        d = pl.program_id(0)
        i = pl.program_id(1)

        @pl.when(i == 0)
        def _():
            st[...] = jnp.zeros_like(st)

        rev = d == 1
        ii = lax.broadcasted_iota(jnp.int32, (q, q), 0)
        jj = lax.broadcasted_iota(jnp.int32, (q, q), 1)
        sgn = jnp.where(rev, -1, 1)
        mask = (jj - ii) * sgn <= 0
        left = lax.broadcasted_iota(jnp.int32, (q, LANES), 1) < HEAD_DIM

        def group(gi, carry):
            goff = pl.multiple_of(gi * STATE, STATE)
            cg = c_ref[:, pl.ds(goff, STATE)].astype(BF16)
            bg = b_ref[:, pl.ds(goff, STATE)].astype(BF16)
            gm = _dot_nt(cg, bg)
            for p in range(HEADS_PER_GROUP // 2):
                pr = gi * (HEADS_PER_GROUP // 2) + p
                off = pl.multiple_of(pr * LANES, LANES)
                xd = x_ref[:, pl.ds(off, LANES)] * dt_ref[:, pl.ds(off, LANES)]
                ms = []
                cols = []
                for hl in range(2):
                    hh = 2 * pr + hl
                    col = cum_ref[:, pl.ds(pl.multiple_of(hh * LANES, LANES), LANES)]
                    row = cumt_ref[pl.ds(hh, 1), :]
                    lm = jnp.where(mask, jnp.exp(jnp.minimum(col - row, 0.0)), 0.0)
                    ms.append((gm * lm).astype(BF16))
                    cols.append(col)
                y = _dot(ms[0], jnp.where(left, xd, 0.0).astype(BF16)) + _dot(ms[1], jnp.where(left, 0.0, xd).astype(BF16))
                ce = jnp.where(left, cols[0], cols[1])
                sprev = st[pr]
                sp_ref[pr] = sprev
                y = y + jnp.exp(ce) * _dot(cg, sprev.astype(BF16))
                y_ref[:, pl.ds(off, LANES)] = y
                tot = jnp.where(rev, ce[0:1, :], ce[q - 1:q, :])
                v = (xd * jnp.exp(tot - ce)).astype(BF16)
                st[pr] = jnp.exp(tot) * sprev + _dot_tn(bg, v)
            return carry

        lax.fori_loop(0, g, group, 0, unroll=4 if g % 4 == 0 else 2)

    in_specs = [
        pl.BlockSpec((q, ds), lambda d, i: (cidx(d, i), 0)),
        pl.BlockSpec((q, nb), lambda d, i: (cidx(d, i), ds // nb)),
        pl.BlockSpec((q, nb), lambda d, i: (cidx(d, i), ds // nb + 1)),
        pl.BlockSpec((None, q, ds), lambda d, i: (d, cidx(d, i), 0)),
        pl.BlockSpec((None, q, h * LANES), lambda d, i: (d, cidx(d, i), 0)),
        pl.BlockSpec((None, h, q), lambda d, i: (d, 0, cidx(d, i))),
    ]
    out_specs = [
        pl.BlockSpec((None, q, ds), lambda d, i: (d, cidx(d, i), 0)),
        pl.BlockSpec((None, None, npair, STATE, LANES), lambda d, i: (d, cidx(d, i), 0, 0, 0)),
    ]
    out_shape = [jax.ShapeDtypeStruct((2, s, ds), F32), jax.ShapeDtypeStruct((2, nc, npair, STATE, LANES), F32)]
    if comm is not None:
        return _comm_call(body, name="ssd_fwd", grid=(2, nc), in_specs=in_specs, out_specs=out_specs, out_shape=out_shape,
                          scratch_shapes=[pltpu.VMEM((npair, STATE, LANES), F32)],
                          args=(xbc, xbc, xbc, dt_e, cum_e, cum_t), comm=comm)
    return pl.pallas_call(
        body, name="ssd_fwd", grid=(2, nc), in_specs=in_specs, out_specs=out_specs, out_shape=out_shape,
        scratch_shapes=[pltpu.VMEM((npair, STATE, LANES), F32)],
        compiler_params=_params(("arbitrary", "arbitrary")),
    )(xbc, xbc, xbc, dt_e, cum_e, cum_t), ()


def _ssd_bwd(xbc, dt_e, cum_e, cum_t, dt_t, a_col, dy, sp, s, h, g, comm=None):
    q, nc, ds, nb = _ssd_specs(s, h, g)
    npair = h // 2

    def cidx(d, i):
        return jnp.where(d == 0, nc - 1 - i, i)

    def body(x_ref, b_ref, c_ref, dt_ref, cum_ref, cumt_ref, dtt_ref, a_ref, dy_ref, sp_ref,
             dx_ref, db_ref, dc_ref, ddt_ref, da_ref, dst, rowp):
        d = pl.program_id(0)
        i = pl.program_id(1)

        @pl.when(i == 0)
        def _():
            dst[...] = jnp.zeros_like(dst)
            da_ref[...] = jnp.zeros_like(da_ref)

        rev = d == 1
        ii = lax.broadcasted_iota(jnp.int32, (q, q), 0)
        jj = lax.broadcasted_iota(jnp.int32, (q, q), 1)
        sgn = jnp.where(rev, -1, 1)
        mask = (jj - ii) * sgn <= 0
        lane = lax.broadcasted_iota(jnp.int32, (q, LANES), 1)
        left = lane < HEAD_DIM
        ones_sum = jnp.ones((q + LANES, LANES), BF16)
        rowp[...] = jnp.zeros_like(rowp)

        def group(gi, carry):
            acc_dcum, acc_tot, acc_dxx = carry
            goff = pl.multiple_of(gi * STATE, STATE)
            cg = c_ref[:, pl.ds(goff, STATE)].astype(BF16)
            bg = b_ref[:, pl.ds(goff, STATE)].astype(BF16)
            gm = _dot_nt(cg, bg)
            dgm = jnp.zeros((q, q), F32)
            dcg = jnp.zeros((q, STATE), F32)
            dbg = jnp.zeros((q, STATE), F32)
            for p in range(HEADS_PER_GROUP // 2):
                pr = gi * (HEADS_PER_GROUP // 2) + p
                off = pl.multiple_of(pr * LANES, LANES)
                xv = x_ref[:, pl.ds(off, LANES)]
                dte = dt_ref[:, pl.ds(off, LANES)]
                xd = xv * dte
                xdb = xd.astype(BF16)
                dyv = dy_ref[:, pl.ds(off, LANES)]
                sprev = sp_ref[pr]
                sprevb = sprev.astype(BF16)
                dsn = dst[pr]
                dsnb = dsn.astype(BF16)
                cols = [cum_ref[:, pl.ds(pl.multiple_of((2 * pr + hl) * LANES, LANES), LANES)] for hl in range(2)]
                ce = jnp.where(left, cols[0], cols[1])
                tot = jnp.where(rev, ce[0:1, :], ce[q - 1:q, :])
                et = jnp.exp(tot)
                r = jnp.exp(tot - ce)
                e = jnp.exp(ce)
                yoff = e * _dot(cg, sprevb)
                dz = (e * dyv).astype(BF16)
                dcg = dcg + _dot_nt(dz, sprevb)
                dsprev = _dot_tn(cg, dz) + et * dsn
                f1 = dyv * yoff
                v = (xd * r).astype(BF16)
                dbg = dbg + _dot_nt(v, dsnb)
                dv = _dot(bg, dsnb)
                dxd = dv * r
                tt = dv * xd * r
                wt = dsn * sprev * et
                for hl in range(2):
                    hh = 2 * pr + hl
                    hm = left if hl == 0 else jnp.logical_not(left)
                    row = cumt_ref[pl.ds(hh, 1), :]
                    lm = jnp.where(mask, jnp.exp(jnp.minimum(cols[hl] - row, 0.0)), 0.0)
                    mf = gm * lm
                    dym = jnp.where(hm, dyv, 0.0).astype(BF16)
                    dm = _dot_nt(dym, xdb)
                    dxd = dxd + _dot_tn(mf.astype(BF16), dym)
                    dgm = dgm + dm * lm
                    em = dm * mf
                    rowp[pl.ds(hh, 1), :] = rowp[pl.ds(hh, 1), :] - jnp.sum(em, axis=0, keepdims=True)
                    terms = jnp.concatenate([em, jnp.where(hm, f1 - tt, 0.0)], axis=1)
                    hi = terms.astype(BF16)
                    lo = (terms - hi.astype(F32)).astype(BF16)
                    colq = _dot(hi, ones_sum) + _dot(lo, ones_sum)
                    acc_dcum = jnp.where(lane == hh, colq, acc_dcum)
                    totq = jnp.sum(jnp.sum(jnp.where(hm, tt + wt, 0.0), axis=1, keepdims=True), axis=0, keepdims=True)
                    acc_tot = jnp.where(lane == hh, totq, acc_tot)
                dxx = dxd * xv
                for hl in range(2):
                    hh = 2 * pr + hl
                    hm = left if hl == 0 else jnp.logical_not(left)
                    acc_dxx = jnp.where(lane == hh, jnp.sum(jnp.where(hm, dxx, 0.0), axis=1, keepdims=True), acc_dxx)
                dx_ref[:, pl.ds(off, LANES)] = dxd * dte
                dst[pr] = dsprev
            dgb = dgm.astype(BF16)
            dc_ref[:, pl.ds(goff, STATE)] = dcg + _dot(dgb, bg)
            db_ref[:, pl.ds(goff, STATE)] = dbg + _dot_tn(dgb, cg)
            return acc_dcum, acc_tot, acc_dxx

        zero = jnp.zeros((q, LANES), F32)
        per_iter = 2

        def some_groups(j, carry):
            for k in range(per_iter):
                carry = group(per_iter * j + k, carry)
            return carry

        acc_dcum, acc_tot, acc_dxx = lax.fori_loop(0, g // per_iter, some_groups, (zero, zero, zero))
        dcum_t = rowp[...] + jnp.transpose(acc_dcum)[:h]
        rmat = ((ii - jj) * sgn >= 0).astype(F32).astype(BF16)
        da_t = _dot3_r(dcum_t, rmat) + jnp.transpose(acc_tot)[:h]
        ddt_ref[...] = da_t * a_ref[...] + jnp.transpose(acc_dxx)[:h]
        da_ref[...] += da_t * dtt_ref[...]

    in_specs = [
        pl.BlockSpec((q, ds), lambda d, i: (cidx(d, i), 0)),
        pl.BlockSpec((q, nb), lambda d, i: (cidx(d, i), ds // nb)),
        pl.BlockSpec((q, nb), lambda d, i: (cidx(d, i), ds // nb + 1)),
        pl.BlockSpec((None, q, ds), lambda d, i: (d, cidx(d, i), 0)),
        pl.BlockSpec((None, q, h * LANES), lambda d, i: (d, cidx(d, i), 0)),
        pl.BlockSpec((None, h, q), lambda d, i: (d, 0, cidx(d, i))),
        pl.BlockSpec((None, h, q), lambda d, i: (d, 0, cidx(d, i))),
        pl.BlockSpec((None, h, LANES), lambda d, i: (d, 0, 0)),
        pl.BlockSpec((q, ds), lambda d, i: (cidx(d, i), 0)),
        pl.BlockSpec((None, None, npair, STATE, LANES), lambda d, i: (d, cidx(d, i), 0, 0, 0)),
    ]
    out_specs = [
        pl.BlockSpec((None, q, ds), lambda d, i: (d, cidx(d, i), 0)),
        pl.BlockSpec((None, q, nb), lambda d, i: (d, cidx(d, i), 0)),
        pl.BlockSpec((None, q, nb), lambda d, i: (d, cidx(d, i), 0)),
        pl.BlockSpec((None, h, q), lambda d, i: (d, 0, cidx(d, i))),
        pl.BlockSpec((None, h, LANES), lambda d, i: (d, 0, 0)),
    ]
    out_shape = [jax.ShapeDtypeStruct((2, s, ds), F32), jax.ShapeDtypeStruct((2, s, nb), F32),
                 jax.ShapeDtypeStruct((2, s, nb), F32), jax.ShapeDtypeStruct((2, h, s), F32),
                 jax.ShapeDtypeStruct((2, h, LANES), F32)]
    scratch = [pltpu.VMEM((npair, STATE, LANES), F32), pltpu.VMEM((h, q), F32)]
    args = (xbc, xbc, xbc, dt_e, cum_e, cum_t, dt_t, a_col, dy, sp)
    if comm is not None:
        return _comm_call(body, name="ssd_bwd", grid=(2, nc), in_specs=in_specs, out_specs=out_specs, out_shape=out_shape,
                          scratch_shapes=scratch, args=args, comm=comm)
    return pl.pallas_call(
        body, name="ssd_bwd", grid=(2, nc), in_specs=in_specs, out_specs=out_specs, out_shape=out_shape,
        scratch_shapes=scratch, compiler_params=_params(("arbitrary", "arbitrary")),
    )(*args), ()


def _dt_bwd(ddt, proj, bias_row, dproj, s, u, h):
    tr = _tile_rows(s, 4 * LANES)
    dtblk = (6 * u) // LANES
    tail = dproj.shape[1] - 6 * u
    assert (6 * u) % tail == 0

    def body(d_ref, raw_ref, b_ref, dproj_in, o_ref, db_ref):
        lane = lax.broadcasted_iota(jnp.int32, (tr, LANES), 1)
        v = jnp.where(lane < 2 * h, d_ref[...] * _sigmoid(raw_ref[...] + b_ref[...]), 0.0)
        o_ref[:, :LANES] = v.astype(BF16)
        o_ref[:, LANES:] = jnp.zeros((tr, tail - LANES), BF16)
        _acc_rows(db_ref, v)

    return _rowcall(body, "dt_bwd", s, tr, [_row(tr, LANES), _row(tr, LANES, dtblk), _full((1, LANES)), ANY],
                    [_row(tr, tail, (6 * u) // tail), _full((1, LANES))],
                    [jax.ShapeDtypeStruct(dproj.shape, BF16), jax.ShapeDtypeStruct((1, LANES), F32)],
                    (ddt, proj, bias_row, dproj), aliases={3: 0})


def _group_rms(v, gw):
    outs, facs = [], []
    for k in range(v.shape[1] // gw):
        blk = v[:, k * gw:(k + 1) * gw]
        f = lax.rsqrt(jnp.mean(blk * blk, axis=-1, keepdims=True) + RMS_EPS)
        outs.append(blk * f)
        facs.append(jnp.broadcast_to(f, blk.shape))
    return jnp.concatenate(outs, axis=1), jnp.concatenate(facs, axis=1)


def _group_rms_bwd(dn, n, fac, gw):
    outs = []
    for k in range(n.shape[1] // gw):
        sl = slice(k * gw, (k + 1) * gw)
        outs.append(fac[:, sl] * (dn[:, sl] - n[:, sl] * jnp.mean(dn[:, sl] * n[:, sl], axis=-1, keepdims=True)))
    return jnp.concatenate(outs, axis=1)


def _gate_norm_fwd(y2, xbc, proj, d_e, norm_w, s, u, g):
    tr = _tile_rows(s, u)
    gw = u // g

    def body(y_ref, x_ref, z_ref, d_ref, w_ref, o_ref):
        ys = y_ref[0] + y_ref[1] + d_ref[...] * x_ref[...]
        n, _ = _group_rms(ys * _silu(z_ref[...]), gw)
        o_ref[...] = (n * w_ref[...]).astype(BF16)

    return _rowcall(body, "gate_norm_fwd", s, tr,
                    [pl.BlockSpec((2, tr, u), lambda i: (0, i, 0)), _row(tr, u), _row(tr, u, P_Z), _full((1, u)), _full((1, u))],
                    _row(tr, u), jax.ShapeDtypeStruct((s, u), BF16), (y2, xbc, proj, d_e, norm_w))


def _gate_norm_bwd(dymix, y2, xbc, proj, d_e, norm_w, pw, s, u, g, comm=None):
    tr = _tile_rows(s, u)
    gw = u // g

    def body(dy_ref, y_ref, x_ref, z_ref, d_ref, w_ref, dys_ref, dz_ref, dxs_ref, dw_ref, dd_ref):
        xv = x_ref[...]
        zv = z_ref[...]
        ys = y_ref[0] + y_ref[1] + d_ref[...] * xv
        sz = _silu(zv)
        n, fac = _group_rms(ys * sz, gw)
        dout = dy_ref[...]
        _acc_rows(dw_ref, dout * n)
        dyg = _group_rms_bwd(dout * w_ref[...], n, fac, gw)
        dys = dyg * sz
        dys_ref[...] = dys
        dz_ref[...] = (dyg * ys * _dsilu(zv)).astype(BF16)
        dxs_ref[...] = dys * d_ref[...]
        _acc_rows(dd_ref, dys * xv)

    v = _full((1, u))
    return _rowcall(body, "gate_norm_bwd", s, tr,
                    [_row(tr, u), pl.BlockSpec((2, tr, u), lambda i: (0, i, 0)), _row(tr, u), _row(tr, u, P_Z), v, v],
                    [_row(tr, u), _row(tr, u, P_Z), _row(tr, u), v, v],
                    [jax.ShapeDtypeStruct((s, u), F32), jax.ShapeDtypeStruct((s, pw), BF16),
                     jax.ShapeDtypeStruct((s, u), F32), jax.ShapeDtypeStruct((1, u), F32), jax.ShapeDtypeStruct((1, u), F32)],
                    (dymix, y2, xbc, proj, d_e, norm_w), comm=comm)


def _shortconv_fwd(proj, conv_w, norm_w, s, u):
    tr = _tile_rows(s, u)
    half = SC_CONV // 2

    def body(hp, hc, hn, b_ref, cp, cc, cn, cw_ref, w_ref, o_ref):
        t = _ext(hp, hc, hn, s, tr) * _ext(cp, cc, cn, s, tr)
        wv = cw_ref[...]
        acc = jnp.zeros_like(t)
        for k in range(SC_CONV):
            acc = acc + _shift(t, k - half) * wv[k:k + 1, :]
        n, _ = _group_rms(b_ref[...] * acc[HALO:HALO + tr], SC_GROUP_WIDTH)
        o_ref[...] = (n * w_ref[...]).astype(BF16)

    in_specs = (_halo_specs(s, tr, u, P_H) + [_row(tr, u, P_B)] + _halo_specs(s, tr, u, P_C)
                + [_full((SC_CONV, u)), _full((1, u))])
    return _rowcall(body, "shortconv_fwd", s, tr, in_specs, _row(tr, u), jax.ShapeDtypeStruct((s, u), BF16),
                    (proj,) * 7 + (conv_w, norm_w))


def _shortconv_bwd(dymix, proj, conv_w, norm_w, dproj, s, u):
    tr = _tile_rows(s, u)
    half = SC_CONV // 2

    def body(dp, dc_, dn, hp, hc, hn, bp, bc, bn, cp, cc, cn, cw_ref, w_ref, dproj_in, o_ref, dcw_ref, dw_ref):
        dout = _ext(dp, dc_, dn, s, tr)
        hv = _ext(hp, hc, hn, s, tr)
        bv = _ext(bp, bc, bn, s, tr)
        cv = _ext(cp, cc, cn, s, tr)
        t = hv * cv
        wv = cw_ref[...]
        acc = jnp.zeros_like(t)
        for k in range(SC_CONV):
            acc = acc + _shift(t, k - half) * wv[k:k + 1, :]
        n, fac = _group_rms(bv * acc, SC_GROUP_WIDTH)
        cur = slice(HALO, HALO + tr)
        _acc_rows(dw_ref, (dout * n)[cur])
        dyv = _group_rms_bwd(dout * w_ref[...], n, fac, SC_GROUP_WIDTH)
        o_ref[:, u:2 * u] = (dyv * acc)[cur].astype(BF16)
        dv = dyv * bv
        dt = jnp.zeros_like(t)
        rows = []
        for k in range(SC_CONV):
            dt = dt + _shift(dv, half - k) * wv[k:k + 1, :]
            rows.append(jnp.sum((_shift(t, k - half) * dv)[cur], axis=0, keepdims=True))
        o_ref[:, :u] = (dt * cv)[cur].astype(BF16)
        o_ref[:, 2 * u:] = (dt * hv)[cur].astype(BF16)
        dwv = jnp.concatenate(rows + [jnp.zeros((SUBLANES - SC_CONV, u), F32)], axis=0)
        first = pl.program_id(0) == 0

        @pl.when(first)
        def _():
            dcw_ref[...] = dwv

        @pl.when(jnp.logical_not(first))
        def _():
            dcw_ref[...] += dwv

    in_specs = (_halo_specs(s, tr, u, 1) + _halo_specs(s, tr, u, P_H) + _halo_specs(s, tr, u, P_B)
                + _halo_specs(s, tr, u, P_C) + [_full((SC_CONV, u)), _full((1, u)), ANY])
    return _rowcall(body, "shortconv_bwd", s, tr, in_specs,
                    [_row(tr, 3 * u, 1), _full((SUBLANES, u)), _full((1, u))],
                    [jax.ShapeDtypeStruct(dproj.shape, BF16), jax.ShapeDtypeStruct((SUBLANES, u), F32),
                     jax.ShapeDtypeStruct((1, u), F32)],
                    (dymix,) * 3 + (proj,) * 9 + (conv_w, norm_w, dproj), aliases={14: 0})


def _adam_math(w, g, m, v):
    m2 = ADAM_B1 * m + (1.0 - ADAM_B1) * g
    v2 = ADAM_B2 * v + (1.0 - ADAM_B2) * (g * g)
    m_hat = m2 / (1.0 - ADAM_B1 ** ADAM_STEP)
    v_hat = v2 / (1.0 - ADAM_B2 ** ADAM_STEP)
    delta = -ADAM_LR * (m_hat / (jnp.sqrt(v_hat) + ADAM_EPS) + ADAM_WD * w)
    return delta, m2, v2


def _adam_rows(r, c):
    return _pick(r, max(SUBLANES, (1 << 20) // (4 * c)), SUBLANES)


def _adam(w, g, m, v, *, name, emit_grad=False):
    r, c = w.shape
    tr = _adam_rows(r, c)
    n_out = 4 if emit_grad else 3

    def body(w_ref, g_ref, m_ref, v_ref, *outs):
        gv = g_ref[...]
        if emit_grad:
            outs[0][...] = gv
        outs[-3][...], outs[-2][...], outs[-1][...] = _adam_math(w_ref[...], gv, m_ref[...], v_ref[...])

    return _rowcall(body, name, r, tr, [_row(tr, c)] * 4, [_row(tr, c)] * n_out,
                    [jax.ShapeDtypeStruct((r, c), F32)] * n_out, (w, g, m, v))


def _adam_outer(w, a_t, bmat, m, v, *, name):
    r, c = w.shape
    tr = _adam_rows(r, c)
    kk = a_t.shape[1]

    def body(w_ref, a_ref, b_ref, m_ref, v_ref, g_ref, d_ref, m2_ref, v2_ref):
        g = _dot(a_ref[...].astype(BF16), b_ref[...].astype(BF16))
        g_ref[...] = g
        d_ref[...], m2_ref[...], v2_ref[...] = _adam_math(w_ref[...], g, m_ref[...], v_ref[...])

    return _rowcall(body, name, r, tr, [_row(tr, c), _row(tr, kk), _full((kk, c)), _row(tr, c), _row(tr, c)],
                    [_row(tr, c)] * 4, [jax.ShapeDtypeStruct((r, c), F32)] * 4, (w, a_t, bmat, m, v))


def _place():
    x, y, c = lax.axis_index("x"), lax.axis_index("y"), lax.axis_index("c")
    return x, y, c


DMA_CHUNKS = 8


def _n_chunks(rows):
    n = DMA_CHUNKS
    while n > 1 and rows % (16 * n):
        n //= 2
    return n


def _allgather_small(v, *, name):
    m_per, n = v.shape

    def body(x_ref, out_ref, send_sems, recv_sems, local_sem):
        x, y, c = _place()
        me, sibling = (x, y, c), (x, y, 1 - c)
        chips = [(1 - x, y), (x, 1 - y), (1 - x, 1 - y)]

        def rows(px, py, pc):
            return out_ref.at[pl.ds((4 * px + 2 * py + pc) * m_per, m_per), :]

        def copy(k, block, to, src=None):
            return pltpu.make_async_remote_copy(
                src_ref=rows(*block) if src is None else src, dst_ref=rows(*block),
                send_sem=send_sems.at[k], recv_sem=recv_sems.at[k], device_id=to, device_id_type=MESH)

        mine = pltpu.make_async_copy(x_ref, rows(*me), local_sem)
        mine.start()
        first = [copy(0, me, sibling, src=x_ref)]
        first += [copy(1 + j, me, (*chip, c), src=x_ref) for j, chip in enumerate(chips)]
        for cp in first:
            cp.start()
        passed = [copy(4 + j, (*chip, c), sibling) for j, chip in enumerate(chips)]
        for j, chip in enumerate(chips):
            copy(1 + j, (*chip, c), me).wait_recv()
            passed[j].start()
        copy(0, sibling, me).wait_recv()
        for j, chip in enumerate(chips):
            copy(4 + j, (*chip, 1 - c), me).wait_recv()
        for cp in first + passed:
            cp.wait_send()
        mine.wait()

    return pl.pallas_call(
        body, name=name, out_shape=jax.ShapeDtypeStruct((N_DEV * m_per, n), v.dtype),
        in_specs=[VMEM_WHOLE], out_specs=VMEM_WHOLE,
        scratch_shapes=[pltpu.SemaphoreType.DMA((7,)), pltpu.SemaphoreType.DMA((7,)), pltpu.SemaphoreType.DMA],
        compiler_params=pltpu.CompilerParams(vmem_limit_bytes=VMEM_LIMIT),
    )(v)


def _chip_id():
    return 2 * lax.axis_index("x") + lax.axis_index("y")


def _core_id():
    return lax.axis_index("c")


def _cast_into_block(wl, *, name, comm=None):
    r, c_ = wl.shape
    tr = _tile_rows(r, c_, 16)

    def body(w_ref, o_ref):
        o_ref[...] = w_ref[...].astype(BF16)

    in_spec = pl.BlockSpec((tr, c_), lambda i: (i, 0))
    out_spec = pl.BlockSpec((None, tr, c_), lambda i: (_chip_id(), i, 0))
    out_shape = jax.ShapeDtypeStruct((N_CHIPS, r, c_), BF16)
    if comm is not None:
        (out,), landed = _comm_call(body, name=name, grid=(r // tr,), in_specs=[in_spec], out_specs=[out_spec],
                                    out_shape=[out_shape], scratch_shapes=[], args=(wl,), comm=comm)
        return out, landed
    return pl.pallas_call(body, name=name, grid=(r // tr,), in_specs=[in_spec], out_specs=out_spec, out_shape=out_shape,
                          compiler_params=_params(("arbitrary",)))(wl)


def _gather_weight(buf, *, name, part=(0, 1)):
    return _run_comm(_gather_comm(buf, part), name=name)[0]


def _gather_comm(buf, part=(0, 1)):
    _, r, c_ = buf.shape
    half = r // 2
    n_all = _n_chunks(half)
    rows = half // n_all
    first_chunk = round(part[0] * n_all)
    nch = round(part[1] * n_all) - first_chunk

    def plan(out_ref):
        x, y, c = _place()
        me, sibling = (x, y, c), (x, y, 1 - c)
        chips = [(1 - x, y), (x, 1 - y), (1 - x, 1 - y)]
        return me, sibling, chips, c

    def copy(out_ref, sems, k, i, block, to):
        part = out_ref.at[2 * block[0] + block[1], pl.ds(block[2] * half + (first_chunk + i) * rows, rows), :]
        return pltpu.make_async_remote_copy(src_ref=part, dst_ref=part, send_sem=sems[0].at[k * nch + i],
                                            recv_sem=sems[1].at[k * nch + i], device_id=to, device_id_type=MESH)

    def start(cins, couts, sems):
        (out_ref,) = couts
        me, sibling, chips, c = plan(out_ref)
        for i in range(nch):
            for j, chip in enumerate(chips):
                copy(out_ref, sems, j, i, me, (*chip, c)).start()

    def finish(cins, couts, sems):
        (out_ref,) = couts
        me, sibling, chips, c = plan(out_ref)
        passed = []
        for i in range(nch):
            for j, chip in enumerate(chips):
                copy(out_ref, sems, j, i, (*chip, c), me).wait_recv()
                passed.append(copy(out_ref, sems, 3 + j, i, (*chip, c), sibling))
                passed[-1].start()
        for i in range(nch):
            for j, chip in enumerate(chips):
                copy(out_ref, sems, 3 + j, i, (*chip, 1 - c), me).wait_recv()
        for i in range(nch):
            for j, chip in enumerate(chips):
                copy(out_ref, sems, j, i, me, (*chip, c)).wait_send()
        for cp in passed:
            cp.wait_send()

    return dict(inputs=[buf], out_shape=[jax.ShapeDtypeStruct(buf.shape, buf.dtype)], aliases={0: 0},
                scratch=[pltpu.SemaphoreType.DMA((6 * nch,)), pltpu.SemaphoreType.DMA((6 * nch,))],
                start=start, finish=finish)


def _merge_comms(*comms):
    inputs, outs, aliases, scratch, spans = [], [], {}, [], []
    for cm in comms:
        i0, o0, s0 = len(inputs), len(outs), len(scratch)
        inputs += cm["inputs"]
        outs += cm["out_shape"]
        scratch += cm["scratch"]
        aliases.update({i0 + k: o0 + v for k, v in cm["aliases"].items()})
        spans.append((i0, len(inputs), o0, len(outs), s0, len(scratch)))

    def run(which):
        def f(cins, couts, sems):
            for cm, (i0, i1, o0, o1, s0, s1) in zip(comms, spans):
                cm[which](cins[i0:i1], couts[o0:o1], sems[s0:s1])
        return f

    return dict(inputs=inputs, out_shape=outs, aliases=aliases, scratch=scratch, start=run("start"), finish=run("finish"))


def _pair_comm(gfull):
    nblk, r, c_ = gfull.shape
    half = r // 2
    nch = _n_chunks(half)
    rows = half // nch

    def copies(cins, couts, sems):
        g_ref, peer_ref = cins[0], couts[0]
        x, y, c = _place()
        return [pltpu.make_async_remote_copy(
            src_ref=g_ref.at[k, pl.ds((1 - c) * half + i * rows, rows), :], dst_ref=peer_ref.at[k, pl.ds(i * rows, rows), :],
            send_sem=sems[0].at[k * nch + i], recv_sem=sems[1].at[k * nch + i],
            device_id=(x, y, 1 - c), device_id_type=MESH) for i in range(nch) for k in range(nblk)]

    def start(cins, couts, sems):
        for cp in copies(cins, couts, sems):
            cp.start()

    def finish(cins, couts, sems):
        cps = copies(cins, couts, sems)
        for cp in cps:
            cp.wait_recv()
        for cp in cps:
            cp.wait_send()

    return dict(inputs=[gfull], out_shape=[jax.ShapeDtypeStruct((nblk, half, c_), gfull.dtype)], aliases={},
                scratch=[pltpu.SemaphoreType.DMA((nblk * nch,)), pltpu.SemaphoreType.DMA((nblk * nch,))],
                start=start, finish=finish)


def _swap_comm(buf):
    r, c_ = buf.shape
    half = r // 2
    nch = _n_chunks(half)
    rows = half // nch

    def copy(out_ref, sems, i, pc):
        part = out_ref.at[pl.ds(pc * half + i * rows, rows), :]
        x, y, c = _place()
        return pltpu.make_async_remote_copy(src_ref=part, dst_ref=part, send_sem=sems[0].at[i], recv_sem=sems[1].at[i],
                                            device_id=(x, y, 1 - c), device_id_type=MESH)

    def start(cins, couts, sems):
        c = _core_id()
        for i in range(nch):
            copy(couts[0], sems, i, c).start()

    def finish(cins, couts, sems):
        c = _core_id()
        for i in range(nch):
            copy(couts[0], sems, i, 1 - c).wait_recv()
        for i in range(nch):
            copy(couts[0], sems, i, c).wait_send()

    return dict(inputs=[buf], out_shape=[jax.ShapeDtypeStruct(buf.shape, buf.dtype)], aliases={0: 0},
                scratch=[pltpu.SemaphoreType.DMA((nch,)), pltpu.SemaphoreType.DMA((nch,))], start=start, finish=finish)


def _run_comm(cm, *, name):
    nci, nco = len(cm["inputs"]), len(cm["out_shape"])

    def body(*refs):
        cins, couts, sems = refs[:nci], refs[nci:nci + nco], refs[nci + nco:]
        cm["start"](cins, couts, sems)
        cm["finish"](cins, couts, sems)

    return pl.pallas_call(
        body, name=name, out_shape=list(cm["out_shape"]), in_specs=[ANY] * nci, out_specs=[ANY] * nco,
        input_output_aliases=dict(cm["aliases"]), scratch_shapes=cm["scratch"],
    )(*cm["inputs"])


def _pair_add(gfull, peer, *, name):
    nblk, r, c_ = gfull.shape
    half = r // 2
    tr = _pick(half, max(16, (1 << 20) // (2 * c_)), 16)
    per = half // tr

    def body(g_ref, p_ref, o_ref):
        o_ref[...] = (g_ref[...].astype(F32) + p_ref[...].astype(F32)).astype(BF16)

    return pl.pallas_call(
        body, name=name, grid=(nblk, per),
        in_specs=[pl.BlockSpec((None, tr, c_), lambda k, i: (k, _core_id() * per + i, 0)),
                  pl.BlockSpec((None, tr, c_), lambda k, i: (k, i, 0))],
        out_specs=pl.BlockSpec((None, tr, c_), lambda k, i: (k, i, 0)),
        out_shape=jax.ShapeDtypeStruct((nblk, half, c_), BF16),
        compiler_params=_params(("arbitrary", "arbitrary")))(gfull, peer)


def _scatter_comm(pre, part=(0, 1), recv=None):
    _, half, c_ = pre.shape
    n_all = _n_chunks(half)
    rows = half // n_all
    first_chunk = round(part[0] * n_all)
    nch = round(part[1] * n_all) - first_chunk

    def copies(cins, couts, sems):
        p_ref, r_ref = cins[0], couts[0]
        x, y, c = _place()
        out = []
        for i in range(nch):
            at = pl.ds((first_chunk + i) * rows, rows)
            for j, (tx, ty) in reversed(list(enumerate([(1 - x, y), (x, 1 - y), (1 - x, 1 - y)]))):
                out.append(pltpu.make_async_remote_copy(
                    src_ref=p_ref.at[2 * tx + ty, at, :], dst_ref=r_ref.at[j, at, :],
                    send_sem=sems[0].at[j * nch + i], recv_sem=sems[1].at[j * nch + i],
                    device_id=(tx, ty, c), device_id_type=MESH))
        return out

    def start(cins, couts, sems):
        for cp in copies(cins, couts, sems):
            cp.start()

    def finish(cins, couts, sems):
        cps = copies(cins, couts, sems)
        for cp in cps:
            cp.wait_recv()
        for cp in cps:
            cp.wait_send()

    return dict(inputs=[pre] if recv is None else [pre, recv], out_shape=[jax.ShapeDtypeStruct((3, half, c_), pre.dtype)],
                aliases={} if recv is None else {1: 0},
                scratch=[pltpu.SemaphoreType.DMA((3 * nch,)), pltpu.SemaphoreType.DMA((3 * nch,))],
                start=start, finish=finish)


def _sum_into_half(pre, recv, *, name, comm=None):
    _, half, c_ = pre.shape
    n = recv.shape[0]
    tr = _pick(half, max(16, (1 << 19) // (2 * c_)), 16)
    per = half // tr

    def body(g_ref, r_ref, o_ref):
        acc = g_ref[...].astype(F32)
        for k in range(n):
            acc = acc + r_ref[k].astype(F32)
        o_ref[...] = acc

    in_specs = [pl.BlockSpec((None, tr, c_), lambda i: (_chip_id(), i, 0)), pl.BlockSpec((n, tr, c_), lambda i: (0, i, 0))]
    out_spec = pl.BlockSpec((tr, c_), lambda i: (_core_id() * per + i, 0))
    out_shape = jax.ShapeDtypeStruct((2 * half, c_), F32)
    if comm is not None:
        (out,), landed = _comm_call(body, name=name, grid=(per,), in_specs=in_specs, out_specs=[out_spec],
                                    out_shape=[out_shape], scratch_shapes=[], args=(pre, recv), comm=comm)
        return out, landed
    return pl.pallas_call(body, name=name, grid=(per,), in_specs=in_specs, out_specs=out_spec, out_shape=out_shape,
                          compiler_params=_params(("arbitrary",)))(pre, recv)


def _sum_slots(recv, *, name):
    n, r, c_ = recv.shape
    tr = _pick(r, max(16, (1 << 19) // (2 * c_)), 16)

    def body(r_ref, o_ref):
        acc = r_ref[0].astype(F32)
        for k in range(1, n):
            acc = acc + r_ref[k].astype(F32)
        o_ref[...] = acc

    return _rowcall(body, name, r, tr, [pl.BlockSpec((n, tr, c_), lambda i: (0, i, 0))], _row(tr, c_),
                    jax.ShapeDtypeStruct((r, c_), F32), (recv,))


def _prereduce(gfull, *, name):
    return _pair_add(gfull, _run_comm(_pair_comm(gfull), name=name + "_pair")[0], name=name + "_padd")


PACK_ROWS = 16


def _pack(parts):
    flat = [p.reshape(-1).astype(F32) for p in parts]
    n = sum(f.shape[0] for f in flat)
    unit = PACK_ROWS * LANES
    total = -(-n // unit) * unit
    if total > n:
        flat.append(jnp.zeros((total - n,), F32))
    where, off = [], 0
    for p in parts:
        where.append((off, p.shape))
        off += p.size
    return jnp.concatenate(flat).reshape(total // LANES, LANES), where


def _unpack(flat, where):
    v = flat.reshape(-1)
    return [v[off:off + _size(shape)].reshape(shape) for off, shape in where]


def _size(shape):
    n = 1
    for d in shape:
        n *= d
    return n


def _sample_step(x, target, mods, h1, w_in_p, bufs, sp):
    s, d = x.shape
    u = d // 2
    h = u // HEAD_DIM
    g = h // HEADS_PER_GROUP
    pw = w_in_p.shape[1]
    din = 6 * u + 2 * h
    dff_ = bufs["w_up"].shape[2] * N_CHIPS
    shift1, scale1, gate1, shift2, scale2, gate2 = mods

    a_f = -jnp.exp(sp["ssm_a_log_f"].reshape(-1))
    a_b = -jnp.exp(sp["ssm_a_log_b"].reshape(-1))
    pad_l = LANES - 2 * h
    a_row = jnp.pad(jnp.concatenate([a_f, a_b]), (0, pad_l)).reshape(1, LANES)
    bias_row = jnp.pad(jnp.concatenate([sp["ssm_dt_bias_f"].reshape(-1), sp["ssm_dt_bias_b"].reshape(-1)]),
                       (0, pad_l)).reshape(1, LANES)
    a_col = jnp.broadcast_to(jnp.stack([a_f, a_b])[:, :, None], (2, h, LANES))
    d_e = jnp.repeat(sp["ssm_d"].reshape(-1), HEAD_DIM).reshape(1, u)
    conv_w, conv_b = sp["ssm_conv_w"], sp["ssm_conv_b"].reshape(1, 2 * u)
    sc_conv_w = sp["sc_conv_w"]
    ssm_norm_w, sc_norm_w = sp["ssm_norm_w"].reshape(1, u), sp["sc_norm_w"].reshape(1, u)
    ln1_g, ln1_b = sp["ln1_g"].reshape(1, d), sp["ln1_b"].reshape(1, d)
    ln2_g, ln2_b = sp["ln2_g"].reshape(1, d), sp["ln2_b"].reshape(1, d)

    e = 1.0 / DMA_CHUNKS
    proj, (w_up_b, w_out_blk) = _matmul(
        h1, w_in_p, name="mm_proj", tn=1280,
        comm=_merge_comms(_gather_comm(bufs["w_up"], (0, 3 * e)), _gather_comm(bufs["w_out"])))
    w_out = w_out_blk.reshape(d, d)
    xbc, (w_down_b,) = _conv_silu_fwd(proj, conv_w, conv_b, s, u, comm=_gather_comm(bufs["w_down"], (0, e)))
    (dt, cum, dt_e, cum_e), (w_up_b,) = _dt_prep(proj, bias_row, a_row, s, u, h, comm=_gather_comm(w_up_b, (3 * e, 4 * e)))
    cum_t = jnp.stack([cum[:, :h].T, cum[:, h:2 * h].T])
    dt_t = jnp.stack([dt[:, :h].T, dt[:, h:2 * h].T])
    (y2, states), (w_up_b,) = _ssd_fwd(xbc, dt_e, cum_e, cum_t, s, h, g, comm=_gather_comm(w_up_b, (4 * e, 6 * e)))
    y_ssm = _gate_norm_fwd(y2, xbc, proj, d_e, ssm_norm_w, s, u, g)
    y_sc = _shortconv_fwd(proj, sc_conv_w, sc_norm_w, s, u)
    ymix = jnp.concatenate([y_ssm, y_sc], axis=1)
    mix, (w_up_b,) = _matmul(ymix, w_out, name="mm_mix", comm=_gather_comm(w_up_b, (6 * e, 7 * e)))
    (x1, h2), (w_up_blk,) = _ln1_fwd(x, mix, gate1, ln1_g, ln1_b, scale2, shift2, comm=_gather_comm(w_up_b, (7 * e, 1)))
    (up, ff), (w_down_blk,) = _matmul(h2, w_up_blk, name="mm_up", b_blocks=N_CHIPS, epilogue="relu2",
                                      comm=_gather_comm(w_down_b, (e, 1)))
    w_down = w_down_blk.reshape(dff_, d)
    f = _matmul(ff, w_down, name="mm_down")
    df, dr2, loss, dg2, db2, dgate2 = _ln2_loss_bwd(x1, f, target, gate2, ln2_g, ln2_b)

    gw_down = _matmul(ff, df, name="mm_gw_down", ta=True, out_dtype=BF16).reshape(N_CHIPS, dff_ // N_CHIPS, d)
    du, (peer,) = _matmul(df, w_down, name="mm_dff", tb=True, out_dtype=BF16, epilogue="relu2_bwd", extra=up,
                          comm=_pair_comm(gw_down))
    pre_down = _pair_add(gw_down, peer, name="rs_w_down_padd")
    gw_up, (rv_down,) = _matmul(h2, du, name="mm_gw_up", ta=True, out_dtype=BF16, out_blocks=N_CHIPS,
                                comm=_scatter_comm(pre_down, (0, 0.5)))
    dh2, (rv_down, peer) = _matmul(du, w_up_blk, name="mm_dh2", tb=True, b_blocks=N_CHIPS,
                                   comm=_merge_comms(_scatter_comm(pre_down, (0.5, 1), recv=rv_down), _pair_comm(gw_up)))
    pre_up = _pair_add(gw_up, peer, name="rs_w_up_padd")
    (dmix, dxa, dscale2, dshift2, dg1, db1, dgate1), (rv_up,) = _ln1_bwd(
        dh2, dr2, x1, x, mix, scale2, gate1, ln1_g, comm=_scatter_comm(pre_up, (0, e)))
    gw_out, (rv_up,) = _matmul(ymix, dmix, name="mm_gw_out", ta=True, out_dtype=BF16,
                               comm=_scatter_comm(pre_up, (e, 2 * e), recv=rv_up))
    gw_out = gw_out.reshape(N_CHIPS, d // N_CHIPS, d)
    dymix, (rv_up, peer) = _matmul(dmix, w_out, name="mm_dymix", tb=True,
                                   comm=_merge_comms(_scatter_comm(pre_up, (2 * e, 4 * e), recv=rv_up), _pair_comm(gw_out)))
    pre_out = _pair_add(gw_out, peer, name="rs_w_out_padd")
    dys, dproj, dxs, dnw, dd_e = _gate_norm_bwd(dymix, y2, xbc, proj, d_e, ssm_norm_w, pw, s, u, g)
    (dx2, dbb, dcc, ddt_t, da), (rv_up,) = _ssd_bwd(xbc, dt_e, cum_e, cum_t, dt_t, a_col, dys, states, s, h, g,
                                                    comm=_scatter_comm(pre_up, (4 * e, 1), recv=rv_up))
    dxbc = jnp.concatenate([dx2[0] + dx2[1] + dxs, dbb[0] + dbb[1], dcc[0] + dcc[1]], axis=1)
    (dproj, dcw, dcb), (rv_out,) = _conv_silu_bwd(proj, dxbc, conv_w, conv_b, dproj, s, u, comm=_scatter_comm(pre_out))
    ddt = jnp.pad(jnp.concatenate([ddt_t[0].T, ddt_t[1].T], axis=1), ((0, 0), (0, pad_l)))
    dproj, dbias = _dt_bwd(ddt, proj, bias_row, dproj, s, u, h)
    dproj, dscw, dscnw = _shortconv_bwd(dymix, proj, sc_conv_w, sc_norm_w, dproj, s, u)
    gp = _matmul(h1, dproj, name="mm_gw_in", ta=True, out_dtype=BF16, tn=1280)
    pre_in = _from_p_layout(_prereduce(gp[None], name="rs_w_in")[0], u, h, N_CHIPS)
    (grad_x, dscale1, dshift1), (rv_in,) = _matmul(dproj, w_in_p, name="mm_dh1", tb=True, tk=2560, epilogue="dx",
                                                   extra=(dxa, x, scale1), comm=_scatter_comm(pre_in))
    big = {}
    half_down = _sum_into_half(pre_down, rv_down, name="rs_w_down_sum")
    half_up, (big["w_down"],) = _sum_into_half(pre_up, rv_up, name="rs_w_up_sum", comm=_swap_comm(half_down))
    half_out, (big["w_up"],) = _sum_into_half(pre_out, rv_out, name="rs_w_out_sum", comm=_swap_comm(half_up))
    half_in, (big["w_out"],) = _sum_into_half(pre_in, rv_in, name="rs_w_in_sum", comm=_swap_comm(half_out))
    big["w_in"] = _run_comm(_swap_comm(half_in), name="rs_w_in_swap")[0]

    small = {
        "dmod": jnp.concatenate([dshift1, dscale1, dgate1, dshift2, dscale2, dgate2], axis=1),
        "ssm_conv_b": dcb,
        "ssm_dt_bias_f": dbias[0, :h], "ssm_dt_bias_b": dbias[0, h:2 * h],
        "ssm_a_log_f": jnp.sum(da[0], axis=1) * a_f, "ssm_a_log_b": jnp.sum(da[1], axis=1) * a_b,
        "ssm_d": jnp.sum(dd_e.reshape(h, HEAD_DIM), axis=1),
        "ssm_norm_w": dnw, "sc_norm_w": dscnw,
        "ln1_g": dg1, "ln1_b": db1, "ln2_g": dg2, "ln2_b": db2,
        "ssm_conv_w": dcw[:SSM_CONV], "sc_conv_w": dscw[:SC_CONV],
    }
    return loss, grad_x, big, small


WEIGHTS = ['w_ada', 'b_ada', 'w_in', 'ssm_conv_w', 'ssm_conv_b', 'ssm_dt_bias_f', 'ssm_dt_bias_b', 'ssm_a_log_f',
           'ssm_a_log_b', 'ssm_d', 'ssm_norm_w', 'sc_conv_w', 'sc_norm_w', 'w_out', 'ln1_g', 'ln1_b', 'w_up', 'w_down',
           'ln2_g', 'ln2_b']
BIG = ('w_ada', 'w_in', 'w_out', 'w_up', 'w_down')
SMALL = tuple(n for n in WEIGHTS if n not in BIG)
SMALL_SHARDED = ('ssm_conv_w', 'sc_conv_w')


def _p_layout_width(u):
    return -(-(6 * u + LANES) // 512) * 512


def _p_segments(u, h):
    return [((0, u), P_Z * u), ((u, 3 * u), P_X * u), ((3 * u, 3 * u + 2 * h), 6 * u),
            ((3 * u + 2 * h, 6 * u + 2 * h), P_H * u)]


def _to_p_layout(blocks, u, h, pw):
    nblk, d, w = blocks.shape
    parts = []
    for (lo, hi), _ in sorted(_p_segments(u, h), key=lambda t: t[1]):
        for k in range(nblk):
            a, b = max(lo, k * w), min(hi, (k + 1) * w)
            if a < b:
                parts.append(blocks[k][:, a - k * w:b - k * w])
    parts.append(jnp.zeros((d, pw - nblk * w), blocks.dtype))
    return jnp.concatenate(parts, axis=1)


def _from_p_layout(gp, u, h, nblk):
    w = (6 * u + 2 * h) // nblk
    blocks = []
    for k in range(nblk):
        parts = []
        for (lo, hi), poff in _p_segments(u, h):
            a, b = max(lo, k * w), min(hi, (k + 1) * w)
            if a < b:
                parts.append(gp[:, poff + a - lo:poff + b - lo])
        blocks.append(jnp.concatenate(parts, axis=1))
    return jnp.stack(blocks)


def kernel(x, c, w_ada, b_ada, w_in, ssm_conv_w, ssm_conv_b, ssm_dt_bias_f, ssm_dt_bias_b, ssm_a_log_f, ssm_a_log_b, ssm_d, ssm_norm_w, sc_conv_w, sc_norm_w, w_out, ln1_g, ln1_b, w_up, w_down, ln2_g, ln2_b, loss_target, m_w_ada, m_b_ada, m_w_in, m_ssm_conv_w, m_ssm_conv_b, m_ssm_dt_bias_f, m_ssm_dt_bias_b, m_ssm_a_log_f, m_ssm_a_log_b, m_ssm_d, m_ssm_norm_w, m_sc_conv_w, m_sc_norm_w, m_w_out, m_ln1_g, m_ln1_b, m_w_up, m_w_down, m_ln2_g, m_ln2_b, v_w_ada, v_b_ada, v_w_in, v_ssm_conv_w, v_ssm_conv_b, v_ssm_dt_bias_f, v_ssm_dt_bias_b, v_ssm_a_log_f, v_ssm_a_log_b, v_ssm_d, v_ssm_norm_w, v_sc_conv_w, v_sc_norm_w, v_w_out, v_ln1_g, v_ln1_b, v_w_up, v_w_down, v_ln2_g, v_ln2_b):
    given = dict(locals())
    w = {n: given[n][0] for n in WEIGHTS}
    m = {n: given["m_" + n][0] for n in WEIGHTS}
    v = {n: given["v_" + n][0] for n in WEIGHTS}
    xs, tgt = x[0], loss_target[0]
    s, d = xs.shape
    u = d // 2
    h = u // HEAD_DIM
    nmod = N_MOD * d
    nmod_loc = nmod // N_CHIPS
    ax, ay, ac = lax.axis_index("x"), lax.axis_index("y"), lax.axis_index("c")
    chip = 2 * ax + ay
    me = 2 * chip + ac

    pay1, where1 = _pack([c[0], w["ssm_conv_w"], w["sc_conv_w"]])
    g1 = _allgather_small(pay1, name="ag_inputs").reshape(N_DEV, -1)
    per_dev = [_unpack(g1[k], where1) for k in range(N_DEV)]
    c_all = jnp.stack([p[0] for p in per_dev])
    ssm_conv_w_full = jnp.concatenate([per_dev[2 * k][1] for k in range(N_CHIPS)], axis=1)
    sc_conv_w_full = jnp.concatenate([per_dev[2 * k][2] for k in range(N_CHIPS)], axis=1)

    sc_all = _silu(c_all)
    sc16 = jnp.pad(sc_all, ((0, 16 - N_DEV), (0, 0)))
    b_loc = lax.dynamic_slice(w["b_ada"], (chip * nmod_loc,), (nmod_loc,))
    mod_loc = _matmul(sc16, w["w_ada"], name="mm_mod")[:N_DEV] + b_loc[None, :]
    pay2, where2 = _pack([mod_loc])
    g2 = _allgather_small(pay2, name="ag_mod").reshape(N_DEV, -1)
    mod_blocks = jnp.stack([_unpack(g2[2 * k], where2)[0] for k in range(N_CHIPS)])
    mod_mine = lax.dynamic_index_in_dim(mod_blocks, me, axis=1, keepdims=False).reshape(N_MOD, 1, d)
    mods = [mod_mine[k] for k in range(N_MOD)]

    din = w["w_in"].shape[1] * N_CHIPS

    e = 1.0 / DMA_CHUNKS
    g_in = _cast_into_block(w["w_in"], name="cast_w_in")
    bufs = {"w_out": _cast_into_block(w["w_out"], name="cast_w_out")}
    bufs["w_up"], (g_in,) = _cast_into_block(w["w_up"], name="cast_w_up", comm=_gather_comm(g_in, (0, e)))
    bufs["w_down"], (g_in,) = _cast_into_block(w["w_down"], name="cast_w_down", comm=_gather_comm(g_in, (e, 2 * e)))
    h1, (g_in,) = _modulate(xs, mods[1], mods[0], name="modulate1", comm=_gather_comm(g_in, (2 * e, 3 * e)))
    g_in = _gather_weight(g_in, name="gather_w_in", part=(3 * e, 1))
    w_in_p = _to_p_layout(g_in, u, h, _p_layout_width(u))

    sp = {n: w[n] for n in SMALL}
    sp["ssm_conv_w"], sp["sc_conv_w"] = ssm_conv_w_full, sc_conv_w_full
    loss_loc, grad_x, big, small = _sample_step(xs, tgt, mods, h1, w_in_p, bufs, sp)

    small_names = ["dmod"] + [n for n in SMALL if n != "b_ada"]
    pay3, where3 = _pack([loss_loc] + [small[n] for n in small_names])
    g3 = _allgather_small(pay3, name="ag_small_grads")
    tot = _unpack(_sum_slots(g3.reshape(N_DEV, -1, LANES), name="sum_small_grads"), where3)
    loss = tot[0].reshape(())
    gsum = dict(zip(small_names, tot[1:]))
    dmod_all = jnp.stack([_unpack(g3.reshape(N_DEV, -1)[k], where3)[1].reshape(-1) for k in range(N_DEV)])

    grads = {}
    grads["b_ada"] = gsum["dmod"].reshape(-1)
    for n in SMALL:
        if n in SMALL_SHARDED:
            loc = w[n].shape[1]
            grads[n] = lax.dynamic_slice_in_dim(gsum[n], chip * loc, loc, axis=1)
        elif n != "b_ada":
            grads[n] = gsum[n].reshape(w[n].shape)

    delta, new_m, new_v = {}, {}, {}
    dm_loc = lax.dynamic_slice_in_dim(dmod_all, chip * nmod_loc, nmod_loc, axis=1)
    grads["w_ada"], delta["w_ada"], new_m["w_ada"], new_v["w_ada"] = _adam_outer(
        w["w_ada"], sc16.T, jnp.pad(dm_loc, ((0, 16 - N_DEV), (0, 0))), m["w_ada"], v["w_ada"], name="adam_w_ada")
    for n in ("w_in", "w_out", "w_up", "w_down"):
        grads[n], delta[n], new_m[n], new_v[n] = _adam(w[n], big[n], m[n], v[n], name="adam_" + n, emit_grad=True)
    pw_, where_s = _pack([w[n] for n in SMALL])
    pg_, _ = _pack([grads[n] for n in SMALL])
    pm_, _ = _pack([m[n] for n in SMALL])
    pv_, _ = _pack([v[n] for n in SMALL])
    sd, sm, sv = _adam(pw_, pg_, pm_, pv_, name="adam_small")
    for n, a, b_, c_ in zip(SMALL, _unpack(sd, where_s), _unpack(sm, where_s), _unpack(sv, where_s)):
        delta[n], new_m[n], new_v[n] = a, b_, c_

    def lead(t):
        return t[None]

    return (loss, grad_x[None], *[lead(grads[n].reshape(w[n].shape)) for n in WEIGHTS],
            *[lead(delta[n]) for n in WEIGHTS], *[lead(new_m[n]) for n in WEIGHTS], *[lead(new_v[n]) for n in WEIGHTS])
```
